```python
import jax, jax.numpy as jnp
from jax import lax
import numpy as np

D_MODEL = 4096
BATCH = 16
SEQ = 2048
DEPTH = 1

CHUNK = 64
MIX_WIDTH = D_MODEL
GDN_WIDTH = MIX_WIDTH // 2
SC_WIDTH = MIX_WIDTH - GDN_WIDTH
HEAD_DIM = 128
GDN_HEADS = GDN_WIDTH // HEAD_DIM
SC_GROUPS = SC_WIDTH // HEAD_DIM
GDN_CONV = 4
SC_CONV = 3
D_FF = 4 * D_MODEL
NORM_EPS = 1e-6
L2_EPS = 1e-6
IN_COLS = 4 * GDN_WIDTH + 2 * GDN_HEADS + 3 * SC_WIDTH

kernel_name = "hybrid_gdn_shortconv_sqrelu_block"


def rms_norm(x, w):
    xf = x.astype(jnp.float32)
    y = xf * lax.rsqrt(jnp.mean(xf * xf, axis=-1, keepdims=True) + NORM_EPS)
    return (y * w.astype(jnp.float32)).astype(x.dtype)


def l2_normalize(t):
    return t * lax.rsqrt(jnp.sum(t * t, axis=-1, keepdims=True) + L2_EPS)


def causal_depthwise_conv(u, w):
    k_width, ch = w.shape
    return lax.conv_general_dilated(
        u, w[:, None, :].astype(u.dtype), window_strides=(1,), padding=[(k_width - 1, 0)],
        dimension_numbers=("NWC", "WIO", "NWC"), feature_group_count=ch)


def gated_delta_rule_chunked(q, k, v, g, beta):
    bsz, seq, heads, dk = q.shape
    dv = v.shape[-1]
    n = seq // CHUNK
    q = l2_normalize(q.astype(jnp.float32)) * (dk ** -0.5)
    k = l2_normalize(k.astype(jnp.float32))
    v = v.astype(jnp.float32)

    def to_chunks(t):
        return t.reshape(bsz, n, CHUNK, heads, -1).transpose(0, 1, 3, 2, 4)

    q, k, v = to_chunks(q), to_chunks(k), to_chunks(v)
    g = jnp.cumsum(g.reshape(bsz, n, CHUNK, heads).transpose(0, 1, 3, 2), axis=-1)
    beta = beta.reshape(bsz, n, CHUNK, heads).transpose(0, 1, 3, 2)

    tri_incl = jnp.tril(jnp.ones((CHUNK, CHUNK), dtype=bool))
    tri_strict = jnp.tril(jnp.ones((CHUNK, CHUNK), dtype=bool), k=-1)
    decay = jnp.exp(jnp.where(tri_incl, g[..., :, None] - g[..., None, :], -jnp.inf))

    kk = jnp.einsum("bnhid,bnhjd->bnhij", k, k)
    lower = jnp.where(tri_strict, beta[..., None] * kk * decay, 0.0)
    a_mat = lower + jnp.eye(CHUNK, dtype=jnp.float32)
    rhs = jnp.concatenate([v * beta[..., None], k * (beta * jnp.exp(g))[..., None]], axis=-1)
    sol = lax.linalg.triangular_solve(a_mat, rhs, left_side=True, lower=True, unit_diagonal=True)
    u, w = sol[..., :dv], sol[..., dv:]

    qk = jnp.einsum("bnhid,bnhjd->bnhij", q, k) * decay
    q_dec = q * jnp.exp(g)[..., None]
    k_dec = k * jnp.exp(g[..., -1:] - g)[..., None]
    g_last = jnp.exp(g[..., -1])

    def step(state, inp):
        q_c, k_c, u_c, w_c, qk_c, gl_c = inp
        v_new = u_c - jnp.einsum("bhcd,bhde->bhce", w_c, state)
        o_c = (jnp.einsum("bhcd,bhde->bhce", q_c, state)
               + jnp.einsum("bhij,bhje->bhie", qk_c, v_new))
        state = state * gl_c[..., None, None] + jnp.einsum("bhcd,bhce->bhde", k_c, v_new)
        return state, o_c

    xs = tuple(jnp.moveaxis(t, 1, 0) for t in (q_dec, k_dec, u, w, qk, g_last))
    state0 = jnp.zeros((bsz, heads, dk, dv), dtype=jnp.float32)
    _, o = lax.scan(step, state0, xs)
    return o.transpose(1, 0, 3, 2, 4).reshape(bsz, seq, heads, dv)


def hybrid_mixer(xn, w_in, conv_qkv_w, a_log, dt_bias, gdn_norm_w, conv_sc_w, w_out):
    bsz, seq, _ = xn.shape
    proj = jnp.einsum("bsd,dc->bsc", xn, w_in)
    sizes = [3 * GDN_WIDTH, GDN_WIDTH, GDN_HEADS, GDN_HEADS, SC_WIDTH, SC_WIDTH, SC_WIDTH]
    idx = [sum(sizes[: i + 1]) for i in range(len(sizes) - 1)]
    qkv, z, a, b, sc_b, sc_c, sc_h = jnp.split(proj, idx, axis=-1)

    qkv = jax.nn.silu(causal_depthwise_conv(qkv, conv_qkv_w))
    q, k, v = (t.reshape(bsz, seq, GDN_HEADS, HEAD_DIM) for t in jnp.split(qkv, 3, axis=-1))
    g = -jnp.exp(a_log.astype(jnp.float32)) * jax.nn.softplus(a.astype(jnp.float32) + dt_bias.astype(jnp.float32))
    beta = jax.nn.sigmoid(b.astype(jnp.float32))
    o = gated_delta_rule_chunked(q, k, v, g, beta)
    o = o * lax.rsqrt(jnp.mean(o * o, axis=-1, keepdims=True) + NORM_EPS) * gdn_norm_w.astype(jnp.float32)
    o = o * jax.nn.silu(z.astype(jnp.float32).reshape(bsz, seq, GDN_HEADS, HEAD_DIM))
    gdn_out = o.reshape(bsz, seq, GDN_WIDTH).astype(xn.dtype)

    sc_out = sc_b * causal_depthwise_conv(sc_c * sc_h, conv_sc_w)

    mixed = jnp.concatenate([gdn_out, sc_out], axis=-1)
    return jnp.einsum("bsm,md->bsd", mixed, w_out)


def squared_relu_mlp(xn, w_up, w_down):
    hid = jnp.square(jax.nn.relu(jnp.einsum("bsd,df->bsf", xn, w_up)))
    return jnp.einsum("bsf,fd->bsd", hid, w_down)


def _fwd_setup_inputs(seed: int = 0) -> dict:
    key = jax.random.key(seed)
    ks = jax.random.split(key, 16)
    f32 = jnp.float32

    def gain(k):
        return 1.0 + 0.02 * jax.random.normal(k, (DEPTH, D_MODEL), f32)

    x = jax.random.normal(ks[0], (BATCH, SEQ, D_MODEL), f32)
    norm_mix_pre = gain(ks[1])
    w_in = jax.random.normal(ks[2], (DEPTH, D_MODEL, IN_COLS), f32) * D_MODEL ** -0.5
    conv_qkv_w = jax.random.normal(ks[3], (DEPTH, GDN_CONV, 3 * GDN_WIDTH), f32) * GDN_CONV ** -0.5
    a_log = jnp.log(jax.random.uniform(ks[4], (DEPTH, GDN_HEADS), f32, 1.0, 16.0))
    dt_bias = 0.1 * jax.random.normal(ks[5], (DEPTH, GDN_HEADS), f32)
    gdn_norm_w = 1.0 + 0.02 * jax.random.normal(ks[6], (DEPTH, HEAD_DIM), f32)
    conv_sc_w = jax.random.normal(ks[7], (DEPTH, SC_CONV, SC_WIDTH), f32) * SC_CONV ** -0.5
    w_out = jax.random.normal(ks[8], (DEPTH, MIX_WIDTH, D_MODEL), f32) * MIX_WIDTH ** -0.5
    norm_mix_post = gain(ks[9])
    norm_mlp_pre = gain(ks[10])
    w_up = jax.random.normal(ks[11], (DEPTH, D_MODEL, D_FF), f32) * D_MODEL ** -0.5
    w_down = jax.random.normal(ks[12], (DEPTH, D_FF, D_MODEL), f32) * D_FF ** -0.5
    norm_mlp_post = gain(ks[13])
    return {"x": x, "norm_mix_pre": norm_mix_pre, "w_in": w_in, "conv_qkv_w": conv_qkv_w,
            "a_log": a_log, "dt_bias": dt_bias, "gdn_norm_w": gdn_norm_w, "conv_sc_w": conv_sc_w,
            "w_out": w_out, "norm_mix_post": norm_mix_post, "norm_mlp_pre": norm_mlp_pre,
            "w_up": w_up, "w_down": w_down, "norm_mlp_post": norm_mlp_post}


def _fwd_reference(x, norm_mix_pre, w_in, conv_qkv_w, a_log, dt_bias, gdn_norm_w, conv_sc_w,
              w_out, norm_mix_post, norm_mlp_pre, w_up, w_down, norm_mlp_post):
    h = x
    for l in range(DEPTH):
        mix = hybrid_mixer(rms_norm(h, norm_mix_pre[l]), w_in[l], conv_qkv_w[l], a_log[l],
                           dt_bias[l], gdn_norm_w[l], conv_sc_w[l], w_out[l])
        h = h + rms_norm(mix, norm_mix_post[l])
        ff = squared_relu_mlp(rms_norm(h, norm_mlp_pre[l]), w_up[l], w_down[l])
        h = h + rms_norm(ff, norm_mlp_post[l])
    return h


import jax as _jax
import jax.numpy as _jnp

TWIN_FORMAT = 'train_step'
FWD_PARAMS = ['x', 'norm_mix_pre', 'w_in', 'conv_qkv_w', 'a_log', 'dt_bias', 'gdn_norm_w', 'conv_sc_w', 'w_out', 'norm_mix_post', 'norm_mlp_pre', 'w_up', 'w_down', 'norm_mlp_post']
TWIN_WEIGHTS = ['norm_mix_pre', 'w_in', 'conv_qkv_w', 'a_log', 'dt_bias', 'gdn_norm_w', 'conv_sc_w', 'w_out', 'norm_mix_post', 'norm_mlp_pre', 'w_up', 'w_down', 'norm_mlp_post']
TWIN_DIFF_INPUT = 'x'
TWIN_INPUTS = ['x', 'norm_mix_pre', 'w_in', 'conv_qkv_w', 'a_log', 'dt_bias', 'gdn_norm_w', 'conv_sc_w', 'w_out', 'norm_mix_post', 'norm_mlp_pre', 'w_up', 'w_down', 'norm_mlp_post', 'loss_target', 'm_norm_mix_pre', 'm_w_in', 'm_conv_qkv_w', 'm_a_log', 'm_dt_bias', 'm_gdn_norm_w', 'm_conv_sc_w', 'm_w_out', 'm_norm_mix_post', 'm_norm_mlp_pre', 'm_w_up', 'm_w_down', 'm_norm_mlp_post', 'v_norm_mix_pre', 'v_w_in', 'v_conv_qkv_w', 'v_a_log', 'v_dt_bias', 'v_gdn_norm_w', 'v_conv_sc_w', 'v_w_out', 'v_norm_mix_post', 'v_norm_mlp_pre', 'v_w_up', 'v_w_down', 'v_norm_mlp_post']
TWIN_OUTPUTS = ['loss', 'grad_x', 'grad_norm_mix_pre', 'grad_w_in', 'grad_conv_qkv_w', 'grad_a_log', 'grad_dt_bias', 'grad_gdn_norm_w', 'grad_conv_sc_w', 'grad_w_out', 'grad_norm_mix_post', 'grad_norm_mlp_pre', 'grad_w_up', 'grad_w_down', 'grad_norm_mlp_post', 'delta_norm_mix_pre', 'delta_w_in', 'delta_conv_qkv_w', 'delta_a_log', 'delta_dt_bias', 'delta_gdn_norm_w', 'delta_conv_sc_w', 'delta_w_out', 'delta_norm_mix_post', 'delta_norm_mlp_pre', 'delta_w_up', 'delta_w_down', 'delta_norm_mlp_post', 'new_m_norm_mix_pre', 'new_m_w_in', 'new_m_conv_qkv_w', 'new_m_a_log', 'new_m_dt_bias', 'new_m_gdn_norm_w', 'new_m_conv_sc_w', 'new_m_w_out', 'new_m_norm_mix_post', 'new_m_norm_mlp_pre', 'new_m_w_up', 'new_m_w_down', 'new_m_norm_mlp_post', 'new_v_norm_mix_pre', 'new_v_w_in', 'new_v_conv_qkv_w', 'new_v_a_log', 'new_v_dt_bias', 'new_v_gdn_norm_w', 'new_v_conv_sc_w', 'new_v_w_out', 'new_v_norm_mix_post', 'new_v_norm_mlp_pre', 'new_v_w_up', 'new_v_w_down', 'new_v_norm_mlp_post']
TWIN_LEAF_KINDS = {'loss': 'loss', 'grad_x': 'grad_x', 'grad_norm_mix_pre': 'grad_w', 'grad_w_in': 'grad_w', 'grad_conv_qkv_w': 'grad_w', 'grad_a_log': 'grad_w', 'grad_dt_bias': 'grad_w', 'grad_gdn_norm_w': 'grad_w', 'grad_conv_sc_w': 'grad_w', 'grad_w_out': 'grad_w', 'grad_norm_mix_post': 'grad_w', 'grad_norm_mlp_pre': 'grad_w', 'grad_w_up': 'grad_w', 'grad_w_down': 'grad_w', 'grad_norm_mlp_post': 'grad_w', 'delta_norm_mix_pre': 'delta_w', 'delta_w_in': 'delta_w', 'delta_conv_qkv_w': 'delta_w', 'delta_a_log': 'delta_w', 'delta_dt_bias': 'delta_w', 'delta_gdn_norm_w': 'delta_w', 'delta_conv_sc_w': 'delta_w', 'delta_w_out': 'delta_w', 'delta_norm_mix_post': 'delta_w', 'delta_norm_mlp_pre': 'delta_w', 'delta_w_up': 'delta_w', 'delta_w_down': 'delta_w', 'delta_norm_mlp_post': 'delta_w', 'new_m_norm_mix_pre': 'new_m', 'new_m_w_in': 'new_m', 'new_m_conv_qkv_w': 'new_m', 'new_m_a_log': 'new_m', 'new_m_dt_bias': 'new_m', 'new_m_gdn_norm_w': 'new_m', 'new_m_conv_sc_w': 'new_m', 'new_m_w_out': 'new_m', 'new_m_norm_mix_post': 'new_m', 'new_m_norm_mlp_pre': 'new_m', 'new_m_w_up': 'new_m', 'new_m_w_down': 'new_m', 'new_m_norm_mlp_post': 'new_m', 'new_v_norm_mix_pre': 'new_v', 'new_v_w_in': 'new_v', 'new_v_conv_qkv_w': 'new_v', 'new_v_a_log': 'new_v', 'new_v_dt_bias': 'new_v', 'new_v_gdn_norm_w': 'new_v', 'new_v_conv_sc_w': 'new_v', 'new_v_w_out': 'new_v', 'new_v_norm_mix_post': 'new_v', 'new_v_norm_mlp_pre': 'new_v', 'new_v_w_up': 'new_v', 'new_v_w_down': 'new_v', 'new_v_norm_mlp_post': 'new_v'}


def _forward(args):
    return _fwd_reference(*[args[k] for k in FWD_PARAMS])


def _output_shape():
    def fwd():
        inp = _fwd_setup_inputs(0)
        return _fwd_reference(*[inp[k] for k in FWD_PARAMS])
    out = _jax.eval_shape(fwd)
    return out.shape, out.dtype

N_MICROBATCH = 1
ADAM_LR = 0.001
ADAM_B1 = 0.9
ADAM_B2 = 0.999
ADAM_EPS = 1e-08
ADAM_WD = 0.01
ADAM_STEP = 10
PER_EXAMPLE_BATCH_AXIS = {'x': 0, 'loss_target': 0}
SHARED_INPUTS = []
_WEIGHT_DTYPES = {'norm_mix_pre': _jnp.float32, 'w_in': _jnp.float32, 'conv_qkv_w': _jnp.float32, 'a_log': _jnp.float32, 'dt_bias': _jnp.float32, 'gdn_norm_w': _jnp.float32, 'conv_sc_w': _jnp.float32, 'w_out': _jnp.float32, 'norm_mix_post': _jnp.float32, 'norm_mlp_pre': _jnp.float32, 'w_up': _jnp.float32, 'w_down': _jnp.float32, 'norm_mlp_post': _jnp.float32}
MOMENT_SCALE = {'norm_mix_pre': 1.730354e-01, 'w_in': 9.365433e-02, 'conv_qkv_w': 9.597650e-02, 'a_log': 3.729983e-01, 'dt_bias': 2.891203e-01, 'gdn_norm_w': 9.203005e-01, 'conv_sc_w': 1.477060e-01, 'w_out': 1.904038e-01, 'norm_mix_post': 7.997514e+00, 'norm_mlp_pre': 1.377667e-01, 'w_up': 6.835815e-02, 'w_down': 2.023593e-01, 'norm_mlp_post': 8.237117e+00}


def _to_microbatches(a, axis):
    t = _jnp.moveaxis(a, axis, 0)
    t = t.reshape((N_MICROBATCH, t.shape[0] // N_MICROBATCH) + t.shape[1:])
    return _jnp.moveaxis(t, 1, axis + 1)


def setup_inputs(seed: int = 0) -> dict:
    inp = _fwd_setup_inputs(seed)
    key = _jax.random.fold_in(_jax.random.key(seed), 7919)
    shape, _ = _output_shape()
    out = dict(inp)
    out["loss_target"] = _jax.random.normal(_jax.random.fold_in(key, 0), shape, _jnp.float32)
    for i, name in enumerate(TWIN_WEIGHTS):
        w = inp[name].astype(_jnp.float32)
        if MOMENT_SCALE is None:
            s = _jnp.sqrt(_jnp.mean(_jnp.square(w)) + 1e-30)
        else:
            s = MOMENT_SCALE[name]
        km, kv = _jax.random.split(_jax.random.fold_in(key, i + 1))
        out[name] = w
        out["m_" + name] = s * _jax.random.normal(km, w.shape, _jnp.float32)
        out["v_" + name] = (s * s) * _jax.random.uniform(kv, w.shape, _jnp.float32, 0.5, 1.5)
    if N_MICROBATCH > 1:
        for name, axis in PER_EXAMPLE_BATCH_AXIS.items():
            out[name] = _to_microbatches(out[name], axis)
    return {'x': out['x'], 'norm_mix_pre': out['norm_mix_pre'], 'w_in': out['w_in'], 'conv_qkv_w': out['conv_qkv_w'], 'a_log': out['a_log'], 'dt_bias': out['dt_bias'], 'gdn_norm_w': out['gdn_norm_w'], 'conv_sc_w': out['conv_sc_w'], 'w_out': out['w_out'], 'norm_mix_post': out['norm_mix_post'], 'norm_mlp_pre': out['norm_mlp_pre'], 'w_up': out['w_up'], 'w_down': out['w_down'], 'norm_mlp_post': out['norm_mlp_post'], 'loss_target': out['loss_target'], 'm_norm_mix_pre': out['m_norm_mix_pre'], 'm_w_in': out['m_w_in'], 'm_conv_qkv_w': out['m_conv_qkv_w'], 'm_a_log': out['m_a_log'], 'm_dt_bias': out['m_dt_bias'], 'm_gdn_norm_w': out['m_gdn_norm_w'], 'm_conv_sc_w': out['m_conv_sc_w'], 'm_w_out': out['m_w_out'], 'm_norm_mix_post': out['m_norm_mix_post'], 'm_norm_mlp_pre': out['m_norm_mlp_pre'], 'm_w_up': out['m_w_up'], 'm_w_down': out['m_w_down'], 'm_norm_mlp_post': out['m_norm_mlp_post'], 'v_norm_mix_pre': out['v_norm_mix_pre'], 'v_w_in': out['v_w_in'], 'v_conv_qkv_w': out['v_conv_qkv_w'], 'v_a_log': out['v_a_log'], 'v_dt_bias': out['v_dt_bias'], 'v_gdn_norm_w': out['v_gdn_norm_w'], 'v_conv_sc_w': out['v_conv_sc_w'], 'v_w_out': out['v_w_out'], 'v_norm_mix_post': out['v_norm_mix_post'], 'v_norm_mlp_pre': out['v_norm_mlp_pre'], 'v_w_up': out['v_w_up'], 'v_w_down': out['v_w_down'], 'v_norm_mlp_post': out['v_norm_mlp_post']}


def _loss(weights, diff, rest, loss_target):
    with _jax.named_scope("forward"):
        args = {**rest, TWIN_DIFF_INPUT: diff, **{k: w.astype(_WEIGHT_DTYPES[k]) for k, w in weights.items()}}
        y = _forward(args)
    with _jax.named_scope("loss_head"):
        err = _jnp.square(y.astype(_jnp.float32) - loss_target)
        return 0.5 * _jnp.sum(_jnp.mean(err, axis=-1)) if err.ndim else 0.5 * err


def _adamw(w, g, m, v):
    m = ADAM_B1 * m + (1.0 - ADAM_B1) * g
    v = ADAM_B2 * v + (1.0 - ADAM_B2) * _jnp.square(g)
    m_hat = m / (1.0 - ADAM_B1 ** ADAM_STEP)
    v_hat = v / (1.0 - ADAM_B2 ** ADAM_STEP)
    delta = -ADAM_LR * (m_hat / (_jnp.sqrt(v_hat) + ADAM_EPS) + ADAM_WD * w)
    return delta, m, v


def reference(x, norm_mix_pre, w_in, conv_qkv_w, a_log, dt_bias, gdn_norm_w, conv_sc_w, w_out, norm_mix_post, norm_mlp_pre, w_up, w_down, norm_mlp_post, loss_target, m_norm_mix_pre, m_w_in, m_conv_qkv_w, m_a_log, m_dt_bias, m_gdn_norm_w, m_conv_sc_w, m_w_out, m_norm_mix_post, m_norm_mlp_pre, m_w_up, m_w_down, m_norm_mlp_post, v_norm_mix_pre, v_w_in, v_conv_qkv_w, v_a_log, v_dt_bias, v_gdn_norm_w, v_conv_sc_w, v_w_out, v_norm_mix_post, v_norm_mlp_pre, v_w_up, v_w_down, v_norm_mlp_post):
    given = dict(x=x, norm_mix_pre=norm_mix_pre, w_in=w_in, conv_qkv_w=conv_qkv_w, a_log=a_log, dt_bias=dt_bias, gdn_norm_w=gdn_norm_w, conv_sc_w=conv_sc_w, w_out=w_out, norm_mix_post=norm_mix_post, norm_mlp_pre=norm_mlp_pre, w_up=w_up, w_down=w_down, norm_mlp_post=norm_mlp_post, loss_target=loss_target, m_norm_mix_pre=m_norm_mix_pre, m_w_in=m_w_in, m_conv_qkv_w=m_conv_qkv_w, m_a_log=m_a_log, m_dt_bias=m_dt_bias, m_gdn_norm_w=m_gdn_norm_w, m_conv_sc_w=m_conv_sc_w, m_w_out=m_w_out, m_norm_mix_post=m_norm_mix_post, m_norm_mlp_pre=m_norm_mlp_pre, m_w_up=m_w_up, m_w_down=m_w_down, m_norm_mlp_post=m_norm_mlp_post, v_norm_mix_pre=v_norm_mix_pre, v_w_in=v_w_in, v_conv_qkv_w=v_conv_qkv_w, v_a_log=v_a_log, v_dt_bias=v_dt_bias, v_gdn_norm_w=v_gdn_norm_w, v_conv_sc_w=v_conv_sc_w, v_w_out=v_w_out, v_norm_mix_post=v_norm_mix_post, v_norm_mlp_pre=v_norm_mlp_pre, v_w_up=v_w_up, v_w_down=v_w_down, v_norm_mlp_post=v_norm_mlp_post)
    weights = {n: given[n] for n in TWIN_WEIGHTS}
    shared = {n: given[n] for n in SHARED_INPUTS}
    per_example = {n: given[n] for n in ['x']}
    grad_fn = _jax.value_and_grad(_loss, argnums=(0, 1))

    def one_microbatch(ex, loss_target):
        ex = dict(ex)
        diff = ex.pop(TWIN_DIFF_INPUT)
        return grad_fn(weights, diff, {**shared, **ex}, loss_target)

    if N_MICROBATCH == 1:
        loss, (grad_w, grad_x) = one_microbatch(per_example, given["loss_target"])
    else:
        def body(carry, xs):
            loss_sum, grad_sum = carry
            l_k, (gw_k, gx_k) = one_microbatch(xs[0], xs[1])
            with _jax.named_scope("update"):
                return (loss_sum + l_k, _jax.tree.map(_jnp.add, grad_sum, gw_k)), gx_k

        init = (_jnp.zeros((), _jnp.float32), _jax.tree.map(_jnp.zeros_like, weights))
        (loss, grad_w), grad_x = _jax.lax.scan(body, init, (per_example, given["loss_target"]))
    with _jax.named_scope("update"):
        delta_w, new_m, new_v = {}, {}, {}
        for n in TWIN_WEIGHTS:
            delta_w[n], new_m[n], new_v[n] = _adamw(weights[n], grad_w[n], given["m_" + n], given["v_" + n])
    return (loss, grad_x, *[grad_w[n] for n in TWIN_WEIGHTS], *[delta_w[n] for n in TWIN_WEIGHTS],
            *[new_m[n] for n in TWIN_WEIGHTS], *[new_v[n] for n in TWIN_WEIGHTS])
```

```python
import functools

import jax
import jax.numpy as jnp
from jax import lax
from jax.experimental import pallas as pl
from jax.experimental.pallas import tpu as pltpu

CHUNK = 64
NORM_EPS = 1e-6
L2_EPS = 1e-6
N_CHIPS = 4
ADAM_LR = 0.001
ADAM_B1 = 0.9
ADAM_B2 = 0.999
ADAM_EPS = 1e-08
ADAM_WD = 0.01
ADAM_STEP = 10
LANES = 128
VMEM_LIMIT = 56 * 1024 * 1024

F32 = jnp.float32
BF16 = jnp.bfloat16
HI = lax.Precision.HIGHEST
MESH = pl.DeviceIdType.MESH
ANY = pl.BlockSpec(memory_space=pl.ANY)


def _params(n_grid=0):
    return pltpu.CompilerParams(vmem_limit_bytes=VMEM_LIMIT)


def _tile(n, pref, align):
    if n <= pref:
        return n
    t = (pref // align) * align
    while t >= align:
        if n % t == 0:
            return t
        t -= align
    raise ValueError(f"no tile for {n}")


def _sigmoid(x):
    return 1.0 / (1.0 + jnp.exp(-x))


def _softplus(x):
    return jnp.maximum(x, 0.0) + jnp.log(1.0 + jnp.exp(-jnp.abs(x)))


def _rms_fwd(x, g):
    r = lax.rsqrt(jnp.mean(x * x, axis=-1, keepdims=True) + NORM_EPS)
    return x * r * g


def _rms_bwd(dy, x, g):
    r = lax.rsqrt(jnp.mean(x * x, axis=-1, keepdims=True) + NORM_EPS)
    xh = x * r
    dxh = dy * g
    dx = r * (dxh - xh * jnp.mean(dxh * xh, axis=-1, keepdims=True))
    dg = jnp.sum(dy * xh, axis=0, keepdims=True)
    return dx, dg


def _matmul(a, b, form, out_dtypes, name, epilogue=None, extras=(), tm=512, tn=1024, tk=512):
    if form == "nn":
        (m, kd), (_, n) = a.shape, b.shape
        dims = (((1,), (0,)), ((), ()))
    elif form == "nt":
        (m, kd), (n, _) = a.shape, b.shape
        dims = (((1,), (1,)), ((), ()))
    else:
        (kd, m), (_, n) = a.shape, b.shape
        dims = (((0,), (0,)), ((), ()))
    tm, tn, tk = _tile(m, tm, LANES), _tile(n, tn, LANES), _tile(kd, tk, LANES)
    nk = kd // tk
    n_extra = len(extras)
    n_out = len(out_dtypes)

    if form == "nn":
        a_spec = pl.BlockSpec((tm, tk), lambda i, j, k: (i, k))
        b_spec = pl.BlockSpec((tk, tn), lambda i, j, k: (k, j))
    elif form == "nt":
        a_spec = pl.BlockSpec((tm, tk), lambda i, j, k: (i, k))
        b_spec = pl.BlockSpec((tn, tk), lambda i, j, k: (j, k))
    else:
        a_spec = pl.BlockSpec((tk, tm), lambda i, j, k: (k, i))
        b_spec = pl.BlockSpec((tk, tn), lambda i, j, k: (k, j))
    tile_spec = pl.BlockSpec((tm, tn), lambda i, j, k: (i, j))

    def body(a_ref, b_ref, *rest):
        extra_refs = rest[:n_extra]
        out_refs = rest[n_extra:n_extra + n_out]
        acc_ref = rest[-1]
        k = pl.program_id(2)

        @pl.when(k == 0)
        def _():
            acc_ref[...] = jnp.zeros_like(acc_ref)

        acc_ref[...] += lax.dot_general(a_ref[...], b_ref[...], dims, preferred_element_type=F32)

        @pl.when(k == nk - 1)
        def _():
            acc = acc_ref[...]
            outs = (acc,) if epilogue is None else epilogue(acc, *[e[...] for e in extra_refs])
            for o_ref, val in zip(out_refs, outs):
                o_ref[...] = val.astype(o_ref.dtype)

    outs = pl.pallas_call(
        body, name=name, grid=(m // tm, n // tn, nk),
        in_specs=[a_spec, b_spec] + [tile_spec] * n_extra,
        out_specs=[tile_spec] * n_out,
        out_shape=[jax.ShapeDtypeStruct((m, n), dt) for dt in out_dtypes],
        scratch_shapes=[pltpu.VMEM((tm, tn), F32)],
        compiler_params=_params(),
    )(a, b, *extras)
    return outs[0] if n_out == 1 else outs


def _cast_bf16(w, name):
    r, c = w.shape
    tr = _tile(r, 256, 16)

    def body(w_ref, o_ref):
        o_ref[...] = w_ref[...].astype(BF16)

    return pl.pallas_call(
        body, name=name, grid=(r // tr,),
        in_specs=[pl.BlockSpec((tr, c), lambda i: (i, 0))],
        out_specs=pl.BlockSpec((tr, c), lambda i: (i, 0)),
        out_shape=jax.ShapeDtypeStruct((r, c), BF16), compiler_params=_params(),
    )(w)


def _in_segments(gw, heads, sw):
    main = 4 * gw
    return [(0, main, 0), (main + 2 * heads, 3 * sw, main), (main, 2 * heads, main + 3 * sw)]


def _pieces(seg_start, width, dst_start, ics):
    out = []
    g = seg_start
    while g < seg_start + width:
        j, cj = divmod(g, ics)
        wdt = min(ics - cj, seg_start + width - g)
        out.append((j, cj, dst_start + (g - seg_start), wdt))
        g += wdt
    return out


def _repack_w_in(w_sh, gw, heads, sw):
    ns, d, ics = w_sh.shape
    main = 4 * gw + 3 * sw
    tr = _tile(d, 128, 16)
    pieces = [p for seg in _in_segments(gw, heads, sw) for p in _pieces(*seg, ics)]

    def body(w_ref, m_ref, ab_ref):
        ab_ref[...] = jnp.zeros_like(ab_ref)
        for j, cj, cd, wdt in pieces:
            if cd >= main:
                ab_ref[:, cd - main:cd - main + wdt] = w_ref[j, :, cj:cj + wdt]
            else:
                m_ref[:, cd:cd + wdt] = w_ref[j, :, cj:cj + wdt]

    return pl.pallas_call(
        body, name="repack_w_in", grid=(d // tr,),
        in_specs=[pl.BlockSpec((ns, tr, ics), lambda i: (0, i, 0))],
        out_specs=[pl.BlockSpec((tr, main), lambda i: (i, 0)), pl.BlockSpec((tr, LANES), lambda i: (i, 0))],
        out_shape=[jax.ShapeDtypeStruct((d, main), BF16), jax.ShapeDtypeStruct((d, LANES), BF16)],
        compiler_params=_params(),
    )(w_sh)


def _unpack_dw_in(dw_main, dw_ab, gw, heads, sw, ics):
    d = dw_main.shape[0]
    tr = _tile(d, 128, 16)
    main = 4 * gw + 3 * sw
    pieces = [p for seg in _in_segments(gw, heads, sw) for p in _pieces(*seg, ics)]

    def body(m_ref, ab_ref, o_ref):
        for j, cj, cd, wdt in pieces:
            if cd >= main:
                o_ref[j, :, cj:cj + wdt] = ab_ref[:, cd - main:cd - main + wdt].astype(BF16)
            else:
                o_ref[j, :, cj:cj + wdt] = m_ref[:, cd:cd + wdt].astype(BF16)

    return pl.pallas_call(
        body, name="unpack_dw_in", grid=(d // tr,),
        in_specs=[pl.BlockSpec((tr, main), lambda i: (i, 0)), pl.BlockSpec((tr, LANES), lambda i: (i, 0))],
        out_specs=pl.BlockSpec((N_CHIPS, tr, ics), lambda i: (0, i, 0)),
        out_shape=jax.ShapeDtypeStruct((N_CHIPS, d, ics), BF16), compiler_params=_params(),
    )(dw_main, dw_ab)


def _adamw(w, g, m, v, name):
    r, c = w.shape
    tr = _tile(r, 128, 8)
    c1 = 1.0 - ADAM_B1 ** ADAM_STEP
    c2 = 1.0 - ADAM_B2 ** ADAM_STEP

    def body(w_ref, g_ref, m_ref, v_ref, d_ref, nm_ref, nv_ref):
        gg = g_ref[...]
        nm = ADAM_B1 * m_ref[...] + (1.0 - ADAM_B1) * gg
        nv = ADAM_B2 * v_ref[...] + (1.0 - ADAM_B2) * jnp.square(gg)
        m_hat = nm / c1
        v_hat = nv / c2
        d_ref[...] = -ADAM_LR * (m_hat / (jnp.sqrt(v_hat) + ADAM_EPS) + ADAM_WD * w_ref[...])
        nm_ref[...] = nm
        nv_ref[...] = nv

    spec = pl.BlockSpec((tr, c), lambda i: (i, 0))
    return pl.pallas_call(
        body, name=name, grid=(r // tr,), in_specs=[spec] * 4, out_specs=[spec] * 3,
        out_shape=[jax.ShapeDtypeStruct((r, c), F32)] * 3, compiler_params=_params(),
    )(w, g, m, v)


def _row_spec(tt, d):
    return pl.BlockSpec((tt, d), lambda i: (i, 0))


def _vec_spec(d):
    return pl.BlockSpec((1, d), lambda i: (0, 0))


def _norm_fwd(x, g):
    t, d = x.shape
    tt = _tile(t, 256, 16)

    def body(x_ref, g_ref, o_ref):
        o_ref[...] = _rms_fwd(x_ref[...], g_ref[...]).astype(BF16)

    return pl.pallas_call(
        body, name="norm_mix_pre", grid=(t // tt,), in_specs=[_row_spec(tt, d), _vec_spec(d)],
        out_specs=_row_spec(tt, d), out_shape=jax.ShapeDtypeStruct((t, d), BF16), compiler_params=_params(),
    )(x, g)


def _mid_fwd(x, mix, g_post, g_pre):
    t, d = x.shape
    tt = _tile(t, 128, 16)

    def body(x_ref, mix_ref, gp_ref, gn_ref, h_ref, hn_ref):
        h = x_ref[...] + _rms_fwd(mix_ref[...], gp_ref[...])
        h_ref[...] = h
        hn_ref[...] = _rms_fwd(h, gn_ref[...]).astype(BF16)

    return pl.pallas_call(
        body, name="mid_fwd", grid=(t // tt,),
        in_specs=[_row_spec(tt, d), _row_spec(tt, d), _vec_spec(d), _vec_spec(d)],
        out_specs=[_row_spec(tt, d), _row_spec(tt, d)],
        out_shape=[jax.ShapeDtypeStruct((t, d), F32), jax.ShapeDtypeStruct((t, d), BF16)],
        compiler_params=_params(),
    )(x, mix, g_post, g_pre)


def _head_fwd_bwd(h, ff, tgt, g_post):
    t, d = h.shape
    tt = _tile(t, 128, 16)

    def body(h_ref, ff_ref, t_ref, g_ref, loss_ref, dy_ref, dff_ref, dg_ref):
        i = pl.program_id(0)

        @pl.when(i == 0)
        def _():
            loss_ref[...] = jnp.zeros_like(loss_ref)
            dg_ref[...] = jnp.zeros_like(dg_ref)

        ff = ff_ref[...]
        g = g_ref[...]
        e = h_ref[...] + _rms_fwd(ff, g) - t_ref[...]
        loss_ref[...] += 0.5 * jnp.sum(jnp.mean(e * e, axis=-1, keepdims=True))
        dy = e * (1.0 / d)
        dy_ref[...] = dy
        dff, dg = _rms_bwd(dy, ff, g)
        dff_ref[...] = dff.astype(BF16)
        dg_ref[...] += dg

    return pl.pallas_call(
        body, name="loss_head", grid=(t // tt,),
        in_specs=[_row_spec(tt, d)] * 3 + [_vec_spec(d)],
        out_specs=[pl.BlockSpec((8, LANES), lambda i: (0, 0)), _row_spec(tt, d), _row_spec(tt, d), _vec_spec(d)],
        out_shape=[jax.ShapeDtypeStruct((8, LANES), F32), jax.ShapeDtypeStruct((t, d), F32),
                   jax.ShapeDtypeStruct((t, d), BF16), jax.ShapeDtypeStruct((1, d), F32)],
        compiler_params=_params(),
    )(h, ff, tgt, g_post)


def _mid_bwd(d_hn, h, g_pre, dy, mix, g_post):
    t, d = h.shape
    tt = _tile(t, 128, 16)

    def body(dhn_ref, h_ref, gn_ref, dy_ref, mix_ref, gp_ref, dh_ref, dmix_ref, dgn_ref, dgp_ref):
        i = pl.program_id(0)

        @pl.when(i == 0)
        def _():
            dgn_ref[...] = jnp.zeros_like(dgn_ref)
            dgp_ref[...] = jnp.zeros_like(dgp_ref)

        dx, dgn = _rms_bwd(dhn_ref[...], h_ref[...], gn_ref[...])
        dh = dy_ref[...] + dx
        dh_ref[...] = dh
        dmix, dgp = _rms_bwd(dh, mix_ref[...], gp_ref[...])
        dmix_ref[...] = dmix.astype(BF16)
        dgn_ref[...] += dgn
        dgp_ref[...] += dgp

    return pl.pallas_call(
        body, name="mid_bwd", grid=(t // tt,),
        in_specs=[_row_spec(tt, d), _row_spec(tt, d), _vec_spec(d), _row_spec(tt, d), _row_spec(tt, d), _vec_spec(d)],
        out_specs=[_row_spec(tt, d), _row_spec(tt, d), _vec_spec(d), _vec_spec(d)],
        out_shape=[jax.ShapeDtypeStruct((t, d), F32), jax.ShapeDtypeStruct((t, d), BF16),
                   jax.ShapeDtypeStruct((1, d), F32), jax.ShapeDtypeStruct((1, d), F32)],
        compiler_params=_params(),
    )(d_hn, h, g_pre, dy, mix, g_post)


def _first_bwd(d_xn, d_xn_ab, x, g, dh):
    t, d = x.shape
    tt = _tile(t, 128, 16)

    def body(a_ref, b_ref, x_ref, g_ref, dh_ref, dx_ref, dg_ref):
        i = pl.program_id(0)

        @pl.when(i == 0)
        def _():
            dg_ref[...] = jnp.zeros_like(dg_ref)

        dx, dg = _rms_bwd(a_ref[...] + b_ref[...], x_ref[...], g_ref[...])
        dx_ref[...] = dh_ref[...] + dx
        dg_ref[...] += dg

    return pl.pallas_call(
        body, name="first_bwd", grid=(t // tt,),
        in_specs=[_row_spec(tt, d), _row_spec(tt, d), _row_spec(tt, d), _vec_spec(d), _row_spec(tt, d)],
        out_specs=[_row_spec(tt, d), _vec_spec(d)],
        out_shape=[jax.ShapeDtypeStruct((t, d), F32), jax.ShapeDtypeStruct((1, d), F32)],
        compiler_params=_params(),
    )(d_xn, d_xn_ab, x, g, dh)


HALO = 8


def _cur(ts, tc, off):
    return pl.BlockSpec((1, ts, tc), lambda ci, b, s: (b, s, off + ci))


def _prev(ts, tc, off):
    return pl.BlockSpec((1, HALO, tc), lambda ci, b, s: (b, jnp.maximum(s * (ts // HALO) - 1, 0), off + ci))


def _next(ts, tc, off, seq):
    last = seq // HALO - 1
    return pl.BlockSpec((1, HALO, tc), lambda ci, b, s: (b, jnp.minimum((s + 1) * (ts // HALO), last), off + ci))


def _conv_w_spec(kw, tc):
    return pl.BlockSpec((kw, tc), lambda ci, b, s: (0, ci))


def _conv_taps(w, buf, kw, ts):
    acc = w[0:1, :] * buf[HALO - (kw - 1):HALO - (kw - 1) + ts, :]
    for j in range(1, kw):
        acc = acc + w[j:j + 1, :] * buf[HALO - (kw - 1) + j:HALO - (kw - 1) + j + ts, :]
    return acc


def _silu_grad(x):
    s = _sigmoid(x)
    return s * (1.0 + x * (1.0 - s))


def _qkv_conv_fwd(proj3, w, width):
    bsz, seq, _ = proj3.shape
    kw = w.shape[0]
    ts, tc = _tile(seq, 256, 8), _tile(width, 512, LANES)

    def body(u_ref, up_ref, w_ref, o_ref, buf):
        s = pl.program_id(2)
        buf[0:HALO, :] = jnp.where(s == 0, 0.0, up_ref[0])
        buf[HALO:HALO + ts, :] = u_ref[0]
        pre = _conv_taps(w_ref[...], buf, kw, ts)
        o_ref[0] = pre * _sigmoid(pre)

    return pl.pallas_call(
        body, name="qkv_conv_fwd", grid=(width // tc, bsz, seq // ts),
        in_specs=[_cur(ts, tc, 0), _prev(ts, tc, 0), _conv_w_spec(kw, tc)],
        out_specs=_cur(ts, tc, 0), out_shape=jax.ShapeDtypeStruct((bsz, seq, width), F32),
        scratch_shapes=[pltpu.VMEM((HALO + ts, tc), F32)], compiler_params=_params(),
    )(proj3, proj3, w)


def _qkv_conv_bwd(proj3, dact3, w, width):
    bsz, seq, _ = proj3.shape
    kw = w.shape[0]
    ts, tc = _tile(seq, 256, 8), _tile(width, 512, LANES)
    n_s = seq // ts

    def body(u_ref, up_ref, un_ref, d_ref, dn_ref, w_ref, du_ref, dw_ref, buf, gbuf):
        b, s = pl.program_id(1), pl.program_id(2)

        @pl.when((b == 0) & (s == 0))
        def _():
            dw_ref[...] = jnp.zeros_like(dw_ref)

        w = w_ref[...]
        buf[0:HALO, :] = jnp.where(s == 0, 0.0, up_ref[0])
        buf[HALO:HALO + ts, :] = u_ref[0]
        buf[HALO + ts:, :] = un_ref[0]
        acc = w[0:1, :] * buf[HALO - (kw - 1):HALO - (kw - 1) + ts + HALO, :]
        for j in range(1, kw):
            acc = acc + w[j:j + 1, :] * buf[HALO - (kw - 1) + j:HALO - (kw - 1) + j + ts + HALO, :]
        gbuf[0:ts, :] = d_ref[0]
        gbuf[ts:, :] = jnp.where(s == n_s - 1, 0.0, dn_ref[0])
        gbuf[...] = gbuf[...] * _silu_grad(acc)
        g_cur = gbuf[0:ts, :]
        for j in range(kw):
            dw_ref[j:j + 1, :] += jnp.sum(g_cur * buf[HALO - (kw - 1) + j:HALO - (kw - 1) + j + ts, :], axis=0, keepdims=True)
        du = w[0:1, :] * gbuf[kw - 1:kw - 1 + ts, :]
        for j in range(1, kw):
            du = du + w[j:j + 1, :] * gbuf[kw - 1 - j:kw - 1 - j + ts, :]
        du_ref[0] = du.astype(BF16)

    return pl.pallas_call(
        body, name="qkv_conv_bwd", grid=(width // tc, bsz, n_s),
        in_specs=[_cur(ts, tc, 0), _prev(ts, tc, 0), _next(ts, tc, 0, seq), _cur(ts, tc, 0), _next(ts, tc, 0, seq),
                  _conv_w_spec(kw, tc)],
        out_specs=[_cur(ts, tc, 0), _conv_w_spec(kw, tc)],
        out_shape=[jax.ShapeDtypeStruct((bsz, seq, width), BF16), jax.ShapeDtypeStruct((kw, width), F32)],
        scratch_shapes=[pltpu.VMEM((HALO + ts + HALO, tc), F32), pltpu.VMEM((ts + HALO, tc), F32)],
        compiler_params=_params(),
    )(proj3, proj3, proj3, dact3, dact3, w)


def _sc_fwd(proj3, w, off, sw):
    bsz, seq, _ = proj3.shape
    kw = w.shape[0]
    ts, tc = _tile(seq, 256, 8), _tile(sw, 512, LANES)
    ob, oc, oh = off // tc, (off + sw) // tc, (off + 2 * sw) // tc

    def body(b_ref, c_ref, cp_ref, h_ref, hp_ref, w_ref, o_ref, buf):
        s = pl.program_id(2)
        buf[0:HALO, :] = jnp.where(s == 0, 0.0, cp_ref[0] * hp_ref[0])
        buf[HALO:HALO + ts, :] = c_ref[0] * h_ref[0]
        o_ref[0] = (b_ref[0] * _conv_taps(w_ref[...], buf, kw, ts)).astype(BF16)

    return pl.pallas_call(
        body, name="sc_fwd", grid=(sw // tc, bsz, seq // ts),
        in_specs=[_cur(ts, tc, ob), _cur(ts, tc, oc), _prev(ts, tc, oc), _cur(ts, tc, oh), _prev(ts, tc, oh),
                  _conv_w_spec(kw, tc)],
        out_specs=_cur(ts, tc, 0), out_shape=jax.ShapeDtypeStruct((bsz, seq, sw), BF16),
        scratch_shapes=[pltpu.VMEM((HALO + ts, tc), F32)], compiler_params=_params(),
    )(proj3, proj3, proj3, proj3, proj3, w)


def _sc_bwd(proj3, dmixed3, w, off, sw, d_off):
    bsz, seq, _ = proj3.shape
    kw = w.shape[0]
    ts, tc = _tile(seq, 256, 8), _tile(sw, 512, LANES)
    n_s = seq // ts
    ob, oc, oh, od = off // tc, (off + sw) // tc, (off + 2 * sw) // tc, d_off // tc

    def body(d_ref, dn_ref, b_ref, bn_ref, c_ref, cp_ref, h_ref, hp_ref, w_ref,
             db_ref, dc_ref, dh_ref, dw_ref, buf, gbuf):
        b, s = pl.program_id(1), pl.program_id(2)

        @pl.when((b == 0) & (s == 0))
        def _():
            dw_ref[...] = jnp.zeros_like(dw_ref)

        w = w_ref[...]
        cc, hh = c_ref[0], h_ref[0]
        buf[0:HALO, :] = jnp.where(s == 0, 0.0, cp_ref[0] * hp_ref[0])
        buf[HALO:HALO + ts, :] = cc * hh
        dout = d_ref[0]
        db_ref[0] = (dout * _conv_taps(w, buf, kw, ts)).astype(BF16)
        g_cur = dout * b_ref[0]
        gbuf[0:ts, :] = g_cur
        gbuf[ts:, :] = jnp.where(s == n_s - 1, 0.0, dn_ref[0] * bn_ref[0])
        for j in range(kw):
            dw_ref[j:j + 1, :] += jnp.sum(g_cur * buf[HALO - (kw - 1) + j:HALO - (kw - 1) + j + ts, :], axis=0, keepdims=True)
        dp = w[0:1, :] * gbuf[kw - 1:kw - 1 + ts, :]
        for j in range(1, kw):
            dp = dp + w[j:j + 1, :] * gbuf[kw - 1 - j:kw - 1 - j + ts, :]
        dc_ref[0] = (dp * hh).astype(BF16)
        dh_ref[0] = (dp * cc).astype(BF16)

    out = jax.ShapeDtypeStruct((bsz, seq, sw), BF16)
    return pl.pallas_call(
        body, name="sc_bwd", grid=(sw // tc, bsz, n_s),
        in_specs=[_cur(ts, tc, od), _next(ts, tc, od, seq), _cur(ts, tc, ob), _next(ts, tc, ob, seq),
                  _cur(ts, tc, oc), _prev(ts, tc, oc), _cur(ts, tc, oh), _prev(ts, tc, oh), _conv_w_spec(kw, tc)],
        out_specs=[_cur(ts, tc, 0)] * 3 + [_conv_w_spec(kw, tc)],
        out_shape=[out, out, out, jax.ShapeDtypeStruct((kw, sw), F32)],
        scratch_shapes=[pltpu.VMEM((HALO + ts, tc), F32), pltpu.VMEM((ts + HALO, tc), F32)],
        compiler_params=_params(),
    )(dmixed3, dmixed3, proj3, proj3, proj3, proj3, proj3, proj3, w)


def _tri_ones(lower):
    i = lax.broadcasted_iota(jnp.int32, (CHUNK, CHUNK), 0)
    j = lax.broadcasted_iota(jnp.int32, (CHUNK, CHUNK), 1)
    return jnp.where((i >= j) if lower else (j >= i), 1.0, 0.0).astype(F32)


def _gates_fwd(proj_ab, a_log_pad, dt_pad, heads):
    t = proj_ab.shape[0]
    gw = heads * LANES

    def body(ab_ref, al_ref, dt_ref, gc_ref, beta_ref):
        ab = ab_ref[...]
        g = -jnp.exp(al_ref[...]) * _softplus(ab + dt_ref[...])
        gc = jnp.dot(_tri_ones(True), g, precision=HI, preferred_element_type=F32)
        beta = _sigmoid(ab)
        for h in range(heads):
            gc_ref[:, h * LANES:(h + 1) * LANES] = jnp.broadcast_to(gc[:, h:h + 1], (CHUNK, LANES))
            beta_ref[:, h * LANES:(h + 1) * LANES] = jnp.broadcast_to(beta[:, heads + h:heads + h + 1], (CHUNK, LANES))

    return pl.pallas_call(
        body, name="gates_fwd", grid=(t // CHUNK,),
        in_specs=[_row_spec(CHUNK, LANES), _vec_spec(LANES), _vec_spec(LANES)],
        out_specs=[_row_spec(CHUNK, gw), _row_spec(CHUNK, gw)],
        out_shape=[jax.ShapeDtypeStruct((t, gw), F32)] * 2, compiler_params=_params(),
    )(proj_ab, a_log_pad, dt_pad)


def _gates_bwd(proj_ab, a_log_pad, dt_pad, dgc_b, dbeta_b, heads):
    t = proj_ab.shape[0]
    gw = heads * LANES

    def body(ab_ref, al_ref, dt_ref, dgc_ref, dbeta_ref, dab_ref, dal_ref, ddt_ref):
        i = pl.program_id(0)

        @pl.when(i == 0)
        def _():
            dal_ref[...] = jnp.zeros_like(dal_ref)
            ddt_ref[...] = jnp.zeros_like(ddt_ref)

        lane = lax.broadcasted_iota(jnp.int32, (CHUNK, LANES), 1)
        dgc = jnp.zeros((CHUNK, LANES), F32)
        dbeta = jnp.zeros((CHUNK, LANES), F32)
        for h in range(heads):
            dgc = jnp.where(lane == h, dgc_ref[:, h * LANES:(h + 1) * LANES], dgc)
            dbeta = jnp.where(lane == heads + h, dbeta_ref[:, h * LANES:(h + 1) * LANES], dbeta)
        dg = jnp.dot(_tri_ones(False), dgc, precision=HI, preferred_element_type=F32)
        ab = ab_ref[...]
        z = ab + dt_ref[...]
        ea = jnp.exp(al_ref[...])
        da = dg * (-ea) * _sigmoid(z)
        beta = _sigmoid(ab)
        db = dbeta * beta * (1.0 - beta)
        dab_ref[...] = jnp.where(lane < heads, da, jnp.where(lane < 2 * heads, db, 0.0)).astype(BF16)
        da_m = jnp.where(lane < heads, da, 0.0)
        ddt_ref[...] += jnp.sum(da_m, axis=0, keepdims=True)
        dal_ref[...] += jnp.sum(jnp.where(lane < heads, dg * (-ea) * _softplus(z), 0.0), axis=0, keepdims=True)

    return pl.pallas_call(
        body, name="gates_bwd", grid=(t // CHUNK,),
        in_specs=[_row_spec(CHUNK, LANES), _vec_spec(LANES), _vec_spec(LANES), _row_spec(CHUNK, gw), _row_spec(CHUNK, gw)],
        out_specs=[_row_spec(CHUNK, LANES), _vec_spec(LANES), _vec_spec(LANES)],
        out_shape=[jax.ShapeDtypeStruct((t, LANES), BF16), jax.ShapeDtypeStruct((1, LANES), F32),
                   jax.ShapeDtypeStruct((1, LANES), F32)],
        compiler_params=_params(),
    )(proj_ab, a_log_pad, dt_pad, dgc_b, dbeta_b)


def _dot(a, b, dims, hi=False):
    if hi:
        return lax.dot_general(a, b, (dims, ((), ())), precision=HI, preferred_element_type=F32)
    return lax.dot_general(a.astype(BF16), b.astype(BF16), (dims, ((), ())), preferred_element_type=F32)


NN = ((1,), (0,))
NT = ((1,), (1,))
TN = ((0,), (0,))


def _unit_lower_inverse(m):
    i = lax.broadcasted_iota(jnp.int32, (CHUNK, CHUNK), 0)
    j = lax.broadcasted_iota(jnp.int32, (CHUNK, CHUNK), 1)
    t = jnp.where(i == j, 1.0, 0.0).astype(F32)
    shift = 0
    while (1 << shift) < CHUNK:
        same_pair = jnp.right_shift(i, shift + 1) == jnp.right_shift(j, shift + 1)
        other_half = jnp.right_shift(i, shift) != jnp.right_shift(j, shift)
        off = jnp.where(same_pair & other_half, m, 0.0)
        t = t - _dot(_dot(t, off, NN, hi=True), t, NN, hi=True)
        shift += 1
    return t


def _chunk_local(qr, kr, v, gcb, betab, head_dim):
    i = lax.broadcasted_iota(jnp.int32, (CHUNK, CHUNK), 0)
    j = lax.broadcasted_iota(jnp.int32, (CHUNK, CHUNK), 1)
    rq = lax.rsqrt(jnp.sum(qr * qr, axis=-1, keepdims=True) + L2_EPS)
    rk = lax.rsqrt(jnp.sum(kr * kr, axis=-1, keepdims=True) + L2_EPS)
    scale = head_dim ** -0.5
    qh, k = qr * rq, kr * rk
    q = qh * scale
    g_col = gcb[:, 0:CHUNK]
    g_row = gcb.T[0:CHUNK, :]
    decay = jnp.exp(jnp.where(i >= j, g_col - g_row, -jnp.inf))
    beta_col = betab[:, 0:CHUNK]
    kk = _dot(k, k, NT)
    m = jnp.where(i > j, beta_col * kk * decay, 0.0)
    tinv = _unit_lower_inverse(m)
    eg = jnp.exp(gcb)
    rhs_w = k * betab * eg
    u = _dot(tinv, v * betab, NN, hi=True)
    w = _dot(tinv, rhs_w, NN, hi=True)
    qk = _dot(q, k, NT)
    p = qk * decay
    g_last = gcb[CHUNK - 1:CHUNK, :]
    e_last = jnp.exp(g_last - gcb)
    return dict(rq=rq, rk=rk, qh=qh, q=q, k=k, decay=decay, beta_col=beta_col, kk=kk, m=m, tinv=tinv, eg=eg,
                rhs_w=rhs_w, u=u, w=w, qk=qk, p=p, qd=q * eg, kd=k * e_last, e_last=e_last,
                gl=jnp.exp(g_last), scale=scale, strict=i > j, incl=i >= j)


def _gdn_specs(n_chunks, heads, reverse):
    def cidx(c):
        return (n_chunks - 1 - c) if reverse else c

    def tok(off):
        return pl.BlockSpec((CHUNK, LANES), lambda b, h, c: (b * n_chunks + cidx(c), off + h))

    state = pl.BlockSpec((1, LANES, LANES), lambda b, h, c: ((b * heads + h) * n_chunks + cidx(c), 0, 0))
    return tok, state


def _gdn_fwd(qkv_act, gcb, betab, bsz, heads):
    t = qkv_act.shape[0]
    n_chunks = t // bsz // CHUNK
    tok, state = _gdn_specs(n_chunks, heads, False)

    def body(q_ref, k_ref, v_ref, gc_ref, beta_ref, o_ref, s_ref, st):
        @pl.when(pl.program_id(2) == 0)
        def _():
            st[...] = jnp.zeros_like(st)

        v = v_ref[...]
        loc = _chunk_local(q_ref[...], k_ref[...], v, gc_ref[...], beta_ref[...], LANES)
        s0 = st[...]
        s_ref[0] = s0
        v_new = loc["u"] - _dot(loc["w"], s0, NN)
        o_ref[...] = _dot(loc["qd"], s0, NN) + _dot(loc["p"], v_new, NN)
        st[...] = s0 * loc["gl"] + _dot(loc["kd"], v_new, TN)

    return pl.pallas_call(
        body, name="gdn_fwd", grid=(bsz, heads, n_chunks),
        in_specs=[tok(0), tok(heads), tok(2 * heads), tok(0), tok(0)],
        out_specs=[tok(0), state],
        out_shape=[jax.ShapeDtypeStruct((t, heads * LANES), F32),
                   jax.ShapeDtypeStruct((bsz * heads * n_chunks, LANES, LANES), F32)],
        scratch_shapes=[pltpu.VMEM((LANES, LANES), F32)], compiler_params=_params(),
    )(qkv_act, qkv_act, qkv_act, gcb, betab)


def _gdn_bwd(qkv_act, gcb, betab, states, d_o, bsz, heads):
    t = qkv_act.shape[0]
    n_chunks = t // bsz // CHUNK
    tok, state = _gdn_specs(n_chunks, heads, True)

    def rowsum(a):
        return jnp.sum(a, axis=-1, keepdims=True)

    def body(q_ref, k_ref, v_ref, gc_ref, beta_ref, s_ref, do_ref, dq_ref, dk_ref, dv_ref, dgc_ref, dbeta_ref, dst):
        @pl.when(pl.program_id(2) == 0)
        def _():
            dst[...] = jnp.zeros_like(dst)

        qr, kr, v = q_ref[...], k_ref[...], v_ref[...]
        betab = beta_ref[...]
        L = _chunk_local(qr, kr, v, gc_ref[...], betab, LANES)
        q, k, u, w, p, tinv, decay = L["q"], L["k"], L["u"], L["w"], L["p"], L["tinv"], L["decay"]
        s0 = s_ref[0]
        d_out = do_ref[...]
        ds1 = dst[...]
        v_new = u - _dot(w, s0, NN)

        d_vnew = _dot(p, d_out, TN) + _dot(L["kd"], ds1, NN)
        d_qd = _dot(d_out, s0, NT)
        d_kd = _dot(v_new, ds1, NT)
        d_gl = jnp.sum(ds1 * s0)
        d_p = jnp.where(L["incl"], _dot(d_out, v_new, NT), 0.0)
        d_w = -_dot(d_vnew, s0, NT)
        dst[...] = _dot(L["qd"], d_out, TN) + ds1 * L["gl"] - _dot(w, d_vnew, TN)

        d_rhs_u = _dot(tinv, d_vnew, TN, hi=True)
        d_rhs_w = _dot(tinv, d_w, TN, hi=True)
        d_a = -(_dot(d_rhs_u, u, NT, hi=True) + _dot(d_rhs_w, w, NT, hi=True))
        d_m = jnp.where(L["strict"], d_a, 0.0)
        g_kk = d_m * L["beta_col"] * decay
        h_qk = d_p * decay
        e = d_m * L["m"] + d_p * p

        d_q = _dot(h_qk, k, NN) + L["eg"] * d_qd
        d_k = (_dot(g_kk, k, NN) + _dot(g_kk, k, TN) + _dot(h_qk, q, TN)
               + betab * L["eg"] * d_rhs_w + L["e_last"] * d_kd)
        dv_ref[...] = betab * d_rhs_u

        d_beta = rowsum(d_m * L["kk"] * decay)[:, 0:1] + rowsum(d_rhs_u * v) + rowsum(d_rhs_w * k * L["eg"])
        s_kd = rowsum(d_kd * L["kd"])
        d_gc = (rowsum(e) - rowsum(e.T) + rowsum(d_rhs_w * L["rhs_w"]) + rowsum(d_qd * L["qd"]) - s_kd)
        row = lax.broadcasted_iota(jnp.int32, (CHUNK, 1), 0)
        d_gc = d_gc + jnp.where(row == CHUNK - 1, jnp.sum(s_kd) + d_gl * jnp.sum(L["gl"][:, 0:1]), 0.0)
        dgc_ref[...] = jnp.broadcast_to(d_gc, (CHUNK, LANES))
        dbeta_ref[...] = jnp.broadcast_to(d_beta, (CHUNK, LANES))

        d_qh = d_q * L["scale"]
        dq_ref[...] = L["rq"] * (d_qh - L["qh"] * rowsum(d_qh * L["qh"]))
        dk_ref[...] = L["rk"] * (d_k - k * rowsum(d_k * k))

    tok_shape = jax.ShapeDtypeStruct((t, heads * LANES), F32)
    return pl.pallas_call(
        body, name="gdn_bwd", grid=(bsz, heads, n_chunks),
        in_specs=[tok(0), tok(heads), tok(2 * heads), tok(0), tok(0), state, tok(0)],
        out_specs=[tok(0)] * 5,
        out_shape=[tok_shape] * 5,
        scratch_shapes=[pltpu.VMEM((LANES, LANES), F32)], compiler_params=_params(),
    )(qkv_act, qkv_act, qkv_act, gcb, betab, states, d_o)


def _gdn_out_fwd(o, proj, gw_norm, heads, z_off):
    t = o.shape[0]
    ts = _tile(t, 512, 16)
    zb = z_off // LANES

    def body(o_ref, z_ref, w_ref, out_ref):
        z = z_ref[...]
        out_ref[...] = (_rms_fwd(o_ref[...], w_ref[...]) * (z * _sigmoid(z))).astype(BF16)

    return pl.pallas_call(
        body, name="gdn_out_fwd", grid=(t // ts, heads),
        in_specs=[pl.BlockSpec((ts, LANES), lambda i, h: (i, h)), pl.BlockSpec((ts, LANES), lambda i, h: (i, zb + h)),
                  pl.BlockSpec((1, LANES), lambda i, h: (0, 0))],
        out_specs=pl.BlockSpec((ts, LANES), lambda i, h: (i, h)),
        out_shape=jax.ShapeDtypeStruct((t, heads * LANES), BF16), compiler_params=_params(),
    )(o, proj, gw_norm)


def _gdn_out_bwd(dmixed, o, proj, gw_norm, heads, z_off):
    t = o.shape[0]
    ts = _tile(t, 512, 16)
    zb = z_off // LANES

    def body(d_ref, o_ref, z_ref, w_ref, do_ref, dz_ref, dw_ref):
        @pl.when((pl.program_id(0) == 0) & (pl.program_id(1) == 0))
        def _():
            dw_ref[...] = jnp.zeros_like(dw_ref)

        d, oo, z, w = d_ref[...], o_ref[...], z_ref[...], w_ref[...]
        on = _rms_fwd(oo, w)
        dz_ref[...] = (d * on * _silu_grad(z)).astype(BF16)
        d_o, d_w = _rms_bwd(d * (z * _sigmoid(z)), oo, w)
        do_ref[...] = d_o
        dw_ref[...] += d_w

    blk = pl.BlockSpec((ts, LANES), lambda i, h: (i, h))
    vec = pl.BlockSpec((1, LANES), lambda i, h: (0, 0))
    return pl.pallas_call(
        body, name="gdn_out_bwd", grid=(t // ts, heads),
        in_specs=[blk, blk, pl.BlockSpec((ts, LANES), lambda i, h: (i, zb + h)), vec],
        out_specs=[blk, blk, vec],
        out_shape=[jax.ShapeDtypeStruct((t, heads * LANES), F32), jax.ShapeDtypeStruct((t, heads * LANES), BF16),
                   jax.ShapeDtypeStruct((1, LANES), F32)],
        compiler_params=_params(),
    )(dmixed, o, proj, gw_norm)


def _place():
    x, y, c = lax.axis_index("x"), lax.axis_index("y"), lax.axis_index("c")
    return x, y, c, [(1 - x, y), (x, 1 - y), (1 - x, 1 - y)]


def _aligned(start, align):
    return start if isinstance(start, int) else pl.multiple_of(start, align)


class _Layout:
    def __init__(self, kind, shard_shape):
        self.kind = kind
        self.r, self.c = shard_shape

    def full_shape(self):
        r, c = self.r, self.c
        return {"major": (N_CHIPS, r, c), "rows": (N_CHIPS * r, c), "cols": (r, N_CHIPS * c)}[self.kind]

    def region(self, ref, j, half=None):
        r, c = self.r, self.c
        r0, nr = (0, r) if half is None else (half * (r // 2), r // 2)
        if self.kind == "major":
            return ref.at[j, pl.ds(_aligned(r0, 16), nr), :]
        if self.kind == "rows":
            return ref.at[pl.ds(_aligned(j * r + r0, 16), nr), :]
        return ref.at[pl.ds(_aligned(r0, 16), nr), pl.ds(_aligned(j * c, LANES), c)]

    def shard_half(self, ref, half):
        return ref.at[pl.ds(_aligned(half * (self.r // 2), 16), self.r // 2), :]


def _remote(src, dst, send_sem, recv_sem, dev):
    return pltpu.make_async_remote_copy(src_ref=src, dst_ref=dst, send_sem=send_sem, recv_sem=recv_sem,
                                        device_id=dev, device_id_type=MESH)


def _all_gather(big, layouts, small):
    nb, ns = len(big), len(small)
    n_remote = 6 * nb + 3 * ns

    def body(*refs):
        ins, outs = refs[:nb + ns], refs[nb + ns:2 * (nb + ns)]
        send_sems, recv_sems, local_sems = refs[2 * (nb + ns):]
        x, y, c, chips = _place()
        j = 2 * x + y
        local = []
        for i in range(nb):
            local.append(pltpu.make_async_copy(ins[i], layouts[i].region(outs[i], j), local_sems.at[i]))
        for i in range(ns):
            local.append(pltpu.make_async_copy(ins[nb + i], outs[nb + i].at[j], local_sems.at[nb + i]))
        for cp in local:
            cp.start()
        sends = []
        for i in range(nb):
            for p, (px, py) in enumerate(chips):
                k = 3 * i + p
                sends.append(_remote(layouts[i].shard_half(ins[i], c), layouts[i].region(outs[i], j, c),
                                     send_sems.at[k], recv_sems.at[k], (px, py, c)))
        for i in range(ns):
            for p, (px, py) in enumerate(chips):
                k = 6 * nb + 3 * i + p
                sends.append(_remote(ins[nb + i], outs[nb + i].at[j], send_sems.at[k], recv_sems.at[k], (px, py, c)))
        for cp in sends:
            cp.start()
        for i in range(nb):
            for p, (px, py) in enumerate(chips):
                k, jp = 3 * i + p, 2 * px + py
                got = layouts[i].region(outs[i], jp, c)
                _remote(got, got, send_sems.at[k], recv_sems.at[k], (px, py, c)).wait_recv()
                fwd = _remote(got, got, send_sems.at[3 * nb + k], recv_sems.at[3 * nb + k], (x, y, 1 - c))
                fwd.start()
                sends.append(fwd)
        for i in range(ns):
            for p, (px, py) in enumerate(chips):
                k, jp = 6 * nb + 3 * i + p, 2 * px + py
                _remote(ins[nb + i], outs[nb + i].at[jp], send_sems.at[k], recv_sems.at[k], (px, py, c)).wait_recv()
        for i in range(nb):
            for p, (px, py) in enumerate(chips):
                k, jp = 3 * nb + 3 * i + p, 2 * px + py
                got = layouts[i].region(outs[i], jp, 1 - c)
                _remote(got, got, send_sems.at[k], recv_sems.at[k], (x, y, 1 - c)).wait_recv()
        for cp in sends:
            cp.wait_send()
        for cp in local:
            cp.wait()

    out_shape = [jax.ShapeDtypeStruct(lay.full_shape(), BF16) for lay in layouts]
    out_shape += [jax.ShapeDtypeStruct((N_CHIPS,) + s.shape, F32) for s in small]
    return pl.pallas_call(
        body, name="all_gather_weights", in_specs=[ANY] * (nb + ns), out_specs=[ANY] * (nb + ns), out_shape=out_shape,
        scratch_shapes=[pltpu.SemaphoreType.DMA((n_remote,)), pltpu.SemaphoreType.DMA((n_remote,)),
                        pltpu.SemaphoreType.DMA((nb + ns,))],
        compiler_params=_params(),
    )(*big, *small)


def _halves_to_sibling(grads, layouts):
    nw = len(grads)

    def body(*refs):
        ins, keeps, gots = refs[:nw], refs[nw:2 * nw], refs[2 * nw:3 * nw]
        send_sems, recv_sems, local_sems = refs[3 * nw:]
        x, y, c, _ = _place()
        cps, local = [], []
        for i in range(nw):
            for j in range(N_CHIPS):
                k = N_CHIPS * i + j
                local.append(pltpu.make_async_copy(layouts[i].region(ins[i], j, c), keeps[i].at[j], local_sems.at[k]))
                cps.append(_remote(layouts[i].region(ins[i], j, 1 - c), gots[i].at[j],
                                   send_sems.at[k], recv_sems.at[k], (x, y, 1 - c)))
        for cp in local + cps:
            cp.start()
        for cp in cps:
            cp.wait()
        for cp in local:
            cp.wait()

    half = [jax.ShapeDtypeStruct((N_CHIPS, lay.r // 2, lay.c), BF16) for lay in layouts]
    outs = pl.pallas_call(
        body, name="grad_halves_to_sibling", in_specs=[ANY] * nw, out_specs=[ANY] * (2 * nw), out_shape=half + half,
        scratch_shapes=[pltpu.SemaphoreType.DMA((N_CHIPS * nw,)), pltpu.SemaphoreType.DMA((N_CHIPS * nw,)),
                        pltpu.SemaphoreType.DMA((N_CHIPS * nw,))],
        compiler_params=_params(),
    )(*grads)
    return outs[:nw], outs[nw:]


def _add_bf16(a, b, name):
    n, r, c = a.shape
    tr = _tile(r, 256, 16)

    def body(a_ref, b_ref, o_ref):
        o_ref[...] = (a_ref[...].astype(F32) + b_ref[...].astype(F32)).astype(BF16)

    spec = pl.BlockSpec((1, tr, c), lambda j, i: (j, i, 0))
    return pl.pallas_call(
        body, name=name, grid=(n, r // tr), in_specs=[spec, spec], out_specs=spec,
        out_shape=jax.ShapeDtypeStruct((n, r, c), BF16), compiler_params=_params(),
    )(a, b)


def _partials_to_owner(parts):
    nw = len(parts)

    def body(*refs):
        ins, mine, gots = refs[:nw], refs[nw:2 * nw], refs[2 * nw:3 * nw]
        send_sems, recv_sems, local_sems = refs[3 * nw:]
        x, y, c, chips = _place()
        j = 2 * x + y
        cps, local = [], []
        for i in range(nw):
            local.append(pltpu.make_async_copy(ins[i].at[j], mine[i], local_sems.at[i]))
            for p, (px, py) in enumerate(chips):
                k = 3 * i + p
                cps.append(_remote(ins[i].at[2 * px + py], gots[i].at[p], send_sems.at[k], recv_sems.at[k], (px, py, c)))
        for cp in local + cps:
            cp.start()
        for cp in cps:
            cp.wait()
        for cp in local:
            cp.wait()

    own = [jax.ShapeDtypeStruct(p.shape[1:], BF16) for p in parts]
    got = [jax.ShapeDtypeStruct((3,) + p.shape[1:], BF16) for p in parts]
    outs = pl.pallas_call(
        body, name="grad_partials_to_owner", in_specs=[ANY] * nw, out_specs=[ANY] * (2 * nw), out_shape=own + got,
        scratch_shapes=[pltpu.SemaphoreType.DMA((3 * nw,)), pltpu.SemaphoreType.DMA((3 * nw,)),
                        pltpu.SemaphoreType.DMA((nw,))],
        compiler_params=_params(),
    )(*parts)
    return outs[:nw], outs[nw:]


def _sum4(own, got, name):
    r, c = own.shape
    tr = _tile(r, 256, 16)

    def body(o_ref, g_ref, out_ref):
        acc = o_ref[...].astype(F32)
        for p in range(3):
            acc = acc + g_ref[p].astype(F32)
        out_ref[...] = acc

    return pl.pallas_call(
        body, name=name, grid=(r // tr,),
        in_specs=[pl.BlockSpec((tr, c), lambda i: (i, 0)), pl.BlockSpec((3, tr, c), lambda i: (0, i, 0))],
        out_specs=pl.BlockSpec((tr, c), lambda i: (i, 0)),
        out_shape=jax.ShapeDtypeStruct((r, c), F32), compiler_params=_params(),
    )(own, got)


def _join_halves(halves):
    nw = len(halves)

    def body(*refs):
        ins, outs = refs[:nw], refs[nw:2 * nw]
        send_sems, recv_sems, local_sems = refs[2 * nw:]
        x, y, c, _ = _place()
        cps, local = [], []
        for i in range(nw):
            hr = ins[i].shape[0]
            mine = outs[i].at[pl.ds(pl.multiple_of(c * hr, 8), hr), :]
            local.append(pltpu.make_async_copy(ins[i], mine, local_sems.at[i]))
            cps.append(_remote(ins[i], mine, send_sems.at[i], recv_sems.at[i], (x, y, 1 - c)))
        for cp in local + cps:
            cp.start()
        for i, cp in enumerate(cps):
            cp.wait_send()
            hr = ins[i].shape[0]
            theirs = outs[i].at[pl.ds(pl.multiple_of((1 - c) * hr, 8), hr), :]
            _remote(ins[i], theirs, send_sems.at[i], recv_sems.at[i], (x, y, 1 - c)).wait_recv()
        for cp in local:
            cp.wait()

    full = [jax.ShapeDtypeStruct((2 * h.shape[0], h.shape[1]), F32) for h in halves]
    return pl.pallas_call(
        body, name="grad_join_halves", in_specs=[ANY] * nw, out_specs=[ANY] * nw, out_shape=full,
        scratch_shapes=[pltpu.SemaphoreType.DMA((nw,)), pltpu.SemaphoreType.DMA((nw,)), pltpu.SemaphoreType.DMA((nw,))],
        compiler_params=_params(),
    )(*halves)


def _small_all_reduce(buf):
    rows = buf.shape[0]
    n_dev = 8

    def body(b_ref, o_ref, gath, send_sems, recv_sems):
        x, y, c, _ = _place()
        me = 4 * x + 2 * y + c
        gath[me] = b_ref[...]
        cps = []
        for k in range(1, n_dev):
            px, py, pc = (x + (k >> 2)) % 2, (y + ((k >> 1) & 1)) % 2, (c + (k & 1)) % 2
            cps.append(_remote(b_ref, gath.at[me], send_sems.at[k - 1], recv_sems.at[k - 1], (px, py, pc)))
        for cp in cps:
            cp.start()
        for k in range(1, n_dev):
            px, py, pc = (x + (k >> 2)) % 2, (y + ((k >> 1) & 1)) % 2, (c + (k & 1)) % 2
            _remote(b_ref, gath.at[4 * px + 2 * py + pc], send_sems.at[k - 1], recv_sems.at[k - 1], (px, py, pc)).wait_recv()
        for cp in cps:
            cp.wait_send()
        acc = gath[0]
        for dev in range(1, n_dev):
            acc = acc + gath[dev]
        o_ref[...] = acc

    vm = pl.BlockSpec(memory_space=pltpu.VMEM)
    return pl.pallas_call(
        body, name="small_all_reduce", in_specs=[vm], out_specs=vm,
        out_shape=jax.ShapeDtypeStruct((rows, LANES), F32),
        scratch_shapes=[pltpu.VMEM((n_dev, rows, LANES), F32), pltpu.SemaphoreType.DMA((n_dev - 1,)),
                        pltpu.SemaphoreType.DMA((n_dev - 1,))],
        compiler_params=_params(),
    )(buf)


def _pad_lanes(v):
    return jnp.pad(v, ((0, 0), (0, LANES - v.shape[-1])))


def _pack(vectors):
    flat, offs, pos = [], [], 0
    for v in vectors:
        n = v.size
        n_pad = -(-n // LANES) * LANES
        flat.append(jnp.pad(v.reshape(-1), (0, n_pad - n)))
        offs.append((pos, n, v.shape))
        pos += n_pad
    total = -(-pos // (8 * LANES)) * 8 * LANES
    flat.append(jnp.zeros((total - pos,), F32))
    return jnp.concatenate(flat).reshape(-1, LANES), offs


def _unpack(buf, offs):
    flat = buf.reshape(-1)
    return [flat[pos:pos + n].reshape(shape) for pos, n, shape in offs]


def kernel(x, norm_mix_pre, w_in, conv_qkv_w, a_log, dt_bias, gdn_norm_w, conv_sc_w, w_out, norm_mix_post, norm_mlp_pre, w_up, w_down, norm_mlp_post, loss_target, m_norm_mix_pre, m_w_in, m_conv_qkv_w, m_a_log, m_dt_bias, m_gdn_norm_w, m_conv_sc_w, m_w_out, m_norm_mix_post, m_norm_mlp_pre, m_w_up, m_w_down, m_norm_mlp_post, v_norm_mix_pre, v_w_in, v_conv_qkv_w, v_a_log, v_dt_bias, v_gdn_norm_w, v_conv_sc_w, v_w_out, v_norm_mix_post, v_norm_mlp_pre, v_w_up, v_w_down, v_norm_mlp_post):
    bsz, seq, d = x.shape
    t = bsz * seq
    heads, head_dim = a_log.shape[-1], gdn_norm_w.shape[-1]
    assert head_dim == LANES and seq % CHUNK == 0
    gw = heads * head_dim
    sw = conv_sc_w.shape[-1] * N_CHIPS
    ics = w_in.shape[-1]
    main = 4 * gw + 3 * sw
    assert ics * N_CHIPS == main + 2 * heads and 2 * heads <= LANES

    lay_in = _Layout("major", w_in.shape[1:])
    lay_out = _Layout("rows", w_out.shape[1:])
    lay_up = _Layout("cols", w_up.shape[1:])
    lay_down = _Layout("rows", w_down.shape[1:])
    layouts = [lay_in, lay_out, lay_up, lay_down]
    shards = [_cast_bf16(w[0], f"cast_{n}") for w, n in ((w_in, "w_in"), (w_out, "w_out"), (w_up, "w_up"), (w_down, "w_down"))]
    win_sh, wout_f, wup_f, wdown_f, cq_g, cs_g = _all_gather(shards, layouts, [conv_qkv_w[0], conv_sc_w[0]])
    w_main, w_ab = _repack_w_in(win_sh, gw, heads, sw)
    conv_q = cq_g.transpose(1, 0, 2).reshape(conv_qkv_w.shape[1], -1)
    conv_s = cs_g.transpose(1, 0, 2).reshape(conv_sc_w.shape[1], -1)

    x2 = x.reshape(t, d)
    tgt2 = loss_target.reshape(t, d)
    xn = _norm_fwd(x2, norm_mix_pre)
    proj = _matmul(xn, w_main, "nn", [F32], "proj_main")
    proj_ab = _matmul(xn, w_ab, "nn", [F32], "proj_ab")
    proj3 = proj.reshape(bsz, seq, main)
    qkv_act = _qkv_conv_fwd(proj3, conv_q, 3 * gw).reshape(t, 3 * gw)
    a_log_pad, dt_pad = _pad_lanes(a_log), _pad_lanes(dt_bias)
    gcb, betab = _gates_fwd(proj_ab, a_log_pad, dt_pad, heads)
    o_raw, states = _gdn_fwd(qkv_act, gcb, betab, bsz, heads)
    gdn_out = _gdn_out_fwd(o_raw, proj, gdn_norm_w, heads, 3 * gw)
    sc_out = _sc_fwd(proj3, conv_s, 4 * gw, sw).reshape(t, sw)
    mixed = jnp.concatenate([gdn_out, sc_out], axis=1)
    mix = _matmul(mixed, wout_f, "nn", [F32], "mix_out")
    h, hn = _mid_fwd(x2, mix, norm_mix_post, norm_mlp_pre)

    def up_epilogue(acc):
        r = jnp.maximum(acc, 0.0)
        return r, r * r

    relu_up, hid = _matmul(hn, wup_f, "nn", [BF16, BF16], "mlp_up", epilogue=up_epilogue)
    ff = _matmul(hid, wdown_f, "nn", [F32], "mlp_down")
    loss_blk, dy, dff, dg_mlp_post = _head_fwd_bwd(h, ff, tgt2, norm_mlp_post)

    def dup_epilogue(acc, r):
        return (acc * (2.0 * r.astype(F32)),)

    d_up = _matmul(dff, wdown_f, "nt", [BF16], "d_hid", epilogue=dup_epilogue, extras=(relu_up,))
    dw_down = _matmul(hid, dff, "tn", [BF16], "dw_down")
    d_hn = _matmul(d_up, wup_f, "nt", [F32], "d_hn")
    dw_up = _matmul(hn, d_up, "tn", [BF16], "dw_up")
    dh, dmix, dg_mlp_pre, dg_mix_post = _mid_bwd(d_hn, h, norm_mlp_pre, dy, mix, norm_mix_post)
    dmixed = _matmul(dmix, wout_f, "nt", [F32], "d_mixed")
    dw_out = _matmul(mixed, dmix, "tn", [BF16], "dw_out")
    dmixed3 = dmixed.reshape(bsz, seq, d)
    d_b, d_c, d_hsc, dw_conv_s = _sc_bwd(proj3, dmixed3, conv_s, 4 * gw, sw, gw)
    d_o, d_z, dg_gdn_norm = _gdn_out_bwd(dmixed, o_raw, proj, gdn_norm_w, heads, 3 * gw)
    dq, dk, dv, dgc_b, dbeta_b = _gdn_bwd(qkv_act, gcb, betab, states, d_o, bsz, heads)
    d_ab, d_alog, d_dt = _gates_bwd(proj_ab, a_log_pad, dt_pad, dgc_b, dbeta_b, heads)
    dact3 = jnp.concatenate([dq, dk, dv], axis=1).reshape(bsz, seq, 3 * gw)
    d_qkv, dw_conv_q = _qkv_conv_bwd(proj3, dact3, conv_q, 3 * gw)
    d_proj = jnp.concatenate([d_qkv.reshape(t, 3 * gw), d_z, d_b.reshape(t, sw), d_c.reshape(t, sw),
                              d_hsc.reshape(t, sw)], axis=1)
    d_xn = _matmul(d_proj, w_main, "nt", [F32], "d_xn_main")
    d_xn_ab = _matmul(d_ab, w_ab, "nt", [F32], "d_xn_ab")
    dw_main = _matmul(xn, d_proj, "tn", [BF16], "dw_in_main")
    dw_ab = _matmul(xn, d_ab, "tn", [BF16], "dw_in_ab")
    dw_in = _unpack_dw_in(dw_main, dw_ab, gw, heads, sw, ics)
    grad_x, dg_mix_pre = _first_bwd(d_xn, d_xn_ab, x2, norm_mix_pre, dh)

    keeps, gots = _halves_to_sibling([dw_in, dw_out, dw_up, dw_down], layouts)
    parts = [_add_bf16(k, g, f"chip_sum_{i}") for i, (k, g) in enumerate(zip(keeps, gots))]
    owns, recvs = _partials_to_owner(parts)
    halves = [_sum4(o, r, f"shard_sum_{i}") for i, (o, r) in enumerate(zip(owns, recvs))]
    g_in, g_out, g_up, g_down = _join_halves(halves)

    small, offs = _pack([loss_blk[0:1, 0:1], dg_mix_pre, dw_conv_q, d_alog[:, :heads], d_dt[:, :heads], dg_gdn_norm,
                         dw_conv_s, dg_mix_post, dg_mlp_pre, dg_mlp_post])
    (loss, g_mix_pre, g_conv_q_full, g_alog, g_dt, g_gdn_norm, g_conv_s_full, g_mix_post, g_mlp_pre,
     g_mlp_post) = _unpack(_small_all_reduce(small), offs)
    j = 2 * lax.axis_index("x") + lax.axis_index("y")
    cq_w, cs_w = conv_qkv_w.shape[-1], conv_sc_w.shape[-1]
    g_conv_q = lax.dynamic_slice_in_dim(g_conv_q_full, j * cq_w, cq_w, axis=1)
    g_conv_s = lax.dynamic_slice_in_dim(g_conv_s_full, j * cs_w, cs_w, axis=1)

    grads = [g_mix_pre, g_in, g_conv_q, g_alog, g_dt, g_gdn_norm, g_conv_s, g_out, g_mix_post, g_mlp_pre, g_up,
             g_down, g_mlp_post]
    weights = [norm_mix_pre, w_in, conv_qkv_w, a_log, dt_bias, gdn_norm_w, conv_sc_w, w_out, norm_mix_post,
               norm_mlp_pre, w_up, w_down, norm_mlp_post]
    ms = [m_norm_mix_pre, m_w_in, m_conv_qkv_w, m_a_log, m_dt_bias, m_gdn_norm_w, m_conv_sc_w, m_w_out,
          m_norm_mix_post, m_norm_mlp_pre, m_w_up, m_w_down, m_norm_mlp_post]
    vs = [v_norm_mix_pre, v_w_in, v_conv_qkv_w, v_a_log, v_dt_bias, v_gdn_norm_w, v_conv_sc_w, v_w_out,
          v_norm_mix_post, v_norm_mlp_pre, v_w_up, v_w_down, v_norm_mlp_post]
    out_g, out_d, out_m, out_v = [], [], [], []
    for i, (wt, g, m, v) in enumerate(zip(weights, grads, ms, vs)):
        shape2 = wt.shape[-2:] if wt.ndim == 3 else wt.shape
        g2 = g.reshape(shape2)
        dl, nm, nv = _adamw(wt.reshape(shape2), g2, m.reshape(shape2), v.reshape(shape2), f"adamw_{i}")
        out_g.append(g2.reshape(wt.shape))
        out_d.append(dl.reshape(wt.shape))
        out_m.append(nm.reshape(wt.shape))
        out_v.append(nv.reshape(wt.shape))

    return (loss.reshape(()), grad_x.reshape(bsz, seq, d), *out_g, *out_d, *out_m, *out_v)
```

```python
import functools

import jax
import jax.numpy as jnp
from jax import lax
from jax.experimental import pallas as pl
from jax.experimental.pallas import tpu as pltpu

CHUNK = 64
NORM_EPS = 1e-6
L2_EPS = 1e-6
N_CHIPS = 4
ADAM_LR = 0.001
ADAM_B1 = 0.9
ADAM_B2 = 0.999
ADAM_EPS = 1e-08
ADAM_WD = 0.01
ADAM_STEP = 10
LANES = 128
VMEM_LIMIT = 56 * 1024 * 1024

F32 = jnp.float32
BF16 = jnp.bfloat16
HI = lax.Precision.HIGHEST
MESH = pl.DeviceIdType.MESH
ANY = pl.BlockSpec(memory_space=pl.ANY)


def _params(n_grid=0):
    return pltpu.CompilerParams(vmem_limit_bytes=VMEM_LIMIT)


def _tile(n, pref, align):
    if n <= pref:
        return n
    t = (pref // align) * align
    while t >= align:
        if n % t == 0:
            return t
        t -= align
    raise ValueError(f"no tile for {n}")


def _sigmoid(x):
    return 1.0 / (1.0 + jnp.exp(-x))


def _softplus(x):
    return jnp.maximum(x, 0.0) + jnp.log(1.0 + jnp.exp(-jnp.abs(x)))


def _rms_fwd(x, g):
    r = lax.rsqrt(jnp.mean(x * x, axis=-1, keepdims=True) + NORM_EPS)
    return x * r * g


def _rms_bwd(dy, x, g):
    r = lax.rsqrt(jnp.mean(x * x, axis=-1, keepdims=True) + NORM_EPS)
    xh = x * r
    dxh = dy * g
    dx = r * (dxh - xh * jnp.mean(dxh * xh, axis=-1, keepdims=True))
    dg = jnp.sum(dy * xh, axis=0, keepdims=True)
    return dx, dg


def _matmul(a, b, form, out_dtypes, name, epilogue=None, extras=(), tm=1024, tn=1024, tk=2048):
    if form == "nn":
        (m, kd), (_, n) = a.shape, b.shape
        dims = (((1,), (0,)), ((), ()))
    elif form == "nt":
        (m, kd), (n, _) = a.shape, b.shape
        dims = (((1,), (1,)), ((), ()))
    else:
        (kd, m), (_, n) = a.shape, b.shape
        dims = (((0,), (0,)), ((), ()))
    tm, tn, tk = _tile(m, tm, LANES), _tile(n, tn, LANES), _tile(kd, tk, LANES)
    nk = kd // tk
    n_extra = len(extras)
    n_out = len(out_dtypes)

    if form == "nn":
        a_spec = pl.BlockSpec((tm, tk), lambda i, j, k: (i, k))
        b_spec = pl.BlockSpec((tk, tn), lambda i, j, k: (k, j))
    elif form == "nt":
        a_spec = pl.BlockSpec((tm, tk), lambda i, j, k: (i, k))
        b_spec = pl.BlockSpec((tn, tk), lambda i, j, k: (j, k))
    else:
        a_spec = pl.BlockSpec((tk, tm), lambda i, j, k: (k, i))
        b_spec = pl.BlockSpec((tk, tn), lambda i, j, k: (k, j))
    tile_spec = pl.BlockSpec((tm, tn), lambda i, j, k: (i, j))

    def body(a_ref, b_ref, *rest):
        extra_refs = rest[:n_extra]
        out_refs = rest[n_extra:n_extra + n_out]
        acc_ref = rest[-1]
        k = pl.program_id(2)

        @pl.when(k == 0)
        def _():
            acc_ref[...] = jnp.zeros_like(acc_ref)

        acc_ref[...] += lax.dot_general(a_ref[...], b_ref[...], dims, preferred_element_type=F32)

        @pl.when(k == nk - 1)
        def _():
            acc = acc_ref[...]
            outs = (acc,) if epilogue is None else epilogue(acc, *[e[...] for e in extra_refs])
            for o_ref, val in zip(out_refs, outs):
                o_ref[...] = val.astype(o_ref.dtype)

    outs = pl.pallas_call(
        body, name=name, grid=(m // tm, n // tn, nk),
        in_specs=[a_spec, b_spec] + [tile_spec] * n_extra,
        out_specs=[tile_spec] * n_out,
        out_shape=[jax.ShapeDtypeStruct((m, n), dt) for dt in out_dtypes],
        scratch_shapes=[pltpu.VMEM((tm, tn), F32)],
        compiler_params=_params(),
    )(a, b, *extras)
    return outs[0] if n_out == 1 else outs


def _cast_into_layout(w, layout, chip, name):
    r, c = w.shape
    tr = _tile(r, 256, 16)

    def body(chip_ref, w_ref, o_ref):
        o_ref[...] = w_ref[...].astype(BF16)

    return pl.pallas_call(
        body, name=name,
        grid_spec=pltpu.PrefetchScalarGridSpec(
            num_scalar_prefetch=1, grid=(r // tr,),
            in_specs=[pl.BlockSpec((tr, c), lambda i, chip_ref: (i, 0))],
            out_specs=layout.block_spec(tr, lambda i, chip_ref: (chip_ref[0], i))),
        out_shape=jax.ShapeDtypeStruct(layout.full_shape(), BF16), compiler_params=_params(),
    )(chip, w)


def _in_segments(gw, heads, sw):
    main = 4 * gw
    return [(0, main, 0), (main + 2 * heads, 3 * sw, main), (main, 2 * heads, main + 3 * sw)]


def _pieces(seg_start, width, dst_start, ics):
    out = []
    g = seg_start
    while g < seg_start + width:
        j, cj = divmod(g, ics)
        wdt = min(ics - cj, seg_start + width - g)
        out.append((j, cj, dst_start + (g - seg_start), wdt))
        g += wdt
    return out


def _repack_w_in(w_sh, gw, heads, sw):
    ns, d, ics = w_sh.shape
    main = 4 * gw + 3 * sw
    tr = _tile(d, 128, 16)
    pieces = [p for seg in _in_segments(gw, heads, sw) for p in _pieces(*seg, ics)]

    def body(w_ref, m_ref, ab_ref):
        ab_ref[...] = jnp.zeros_like(ab_ref)
        for j, cj, cd, wdt in pieces:
            if cd >= main:
                ab_ref[:, cd - main:cd - main + wdt] = w_ref[j, :, cj:cj + wdt]
            else:
                m_ref[:, cd:cd + wdt] = w_ref[j, :, cj:cj + wdt]

    return pl.pallas_call(
        body, name="repack_w_in", grid=(d // tr,),
        in_specs=[pl.BlockSpec((ns, tr, ics), lambda i: (0, i, 0))],
        out_specs=[pl.BlockSpec((tr, main), lambda i: (i, 0)), pl.BlockSpec((tr, LANES), lambda i: (i, 0))],
        out_shape=[jax.ShapeDtypeStruct((d, main), BF16), jax.ShapeDtypeStruct((d, LANES), BF16)],
        compiler_params=_params(),
    )(w_sh)


def _unpack_dw_in(dw_main, dw_ab, gw, heads, sw, ics):
    d = dw_main.shape[0]
    tr = _tile(d, 128, 16)
    main = 4 * gw + 3 * sw
    pieces = [p for seg in _in_segments(gw, heads, sw) for p in _pieces(*seg, ics)]

    def body(m_ref, ab_ref, o_ref):
        for j, cj, cd, wdt in pieces:
            if cd >= main:
                o_ref[j, :, cj:cj + wdt] = ab_ref[:, cd - main:cd - main + wdt].astype(BF16)
            else:
                o_ref[j, :, cj:cj + wdt] = m_ref[:, cd:cd + wdt].astype(BF16)

    return pl.pallas_call(
        body, name="unpack_dw_in", grid=(d // tr,),
        in_specs=[pl.BlockSpec((tr, main), lambda i: (i, 0)), pl.BlockSpec((tr, LANES), lambda i: (i, 0))],
        out_specs=pl.BlockSpec((N_CHIPS, tr, ics), lambda i: (0, i, 0)),
        out_shape=jax.ShapeDtypeStruct((N_CHIPS, d, ics), BF16), compiler_params=_params(),
    )(dw_main, dw_ab)


def _adamw(w, g, m, v, name):
    r, c = w.shape
    tr = _tile(r, 128, 8)
    c1 = 1.0 - ADAM_B1 ** ADAM_STEP
    c2 = 1.0 - ADAM_B2 ** ADAM_STEP

    def body(w_ref, g_ref, m_ref, v_ref, d_ref, nm_ref, nv_ref):
        gg = g_ref[...]
        nm = ADAM_B1 * m_ref[...] + (1.0 - ADAM_B1) * gg
        nv = ADAM_B2 * v_ref[...] + (1.0 - ADAM_B2) * jnp.square(gg)
        m_hat = nm / c1
        v_hat = nv / c2
        d_ref[...] = -ADAM_LR * (m_hat / (jnp.sqrt(v_hat) + ADAM_EPS) + ADAM_WD * w_ref[...])
        nm_ref[...] = nm
        nv_ref[...] = nv

    spec = pl.BlockSpec((tr, c), lambda i: (i, 0))
    return pl.pallas_call(
        body, name=name, grid=(r // tr,), in_specs=[spec] * 4, out_specs=[spec] * 3,
        out_shape=[jax.ShapeDtypeStruct((r, c), F32)] * 3, compiler_params=_params(),
    )(w, g, m, v)


def _row_spec(tt, d):
    return pl.BlockSpec((tt, d), lambda i: (i, 0))


def _vec_spec(d):
    return pl.BlockSpec((1, d), lambda i: (0, 0))


def _norm_fwd(x, g):
    t, d = x.shape
    tt = _tile(t, 256, 16)

    def body(x_ref, g_ref, o_ref):
        o_ref[...] = _rms_fwd(x_ref[...], g_ref[...]).astype(BF16)

    return pl.pallas_call(
        body, name="norm_mix_pre", grid=(t // tt,), in_specs=[_row_spec(tt, d), _vec_spec(d)],
        out_specs=_row_spec(tt, d), out_shape=jax.ShapeDtypeStruct((t, d), BF16), compiler_params=_params(),
    )(x, g)


def _mid_fwd(x, mix, g_post, g_pre):
    t, d = x.shape
    tt = _tile(t, 128, 16)

    def body(x_ref, mix_ref, gp_ref, gn_ref, h_ref, hn_ref):
        h = x_ref[...] + _rms_fwd(mix_ref[...], gp_ref[...])
        h_ref[...] = h
        hn_ref[...] = _rms_fwd(h, gn_ref[...]).astype(BF16)

    return pl.pallas_call(
        body, name="mid_fwd", grid=(t // tt,),
        in_specs=[_row_spec(tt, d), _row_spec(tt, d), _vec_spec(d), _vec_spec(d)],
        out_specs=[_row_spec(tt, d), _row_spec(tt, d)],
        out_shape=[jax.ShapeDtypeStruct((t, d), F32), jax.ShapeDtypeStruct((t, d), BF16)],
        compiler_params=_params(),
    )(x, mix, g_post, g_pre)


def _head_fwd_bwd(h, ff, tgt, g_post):
    t, d = h.shape
    tt = _tile(t, 128, 16)

    def body(h_ref, ff_ref, t_ref, g_ref, loss_ref, dy_ref, dff_ref, dg_ref):
        i = pl.program_id(0)

        @pl.when(i == 0)
        def _():
            loss_ref[...] = jnp.zeros_like(loss_ref)
            dg_ref[...] = jnp.zeros_like(dg_ref)

        ff = ff_ref[...]
        g = g_ref[...]
        e = h_ref[...] + _rms_fwd(ff, g) - t_ref[...]
        loss_ref[...] += 0.5 * jnp.sum(jnp.mean(e * e, axis=-1, keepdims=True))
        dy = e * (1.0 / d)
        dy_ref[...] = dy
        dff, dg = _rms_bwd(dy, ff, g)
        dff_ref[...] = dff.astype(BF16)
        dg_ref[...] += dg

    return pl.pallas_call(
        body, name="loss_head", grid=(t // tt,),
        in_specs=[_row_spec(tt, d)] * 3 + [_vec_spec(d)],
        out_specs=[pl.BlockSpec((8, LANES), lambda i: (0, 0)), _row_spec(tt, d), _row_spec(tt, d), _vec_spec(d)],
        out_shape=[jax.ShapeDtypeStruct((8, LANES), F32), jax.ShapeDtypeStruct((t, d), F32),
                   jax.ShapeDtypeStruct((t, d), BF16), jax.ShapeDtypeStruct((1, d), F32)],
        compiler_params=_params(),
    )(h, ff, tgt, g_post)


def _mid_bwd(d_hn, h, g_pre, dy, mix, g_post):
    t, d = h.shape
    tt = _tile(t, 128, 16)

    def body(dhn_ref, h_ref, gn_ref, dy_ref, mix_ref, gp_ref, dh_ref, dmix_ref, dgn_ref, dgp_ref):
        i = pl.program_id(0)

        @pl.when(i == 0)
        def _():
            dgn_ref[...] = jnp.zeros_like(dgn_ref)
            dgp_ref[...] = jnp.zeros_like(dgp_ref)

        dx, dgn = _rms_bwd(dhn_ref[...], h_ref[...], gn_ref[...])
        dh = dy_ref[...] + dx
        dh_ref[...] = dh
        dmix, dgp = _rms_bwd(dh, mix_ref[...], gp_ref[...])
        dmix_ref[...] = dmix.astype(BF16)
        dgn_ref[...] += dgn
        dgp_ref[...] += dgp

    return pl.pallas_call(
        body, name="mid_bwd", grid=(t // tt,),
        in_specs=[_row_spec(tt, d), _row_spec(tt, d), _vec_spec(d), _row_spec(tt, d), _row_spec(tt, d), _vec_spec(d)],
        out_specs=[_row_spec(tt, d), _row_spec(tt, d), _vec_spec(d), _vec_spec(d)],
        out_shape=[jax.ShapeDtypeStruct((t, d), F32), jax.ShapeDtypeStruct((t, d), BF16),
                   jax.ShapeDtypeStruct((1, d), F32), jax.ShapeDtypeStruct((1, d), F32)],
        compiler_params=_params(),
    )(d_hn, h, g_pre, dy, mix, g_post)


def _first_bwd(d_xn, d_xn_ab, x, g, dh):
    t, d = x.shape
    tt = _tile(t, 128, 16)

    def body(a_ref, b_ref, x_ref, g_ref, dh_ref, dx_ref, dg_ref):
        i = pl.program_id(0)

        @pl.when(i == 0)
        def _():
            dg_ref[...] = jnp.zeros_like(dg_ref)

        dx, dg = _rms_bwd(a_ref[...] + b_ref[...], x_ref[...], g_ref[...])
        dx_ref[...] = dh_ref[...] + dx
        dg_ref[...] += dg

    return pl.pallas_call(
        body, name="first_bwd", grid=(t // tt,),
        in_specs=[_row_spec(tt, d), _row_spec(tt, d), _row_spec(tt, d), _vec_spec(d), _row_spec(tt, d)],
        out_specs=[_row_spec(tt, d), _vec_spec(d)],
        out_shape=[jax.ShapeDtypeStruct((t, d), F32), jax.ShapeDtypeStruct((1, d), F32)],
        compiler_params=_params(),
    )(d_xn, d_xn_ab, x, g, dh)


HALO = 8


def _cur(ts, tc, off):
    return pl.BlockSpec((1, ts, tc), lambda ci, b, s: (b, s, off + ci))


def _prev(ts, tc, off):
    return pl.BlockSpec((1, HALO, tc), lambda ci, b, s: (b, jnp.maximum(s * (ts // HALO) - 1, 0), off + ci))


def _next(ts, tc, off, seq):
    last = seq // HALO - 1
    return pl.BlockSpec((1, HALO, tc), lambda ci, b, s: (b, jnp.minimum((s + 1) * (ts // HALO), last), off + ci))


def _conv_w_spec(kw, tc):
    return pl.BlockSpec((kw, tc), lambda ci, b, s: (0, ci))


def _conv_taps(w, buf, kw, ts):
    acc = w[0:1, :] * buf[HALO - (kw - 1):HALO - (kw - 1) + ts, :]
    for j in range(1, kw):
        acc = acc + w[j:j + 1, :] * buf[HALO - (kw - 1) + j:HALO - (kw - 1) + j + ts, :]
    return acc


def _silu_grad(x):
    s = _sigmoid(x)
    return s * (1.0 + x * (1.0 - s))


def _qkv_conv_fwd(proj3, w, width):
    bsz, seq, _ = proj3.shape
    kw = w.shape[0]
    ts, tc = _tile(seq, 256, 8), _tile(width, 512, LANES)

    def body(u_ref, up_ref, w_ref, o_ref, buf):
        s = pl.program_id(2)
        buf[0:HALO, :] = jnp.where(s == 0, 0.0, up_ref[0])
        buf[HALO:HALO + ts, :] = u_ref[0]
        pre = _conv_taps(w_ref[...], buf, kw, ts)
        o_ref[0] = pre * _sigmoid(pre)

    return pl.pallas_call(
        body, name="qkv_conv_fwd", grid=(width // tc, bsz, seq // ts),
        in_specs=[_cur(ts, tc, 0), _prev(ts, tc, 0), _conv_w_spec(kw, tc)],
        out_specs=_cur(ts, tc, 0), out_shape=jax.ShapeDtypeStruct((bsz, seq, width), F32),
        scratch_shapes=[pltpu.VMEM((HALO + ts, tc), F32)], compiler_params=_params(),
    )(proj3, proj3, w)


def _qkv_conv_bwd(proj3, dact3, w, width):
    bsz, seq, _ = proj3.shape
    kw = w.shape[0]
    ts, tc = _tile(seq, 256, 8), _tile(width, 512, LANES)
    n_s = seq // ts

    def body(u_ref, up_ref, un_ref, d_ref, dn_ref, w_ref, du_ref, dw_ref, buf, gbuf):
        b, s = pl.program_id(1), pl.program_id(2)

        @pl.when((b == 0) & (s == 0))
        def _():
            dw_ref[...] = jnp.zeros_like(dw_ref)

        w = w_ref[...]
        buf[0:HALO, :] = jnp.where(s == 0, 0.0, up_ref[0])
        buf[HALO:HALO + ts, :] = u_ref[0]
        buf[HALO + ts:, :] = un_ref[0]
        acc = w[0:1, :] * buf[HALO - (kw - 1):HALO - (kw - 1) + ts + HALO, :]
        for j in range(1, kw):
            acc = acc + w[j:j + 1, :] * buf[HALO - (kw - 1) + j:HALO - (kw - 1) + j + ts + HALO, :]
        gbuf[0:ts, :] = d_ref[0]
        gbuf[ts:, :] = jnp.where(s == n_s - 1, 0.0, dn_ref[0])
        gbuf[...] = gbuf[...] * _silu_grad(acc)
        g_cur = gbuf[0:ts, :]
        for j in range(kw):
            dw_ref[j:j + 1, :] += jnp.sum(g_cur * buf[HALO - (kw - 1) + j:HALO - (kw - 1) + j + ts, :], axis=0, keepdims=True)
        du = w[0:1, :] * gbuf[kw - 1:kw - 1 + ts, :]
        for j in range(1, kw):
            du = du + w[j:j + 1, :] * gbuf[kw - 1 - j:kw - 1 - j + ts, :]
        du_ref[0] = du.astype(BF16)

    return pl.pallas_call(
        body, name="qkv_conv_bwd", grid=(width // tc, bsz, n_s),
        in_specs=[_cur(ts, tc, 0), _prev(ts, tc, 0), _next(ts, tc, 0, seq), _cur(ts, tc, 0), _next(ts, tc, 0, seq),
                  _conv_w_spec(kw, tc)],
        out_specs=[_cur(ts, tc, 0), _conv_w_spec(kw, tc)],
        out_shape=[jax.ShapeDtypeStruct((bsz, seq, width), BF16), jax.ShapeDtypeStruct((kw, width), F32)],
        scratch_shapes=[pltpu.VMEM((HALO + ts + HALO, tc), F32), pltpu.VMEM((ts + HALO, tc), F32)],
        compiler_params=_params(),
    )(proj3, proj3, proj3, dact3, dact3, w)


def _sc_fwd(proj3, w, off, sw):
    bsz, seq, _ = proj3.shape
    kw = w.shape[0]
    ts, tc = _tile(seq, 256, 8), _tile(sw, 512, LANES)
    ob, oc, oh = off // tc, (off + sw) // tc, (off + 2 * sw) // tc

    def body(b_ref, c_ref, cp_ref, h_ref, hp_ref, w_ref, o_ref, buf):
        s = pl.program_id(2)
        buf[0:HALO, :] = jnp.where(s == 0, 0.0, cp_ref[0] * hp_ref[0])
        buf[HALO:HALO + ts, :] = c_ref[0] * h_ref[0]
        o_ref[0] = (b_ref[0] * _conv_taps(w_ref[...], buf, kw, ts)).astype(BF16)

    return pl.pallas_call(
        body, name="sc_fwd", grid=(sw // tc, bsz, seq // ts),
        in_specs=[_cur(ts, tc, ob), _cur(ts, tc, oc), _prev(ts, tc, oc), _cur(ts, tc, oh), _prev(ts, tc, oh),
                  _conv_w_spec(kw, tc)],
        out_specs=_cur(ts, tc, 0), out_shape=jax.ShapeDtypeStruct((bsz, seq, sw), BF16),
        scratch_shapes=[pltpu.VMEM((HALO + ts, tc), F32)], compiler_params=_params(),
    )(proj3, proj3, proj3, proj3, proj3, w)


def _sc_bwd(proj3, dmixed3, w, off, sw, d_off):
    bsz, seq, _ = proj3.shape
    kw = w.shape[0]
    ts, tc = _tile(seq, 256, 8), _tile(sw, 512, LANES)
    n_s = seq // ts
    ob, oc, oh, od = off // tc, (off + sw) // tc, (off + 2 * sw) // tc, d_off // tc

    def body(d_ref, dn_ref, b_ref, bn_ref, c_ref, cp_ref, h_ref, hp_ref, w_ref,
             db_ref, dc_ref, dh_ref, dw_ref, buf, gbuf):
        b, s = pl.program_id(1), pl.program_id(2)

        @pl.when((b == 0) & (s == 0))
        def _():
            dw_ref[...] = jnp.zeros_like(dw_ref)

        w = w_ref[...]
        cc, hh = c_ref[0], h_ref[0]
        buf[0:HALO, :] = jnp.where(s == 0, 0.0, cp_ref[0] * hp_ref[0])
        buf[HALO:HALO + ts, :] = cc * hh
        dout = d_ref[0]
        db_ref[0] = (dout * _conv_taps(w, buf, kw, ts)).astype(BF16)
        g_cur = dout * b_ref[0]
        gbuf[0:ts, :] = g_cur
        gbuf[ts:, :] = jnp.where(s == n_s - 1, 0.0, dn_ref[0] * bn_ref[0])
        for j in range(kw):
            dw_ref[j:j + 1, :] += jnp.sum(g_cur * buf[HALO - (kw - 1) + j:HALO - (kw - 1) + j + ts, :], axis=0, keepdims=True)
        dp = w[0:1, :] * gbuf[kw - 1:kw - 1 + ts, :]
        for j in range(1, kw):
            dp = dp + w[j:j + 1, :] * gbuf[kw - 1 - j:kw - 1 - j + ts, :]
        dc_ref[0] = (dp * hh).astype(BF16)
        dh_ref[0] = (dp * cc).astype(BF16)

    out = jax.ShapeDtypeStruct((bsz, seq, sw), BF16)
    return pl.pallas_call(
        body, name="sc_bwd", grid=(sw // tc, bsz, n_s),
        in_specs=[_cur(ts, tc, od), _next(ts, tc, od, seq), _cur(ts, tc, ob), _next(ts, tc, ob, seq),
                  _cur(ts, tc, oc), _prev(ts, tc, oc), _cur(ts, tc, oh), _prev(ts, tc, oh), _conv_w_spec(kw, tc)],
        out_specs=[_cur(ts, tc, 0)] * 3 + [_conv_w_spec(kw, tc)],
        out_shape=[out, out, out, jax.ShapeDtypeStruct((kw, sw), F32)],
        scratch_shapes=[pltpu.VMEM((HALO + ts, tc), F32), pltpu.VMEM((ts + HALO, tc), F32)],
        compiler_params=_params(),
    )(dmixed3, dmixed3, proj3, proj3, proj3, proj3, proj3, proj3, w)


def _tri_ones(lower):
    i = lax.broadcasted_iota(jnp.int32, (CHUNK, CHUNK), 0)
    j = lax.broadcasted_iota(jnp.int32, (CHUNK, CHUNK), 1)
    return jnp.where((i >= j) if lower else (j >= i), 1.0, 0.0).astype(F32)


def _gates_fwd(proj_ab, a_log_pad, dt_pad, heads):
    t = proj_ab.shape[0]
    gw = heads * LANES

    def body(ab_ref, al_ref, dt_ref, gc_ref, beta_ref):
        ab = ab_ref[...]
        g = -jnp.exp(al_ref[...]) * _softplus(ab + dt_ref[...])
        gc = jnp.dot(_tri_ones(True), g, precision=HI, preferred_element_type=F32)
        beta = _sigmoid(ab)
        for h in range(heads):
            gc_ref[:, h * LANES:(h + 1) * LANES] = jnp.broadcast_to(gc[:, h:h + 1], (CHUNK, LANES))
            beta_ref[:, h * LANES:(h + 1) * LANES] = jnp.broadcast_to(beta[:, heads + h:heads + h + 1], (CHUNK, LANES))

    return pl.pallas_call(
        body, name="gates_fwd", grid=(t // CHUNK,),
        in_specs=[_row_spec(CHUNK, LANES), _vec_spec(LANES), _vec_spec(LANES)],
        out_specs=[_row_spec(CHUNK, gw), _row_spec(CHUNK, gw)],
        out_shape=[jax.ShapeDtypeStruct((t, gw), F32)] * 2, compiler_params=_params(),
    )(proj_ab, a_log_pad, dt_pad)


def _gates_bwd(proj_ab, a_log_pad, dt_pad, dgc_b, dbeta_b, heads):
    t = proj_ab.shape[0]
    gw = heads * LANES

    def body(ab_ref, al_ref, dt_ref, dgc_ref, dbeta_ref, dab_ref, dal_ref, ddt_ref):
        i = pl.program_id(0)

        @pl.when(i == 0)
        def _():
            dal_ref[...] = jnp.zeros_like(dal_ref)
            ddt_ref[...] = jnp.zeros_like(ddt_ref)

        lane = lax.broadcasted_iota(jnp.int32, (CHUNK, LANES), 1)
        dgc = jnp.zeros((CHUNK, LANES), F32)
        dbeta = jnp.zeros((CHUNK, LANES), F32)
        for h in range(heads):
            dgc = jnp.where(lane == h, dgc_ref[:, h * LANES:(h + 1) * LANES], dgc)
            dbeta = jnp.where(lane == heads + h, dbeta_ref[:, h * LANES:(h + 1) * LANES], dbeta)
        dg = jnp.dot(_tri_ones(False), dgc, precision=HI, preferred_element_type=F32)
        ab = ab_ref[...]
        z = ab + dt_ref[...]
        ea = jnp.exp(al_ref[...])
        da = dg * (-ea) * _sigmoid(z)
        beta = _sigmoid(ab)
        db = dbeta * beta * (1.0 - beta)
        dab_ref[...] = jnp.where(lane < heads, da, jnp.where(lane < 2 * heads, db, 0.0)).astype(BF16)
        da_m = jnp.where(lane < heads, da, 0.0)
        ddt_ref[...] += jnp.sum(da_m, axis=0, keepdims=True)
        dal_ref[...] += jnp.sum(jnp.where(lane < heads, dg * (-ea) * _softplus(z), 0.0), axis=0, keepdims=True)

    return pl.pallas_call(
        body, name="gates_bwd", grid=(t // CHUNK,),
        in_specs=[_row_spec(CHUNK, LANES), _vec_spec(LANES), _vec_spec(LANES), _row_spec(CHUNK, gw), _row_spec(CHUNK, gw)],
        out_specs=[_row_spec(CHUNK, LANES), _vec_spec(LANES), _vec_spec(LANES)],
        out_shape=[jax.ShapeDtypeStruct((t, LANES), BF16), jax.ShapeDtypeStruct((1, LANES), F32),
                   jax.ShapeDtypeStruct((1, LANES), F32)],
        compiler_params=_params(),
    )(proj_ab, a_log_pad, dt_pad, dgc_b, dbeta_b)


def _dot(a, b, dims, hi=False):
    if hi:
        return lax.dot_general(a, b, (dims, ((), ())), precision=HI, preferred_element_type=F32)
    return lax.dot_general(a.astype(BF16), b.astype(BF16), (dims, ((), ())), preferred_element_type=F32)


NN = ((1,), (0,))
NT = ((1,), (1,))
TN = ((0,), (0,))


def _unit_lower_inverse(m):
    i = lax.broadcasted_iota(jnp.int32, (CHUNK, CHUNK), 0)
    j = lax.broadcasted_iota(jnp.int32, (CHUNK, CHUNK), 1)
    t = jnp.where(i == j, 1.0, 0.0).astype(F32)
    shift = 0
    while (1 << shift) < CHUNK:
        same_pair = jnp.right_shift(i, shift + 1) == jnp.right_shift(j, shift + 1)
        other_half = jnp.right_shift(i, shift) != jnp.right_shift(j, shift)
        off = jnp.where(same_pair & other_half, m, 0.0)
        t = t - _dot(_dot(t, off, NN, hi=True), t, NN, hi=True)
        shift += 1
    return t


def _chunk_local(qr, kr, v, gcb, betab, head_dim):
    i = lax.broadcasted_iota(jnp.int32, (CHUNK, CHUNK), 0)
    j = lax.broadcasted_iota(jnp.int32, (CHUNK, CHUNK), 1)
    rq = lax.rsqrt(jnp.sum(qr * qr, axis=-1, keepdims=True) + L2_EPS)
    rk = lax.rsqrt(jnp.sum(kr * kr, axis=-1, keepdims=True) + L2_EPS)
    scale = head_dim ** -0.5
    qh, k = qr * rq, kr * rk
    q = qh * scale
    g_col = gcb[:, 0:CHUNK]
    g_row = gcb.T[0:CHUNK, :]
    decay = jnp.exp(jnp.where(i >= j, g_col - g_row, -jnp.inf))
    beta_col = betab[:, 0:CHUNK]
    kk = _dot(k, k, NT)
    m = jnp.where(i > j, beta_col * kk * decay, 0.0)
    tinv = _unit_lower_inverse(m)
    eg = jnp.exp(gcb)
    rhs_w = k * betab * eg
    u = _dot(tinv, v * betab, NN, hi=True)
    w = _dot(tinv, rhs_w, NN, hi=True)
    qk = _dot(q, k, NT)
    p = qk * decay
    g_last = gcb[CHUNK - 1:CHUNK, :]
    e_last = jnp.exp(g_last - gcb)
    return dict(rq=rq, rk=rk, qh=qh, q=q, k=k, decay=decay, beta_col=beta_col, kk=kk, m=m, tinv=tinv, eg=eg,
                rhs_w=rhs_w, u=u, w=w, qk=qk, p=p, qd=q * eg, kd=k * e_last, e_last=e_last,
                gl=jnp.exp(g_last), scale=scale, strict=i > j, incl=i >= j)


GDN_HEAD_GROUP = 4


def _gdn_specs(n_chunks, heads, reverse):
    hg = min(GDN_HEAD_GROUP, heads)
    assert heads % hg == 0

    def cidx(c):
        return (n_chunks - 1 - c) if reverse else c

    def tok(off):
        return pl.BlockSpec((CHUNK, hg * LANES), lambda b, h, c: (b * n_chunks + cidx(c), off // hg + h))

    state = pl.BlockSpec((None, hg, LANES, LANES), lambda b, h, c: (b * n_chunks + cidx(c), h, 0, 0))
    return hg, tok, state


def _gdn_fwd(qkv_act, gcb, betab, bsz, heads):
    t = qkv_act.shape[0]
    n_chunks = t // bsz // CHUNK
    hg, tok, state = _gdn_specs(n_chunks, heads, False)

    def body(q_ref, k_ref, v_ref, gc_ref, beta_ref, o_ref, s_ref, st):
        @pl.when(pl.program_id(2) == 0)
        def _():
            st[...] = jnp.zeros_like(st)

        for hh in range(hg):
            sl = slice(hh * LANES, (hh + 1) * LANES)
            loc = _chunk_local(q_ref[:, sl], k_ref[:, sl], v_ref[:, sl], gc_ref[:, sl], beta_ref[:, sl], LANES)
            s0 = st[hh]
            s_ref[hh] = s0
            v_new = loc["u"] - _dot(loc["w"], s0, NN)
            o_ref[:, sl] = _dot(loc["qd"], s0, NN) + _dot(loc["p"], v_new, NN)
            st[hh] = s0 * loc["gl"] + _dot(loc["kd"], v_new, TN)

    return pl.pallas_call(
        body, name="gdn_fwd", grid=(bsz, heads // hg, n_chunks),
        in_specs=[tok(0), tok(heads), tok(2 * heads), tok(0), tok(0)],
        out_specs=[tok(0), state],
        out_shape=[jax.ShapeDtypeStruct((t, heads * LANES), F32),
                   jax.ShapeDtypeStruct((bsz * n_chunks, heads, LANES, LANES), F32)],
        scratch_shapes=[pltpu.VMEM((hg, LANES, LANES), F32)], compiler_params=_params(),
    )(qkv_act, qkv_act, qkv_act, gcb, betab)


def _gdn_bwd(qkv_act, gcb, betab, states, d_o, bsz, heads):
    t = qkv_act.shape[0]
    n_chunks = t // bsz // CHUNK
    hg, tok, state = _gdn_specs(n_chunks, heads, True)

    def rowsum(a):
        return jnp.sum(a, axis=-1, keepdims=True)

    def one_head(sl, hh, q_ref, k_ref, v_ref, gc_ref, beta_ref, s_ref, do_ref, dq_ref, dk_ref, dv_ref, dgc_ref,
                 dbeta_ref, dst):
        qr, kr, v = q_ref[:, sl], k_ref[:, sl], v_ref[:, sl]
        betab = beta_ref[:, sl]
        L = _chunk_local(qr, kr, v, gc_ref[:, sl], betab, LANES)
        q, k, u, w, p, tinv, decay = L["q"], L["k"], L["u"], L["w"], L["p"], L["tinv"], L["decay"]
        s0 = s_ref[hh]
        d_out = do_ref[:, sl]
        ds1 = dst[hh]
        v_new = u - _dot(w, s0, NN)

        d_vnew = _dot(p, d_out, TN) + _dot(L["kd"], ds1, NN)
        d_qd = _dot(d_out, s0, NT)
        d_kd = _dot(v_new, ds1, NT)
        d_gl = jnp.sum(ds1 * s0)
        d_p = jnp.where(L["incl"], _dot(d_out, v_new, NT), 0.0)
        d_w = -_dot(d_vnew, s0, NT)
        dst[hh] = _dot(L["qd"], d_out, TN) + ds1 * L["gl"] - _dot(w, d_vnew, TN)

        d_rhs_u = _dot(tinv, d_vnew, TN, hi=True)
        d_rhs_w = _dot(tinv, d_w, TN, hi=True)
        d_a = -(_dot(d_rhs_u, u, NT, hi=True) + _dot(d_rhs_w, w, NT, hi=True))
        d_m = jnp.where(L["strict"], d_a, 0.0)
        g_kk = d_m * L["beta_col"] * decay
        h_qk = d_p * decay
        e = d_m * L["m"] + d_p * p

        d_q = _dot(h_qk, k, NN) + L["eg"] * d_qd
        d_k = (_dot(g_kk, k, NN) + _dot(g_kk, k, TN) + _dot(h_qk, q, TN)
               + betab * L["eg"] * d_rhs_w + L["e_last"] * d_kd)
        dv_ref[:, sl] = betab * d_rhs_u

        d_beta = rowsum(d_m * L["kk"] * decay) + rowsum(d_rhs_u * v) + rowsum(d_rhs_w * k * L["eg"])
        s_kd = rowsum(d_kd * L["kd"])
        d_gc = (rowsum(e) - rowsum(e.T) + rowsum(d_rhs_w * L["rhs_w"]) + rowsum(d_qd * L["qd"]) - s_kd)
        row = lax.broadcasted_iota(jnp.int32, (CHUNK, 1), 0)
        d_gc = d_gc + jnp.where(row == CHUNK - 1, jnp.sum(s_kd) + d_gl * jnp.sum(L["gl"][:, 0:1]), 0.0)
        dgc_ref[:, sl] = jnp.broadcast_to(d_gc, (CHUNK, LANES))
        dbeta_ref[:, sl] = jnp.broadcast_to(d_beta, (CHUNK, LANES))

        d_qh = d_q * L["scale"]
        dq_ref[:, sl] = L["rq"] * (d_qh - L["qh"] * rowsum(d_qh * L["qh"]))
        dk_ref[:, sl] = L["rk"] * (d_k - k * rowsum(d_k * k))

    def body(*refs):
        @pl.when(pl.program_id(2) == 0)
        def _():
            refs[-1][...] = jnp.zeros_like(refs[-1])

        for hh in range(hg):
            one_head(slice(hh * LANES, (hh + 1) * LANES), hh, *refs)

    tok_shape = jax.ShapeDtypeStruct((t, heads * LANES), F32)
    return pl.pallas_call(
        body, name="gdn_bwd", grid=(bsz, heads // hg, n_chunks),
        in_specs=[tok(0), tok(heads), tok(2 * heads), tok(0), tok(0), state, tok(0)],
        out_specs=[tok(0)] * 5,
        out_shape=[tok_shape] * 5,
        scratch_shapes=[pltpu.VMEM((hg, LANES, LANES), F32)], compiler_params=_params(),
    )(qkv_act, qkv_act, qkv_act, gcb, betab, states, d_o)


def _gdn_out_fwd(o, proj, gw_norm, heads, z_off):
    t = o.shape[0]
    ts = _tile(t, 512, 16)
    zb = z_off // LANES

    def body(o_ref, z_ref, w_ref, out_ref):
        z = z_ref[...]
        out_ref[...] = (_rms_fwd(o_ref[...], w_ref[...]) * (z * _sigmoid(z))).astype(BF16)

    return pl.pallas_call(
        body, name="gdn_out_fwd", grid=(t // ts, heads),
        in_specs=[pl.BlockSpec((ts, LANES), lambda i, h: (i, h)), pl.BlockSpec((ts, LANES), lambda i, h: (i, zb + h)),
                  pl.BlockSpec((1, LANES), lambda i, h: (0, 0))],
        out_specs=pl.BlockSpec((ts, LANES), lambda i, h: (i, h)),
        out_shape=jax.ShapeDtypeStruct((t, heads * LANES), BF16), compiler_params=_params(),
    )(o, proj, gw_norm)


def _gdn_out_bwd(dmixed, o, proj, gw_norm, heads, z_off):
    t = o.shape[0]
    ts = _tile(t, 512, 16)
    zb = z_off // LANES

    def body(d_ref, o_ref, z_ref, w_ref, do_ref, dz_ref, dw_ref):
        @pl.when((pl.program_id(0) == 0) & (pl.program_id(1) == 0))
        def _():
            dw_ref[...] = jnp.zeros_like(dw_ref)

        d, oo, z, w = d_ref[...], o_ref[...], z_ref[...], w_ref[...]
        on = _rms_fwd(oo, w)
        dz_ref[...] = (d * on * _silu_grad(z)).astype(BF16)
        d_o, d_w = _rms_bwd(d * (z * _sigmoid(z)), oo, w)
        do_ref[...] = d_o
        dw_ref[...] += d_w

    blk = pl.BlockSpec((ts, LANES), lambda i, h: (i, h))
    vec = pl.BlockSpec((1, LANES), lambda i, h: (0, 0))
    return pl.pallas_call(
        body, name="gdn_out_bwd", grid=(t // ts, heads),
        in_specs=[blk, blk, pl.BlockSpec((ts, LANES), lambda i, h: (i, zb + h)), vec],
        out_specs=[blk, blk, vec],
        out_shape=[jax.ShapeDtypeStruct((t, heads * LANES), F32), jax.ShapeDtypeStruct((t, heads * LANES), BF16),
                   jax.ShapeDtypeStruct((1, LANES), F32)],
        compiler_params=_params(),
    )(dmixed, o, proj, gw_norm)


def _place():
    x, y, c = lax.axis_index("x"), lax.axis_index("y"), lax.axis_index("c")
    return x, y, c, [(1 - x, y), (x, 1 - y), (1 - x, 1 - y)]


def _aligned(start, align):
    return start if isinstance(start, int) else pl.multiple_of(start, align)


class _Layout:
    def __init__(self, kind, shard_shape):
        self.kind = kind
        self.r, self.c = shard_shape

    def full_shape(self):
        r, c = self.r, self.c
        return {"major": (N_CHIPS, r, c), "rows": (N_CHIPS * r, c), "cols": (r, N_CHIPS * c)}[self.kind]

    def region(self, ref, j, half=None):
        r, c = self.r, self.c
        r0, nr = (0, r) if half is None else (half * (r // 2), r // 2)
        if self.kind == "major":
            return ref.at[j, pl.ds(_aligned(r0, 16), nr), :]
        if self.kind == "rows":
            return ref.at[pl.ds(_aligned(j * r + r0, 16), nr), :]
        return ref.at[pl.ds(_aligned(r0, 16), nr), pl.ds(_aligned(j * c, LANES), c)]

    def block_spec(self, tr, where):
        r, c = self.r, self.c
        if self.kind == "major":
            return pl.BlockSpec((None, tr, c), lambda *a: (where(*a)[0], where(*a)[1], 0))
        if self.kind == "rows":
            return pl.BlockSpec((tr, c), lambda *a: (where(*a)[0] * (r // tr) + where(*a)[1], 0))
        return pl.BlockSpec((tr, c), lambda *a: (where(*a)[1], where(*a)[0]))


def _remote(src, dst, send_sem, recv_sem, dev):
    return pltpu.make_async_remote_copy(src_ref=src, dst_ref=dst, send_sem=send_sem, recv_sem=recv_sem,
                                        device_id=dev, device_id_type=MESH)


def _all_gather(big, layouts, small):
    nb, ns = len(big), len(small)
    n_remote = 6 * nb + 3 * ns

    def body(*refs):
        ins, outs = refs[:nb + ns], refs[nb + ns:2 * (nb + ns)]
        send_sems, recv_sems, local_sems = refs[2 * (nb + ns):]
        x, y, c, chips = _place()
        j = 2 * x + y
        local = []
        for i in range(ns):
            local.append(pltpu.make_async_copy(ins[nb + i], outs[nb + i].at[j], local_sems.at[i]))
        for cp in local:
            cp.start()
        sends = []
        for i in range(nb):
            for p, (px, py) in enumerate(chips):
                k = 3 * i + p
                mine = layouts[i].region(outs[i], j, c)
                sends.append(_remote(mine, mine, send_sems.at[k], recv_sems.at[k], (px, py, c)))
        for i in range(ns):
            for p, (px, py) in enumerate(chips):
                k = 6 * nb + 3 * i + p
                sends.append(_remote(ins[nb + i], outs[nb + i].at[j], send_sems.at[k], recv_sems.at[k], (px, py, c)))
        for cp in sends:
            cp.start()
        for i in range(nb):
            for p, (px, py) in enumerate(chips):
                k, jp = 3 * i + p, 2 * px + py
                got = layouts[i].region(outs[i], jp, c)
                _remote(got, got, send_sems.at[k], recv_sems.at[k], (px, py, c)).wait_recv()
                fwd = _remote(got, got, send_sems.at[3 * nb + k], recv_sems.at[3 * nb + k], (x, y, 1 - c))
                fwd.start()
                sends.append(fwd)
        for i in range(ns):
            for p, (px, py) in enumerate(chips):
                k, jp = 6 * nb + 3 * i + p, 2 * px + py
                _remote(ins[nb + i], outs[nb + i].at[jp], send_sems.at[k], recv_sems.at[k], (px, py, c)).wait_recv()
        for i in range(nb):
            for p, (px, py) in enumerate(chips):
                k, jp = 3 * nb + 3 * i + p, 2 * px + py
                got = layouts[i].region(outs[i], jp, 1 - c)
                _remote(got, got, send_sems.at[k], recv_sems.at[k], (x, y, 1 - c)).wait_recv()
        for cp in sends:
            cp.wait_send()
        for cp in local:
            cp.wait()

    out_shape = [jax.ShapeDtypeStruct(lay.full_shape(), BF16) for lay in layouts]
    out_shape += [jax.ShapeDtypeStruct((N_CHIPS,) + s.shape, F32) for s in small]
    return pl.pallas_call(
        body, name="all_gather_weights", in_specs=[ANY] * (nb + ns), out_specs=[ANY] * (nb + ns), out_shape=out_shape,
        input_output_aliases={i: i for i in range(nb)},
        scratch_shapes=[pltpu.SemaphoreType.DMA((n_remote,)), pltpu.SemaphoreType.DMA((n_remote,)),
                        pltpu.SemaphoreType.DMA((ns,))],
        compiler_params=_params(),
    )(*big, *small)


def _halves_to_sibling(grads, layouts):
    nw = len(grads)

    def body(*refs):
        ins, gots = refs[:nw], refs[nw:2 * nw]
        send_sems, recv_sems = refs[2 * nw:]
        x, y, c, _ = _place()
        cps = []
        for i in range(nw):
            for j in range(N_CHIPS):
                k = N_CHIPS * i + j
                cps.append(_remote(layouts[i].region(ins[i], j, 1 - c), gots[i].at[j],
                                   send_sems.at[k], recv_sems.at[k], (x, y, 1 - c)))
        for cp in cps:
            cp.start()
        for cp in cps:
            cp.wait()

    half = [jax.ShapeDtypeStruct((N_CHIPS, lay.r // 2, lay.c), BF16) for lay in layouts]
    return pl.pallas_call(
        body, name="grad_halves_to_sibling", in_specs=[ANY] * nw, out_specs=[ANY] * nw, out_shape=half,
        scratch_shapes=[pltpu.SemaphoreType.DMA((N_CHIPS * nw,)), pltpu.SemaphoreType.DMA((N_CHIPS * nw,))],
        compiler_params=_params(),
    )(*grads)


def _chip_sum(grad, got, layout, core, name):
    n, hr, c = got.shape
    tr = _tile(hr, 256, 16)
    nb = hr // tr

    def body(core_ref, a_ref, b_ref, o_ref):
        o_ref[...] = (a_ref[...].astype(F32) + b_ref[...].astype(F32)).astype(BF16)

    spec = pl.BlockSpec((None, tr, c), lambda j, i, core_ref: (j, i, 0))
    return pl.pallas_call(
        body, name=name,
        grid_spec=pltpu.PrefetchScalarGridSpec(
            num_scalar_prefetch=1, grid=(n, nb),
            in_specs=[layout.block_spec(tr, lambda j, i, core_ref: (j, core_ref[0] * nb + i)), spec],
            out_specs=spec),
        out_shape=jax.ShapeDtypeStruct((n, hr, c), BF16), compiler_params=_params(),
    )(core, grad, got)


def _partials_to_owner(parts):
    nw = len(parts)

    def body(*refs):
        ins, gots = refs[:nw], refs[nw:2 * nw]
        send_sems, recv_sems = refs[2 * nw:]
        x, y, c, chips = _place()
        cps = []
        for i in range(nw):
            for p, (px, py) in enumerate(chips):
                k = 3 * i + p
                cps.append(_remote(ins[i].at[2 * px + py], gots[i].at[p], send_sems.at[k], recv_sems.at[k], (px, py, c)))
        for cp in cps:
            cp.start()
        for cp in cps:
            cp.wait()

    got = [jax.ShapeDtypeStruct((3,) + p.shape[1:], BF16) for p in parts]
    return pl.pallas_call(
        body, name="grad_partials_to_owner", in_specs=[ANY] * nw, out_specs=[ANY] * nw, out_shape=got,
        scratch_shapes=[pltpu.SemaphoreType.DMA((3 * nw,)), pltpu.SemaphoreType.DMA((3 * nw,))],
        compiler_params=_params(),
    )(*parts)


def _shard_sum(parts, got, chip, name):
    _, r, c = parts.shape
    tr = _tile(r, 256, 16)

    def body(chip_ref, o_ref, g_ref, out_ref):
        acc = o_ref[...].astype(F32)
        for p in range(3):
            acc = acc + g_ref[p].astype(F32)
        out_ref[...] = acc

    return pl.pallas_call(
        body, name=name,
        grid_spec=pltpu.PrefetchScalarGridSpec(
            num_scalar_prefetch=1, grid=(r // tr,),
            in_specs=[pl.BlockSpec((None, tr, c), lambda i, chip_ref: (chip_ref[0], i, 0)),
                      pl.BlockSpec((3, tr, c), lambda i, chip_ref: (0, i, 0))],
            out_specs=pl.BlockSpec((tr, c), lambda i, chip_ref: (i, 0))),
        out_shape=jax.ShapeDtypeStruct((r, c), F32), compiler_params=_params(),
    )(chip, parts, got)


def _halves_to_sibling_f32(halves):
    nw = len(halves)

    def body(*refs):
        ins, outs = refs[:nw], refs[nw:2 * nw]
        send_sems, recv_sems = refs[2 * nw:]
        x, y, c, _ = _place()
        cps = [_remote(ins[i], outs[i], send_sems.at[i], recv_sems.at[i], (x, y, 1 - c)) for i in range(nw)]
        for cp in cps:
            cp.start()
        for cp in cps:
            cp.wait()

    return pl.pallas_call(
        body, name="grad_join_halves", in_specs=[ANY] * nw, out_specs=[ANY] * nw,
        out_shape=[jax.ShapeDtypeStruct(h.shape, F32) for h in halves],
        scratch_shapes=[pltpu.SemaphoreType.DMA((nw,)), pltpu.SemaphoreType.DMA((nw,))],
        compiler_params=_params(),
    )(*halves)


def _adamw_halves(w, mine, theirs, m, v, core, name):
    r, c = w.shape
    hr = r // 2
    tr = _tile(hr, 128, 8)
    nb = hr // tr
    c1 = 1.0 - ADAM_B1 ** ADAM_STEP
    c2 = 1.0 - ADAM_B2 ** ADAM_STEP

    def body(core_ref, w_ref, a_ref, b_ref, m_ref, v_ref, g_ref, d_ref, nm_ref, nv_ref):
        gg = jnp.where(pl.program_id(0) == core_ref[0], a_ref[...], b_ref[...])
        nm = ADAM_B1 * m_ref[...] + (1.0 - ADAM_B1) * gg
        nv = ADAM_B2 * v_ref[...] + (1.0 - ADAM_B2) * jnp.square(gg)
        m_hat = nm / c1
        v_hat = nv / c2
        g_ref[...] = gg
        d_ref[...] = -ADAM_LR * (m_hat / (jnp.sqrt(v_hat) + ADAM_EPS) + ADAM_WD * w_ref[...])
        nm_ref[...] = nm
        nv_ref[...] = nv

    full = pl.BlockSpec((tr, c), lambda hf, i, core_ref: (hf * nb + i, 0))
    half = pl.BlockSpec((tr, c), lambda hf, i, core_ref: (i, 0))
    return pl.pallas_call(
        body, name=name,
        grid_spec=pltpu.PrefetchScalarGridSpec(
            num_scalar_prefetch=1, grid=(2, nb), in_specs=[full, half, half, full, full], out_specs=[full] * 4),
        out_shape=[jax.ShapeDtypeStruct((r, c), F32)] * 4, compiler_params=_params(),
    )(core, w, mine, theirs, m, v)


def _small_all_reduce(buf):
    rows = buf.shape[0]
    n_dev = 8

    def body(b_ref, o_ref, gath, send_sems, recv_sems):
        x, y, c, _ = _place()
        me = 4 * x + 2 * y + c
        gath[me] = b_ref[...]
        cps = []
        for k in range(1, n_dev):
            px, py, pc = (x + (k >> 2)) % 2, (y + ((k >> 1) & 1)) % 2, (c + (k & 1)) % 2
            cps.append(_remote(b_ref, gath.at[me], send_sems.at[k - 1], recv_sems.at[k - 1], (px, py, pc)))
        for cp in cps:
            cp.start()
        for k in range(1, n_dev):
            px, py, pc = (x + (k >> 2)) % 2, (y + ((k >> 1) & 1)) % 2, (c + (k & 1)) % 2
            _remote(b_ref, gath.at[4 * px + 2 * py + pc], send_sems.at[k - 1], recv_sems.at[k - 1], (px, py, pc)).wait_recv()
        for cp in cps:
            cp.wait_send()
        acc = gath[0]
        for dev in range(1, n_dev):
            acc = acc + gath[dev]
        o_ref[...] = acc

    vm = pl.BlockSpec(memory_space=pltpu.VMEM)
    return pl.pallas_call(
        body, name="small_all_reduce", in_specs=[vm], out_specs=vm,
        out_shape=jax.ShapeDtypeStruct((rows, LANES), F32),
        scratch_shapes=[pltpu.VMEM((n_dev, rows, LANES), F32), pltpu.SemaphoreType.DMA((n_dev - 1,)),
                        pltpu.SemaphoreType.DMA((n_dev - 1,))],
        compiler_params=_params(),
    )(buf)


def _pad_lanes(v):
    return jnp.pad(v, ((0, 0), (0, LANES - v.shape[-1])))


def _pack(vectors):
    flat, offs, pos = [], [], 0
    for v in vectors:
        n = v.size
        n_pad = -(-n // LANES) * LANES
        flat.append(jnp.pad(v.reshape(-1), (0, n_pad - n)))
        offs.append((pos, n, v.shape))
        pos += n_pad
    total = -(-pos // (8 * LANES)) * 8 * LANES
    flat.append(jnp.zeros((total - pos,), F32))
    return jnp.concatenate(flat).reshape(-1, LANES), offs


def _unpack(buf, offs):
    flat = buf.reshape(-1)
    return [flat[pos:pos + n].reshape(shape) for pos, n, shape in offs]


def kernel(x, norm_mix_pre, w_in, conv_qkv_w, a_log, dt_bias, gdn_norm_w, conv_sc_w, w_out, norm_mix_post, norm_mlp_pre, w_up, w_down, norm_mlp_post, loss_target, m_norm_mix_pre, m_w_in, m_conv_qkv_w, m_a_log, m_dt_bias, m_gdn_norm_w, m_conv_sc_w, m_w_out, m_norm_mix_post, m_norm_mlp_pre, m_w_up, m_w_down, m_norm_mlp_post, v_norm_mix_pre, v_w_in, v_conv_qkv_w, v_a_log, v_dt_bias, v_gdn_norm_w, v_conv_sc_w, v_w_out, v_norm_mix_post, v_norm_mlp_pre, v_w_up, v_w_down, v_norm_mlp_post):
    bsz, seq, d = x.shape
    t = bsz * seq
    heads, head_dim = a_log.shape[-1], gdn_norm_w.shape[-1]
    assert head_dim == LANES and seq % CHUNK == 0
    gw = heads * head_dim
    sw = conv_sc_w.shape[-1] * N_CHIPS
    ics = w_in.shape[-1]
    main = 4 * gw + 3 * sw
    assert ics * N_CHIPS == main + 2 * heads and 2 * heads <= LANES

    lay_in = _Layout("major", w_in.shape[1:])
    lay_out = _Layout("rows", w_out.shape[1:])
    lay_up = _Layout("cols", w_up.shape[1:])
    lay_down = _Layout("rows", w_down.shape[1:])
    layouts = [lay_in, lay_out, lay_up, lay_down]
    chip = (2 * lax.axis_index("x") + lax.axis_index("y")).astype(jnp.int32).reshape(1)
    core = lax.axis_index("c").astype(jnp.int32).reshape(1)
    shards = [_cast_into_layout(w[0], lay, chip, f"cast_{n}")
              for w, lay, n in zip((w_in, w_out, w_up, w_down), layouts, ("w_in", "w_out", "w_up", "w_down"))]
    win_sh, wout_f, wup_f, wdown_f, cq_g, cs_g = _all_gather(shards, layouts, [conv_qkv_w[0], conv_sc_w[0]])
    w_main, w_ab = _repack_w_in(win_sh, gw, heads, sw)
    conv_q = cq_g.transpose(1, 0, 2).reshape(conv_qkv_w.shape[1], -1)
    conv_s = cs_g.transpose(1, 0, 2).reshape(conv_sc_w.shape[1], -1)

    x2 = x.reshape(t, d)
    tgt2 = loss_target.reshape(t, d)
    xn = _norm_fwd(x2, norm_mix_pre)
    proj = _matmul(xn, w_main, "nn", [F32], "proj_main")
    proj_ab = _matmul(xn, w_ab, "nn", [F32], "proj_ab")
    proj3 = proj.reshape(bsz, seq, main)
    qkv_act = _qkv_conv_fwd(proj3, conv_q, 3 * gw).reshape(t, 3 * gw)
    a_log_pad, dt_pad = _pad_lanes(a_log), _pad_lanes(dt_bias)
    gcb, betab = _gates_fwd(proj_ab, a_log_pad, dt_pad, heads)
    o_raw, states = _gdn_fwd(qkv_act, gcb, betab, bsz, heads)
    gdn_out = _gdn_out_fwd(o_raw, proj, gdn_norm_w, heads, 3 * gw)
    sc_out = _sc_fwd(proj3, conv_s, 4 * gw, sw).reshape(t, sw)
    mixed = jnp.concatenate([gdn_out, sc_out], axis=1)
    mix = _matmul(mixed, wout_f, "nn", [F32], "mix_out")
    h, hn = _mid_fwd(x2, mix, norm_mix_post, norm_mlp_pre)

    def up_epilogue(acc):
        r = jnp.maximum(acc, 0.0)
        return r, r * r

    relu_up, hid = _matmul(hn, wup_f, "nn", [BF16, BF16], "mlp_up", epilogue=up_epilogue)
    ff = _matmul(hid, wdown_f, "nn", [F32], "mlp_down")
    loss_blk, dy, dff, dg_mlp_post = _head_fwd_bwd(h, ff, tgt2, norm_mlp_post)

    def dup_epilogue(acc, r):
        return (acc * (2.0 * r.astype(F32)),)

    d_up = _matmul(dff, wdown_f, "nt", [BF16], "d_hid", epilogue=dup_epilogue, extras=(relu_up,))
    dw_down = _matmul(hid, dff, "tn", [BF16], "dw_down")
    d_hn = _matmul(d_up, wup_f, "nt", [F32], "d_hn")
    dw_up = _matmul(hn, d_up, "tn", [BF16], "dw_up")
    dh, dmix, dg_mlp_pre, dg_mix_post = _mid_bwd(d_hn, h, norm_mlp_pre, dy, mix, norm_mix_post)
    dmixed = _matmul(dmix, wout_f, "nt", [F32], "d_mixed")
    dw_out = _matmul(mixed, dmix, "tn", [BF16], "dw_out")
    dmixed3 = dmixed.reshape(bsz, seq, d)
    d_b, d_c, d_hsc, dw_conv_s = _sc_bwd(proj3, dmixed3, conv_s, 4 * gw, sw, gw)
    d_o, d_z, dg_gdn_norm = _gdn_out_bwd(dmixed, o_raw, proj, gdn_norm_w, heads, 3 * gw)
    dq, dk, dv, dgc_b, dbeta_b = _gdn_bwd(qkv_act, gcb, betab, states, d_o, bsz, heads)
    d_ab, d_alog, d_dt = _gates_bwd(proj_ab, a_log_pad, dt_pad, dgc_b, dbeta_b, heads)
    dact3 = jnp.concatenate([dq, dk, dv], axis=1).reshape(bsz, seq, 3 * gw)
    d_qkv, dw_conv_q = _qkv_conv_bwd(proj3, dact3, conv_q, 3 * gw)
    d_proj = jnp.concatenate([d_qkv.reshape(t, 3 * gw), d_z, d_b.reshape(t, sw), d_c.reshape(t, sw),
                              d_hsc.reshape(t, sw)], axis=1)
    d_xn = _matmul(d_proj, w_main, "nt", [F32], "d_xn_main")
    d_xn_ab = _matmul(d_ab, w_ab, "nt", [F32], "d_xn_ab")
    dw_main = _matmul(xn, d_proj, "tn", [BF16], "dw_in_main")
    dw_ab = _matmul(xn, d_ab, "tn", [BF16], "dw_in_ab")
    dw_in = _unpack_dw_in(dw_main, dw_ab, gw, heads, sw, ics)
    grad_x, dg_mix_pre = _first_bwd(d_xn, d_xn_ab, x2, norm_mix_pre, dh)

    full_grads = [dw_in, dw_out, dw_up, dw_down]
    gots = _halves_to_sibling(full_grads, layouts)
    parts = [_chip_sum(g, got, lay, core, f"chip_sum_{i}") for i, (g, got, lay) in enumerate(zip(full_grads, gots, layouts))]
    recvs = _partials_to_owner(parts)
    halves = [_shard_sum(p, r, chip, f"shard_sum_{i}") for i, (p, r) in enumerate(zip(parts, recvs))]
    sibling_halves = _halves_to_sibling_f32(halves)

    small, offs = _pack([loss_blk[0:1, 0:1], dg_mix_pre, dw_conv_q, d_alog[:, :heads], d_dt[:, :heads], dg_gdn_norm,
                         dw_conv_s, dg_mix_post, dg_mlp_pre, dg_mlp_post])
    (loss, g_mix_pre, g_conv_q_full, g_alog, g_dt, g_gdn_norm, g_conv_s_full, g_mix_post, g_mlp_pre,
     g_mlp_post) = _unpack(_small_all_reduce(small), offs)
    j = 2 * lax.axis_index("x") + lax.axis_index("y")
    cq_w, cs_w = conv_qkv_w.shape[-1], conv_sc_w.shape[-1]
    g_conv_q = lax.dynamic_slice_in_dim(g_conv_q_full, j * cq_w, cq_w, axis=1)
    g_conv_s = lax.dynamic_slice_in_dim(g_conv_s_full, j * cs_w, cs_w, axis=1)

    big = dict(zip((1, 7, 10, 11), zip(halves, sibling_halves)))
    grads = [g_mix_pre, None, g_conv_q, g_alog, g_dt, g_gdn_norm, g_conv_s, None, g_mix_post, g_mlp_pre, None,
             None, g_mlp_post]
    weights = [norm_mix_pre, w_in, conv_qkv_w, a_log, dt_bias, gdn_norm_w, conv_sc_w, w_out, norm_mix_post,
               norm_mlp_pre, w_up, w_down, norm_mlp_post]
    ms = [m_norm_mix_pre, m_w_in, m_conv_qkv_w, m_a_log, m_dt_bias, m_gdn_norm_w, m_conv_sc_w, m_w_out,
          m_norm_mix_post, m_norm_mlp_pre, m_w_up, m_w_down, m_norm_mlp_post]
    vs = [v_norm_mix_pre, v_w_in, v_conv_qkv_w, v_a_log, v_dt_bias, v_gdn_norm_w, v_conv_sc_w, v_w_out,
          v_norm_mix_post, v_norm_mlp_pre, v_w_up, v_w_down, v_norm_mlp_post]
    out_g, out_d, out_m, out_v = [], [], [], []
    for i, (wt, g, m, v) in enumerate(zip(weights, grads, ms, vs)):
        shape2 = wt.shape[-2:] if wt.ndim == 3 else wt.shape
        if i in big:
            g2, dl, nm, nv = _adamw_halves(wt.reshape(shape2), *big[i], m.reshape(shape2), v.reshape(shape2), core,
                                           f"adamw_{i}")
        else:
            g2 = g.reshape(shape2)
            dl, nm, nv = _adamw(wt.reshape(shape2), g2, m.reshape(shape2), v.reshape(shape2), f"adamw_{i}")
        out_g.append(g2.reshape(wt.shape))
        out_d.append(dl.reshape(wt.shape))
        out_m.append(nm.reshape(wt.shape))
        out_v.append(nv.reshape(wt.shape))

    return (loss.reshape(()), grad_x.reshape(bsz, seq, d), *out_g, *out_d, *out_m, *out_v)
```

```python
import functools

import jax
import jax.numpy as jnp
from jax import lax
from jax.experimental import pallas as pl
from jax.experimental.pallas import tpu as pltpu

CHUNK = 64
NORM_EPS = 1e-6
L2_EPS = 1e-6
N_CHIPS = 4
ADAM_LR = 0.001
ADAM_B1 = 0.9
ADAM_B2 = 0.999
ADAM_EPS = 1e-08
ADAM_WD = 0.01
ADAM_STEP = 10
LANES = 128
VMEM_LIMIT = 56 * 1024 * 1024

F32 = jnp.float32
BF16 = jnp.bfloat16
HI = lax.Precision.HIGHEST
MESH = pl.DeviceIdType.MESH
ANY = pl.BlockSpec(memory_space=pl.ANY)


def _params(n_grid=0):
    return pltpu.CompilerParams(vmem_limit_bytes=VMEM_LIMIT)


def _tile(n, pref, align):
    if n <= pref:
        return n
    t = (pref // align) * align
    while t >= align:
        if n % t == 0:
            return t
        t -= align
    raise ValueError(f"no tile for {n}")


def _sigmoid(x):
    return 1.0 / (1.0 + jnp.exp(-x))


def _softplus(x):
    return jnp.maximum(x, 0.0) + jnp.log(1.0 + jnp.exp(-jnp.abs(x)))


def _rms_fwd(x, g):
    r = lax.rsqrt(jnp.mean(x * x, axis=-1, keepdims=True) + NORM_EPS)
    return x * r * g


def _rms_bwd(dy, x, g):
    r = lax.rsqrt(jnp.mean(x * x, axis=-1, keepdims=True) + NORM_EPS)
    xh = x * r
    dxh = dy * g
    dx = r * (dxh - xh * jnp.mean(dxh * xh, axis=-1, keepdims=True))
    dg = jnp.sum(dy * xh, axis=0, keepdims=True)
    return dx, dg


def _matmul(a, b, form, out_dtypes, name, epilogue=None, extras=(), tm=1024, tn=1024, tk=2048):
    if form == "nn":
        (m, kd), (_, n) = a.shape, b.shape
        dims = (((1,), (0,)), ((), ()))
    elif form == "nt":
        (m, kd), (n, _) = a.shape, b.shape
        dims = (((1,), (1,)), ((), ()))
    else:
        (kd, m), (_, n) = a.shape, b.shape
        dims = (((0,), (0,)), ((), ()))
    tm, tn, tk = _tile(m, tm, LANES), _tile(n, tn, LANES), _tile(kd, tk, LANES)
    nk = kd // tk
    n_extra = len(extras)
    n_out = len(out_dtypes)

    if form == "nn":
        a_spec = pl.BlockSpec((tm, tk), lambda i, j, k: (i, k))
        b_spec = pl.BlockSpec((tk, tn), lambda i, j, k: (k, j))
    elif form == "nt":
        a_spec = pl.BlockSpec((tm, tk), lambda i, j, k: (i, k))
        b_spec = pl.BlockSpec((tn, tk), lambda i, j, k: (j, k))
    else:
        a_spec = pl.BlockSpec((tk, tm), lambda i, j, k: (k, i))
        b_spec = pl.BlockSpec((tk, tn), lambda i, j, k: (k, j))
    tile_spec = pl.BlockSpec((tm, tn), lambda i, j, k: (i, j))

    def body(a_ref, b_ref, *rest):
        extra_refs = rest[:n_extra]
        out_refs = rest[n_extra:n_extra + n_out]
        acc_ref = rest[-1]
        k = pl.program_id(2)

        @pl.when(k == 0)
        def _():
            acc_ref[...] = jnp.zeros_like(acc_ref)

        acc_ref[...] += lax.dot_general(a_ref[...], b_ref[...], dims, preferred_element_type=F32)

        @pl.when(k == nk - 1)
        def _():
            acc = acc_ref[...]
            outs = (acc,) if epilogue is None else epilogue(acc, *[e[...] for e in extra_refs])
            for o_ref, val in zip(out_refs, outs):
                o_ref[...] = val.astype(o_ref.dtype)

    outs = pl.pallas_call(
        body, name=name, grid=(m // tm, n // tn, nk),
        in_specs=[a_spec, b_spec] + [tile_spec] * n_extra,
        out_specs=[tile_spec] * n_out,
        out_shape=[jax.ShapeDtypeStruct((m, n), dt) for dt in out_dtypes],
        scratch_shapes=[pltpu.VMEM((tm, tn), F32)],
        compiler_params=_params(),
    )(a, b, *extras)
    return outs[0] if n_out == 1 else outs


def _cast_into_layout(w, layout, chip, name):
    r, c = w.shape
    tr = _tile(r, 256, 16)

    def body(chip_ref, w_ref, o_ref):
        o_ref[...] = w_ref[...].astype(BF16)

    return pl.pallas_call(
        body, name=name,
        grid_spec=pltpu.PrefetchScalarGridSpec(
            num_scalar_prefetch=1, grid=(r // tr,),
            in_specs=[pl.BlockSpec((tr, c), lambda i, chip_ref: (i, 0))],
            out_specs=layout.block_spec(tr, lambda i, chip_ref: (chip_ref[0], i))),
        out_shape=jax.ShapeDtypeStruct(layout.full_shape(), BF16), compiler_params=_params(),
    )(chip, w)


def _in_segments(gw, heads, sw):
    main = 4 * gw
    return [(0, main, 0), (main + 2 * heads, 3 * sw, main), (main, 2 * heads, main + 3 * sw)]


def _pieces(seg_start, width, dst_start, ics):
    out = []
    g = seg_start
    while g < seg_start + width:
        j, cj = divmod(g, ics)
        wdt = min(ics - cj, seg_start + width - g)
        out.append((j, cj, dst_start + (g - seg_start), wdt))
        g += wdt
    return out


def _repack_w_in(w_sh, gw, heads, sw, after):
    ns, d, ics = w_sh.shape
    main = 4 * gw + 3 * sw
    tr = _tile(d, 128, 16)
    pieces = [p for seg in _in_segments(gw, heads, sw) for p in _pieces(*seg, ics)]

    def body(w_ref, after_ref, m_ref, ab_ref):
        ab_ref[...] = jnp.zeros_like(ab_ref)
        for j, cj, cd, wdt in pieces:
            if cd >= main:
                ab_ref[:, cd - main:cd - main + wdt] = w_ref[j, :, cj:cj + wdt]
            else:
                m_ref[:, cd:cd + wdt] = w_ref[j, :, cj:cj + wdt]

    return pl.pallas_call(
        body, name="repack_w_in", grid=(d // tr,),
        in_specs=[pl.BlockSpec((ns, tr, ics), lambda i: (0, i, 0)), ANY],
        out_specs=[pl.BlockSpec((tr, main), lambda i: (i, 0)), pl.BlockSpec((tr, LANES), lambda i: (i, 0))],
        out_shape=[jax.ShapeDtypeStruct((d, main), BF16), jax.ShapeDtypeStruct((d, LANES), BF16)],
        compiler_params=_params(),
    )(w_sh, after)


def _unpack_dw_in(dw_main, dw_ab, gw, heads, sw, ics):
    d = dw_main.shape[0]
    tr = _tile(d, 128, 16)
    main = 4 * gw + 3 * sw
    pieces = [p for seg in _in_segments(gw, heads, sw) for p in _pieces(*seg, ics)]

    def body(m_ref, ab_ref, o_ref):
        for j, cj, cd, wdt in pieces:
            if cd >= main:
                o_ref[j, :, cj:cj + wdt] = ab_ref[:, cd - main:cd - main + wdt].astype(BF16)
            else:
                o_ref[j, :, cj:cj + wdt] = m_ref[:, cd:cd + wdt].astype(BF16)

    return pl.pallas_call(
        body, name="unpack_dw_in", grid=(d // tr,),
        in_specs=[pl.BlockSpec((tr, main), lambda i: (i, 0)), pl.BlockSpec((tr, LANES), lambda i: (i, 0))],
        out_specs=pl.BlockSpec((N_CHIPS, tr, ics), lambda i: (0, i, 0)),
        out_shape=jax.ShapeDtypeStruct((N_CHIPS, d, ics), BF16), compiler_params=_params(),
    )(dw_main, dw_ab)


def _adamw(w, g, m, v, name):
    r, c = w.shape
    tr = _tile(r, 128, 8)
    c1 = 1.0 - ADAM_B1 ** ADAM_STEP
    c2 = 1.0 - ADAM_B2 ** ADAM_STEP

    def body(w_ref, g_ref, m_ref, v_ref, d_ref, nm_ref, nv_ref):
        gg = g_ref[...]
        nm = ADAM_B1 * m_ref[...] + (1.0 - ADAM_B1) * gg
        nv = ADAM_B2 * v_ref[...] + (1.0 - ADAM_B2) * jnp.square(gg)
        m_hat = nm / c1
        v_hat = nv / c2
        d_ref[...] = -ADAM_LR * (m_hat / (jnp.sqrt(v_hat) + ADAM_EPS) + ADAM_WD * w_ref[...])
        nm_ref[...] = nm
        nv_ref[...] = nv

    spec = pl.BlockSpec((tr, c), lambda i: (i, 0))
    return pl.pallas_call(
        body, name=name, grid=(r // tr,), in_specs=[spec] * 4, out_specs=[spec] * 3,
        out_shape=[jax.ShapeDtypeStruct((r, c), F32)] * 3, compiler_params=_params(),
    )(w, g, m, v)


def _row_spec(tt, d):
    return pl.BlockSpec((tt, d), lambda i: (i, 0))


def _vec_spec(d):
    return pl.BlockSpec((1, d), lambda i: (0, 0))


def _norm_fwd(x, g):
    t, d = x.shape
    tt = _tile(t, 256, 16)

    def body(x_ref, g_ref, o_ref):
        o_ref[...] = _rms_fwd(x_ref[...], g_ref[...]).astype(BF16)

    return pl.pallas_call(
        body, name="norm_mix_pre", grid=(t // tt,), in_specs=[_row_spec(tt, d), _vec_spec(d)],
        out_specs=_row_spec(tt, d), out_shape=jax.ShapeDtypeStruct((t, d), BF16), compiler_params=_params(),
    )(x, g)


def _mid_fwd(x, mix, g_post, g_pre):
    t, d = x.shape
    tt = _tile(t, 128, 16)

    def body(x_ref, mix_ref, gp_ref, gn_ref, h_ref, hn_ref):
        h = x_ref[...] + _rms_fwd(mix_ref[...], gp_ref[...])
        h_ref[...] = h
        hn_ref[...] = _rms_fwd(h, gn_ref[...]).astype(BF16)

    return pl.pallas_call(
        body, name="mid_fwd", grid=(t // tt,),
        in_specs=[_row_spec(tt, d), _row_spec(tt, d), _vec_spec(d), _vec_spec(d)],
        out_specs=[_row_spec(tt, d), _row_spec(tt, d)],
        out_shape=[jax.ShapeDtypeStruct((t, d), F32), jax.ShapeDtypeStruct((t, d), BF16)],
        compiler_params=_params(),
    )(x, mix, g_post, g_pre)


def _head_fwd_bwd(h, ff, tgt, g_post):
    t, d = h.shape
    tt = _tile(t, 128, 16)

    def body(h_ref, ff_ref, t_ref, g_ref, loss_ref, dy_ref, dff_ref, dg_ref):
        i = pl.program_id(0)

        @pl.when(i == 0)
        def _():
            loss_ref[...] = jnp.zeros_like(loss_ref)
            dg_ref[...] = jnp.zeros_like(dg_ref)

        ff = ff_ref[...]
        g = g_ref[...]
        e = h_ref[...] + _rms_fwd(ff, g) - t_ref[...]
        loss_ref[...] += 0.5 * jnp.sum(jnp.mean(e * e, axis=-1, keepdims=True))
        dy = e * (1.0 / d)
        dy_ref[...] = dy
        dff, dg = _rms_bwd(dy, ff, g)
        dff_ref[...] = dff.astype(BF16)
        dg_ref[...] += dg

    return pl.pallas_call(
        body, name="loss_head", grid=(t // tt,),
        in_specs=[_row_spec(tt, d)] * 3 + [_vec_spec(d)],
        out_specs=[pl.BlockSpec((8, LANES), lambda i: (0, 0)), _row_spec(tt, d), _row_spec(tt, d), _vec_spec(d)],
        out_shape=[jax.ShapeDtypeStruct((8, LANES), F32), jax.ShapeDtypeStruct((t, d), F32),
                   jax.ShapeDtypeStruct((t, d), BF16), jax.ShapeDtypeStruct((1, d), F32)],
        compiler_params=_params(),
    )(h, ff, tgt, g_post)


def _mid_bwd(d_hn, h, g_pre, dy, mix, g_post, after):
    t, d = h.shape
    tt = _tile(t, 128, 16)

    def body(dhn_ref, h_ref, gn_ref, dy_ref, mix_ref, gp_ref, after_ref, dh_ref, dmix_ref, dgn_ref, dgp_ref):
        i = pl.program_id(0)

        @pl.when(i == 0)
        def _():
            dgn_ref[...] = jnp.zeros_like(dgn_ref)
            dgp_ref[...] = jnp.zeros_like(dgp_ref)

        dx, dgn = _rms_bwd(dhn_ref[...], h_ref[...], gn_ref[...])
        dh = dy_ref[...] + dx
        dh_ref[...] = dh
        dmix, dgp = _rms_bwd(dh, mix_ref[...], gp_ref[...])
        dmix_ref[...] = dmix.astype(BF16)
        dgn_ref[...] += dgn
        dgp_ref[...] += dgp

    return pl.pallas_call(
        body, name="mid_bwd", grid=(t // tt,),
        in_specs=[_row_spec(tt, d), _row_spec(tt, d), _vec_spec(d), _row_spec(tt, d), _row_spec(tt, d), _vec_spec(d),
                  ANY],
        out_specs=[_row_spec(tt, d), _row_spec(tt, d), _vec_spec(d), _vec_spec(d)],
        out_shape=[jax.ShapeDtypeStruct((t, d), F32), jax.ShapeDtypeStruct((t, d), BF16),
                   jax.ShapeDtypeStruct((1, d), F32), jax.ShapeDtypeStruct((1, d), F32)],
        compiler_params=_params(),
    )(d_hn, h, g_pre, dy, mix, g_post, after)


def _first_bwd(d_xn, d_xn_ab, x, g, dh):
    t, d = x.shape
    tt = _tile(t, 128, 16)

    def body(a_ref, b_ref, x_ref, g_ref, dh_ref, dx_ref, dg_ref):
        i = pl.program_id(0)

        @pl.when(i == 0)
        def _():
            dg_ref[...] = jnp.zeros_like(dg_ref)

        dx, dg = _rms_bwd(a_ref[...] + b_ref[...], x_ref[...], g_ref[...])
        dx_ref[...] = dh_ref[...] + dx
        dg_ref[...] += dg

    return pl.pallas_call(
        body, name="first_bwd", grid=(t // tt,),
        in_specs=[_row_spec(tt, d), _row_spec(tt, d), _row_spec(tt, d), _vec_spec(d), _row_spec(tt, d)],
        out_specs=[_row_spec(tt, d), _vec_spec(d)],
        out_shape=[jax.ShapeDtypeStruct((t, d), F32), jax.ShapeDtypeStruct((1, d), F32)],
        compiler_params=_params(),
    )(d_xn, d_xn_ab, x, g, dh)


HALO = 8


def _cur(ts, tc, off):
    return pl.BlockSpec((1, ts, tc), lambda ci, b, s: (b, s, off + ci))


def _prev(ts, tc, off):
    return pl.BlockSpec((1, HALO, tc), lambda ci, b, s: (b, jnp.maximum(s * (ts // HALO) - 1, 0), off + ci))


def _next(ts, tc, off, seq):
    last = seq // HALO - 1
    return pl.BlockSpec((1, HALO, tc), lambda ci, b, s: (b, jnp.minimum((s + 1) * (ts // HALO), last), off + ci))


def _conv_w_spec(kw, tc):
    return pl.BlockSpec((kw, tc), lambda ci, b, s: (0, ci))


def _conv_taps(w, buf, kw, ts):
    acc = w[0:1, :] * buf[HALO - (kw - 1):HALO - (kw - 1) + ts, :]
    for j in range(1, kw):
        acc = acc + w[j:j + 1, :] * buf[HALO - (kw - 1) + j:HALO - (kw - 1) + j + ts, :]
    return acc


def _silu_grad(x):
    s = _sigmoid(x)
    return s * (1.0 + x * (1.0 - s))


def _qkv_conv_fwd(proj3, w, width):
    bsz, seq, _ = proj3.shape
    kw = w.shape[0]
    ts, tc = _tile(seq, 256, 8), _tile(width, 512, LANES)

    def body(u_ref, up_ref, w_ref, o_ref, buf):
        s = pl.program_id(2)
        buf[0:HALO, :] = jnp.where(s == 0, 0.0, up_ref[0])
        buf[HALO:HALO + ts, :] = u_ref[0]
        pre = _conv_taps(w_ref[...], buf, kw, ts)
        o_ref[0] = pre * _sigmoid(pre)

    return pl.pallas_call(
        body, name="qkv_conv_fwd", grid=(width // tc, bsz, seq // ts),
        in_specs=[_cur(ts, tc, 0), _prev(ts, tc, 0), _conv_w_spec(kw, tc)],
        out_specs=_cur(ts, tc, 0), out_shape=jax.ShapeDtypeStruct((bsz, seq, width), F32),
        scratch_shapes=[pltpu.VMEM((HALO + ts, tc), F32)], compiler_params=_params(),
    )(proj3, proj3, w)


def _qkv_conv_bwd(proj3, dact3, w, width):
    bsz, seq, _ = proj3.shape
    kw = w.shape[0]
    ts, tc = _tile(seq, 256, 8), _tile(width, 512, LANES)
    n_s = seq // ts

    def body(u_ref, up_ref, un_ref, d_ref, dn_ref, w_ref, du_ref, dw_ref, buf, gbuf):
        b, s = pl.program_id(1), pl.program_id(2)

        @pl.when((b == 0) & (s == 0))
        def _():
            dw_ref[...] = jnp.zeros_like(dw_ref)

        w = w_ref[...]
        buf[0:HALO, :] = jnp.where(s == 0, 0.0, up_ref[0])
        buf[HALO:HALO + ts, :] = u_ref[0]
        buf[HALO + ts:, :] = un_ref[0]
        acc = w[0:1, :] * buf[HALO - (kw - 1):HALO - (kw - 1) + ts + HALO, :]
        for j in range(1, kw):
            acc = acc + w[j:j + 1, :] * buf[HALO - (kw - 1) + j:HALO - (kw - 1) + j + ts + HALO, :]
        gbuf[0:ts, :] = d_ref[0]
        gbuf[ts:, :] = jnp.where(s == n_s - 1, 0.0, dn_ref[0])
        gbuf[...] = gbuf[...] * _silu_grad(acc)
        g_cur = gbuf[0:ts, :]
        for j in range(kw):
            dw_ref[j:j + 1, :] += jnp.sum(g_cur * buf[HALO - (kw - 1) + j:HALO - (kw - 1) + j + ts, :], axis=0, keepdims=True)
        du = w[0:1, :] * gbuf[kw - 1:kw - 1 + ts, :]
        for j in range(1, kw):
            du = du + w[j:j + 1, :] * gbuf[kw - 1 - j:kw - 1 - j + ts, :]
        du_ref[0] = du.astype(BF16)

    return pl.pallas_call(
        body, name="qkv_conv_bwd", grid=(width // tc, bsz, n_s),
        in_specs=[_cur(ts, tc, 0), _prev(ts, tc, 0), _next(ts, tc, 0, seq), _cur(ts, tc, 0), _next(ts, tc, 0, seq),
                  _conv_w_spec(kw, tc)],
        out_specs=[_cur(ts, tc, 0), _conv_w_spec(kw, tc)],
        out_shape=[jax.ShapeDtypeStruct((bsz, seq, width), BF16), jax.ShapeDtypeStruct((kw, width), F32)],
        scratch_shapes=[pltpu.VMEM((HALO + ts + HALO, tc), F32), pltpu.VMEM((ts + HALO, tc), F32)],
        compiler_params=_params(),
    )(proj3, proj3, proj3, dact3, dact3, w)


def _sc_fwd(proj3, w, off, sw):
    bsz, seq, _ = proj3.shape
    kw = w.shape[0]
    ts, tc = _tile(seq, 256, 8), _tile(sw, 512, LANES)
    ob, oc, oh = off // tc, (off + sw) // tc, (off + 2 * sw) // tc

    def body(b_ref, c_ref, cp_ref, h_ref, hp_ref, w_ref, o_ref, buf):
        s = pl.program_id(2)
        buf[0:HALO, :] = jnp.where(s == 0, 0.0, cp_ref[0] * hp_ref[0])
        buf[HALO:HALO + ts, :] = c_ref[0] * h_ref[0]
        o_ref[0] = (b_ref[0] * _conv_taps(w_ref[...], buf, kw, ts)).astype(BF16)

    return pl.pallas_call(
        body, name="sc_fwd", grid=(sw // tc, bsz, seq // ts),
        in_specs=[_cur(ts, tc, ob), _cur(ts, tc, oc), _prev(ts, tc, oc), _cur(ts, tc, oh), _prev(ts, tc, oh),
                  _conv_w_spec(kw, tc)],
        out_specs=_cur(ts, tc, 0), out_shape=jax.ShapeDtypeStruct((bsz, seq, sw), BF16),
        scratch_shapes=[pltpu.VMEM((HALO + ts, tc), F32)], compiler_params=_params(),
    )(proj3, proj3, proj3, proj3, proj3, w)


def _sc_bwd(proj3, dmixed3, w, off, sw, d_off):
    bsz, seq, _ = proj3.shape
    kw = w.shape[0]
    ts, tc = _tile(seq, 256, 8), _tile(sw, 512, LANES)
    n_s = seq // ts
    ob, oc, oh, od = off // tc, (off + sw) // tc, (off + 2 * sw) // tc, d_off // tc

    def body(d_ref, dn_ref, b_ref, bn_ref, c_ref, cp_ref, h_ref, hp_ref, w_ref,
             db_ref, dc_ref, dh_ref, dw_ref, buf, gbuf):
        b, s = pl.program_id(1), pl.program_id(2)

        @pl.when((b == 0) & (s == 0))
        def _():
            dw_ref[...] = jnp.zeros_like(dw_ref)

        w = w_ref[...]
        cc, hh = c_ref[0], h_ref[0]
        buf[0:HALO, :] = jnp.where(s == 0, 0.0, cp_ref[0] * hp_ref[0])
        buf[HALO:HALO + ts, :] = cc * hh
        dout = d_ref[0]
        db_ref[0] = (dout * _conv_taps(w, buf, kw, ts)).astype(BF16)
        g_cur = dout * b_ref[0]
        gbuf[0:ts, :] = g_cur
        gbuf[ts:, :] = jnp.where(s == n_s - 1, 0.0, dn_ref[0] * bn_ref[0])
        for j in range(kw):
            dw_ref[j:j + 1, :] += jnp.sum(g_cur * buf[HALO - (kw - 1) + j:HALO - (kw - 1) + j + ts, :], axis=0, keepdims=True)
        dp = w[0:1, :] * gbuf[kw - 1:kw - 1 + ts, :]
        for j in range(1, kw):
            dp = dp + w[j:j + 1, :] * gbuf[kw - 1 - j:kw - 1 - j + ts, :]
        dc_ref[0] = (dp * hh).astype(BF16)
        dh_ref[0] = (dp * cc).astype(BF16)

    out = jax.ShapeDtypeStruct((bsz, seq, sw), BF16)
    return pl.pallas_call(
        body, name="sc_bwd", grid=(sw // tc, bsz, n_s),
        in_specs=[_cur(ts, tc, od), _next(ts, tc, od, seq), _cur(ts, tc, ob), _next(ts, tc, ob, seq),
                  _cur(ts, tc, oc), _prev(ts, tc, oc), _cur(ts, tc, oh), _prev(ts, tc, oh), _conv_w_spec(kw, tc)],
        out_specs=[_cur(ts, tc, 0)] * 3 + [_conv_w_spec(kw, tc)],
        out_shape=[out, out, out, jax.ShapeDtypeStruct((kw, sw), F32)],
        scratch_shapes=[pltpu.VMEM((HALO + ts, tc), F32), pltpu.VMEM((ts + HALO, tc), F32)],
        compiler_params=_params(),
    )(dmixed3, dmixed3, proj3, proj3, proj3, proj3, proj3, proj3, w)


def _tri_ones(lower):
    i = lax.broadcasted_iota(jnp.int32, (CHUNK, CHUNK), 0)
    j = lax.broadcasted_iota(jnp.int32, (CHUNK, CHUNK), 1)
    return jnp.where((i >= j) if lower else (j >= i), 1.0, 0.0).astype(F32)


def _gates_fwd(proj_ab, a_log_pad, dt_pad, heads):
    t = proj_ab.shape[0]
    gw = heads * LANES

    def body(ab_ref, al_ref, dt_ref, gc_ref, beta_ref):
        ab = ab_ref[...]
        g = -jnp.exp(al_ref[...]) * _softplus(ab + dt_ref[...])
        gc = jnp.dot(_tri_ones(True), g, precision=HI, preferred_element_type=F32)
        beta = _sigmoid(ab)
        for h in range(heads):
            gc_ref[:, h * LANES:(h + 1) * LANES] = jnp.broadcast_to(gc[:, h:h + 1], (CHUNK, LANES))
            beta_ref[:, h * LANES:(h + 1) * LANES] = jnp.broadcast_to(beta[:, heads + h:heads + h + 1], (CHUNK, LANES))

    return pl.pallas_call(
        body, name="gates_fwd", grid=(t // CHUNK,),
        in_specs=[_row_spec(CHUNK, LANES), _vec_spec(LANES), _vec_spec(LANES)],
        out_specs=[_row_spec(CHUNK, gw), _row_spec(CHUNK, gw)],
        out_shape=[jax.ShapeDtypeStruct((t, gw), F32)] * 2, compiler_params=_params(),
    )(proj_ab, a_log_pad, dt_pad)


def _gates_bwd(proj_ab, a_log_pad, dt_pad, dgc_b, dbeta_b, heads):
    t = proj_ab.shape[0]
    gw = heads * LANES

    def body(ab_ref, al_ref, dt_ref, dgc_ref, dbeta_ref, dab_ref, dal_ref, ddt_ref):
        i = pl.program_id(0)

        @pl.when(i == 0)
        def _():
            dal_ref[...] = jnp.zeros_like(dal_ref)
            ddt_ref[...] = jnp.zeros_like(ddt_ref)

        lane = lax.broadcasted_iota(jnp.int32, (CHUNK, LANES), 1)
        dgc = jnp.zeros((CHUNK, LANES), F32)
        dbeta = jnp.zeros((CHUNK, LANES), F32)
        for h in range(heads):
            dgc = jnp.where(lane == h, dgc_ref[:, h * LANES:(h + 1) * LANES], dgc)
            dbeta = jnp.where(lane == heads + h, dbeta_ref[:, h * LANES:(h + 1) * LANES], dbeta)
        dg = jnp.dot(_tri_ones(False), dgc, precision=HI, preferred_element_type=F32)
        ab = ab_ref[...]
        z = ab + dt_ref[...]
        ea = jnp.exp(al_ref[...])
        da = dg * (-ea) * _sigmoid(z)
        beta = _sigmoid(ab)
        db = dbeta * beta * (1.0 - beta)
        dab_ref[...] = jnp.where(lane < heads, da, jnp.where(lane < 2 * heads, db, 0.0)).astype(BF16)
        da_m = jnp.where(lane < heads, da, 0.0)
        ddt_ref[...] += jnp.sum(da_m, axis=0, keepdims=True)
        dal_ref[...] += jnp.sum(jnp.where(lane < heads, dg * (-ea) * _softplus(z), 0.0), axis=0, keepdims=True)

    return pl.pallas_call(
        body, name="gates_bwd", grid=(t // CHUNK,),
        in_specs=[_row_spec(CHUNK, LANES), _vec_spec(LANES), _vec_spec(LANES), _row_spec(CHUNK, gw), _row_spec(CHUNK, gw)],
        out_specs=[_row_spec(CHUNK, LANES), _vec_spec(LANES), _vec_spec(LANES)],
        out_shape=[jax.ShapeDtypeStruct((t, LANES), BF16), jax.ShapeDtypeStruct((1, LANES), F32),
                   jax.ShapeDtypeStruct((1, LANES), F32)],
        compiler_params=_params(),
    )(proj_ab, a_log_pad, dt_pad, dgc_b, dbeta_b)


def _dot(a, b, dims, hi=False):
    if hi:
        return lax.dot_general(a, b, (dims, ((), ())), precision=HI, preferred_element_type=F32)
    return lax.dot_general(a.astype(BF16), b.astype(BF16), (dims, ((), ())), preferred_element_type=F32)


NN = ((1,), (0,))
NT = ((1,), (1,))
TN = ((0,), (0,))


def _each(f, *lists):
    return [f(*xs) for xs in zip(*lists)]


def _dots(a, b, dims, hi=False):
    return _each(lambda x, y: _dot(x, y, dims, hi=hi), a, b)


def _unit_lower_inverse(ms):
    i = lax.broadcasted_iota(jnp.int32, (CHUNK, CHUNK), 0)
    j = lax.broadcasted_iota(jnp.int32, (CHUNK, CHUNK), 1)
    ts = [jnp.where(i == j, 1.0, 0.0).astype(F32)] * len(ms)
    shift = 0
    while (1 << shift) < CHUNK:
        same_pair = jnp.right_shift(i, shift + 1) == jnp.right_shift(j, shift + 1)
        other_half = jnp.right_shift(i, shift) != jnp.right_shift(j, shift)
        offs = [jnp.where(same_pair & other_half, m, 0.0) for m in ms]
        corr = _dots(_dots(ts, offs, NN, hi=True), ts, NN, hi=True)
        ts = _each(lambda t, c: t - c, ts, corr)
        shift += 1
    return ts


def _chunk_local(qrs, krs, vs, gcbs, betabs, head_dim):
    i = lax.broadcasted_iota(jnp.int32, (CHUNK, CHUNK), 0)
    j = lax.broadcasted_iota(jnp.int32, (CHUNK, CHUNK), 1)
    scale = head_dim ** -0.5
    rqs = [lax.rsqrt(jnp.sum(q * q, axis=-1, keepdims=True) + L2_EPS) for q in qrs]
    rks = [lax.rsqrt(jnp.sum(k * k, axis=-1, keepdims=True) + L2_EPS) for k in krs]
    qhs = _each(lambda a, r: a * r, qrs, rqs)
    ks = _each(lambda a, r: a * r, krs, rks)
    qs = [a * scale for a in qhs]
    decays = [jnp.exp(jnp.where(i >= j, g[:, 0:CHUNK] - g.T[0:CHUNK, :], -jnp.inf)) for g in gcbs]
    kks = _dots(ks, ks, NT)
    qks = _dots(qs, ks, NT)
    ms = _each(lambda b, kk, d: jnp.where(i > j, b[:, 0:CHUNK] * kk * d, 0.0), betabs, kks, decays)
    tinvs = _unit_lower_inverse(ms)
    egs = [jnp.exp(g) for g in gcbs]
    rhs_ws = _each(lambda k, b, e: k * b * e, ks, betabs, egs)
    us = _dots(tinvs, _each(lambda v, b: v * b, vs, betabs), NN, hi=True)
    ws = _dots(tinvs, rhs_ws, NN, hi=True)
    out = []
    for h in range(len(qrs)):
        g_last = gcbs[h][CHUNK - 1:CHUNK, :]
        e_last = jnp.exp(g_last - gcbs[h])
        out.append(dict(rq=rqs[h], rk=rks[h], qh=qhs[h], q=qs[h], k=ks[h], decay=decays[h],
                        beta_col=betabs[h][:, 0:CHUNK], kk=kks[h], m=ms[h], tinv=tinvs[h], eg=egs[h], rhs_w=rhs_ws[h],
                        u=us[h], w=ws[h], p=qks[h] * decays[h], qd=qs[h] * egs[h], kd=ks[h] * e_last, e_last=e_last,
                        gl=jnp.exp(g_last), scale=scale, strict=i > j, incl=i >= j))
    return out


def _field(dicts, name):
    return [d[name] for d in dicts]


GDN_HEAD_GROUP = 8


def _gdn_specs(n_chunks, heads, reverse):
    hg = min(GDN_HEAD_GROUP, heads)
    assert heads % hg == 0

    def cidx(c):
        return (n_chunks - 1 - c) if reverse else c

    def tok(off):
        return pl.BlockSpec((CHUNK, hg * LANES), lambda b, h, c: (b * n_chunks + cidx(c), off // hg + h))

    state = pl.BlockSpec((None, hg, LANES, LANES), lambda b, h, c: (b * n_chunks + cidx(c), h, 0, 0))
    return hg, tok, state


def _gdn_fwd(qkv_act, gcb, betab, bsz, heads):
    t = qkv_act.shape[0]
    n_chunks = t // bsz // CHUNK
    hg, tok, state = _gdn_specs(n_chunks, heads, False)

    def body(q_ref, k_ref, v_ref, gc_ref, beta_ref, o_ref, s_ref, st):
        @pl.when(pl.program_id(2) == 0)
        def _():
            st[...] = jnp.zeros_like(st)

        sls = [slice(hh * LANES, (hh + 1) * LANES) for hh in range(hg)]
        loc = _chunk_local(*[[r[:, sl] for sl in sls] for r in (q_ref, k_ref, v_ref, gc_ref, beta_ref)], LANES)
        s0 = [st[hh] for hh in range(hg)]
        v_new = _each(lambda u, ws: u - ws, _field(loc, "u"), _dots(_field(loc, "w"), s0, NN))
        o_state = _dots(_field(loc, "qd"), s0, NN)
        o_local = _dots(_field(loc, "p"), v_new, NN)
        s_add = _dots(_field(loc, "kd"), v_new, TN)
        for hh in range(hg):
            o_ref[:, sls[hh]] = o_state[hh] + o_local[hh]
            s_ref[hh] = s0[hh]
            st[hh] = s0[hh] * loc[hh]["gl"] + s_add[hh]

    return pl.pallas_call(
        body, name="gdn_fwd", grid=(bsz, heads // hg, n_chunks),
        in_specs=[tok(0), tok(heads), tok(2 * heads), tok(0), tok(0)],
        out_specs=[tok(0), state],
        out_shape=[jax.ShapeDtypeStruct((t, heads * LANES), F32),
                   jax.ShapeDtypeStruct((bsz * n_chunks, heads, LANES, LANES), F32)],
        scratch_shapes=[pltpu.VMEM((hg, LANES, LANES), F32)], compiler_params=_params(),
    )(qkv_act, qkv_act, qkv_act, gcb, betab)


def _gdn_bwd(qkv_act, gcb, betab, states, d_o, bsz, heads):
    t = qkv_act.shape[0]
    n_chunks = t // bsz // CHUNK
    hg, tok, state = _gdn_specs(n_chunks, heads, True)

    def rowsum(a):
        return jnp.sum(a, axis=-1, keepdims=True)

    def finish_head(sl, L, v, betab, d_qd, d_kd, d_gl, d_p, d_m, d_rhs_u, d_rhs_w, d_q, d_k,
                    dq_ref, dk_ref, dv_ref, dgc_ref, dbeta_ref):
        k, decay = L["k"], L["decay"]
        dv_ref[:, sl] = betab * d_rhs_u
        e = d_m * L["m"] + d_p * L["p"]
        d_beta = rowsum(d_m * L["kk"] * decay) + rowsum(d_rhs_u * v) + rowsum(d_rhs_w * k * L["eg"])
        s_kd = rowsum(d_kd * L["kd"])
        d_gc = (rowsum(e) - rowsum(e.T) + rowsum(d_rhs_w * L["rhs_w"]) + rowsum(d_qd * L["qd"]) - s_kd)
        row = lax.broadcasted_iota(jnp.int32, (CHUNK, 1), 0)
        d_gc = d_gc + jnp.where(row == CHUNK - 1, jnp.sum(s_kd) + d_gl * jnp.sum(L["gl"][:, 0:1]), 0.0)
        dgc_ref[:, sl] = jnp.broadcast_to(d_gc, (CHUNK, LANES))
        dbeta_ref[:, sl] = jnp.broadcast_to(d_beta, (CHUNK, LANES))
        d_qh = d_q * L["scale"]
        dq_ref[:, sl] = L["rq"] * (d_qh - L["qh"] * rowsum(d_qh * L["qh"]))
        dk_ref[:, sl] = L["rk"] * (d_k - k * rowsum(d_k * k))

    def body(q_ref, k_ref, v_ref, gc_ref, beta_ref, s_ref, do_ref, dq_ref, dk_ref, dv_ref, dgc_ref, dbeta_ref, dst):
        @pl.when(pl.program_id(2) == 0)
        def _():
            dst[...] = jnp.zeros_like(dst)

        sls = [slice(hh * LANES, (hh + 1) * LANES) for hh in range(hg)]
        vs = [v_ref[:, sl] for sl in sls]
        betabs = [beta_ref[:, sl] for sl in sls]
        loc = _chunk_local([q_ref[:, sl] for sl in sls], [k_ref[:, sl] for sl in sls], vs,
                           [gc_ref[:, sl] for sl in sls], betabs, LANES)
        q, k, u, w, p, tinv, decay, qd, kd, eg = (_field(loc, n) for n in
                                                  ("q", "k", "u", "w", "p", "tinv", "decay", "qd", "kd", "eg"))
        s0 = [s_ref[hh] for hh in range(hg)]
        d_out = [do_ref[:, sl] for sl in sls]
        ds1 = [dst[hh] for hh in range(hg)]
        v_new = _each(lambda a, b: a - b, u, _dots(w, s0, NN))

        d_vnew = _each(lambda a, b: a + b, _dots(p, d_out, TN), _dots(kd, ds1, NN))
        d_qd = _dots(d_out, s0, NT)
        d_kd = _dots(v_new, ds1, NT)
        d_gl = _each(lambda a, b: jnp.sum(a * b), ds1, s0)
        d_p = _each(lambda L, a: jnp.where(L["incl"], a, 0.0), loc, _dots(d_out, v_new, NT))
        d_w = [-a for a in _dots(d_vnew, s0, NT)]
        ds_out, ds_vn = _dots(qd, d_out, TN), _dots(w, d_vnew, TN)
        for hh in range(hg):
            dst[hh] = ds_out[hh] + ds1[hh] * loc[hh]["gl"] - ds_vn[hh]

        d_rhs_u = _dots(tinv, d_vnew, TN, hi=True)
        d_rhs_w = _dots(tinv, d_w, TN, hi=True)
        d_a = _each(lambda a, b: -(a + b), _dots(d_rhs_u, u, NT, hi=True), _dots(d_rhs_w, w, NT, hi=True))
        d_m = _each(lambda L, a: jnp.where(L["strict"], a, 0.0), loc, d_a)
        g_kk = _each(lambda L, a: a * L["beta_col"] * L["decay"], loc, d_m)
        h_qk = _each(lambda a, d: a * d, d_p, decay)

        d_q = _each(lambda a, e, b: a + e * b, _dots(h_qk, k, NN), eg, d_qd)
        d_k = _each(lambda a, b, c, L, bb, rw, dk: a + b + c + bb * L["eg"] * rw + L["e_last"] * dk,
                    _dots(g_kk, k, NN), _dots(g_kk, k, TN), _dots(h_qk, q, TN), loc, betabs, d_rhs_w, d_kd)
        for hh in range(hg):
            finish_head(sls[hh], loc[hh], vs[hh], betabs[hh], d_qd[hh], d_kd[hh], d_gl[hh], d_p[hh], d_m[hh],
                        d_rhs_u[hh], d_rhs_w[hh], d_q[hh], d_k[hh], dq_ref, dk_ref, dv_ref, dgc_ref, dbeta_ref)

    tok_shape = jax.ShapeDtypeStruct((t, heads * LANES), F32)
    return pl.pallas_call(
        body, name="gdn_bwd", grid=(bsz, heads // hg, n_chunks),
        in_specs=[tok(0), tok(heads), tok(2 * heads), tok(0), tok(0), state, tok(0)],
        out_specs=[tok(0)] * 5,
        out_shape=[tok_shape] * 5,
        scratch_shapes=[pltpu.VMEM((hg, LANES, LANES), F32)], compiler_params=_params(),
    )(qkv_act, qkv_act, qkv_act, gcb, betab, states, d_o)


def _gdn_out_fwd(o, proj, gw_norm, heads, z_off):
    t = o.shape[0]
    ts = _tile(t, 512, 16)
    zb = z_off // LANES

    def body(o_ref, z_ref, w_ref, out_ref):
        z = z_ref[...]
        out_ref[...] = (_rms_fwd(o_ref[...], w_ref[...]) * (z * _sigmoid(z))).astype(BF16)

    return pl.pallas_call(
        body, name="gdn_out_fwd", grid=(t // ts, heads),
        in_specs=[pl.BlockSpec((ts, LANES), lambda i, h: (i, h)), pl.BlockSpec((ts, LANES), lambda i, h: (i, zb + h)),
                  pl.BlockSpec((1, LANES), lambda i, h: (0, 0))],
        out_specs=pl.BlockSpec((ts, LANES), lambda i, h: (i, h)),
        out_shape=jax.ShapeDtypeStruct((t, heads * LANES), BF16), compiler_params=_params(),
    )(o, proj, gw_norm)


def _gdn_out_bwd(dmixed, o, proj, gw_norm, heads, z_off):
    t = o.shape[0]
    ts = _tile(t, 512, 16)
    zb = z_off // LANES

    def body(d_ref, o_ref, z_ref, w_ref, do_ref, dz_ref, dw_ref):
        @pl.when((pl.program_id(0) == 0) & (pl.program_id(1) == 0))
        def _():
            dw_ref[...] = jnp.zeros_like(dw_ref)

        d, oo, z, w = d_ref[...], o_ref[...], z_ref[...], w_ref[...]
        on = _rms_fwd(oo, w)
        dz_ref[...] = (d * on * _silu_grad(z)).astype(BF16)
        d_o, d_w = _rms_bwd(d * (z * _sigmoid(z)), oo, w)
        do_ref[...] = d_o
        dw_ref[...] += d_w

    blk = pl.BlockSpec((ts, LANES), lambda i, h: (i, h))
    vec = pl.BlockSpec((1, LANES), lambda i, h: (0, 0))
    return pl.pallas_call(
        body, name="gdn_out_bwd", grid=(t // ts, heads),
        in_specs=[blk, blk, pl.BlockSpec((ts, LANES), lambda i, h: (i, zb + h)), vec],
        out_specs=[blk, blk, vec],
        out_shape=[jax.ShapeDtypeStruct((t, heads * LANES), F32), jax.ShapeDtypeStruct((t, heads * LANES), BF16),
                   jax.ShapeDtypeStruct((1, LANES), F32)],
        compiler_params=_params(),
    )(dmixed, o, proj, gw_norm)


def _place():
    x, y, c = lax.axis_index("x"), lax.axis_index("y"), lax.axis_index("c")
    return x, y, c, [(1 - x, y), (x, 1 - y), (1 - x, 1 - y)]


def _aligned(start, align):
    return start if isinstance(start, int) else pl.multiple_of(start, align)


class _Layout:
    def __init__(self, kind, shard_shape):
        self.kind = kind
        self.r, self.c = shard_shape

    def full_shape(self):
        r, c = self.r, self.c
        return {"major": (N_CHIPS, r, c), "rows": (N_CHIPS * r, c), "cols": (r, N_CHIPS * c)}[self.kind]

    def region(self, ref, j, half=None):
        r, c = self.r, self.c
        r0, nr = (0, r) if half is None else (half * (r // 2), r // 2)
        if self.kind == "major":
            return ref.at[j, pl.ds(_aligned(r0, 16), nr), :]
        if self.kind == "rows":
            return ref.at[pl.ds(_aligned(j * r + r0, 16), nr), :]
        return ref.at[pl.ds(_aligned(r0, 16), nr), pl.ds(_aligned(j * c, LANES), c)]

    def block_spec(self, tr, where):
        r, c = self.r, self.c
        if self.kind == "major":
            return pl.BlockSpec((None, tr, c), lambda *a: (where(*a)[0], where(*a)[1], 0))
        if self.kind == "rows":
            return pl.BlockSpec((tr, c), lambda *a: (where(*a)[0] * (r // tr) + where(*a)[1], 0))
        return pl.BlockSpec((tr, c), lambda *a: (where(*a)[1], where(*a)[0]))


def _remote(src, dst, send_sem, recv_sem, dev):
    return pltpu.make_async_remote_copy(src_ref=src, dst_ref=dst, send_sem=send_sem, recv_sem=recv_sem,
                                        device_id=dev, device_id_type=MESH)


def _all_gather(big, layouts, small):
    nb, ns = len(big), len(small)
    n_remote = 6 * nb + 3 * ns

    def body(*refs):
        ins, outs = refs[:nb + ns], refs[nb + ns:2 * (nb + ns)]
        send_sems, recv_sems, local_sems = refs[2 * (nb + ns):]
        x, y, c, chips = _place()
        j = 2 * x + y
        local = []
        for i in range(ns):
            local.append(pltpu.make_async_copy(ins[nb + i], outs[nb + i].at[j], local_sems.at[i]))
        for cp in local:
            cp.start()
        sends = []
        for i in range(nb):
            for p, (px, py) in enumerate(chips):
                k = 3 * i + p
                mine = layouts[i].region(outs[i], j, c)
                sends.append(_remote(mine, mine, send_sems.at[k], recv_sems.at[k], (px, py, c)))
        for i in range(ns):
            for p, (px, py) in enumerate(chips):
                k = 6 * nb + 3 * i + p
                sends.append(_remote(ins[nb + i], outs[nb + i].at[j], send_sems.at[k], recv_sems.at[k], (px, py, c)))
        for cp in sends:
            cp.start()
        for i in range(nb):
            for p, (px, py) in enumerate(chips):
                k, jp = 3 * i + p, 2 * px + py
                got = layouts[i].region(outs[i], jp, c)
                _remote(got, got, send_sems.at[k], recv_sems.at[k], (px, py, c)).wait_recv()
                fwd = _remote(got, got, send_sems.at[3 * nb + k], recv_sems.at[3 * nb + k], (x, y, 1 - c))
                fwd.start()
                sends.append(fwd)
        for i in range(ns):
            for p, (px, py) in enumerate(chips):
                k, jp = 6 * nb + 3 * i + p, 2 * px + py
                _remote(ins[nb + i], outs[nb + i].at[jp], send_sems.at[k], recv_sems.at[k], (px, py, c)).wait_recv()
        for i in range(nb):
            for p, (px, py) in enumerate(chips):
                k, jp = 3 * nb + 3 * i + p, 2 * px + py
                got = layouts[i].region(outs[i], jp, 1 - c)
                _remote(got, got, send_sems.at[k], recv_sems.at[k], (x, y, 1 - c)).wait_recv()
        for cp in sends:
            cp.wait_send()
        for cp in local:
            cp.wait()

    out_shape = [jax.ShapeDtypeStruct(lay.full_shape(), BF16) for lay in layouts]
    out_shape += [jax.ShapeDtypeStruct((N_CHIPS,) + s.shape, F32) for s in small]
    return pl.pallas_call(
        body, name="all_gather_weights", in_specs=[ANY] * (nb + ns), out_specs=[ANY] * (nb + ns), out_shape=out_shape,
        input_output_aliases={i: i for i in range(nb)},
        scratch_shapes=[pltpu.SemaphoreType.DMA((n_remote,)), pltpu.SemaphoreType.DMA((n_remote,)),
                        pltpu.SemaphoreType.DMA((ns,))],
        compiler_params=_params(),
    )(*big, *small)


HBM = pl.BlockSpec(memory_space=pltpu.HBM)
SEM = pl.BlockSpec(memory_space=pltpu.SEMAPHORE)
SPLIT_COPY = pltpu.CompilerParams(has_side_effects=pltpu.SideEffectType.DATAFLOW_SIDE_EFFECTING)


def _in_hbm(a):
    return pltpu.with_memory_space_constraint(a, pltpu.HBM)


def _gather_start(bufs, layouts, after):
    nb = len(bufs)

    def body(*refs):
        send_sems, recv_sems = refs[nb + 1], refs[nb + 2]
        thru, token = refs[nb + 3:2 * nb + 3], refs[2 * nb + 3]
        x, y, c, chips = _place()
        j = 2 * x + y
        for i in range(nb):
            mine = layouts[i].region(thru[i], j, c)
            for p, (px, py) in enumerate(chips):
                _remote(mine, mine, send_sems.at[3 * i + p], recv_sems.at[3 * i + p], (px, py, c)).start()
        token[...] = jnp.zeros_like(token)

    outs = pl.pallas_call(
        body, name="gather_start", in_specs=[HBM] * nb + [ANY],
        out_specs=[SEM, SEM] + [HBM] * nb + [pl.BlockSpec(memory_space=pltpu.VMEM)],
        out_shape=[pltpu.SemaphoreType.DMA((3 * nb,)), pltpu.SemaphoreType.DMA((3 * nb,))]
        + [pltpu.HBM(b.shape, b.dtype) for b in bufs] + [jax.ShapeDtypeStruct((8, LANES), F32)],
        input_output_aliases={i: 2 + i for i in range(nb)}, compiler_params=SPLIT_COPY,
    )(*[_in_hbm(b) for b in bufs], after)
    return outs[0], outs[1], outs[2:2 + nb], outs[2 + nb]


def _gather_wait(send_sems, recv_sems, bufs, layouts, after):
    nb = len(bufs)

    def body(*refs):
        send_sems, recv_sems = refs[nb], refs[nb + 1]
        outs = refs[nb + 3:]
        x, y, c, chips = _place()
        j = 2 * x + y
        for i in range(nb):
            mine = layouts[i].region(outs[i], j, c)
            for p, (px, py) in enumerate(chips):
                cp = _remote(mine, layouts[i].region(outs[i], 2 * px + py, c), send_sems.at[3 * i + p],
                             recv_sems.at[3 * i + p], (px, py, c))
                cp.wait_send()
                cp.wait_recv()

    return pl.pallas_call(
        body, name="gather_wait", in_specs=[HBM] * nb + [SEM, SEM, ANY], out_specs=[HBM] * nb,
        out_shape=[pltpu.HBM(b.shape, b.dtype) for b in bufs],
        input_output_aliases={i: i for i in range(nb)}, compiler_params=SPLIT_COPY,
    )(*bufs, send_sems, recv_sems, after)


def _forward_to_sibling(bufs, layouts):
    nb = len(bufs)

    def body(*refs):
        outs = refs[nb:2 * nb]
        send_sems, recv_sems = refs[2 * nb:]
        x, y, c, chips = _place()
        cps = []
        for i in range(nb):
            for p, (px, py) in enumerate(chips):
                got = layouts[i].region(outs[i], 2 * px + py, c)
                cps.append(_remote(got, got, send_sems.at[3 * i + p], recv_sems.at[3 * i + p], (x, y, 1 - c)))
        for cp in cps:
            cp.start()
        for i in range(nb):
            for p, (px, py) in enumerate(chips):
                theirs = layouts[i].region(outs[i], 2 * px + py, 1 - c)
                _remote(theirs, theirs, send_sems.at[3 * i + p], recv_sems.at[3 * i + p], (x, y, 1 - c)).wait_recv()
        for cp in cps:
            cp.wait_send()

    return pl.pallas_call(
        body, name="gather_forward_to_sibling", in_specs=[ANY] * nb, out_specs=[ANY] * nb,
        out_shape=[jax.ShapeDtypeStruct(b.shape, b.dtype) for b in bufs],
        input_output_aliases={i: i for i in range(nb)},
        scratch_shapes=[pltpu.SemaphoreType.DMA((3 * nb,)), pltpu.SemaphoreType.DMA((3 * nb,))],
        compiler_params=_params(),
    )(*bufs)


def _halves_to_sibling(grads, layouts, name):
    nw = len(grads)

    def body(*refs):
        ins, gots = refs[:nw], refs[nw:2 * nw]
        send_sems, recv_sems = refs[2 * nw:]
        x, y, c, _ = _place()
        cps = []
        for i in range(nw):
            for j in range(N_CHIPS):
                k = N_CHIPS * i + j
                cps.append(_remote(layouts[i].region(ins[i], j, 1 - c), gots[i].at[j],
                                   send_sems.at[k], recv_sems.at[k], (x, y, 1 - c)))
        for cp in cps:
            cp.start()
        for cp in cps:
            cp.wait()

    half = [jax.ShapeDtypeStruct((N_CHIPS, lay.r // 2, lay.c), BF16) for lay in layouts]
    return pl.pallas_call(
        body, name=name, in_specs=[ANY] * nw, out_specs=[ANY] * nw, out_shape=half,
        scratch_shapes=[pltpu.SemaphoreType.DMA((N_CHIPS * nw,)), pltpu.SemaphoreType.DMA((N_CHIPS * nw,))],
        compiler_params=_params(),
    )(*grads)


def _chip_sum(grad, got, layout, core, name):
    n, hr, c = got.shape
    tr = _tile(hr, 256, 16)
    nb = hr // tr

    def body(core_ref, a_ref, b_ref, o_ref):
        o_ref[...] = (a_ref[...].astype(F32) + b_ref[...].astype(F32)).astype(BF16)

    spec = pl.BlockSpec((None, tr, c), lambda j, i, core_ref: (j, i, 0))
    return pl.pallas_call(
        body, name=name,
        grid_spec=pltpu.PrefetchScalarGridSpec(
            num_scalar_prefetch=1, grid=(n, nb),
            in_specs=[layout.block_spec(tr, lambda j, i, core_ref: (j, core_ref[0] * nb + i)), spec],
            out_specs=spec),
        out_shape=jax.ShapeDtypeStruct((n, hr, c), BF16), compiler_params=_params(),
    )(core, grad, got)


def _partials_to_owner(parts):
    nw = len(parts)

    def body(*refs):
        ins, gots = refs[:nw], refs[nw:2 * nw]
        send_sems, recv_sems = refs[2 * nw:]
        x, y, c, chips = _place()
        cps = []
        for i in range(nw):
            for p, (px, py) in enumerate(chips):
                k = 3 * i + p
                cps.append(_remote(ins[i].at[2 * px + py], gots[i].at[p], send_sems.at[k], recv_sems.at[k], (px, py, c)))
        for cp in cps:
            cp.start()
        for cp in cps:
            cp.wait()

    got = [jax.ShapeDtypeStruct((3,) + p.shape[1:], BF16) for p in parts]
    return pl.pallas_call(
        body, name="grad_partials_to_owner", in_specs=[ANY] * nw, out_specs=[ANY] * nw, out_shape=got,
        scratch_shapes=[pltpu.SemaphoreType.DMA((3 * nw,)), pltpu.SemaphoreType.DMA((3 * nw,))],
        compiler_params=_params(),
    )(*parts)


def _partials_start(parts, after):
    nw = len(parts)
    lands = [lax.empty((3,) + p.shape[1:], BF16) for p in parts]

    def body(*refs):
        send_sems, recv_sems = refs[2 * nw + 1], refs[2 * nw + 2]
        src, dst = refs[2 * nw + 3:3 * nw + 3], refs[3 * nw + 3:4 * nw + 3]
        token = refs[4 * nw + 3]
        x, y, c, chips = _place()
        for i in range(nw):
            for p, (px, py) in enumerate(chips):
                _remote(src[i].at[2 * px + py], dst[i].at[p], send_sems.at[3 * i + p], recv_sems.at[3 * i + p],
                        (px, py, c)).start()
        token[...] = jnp.zeros_like(token)

    outs = pl.pallas_call(
        body, name="grad_partials_start", in_specs=[HBM] * (2 * nw) + [ANY],
        out_specs=[SEM, SEM] + [HBM] * (2 * nw) + [pl.BlockSpec(memory_space=pltpu.VMEM)],
        out_shape=[pltpu.SemaphoreType.DMA((3 * nw,)), pltpu.SemaphoreType.DMA((3 * nw,))]
        + [pltpu.HBM(a.shape, a.dtype) for a in parts + lands] + [jax.ShapeDtypeStruct((8, LANES), F32)],
        input_output_aliases={i: 2 + i for i in range(2 * nw)}, compiler_params=SPLIT_COPY,
    )(*[_in_hbm(a) for a in parts + lands], after)
    return outs[0], outs[1], outs[2:2 + nw], outs[2 + nw:2 + 2 * nw], outs[2 + 2 * nw]


def _partials_wait(send_sems, recv_sems, parts, lands, after):
    nw = len(parts)

    def body(*refs):
        send_sems, recv_sems = refs[2 * nw], refs[2 * nw + 1]
        src, dst = refs[2 * nw + 3:3 * nw + 3], refs[3 * nw + 3:4 * nw + 3]
        x, y, c, chips = _place()
        for i in range(nw):
            for p, (px, py) in enumerate(chips):
                cp = _remote(src[i].at[2 * px + py], dst[i].at[p], send_sems.at[3 * i + p], recv_sems.at[3 * i + p],
                             (px, py, c))
                cp.wait_send()
                cp.wait_recv()

    outs = pl.pallas_call(
        body, name="grad_partials_wait", in_specs=[HBM] * (2 * nw) + [SEM, SEM, ANY], out_specs=[HBM] * (2 * nw),
        out_shape=[pltpu.HBM(a.shape, a.dtype) for a in list(parts) + list(lands)],
        input_output_aliases={i: i for i in range(2 * nw)}, compiler_params=SPLIT_COPY,
    )(*parts, *lands, send_sems, recv_sems, after)
    return outs[:nw], outs[nw:]


def _shard_sum(parts, got, chip, name):
    _, r, c = parts.shape
    tr = _tile(r, 256, 16)

    def body(chip_ref, o_ref, g_ref, out_ref):
        acc = o_ref[...].astype(F32)
        for p in range(3):
            acc = acc + g_ref[p].astype(F32)
        out_ref[...] = acc

    return pl.pallas_call(
        body, name=name,
        grid_spec=pltpu.PrefetchScalarGridSpec(
            num_scalar_prefetch=1, grid=(r // tr,),
            in_specs=[pl.BlockSpec((None, tr, c), lambda i, chip_ref: (chip_ref[0], i, 0)),
                      pl.BlockSpec((3, tr, c), lambda i, chip_ref: (0, i, 0))],
            out_specs=pl.BlockSpec((tr, c), lambda i, chip_ref: (i, 0))),
        out_shape=jax.ShapeDtypeStruct((r, c), F32), compiler_params=_params(),
    )(chip, parts, got)


def _halves_to_sibling_f32(halves):
    nw = len(halves)

    def body(*refs):
        ins, outs = refs[:nw], refs[nw:2 * nw]
        send_sems, recv_sems = refs[2 * nw:]
        x, y, c, _ = _place()
        cps = [_remote(ins[i], outs[i], send_sems.at[i], recv_sems.at[i], (x, y, 1 - c)) for i in range(nw)]
        for cp in cps:
            cp.start()
        for cp in cps:
            cp.wait()

    return pl.pallas_call(
        body, name="grad_join_halves", in_specs=[ANY] * nw, out_specs=[ANY] * nw,
        out_shape=[jax.ShapeDtypeStruct(h.shape, F32) for h in halves],
        scratch_shapes=[pltpu.SemaphoreType.DMA((nw,)), pltpu.SemaphoreType.DMA((nw,))],
        compiler_params=_params(),
    )(*halves)


def _adamw_halves(w, mine, theirs, m, v, core, name):
    r, c = w.shape
    hr = r // 2
    tr = _tile(hr, 128, 8)
    nb = hr // tr
    c1 = 1.0 - ADAM_B1 ** ADAM_STEP
    c2 = 1.0 - ADAM_B2 ** ADAM_STEP

    def body(core_ref, w_ref, a_ref, b_ref, m_ref, v_ref, g_ref, d_ref, nm_ref, nv_ref):
        gg = jnp.where(pl.program_id(0) == core_ref[0], a_ref[...], b_ref[...])
        nm = ADAM_B1 * m_ref[...] + (1.0 - ADAM_B1) * gg
        nv = ADAM_B2 * v_ref[...] + (1.0 - ADAM_B2) * jnp.square(gg)
        m_hat = nm / c1
        v_hat = nv / c2
        g_ref[...] = gg
        d_ref[...] = -ADAM_LR * (m_hat / (jnp.sqrt(v_hat) + ADAM_EPS) + ADAM_WD * w_ref[...])
        nm_ref[...] = nm
        nv_ref[...] = nv

    full = pl.BlockSpec((tr, c), lambda hf, i, core_ref: (hf * nb + i, 0))
    half = pl.BlockSpec((tr, c), lambda hf, i, core_ref: (i, 0))
    return pl.pallas_call(
        body, name=name,
        grid_spec=pltpu.PrefetchScalarGridSpec(
            num_scalar_prefetch=1, grid=(2, nb), in_specs=[full, half, half, full, full], out_specs=[full] * 4),
        out_shape=[jax.ShapeDtypeStruct((r, c), F32)] * 4, compiler_params=_params(),
    )(core, w, mine, theirs, m, v)


def _small_all_reduce(buf):
    rows = buf.shape[0]
    n_dev = 8

    def body(b_ref, o_ref, gath, send_sems, recv_sems):
        x, y, c, _ = _place()
        me = 4 * x + 2 * y + c
        gath[me] = b_ref[...]
        cps = []
        for k in range(1, n_dev):
            px, py, pc = (x + (k >> 2)) % 2, (y + ((k >> 1) & 1)) % 2, (c + (k & 1)) % 2
            cps.append(_remote(b_ref, gath.at[me], send_sems.at[k - 1], recv_sems.at[k - 1], (px, py, pc)))
        for cp in cps:
            cp.start()
        for k in range(1, n_dev):
            px, py, pc = (x + (k >> 2)) % 2, (y + ((k >> 1) & 1)) % 2, (c + (k & 1)) % 2
            _remote(b_ref, gath.at[4 * px + 2 * py + pc], send_sems.at[k - 1], recv_sems.at[k - 1], (px, py, pc)).wait_recv()
        for cp in cps:
            cp.wait_send()
        acc = gath[0]
        for dev in range(1, n_dev):
            acc = acc + gath[dev]
        o_ref[...] = acc

    vm = pl.BlockSpec(memory_space=pltpu.VMEM)
    return pl.pallas_call(
        body, name="small_all_reduce", in_specs=[vm], out_specs=vm,
        out_shape=jax.ShapeDtypeStruct((rows, LANES), F32),
        scratch_shapes=[pltpu.VMEM((n_dev, rows, LANES), F32), pltpu.SemaphoreType.DMA((n_dev - 1,)),
                        pltpu.SemaphoreType.DMA((n_dev - 1,))],
        compiler_params=_params(),
    )(buf)


def _pad_lanes(v):
    return jnp.pad(v, ((0, 0), (0, LANES - v.shape[-1])))


def _pack(vectors):
    flat, offs, pos = [], [], 0
    for v in vectors:
        n = v.size
        n_pad = -(-n // LANES) * LANES
        flat.append(jnp.pad(v.reshape(-1), (0, n_pad - n)))
        offs.append((pos, n, v.shape))
        pos += n_pad
    total = -(-pos // (8 * LANES)) * 8 * LANES
    flat.append(jnp.zeros((total - pos,), F32))
    return jnp.concatenate(flat).reshape(-1, LANES), offs


def _unpack(buf, offs):
    flat = buf.reshape(-1)
    return [flat[pos:pos + n].reshape(shape) for pos, n, shape in offs]


def kernel(x, norm_mix_pre, w_in, conv_qkv_w, a_log, dt_bias, gdn_norm_w, conv_sc_w, w_out, norm_mix_post, norm_mlp_pre, w_up, w_down, norm_mlp_post, loss_target, m_norm_mix_pre, m_w_in, m_conv_qkv_w, m_a_log, m_dt_bias, m_gdn_norm_w, m_conv_sc_w, m_w_out, m_norm_mix_post, m_norm_mlp_pre, m_w_up, m_w_down, m_norm_mlp_post, v_norm_mix_pre, v_w_in, v_conv_qkv_w, v_a_log, v_dt_bias, v_gdn_norm_w, v_conv_sc_w, v_w_out, v_norm_mix_post, v_norm_mlp_pre, v_w_up, v_w_down, v_norm_mlp_post):
    bsz, seq, d = x.shape
    t = bsz * seq
    heads, head_dim = a_log.shape[-1], gdn_norm_w.shape[-1]
    assert head_dim == LANES and seq % CHUNK == 0
    gw = heads * head_dim
    sw = conv_sc_w.shape[-1] * N_CHIPS
    ics = w_in.shape[-1]
    main = 4 * gw + 3 * sw
    assert ics * N_CHIPS == main + 2 * heads and 2 * heads <= LANES

    lay_in = _Layout("major", w_in.shape[1:])
    lay_out = _Layout("rows", w_out.shape[1:])
    lay_up = _Layout("cols", w_up.shape[1:])
    lay_down = _Layout("rows", w_down.shape[1:])
    layouts = [lay_in, lay_out, lay_up, lay_down]
    chip = (2 * lax.axis_index("x") + lax.axis_index("y")).astype(jnp.int32).reshape(1)
    core = lax.axis_index("c").astype(jnp.int32).reshape(1)
    shards = [_cast_into_layout(w[0], lay, chip, f"cast_{n}")
              for w, lay, n in zip((w_in, w_out, w_up, w_down), layouts, ("w_in", "w_out", "w_up", "w_down"))]
    win_sh, cq_g, cs_g = _all_gather(shards[:1], layouts[:1], [conv_qkv_w[0], conv_sc_w[0]])
    late_send, late_recv, late_bufs, late_token = _gather_start(shards[1:], layouts[1:], cq_g)
    w_main, w_ab = _repack_w_in(win_sh, gw, heads, sw, late_token)
    conv_q = cq_g.transpose(1, 0, 2).reshape(conv_qkv_w.shape[1], -1)
    conv_s = cs_g.transpose(1, 0, 2).reshape(conv_sc_w.shape[1], -1)

    x2 = x.reshape(t, d)
    tgt2 = loss_target.reshape(t, d)
    xn = _norm_fwd(x2, norm_mix_pre)
    proj = _matmul(xn, w_main, "nn", [F32], "proj_main")
    proj_ab = _matmul(xn, w_ab, "nn", [F32], "proj_ab")
    proj3 = proj.reshape(bsz, seq, main)
    qkv_act = _qkv_conv_fwd(proj3, conv_q, 3 * gw).reshape(t, 3 * gw)
    a_log_pad, dt_pad = _pad_lanes(a_log), _pad_lanes(dt_bias)
    gcb, betab = _gates_fwd(proj_ab, a_log_pad, dt_pad, heads)
    o_raw, states = _gdn_fwd(qkv_act, gcb, betab, bsz, heads)
    gdn_out = _gdn_out_fwd(o_raw, proj, gdn_norm_w, heads, 3 * gw)
    sc_out = _sc_fwd(proj3, conv_s, 4 * gw, sw).reshape(t, sw)
    mixed = jnp.concatenate([gdn_out, sc_out], axis=1)
    late_bufs = _gather_wait(late_send, late_recv, late_bufs, layouts[1:], mixed)
    wout_f, wup_f, wdown_f = _forward_to_sibling(late_bufs, layouts[1:])
    mix = _matmul(mixed, wout_f, "nn", [F32], "mix_out")
    h, hn = _mid_fwd(x2, mix, norm_mix_post, norm_mlp_pre)

    def up_epilogue(acc):
        r = jnp.maximum(acc, 0.0)
        return r, r * r

    relu_up, hid = _matmul(hn, wup_f, "nn", [BF16, BF16], "mlp_up", epilogue=up_epilogue)
    ff = _matmul(hid, wdown_f, "nn", [F32], "mlp_down")
    loss_blk, dy, dff, dg_mlp_post = _head_fwd_bwd(h, ff, tgt2, norm_mlp_post)

    def dup_epilogue(acc, r):
        return (acc * (2.0 * r.astype(F32)),)

    d_up = _matmul(dff, wdown_f, "nt", [BF16], "d_hid", epilogue=dup_epilogue, extras=(relu_up,))
    dw_down = _matmul(hid, dff, "tn", [BF16], "dw_down")
    d_hn = _matmul(d_up, wup_f, "nt", [F32], "d_hn")
    dw_up = _matmul(hn, d_up, "tn", [BF16], "dw_up")
    mlp_grads, mlp_layouts = [dw_up, dw_down], [lay_up, lay_down]
    mlp_gots = _halves_to_sibling(mlp_grads, mlp_layouts, "mlp_grad_halves_to_sibling")
    mlp_parts = [_chip_sum(g, got, lay, core, f"chip_sum_mlp_{i}")
                 for i, (g, got, lay) in enumerate(zip(mlp_grads, mlp_gots, mlp_layouts))]
    mlp_send, mlp_recv, mlp_parts, mlp_lands, mlp_token = _partials_start(mlp_parts, dff)
    dh, dmix, dg_mlp_pre, dg_mix_post = _mid_bwd(d_hn, h, norm_mlp_pre, dy, mix, norm_mix_post, mlp_token)
    dmixed = _matmul(dmix, wout_f, "nt", [F32], "d_mixed")
    dw_out = _matmul(mixed, dmix, "tn", [BF16], "dw_out")
    dmixed3 = dmixed.reshape(bsz, seq, d)
    d_b, d_c, d_hsc, dw_conv_s = _sc_bwd(proj3, dmixed3, conv_s, 4 * gw, sw, gw)
    d_o, d_z, dg_gdn_norm = _gdn_out_bwd(dmixed, o_raw, proj, gdn_norm_w, heads, 3 * gw)
    dq, dk, dv, dgc_b, dbeta_b = _gdn_bwd(qkv_act, gcb, betab, states, d_o, bsz, heads)
    d_ab, d_alog, d_dt = _gates_bwd(proj_ab, a_log_pad, dt_pad, dgc_b, dbeta_b, heads)
    dact3 = jnp.concatenate([dq, dk, dv], axis=1).reshape(bsz, seq, 3 * gw)
    d_qkv, dw_conv_q = _qkv_conv_bwd(proj3, dact3, conv_q, 3 * gw)
    d_proj = jnp.concatenate([d_qkv.reshape(t, 3 * gw), d_z, d_b.reshape(t, sw), d_c.reshape(t, sw),
                              d_hsc.reshape(t, sw)], axis=1)
    d_xn = _matmul(d_proj, w_main, "nt", [F32], "d_xn_main")
    d_xn_ab = _matmul(d_ab, w_ab, "nt", [F32], "d_xn_ab")
    dw_main = _matmul(xn, d_proj, "tn", [BF16], "dw_in_main")
    dw_ab = _matmul(xn, d_ab, "tn", [BF16], "dw_in_ab")
    dw_in = _unpack_dw_in(dw_main, dw_ab, gw, heads, sw, ics)
    grad_x, dg_mix_pre = _first_bwd(d_xn, d_xn_ab, x2, norm_mix_pre, dh)

    mix_grads, mix_layouts = [dw_in, dw_out], [lay_in, lay_out]
    mix_gots = _halves_to_sibling(mix_grads, mix_layouts, "mix_grad_halves_to_sibling")
    mix_parts = [_chip_sum(g, got, lay, core, f"chip_sum_mix_{i}")
                 for i, (g, got, lay) in enumerate(zip(mix_grads, mix_gots, mix_layouts))]
    mix_recvs = _partials_to_owner(mix_parts)
    mlp_parts, mlp_recvs = _partials_wait(mlp_send, mlp_recv, mlp_parts, mlp_lands, grad_x)
    halves = [_shard_sum(p, r, chip, f"shard_sum_{i}")
              for i, (p, r) in enumerate(zip(list(mix_parts) + list(mlp_parts), list(mix_recvs) + list(mlp_recvs)))]
    sibling_halves = _halves_to_sibling_f32(halves)

    small, offs = _pack([loss_blk[0:1, 0:1], dg_mix_pre, dw_conv_q, d_alog[:, :heads], d_dt[:, :heads], dg_gdn_norm,
                         dw_conv_s, dg_mix_post, dg_mlp_pre, dg_mlp_post])
    (loss, g_mix_pre, g_conv_q_full, g_alog, g_dt, g_gdn_norm, g_conv_s_full, g_mix_post, g_mlp_pre,
     g_mlp_post) = _unpack(_small_all_reduce(small), offs)
    j = 2 * lax.axis_index("x") + lax.axis_index("y")
    cq_w, cs_w = conv_qkv_w.shape[-1], conv_sc_w.shape[-1]
    g_conv_q = lax.dynamic_slice_in_dim(g_conv_q_full, j * cq_w, cq_w, axis=1)
    g_conv_s = lax.dynamic_slice_in_dim(g_conv_s_full, j * cs_w, cs_w, axis=1)

    big = dict(zip((1, 7, 10, 11), zip(halves, sibling_halves)))
    grads = [g_mix_pre, None, g_conv_q, g_alog, g_dt, g_gdn_norm, g_conv_s, None, g_mix_post, g_mlp_pre, None,
             None, g_mlp_post]
    weights = [norm_mix_pre, w_in, conv_qkv_w, a_log, dt_bias, gdn_norm_w, conv_sc_w, w_out, norm_mix_post,
               norm_mlp_pre, w_up, w_down, norm_mlp_post]
    ms = [m_norm_mix_pre, m_w_in, m_conv_qkv_w, m_a_log, m_dt_bias, m_gdn_norm_w, m_conv_sc_w, m_w_out,
          m_norm_mix_post, m_norm_mlp_pre, m_w_up, m_w_down, m_norm_mlp_post]
    vs = [v_norm_mix_pre, v_w_in, v_conv_qkv_w, v_a_log, v_dt_bias, v_gdn_norm_w, v_conv_sc_w, v_w_out,
          v_norm_mix_post, v_norm_mlp_pre, v_w_up, v_w_down, v_norm_mlp_post]
    out_g, out_d, out_m, out_v = [], [], [], []
    for i, (wt, g, m, v) in enumerate(zip(weights, grads, ms, vs)):
        shape2 = wt.shape[-2:] if wt.ndim == 3 else wt.shape
        if i in big:
            g2, dl, nm, nv = _adamw_halves(wt.reshape(shape2), *big[i], m.reshape(shape2), v.reshape(shape2), core,
                                           f"adamw_{i}")
        else:
            g2 = g.reshape(shape2)
            dl, nm, nv = _adamw(wt.reshape(shape2), g2, m.reshape(shape2), v.reshape(shape2), f"adamw_{i}")
        out_g.append(g2.reshape(wt.shape))
        out_d.append(dl.reshape(wt.shape))
        out_m.append(nm.reshape(wt.shape))
        out_v.append(nv.reshape(wt.shape))

    return (loss.reshape(()), grad_x.reshape(bsz, seq, d), *out_g, *out_d, *out_m, *out_v)
```

```python
import functools

import jax
import jax.numpy as jnp
from jax import lax
from jax.experimental import pallas as pl
from jax.experimental.pallas import tpu as pltpu

CHUNK = 64
NORM_EPS = 1e-6
L2_EPS = 1e-6
N_CHIPS = 4
ADAM_LR = 0.001
ADAM_B1 = 0.9
ADAM_B2 = 0.999
ADAM_EPS = 1e-08
ADAM_WD = 0.01
ADAM_STEP = 10
LANES = 128
VMEM_LIMIT = 56 * 1024 * 1024

F32 = jnp.float32
BF16 = jnp.bfloat16
HI = lax.Precision.HIGH
EXACT_SUM = lax.Precision.HIGHEST
MESH = pl.DeviceIdType.MESH
ANY = pl.BlockSpec(memory_space=pl.ANY)


def _params(n_grid=0):
    return pltpu.CompilerParams(vmem_limit_bytes=VMEM_LIMIT)


def _tile(n, pref, align):
    if n <= pref:
        return n
    t = (pref // align) * align
    while t >= align:
        if n % t == 0:
            return t
        t -= align
    raise ValueError(f"no tile for {n}")


def _sigmoid(x):
    return 1.0 / (1.0 + jnp.exp(-x))


def _softplus(x):
    return jnp.maximum(x, 0.0) + jnp.log(1.0 + jnp.exp(-jnp.abs(x)))


def _rms_fwd(x, g):
    r = lax.rsqrt(jnp.mean(x * x, axis=-1, keepdims=True) + NORM_EPS)
    return x * r * g


def _rms_bwd(dy, x, g):
    r = lax.rsqrt(jnp.mean(x * x, axis=-1, keepdims=True) + NORM_EPS)
    xh = x * r
    dxh = dy * g
    dx = r * (dxh - xh * jnp.mean(dxh * xh, axis=-1, keepdims=True))
    dg = jnp.sum(dy * xh, axis=0, keepdims=True)
    return dx, dg


def _matmul(a, b, form, out_dtypes, name, epilogue=None, extras=(), after=None, tm=1024, tn=1024, tk=2048):
    if form == "nn":
        (m, kd), (_, n) = a.shape, b.shape
        dims = (((1,), (0,)), ((), ()))
    elif form == "nt":
        (m, kd), (n, _) = a.shape, b.shape
        dims = (((1,), (1,)), ((), ()))
    else:
        (kd, m), (_, n) = a.shape, b.shape
        dims = (((0,), (0,)), ((), ()))
    tm, tn, tk = _tile(m, tm, LANES), _tile(n, tn, LANES), _tile(kd, tk, LANES)
    nk = kd // tk
    n_extra = len(extras)
    n_out = len(out_dtypes)

    if form == "nn":
        a_spec = pl.BlockSpec((tm, tk), lambda i, j, k: (i, k))
        b_spec = pl.BlockSpec((tk, tn), lambda i, j, k: (k, j))
    elif form == "nt":
        a_spec = pl.BlockSpec((tm, tk), lambda i, j, k: (i, k))
        b_spec = pl.BlockSpec((tn, tk), lambda i, j, k: (j, k))
    else:
        a_spec = pl.BlockSpec((tk, tm), lambda i, j, k: (k, i))
        b_spec = pl.BlockSpec((tk, tn), lambda i, j, k: (k, j))
    tile_spec = pl.BlockSpec((tm, tn), lambda i, j, k: (i, j))

    order_only = [] if after is None else [after]
    n_skip = n_extra + len(order_only)

    def body(a_ref, b_ref, *rest):
        extra_refs = rest[:n_extra]
        out_refs = rest[n_skip:n_skip + n_out]
        acc_ref = rest[-1]
        k = pl.program_id(2)

        @pl.when(k == 0)
        def _():
            acc_ref[...] = jnp.zeros_like(acc_ref)

        acc_ref[...] += lax.dot_general(a_ref[...], b_ref[...], dims, preferred_element_type=F32)

        @pl.when(k == nk - 1)
        def _():
            acc = acc_ref[...]
            outs = (acc,) if epilogue is None else epilogue(acc, *[e[...] for e in extra_refs])
            for o_ref, val in zip(out_refs, outs):
                o_ref[...] = val.astype(o_ref.dtype)

    outs = pl.pallas_call(
        body, name=name, grid=(m // tm, n // tn, nk),
        in_specs=[a_spec, b_spec] + [tile_spec] * n_extra + [ANY] * len(order_only),
        out_specs=[tile_spec] * n_out,
        out_shape=[jax.ShapeDtypeStruct((m, n), dt) for dt in out_dtypes],
        scratch_shapes=[pltpu.VMEM((tm, tn), F32)],
        compiler_params=_params(),
    )(a, b, *extras, *order_only)
    return outs[0] if n_out == 1 else outs


def _cast_into_layout(w, layout, chip, name):
    r, c = w.shape
    tr = _tile(r, 256, 16)

    def body(chip_ref, w_ref, o_ref):
        o_ref[...] = w_ref[...].astype(BF16)

    return pl.pallas_call(
        body, name=name,
        grid_spec=pltpu.PrefetchScalarGridSpec(
            num_scalar_prefetch=1, grid=(r // tr,),
            in_specs=[pl.BlockSpec((tr, c), lambda i, chip_ref: (i, 0))],
            out_specs=layout.block_spec(tr, lambda i, chip_ref: (chip_ref[0], i))),
        out_shape=jax.ShapeDtypeStruct(layout.full_shape(), BF16), compiler_params=_params(),
    )(chip, w)


def _in_segments(gw, heads, sw):
    main = 4 * gw
    return [(0, main, 0), (main + 2 * heads, 3 * sw, main), (main, 2 * heads, main + 3 * sw)]


def _pieces(seg_start, width, dst_start, ics):
    out = []
    g = seg_start
    while g < seg_start + width:
        j, cj = divmod(g, ics)
        wdt = min(ics - cj, seg_start + width - g)
        out.append((j, cj, dst_start + (g - seg_start), wdt))
        g += wdt
    return out


def _repack_w_in(w_sh, gw, heads, sw, after):
    ns, d, ics = w_sh.shape
    main = 4 * gw + 3 * sw
    tr = _tile(d, 128, 16)
    pieces = [p for seg in _in_segments(gw, heads, sw) for p in _pieces(*seg, ics)]

    def body(w_ref, after_ref, m_ref, ab_ref):
        ab_ref[...] = jnp.zeros_like(ab_ref)
        for j, cj, cd, wdt in pieces:
            if cd >= main:
                ab_ref[:, cd - main:cd - main + wdt] = w_ref[j, :, cj:cj + wdt]
            else:
                m_ref[:, cd:cd + wdt] = w_ref[j, :, cj:cj + wdt]

    return pl.pallas_call(
        body, name="repack_w_in", grid=(d // tr,),
        in_specs=[pl.BlockSpec((ns, tr, ics), lambda i: (0, i, 0)), ANY],
        out_specs=[pl.BlockSpec((tr, main), lambda i: (i, 0)), pl.BlockSpec((tr, LANES), lambda i: (i, 0))],
        out_shape=[jax.ShapeDtypeStruct((d, main), BF16), jax.ShapeDtypeStruct((d, LANES), BF16)],
        compiler_params=_params(),
    )(w_sh, after)


def _unpack_dw_in(dw_main, dw_ab, gw, heads, sw, ics):
    d = dw_main.shape[0]
    tr = _tile(d, 128, 16)
    main = 4 * gw + 3 * sw
    pieces = [p for seg in _in_segments(gw, heads, sw) for p in _pieces(*seg, ics)]

    def body(m_ref, ab_ref, o_ref):
        for j, cj, cd, wdt in pieces:
            if cd >= main:
                o_ref[j, :, cj:cj + wdt] = ab_ref[:, cd - main:cd - main + wdt].astype(BF16)
            else:
                o_ref[j, :, cj:cj + wdt] = m_ref[:, cd:cd + wdt].astype(BF16)

    return pl.pallas_call(
        body, name="unpack_dw_in", grid=(d // tr,),
        in_specs=[pl.BlockSpec((tr, main), lambda i: (i, 0)), pl.BlockSpec((tr, LANES), lambda i: (i, 0))],
        out_specs=pl.BlockSpec((N_CHIPS, tr, ics), lambda i: (0, i, 0)),
        out_shape=jax.ShapeDtypeStruct((N_CHIPS, d, ics), BF16), compiler_params=_params(),
    )(dw_main, dw_ab)


def _adamw(w, g, m, v, name):
    r, c = w.shape
    tr = _tile(r, 128, 8)
    c1 = 1.0 - ADAM_B1 ** ADAM_STEP
    c2 = 1.0 - ADAM_B2 ** ADAM_STEP

    def body(w_ref, g_ref, m_ref, v_ref, d_ref, nm_ref, nv_ref):
        gg = g_ref[...]
        nm = ADAM_B1 * m_ref[...] + (1.0 - ADAM_B1) * gg
        nv = ADAM_B2 * v_ref[...] + (1.0 - ADAM_B2) * jnp.square(gg)
        m_hat = nm / c1
        v_hat = nv / c2
        d_ref[...] = -ADAM_LR * (m_hat / (jnp.sqrt(v_hat) + ADAM_EPS) + ADAM_WD * w_ref[...])
        nm_ref[...] = nm
        nv_ref[...] = nv

    spec = pl.BlockSpec((tr, c), lambda i: (i, 0))
    return pl.pallas_call(
        body, name=name, grid=(r // tr,), in_specs=[spec] * 4, out_specs=[spec] * 3,
        out_shape=[jax.ShapeDtypeStruct((r, c), F32)] * 3, compiler_params=_params(),
    )(w, g, m, v)


def _row_spec(tt, d):
    return pl.BlockSpec((tt, d), lambda i: (i, 0))


def _vec_spec(d):
    return pl.BlockSpec((1, d), lambda i: (0, 0))


def _norm_fwd(x, g):
    t, d = x.shape
    tt = _tile(t, 256, 16)

    def body(x_ref, g_ref, o_ref):
        o_ref[...] = _rms_fwd(x_ref[...], g_ref[...]).astype(BF16)

    return pl.pallas_call(
        body, name="norm_mix_pre", grid=(t // tt,), in_specs=[_row_spec(tt, d), _vec_spec(d)],
        out_specs=_row_spec(tt, d), out_shape=jax.ShapeDtypeStruct((t, d), BF16), compiler_params=_params(),
    )(x, g)


def _mid_fwd(x, mix, g_post, g_pre):
    t, d = x.shape
    tt = _tile(t, 128, 16)

    def body(x_ref, mix_ref, gp_ref, gn_ref, h_ref, hn_ref):
        h = x_ref[...] + _rms_fwd(mix_ref[...], gp_ref[...])
        h_ref[...] = h
        hn_ref[...] = _rms_fwd(h, gn_ref[...]).astype(BF16)

    return pl.pallas_call(
        body, name="mid_fwd", grid=(t // tt,),
        in_specs=[_row_spec(tt, d), _row_spec(tt, d), _vec_spec(d), _vec_spec(d)],
        out_specs=[_row_spec(tt, d), _row_spec(tt, d)],
        out_shape=[jax.ShapeDtypeStruct((t, d), F32), jax.ShapeDtypeStruct((t, d), BF16)],
        compiler_params=_params(),
    )(x, mix, g_post, g_pre)


def _head_fwd_bwd(h, ff, tgt, g_post):
    t, d = h.shape
    tt = _tile(t, 128, 16)

    def body(h_ref, ff_ref, t_ref, g_ref, loss_ref, dy_ref, dff_ref, dg_ref):
        i = pl.program_id(0)

        @pl.when(i == 0)
        def _():
            loss_ref[...] = jnp.zeros_like(loss_ref)
            dg_ref[...] = jnp.zeros_like(dg_ref)

        ff = ff_ref[...]
        g = g_ref[...]
        e = h_ref[...] + _rms_fwd(ff, g) - t_ref[...]
        loss_ref[...] += 0.5 * jnp.sum(jnp.mean(e * e, axis=-1, keepdims=True))
        dy = e * (1.0 / d)
        dy_ref[...] = dy
        dff, dg = _rms_bwd(dy, ff, g)
        dff_ref[...] = dff.astype(BF16)
        dg_ref[...] += dg

    return pl.pallas_call(
        body, name="loss_head", grid=(t // tt,),
        in_specs=[_row_spec(tt, d)] * 3 + [_vec_spec(d)],
        out_specs=[pl.BlockSpec((8, LANES), lambda i: (0, 0)), _row_spec(tt, d), _row_spec(tt, d), _vec_spec(d)],
        out_shape=[jax.ShapeDtypeStruct((8, LANES), F32), jax.ShapeDtypeStruct((t, d), F32),
                   jax.ShapeDtypeStruct((t, d), BF16), jax.ShapeDtypeStruct((1, d), F32)],
        compiler_params=_params(),
    )(h, ff, tgt, g_post)


def _mid_bwd(d_hn, h, g_pre, dy, mix, g_post, after):
    t, d = h.shape
    tt = _tile(t, 128, 16)

    def body(dhn_ref, h_ref, gn_ref, dy_ref, mix_ref, gp_ref, after_ref, dh_ref, dmix_ref, dgn_ref, dgp_ref):
        i = pl.program_id(0)

        @pl.when(i == 0)
        def _():
            dgn_ref[...] = jnp.zeros_like(dgn_ref)
            dgp_ref[...] = jnp.zeros_like(dgp_ref)

        dx, dgn = _rms_bwd(dhn_ref[...], h_ref[...], gn_ref[...])
        dh = dy_ref[...] + dx
        dh_ref[...] = dh
        dmix, dgp = _rms_bwd(dh, mix_ref[...], gp_ref[...])
        dmix_ref[...] = dmix.astype(BF16)
        dgn_ref[...] += dgn
        dgp_ref[...] += dgp

    return pl.pallas_call(
        body, name="mid_bwd", grid=(t // tt,),
        in_specs=[_row_spec(tt, d), _row_spec(tt, d), _vec_spec(d), _row_spec(tt, d), _row_spec(tt, d), _vec_spec(d),
                  ANY],
        out_specs=[_row_spec(tt, d), _row_spec(tt, d), _vec_spec(d), _vec_spec(d)],
        out_shape=[jax.ShapeDtypeStruct((t, d), F32), jax.ShapeDtypeStruct((t, d), BF16),
                   jax.ShapeDtypeStruct((1, d), F32), jax.ShapeDtypeStruct((1, d), F32)],
        compiler_params=_params(),
    )(d_hn, h, g_pre, dy, mix, g_post, after)


def _first_bwd(d_xn, d_xn_ab, x, g, dh):
    t, d = x.shape
    tt = _tile(t, 128, 16)

    def body(a_ref, b_ref, x_ref, g_ref, dh_ref, dx_ref, dg_ref):
        i = pl.program_id(0)

        @pl.when(i == 0)
        def _():
            dg_ref[...] = jnp.zeros_like(dg_ref)

        dx, dg = _rms_bwd(a_ref[...] + b_ref[...], x_ref[...], g_ref[...])
        dx_ref[...] = dh_ref[...] + dx
        dg_ref[...] += dg

    return pl.pallas_call(
        body, name="first_bwd", grid=(t // tt,),
        in_specs=[_row_spec(tt, d), _row_spec(tt, d), _row_spec(tt, d), _vec_spec(d), _row_spec(tt, d)],
        out_specs=[_row_spec(tt, d), _vec_spec(d)],
        out_shape=[jax.ShapeDtypeStruct((t, d), F32), jax.ShapeDtypeStruct((1, d), F32)],
        compiler_params=_params(),
    )(d_xn, d_xn_ab, x, g, dh)


HALO = 8


def _cur(ts, tc, off):
    return pl.BlockSpec((1, ts, tc), lambda ci, b, s: (b, s, off + ci))


def _prev(ts, tc, off):
    return pl.BlockSpec((1, HALO, tc), lambda ci, b, s: (b, jnp.maximum(s * (ts // HALO) - 1, 0), off + ci))


def _next(ts, tc, off, seq):
    last = seq // HALO - 1
    return pl.BlockSpec((1, HALO, tc), lambda ci, b, s: (b, jnp.minimum((s + 1) * (ts // HALO), last), off + ci))


def _conv_w_spec(kw, tc):
    return pl.BlockSpec((kw, tc), lambda ci, b, s: (0, ci))


def _conv_taps(w, buf, kw, ts):
    acc = w[0:1, :] * buf[HALO - (kw - 1):HALO - (kw - 1) + ts, :]
    for j in range(1, kw):
        acc = acc + w[j:j + 1, :] * buf[HALO - (kw - 1) + j:HALO - (kw - 1) + j + ts, :]
    return acc


def _silu_grad(x):
    s = _sigmoid(x)
    return s * (1.0 + x * (1.0 - s))


def _qkv_conv_fwd(proj3, w, width):
    bsz, seq, _ = proj3.shape
    kw = w.shape[0]
    ts, tc = _tile(seq, 256, 8), _tile(width, 512, LANES)

    def body(u_ref, up_ref, w_ref, o_ref, buf):
        s = pl.program_id(2)
        buf[0:HALO, :] = jnp.where(s == 0, 0.0, up_ref[0])
        buf[HALO:HALO + ts, :] = u_ref[0]
        pre = _conv_taps(w_ref[...], buf, kw, ts)
        o_ref[0] = pre * _sigmoid(pre)

    return pl.pallas_call(
        body, name="qkv_conv_fwd", grid=(width // tc, bsz, seq // ts),
        in_specs=[_cur(ts, tc, 0), _prev(ts, tc, 0), _conv_w_spec(kw, tc)],
        out_specs=_cur(ts, tc, 0), out_shape=jax.ShapeDtypeStruct((bsz, seq, width), F32),
        scratch_shapes=[pltpu.VMEM((HALO + ts, tc), F32)], compiler_params=_params(),
    )(proj3, proj3, w)


def _qkv_conv_bwd(proj3, dact3, w, width):
    bsz, seq, _ = proj3.shape
    kw = w.shape[0]
    ts, tc = _tile(seq, 256, 8), _tile(width, 512, LANES)
    n_s = seq // ts

    def body(u_ref, up_ref, un_ref, d_ref, dn_ref, w_ref, du_ref, dw_ref, buf, gbuf):
        b, s = pl.program_id(1), pl.program_id(2)

        @pl.when((b == 0) & (s == 0))
        def _():
            dw_ref[...] = jnp.zeros_like(dw_ref)

        w = w_ref[...]
        buf[0:HALO, :] = jnp.where(s == 0, 0.0, up_ref[0])
        buf[HALO:HALO + ts, :] = u_ref[0]
        buf[HALO + ts:, :] = un_ref[0]
        acc = w[0:1, :] * buf[HALO - (kw - 1):HALO - (kw - 1) + ts + HALO, :]
        for j in range(1, kw):
            acc = acc + w[j:j + 1, :] * buf[HALO - (kw - 1) + j:HALO - (kw - 1) + j + ts + HALO, :]
        gbuf[0:ts, :] = d_ref[0]
        gbuf[ts:, :] = jnp.where(s == n_s - 1, 0.0, dn_ref[0])
        gbuf[...] = gbuf[...] * _silu_grad(acc)
        g_cur = gbuf[0:ts, :]
        for j in range(kw):
            dw_ref[j:j + 1, :] += jnp.sum(g_cur * buf[HALO - (kw - 1) + j:HALO - (kw - 1) + j + ts, :], axis=0, keepdims=True)
        du = w[0:1, :] * gbuf[kw - 1:kw - 1 + ts, :]
        for j in range(1, kw):
            du = du + w[j:j + 1, :] * gbuf[kw - 1 - j:kw - 1 - j + ts, :]
        du_ref[0] = du.astype(BF16)

    return pl.pallas_call(
        body, name="qkv_conv_bwd", grid=(width // tc, bsz, n_s),
        in_specs=[_cur(ts, tc, 0), _prev(ts, tc, 0), _next(ts, tc, 0, seq), _cur(ts, tc, 0), _next(ts, tc, 0, seq),
                  _conv_w_spec(kw, tc)],
        out_specs=[_cur(ts, tc, 0), _conv_w_spec(kw, tc)],
        out_shape=[jax.ShapeDtypeStruct((bsz, seq, width), BF16), jax.ShapeDtypeStruct((kw, width), F32)],
        scratch_shapes=[pltpu.VMEM((HALO + ts + HALO, tc), F32), pltpu.VMEM((ts + HALO, tc), F32)],
        compiler_params=_params(),
    )(proj3, proj3, proj3, dact3, dact3, w)


def _sc_fwd(proj3, w, off, sw):
    bsz, seq, _ = proj3.shape
    kw = w.shape[0]
    ts, tc = _tile(seq, 256, 8), _tile(sw, 512, LANES)
    ob, oc, oh = off // tc, (off + sw) // tc, (off + 2 * sw) // tc

    def body(b_ref, c_ref, cp_ref, h_ref, hp_ref, w_ref, o_ref, buf):
        s = pl.program_id(2)
        buf[0:HALO, :] = jnp.where(s == 0, 0.0, cp_ref[0] * hp_ref[0])
        buf[HALO:HALO + ts, :] = c_ref[0] * h_ref[0]
        o_ref[0] = (b_ref[0] * _conv_taps(w_ref[...], buf, kw, ts)).astype(BF16)

    return pl.pallas_call(
        body, name="sc_fwd", grid=(sw // tc, bsz, seq // ts),
        in_specs=[_cur(ts, tc, ob), _cur(ts, tc, oc), _prev(ts, tc, oc), _cur(ts, tc, oh), _prev(ts, tc, oh),
                  _conv_w_spec(kw, tc)],
        out_specs=_cur(ts, tc, 0), out_shape=jax.ShapeDtypeStruct((bsz, seq, sw), BF16),
        scratch_shapes=[pltpu.VMEM((HALO + ts, tc), F32)], compiler_params=_params(),
    )(proj3, proj3, proj3, proj3, proj3, w)


def _sc_bwd(proj3, dmixed3, w, off, sw, d_off):
    bsz, seq, _ = proj3.shape
    kw = w.shape[0]
    ts, tc = _tile(seq, 256, 8), _tile(sw, 512, LANES)
    n_s = seq // ts
    ob, oc, oh, od = off // tc, (off + sw) // tc, (off + 2 * sw) // tc, d_off // tc

    def body(d_ref, dn_ref, b_ref, bn_ref, c_ref, cp_ref, h_ref, hp_ref, w_ref,
             db_ref, dc_ref, dh_ref, dw_ref, buf, gbuf):
        b, s = pl.program_id(1), pl.program_id(2)

        @pl.when((b == 0) & (s == 0))
        def _():
            dw_ref[...] = jnp.zeros_like(dw_ref)

        w = w_ref[...]
        cc, hh = c_ref[0], h_ref[0]
        buf[0:HALO, :] = jnp.where(s == 0, 0.0, cp_ref[0] * hp_ref[0])
        buf[HALO:HALO + ts, :] = cc * hh
        dout = d_ref[0]
        db_ref[0] = (dout * _conv_taps(w, buf, kw, ts)).astype(BF16)
        g_cur = dout * b_ref[0]
        gbuf[0:ts, :] = g_cur
        gbuf[ts:, :] = jnp.where(s == n_s - 1, 0.0, dn_ref[0] * bn_ref[0])
        for j in range(kw):
            dw_ref[j:j + 1, :] += jnp.sum(g_cur * buf[HALO - (kw - 1) + j:HALO - (kw - 1) + j + ts, :], axis=0, keepdims=True)
        dp = w[0:1, :] * gbuf[kw - 1:kw - 1 + ts, :]
        for j in range(1, kw):
            dp = dp + w[j:j + 1, :] * gbuf[kw - 1 - j:kw - 1 - j + ts, :]
        dc_ref[0] = (dp * hh).astype(BF16)
        dh_ref[0] = (dp * cc).astype(BF16)

    out = jax.ShapeDtypeStruct((bsz, seq, sw), BF16)
    return pl.pallas_call(
        body, name="sc_bwd", grid=(sw // tc, bsz, n_s),
        in_specs=[_cur(ts, tc, od), _next(ts, tc, od, seq), _cur(ts, tc, ob), _next(ts, tc, ob, seq),
                  _cur(ts, tc, oc), _prev(ts, tc, oc), _cur(ts, tc, oh), _prev(ts, tc, oh), _conv_w_spec(kw, tc)],
        out_specs=[_cur(ts, tc, 0)] * 3 + [_conv_w_spec(kw, tc)],
        out_shape=[out, out, out, jax.ShapeDtypeStruct((kw, sw), F32)],
        scratch_shapes=[pltpu.VMEM((HALO + ts, tc), F32), pltpu.VMEM((ts + HALO, tc), F32)],
        compiler_params=_params(),
    )(dmixed3, dmixed3, proj3, proj3, proj3, proj3, proj3, proj3, w)


def _tri_ones(lower):
    i = lax.broadcasted_iota(jnp.int32, (CHUNK, CHUNK), 0)
    j = lax.broadcasted_iota(jnp.int32, (CHUNK, CHUNK), 1)
    return jnp.where((i >= j) if lower else (j >= i), 1.0, 0.0).astype(F32)


def _gates_fwd(proj_ab, a_log_pad, dt_pad, heads):
    t = proj_ab.shape[0]
    gw = heads * LANES

    def body(ab_ref, al_ref, dt_ref, gc_ref, beta_ref):
        ab = ab_ref[...]
        g = -jnp.exp(al_ref[...]) * _softplus(ab + dt_ref[...])
        gc = jnp.dot(_tri_ones(True), g, precision=EXACT_SUM, preferred_element_type=F32)
        beta = _sigmoid(ab)
        for h in range(heads):
            gc_ref[:, h * LANES:(h + 1) * LANES] = jnp.broadcast_to(gc[:, h:h + 1], (CHUNK, LANES))
            beta_ref[:, h * LANES:(h + 1) * LANES] = jnp.broadcast_to(beta[:, heads + h:heads + h + 1], (CHUNK, LANES))

    return pl.pallas_call(
        body, name="gates_fwd", grid=(t // CHUNK,),
        in_specs=[_row_spec(CHUNK, LANES), _vec_spec(LANES), _vec_spec(LANES)],
        out_specs=[_row_spec(CHUNK, gw), _row_spec(CHUNK, gw)],
        out_shape=[jax.ShapeDtypeStruct((t, gw), F32)] * 2, compiler_params=_params(),
    )(proj_ab, a_log_pad, dt_pad)


def _gates_bwd(proj_ab, a_log_pad, dt_pad, dgc_b, dbeta_b, heads):
    t = proj_ab.shape[0]
    gw = heads * LANES

    def body(ab_ref, al_ref, dt_ref, dgc_ref, dbeta_ref, dab_ref, dal_ref, ddt_ref):
        i = pl.program_id(0)

        @pl.when(i == 0)
        def _():
            dal_ref[...] = jnp.zeros_like(dal_ref)
            ddt_ref[...] = jnp.zeros_like(ddt_ref)

        lane = lax.broadcasted_iota(jnp.int32, (CHUNK, LANES), 1)
        dgc = jnp.zeros((CHUNK, LANES), F32)
        dbeta = jnp.zeros((CHUNK, LANES), F32)
        for h in range(heads):
            dgc = jnp.where(lane == h, dgc_ref[:, h * LANES:(h + 1) * LANES], dgc)
            dbeta = jnp.where(lane == heads + h, dbeta_ref[:, h * LANES:(h + 1) * LANES], dbeta)
        dg = jnp.dot(_tri_ones(False), dgc, precision=EXACT_SUM, preferred_element_type=F32)
        ab = ab_ref[...]
        z = ab + dt_ref[...]
        ea = jnp.exp(al_ref[...])
        da = dg * (-ea) * _sigmoid(z)
        beta = _sigmoid(ab)
        db = dbeta * beta * (1.0 - beta)
        dab_ref[...] = jnp.where(lane < heads, da, jnp.where(lane < 2 * heads, db, 0.0)).astype(BF16)
        da_m = jnp.where(lane < heads, da, 0.0)
        ddt_ref[...] += jnp.sum(da_m, axis=0, keepdims=True)
        dal_ref[...] += jnp.sum(jnp.where(lane < heads, dg * (-ea) * _softplus(z), 0.0), axis=0, keepdims=True)

    return pl.pallas_call(
        body, name="gates_bwd", grid=(t // CHUNK,),
        in_specs=[_row_spec(CHUNK, LANES), _vec_spec(LANES), _vec_spec(LANES), _row_spec(CHUNK, gw), _row_spec(CHUNK, gw)],
        out_specs=[_row_spec(CHUNK, LANES), _vec_spec(LANES), _vec_spec(LANES)],
        out_shape=[jax.ShapeDtypeStruct((t, LANES), BF16), jax.ShapeDtypeStruct((1, LANES), F32),
                   jax.ShapeDtypeStruct((1, LANES), F32)],
        compiler_params=_params(),
    )(proj_ab, a_log_pad, dt_pad, dgc_b, dbeta_b)


def _dot(a, b, dims, hi=False):
    if hi:
        return lax.dot_general(a, b, (dims, ((), ())), precision=HI, preferred_element_type=F32)
    return lax.dot_general(a.astype(BF16), b.astype(BF16), (dims, ((), ())), preferred_element_type=F32)


NN = ((1,), (0,))
NT = ((1,), (1,))
TN = ((0,), (0,))


def _each(f, *lists):
    return [f(*xs) for xs in zip(*lists)]


def _dots(a, b, dims, hi=False):
    return _each(lambda x, y: _dot(x, y, dims, hi=hi), a, b)


def _unit_lower_inverse(ms):
    i = lax.broadcasted_iota(jnp.int32, (CHUNK, CHUNK), 0)
    j = lax.broadcasted_iota(jnp.int32, (CHUNK, CHUNK), 1)
    eye = jnp.where(i == j, 1.0, 0.0).astype(F32)
    ts = [eye - jnp.where(jnp.right_shift(i, 1) == jnp.right_shift(j, 1), m, 0.0) for m in ms]
    shift = 1
    while (1 << shift) < CHUNK:
        same_pair = jnp.right_shift(i, shift + 1) == jnp.right_shift(j, shift + 1)
        other_half = jnp.right_shift(i, shift) != jnp.right_shift(j, shift)
        offs = [jnp.where(same_pair & other_half, m, 0.0) for m in ms]
        corr = _dots(_dots(ts, offs, NN, hi=True), ts, NN, hi=True)
        ts = _each(lambda t, c: t - c, ts, corr)
        shift += 1
    return ts


def _chunk_local(qrs, krs, vs, gcbs, betabs, head_dim):
    i = lax.broadcasted_iota(jnp.int32, (CHUNK, CHUNK), 0)
    j = lax.broadcasted_iota(jnp.int32, (CHUNK, CHUNK), 1)
    scale = head_dim ** -0.5
    rqs = [lax.rsqrt(jnp.sum(q * q, axis=-1, keepdims=True) + L2_EPS) for q in qrs]
    rks = [lax.rsqrt(jnp.sum(k * k, axis=-1, keepdims=True) + L2_EPS) for k in krs]
    qhs = _each(lambda a, r: a * r, qrs, rqs)
    ks = _each(lambda a, r: a * r, krs, rks)
    qs = [a * scale for a in qhs]
    decays = [jnp.exp(jnp.where(i >= j, g[:, 0:CHUNK] - g.T[0:CHUNK, :], -jnp.inf)) for g in gcbs]
    kks = _dots(ks, ks, NT)
    qks = _dots(qs, ks, NT)
    ms = _each(lambda b, kk, d: jnp.where(i > j, b[:, 0:CHUNK] * kk * d, 0.0), betabs, kks, decays)
    tinvs = _unit_lower_inverse(ms)
    egs = [jnp.exp(g) for g in gcbs]
    rhs_ws = _each(lambda k, b, e: k * b * e, ks, betabs, egs)
    us = _dots(tinvs, _each(lambda v, b: v * b, vs, betabs), NN, hi=True)
    ws = _dots(tinvs, rhs_ws, NN, hi=True)
    out = []
    for h in range(len(qrs)):
        g_last = gcbs[h][CHUNK - 1:CHUNK, :]
        e_last = jnp.exp(g_last - gcbs[h])
        out.append(dict(rq=rqs[h], rk=rks[h], qh=qhs[h], q=qs[h], k=ks[h], decay=decays[h],
                        beta_col=betabs[h][:, 0:CHUNK], kk=kks[h], m=ms[h], tinv=tinvs[h], eg=egs[h], rhs_w=rhs_ws[h],
                        u=us[h], w=ws[h], p=qks[h] * decays[h], qd=qs[h] * egs[h], kd=ks[h] * e_last, e_last=e_last,
                        gl=jnp.exp(g_last), scale=scale, strict=i > j, incl=i >= j))
    return out


def _field(dicts, name):
    return [d[name] for d in dicts]


GDN_HEAD_GROUP = 8


def _gdn_specs(n_chunks, heads, reverse):
    hg = min(GDN_HEAD_GROUP, heads)
    assert heads % hg == 0

    def cidx(c):
        return (n_chunks - 1 - c) if reverse else c

    def tok(off):
        return pl.BlockSpec((CHUNK, hg * LANES), lambda b, h, c: (b * n_chunks + cidx(c), off // hg + h))

    state = pl.BlockSpec((None, hg, LANES, LANES), lambda b, h, c: (b * n_chunks + cidx(c), h, 0, 0))
    return hg, tok, state


def _gdn_fwd(qkv_act, gcb, betab, bsz, heads):
    t = qkv_act.shape[0]
    n_chunks = t // bsz // CHUNK
    hg, tok, state = _gdn_specs(n_chunks, heads, False)

    def body(q_ref, k_ref, v_ref, gc_ref, beta_ref, o_ref, s_ref, st):
        @pl.when(pl.program_id(2) == 0)
        def _():
            st[...] = jnp.zeros_like(st)

        sls = [slice(hh * LANES, (hh + 1) * LANES) for hh in range(hg)]
        loc = _chunk_local(*[[r[:, sl] for sl in sls] for r in (q_ref, k_ref, v_ref, gc_ref, beta_ref)], LANES)
        s0 = [st[hh] for hh in range(hg)]
        v_new = _each(lambda u, ws: u - ws, _field(loc, "u"), _dots(_field(loc, "w"), s0, NN))
        o_state = _dots(_field(loc, "qd"), s0, NN)
        o_local = _dots(_field(loc, "p"), v_new, NN)
        s_add = _dots(_field(loc, "kd"), v_new, TN)
        for hh in range(hg):
            o_ref[:, sls[hh]] = o_state[hh] + o_local[hh]
            s_ref[hh] = s0[hh]
            st[hh] = s0[hh] * loc[hh]["gl"] + s_add[hh]

    return pl.pallas_call(
        body, name="gdn_fwd", grid=(bsz, heads // hg, n_chunks),
        in_specs=[tok(0), tok(heads), tok(2 * heads), tok(0), tok(0)],
        out_specs=[tok(0), state],
        out_shape=[jax.ShapeDtypeStruct((t, heads * LANES), F32),
                   jax.ShapeDtypeStruct((bsz * n_chunks, heads, LANES, LANES), F32)],
        scratch_shapes=[pltpu.VMEM((hg, LANES, LANES), F32)], compiler_params=_params(),
    )(qkv_act, qkv_act, qkv_act, gcb, betab)


def _gdn_bwd(qkv_act, gcb, betab, states, d_o, bsz, heads):
    t = qkv_act.shape[0]
    n_chunks = t // bsz // CHUNK
    hg, tok, state = _gdn_specs(n_chunks, heads, True)

    def rowsum(a):
        return jnp.sum(a, axis=-1, keepdims=True)

    def finish_head(sl, L, v, betab, d_qd, d_kd, d_gl, d_p, d_m, d_rhs_u, d_rhs_w, d_q, d_k,
                    dq_ref, dk_ref, dv_ref, dgc_ref, dbeta_ref):
        k, decay = L["k"], L["decay"]
        dv_ref[:, sl] = betab * d_rhs_u
        e = d_m * L["m"] + d_p * L["p"]
        d_beta = rowsum(d_m * L["kk"] * decay) + rowsum(d_rhs_u * v) + rowsum(d_rhs_w * k * L["eg"])
        s_kd = rowsum(d_kd * L["kd"])
        d_gc = (rowsum(e) - rowsum(e.T) + rowsum(d_rhs_w * L["rhs_w"]) + rowsum(d_qd * L["qd"]) - s_kd)
        row = lax.broadcasted_iota(jnp.int32, (CHUNK, 1), 0)
        d_gc = d_gc + jnp.where(row == CHUNK - 1, jnp.sum(s_kd) + d_gl * jnp.sum(L["gl"][:, 0:1]), 0.0)
        dgc_ref[:, sl] = jnp.broadcast_to(d_gc, (CHUNK, LANES))
        dbeta_ref[:, sl] = jnp.broadcast_to(d_beta, (CHUNK, LANES))
        d_qh = d_q * L["scale"]
        dq_ref[:, sl] = L["rq"] * (d_qh - L["qh"] * rowsum(d_qh * L["qh"]))
        dk_ref[:, sl] = L["rk"] * (d_k - k * rowsum(d_k * k))

    def body(q_ref, k_ref, v_ref, gc_ref, beta_ref, s_ref, do_ref, dq_ref, dk_ref, dv_ref, dgc_ref, dbeta_ref, dst):
        @pl.when(pl.program_id(2) == 0)
        def _():
            dst[...] = jnp.zeros_like(dst)

        sls = [slice(hh * LANES, (hh + 1) * LANES) for hh in range(hg)]
        vs = [v_ref[:, sl] for sl in sls]
        betabs = [beta_ref[:, sl] for sl in sls]
        loc = _chunk_local([q_ref[:, sl] for sl in sls], [k_ref[:, sl] for sl in sls], vs,
                           [gc_ref[:, sl] for sl in sls], betabs, LANES)
        q, k, u, w, p, tinv, decay, qd, kd, eg = (_field(loc, n) for n in
                                                  ("q", "k", "u", "w", "p", "tinv", "decay", "qd", "kd", "eg"))
        s0 = [s_ref[hh] for hh in range(hg)]
        d_out = [do_ref[:, sl] for sl in sls]
        ds1 = [dst[hh] for hh in range(hg)]
        v_new = _each(lambda a, b: a - b, u, _dots(w, s0, NN))

        d_vnew = _each(lambda a, b: a + b, _dots(p, d_out, TN), _dots(kd, ds1, NN))
        d_qd = _dots(d_out, s0, NT)
        d_kd = _dots(v_new, ds1, NT)
        d_gl = _each(lambda a, b: jnp.sum(a * b), ds1, s0)
        d_p = _each(lambda L, a: jnp.where(L["incl"], a, 0.0), loc, _dots(d_out, v_new, NT))
        d_w = [-a for a in _dots(d_vnew, s0, NT)]
        ds_out, ds_vn = _dots(qd, d_out, TN), _dots(w, d_vnew, TN)
        for hh in range(hg):
            dst[hh] = ds_out[hh] + ds1[hh] * loc[hh]["gl"] - ds_vn[hh]

        d_rhs_u = _dots(tinv, d_vnew, TN, hi=True)
        d_rhs_w = _dots(tinv, d_w, TN, hi=True)
        d_a = _each(lambda a, b: -(a + b), _dots(d_rhs_u, u, NT, hi=True), _dots(d_rhs_w, w, NT, hi=True))
        d_m = _each(lambda L, a: jnp.where(L["strict"], a, 0.0), loc, d_a)
        g_kk = _each(lambda L, a: a * L["beta_col"] * L["decay"], loc, d_m)
        h_qk = _each(lambda a, d: a * d, d_p, decay)

        d_q = _each(lambda a, e, b: a + e * b, _dots(h_qk, k, NN), eg, d_qd)
        d_k = _each(lambda a, b, c, L, bb, rw, dk: a + b + c + bb * L["eg"] * rw + L["e_last"] * dk,
                    _dots(g_kk, k, NN), _dots(g_kk, k, TN), _dots(h_qk, q, TN), loc, betabs, d_rhs_w, d_kd)
        for hh in range(hg):
            finish_head(sls[hh], loc[hh], vs[hh], betabs[hh], d_qd[hh], d_kd[hh], d_gl[hh], d_p[hh], d_m[hh],
                        d_rhs_u[hh], d_rhs_w[hh], d_q[hh], d_k[hh], dq_ref, dk_ref, dv_ref, dgc_ref, dbeta_ref)

    tok_shape = jax.ShapeDtypeStruct((t, heads * LANES), F32)
    return pl.pallas_call(
        body, name="gdn_bwd", grid=(bsz, heads // hg, n_chunks),
        in_specs=[tok(0), tok(heads), tok(2 * heads), tok(0), tok(0), state, tok(0)],
        out_specs=[tok(0)] * 5,
        out_shape=[tok_shape] * 5,
        scratch_shapes=[pltpu.VMEM((hg, LANES, LANES), F32)], compiler_params=_params(),
    )(qkv_act, qkv_act, qkv_act, gcb, betab, states, d_o)


def _gdn_out_fwd(o, proj, gw_norm, heads, z_off):
    t = o.shape[0]
    ts = _tile(t, 512, 16)
    zb = z_off // LANES

    def body(o_ref, z_ref, w_ref, out_ref):
        z = z_ref[...]
        out_ref[...] = (_rms_fwd(o_ref[...], w_ref[...]) * (z * _sigmoid(z))).astype(BF16)

    return pl.pallas_call(
        body, name="gdn_out_fwd", grid=(t // ts, heads),
        in_specs=[pl.BlockSpec((ts, LANES), lambda i, h: (i, h)), pl.BlockSpec((ts, LANES), lambda i, h: (i, zb + h)),
                  pl.BlockSpec((1, LANES), lambda i, h: (0, 0))],
        out_specs=pl.BlockSpec((ts, LANES), lambda i, h: (i, h)),
        out_shape=jax.ShapeDtypeStruct((t, heads * LANES), BF16), compiler_params=_params(),
    )(o, proj, gw_norm)


def _gdn_out_bwd(dmixed, o, proj, gw_norm, heads, z_off, after):
    t = o.shape[0]
    ts = _tile(t, 512, 16)
    zb = z_off // LANES

    def body(d_ref, o_ref, z_ref, w_ref, after_ref, do_ref, dz_ref, dw_ref):
        @pl.when((pl.program_id(0) == 0) & (pl.program_id(1) == 0))
        def _():
            dw_ref[...] = jnp.zeros_like(dw_ref)

        d, oo, z, w = d_ref[...], o_ref[...], z_ref[...], w_ref[...]
        on = _rms_fwd(oo, w)
        dz_ref[...] = (d * on * _silu_grad(z)).astype(BF16)
        d_o, d_w = _rms_bwd(d * (z * _sigmoid(z)), oo, w)
        do_ref[...] = d_o
        dw_ref[...] += d_w

    blk = pl.BlockSpec((ts, LANES), lambda i, h: (i, h))
    vec = pl.BlockSpec((1, LANES), lambda i, h: (0, 0))
    return pl.pallas_call(
        body, name="gdn_out_bwd", grid=(t // ts, heads),
        in_specs=[blk, blk, pl.BlockSpec((ts, LANES), lambda i, h: (i, zb + h)), vec, ANY],
        out_specs=[blk, blk, vec],
        out_shape=[jax.ShapeDtypeStruct((t, heads * LANES), F32), jax.ShapeDtypeStruct((t, heads * LANES), BF16),
                   jax.ShapeDtypeStruct((1, LANES), F32)],
        compiler_params=_params(),
    )(dmixed, o, proj, gw_norm, after)


def _place():
    x, y, c = lax.axis_index("x"), lax.axis_index("y"), lax.axis_index("c")
    return x, y, c, [(1 - x, y), (x, 1 - y), (1 - x, 1 - y)]


def _aligned(start, align):
    return start if isinstance(start, int) else pl.multiple_of(start, align)


class _Layout:
    def __init__(self, kind, shard_shape):
        self.kind = kind
        self.r, self.c = shard_shape

    def full_shape(self):
        r, c = self.r, self.c
        return {"major": (N_CHIPS, r, c), "rows": (N_CHIPS * r, c), "cols": (r, N_CHIPS * c)}[self.kind]

    def region(self, ref, j, half=None):
        r, c = self.r, self.c
        r0, nr = (0, r) if half is None else (half * (r // 2), r // 2)
        if self.kind == "major":
            return ref.at[j, pl.ds(_aligned(r0, 16), nr), :]
        if self.kind == "rows":
            return ref.at[pl.ds(_aligned(j * r + r0, 16), nr), :]
        return ref.at[pl.ds(_aligned(r0, 16), nr), pl.ds(_aligned(j * c, LANES), c)]

    def block_spec(self, tr, where):
        r, c = self.r, self.c
        if self.kind == "major":
            return pl.BlockSpec((None, tr, c), lambda *a: (where(*a)[0], where(*a)[1], 0))
        if self.kind == "rows":
            return pl.BlockSpec((tr, c), lambda *a: (where(*a)[0] * (r // tr) + where(*a)[1], 0))
        return pl.BlockSpec((tr, c), lambda *a: (where(*a)[1], where(*a)[0]))


def _remote(src, dst, send_sem, recv_sem, dev):
    return pltpu.make_async_remote_copy(src_ref=src, dst_ref=dst, send_sem=send_sem, recv_sem=recv_sem,
                                        device_id=dev, device_id_type=MESH)


def _all_gather(big, layouts, small):
    nb, ns = len(big), len(small)
    n_remote = 6 * nb + 3 * ns

    def body(*refs):
        ins, outs = refs[:nb + ns], refs[nb + ns:2 * (nb + ns)]
        send_sems, recv_sems, local_sems = refs[2 * (nb + ns):]
        x, y, c, chips = _place()
        j = 2 * x + y
        local = []
        for i in range(ns):
            local.append(pltpu.make_async_copy(ins[nb + i], outs[nb + i].at[j], local_sems.at[i]))
        for cp in local:
            cp.start()
        sends = []
        for i in range(nb):
            for p, (px, py) in enumerate(chips):
                k = 3 * i + p
                mine = layouts[i].region(outs[i], j, c)
                sends.append(_remote(mine, mine, send_sems.at[k], recv_sems.at[k], (px, py, c)))
        for i in range(ns):
            for p, (px, py) in enumerate(chips):
                k = 6 * nb + 3 * i + p
                sends.append(_remote(ins[nb + i], outs[nb + i].at[j], send_sems.at[k], recv_sems.at[k], (px, py, c)))
        for cp in sends:
            cp.start()
        for i in range(nb):
            for p, (px, py) in enumerate(chips):
                k, jp = 3 * i + p, 2 * px + py
                got = layouts[i].region(outs[i], jp, c)
                _remote(got, got, send_sems.at[k], recv_sems.at[k], (px, py, c)).wait_recv()
                fwd = _remote(got, got, send_sems.at[3 * nb + k], recv_sems.at[3 * nb + k], (x, y, 1 - c))
                fwd.start()
                sends.append(fwd)
        for i in range(ns):
            for p, (px, py) in enumerate(chips):
                k, jp = 6 * nb + 3 * i + p, 2 * px + py
                _remote(ins[nb + i], outs[nb + i].at[jp], send_sems.at[k], recv_sems.at[k], (px, py, c)).wait_recv()
        for i in range(nb):
            for p, (px, py) in enumerate(chips):
                k, jp = 3 * nb + 3 * i + p, 2 * px + py
                got = layouts[i].region(outs[i], jp, 1 - c)
                _remote(got, got, send_sems.at[k], recv_sems.at[k], (x, y, 1 - c)).wait_recv()
        for cp in sends:
            cp.wait_send()
        for cp in local:
            cp.wait()

    out_shape = [jax.ShapeDtypeStruct(lay.full_shape(), BF16) for lay in layouts]
    out_shape += [jax.ShapeDtypeStruct((N_CHIPS,) + s.shape, F32) for s in small]
    return pl.pallas_call(
        body, name="all_gather_weights", in_specs=[ANY] * (nb + ns), out_specs=[ANY] * (nb + ns), out_shape=out_shape,
        input_output_aliases={i: i for i in range(nb)},
        scratch_shapes=[pltpu.SemaphoreType.DMA((n_remote,)), pltpu.SemaphoreType.DMA((n_remote,)),
                        pltpu.SemaphoreType.DMA((ns,))],
        compiler_params=_params(),
    )(*big, *small)


HBM = pl.BlockSpec(memory_space=pltpu.HBM)
SEM = pl.BlockSpec(memory_space=pltpu.SEMAPHORE)
SPLIT_COPY = pltpu.CompilerParams(has_side_effects=pltpu.SideEffectType.DATAFLOW_SIDE_EFFECTING)


def _in_hbm(a):
    return pltpu.with_memory_space_constraint(a, pltpu.HBM)


def _gather_start(bufs, layouts, after):
    nb = len(bufs)

    def body(*refs):
        send_sems, recv_sems = refs[nb + 1], refs[nb + 2]
        thru, token = refs[nb + 3:2 * nb + 3], refs[2 * nb + 3]
        x, y, c, chips = _place()
        j = 2 * x + y
        for i in range(nb):
            mine = layouts[i].region(thru[i], j, c)
            for p, (px, py) in enumerate(chips):
                _remote(mine, mine, send_sems.at[3 * i + p], recv_sems.at[3 * i + p], (px, py, c)).start()
        token[...] = jnp.zeros_like(token)

    outs = pl.pallas_call(
        body, name="gather_start", in_specs=[HBM] * nb + [ANY],
        out_specs=[SEM, SEM] + [HBM] * nb + [pl.BlockSpec(memory_space=pltpu.VMEM)],
        out_shape=[pltpu.SemaphoreType.DMA((3 * nb,)), pltpu.SemaphoreType.DMA((3 * nb,))]
        + [pltpu.HBM(b.shape, b.dtype) for b in bufs] + [jax.ShapeDtypeStruct((8, LANES), F32)],
        input_output_aliases={i: 2 + i for i in range(nb)}, compiler_params=SPLIT_COPY,
    )(*[_in_hbm(b) for b in bufs], after)
    return outs[0], outs[1], outs[2:2 + nb], outs[2 + nb]


def _gather_wait(send_sems, recv_sems, bufs, layouts, after):
    nb = len(bufs)

    def body(*refs):
        send_sems, recv_sems = refs[nb], refs[nb + 1]
        outs = refs[nb + 3:]
        x, y, c, chips = _place()
        j = 2 * x + y
        for i in range(nb):
            mine = layouts[i].region(outs[i], j, c)
            for p, (px, py) in enumerate(chips):
                cp = _remote(mine, layouts[i].region(outs[i], 2 * px + py, c), send_sems.at[3 * i + p],
                             recv_sems.at[3 * i + p], (px, py, c))
                cp.wait_send()
                cp.wait_recv()

    return pl.pallas_call(
        body, name="gather_wait", in_specs=[HBM] * nb + [SEM, SEM, ANY], out_specs=[HBM] * nb,
        out_shape=[pltpu.HBM(b.shape, b.dtype) for b in bufs],
        input_output_aliases={i: i for i in range(nb)}, compiler_params=SPLIT_COPY,
    )(*bufs, send_sems, recv_sems, after)


def _forward_to_sibling(bufs, layouts):
    nb = len(bufs)

    def body(*refs):
        outs = refs[nb:2 * nb]
        send_sems, recv_sems = refs[2 * nb:]
        x, y, c, chips = _place()
        cps = []
        for i in range(nb):
            for p, (px, py) in enumerate(chips):
                got = layouts[i].region(outs[i], 2 * px + py, c)
                cps.append(_remote(got, got, send_sems.at[3 * i + p], recv_sems.at[3 * i + p], (x, y, 1 - c)))
        for cp in cps:
            cp.start()
        for i in range(nb):
            for p, (px, py) in enumerate(chips):
                theirs = layouts[i].region(outs[i], 2 * px + py, 1 - c)
                _remote(theirs, theirs, send_sems.at[3 * i + p], recv_sems.at[3 * i + p], (x, y, 1 - c)).wait_recv()
        for cp in cps:
            cp.wait_send()

    return pl.pallas_call(
        body, name="gather_forward_to_sibling", in_specs=[ANY] * nb, out_specs=[ANY] * nb,
        out_shape=[jax.ShapeDtypeStruct(b.shape, b.dtype) for b in bufs],
        input_output_aliases={i: i for i in range(nb)},
        scratch_shapes=[pltpu.SemaphoreType.DMA((3 * nb,)), pltpu.SemaphoreType.DMA((3 * nb,))],
        compiler_params=_params(),
    )(*bufs)


def _halves_to_sibling(grads, layouts, name):
    nw = len(grads)

    def body(*refs):
        ins, gots = refs[:nw], refs[nw:2 * nw]
        send_sems, recv_sems = refs[2 * nw:]
        x, y, c, _ = _place()
        cps = []
        for i in range(nw):
            for j in range(N_CHIPS):
                k = N_CHIPS * i + j
                cps.append(_remote(layouts[i].region(ins[i], j, 1 - c), gots[i].at[j],
                                   send_sems.at[k], recv_sems.at[k], (x, y, 1 - c)))
        for cp in cps:
            cp.start()
        for cp in cps:
            cp.wait()

    half = [jax.ShapeDtypeStruct((N_CHIPS, lay.r // 2, lay.c), BF16) for lay in layouts]
    return pl.pallas_call(
        body, name=name, in_specs=[ANY] * nw, out_specs=[ANY] * nw, out_shape=half,
        scratch_shapes=[pltpu.SemaphoreType.DMA((N_CHIPS * nw,)), pltpu.SemaphoreType.DMA((N_CHIPS * nw,))],
        compiler_params=_params(),
    )(*grads)


def _chip_sum(grad, got, layout, core, name):
    n, hr, c = got.shape
    tr = _tile(hr, 256, 16)
    nb = hr // tr

    def body(core_ref, a_ref, b_ref, o_ref):
        o_ref[...] = (a_ref[...].astype(F32) + b_ref[...].astype(F32)).astype(BF16)

    spec = pl.BlockSpec((None, tr, c), lambda j, i, core_ref: (j, i, 0))
    return pl.pallas_call(
        body, name=name,
        grid_spec=pltpu.PrefetchScalarGridSpec(
            num_scalar_prefetch=1, grid=(n, nb),
            in_specs=[layout.block_spec(tr, lambda j, i, core_ref: (j, core_ref[0] * nb + i)), spec],
            out_specs=spec),
        out_shape=jax.ShapeDtypeStruct((n, hr, c), BF16), compiler_params=_params(),
    )(core, grad, got)


def _partials_start(parts, after, name):
    nw = len(parts)
    lands = [lax.empty((3,) + p.shape[1:], BF16) for p in parts]

    def body(*refs):
        send_sems, recv_sems = refs[2 * nw + 1], refs[2 * nw + 2]
        src, dst = refs[2 * nw + 3:3 * nw + 3], refs[3 * nw + 3:4 * nw + 3]
        token = refs[4 * nw + 3]
        x, y, c, chips = _place()
        for i in range(nw):
            for p, (px, py) in enumerate(chips):
                _remote(src[i].at[2 * px + py], dst[i].at[p], send_sems.at[3 * i + p], recv_sems.at[3 * i + p],
                        (px, py, c)).start()
        token[...] = jnp.zeros_like(token)

    outs = pl.pallas_call(
        body, name=name, in_specs=[HBM] * (2 * nw) + [ANY],
        out_specs=[SEM, SEM] + [HBM] * (2 * nw) + [pl.BlockSpec(memory_space=pltpu.VMEM)],
        out_shape=[pltpu.SemaphoreType.DMA((3 * nw,)), pltpu.SemaphoreType.DMA((3 * nw,))]
        + [pltpu.HBM(a.shape, a.dtype) for a in parts + lands] + [jax.ShapeDtypeStruct((8, LANES), F32)],
        input_output_aliases={i: 2 + i for i in range(2 * nw)}, compiler_params=SPLIT_COPY,
    )(*[_in_hbm(a) for a in parts + lands], after)
    return outs[0], outs[1], outs[2:2 + nw], outs[2 + nw:2 + 2 * nw], outs[2 + 2 * nw]


def _partials_wait(send_sems, recv_sems, parts, lands, after, name):
    nw = len(parts)

    def body(*refs):
        send_sems, recv_sems = refs[2 * nw], refs[2 * nw + 1]
        src, dst = refs[2 * nw + 3:3 * nw + 3], refs[3 * nw + 3:4 * nw + 3]
        x, y, c, chips = _place()
        for i in range(nw):
            for p, (px, py) in enumerate(chips):
                cp = _remote(src[i].at[2 * px + py], dst[i].at[p], send_sems.at[3 * i + p], recv_sems.at[3 * i + p],
                             (px, py, c))
                cp.wait_send()
                cp.wait_recv()

    outs = pl.pallas_call(
        body, name=name, in_specs=[HBM] * (2 * nw) + [SEM, SEM, ANY], out_specs=[HBM] * (2 * nw),
        out_shape=[pltpu.HBM(a.shape, a.dtype) for a in list(parts) + list(lands)],
        input_output_aliases={i: i for i in range(2 * nw)}, compiler_params=SPLIT_COPY,
    )(*parts, *lands, send_sems, recv_sems, after)
    return outs[:nw], outs[nw:]


def _shard_sum(parts, got, chip, name):
    _, r, c = parts.shape
    tr = _tile(r, 256, 16)

    def body(chip_ref, o_ref, g_ref, out_ref):
        acc = o_ref[...].astype(F32)
        for p in range(3):
            acc = acc + g_ref[p].astype(F32)
        out_ref[...] = acc

    return pl.pallas_call(
        body, name=name,
        grid_spec=pltpu.PrefetchScalarGridSpec(
            num_scalar_prefetch=1, grid=(r // tr,),
            in_specs=[pl.BlockSpec((None, tr, c), lambda i, chip_ref: (chip_ref[0], i, 0)),
                      pl.BlockSpec((3, tr, c), lambda i, chip_ref: (0, i, 0))],
            out_specs=pl.BlockSpec((tr, c), lambda i, chip_ref: (i, 0))),
        out_shape=jax.ShapeDtypeStruct((r, c), F32), compiler_params=_params(),
    )(chip, parts, got)


def _halves_to_sibling_f32(halves, name):
    nw = len(halves)

    def body(*refs):
        ins, outs = refs[:nw], refs[nw:2 * nw]
        send_sems, recv_sems = refs[2 * nw:]
        x, y, c, _ = _place()
        cps = [_remote(ins[i], outs[i], send_sems.at[i], recv_sems.at[i], (x, y, 1 - c)) for i in range(nw)]
        for cp in cps:
            cp.start()
        for cp in cps:
            cp.wait()

    return pl.pallas_call(
        body, name=name, in_specs=[ANY] * nw, out_specs=[ANY] * nw,
        out_shape=[jax.ShapeDtypeStruct(h.shape, F32) for h in halves],
        scratch_shapes=[pltpu.SemaphoreType.DMA((nw,)), pltpu.SemaphoreType.DMA((nw,))],
        compiler_params=_params(),
    )(*halves)


def _adamw_halves(w, mine, theirs, m, v, core, name):
    r, c = w.shape
    hr = r // 2
    tr = _tile(hr, 128, 8)
    nb = hr // tr
    c1 = 1.0 - ADAM_B1 ** ADAM_STEP
    c2 = 1.0 - ADAM_B2 ** ADAM_STEP

    def body(core_ref, w_ref, a_ref, b_ref, m_ref, v_ref, g_ref, d_ref, nm_ref, nv_ref):
        gg = jnp.where(pl.program_id(0) == core_ref[0], a_ref[...], b_ref[...])
        nm = ADAM_B1 * m_ref[...] + (1.0 - ADAM_B1) * gg
        nv = ADAM_B2 * v_ref[...] + (1.0 - ADAM_B2) * jnp.square(gg)
        m_hat = nm / c1
        v_hat = nv / c2
        g_ref[...] = gg
        d_ref[...] = -ADAM_LR * (m_hat / (jnp.sqrt(v_hat) + ADAM_EPS) + ADAM_WD * w_ref[...])
        nm_ref[...] = nm
        nv_ref[...] = nv

    full = pl.BlockSpec((tr, c), lambda hf, i, core_ref: (hf * nb + i, 0))
    half = pl.BlockSpec((tr, c), lambda hf, i, core_ref: (i, 0))
    return pl.pallas_call(
        body, name=name,
        grid_spec=pltpu.PrefetchScalarGridSpec(
            num_scalar_prefetch=1, grid=(2, nb), in_specs=[full, half, half, full, full], out_specs=[full] * 4),
        out_shape=[jax.ShapeDtypeStruct((r, c), F32)] * 4, compiler_params=_params(),
    )(core, w, mine, theirs, m, v)


def _small_all_reduce(buf):
    rows = buf.shape[0]
    n_dev = 8

    def body(b_ref, o_ref, gath, send_sems, recv_sems):
        x, y, c, _ = _place()
        me = 4 * x + 2 * y + c
        gath[me] = b_ref[...]
        cps = []
        for k in range(1, n_dev):
            px, py, pc = (x + (k >> 2)) % 2, (y + ((k >> 1) & 1)) % 2, (c + (k & 1)) % 2
            cps.append(_remote(b_ref, gath.at[me], send_sems.at[k - 1], recv_sems.at[k - 1], (px, py, pc)))
        for cp in cps:
            cp.start()
        for k in range(1, n_dev):
            px, py, pc = (x + (k >> 2)) % 2, (y + ((k >> 1) & 1)) % 2, (c + (k & 1)) % 2
            _remote(b_ref, gath.at[4 * px + 2 * py + pc], send_sems.at[k - 1], recv_sems.at[k - 1], (px, py, pc)).wait_recv()
        for cp in cps:
            cp.wait_send()
        acc = gath[0]
        for dev in range(1, n_dev):
            acc = acc + gath[dev]
        o_ref[...] = acc

    vm = pl.BlockSpec(memory_space=pltpu.VMEM)
    return pl.pallas_call(
        body, name="small_all_reduce", in_specs=[vm], out_specs=vm,
        out_shape=jax.ShapeDtypeStruct((rows, LANES), F32),
        scratch_shapes=[pltpu.VMEM((n_dev, rows, LANES), F32), pltpu.SemaphoreType.DMA((n_dev - 1,)),
                        pltpu.SemaphoreType.DMA((n_dev - 1,))],
        compiler_params=_params(),
    )(buf)


def _pad_lanes(v):
    return jnp.pad(v, ((0, 0), (0, LANES - v.shape[-1])))


def _pack(vectors):
    flat, offs, pos = [], [], 0
    for v in vectors:
        n = v.size
        n_pad = -(-n // LANES) * LANES
        flat.append(jnp.pad(v.reshape(-1), (0, n_pad - n)))
        offs.append((pos, n, v.shape))
        pos += n_pad
    total = -(-pos // (8 * LANES)) * 8 * LANES
    flat.append(jnp.zeros((total - pos,), F32))
    return jnp.concatenate(flat).reshape(-1, LANES), offs


def _unpack(buf, offs):
    flat = buf.reshape(-1)
    return [flat[pos:pos + n].reshape(shape) for pos, n, shape in offs]


def kernel(x, norm_mix_pre, w_in, conv_qkv_w, a_log, dt_bias, gdn_norm_w, conv_sc_w, w_out, norm_mix_post, norm_mlp_pre, w_up, w_down, norm_mlp_post, loss_target, m_norm_mix_pre, m_w_in, m_conv_qkv_w, m_a_log, m_dt_bias, m_gdn_norm_w, m_conv_sc_w, m_w_out, m_norm_mix_post, m_norm_mlp_pre, m_w_up, m_w_down, m_norm_mlp_post, v_norm_mix_pre, v_w_in, v_conv_qkv_w, v_a_log, v_dt_bias, v_gdn_norm_w, v_conv_sc_w, v_w_out, v_norm_mix_post, v_norm_mlp_pre, v_w_up, v_w_down, v_norm_mlp_post):
    bsz, seq, d = x.shape
    t = bsz * seq
    heads, head_dim = a_log.shape[-1], gdn_norm_w.shape[-1]
    assert head_dim == LANES and seq % CHUNK == 0
    gw = heads * head_dim
    sw = conv_sc_w.shape[-1] * N_CHIPS
    ics = w_in.shape[-1]
    main = 4 * gw + 3 * sw
    assert ics * N_CHIPS == main + 2 * heads and 2 * heads <= LANES

    lay_in = _Layout("major", w_in.shape[1:])
    lay_out = _Layout("rows", w_out.shape[1:])
    lay_up = _Layout("cols", w_up.shape[1:])
    lay_down = _Layout("rows", w_down.shape[1:])
    layouts = [lay_in, lay_out, lay_up, lay_down]
    chip = (2 * lax.axis_index("x") + lax.axis_index("y")).astype(jnp.int32).reshape(1)
    core = lax.axis_index("c").astype(jnp.int32).reshape(1)
    shards = [_cast_into_layout(w[0], lay, chip, f"cast_{n}")
              for w, lay, n in zip((w_in, w_out, w_up, w_down), layouts, ("w_in", "w_out", "w_up", "w_down"))]
    win_sh, cq_g, cs_g = _all_gather(shards[:1], layouts[:1], [conv_qkv_w[0], conv_sc_w[0]])
    late_send, late_recv, late_bufs, late_token = _gather_start(shards[1:], layouts[1:], cq_g)
    w_main, w_ab = _repack_w_in(win_sh, gw, heads, sw, late_token)
    conv_q = cq_g.transpose(1, 0, 2).reshape(conv_qkv_w.shape[1], -1)
    conv_s = cs_g.transpose(1, 0, 2).reshape(conv_sc_w.shape[1], -1)

    x2 = x.reshape(t, d)
    tgt2 = loss_target.reshape(t, d)
    xn = _norm_fwd(x2, norm_mix_pre)
    proj = _matmul(xn, w_main, "nn", [F32], "proj_main")
    proj_ab = _matmul(xn, w_ab, "nn", [F32], "proj_ab")
    proj3 = proj.reshape(bsz, seq, main)
    qkv_act = _qkv_conv_fwd(proj3, conv_q, 3 * gw).reshape(t, 3 * gw)
    a_log_pad, dt_pad = _pad_lanes(a_log), _pad_lanes(dt_bias)
    gcb, betab = _gates_fwd(proj_ab, a_log_pad, dt_pad, heads)
    o_raw, states = _gdn_fwd(qkv_act, gcb, betab, bsz, heads)
    gdn_out = _gdn_out_fwd(o_raw, proj, gdn_norm_w, heads, 3 * gw)
    sc_out = _sc_fwd(proj3, conv_s, 4 * gw, sw).reshape(t, sw)
    mixed = jnp.concatenate([gdn_out, sc_out], axis=1)
    late_bufs = _gather_wait(late_send, late_recv, late_bufs, layouts[1:], mixed)
    wout_f, wup_f, wdown_f = _forward_to_sibling(late_bufs, layouts[1:])
    mix = _matmul(mixed, wout_f, "nn", [F32], "mix_out")
    h, hn = _mid_fwd(x2, mix, norm_mix_post, norm_mlp_pre)

    def up_epilogue(acc):
        r = jnp.maximum(acc, 0.0)
        return r, r * r

    relu_up, hid = _matmul(hn, wup_f, "nn", [BF16, BF16], "mlp_up", epilogue=up_epilogue)
    ff = _matmul(hid, wdown_f, "nn", [F32], "mlp_down")
    loss_blk, dy, dff, dg_mlp_post = _head_fwd_bwd(h, ff, tgt2, norm_mlp_post)

    def dup_epilogue(acc, r):
        return (acc * (2.0 * r.astype(F32)),)

    d_up = _matmul(dff, wdown_f, "nt", [BF16], "d_hid", epilogue=dup_epilogue, extras=(relu_up,))
    dw_down = _matmul(hid, dff, "tn", [BF16], "dw_down")
    d_hn = _matmul(d_up, wup_f, "nt", [F32], "d_hn")
    dw_up = _matmul(hn, d_up, "tn", [BF16], "dw_up")
    mlp_grads, mlp_layouts = [dw_up, dw_down], [lay_up, lay_down]
    mlp_gots = _halves_to_sibling(mlp_grads, mlp_layouts, "mlp_grad_halves_to_sibling")
    mlp_parts = [_chip_sum(g, got, lay, core, f"chip_sum_mlp_{i}")
                 for i, (g, got, lay) in enumerate(zip(mlp_grads, mlp_gots, mlp_layouts))]
    mlp_send, mlp_recv, mlp_parts, mlp_lands, mlp_token = _partials_start(mlp_parts, dff, "mlp_partials_start")
    dh, dmix, dg_mlp_pre, dg_mix_post = _mid_bwd(d_hn, h, norm_mlp_pre, dy, mix, norm_mix_post, mlp_token)
    dmixed = _matmul(dmix, wout_f, "nt", [F32], "d_mixed")
    dw_out = _matmul(mixed, dmix, "tn", [BF16], "dw_out")
    out_gots = _halves_to_sibling([dw_out], [lay_out], "out_grad_halves_to_sibling")
    out_parts = [_chip_sum(dw_out, out_gots[0], lay_out, core, "chip_sum_out")]
    out_send, out_recv, out_parts, out_lands, out_token = _partials_start(out_parts, dmixed, "out_partials_start")
    dmixed3 = dmixed.reshape(bsz, seq, d)
    d_b, d_c, d_hsc, dw_conv_s = _sc_bwd(proj3, dmixed3, conv_s, 4 * gw, sw, gw)
    d_o, d_z, dg_gdn_norm = _gdn_out_bwd(dmixed, o_raw, proj, gdn_norm_w, heads, 3 * gw, out_token)
    dq, dk, dv, dgc_b, dbeta_b = _gdn_bwd(qkv_act, gcb, betab, states, d_o, bsz, heads)
    d_ab, d_alog, d_dt = _gates_bwd(proj_ab, a_log_pad, dt_pad, dgc_b, dbeta_b, heads)
    dact3 = jnp.concatenate([dq, dk, dv], axis=1).reshape(bsz, seq, 3 * gw)
    d_qkv, dw_conv_q = _qkv_conv_bwd(proj3, dact3, conv_q, 3 * gw)
    d_proj = jnp.concatenate([d_qkv.reshape(t, 3 * gw), d_z, d_b.reshape(t, sw), d_c.reshape(t, sw),
                              d_hsc.reshape(t, sw)], axis=1)
    dw_main = _matmul(xn, d_proj, "tn", [BF16], "dw_in_main")
    dw_ab = _matmul(xn, d_ab, "tn", [BF16], "dw_in_ab")
    dw_in = _unpack_dw_in(dw_main, dw_ab, gw, heads, sw, ics)
    in_gots = _halves_to_sibling([dw_in], [lay_in], "in_grad_halves_to_sibling")
    in_parts = [_chip_sum(dw_in, in_gots[0], lay_in, core, "chip_sum_in")]
    in_send, in_recv, in_parts, in_lands, in_token = _partials_start(in_parts, dw_in, "in_partials_start")
    d_xn = _matmul(d_proj, w_main, "nt", [F32], "d_xn_main", after=in_token)
    d_xn_ab = _matmul(d_ab, w_ab, "nt", [F32], "d_xn_ab")
    grad_x, dg_mix_pre = _first_bwd(d_xn, d_xn_ab, x2, norm_mix_pre, dh)

    def finish(parts, recvs, names, tag):
        halves = [_shard_sum(p, r, chip, f"shard_sum_{n}") for p, r, n in zip(parts, recvs, names)]
        return halves, _halves_to_sibling_f32(halves, f"{tag}_grad_join_halves")

    def adamw_big(wt, m, v, mine, theirs, name):
        return _adamw_halves(wt[0], mine, theirs, m[0], v[0], core, name)

    mlp_parts, mlp_recvs = _partials_wait(mlp_send, mlp_recv, mlp_parts, mlp_lands, grad_x, "mlp_partials_wait")
    mlp_halves, mlp_sibling = finish(mlp_parts, mlp_recvs, ("w_up", "w_down"), "mlp")
    res_up = adamw_big(w_up, m_w_up, v_w_up, mlp_halves[0], mlp_sibling[0], "adamw_w_up")
    res_down = adamw_big(w_down, m_w_down, v_w_down, mlp_halves[1], mlp_sibling[1], "adamw_w_down")
    out_parts, out_recvs = _partials_wait(out_send, out_recv, out_parts, out_lands, res_down[1], "out_partials_wait")
    in_parts, in_recvs = _partials_wait(in_send, in_recv, in_parts, in_lands, out_recvs[0], "in_partials_wait")
    mix_halves, mix_sibling = finish(list(in_parts) + list(out_parts), list(in_recvs) + list(out_recvs),
                                     ("w_in", "w_out"), "mix")
    res_in = adamw_big(w_in, m_w_in, v_w_in, mix_halves[0], mix_sibling[0], "adamw_w_in")
    res_out = adamw_big(w_out, m_w_out, v_w_out, mix_halves[1], mix_sibling[1], "adamw_w_out")

    small, offs = _pack([loss_blk[0:1, 0:1], dg_mix_pre, dw_conv_q, d_alog[:, :heads], d_dt[:, :heads], dg_gdn_norm,
                         dw_conv_s, dg_mix_post, dg_mlp_pre, dg_mlp_post])
    (loss, g_mix_pre, g_conv_q_full, g_alog, g_dt, g_gdn_norm, g_conv_s_full, g_mix_post, g_mlp_pre,
     g_mlp_post) = _unpack(_small_all_reduce(small), offs)
    j = 2 * lax.axis_index("x") + lax.axis_index("y")
    cq_w, cs_w = conv_qkv_w.shape[-1], conv_sc_w.shape[-1]
    g_conv_q = lax.dynamic_slice_in_dim(g_conv_q_full, j * cq_w, cq_w, axis=1)
    g_conv_s = lax.dynamic_slice_in_dim(g_conv_s_full, j * cs_w, cs_w, axis=1)

    big = {1: res_in, 7: res_out, 10: res_up, 11: res_down}
    grads = [g_mix_pre, None, g_conv_q, g_alog, g_dt, g_gdn_norm, g_conv_s, None, g_mix_post, g_mlp_pre, None,
             None, g_mlp_post]
    weights = [norm_mix_pre, w_in, conv_qkv_w, a_log, dt_bias, gdn_norm_w, conv_sc_w, w_out, norm_mix_post,
               norm_mlp_pre, w_up, w_down, norm_mlp_post]
    ms = [m_norm_mix_pre, m_w_in, m_conv_qkv_w, m_a_log, m_dt_bias, m_gdn_norm_w, m_conv_sc_w, m_w_out,
          m_norm_mix_post, m_norm_mlp_pre, m_w_up, m_w_down, m_norm_mlp_post]
    vs = [v_norm_mix_pre, v_w_in, v_conv_qkv_w, v_a_log, v_dt_bias, v_gdn_norm_w, v_conv_sc_w, v_w_out,
          v_norm_mix_post, v_norm_mlp_pre, v_w_up, v_w_down, v_norm_mlp_post]
    out_g, out_d, out_m, out_v = [], [], [], []
    for i, (wt, g, m, v) in enumerate(zip(weights, grads, ms, vs)):
        shape2 = wt.shape[-2:] if wt.ndim == 3 else wt.shape
        if i in big:
            g2, dl, nm, nv = big[i]
        else:
            g2 = g.reshape(shape2)
            dl, nm, nv = _adamw(wt.reshape(shape2), g2, m.reshape(shape2), v.reshape(shape2), f"adamw_{i}")
        out_g.append(g2.reshape(wt.shape))
        out_d.append(dl.reshape(wt.shape))
        out_m.append(nm.reshape(wt.shape))
        out_v.append(nv.reshape(wt.shape))

    return (loss.reshape(()), grad_x.reshape(bsz, seq, d), *out_g, *out_d, *out_m, *out_v)
```

```python
import functools

import jax
import jax.numpy as jnp
from jax import lax
from jax.experimental import pallas as pl
from jax.experimental.pallas import tpu as pltpu

CHUNK = 64
NORM_EPS = 1e-6
L2_EPS = 1e-6
N_CHIPS = 4
ADAM_LR = 0.001
ADAM_B1 = 0.9
ADAM_B2 = 0.999
ADAM_EPS = 1e-08
ADAM_WD = 0.01
ADAM_STEP = 10
LANES = 128
VMEM_LIMIT = 56 * 1024 * 1024

F32 = jnp.float32
BF16 = jnp.bfloat16
HI = lax.Precision.HIGH
EXACT_SUM = lax.Precision.HIGHEST
MESH = pl.DeviceIdType.MESH
ANY = pl.BlockSpec(memory_space=pl.ANY)


def _params(n_grid=0):
    return pltpu.CompilerParams(vmem_limit_bytes=VMEM_LIMIT)


def _tile(n, pref, align):
    if n <= pref:
        return n
    t = (pref // align) * align
    while t >= align:
        if n % t == 0:
            return t
        t -= align
    raise ValueError(f"no tile for {n}")


def _sigmoid(x):
    return 1.0 / (1.0 + jnp.exp(-x))


def _softplus(x):
    return jnp.maximum(x, 0.0) + jnp.log(1.0 + jnp.exp(-jnp.abs(x)))


def _rms_fwd(x, g):
    r = lax.rsqrt(jnp.mean(x * x, axis=-1, keepdims=True) + NORM_EPS)
    return x * r * g


def _rms_bwd(dy, x, g):
    r = lax.rsqrt(jnp.mean(x * x, axis=-1, keepdims=True) + NORM_EPS)
    xh = x * r
    dxh = dy * g
    dx = r * (dxh - xh * jnp.mean(dxh * xh, axis=-1, keepdims=True))
    dg = jnp.sum(dy * xh, axis=0, keepdims=True)
    return dx, dg


def _matmul(a, b, form, out_dtypes, name, epilogue=None, extras=(), after=None, tm=1024, tn=1024, tk=4096):
    if form == "nn":
        (m, kd), (_, n) = a.shape, b.shape
        dims = (((1,), (0,)), ((), ()))
    elif form == "nt":
        (m, kd), (n, _) = a.shape, b.shape
        dims = (((1,), (1,)), ((), ()))
    else:
        (kd, m), (_, n) = a.shape, b.shape
        dims = (((0,), (0,)), ((), ()))
    tm, tn, tk = _tile(m, tm, LANES), _tile(n, tn, LANES), _tile(kd, tk, LANES)
    nk = kd // tk
    n_extra = len(extras)
    n_out = len(out_dtypes)

    if form == "nn":
        a_spec = pl.BlockSpec((tm, tk), lambda i, j, k: (i, k))
        b_spec = pl.BlockSpec((tk, tn), lambda i, j, k: (k, j))
    elif form == "nt":
        a_spec = pl.BlockSpec((tm, tk), lambda i, j, k: (i, k))
        b_spec = pl.BlockSpec((tn, tk), lambda i, j, k: (j, k))
    else:
        a_spec = pl.BlockSpec((tk, tm), lambda i, j, k: (k, i))
        b_spec = pl.BlockSpec((tk, tn), lambda i, j, k: (k, j))
    tile_spec = pl.BlockSpec((tm, tn), lambda i, j, k: (i, j))

    order_only = [] if after is None else [after]
    n_skip = n_extra + len(order_only)

    def finish(acc, extra_refs, out_refs):
        outs = (acc,) if epilogue is None else epilogue(acc, *[e[...] for e in extra_refs])
        for o_ref, val in zip(out_refs, outs):
            o_ref[...] = val.astype(o_ref.dtype)

    def body(a_ref, b_ref, *rest):
        extra_refs = rest[:n_extra]
        out_refs = rest[n_skip:n_skip + n_out]
        if nk == 1:
            finish(lax.dot_general(a_ref[...], b_ref[...], dims, preferred_element_type=F32), extra_refs, out_refs)
            return
        acc_ref = rest[-1]
        k = pl.program_id(2)

        @pl.when(k == 0)
        def _():
            acc_ref[...] = jnp.zeros_like(acc_ref)

        acc_ref[...] += lax.dot_general(a_ref[...], b_ref[...], dims, preferred_element_type=F32)

        @pl.when(k == nk - 1)
        def _():
            finish(acc_ref[...], extra_refs, out_refs)

    outs = pl.pallas_call(
        body, name=name, grid=(m // tm, n // tn, nk),
        in_specs=[a_spec, b_spec] + [tile_spec] * n_extra + [ANY] * len(order_only),
        out_specs=[tile_spec] * n_out,
        out_shape=[jax.ShapeDtypeStruct((m, n), dt) for dt in out_dtypes],
        scratch_shapes=[pltpu.VMEM((tm, tn), F32)] if nk > 1 else [],
        compiler_params=_params(),
    )(a, b, *extras, *order_only)
    return outs[0] if n_out == 1 else outs


def _cast_into_layout(w, layout, chip, name):
    r, c = w.shape
    tr = _tile(r, 256, 16)

    def body(chip_ref, w_ref, o_ref):
        o_ref[...] = w_ref[...].astype(BF16)

    return pl.pallas_call(
        body, name=name,
        grid_spec=pltpu.PrefetchScalarGridSpec(
            num_scalar_prefetch=1, grid=(r // tr,),
            in_specs=[pl.BlockSpec((tr, c), lambda i, chip_ref: (i, 0))],
            out_specs=layout.block_spec(tr, lambda i, chip_ref: (chip_ref[0], i))),
        out_shape=jax.ShapeDtypeStruct(layout.full_shape(), BF16), compiler_params=_params(),
    )(chip, w)


def _in_segments(gw, heads, sw):
    main = 4 * gw
    return [(0, main, 0), (main + 2 * heads, 3 * sw, main), (main, 2 * heads, main + 3 * sw)]


def _pieces(seg_start, width, dst_start, ics):
    out = []
    g = seg_start
    while g < seg_start + width:
        j, cj = divmod(g, ics)
        wdt = min(ics - cj, seg_start + width - g)
        out.append((j, cj, dst_start + (g - seg_start), wdt))
        g += wdt
    return out


def _repack_w_in(w_sh, gw, heads, sw, after):
    ns, d, ics = w_sh.shape
    main = 4 * gw + 3 * sw
    tr = _tile(d, 128, 16)
    pieces = [p for seg in _in_segments(gw, heads, sw) for p in _pieces(*seg, ics)]

    def body(w_ref, after_ref, m_ref, ab_ref):
        ab_ref[...] = jnp.zeros_like(ab_ref)
        for j, cj, cd, wdt in pieces:
            if cd >= main:
                ab_ref[:, cd - main:cd - main + wdt] = w_ref[j, :, cj:cj + wdt]
            else:
                m_ref[:, cd:cd + wdt] = w_ref[j, :, cj:cj + wdt]

    return pl.pallas_call(
        body, name="repack_w_in", grid=(d // tr,),
        in_specs=[pl.BlockSpec((ns, tr, ics), lambda i: (0, i, 0)), ANY],
        out_specs=[pl.BlockSpec((tr, main), lambda i: (i, 0)), pl.BlockSpec((tr, LANES), lambda i: (i, 0))],
        out_shape=[jax.ShapeDtypeStruct((d, main), BF16), jax.ShapeDtypeStruct((d, LANES), BF16)],
        compiler_params=_params(),
    )(w_sh, after)


def _unpack_dw_in(dw_main, dw_ab, gw, heads, sw, ics):
    d = dw_main.shape[0]
    tr = _tile(d, 128, 16)
    main = 4 * gw + 3 * sw
    pieces = [p for seg in _in_segments(gw, heads, sw) for p in _pieces(*seg, ics)]

    def body(m_ref, ab_ref, o_ref):
        for j, cj, cd, wdt in pieces:
            if cd >= main:
                o_ref[j, :, cj:cj + wdt] = ab_ref[:, cd - main:cd - main + wdt].astype(BF16)
            else:
                o_ref[j, :, cj:cj + wdt] = m_ref[:, cd:cd + wdt].astype(BF16)

    return pl.pallas_call(
        body, name="unpack_dw_in", grid=(d // tr,),
        in_specs=[pl.BlockSpec((tr, main), lambda i: (i, 0)), pl.BlockSpec((tr, LANES), lambda i: (i, 0))],
        out_specs=pl.BlockSpec((N_CHIPS, tr, ics), lambda i: (0, i, 0)),
        out_shape=jax.ShapeDtypeStruct((N_CHIPS, d, ics), BF16), compiler_params=_params(),
    )(dw_main, dw_ab)


def _adamw(w, g, m, v, name):
    r, c = w.shape
    tr = _tile(r, 128, 8)
    c1 = 1.0 - ADAM_B1 ** ADAM_STEP
    c2 = 1.0 - ADAM_B2 ** ADAM_STEP

    def body(w_ref, g_ref, m_ref, v_ref, d_ref, nm_ref, nv_ref):
        gg = g_ref[...]
        nm = ADAM_B1 * m_ref[...] + (1.0 - ADAM_B1) * gg
        nv = ADAM_B2 * v_ref[...] + (1.0 - ADAM_B2) * jnp.square(gg)
        m_hat = nm / c1
        v_hat = nv / c2
        d_ref[...] = -ADAM_LR * (m_hat / (jnp.sqrt(v_hat) + ADAM_EPS) + ADAM_WD * w_ref[...])
        nm_ref[...] = nm
        nv_ref[...] = nv

    spec = pl.BlockSpec((tr, c), lambda i: (i, 0))
    return pl.pallas_call(
        body, name=name, grid=(r // tr,), in_specs=[spec] * 4, out_specs=[spec] * 3,
        out_shape=[jax.ShapeDtypeStruct((r, c), F32)] * 3, compiler_params=_params(),
    )(w, g, m, v)


def _row_spec(tt, d):
    return pl.BlockSpec((tt, d), lambda i: (i, 0))


def _vec_spec(d):
    return pl.BlockSpec((1, d), lambda i: (0, 0))


def _norm_fwd(x, g):
    t, d = x.shape
    tt = _tile(t, 256, 16)

    def body(x_ref, g_ref, o_ref):
        o_ref[...] = _rms_fwd(x_ref[...], g_ref[...]).astype(BF16)

    return pl.pallas_call(
        body, name="norm_mix_pre", grid=(t // tt,), in_specs=[_row_spec(tt, d), _vec_spec(d)],
        out_specs=_row_spec(tt, d), out_shape=jax.ShapeDtypeStruct((t, d), BF16), compiler_params=_params(),
    )(x, g)


def _mid_fwd(x, mix, g_post, g_pre):
    t, d = x.shape
    tt = _tile(t, 128, 16)

    def body(x_ref, mix_ref, gp_ref, gn_ref, h_ref, hn_ref):
        h = x_ref[...] + _rms_fwd(mix_ref[...], gp_ref[...])
        h_ref[...] = h
        hn_ref[...] = _rms_fwd(h, gn_ref[...]).astype(BF16)

    return pl.pallas_call(
        body, name="mid_fwd", grid=(t // tt,),
        in_specs=[_row_spec(tt, d), _row_spec(tt, d), _vec_spec(d), _vec_spec(d)],
        out_specs=[_row_spec(tt, d), _row_spec(tt, d)],
        out_shape=[jax.ShapeDtypeStruct((t, d), F32), jax.ShapeDtypeStruct((t, d), BF16)],
        compiler_params=_params(),
    )(x, mix, g_post, g_pre)


def _head_fwd_bwd(h, ff, tgt, g_post):
    t, d = h.shape
    tt = _tile(t, 128, 16)

    def body(h_ref, ff_ref, t_ref, g_ref, loss_ref, dy_ref, dff_ref, dg_ref):
        i = pl.program_id(0)

        @pl.when(i == 0)
        def _():
            loss_ref[...] = jnp.zeros_like(loss_ref)
            dg_ref[...] = jnp.zeros_like(dg_ref)

        ff = ff_ref[...]
        g = g_ref[...]
        e = h_ref[...] + _rms_fwd(ff, g) - t_ref[...]
        loss_ref[...] += 0.5 * jnp.sum(jnp.mean(e * e, axis=-1, keepdims=True))
        dy = e * (1.0 / d)
        dy_ref[...] = dy
        dff, dg = _rms_bwd(dy, ff, g)
        dff_ref[...] = dff.astype(BF16)
        dg_ref[...] += dg

    return pl.pallas_call(
        body, name="loss_head", grid=(t // tt,),
        in_specs=[_row_spec(tt, d)] * 3 + [_vec_spec(d)],
        out_specs=[pl.BlockSpec((8, LANES), lambda i: (0, 0)), _row_spec(tt, d), _row_spec(tt, d), _vec_spec(d)],
        out_shape=[jax.ShapeDtypeStruct((8, LANES), F32), jax.ShapeDtypeStruct((t, d), F32),
                   jax.ShapeDtypeStruct((t, d), BF16), jax.ShapeDtypeStruct((1, d), F32)],
        compiler_params=_params(),
    )(h, ff, tgt, g_post)


def _mid_bwd(d_hn, h, g_pre, dy, mix, g_post, after):
    t, d = h.shape
    tt = _tile(t, 128, 16)

    def body(dhn_ref, h_ref, gn_ref, dy_ref, mix_ref, gp_ref, after_ref, dh_ref, dmix_ref, dgn_ref, dgp_ref):
        i = pl.program_id(0)

        @pl.when(i == 0)
        def _():
            dgn_ref[...] = jnp.zeros_like(dgn_ref)
            dgp_ref[...] = jnp.zeros_like(dgp_ref)

        dx, dgn = _rms_bwd(dhn_ref[...], h_ref[...], gn_ref[...])
        dh = dy_ref[...] + dx
        dh_ref[...] = dh
        dmix, dgp = _rms_bwd(dh, mix_ref[...], gp_ref[...])
        dmix_ref[...] = dmix.astype(BF16)
        dgn_ref[...] += dgn
        dgp_ref[...] += dgp

    return pl.pallas_call(
        body, name="mid_bwd", grid=(t // tt,),
        in_specs=[_row_spec(tt, d), _row_spec(tt, d), _vec_spec(d), _row_spec(tt, d), _row_spec(tt, d), _vec_spec(d),
                  ANY],
        out_specs=[_row_spec(tt, d), _row_spec(tt, d), _vec_spec(d), _vec_spec(d)],
        out_shape=[jax.ShapeDtypeStruct((t, d), F32), jax.ShapeDtypeStruct((t, d), BF16),
                   jax.ShapeDtypeStruct((1, d), F32), jax.ShapeDtypeStruct((1, d), F32)],
        compiler_params=_params(),
    )(d_hn, h, g_pre, dy, mix, g_post, after)


def _first_bwd(d_xn, d_xn_ab, x, g, dh):
    t, d = x.shape
    tt = _tile(t, 128, 16)

    def body(a_ref, b_ref, x_ref, g_ref, dh_ref, dx_ref, dg_ref):
        i = pl.program_id(0)

        @pl.when(i == 0)
        def _():
            dg_ref[...] = jnp.zeros_like(dg_ref)

        dx, dg = _rms_bwd(a_ref[...] + b_ref[...], x_ref[...], g_ref[...])
        dx_ref[...] = dh_ref[...] + dx
        dg_ref[...] += dg

    return pl.pallas_call(
        body, name="first_bwd", grid=(t // tt,),
        in_specs=[_row_spec(tt, d), _row_spec(tt, d), _row_spec(tt, d), _vec_spec(d), _row_spec(tt, d)],
        out_specs=[_row_spec(tt, d), _vec_spec(d)],
        out_shape=[jax.ShapeDtypeStruct((t, d), F32), jax.ShapeDtypeStruct((1, d), F32)],
        compiler_params=_params(),
    )(d_xn, d_xn_ab, x, g, dh)


HALO = 8


def _cur(ts, tc, off):
    return pl.BlockSpec((1, ts, tc), lambda ci, b, s: (b, s, off + ci))


def _prev(ts, tc, off):
    return pl.BlockSpec((1, HALO, tc), lambda ci, b, s: (b, jnp.maximum(s * (ts // HALO) - 1, 0), off + ci))


def _next(ts, tc, off, seq):
    last = seq // HALO - 1
    return pl.BlockSpec((1, HALO, tc), lambda ci, b, s: (b, jnp.minimum((s + 1) * (ts // HALO), last), off + ci))


def _conv_w_spec(kw, tc):
    return pl.BlockSpec((kw, tc), lambda ci, b, s: (0, ci))


def _conv_taps(w, buf, kw, ts):
    acc = w[0:1, :] * buf[HALO - (kw - 1):HALO - (kw - 1) + ts, :]
    for j in range(1, kw):
        acc = acc + w[j:j + 1, :] * buf[HALO - (kw - 1) + j:HALO - (kw - 1) + j + ts, :]
    return acc


def _silu_grad(x):
    s = _sigmoid(x)
    return s * (1.0 + x * (1.0 - s))


def _qkv_conv_fwd(proj3, w, width):
    bsz, seq, _ = proj3.shape
    kw = w.shape[0]
    ts, tc = _tile(seq, 256, 8), _tile(width, 512, LANES)

    def body(u_ref, up_ref, w_ref, o_ref, buf):
        s = pl.program_id(2)
        buf[0:HALO, :] = jnp.where(s == 0, 0.0, up_ref[0])
        buf[HALO:HALO + ts, :] = u_ref[0]
        pre = _conv_taps(w_ref[...], buf, kw, ts)
        o_ref[0] = pre * _sigmoid(pre)

    return pl.pallas_call(
        body, name="qkv_conv_fwd", grid=(width // tc, bsz, seq // ts),
        in_specs=[_cur(ts, tc, 0), _prev(ts, tc, 0), _conv_w_spec(kw, tc)],
        out_specs=_cur(ts, tc, 0), out_shape=jax.ShapeDtypeStruct((bsz, seq, width), F32),
        scratch_shapes=[pltpu.VMEM((HALO + ts, tc), F32)], compiler_params=_params(),
    )(proj3, proj3, w)


def _qkv_conv_bwd(proj3, dparts, w, width):
    bsz, seq, _ = proj3.shape
    kw = w.shape[0]
    n_parts = len(dparts)
    part_w = width // n_parts
    ts, tc = _tile(seq, 256, 8), _tile(part_w, 512, LANES)
    n_s = seq // ts
    npt = part_w // tc
    last = seq // HALO - 1

    def part_cur(p):
        def index(ci, b, s):
            use = (ci // npt) == p
            return jnp.where(use, b, 0), jnp.where(use, s, 0), jnp.where(use, ci % npt, 0)
        return pl.BlockSpec((1, ts, tc), index)

    def part_next(p):
        def index(ci, b, s):
            use = (ci // npt) == p
            return (jnp.where(use, b, 0), jnp.where(use, jnp.minimum((s + 1) * (ts // HALO), last), 0),
                    jnp.where(use, ci % npt, 0))
        return pl.BlockSpec((1, HALO, tc), index)

    def body(u_ref, up_ref, un_ref, *rest):
        d_refs, dn_refs = rest[:n_parts], rest[n_parts:2 * n_parts]
        w_ref, du_ref, dw_ref, buf, gbuf = rest[2 * n_parts:]
        ci, b, s = pl.program_id(0), pl.program_id(1), pl.program_id(2)

        @pl.when((b == 0) & (s == 0))
        def _():
            dw_ref[...] = jnp.zeros_like(dw_ref)

        w = w_ref[...]
        buf[0:HALO, :] = jnp.where(s == 0, 0.0, up_ref[0])
        buf[HALO:HALO + ts, :] = u_ref[0]
        buf[HALO + ts:, :] = un_ref[0]
        acc = w[0:1, :] * buf[HALO - (kw - 1):HALO - (kw - 1) + ts + HALO, :]
        for j in range(1, kw):
            acc = acc + w[j:j + 1, :] * buf[HALO - (kw - 1) + j:HALO - (kw - 1) + j + ts + HALO, :]
        for p in range(n_parts):
            @pl.when(ci // npt == p)
            def _(p=p):
                gbuf[0:ts, :] = d_refs[p][0]
                gbuf[ts:, :] = jnp.where(s == n_s - 1, 0.0, dn_refs[p][0])

        gbuf[...] = gbuf[...] * _silu_grad(acc)
        g_cur = gbuf[0:ts, :]
        for j in range(kw):
            dw_ref[j:j + 1, :] += jnp.sum(g_cur * buf[HALO - (kw - 1) + j:HALO - (kw - 1) + j + ts, :], axis=0, keepdims=True)
        du = w[0:1, :] * gbuf[kw - 1:kw - 1 + ts, :]
        for j in range(1, kw):
            du = du + w[j:j + 1, :] * gbuf[kw - 1 - j:kw - 1 - j + ts, :]
        du_ref[0] = du.astype(BF16)

    return pl.pallas_call(
        body, name="qkv_conv_bwd", grid=(width // tc, bsz, n_s),
        in_specs=[_cur(ts, tc, 0), _prev(ts, tc, 0), _next(ts, tc, 0, seq)]
        + [part_cur(p) for p in range(n_parts)] + [part_next(p) for p in range(n_parts)] + [_conv_w_spec(kw, tc)],
        out_specs=[_cur(ts, tc, 0), _conv_w_spec(kw, tc)],
        out_shape=[jax.ShapeDtypeStruct((bsz, seq, width), BF16), jax.ShapeDtypeStruct((kw, width), F32)],
        scratch_shapes=[pltpu.VMEM((HALO + ts + HALO, tc), F32), pltpu.VMEM((ts + HALO, tc), F32)],
        compiler_params=_params(),
    )(proj3, proj3, proj3, *dparts, *dparts, w)


def _sc_fwd(proj3, w, off, sw):
    bsz, seq, _ = proj3.shape
    kw = w.shape[0]
    ts, tc = _tile(seq, 256, 8), _tile(sw, 512, LANES)
    ob, oc, oh = off // tc, (off + sw) // tc, (off + 2 * sw) // tc

    def body(b_ref, c_ref, cp_ref, h_ref, hp_ref, w_ref, o_ref, buf):
        s = pl.program_id(2)
        buf[0:HALO, :] = jnp.where(s == 0, 0.0, cp_ref[0] * hp_ref[0])
        buf[HALO:HALO + ts, :] = c_ref[0] * h_ref[0]
        o_ref[0] = (b_ref[0] * _conv_taps(w_ref[...], buf, kw, ts)).astype(BF16)

    return pl.pallas_call(
        body, name="sc_fwd", grid=(sw // tc, bsz, seq // ts),
        in_specs=[_cur(ts, tc, ob), _cur(ts, tc, oc), _prev(ts, tc, oc), _cur(ts, tc, oh), _prev(ts, tc, oh),
                  _conv_w_spec(kw, tc)],
        out_specs=_cur(ts, tc, 0), out_shape=jax.ShapeDtypeStruct((bsz, seq, sw), BF16),
        scratch_shapes=[pltpu.VMEM((HALO + ts, tc), F32)], compiler_params=_params(),
    )(proj3, proj3, proj3, proj3, proj3, w)


def _sc_bwd(proj3, dmixed3, w, off, sw, d_off):
    bsz, seq, _ = proj3.shape
    kw = w.shape[0]
    ts, tc = _tile(seq, 256, 8), _tile(sw, 512, LANES)
    n_s = seq // ts
    ob, oc, oh, od = off // tc, (off + sw) // tc, (off + 2 * sw) // tc, d_off // tc

    def body(d_ref, dn_ref, b_ref, bn_ref, c_ref, cp_ref, h_ref, hp_ref, w_ref,
             db_ref, dc_ref, dh_ref, dw_ref, buf, gbuf):
        b, s = pl.program_id(1), pl.program_id(2)

        @pl.when((b == 0) & (s == 0))
        def _():
            dw_ref[...] = jnp.zeros_like(dw_ref)

        w = w_ref[...]
        cc, hh = c_ref[0], h_ref[0]
        buf[0:HALO, :] = jnp.where(s == 0, 0.0, cp_ref[0] * hp_ref[0])
        buf[HALO:HALO + ts, :] = cc * hh
        dout = d_ref[0]
        db_ref[0] = (dout * _conv_taps(w, buf, kw, ts)).astype(BF16)
        g_cur = dout * b_ref[0]
        gbuf[0:ts, :] = g_cur
        gbuf[ts:, :] = jnp.where(s == n_s - 1, 0.0, dn_ref[0] * bn_ref[0])
        for j in range(kw):
            dw_ref[j:j + 1, :] += jnp.sum(g_cur * buf[HALO - (kw - 1) + j:HALO - (kw - 1) + j + ts, :], axis=0, keepdims=True)
        dp = w[0:1, :] * gbuf[kw - 1:kw - 1 + ts, :]
        for j in range(1, kw):
            dp = dp + w[j:j + 1, :] * gbuf[kw - 1 - j:kw - 1 - j + ts, :]
        dc_ref[0] = (dp * hh).astype(BF16)
        dh_ref[0] = (dp * cc).astype(BF16)

    out = jax.ShapeDtypeStruct((bsz, seq, sw), BF16)
    return pl.pallas_call(
        body, name="sc_bwd", grid=(sw // tc, bsz, n_s),
        in_specs=[_cur(ts, tc, od), _next(ts, tc, od, seq), _cur(ts, tc, ob), _next(ts, tc, ob, seq),
                  _cur(ts, tc, oc), _prev(ts, tc, oc), _cur(ts, tc, oh), _prev(ts, tc, oh), _conv_w_spec(kw, tc)],
        out_specs=[_cur(ts, tc, 0)] * 3 + [_conv_w_spec(kw, tc)],
        out_shape=[out, out, out, jax.ShapeDtypeStruct((kw, sw), F32)],
        scratch_shapes=[pltpu.VMEM((HALO + ts, tc), F32), pltpu.VMEM((ts + HALO, tc), F32)],
        compiler_params=_params(),
    )(dmixed3, dmixed3, proj3, proj3, proj3, proj3, proj3, proj3, w)


def _tri_ones(lower):
    i = lax.broadcasted_iota(jnp.int32, (CHUNK, CHUNK), 0)
    j = lax.broadcasted_iota(jnp.int32, (CHUNK, CHUNK), 1)
    return jnp.where((i >= j) if lower else (j >= i), 1.0, 0.0).astype(F32)


def _gates_fwd(proj_ab, a_log_pad, dt_pad, heads):
    t = proj_ab.shape[0]
    gw = heads * LANES

    def body(ab_ref, al_ref, dt_ref, gc_ref, beta_ref):
        ab = ab_ref[...]
        g = -jnp.exp(al_ref[...]) * _softplus(ab + dt_ref[...])
        gc = jnp.dot(_tri_ones(True), g, precision=EXACT_SUM, preferred_element_type=F32)
        beta = _sigmoid(ab)
        for h in range(heads):
            gc_ref[:, h * LANES:(h + 1) * LANES] = jnp.broadcast_to(gc[:, h:h + 1], (CHUNK, LANES))
            beta_ref[:, h * LANES:(h + 1) * LANES] = jnp.broadcast_to(beta[:, heads + h:heads + h + 1], (CHUNK, LANES))

    return pl.pallas_call(
        body, name="gates_fwd", grid=(t // CHUNK,),
        in_specs=[_row_spec(CHUNK, LANES), _vec_spec(LANES), _vec_spec(LANES)],
        out_specs=[_row_spec(CHUNK, gw), _row_spec(CHUNK, gw)],
        out_shape=[jax.ShapeDtypeStruct((t, gw), F32)] * 2, compiler_params=_params(),
    )(proj_ab, a_log_pad, dt_pad)


def _gates_bwd(proj_ab, a_log_pad, dt_pad, dgc_b, dbeta_b, heads):
    t = proj_ab.shape[0]
    gw = heads * LANES

    def body(ab_ref, al_ref, dt_ref, dgc_ref, dbeta_ref, dab_ref, dal_ref, ddt_ref):
        i = pl.program_id(0)

        @pl.when(i == 0)
        def _():
            dal_ref[...] = jnp.zeros_like(dal_ref)
            ddt_ref[...] = jnp.zeros_like(ddt_ref)

        lane = lax.broadcasted_iota(jnp.int32, (CHUNK, LANES), 1)
        dgc = jnp.zeros((CHUNK, LANES), F32)
        dbeta = jnp.zeros((CHUNK, LANES), F32)
        for h in range(heads):
            dgc = jnp.where(lane == h, dgc_ref[:, h * LANES:(h + 1) * LANES], dgc)
            dbeta = jnp.where(lane == heads + h, dbeta_ref[:, h * LANES:(h + 1) * LANES], dbeta)
        dg = jnp.dot(_tri_ones(False), dgc, precision=EXACT_SUM, preferred_element_type=F32)
        ab = ab_ref[...]
        z = ab + dt_ref[...]
        ea = jnp.exp(al_ref[...])
        da = dg * (-ea) * _sigmoid(z)
        beta = _sigmoid(ab)
        db = dbeta * beta * (1.0 - beta)
        dab_ref[...] = jnp.where(lane < heads, da, jnp.where(lane < 2 * heads, db, 0.0)).astype(BF16)
        da_m = jnp.where(lane < heads, da, 0.0)
        ddt_ref[...] += jnp.sum(da_m, axis=0, keepdims=True)
        dal_ref[...] += jnp.sum(jnp.where(lane < heads, dg * (-ea) * _softplus(z), 0.0), axis=0, keepdims=True)

    return pl.pallas_call(
        body, name="gates_bwd", grid=(t // CHUNK,),
        in_specs=[_row_spec(CHUNK, LANES), _vec_spec(LANES), _vec_spec(LANES), _row_spec(CHUNK, gw), _row_spec(CHUNK, gw)],
        out_specs=[_row_spec(CHUNK, LANES), _vec_spec(LANES), _vec_spec(LANES)],
        out_shape=[jax.ShapeDtypeStruct((t, LANES), BF16), jax.ShapeDtypeStruct((1, LANES), F32),
                   jax.ShapeDtypeStruct((1, LANES), F32)],
        compiler_params=_params(),
    )(proj_ab, a_log_pad, dt_pad, dgc_b, dbeta_b)


def _dot(a, b, dims, hi=False):
    if hi:
        return lax.dot_general(a, b, (dims, ((), ())), precision=HI, preferred_element_type=F32)
    return lax.dot_general(a.astype(BF16), b.astype(BF16), (dims, ((), ())), preferred_element_type=F32)


NN = ((1,), (0,))
NT = ((1,), (1,))
TN = ((0,), (0,))


def _each(f, *lists):
    return [f(*xs) for xs in zip(*lists)]


def _dots(a, b, dims, hi=False):
    return _each(lambda x, y: _dot(x, y, dims, hi=hi), a, b)


def _unit_lower_inverse(ms):
    i = lax.broadcasted_iota(jnp.int32, (CHUNK, CHUNK), 0)
    j = lax.broadcasted_iota(jnp.int32, (CHUNK, CHUNK), 1)
    eye = jnp.where(i == j, 1.0, 0.0).astype(F32)
    ts = [eye - jnp.where(jnp.right_shift(i, 1) == jnp.right_shift(j, 1), m, 0.0) for m in ms]
    shift = 1
    while (1 << shift) < CHUNK:
        same_pair = jnp.right_shift(i, shift + 1) == jnp.right_shift(j, shift + 1)
        other_half = jnp.right_shift(i, shift) != jnp.right_shift(j, shift)
        offs = [jnp.where(same_pair & other_half, m, 0.0) for m in ms]
        corr = _dots(_dots(ts, offs, NN, hi=True), ts, NN, hi=True)
        ts = _each(lambda t, c: t - c, ts, corr)
        shift += 1
    return ts


def _chunk_local(qrs, krs, vs, gcbs, betabs, head_dim):
    i = lax.broadcasted_iota(jnp.int32, (CHUNK, CHUNK), 0)
    j = lax.broadcasted_iota(jnp.int32, (CHUNK, CHUNK), 1)
    scale = head_dim ** -0.5
    rqs = [lax.rsqrt(jnp.sum(q * q, axis=-1, keepdims=True) + L2_EPS) for q in qrs]
    rks = [lax.rsqrt(jnp.sum(k * k, axis=-1, keepdims=True) + L2_EPS) for k in krs]
    qhs = _each(lambda a, r: a * r, qrs, rqs)
    ks = _each(lambda a, r: a * r, krs, rks)
    qs = [a * scale for a in qhs]
    decays = [jnp.exp(jnp.where(i >= j, g[:, 0:CHUNK] - g.T[0:CHUNK, :], -jnp.inf)) for g in gcbs]
    kks = _dots(ks, ks, NT)
    qks = _dots(qs, ks, NT)
    ms = _each(lambda b, kk, d: jnp.where(i > j, b[:, 0:CHUNK] * kk * d, 0.0), betabs, kks, decays)
    tinvs = _unit_lower_inverse(ms)
    egs = [jnp.exp(g) for g in gcbs]
    rhs_ws = _each(lambda k, b, e: k * b * e, ks, betabs, egs)
    us = _dots(tinvs, _each(lambda v, b: v * b, vs, betabs), NN, hi=True)
    ws = _dots(tinvs, rhs_ws, NN, hi=True)
    out = []
    for h in range(len(qrs)):
        g_last = gcbs[h][CHUNK - 1:CHUNK, :]
        e_last = jnp.exp(g_last - gcbs[h])
        out.append(dict(rq=rqs[h], rk=rks[h], qh=qhs[h], q=qs[h], k=ks[h], decay=decays[h],
                        beta_col=betabs[h][:, 0:CHUNK], kk=kks[h], m=ms[h], tinv=tinvs[h], eg=egs[h], rhs_w=rhs_ws[h],
                        u=us[h], w=ws[h], p=qks[h] * decays[h], qd=qs[h] * egs[h], kd=ks[h] * e_last, e_last=e_last,
                        gl=jnp.exp(g_last), scale=scale, strict=i > j, incl=i >= j))
    return out


def _field(dicts, name):
    return [d[name] for d in dicts]


GDN_HEAD_GROUP = 16


def _gdn_specs(n_chunks, heads, reverse):
    hg = min(GDN_HEAD_GROUP, heads)
    assert heads % hg == 0

    def cidx(c):
        return (n_chunks - 1 - c) if reverse else c

    def tok(off):
        return pl.BlockSpec((CHUNK, hg * LANES), lambda b, h, c: (b * n_chunks + cidx(c), off // hg + h))

    state = pl.BlockSpec((None, hg, LANES, LANES), lambda b, h, c: (b * n_chunks + cidx(c), h, 0, 0))
    return hg, tok, state


def _gdn_fwd(qkv_act, gcb, betab, bsz, heads):
    t = qkv_act.shape[0]
    n_chunks = t // bsz // CHUNK
    hg, tok, state = _gdn_specs(n_chunks, heads, False)

    def body(q_ref, k_ref, v_ref, gc_ref, beta_ref, o_ref, s_ref, st):
        @pl.when(pl.program_id(2) == 0)
        def _():
            st[...] = jnp.zeros_like(st)

        sls = [slice(hh * LANES, (hh + 1) * LANES) for hh in range(hg)]
        loc = _chunk_local(*[[r[:, sl] for sl in sls] for r in (q_ref, k_ref, v_ref, gc_ref, beta_ref)], LANES)
        s0 = [st[hh] for hh in range(hg)]
        v_new = _each(lambda u, ws: u - ws, _field(loc, "u"), _dots(_field(loc, "w"), s0, NN))
        o_state = _dots(_field(loc, "qd"), s0, NN)
        o_local = _dots(_field(loc, "p"), v_new, NN)
        s_add = _dots(_field(loc, "kd"), v_new, TN)
        for hh in range(hg):
            o_ref[:, sls[hh]] = o_state[hh] + o_local[hh]
            s_ref[hh] = s0[hh]
            st[hh] = s0[hh] * loc[hh]["gl"] + s_add[hh]

    return pl.pallas_call(
        body, name="gdn_fwd", grid=(bsz, heads // hg, n_chunks),
        in_specs=[tok(0), tok(heads), tok(2 * heads), tok(0), tok(0)],
        out_specs=[tok(0), state],
        out_shape=[jax.ShapeDtypeStruct((t, heads * LANES), F32),
                   jax.ShapeDtypeStruct((bsz * n_chunks, heads, LANES, LANES), F32)],
        scratch_shapes=[pltpu.VMEM((hg, LANES, LANES), F32)], compiler_params=_params(),
    )(qkv_act, qkv_act, qkv_act, gcb, betab)


def _gdn_bwd(qkv_act, gcb, betab, states, d_o, bsz, heads):
    t = qkv_act.shape[0]
    n_chunks = t // bsz // CHUNK
    hg, tok, state = _gdn_specs(n_chunks, heads, True)

    def rowsum(a):
        return jnp.sum(a, axis=-1, keepdims=True)

    def finish_head(sl, L, v, betab, d_qd, d_kd, d_gl, d_p, d_m, d_rhs_u, d_rhs_w, d_q, d_k,
                    dq_ref, dk_ref, dv_ref, dgc_ref, dbeta_ref):
        k, decay = L["k"], L["decay"]
        dv_ref[:, sl] = betab * d_rhs_u
        e = d_m * L["m"] + d_p * L["p"]
        d_beta = rowsum(d_m * L["kk"] * decay) + rowsum(d_rhs_u * v) + rowsum(d_rhs_w * k * L["eg"])
        s_kd = rowsum(d_kd * L["kd"])
        d_gc = (rowsum(e) - rowsum(e.T) + rowsum(d_rhs_w * L["rhs_w"]) + rowsum(d_qd * L["qd"]) - s_kd)
        row = lax.broadcasted_iota(jnp.int32, (CHUNK, 1), 0)
        d_gc = d_gc + jnp.where(row == CHUNK - 1, jnp.sum(s_kd) + d_gl * jnp.sum(L["gl"][:, 0:1]), 0.0)
        dgc_ref[:, sl] = jnp.broadcast_to(d_gc, (CHUNK, LANES))
        dbeta_ref[:, sl] = jnp.broadcast_to(d_beta, (CHUNK, LANES))
        d_qh = d_q * L["scale"]
        dq_ref[:, sl] = L["rq"] * (d_qh - L["qh"] * rowsum(d_qh * L["qh"]))
        dk_ref[:, sl] = L["rk"] * (d_k - k * rowsum(d_k * k))

    def body(q_ref, k_ref, v_ref, gc_ref, beta_ref, s_ref, do_ref, dq_ref, dk_ref, dv_ref, dgc_ref, dbeta_ref, dst):
        @pl.when(pl.program_id(2) == 0)
        def _():
            dst[...] = jnp.zeros_like(dst)

        sls = [slice(hh * LANES, (hh + 1) * LANES) for hh in range(hg)]
        vs = [v_ref[:, sl] for sl in sls]
        betabs = [beta_ref[:, sl] for sl in sls]
        loc = _chunk_local([q_ref[:, sl] for sl in sls], [k_ref[:, sl] for sl in sls], vs,
                           [gc_ref[:, sl] for sl in sls], betabs, LANES)
        q, k, u, w, p, tinv, decay, qd, kd, eg = (_field(loc, n) for n in
                                                  ("q", "k", "u", "w", "p", "tinv", "decay", "qd", "kd", "eg"))
        s0 = [s_ref[hh] for hh in range(hg)]
        d_out = [do_ref[:, sl] for sl in sls]
        ds1 = [dst[hh] for hh in range(hg)]
        v_new = _each(lambda a, b: a - b, u, _dots(w, s0, NN))

        d_vnew = _each(lambda a, b: a + b, _dots(p, d_out, TN), _dots(kd, ds1, NN))
        d_qd = _dots(d_out, s0, NT)
        d_kd = _dots(v_new, ds1, NT)
        d_gl = _each(lambda a, b: jnp.sum(a * b), ds1, s0)
        d_p = _each(lambda L, a: jnp.where(L["incl"], a, 0.0), loc, _dots(d_out, v_new, NT))
        d_w = [-a for a in _dots(d_vnew, s0, NT)]
        ds_out, ds_vn = _dots(qd, d_out, TN), _dots(w, d_vnew, TN)
        for hh in range(hg):
            dst[hh] = ds_out[hh] + ds1[hh] * loc[hh]["gl"] - ds_vn[hh]

        d_rhs_u = _dots(tinv, d_vnew, TN, hi=True)
        d_rhs_w = _dots(tinv, d_w, TN, hi=True)
        d_a = _each(lambda a, b: -(a + b), _dots(d_rhs_u, u, NT, hi=True), _dots(d_rhs_w, w, NT, hi=True))
        d_m = _each(lambda L, a: jnp.where(L["strict"], a, 0.0), loc, d_a)
        g_kk = _each(lambda L, a: a * L["beta_col"] * L["decay"], loc, d_m)
        h_qk = _each(lambda a, d: a * d, d_p, decay)

        d_q = _each(lambda a, e, b: a + e * b, _dots(h_qk, k, NN), eg, d_qd)
        d_k = _each(lambda a, b, c, L, bb, rw, dk: a + b + c + bb * L["eg"] * rw + L["e_last"] * dk,
                    _dots(g_kk, k, NN), _dots(g_kk, k, TN), _dots(h_qk, q, TN), loc, betabs, d_rhs_w, d_kd)
        for hh in range(hg):
            finish_head(sls[hh], loc[hh], vs[hh], betabs[hh], d_qd[hh], d_kd[hh], d_gl[hh], d_p[hh], d_m[hh],
                        d_rhs_u[hh], d_rhs_w[hh], d_q[hh], d_k[hh], dq_ref, dk_ref, dv_ref, dgc_ref, dbeta_ref)

    tok_shape = jax.ShapeDtypeStruct((t, heads * LANES), F32)
    return pl.pallas_call(
        body, name="gdn_bwd", grid=(bsz, heads // hg, n_chunks),
        in_specs=[tok(0), tok(heads), tok(2 * heads), tok(0), tok(0), state, tok(0)],
        out_specs=[tok(0)] * 5,
        out_shape=[tok_shape] * 5,
        scratch_shapes=[pltpu.VMEM((hg, LANES, LANES), F32)], compiler_params=_params(),
    )(qkv_act, qkv_act, qkv_act, gcb, betab, states, d_o)


def _gdn_out_fwd(o, proj, gw_norm, heads, z_off):
    t = o.shape[0]
    ts = _tile(t, 512, 16)
    zb = z_off // LANES

    def body(o_ref, z_ref, w_ref, out_ref):
        z = z_ref[...]
        out_ref[...] = (_rms_fwd(o_ref[...], w_ref[...]) * (z * _sigmoid(z))).astype(BF16)

    return pl.pallas_call(
        body, name="gdn_out_fwd", grid=(t // ts, heads),
        in_specs=[pl.BlockSpec((ts, LANES), lambda i, h: (i, h)), pl.BlockSpec((ts, LANES), lambda i, h: (i, zb + h)),
                  pl.BlockSpec((1, LANES), lambda i, h: (0, 0))],
        out_specs=pl.BlockSpec((ts, LANES), lambda i, h: (i, h)),
        out_shape=jax.ShapeDtypeStruct((t, heads * LANES), BF16), compiler_params=_params(),
    )(o, proj, gw_norm)


def _gdn_out_bwd(dmixed, o, proj, gw_norm, heads, z_off, after):
    t = o.shape[0]
    ts = _tile(t, 512, 16)
    zb = z_off // LANES

    def body(d_ref, o_ref, z_ref, w_ref, after_ref, do_ref, dz_ref, dw_ref):
        @pl.when((pl.program_id(0) == 0) & (pl.program_id(1) == 0))
        def _():
            dw_ref[...] = jnp.zeros_like(dw_ref)

        d, oo, z, w = d_ref[...], o_ref[...], z_ref[...], w_ref[...]
        on = _rms_fwd(oo, w)
        dz_ref[...] = (d * on * _silu_grad(z)).astype(BF16)
        d_o, d_w = _rms_bwd(d * (z * _sigmoid(z)), oo, w)
        do_ref[...] = d_o
        dw_ref[...] += d_w

    blk = pl.BlockSpec((ts, LANES), lambda i, h: (i, h))
    vec = pl.BlockSpec((1, LANES), lambda i, h: (0, 0))
    return pl.pallas_call(
        body, name="gdn_out_bwd", grid=(t // ts, heads),
        in_specs=[blk, blk, pl.BlockSpec((ts, LANES), lambda i, h: (i, zb + h)), vec, ANY],
        out_specs=[blk, blk, vec],
        out_shape=[jax.ShapeDtypeStruct((t, heads * LANES), F32), jax.ShapeDtypeStruct((t, heads * LANES), BF16),
                   jax.ShapeDtypeStruct((1, LANES), F32)],
        compiler_params=_params(),
    )(dmixed, o, proj, gw_norm, after)


def _place():
    x, y, c = lax.axis_index("x"), lax.axis_index("y"), lax.axis_index("c")
    return x, y, c, [(1 - x, y), (x, 1 - y), (1 - x, 1 - y)]


def _aligned(start, align):
    return start if isinstance(start, int) else pl.multiple_of(start, align)


class _Layout:
    def __init__(self, kind, shard_shape):
        self.kind = kind
        self.r, self.c = shard_shape

    def full_shape(self):
        r, c = self.r, self.c
        return {"major": (N_CHIPS, r, c), "rows": (N_CHIPS * r, c), "cols": (r, N_CHIPS * c)}[self.kind]

    def region(self, ref, j, half=None):
        r, c = self.r, self.c
        r0, nr = (0, r) if half is None else (half * (r // 2), r // 2)
        if self.kind == "major":
            return ref.at[j, pl.ds(_aligned(r0, 16), nr), :]
        if self.kind == "rows":
            return ref.at[pl.ds(_aligned(j * r + r0, 16), nr), :]
        return ref.at[pl.ds(_aligned(r0, 16), nr), pl.ds(_aligned(j * c, LANES), c)]

    def block_spec(self, tr, where):
        r, c = self.r, self.c
        if self.kind == "major":
            return pl.BlockSpec((None, tr, c), lambda *a: (where(*a)[0], where(*a)[1], 0))
        if self.kind == "rows":
            return pl.BlockSpec((tr, c), lambda *a: (where(*a)[0] * (r // tr) + where(*a)[1], 0))
        return pl.BlockSpec((tr, c), lambda *a: (where(*a)[1], where(*a)[0]))


def _remote(src, dst, send_sem, recv_sem, dev):
    return pltpu.make_async_remote_copy(src_ref=src, dst_ref=dst, send_sem=send_sem, recv_sem=recv_sem,
                                        device_id=dev, device_id_type=MESH)


def _all_gather(big, layouts, small):
    nb, ns = len(big), len(small)
    n_remote = 6 * nb + 3 * ns

    def body(*refs):
        ins, outs = refs[:nb + ns], refs[nb + ns:2 * (nb + ns)]
        send_sems, recv_sems, local_sems = refs[2 * (nb + ns):]
        x, y, c, chips = _place()
        j = 2 * x + y
        local = []
        for i in range(ns):
            local.append(pltpu.make_async_copy(ins[nb + i], outs[nb + i].at[j], local_sems.at[i]))
        for cp in local:
            cp.start()
        sends = []
        for i in range(nb):
            for p, (px, py) in enumerate(chips):
                k = 3 * i + p
                mine = layouts[i].region(outs[i], j, c)
                sends.append(_remote(mine, mine, send_sems.at[k], recv_sems.at[k], (px, py, c)))
        for i in range(ns):
            for p, (px, py) in enumerate(chips):
                k = 6 * nb + 3 * i + p
                sends.append(_remote(ins[nb + i], outs[nb + i].at[j], send_sems.at[k], recv_sems.at[k], (px, py, c)))
        for cp in sends:
            cp.start()
        for i in range(nb):
            for p, (px, py) in enumerate(chips):
                k, jp = 3 * i + p, 2 * px + py
                got = layouts[i].region(outs[i], jp, c)
                _remote(got, got, send_sems.at[k], recv_sems.at[k], (px, py, c)).wait_recv()
                fwd = _remote(got, got, send_sems.at[3 * nb + k], recv_sems.at[3 * nb + k], (x, y, 1 - c))
                fwd.start()
                sends.append(fwd)
        for i in range(ns):
            for p, (px, py) in enumerate(chips):
                k, jp = 6 * nb + 3 * i + p, 2 * px + py
                _remote(ins[nb + i], outs[nb + i].at[jp], send_sems.at[k], recv_sems.at[k], (px, py, c)).wait_recv()
        for i in range(nb):
            for p, (px, py) in enumerate(chips):
                k, jp = 3 * nb + 3 * i + p, 2 * px + py
                got = layouts[i].region(outs[i], jp, 1 - c)
                _remote(got, got, send_sems.at[k], recv_sems.at[k], (x, y, 1 - c)).wait_recv()
        for cp in sends:
            cp.wait_send()
        for cp in local:
            cp.wait()

    out_shape = [jax.ShapeDtypeStruct(lay.full_shape(), BF16) for lay in layouts]
    out_shape += [jax.ShapeDtypeStruct((N_CHIPS,) + s.shape, F32) for s in small]
    return pl.pallas_call(
        body, name="all_gather_weights", in_specs=[ANY] * (nb + ns), out_specs=[ANY] * (nb + ns), out_shape=out_shape,
        input_output_aliases={i: i for i in range(nb)},
        scratch_shapes=[pltpu.SemaphoreType.DMA((n_remote,)), pltpu.SemaphoreType.DMA((n_remote,)),
                        pltpu.SemaphoreType.DMA((ns,))],
        compiler_params=_params(),
    )(*big, *small)


HBM = pl.BlockSpec(memory_space=pltpu.HBM)
SEM = pl.BlockSpec(memory_space=pltpu.SEMAPHORE)
SPLIT_COPY = pltpu.CompilerParams(has_side_effects=pltpu.SideEffectType.DATAFLOW_SIDE_EFFECTING)


def _in_hbm(a):
    return pltpu.with_memory_space_constraint(a, pltpu.HBM)


def _gather_start(bufs, layouts, after):
    nb = len(bufs)

    def body(*refs):
        send_sems, recv_sems = refs[nb + 1], refs[nb + 2]
        thru, token = refs[nb + 3:2 * nb + 3], refs[2 * nb + 3]
        x, y, c, chips = _place()
        j = 2 * x + y
        for i in range(nb):
            mine = layouts[i].region(thru[i], j, c)
            for p, (px, py) in enumerate(chips):
                _remote(mine, mine, send_sems.at[3 * i + p], recv_sems.at[3 * i + p], (px, py, c)).start()
        token[...] = jnp.zeros_like(token)

    outs = pl.pallas_call(
        body, name="gather_start", in_specs=[HBM] * nb + [ANY],
        out_specs=[SEM, SEM] + [HBM] * nb + [pl.BlockSpec(memory_space=pltpu.VMEM)],
        out_shape=[pltpu.SemaphoreType.DMA((3 * nb,)), pltpu.SemaphoreType.DMA((3 * nb,))]
        + [pltpu.HBM(b.shape, b.dtype) for b in bufs] + [jax.ShapeDtypeStruct((8, LANES), F32)],
        input_output_aliases={i: 2 + i for i in range(nb)}, compiler_params=SPLIT_COPY,
    )(*[_in_hbm(b) for b in bufs], after)
    return outs[0], outs[1], outs[2:2 + nb], outs[2 + nb]


def _gather_wait(send_sems, recv_sems, bufs, layouts, after):
    nb = len(bufs)

    def body(*refs):
        send_sems, recv_sems = refs[nb], refs[nb + 1]
        outs = refs[nb + 3:]
        x, y, c, chips = _place()
        j = 2 * x + y
        for i in range(nb):
            mine = layouts[i].region(outs[i], j, c)
            for p, (px, py) in enumerate(chips):
                cp = _remote(mine, layouts[i].region(outs[i], 2 * px + py, c), send_sems.at[3 * i + p],
                             recv_sems.at[3 * i + p], (px, py, c))
                cp.wait_send()
                cp.wait_recv()

    return pl.pallas_call(
        body, name="gather_wait", in_specs=[HBM] * nb + [SEM, SEM, ANY], out_specs=[HBM] * nb,
        out_shape=[pltpu.HBM(b.shape, b.dtype) for b in bufs],
        input_output_aliases={i: i for i in range(nb)}, compiler_params=SPLIT_COPY,
    )(*bufs, send_sems, recv_sems, after)


def _forward_to_sibling(bufs, layouts):
    nb = len(bufs)

    def body(*refs):
        outs = refs[nb:2 * nb]
        send_sems, recv_sems = refs[2 * nb:]
        x, y, c, chips = _place()
        cps = []
        for i in range(nb):
            for p, (px, py) in enumerate(chips):
                got = layouts[i].region(outs[i], 2 * px + py, c)
                cps.append(_remote(got, got, send_sems.at[3 * i + p], recv_sems.at[3 * i + p], (x, y, 1 - c)))
        for cp in cps:
            cp.start()
        for i in range(nb):
            for p, (px, py) in enumerate(chips):
                theirs = layouts[i].region(outs[i], 2 * px + py, 1 - c)
                _remote(theirs, theirs, send_sems.at[3 * i + p], recv_sems.at[3 * i + p], (x, y, 1 - c)).wait_recv()
        for cp in cps:
            cp.wait_send()

    return pl.pallas_call(
        body, name="gather_forward_to_sibling", in_specs=[ANY] * nb, out_specs=[ANY] * nb,
        out_shape=[jax.ShapeDtypeStruct(b.shape, b.dtype) for b in bufs],
        input_output_aliases={i: i for i in range(nb)},
        scratch_shapes=[pltpu.SemaphoreType.DMA((3 * nb,)), pltpu.SemaphoreType.DMA((3 * nb,))],
        compiler_params=_params(),
    )(*bufs)


def _halves_to_sibling(grads, layouts, name):
    nw = len(grads)

    def body(*refs):
        ins, gots = refs[:nw], refs[nw:2 * nw]
        send_sems, recv_sems = refs[2 * nw:]
        x, y, c, _ = _place()
        cps = []
        for i in range(nw):
            for j in range(N_CHIPS):
                k = N_CHIPS * i + j
                cps.append(_remote(layouts[i].region(ins[i], j, 1 - c), gots[i].at[j],
                                   send_sems.at[k], recv_sems.at[k], (x, y, 1 - c)))
        for cp in cps:
            cp.start()
        for cp in cps:
            cp.wait()

    half = [jax.ShapeDtypeStruct((N_CHIPS, lay.r // 2, lay.c), BF16) for lay in layouts]
    return pl.pallas_call(
        body, name=name, in_specs=[ANY] * nw, out_specs=[ANY] * nw, out_shape=half,
        scratch_shapes=[pltpu.SemaphoreType.DMA((N_CHIPS * nw,)), pltpu.SemaphoreType.DMA((N_CHIPS * nw,))],
        compiler_params=_params(),
    )(*grads)


def _chip_sum(grad, got, layout, core, name):
    n, hr, c = got.shape
    tr = _tile(hr, 256, 16)
    nb = hr // tr

    def body(core_ref, a_ref, b_ref, o_ref):
        o_ref[...] = (a_ref[...].astype(F32) + b_ref[...].astype(F32)).astype(BF16)

    spec = pl.BlockSpec((None, tr, c), lambda j, i, core_ref: (j, i, 0))
    return pl.pallas_call(
        body, name=name,
        grid_spec=pltpu.PrefetchScalarGridSpec(
            num_scalar_prefetch=1, grid=(n, nb),
            in_specs=[layout.block_spec(tr, lambda j, i, core_ref: (j, core_ref[0] * nb + i)), spec],
            out_specs=spec),
        out_shape=jax.ShapeDtypeStruct((n, hr, c), BF16), compiler_params=_params(),
    )(core, grad, got)


def _partials_start(parts, after, name):
    nw = len(parts)
    lands = [lax.empty((3,) + p.shape[1:], BF16) for p in parts]

    def body(*refs):
        send_sems, recv_sems = refs[2 * nw + 1], refs[2 * nw + 2]
        src, dst = refs[2 * nw + 3:3 * nw + 3], refs[3 * nw + 3:4 * nw + 3]
        token = refs[4 * nw + 3]
        x, y, c, chips = _place()
        for i in range(nw):
            for p, (px, py) in enumerate(chips):
                _remote(src[i].at[2 * px + py], dst[i].at[p], send_sems.at[3 * i + p], recv_sems.at[3 * i + p],
                        (px, py, c)).start()
        token[...] = jnp.zeros_like(token)

    outs = pl.pallas_call(
        body, name=name, in_specs=[HBM] * (2 * nw) + [ANY],
        out_specs=[SEM, SEM] + [HBM] * (2 * nw) + [pl.BlockSpec(memory_space=pltpu.VMEM)],
        out_shape=[pltpu.SemaphoreType.DMA((3 * nw,)), pltpu.SemaphoreType.DMA((3 * nw,))]
        + [pltpu.HBM(a.shape, a.dtype) for a in parts + lands] + [jax.ShapeDtypeStruct((8, LANES), F32)],
        input_output_aliases={i: 2 + i for i in range(2 * nw)}, compiler_params=SPLIT_COPY,
    )(*[_in_hbm(a) for a in parts + lands], after)
    return outs[0], outs[1], outs[2:2 + nw], outs[2 + nw:2 + 2 * nw], outs[2 + 2 * nw]


def _partials_wait(send_sems, recv_sems, parts, lands, after, name):
    nw = len(parts)

    def body(*refs):
        send_sems, recv_sems = refs[2 * nw], refs[2 * nw + 1]
        src, dst = refs[2 * nw + 3:3 * nw + 3], refs[3 * nw + 3:4 * nw + 3]
        x, y, c, chips = _place()
        for i in range(nw):
            for p, (px, py) in enumerate(chips):
                cp = _remote(src[i].at[2 * px + py], dst[i].at[p], send_sems.at[3 * i + p], recv_sems.at[3 * i + p],
                             (px, py, c))
                cp.wait_send()
                cp.wait_recv()

    outs = pl.pallas_call(
        body, name=name, in_specs=[HBM] * (2 * nw) + [SEM, SEM, ANY], out_specs=[HBM] * (2 * nw),
        out_shape=[pltpu.HBM(a.shape, a.dtype) for a in list(parts) + list(lands)],
        input_output_aliases={i: i for i in range(2 * nw)}, compiler_params=SPLIT_COPY,
    )(*parts, *lands, send_sems, recv_sems, after)
    return outs[:nw], outs[nw:]


def _shard_sum(parts, got, chip, name):
    _, r, c = parts.shape
    tr = _tile(r, 256, 16)

    def body(chip_ref, o_ref, g_ref, out_ref):
        acc = o_ref[...].astype(F32)
        for p in range(3):
            acc = acc + g_ref[p].astype(F32)
        out_ref[...] = acc

    return pl.pallas_call(
        body, name=name,
        grid_spec=pltpu.PrefetchScalarGridSpec(
            num_scalar_prefetch=1, grid=(r // tr,),
            in_specs=[pl.BlockSpec((None, tr, c), lambda i, chip_ref: (chip_ref[0], i, 0)),
                      pl.BlockSpec((3, tr, c), lambda i, chip_ref: (0, i, 0))],
            out_specs=pl.BlockSpec((tr, c), lambda i, chip_ref: (i, 0))),
        out_shape=jax.ShapeDtypeStruct((r, c), F32), compiler_params=_params(),
    )(chip, parts, got)


def _join_start(halves, name):
    nw = len(halves)
    lands = [lax.empty(h.shape, F32) for h in halves]

    def body(*refs):
        send_sems, recv_sems = refs[2 * nw], refs[2 * nw + 1]
        src, dst, token = refs[2 * nw + 2:3 * nw + 2], refs[3 * nw + 2:4 * nw + 2], refs[4 * nw + 2]
        x, y, c, _ = _place()
        for i in range(nw):
            _remote(src[i], dst[i], send_sems.at[i], recv_sems.at[i], (x, y, 1 - c)).start()
        token[...] = jnp.zeros_like(token)

    outs = pl.pallas_call(
        body, name=name, in_specs=[HBM] * (2 * nw),
        out_specs=[SEM, SEM] + [HBM] * (2 * nw) + [pl.BlockSpec(memory_space=pltpu.VMEM)],
        out_shape=[pltpu.SemaphoreType.DMA((nw,)), pltpu.SemaphoreType.DMA((nw,))]
        + [pltpu.HBM(a.shape, a.dtype) for a in list(halves) + lands] + [jax.ShapeDtypeStruct((8, LANES), F32)],
        input_output_aliases={i: 2 + i for i in range(2 * nw)}, compiler_params=SPLIT_COPY,
    )(*[_in_hbm(a) for a in list(halves) + lands])
    return outs[0], outs[1], outs[2:2 + nw], outs[2 + nw:2 + 2 * nw], outs[2 + 2 * nw]


def _join_wait(send_sems, recv_sems, halves, lands, after, name):
    nw = len(halves)

    def body(*refs):
        send_sems, recv_sems = refs[2 * nw], refs[2 * nw + 1]
        src, dst = refs[2 * nw + 3:3 * nw + 3], refs[3 * nw + 3:4 * nw + 3]
        x, y, c, _ = _place()
        for i in range(nw):
            cp = _remote(src[i], dst[i], send_sems.at[i], recv_sems.at[i], (x, y, 1 - c))
            cp.wait_send()
            cp.wait_recv()

    outs = pl.pallas_call(
        body, name=name, in_specs=[HBM] * (2 * nw) + [SEM, SEM, ANY], out_specs=[HBM] * (2 * nw),
        out_shape=[pltpu.HBM(a.shape, a.dtype) for a in list(halves) + list(lands)],
        input_output_aliases={i: i for i in range(2 * nw)}, compiler_params=SPLIT_COPY,
    )(*halves, *lands, send_sems, recv_sems, after)
    return outs[nw:]


def _adamw_rows(w, g_half, m, v, half, filled, after, name):
    r, c = w.shape
    hr = r // 2
    tr = _tile(hr, 128, 8)
    nb = hr // tr
    c1 = 1.0 - ADAM_B1 ** ADAM_STEP
    c2 = 1.0 - ADAM_B2 ** ADAM_STEP
    n_prev = 0 if filled is None else 4

    def body(half_ref, w_ref, gin_ref, m_ref, v_ref, *rest):
        g_ref, d_ref, nm_ref, nv_ref = rest[1 + n_prev:]
        gg = gin_ref[...]
        nm = ADAM_B1 * m_ref[...] + (1.0 - ADAM_B1) * gg
        nv = ADAM_B2 * v_ref[...] + (1.0 - ADAM_B2) * jnp.square(gg)
        m_hat = nm / c1
        v_hat = nv / c2
        g_ref[...] = gg
        d_ref[...] = -ADAM_LR * (m_hat / (jnp.sqrt(v_hat) + ADAM_EPS) + ADAM_WD * w_ref[...])
        nm_ref[...] = nm
        nv_ref[...] = nv

    full = pl.BlockSpec((tr, c), lambda i, half_ref: (half_ref[0] * nb + i, 0))
    part = pl.BlockSpec((tr, c), lambda i, half_ref: (i, 0))
    return pl.pallas_call(
        body, name=name,
        grid_spec=pltpu.PrefetchScalarGridSpec(
            num_scalar_prefetch=1, grid=(nb,), in_specs=[full, part, full, full] + [ANY] * (1 + n_prev),
            out_specs=[full] * 4),
        out_shape=[jax.ShapeDtypeStruct((r, c), F32)] * 4,
        input_output_aliases={6 + k: k for k in range(n_prev)}, compiler_params=_params(),
    )(half, w, g_half, m, v, after, *([] if filled is None else filled))


def _small_all_reduce(buf):
    rows = buf.shape[0]
    n_dev = 8

    def body(b_ref, o_ref, gath, send_sems, recv_sems):
        x, y, c, _ = _place()
        me = 4 * x + 2 * y + c
        gath[me] = b_ref[...]
        cps = []
        for k in range(1, n_dev):
            px, py, pc = (x + (k >> 2)) % 2, (y + ((k >> 1) & 1)) % 2, (c + (k & 1)) % 2
            cps.append(_remote(b_ref, gath.at[me], send_sems.at[k - 1], recv_sems.at[k - 1], (px, py, pc)))
        for cp in cps:
            cp.start()
        for k in range(1, n_dev):
            px, py, pc = (x + (k >> 2)) % 2, (y + ((k >> 1) & 1)) % 2, (c + (k & 1)) % 2
            _remote(b_ref, gath.at[4 * px + 2 * py + pc], send_sems.at[k - 1], recv_sems.at[k - 1], (px, py, pc)).wait_recv()
        for cp in cps:
            cp.wait_send()
        acc = gath[0]
        for dev in range(1, n_dev):
            acc = acc + gath[dev]
        o_ref[...] = acc

    vm = pl.BlockSpec(memory_space=pltpu.VMEM)
    return pl.pallas_call(
        body, name="small_all_reduce", in_specs=[vm], out_specs=vm,
        out_shape=jax.ShapeDtypeStruct((rows, LANES), F32),
        scratch_shapes=[pltpu.VMEM((n_dev, rows, LANES), F32), pltpu.SemaphoreType.DMA((n_dev - 1,)),
                        pltpu.SemaphoreType.DMA((n_dev - 1,))],
        compiler_params=_params(),
    )(buf)


def _pad_lanes(v):
    return jnp.pad(v, ((0, 0), (0, LANES - v.shape[-1])))


def _pack(vectors):
    flat, offs, pos = [], [], 0
    for v in vectors:
        n = v.size
        n_pad = -(-n // LANES) * LANES
        flat.append(jnp.pad(v.reshape(-1), (0, n_pad - n)))
        offs.append((pos, n, v.shape))
        pos += n_pad
    total = -(-pos // (8 * LANES)) * 8 * LANES
    flat.append(jnp.zeros((total - pos,), F32))
    return jnp.concatenate(flat).reshape(-1, LANES), offs


def _unpack(buf, offs):
    flat = buf.reshape(-1)
    return [flat[pos:pos + n].reshape(shape) for pos, n, shape in offs]


def kernel(x, norm_mix_pre, w_in, conv_qkv_w, a_log, dt_bias, gdn_norm_w, conv_sc_w, w_out, norm_mix_post, norm_mlp_pre, w_up, w_down, norm_mlp_post, loss_target, m_norm_mix_pre, m_w_in, m_conv_qkv_w, m_a_log, m_dt_bias, m_gdn_norm_w, m_conv_sc_w, m_w_out, m_norm_mix_post, m_norm_mlp_pre, m_w_up, m_w_down, m_norm_mlp_post, v_norm_mix_pre, v_w_in, v_conv_qkv_w, v_a_log, v_dt_bias, v_gdn_norm_w, v_conv_sc_w, v_w_out, v_norm_mix_post, v_norm_mlp_pre, v_w_up, v_w_down, v_norm_mlp_post):
    bsz, seq, d = x.shape
    t = bsz * seq
    heads, head_dim = a_log.shape[-1], gdn_norm_w.shape[-1]
    assert head_dim == LANES and seq % CHUNK == 0
    gw = heads * head_dim
    sw = conv_sc_w.shape[-1] * N_CHIPS
    ics = w_in.shape[-1]
    main = 4 * gw + 3 * sw
    assert ics * N_CHIPS == main + 2 * heads and 2 * heads <= LANES

    lay_in = _Layout("major", w_in.shape[1:])
    lay_out = _Layout("rows", w_out.shape[1:])
    lay_up = _Layout("cols", w_up.shape[1:])
    lay_down = _Layout("rows", w_down.shape[1:])
    layouts = [lay_in, lay_out, lay_up, lay_down]
    chip = (2 * lax.axis_index("x") + lax.axis_index("y")).astype(jnp.int32).reshape(1)
    core = lax.axis_index("c").astype(jnp.int32).reshape(1)
    shards = [_cast_into_layout(w[0], lay, chip, f"cast_{n}")
              for w, lay, n in zip((w_in, w_out, w_up, w_down), layouts, ("w_in", "w_out", "w_up", "w_down"))]
    win_sh, cq_g, cs_g = _all_gather(shards[:1], layouts[:1], [conv_qkv_w[0], conv_sc_w[0]])
    late_send, late_recv, late_bufs, late_token = _gather_start(shards[1:], layouts[1:], cq_g)
    w_main, w_ab = _repack_w_in(win_sh, gw, heads, sw, late_token)
    conv_q = cq_g.transpose(1, 0, 2).reshape(conv_qkv_w.shape[1], -1)
    conv_s = cs_g.transpose(1, 0, 2).reshape(conv_sc_w.shape[1], -1)

    x2 = x.reshape(t, d)
    tgt2 = loss_target.reshape(t, d)
    xn = _norm_fwd(x2, norm_mix_pre)
    proj = _matmul(xn, w_main, "nn", [F32], "proj_main")
    proj_ab = _matmul(xn, w_ab, "nn", [F32], "proj_ab")
    proj3 = proj.reshape(bsz, seq, main)
    qkv_act = _qkv_conv_fwd(proj3, conv_q, 3 * gw).reshape(t, 3 * gw)
    a_log_pad, dt_pad = _pad_lanes(a_log), _pad_lanes(dt_bias)
    gcb, betab = _gates_fwd(proj_ab, a_log_pad, dt_pad, heads)
    o_raw, states = _gdn_fwd(qkv_act, gcb, betab, bsz, heads)
    gdn_out = _gdn_out_fwd(o_raw, proj, gdn_norm_w, heads, 3 * gw)
    sc_out = _sc_fwd(proj3, conv_s, 4 * gw, sw).reshape(t, sw)
    mixed = jnp.concatenate([gdn_out, sc_out], axis=1)
    late_bufs = _gather_wait(late_send, late_recv, late_bufs, layouts[1:], mixed)
    wout_f, wup_f, wdown_f = _forward_to_sibling(late_bufs, layouts[1:])
    mix = _matmul(mixed, wout_f, "nn", [F32], "mix_out")
    h, hn = _mid_fwd(x2, mix, norm_mix_post, norm_mlp_pre)

    def up_epilogue(acc):
        r = jnp.maximum(acc, 0.0)
        return r, r * r

    relu_up, hid = _matmul(hn, wup_f, "nn", [BF16, BF16], "mlp_up", epilogue=up_epilogue)
    ff = _matmul(hid, wdown_f, "nn", [F32], "mlp_down")
    loss_blk, dy, dff, dg_mlp_post = _head_fwd_bwd(h, ff, tgt2, norm_mlp_post)

    def dup_epilogue(acc, r):
        return (acc * (2.0 * r.astype(F32)),)

    d_up = _matmul(dff, wdown_f, "nt", [BF16], "d_hid", epilogue=dup_epilogue, extras=(relu_up,))
    dw_down = _matmul(hid, dff, "tn", [BF16], "dw_down")
    d_hn = _matmul(d_up, wup_f, "nt", [F32], "d_hn")
    dw_up = _matmul(hn, d_up, "tn", [BF16], "dw_up")
    mlp_grads, mlp_layouts = [dw_up, dw_down], [lay_up, lay_down]
    mlp_gots = _halves_to_sibling(mlp_grads, mlp_layouts, "mlp_grad_halves_to_sibling")
    mlp_parts = [_chip_sum(g, got, lay, core, f"chip_sum_mlp_{i}")
                 for i, (g, got, lay) in enumerate(zip(mlp_grads, mlp_gots, mlp_layouts))]
    mlp_send, mlp_recv, mlp_parts, mlp_lands, mlp_token = _partials_start(mlp_parts, dff, "mlp_partials_start")
    dh, dmix, dg_mlp_pre, dg_mix_post = _mid_bwd(d_hn, h, norm_mlp_pre, dy, mix, norm_mix_post, mlp_token)
    dmixed = _matmul(dmix, wout_f, "nt", [F32], "d_mixed")
    dw_out = _matmul(mixed, dmix, "tn", [BF16], "dw_out")
    out_gots = _halves_to_sibling([dw_out], [lay_out], "out_grad_halves_to_sibling")
    out_parts = [_chip_sum(dw_out, out_gots[0], lay_out, core, "chip_sum_out")]
    out_send, out_recv, out_parts, out_lands, out_token = _partials_start(out_parts, dmixed, "out_partials_start")
    dmixed3 = dmixed.reshape(bsz, seq, d)
    d_b, d_c, d_hsc, dw_conv_s = _sc_bwd(proj3, dmixed3, conv_s, 4 * gw, sw, gw)
    d_o, d_z, dg_gdn_norm = _gdn_out_bwd(dmixed, o_raw, proj, gdn_norm_w, heads, 3 * gw, out_token)
    dq, dk, dv, dgc_b, dbeta_b = _gdn_bwd(qkv_act, gcb, betab, states, d_o, bsz, heads)
    d_ab, d_alog, d_dt = _gates_bwd(proj_ab, a_log_pad, dt_pad, dgc_b, dbeta_b, heads)
    d_qkv, dw_conv_q = _qkv_conv_bwd(proj3, [a.reshape(bsz, seq, gw) for a in (dq, dk, dv)], conv_q, 3 * gw)
    d_proj = jnp.concatenate([d_qkv.reshape(t, 3 * gw), d_z, d_b.reshape(t, sw), d_c.reshape(t, sw),
                              d_hsc.reshape(t, sw)], axis=1)
    dw_main = _matmul(xn, d_proj, "tn", [BF16], "dw_in_main")
    dw_ab = _matmul(xn, d_ab, "tn", [BF16], "dw_in_ab")
    dw_in = _unpack_dw_in(dw_main, dw_ab, gw, heads, sw, ics)
    in_gots = _halves_to_sibling([dw_in], [lay_in], "in_grad_halves_to_sibling")
    in_parts = [_chip_sum(dw_in, in_gots[0], lay_in, core, "chip_sum_in")]
    in_send, in_recv, in_parts, in_lands, in_token = _partials_start(in_parts, dw_in, "in_partials_start")
    d_xn = _matmul(d_proj, w_main, "nt", [F32], "d_xn_main", after=in_token)
    d_xn_ab = _matmul(d_ab, w_ab, "nt", [F32], "d_xn_ab")
    grad_x, dg_mix_pre = _first_bwd(d_xn, d_xn_ab, x2, norm_mix_pre, dh)

    other_core = 1 - core

    def finish(parts, recvs, weights3, names, tag):
        halves = [_shard_sum(p, r, chip, f"shard_sum_{n}") for p, r, n in zip(parts, recvs, names)]
        send, recv, halves, lands, token = _join_start(halves, f"{tag}_join_start")
        own = [_adamw_rows(wt[0], h, m[0], v[0], core, None, token, f"adamw_own_{n}")
               for (wt, m, v), h, n in zip(weights3, halves, names)]
        theirs = _join_wait(send, recv, halves, lands, own[-1][1], f"{tag}_join_wait")
        return [_adamw_rows(wt[0], h, m[0], v[0], other_core, o, h, f"adamw_sibling_{n}")
                for (wt, m, v), h, o, n in zip(weights3, theirs, own, names)]

    mlp_parts, mlp_recvs = _partials_wait(mlp_send, mlp_recv, mlp_parts, mlp_lands, grad_x, "mlp_partials_wait")
    res_up, res_down = finish(mlp_parts, mlp_recvs, [(w_up, m_w_up, v_w_up), (w_down, m_w_down, v_w_down)],
                              ("w_up", "w_down"), "mlp")
    out_parts, out_recvs = _partials_wait(out_send, out_recv, out_parts, out_lands, res_down[1], "out_partials_wait")
    in_parts, in_recvs = _partials_wait(in_send, in_recv, in_parts, in_lands, out_recvs[0], "in_partials_wait")
    res_in, res_out = finish(list(in_parts) + list(out_parts), list(in_recvs) + list(out_recvs),
                             [(w_in, m_w_in, v_w_in), (w_out, m_w_out, v_w_out)], ("w_in", "w_out"), "mix")

    small, offs = _pack([loss_blk[0:1, 0:1], dg_mix_pre, dw_conv_q, d_alog[:, :heads], d_dt[:, :heads], dg_gdn_norm,
                         dw_conv_s, dg_mix_post, dg_mlp_pre, dg_mlp_post])
    (loss, g_mix_pre, g_conv_q_full, g_alog, g_dt, g_gdn_norm, g_conv_s_full, g_mix_post, g_mlp_pre,
     g_mlp_post) = _unpack(_small_all_reduce(small), offs)
    j = 2 * lax.axis_index("x") + lax.axis_index("y")
    cq_w, cs_w = conv_qkv_w.shape[-1], conv_sc_w.shape[-1]
    g_conv_q = lax.dynamic_slice_in_dim(g_conv_q_full, j * cq_w, cq_w, axis=1)
    g_conv_s = lax.dynamic_slice_in_dim(g_conv_s_full, j * cs_w, cs_w, axis=1)

    big = {1: res_in, 7: res_out, 10: res_up, 11: res_down}
    grads = [g_mix_pre, None, g_conv_q, g_alog, g_dt, g_gdn_norm, g_conv_s, None, g_mix_post, g_mlp_pre, None,
             None, g_mlp_post]
    weights = [norm_mix_pre, w_in, conv_qkv_w, a_log, dt_bias, gdn_norm_w, conv_sc_w, w_out, norm_mix_post,
               norm_mlp_pre, w_up, w_down, norm_mlp_post]
    ms = [m_norm_mix_pre, m_w_in, m_conv_qkv_w, m_a_log, m_dt_bias, m_gdn_norm_w, m_conv_sc_w, m_w_out,
          m_norm_mix_post, m_norm_mlp_pre, m_w_up, m_w_down, m_norm_mlp_post]
    vs = [v_norm_mix_pre, v_w_in, v_conv_qkv_w, v_a_log, v_dt_bias, v_gdn_norm_w, v_conv_sc_w, v_w_out,
          v_norm_mix_post, v_norm_mlp_pre, v_w_up, v_w_down, v_norm_mlp_post]
    out_g, out_d, out_m, out_v = [], [], [], []
    for i, (wt, g, m, v) in enumerate(zip(weights, grads, ms, vs)):
        shape2 = wt.shape[-2:] if wt.ndim == 3 else wt.shape
        if i in big:
            g2, dl, nm, nv = big[i]
        else:
            g2 = g.reshape(shape2)
            dl, nm, nv = _adamw(wt.reshape(shape2), g2, m.reshape(shape2), v.reshape(shape2), f"adamw_{i}")
        out_g.append(g2.reshape(wt.shape))
        out_d.append(dl.reshape(wt.shape))
        out_m.append(nm.reshape(wt.shape))
        out_v.append(nv.reshape(wt.shape))

    return (loss.reshape(()), grad_x.reshape(bsz, seq, d), *out_g, *out_d, *out_m, *out_v)
```

```python
import functools

import jax
import jax.numpy as jnp
from jax import lax
from jax.experimental import pallas as pl
from jax.experimental.pallas import tpu as pltpu

CHUNK = 64
NORM_EPS = 1e-6
L2_EPS = 1e-6
N_CHIPS = 4
ADAM_LR = 0.001
ADAM_B1 = 0.9
ADAM_B2 = 0.999
ADAM_EPS = 1e-08
ADAM_WD = 0.01
ADAM_STEP = 10
LANES = 128
VMEM_LIMIT = 56 * 1024 * 1024

F32 = jnp.float32
BF16 = jnp.bfloat16
HI = lax.Precision.HIGH
EXACT_SUM = lax.Precision.HIGHEST
MESH = pl.DeviceIdType.MESH
ANY = pl.BlockSpec(memory_space=pl.ANY)


def _params(n_grid=0):
    return pltpu.CompilerParams(vmem_limit_bytes=VMEM_LIMIT)


def _tile(n, pref, align):
    if n <= pref:
        return n
    t = (pref // align) * align
    while t >= align:
        if n % t == 0:
            return t
        t -= align
    raise ValueError(f"no tile for {n}")


def _sigmoid(x):
    return 1.0 / (1.0 + jnp.exp(-x))


def _softplus(x):
    return jnp.maximum(x, 0.0) + jnp.log(1.0 + jnp.exp(-jnp.abs(x)))


def _rms_fwd(x, g):
    r = lax.rsqrt(jnp.mean(x * x, axis=-1, keepdims=True) + NORM_EPS)
    return x * r * g


def _rms_bwd(dy, x, g):
    r = lax.rsqrt(jnp.mean(x * x, axis=-1, keepdims=True) + NORM_EPS)
    xh = x * r
    dxh = dy * g
    dx = r * (dxh - xh * jnp.mean(dxh * xh, axis=-1, keepdims=True))
    dg = jnp.sum(dy * xh, axis=0, keepdims=True)
    return dx, dg


def _matmul(a, b, form, out_dtypes, name, epilogue=None, extras=(), after=None, tm=1024, tn=1024, tk=4096):
    if form == "nn":
        (m, kd), (_, n) = a.shape, b.shape
        dims = (((1,), (0,)), ((), ()))
    elif form == "nt":
        (m, kd), (n, _) = a.shape, b.shape
        dims = (((1,), (1,)), ((), ()))
    else:
        (kd, m), (_, n) = a.shape, b.shape
        dims = (((0,), (0,)), ((), ()))
    tm, tn, tk = _tile(m, tm, LANES), _tile(n, tn, LANES), _tile(kd, tk, LANES)
    nk = kd // tk
    n_extra = len(extras)
    n_out = len(out_dtypes)

    if form == "nn":
        a_spec = pl.BlockSpec((tm, tk), lambda i, j, k: (i, k))
        b_spec = pl.BlockSpec((tk, tn), lambda i, j, k: (k, j))
    elif form == "nt":
        a_spec = pl.BlockSpec((tm, tk), lambda i, j, k: (i, k))
        b_spec = pl.BlockSpec((tn, tk), lambda i, j, k: (j, k))
    else:
        a_spec = pl.BlockSpec((tk, tm), lambda i, j, k: (k, i))
        b_spec = pl.BlockSpec((tk, tn), lambda i, j, k: (k, j))
    tile_spec = pl.BlockSpec((tm, tn), lambda i, j, k: (i, j))

    order_only = [] if after is None else [after]
    n_skip = n_extra + len(order_only)

    def finish(acc, extra_refs, out_refs):
        outs = (acc,) if epilogue is None else epilogue(acc, *[e[...] for e in extra_refs])
        for o_ref, val in zip(out_refs, outs):
            o_ref[...] = val.astype(o_ref.dtype)

    def body(a_ref, b_ref, *rest):
        extra_refs = rest[:n_extra]
        out_refs = rest[n_skip:n_skip + n_out]
        if nk == 1:
            finish(lax.dot_general(a_ref[...], b_ref[...], dims, preferred_element_type=F32), extra_refs, out_refs)
            return
        acc_ref = rest[-1]
        k = pl.program_id(2)

        @pl.when(k == 0)
        def _():
            acc_ref[...] = jnp.zeros_like(acc_ref)

        acc_ref[...] += lax.dot_general(a_ref[...], b_ref[...], dims, preferred_element_type=F32)

        @pl.when(k == nk - 1)
        def _():
            finish(acc_ref[...], extra_refs, out_refs)

    outs = pl.pallas_call(
        body, name=name, grid=(m // tm, n // tn, nk),
        in_specs=[a_spec, b_spec] + [tile_spec] * n_extra + [ANY] * len(order_only),
        out_specs=[tile_spec] * n_out,
        out_shape=[jax.ShapeDtypeStruct((m, n), dt) for dt in out_dtypes],
        scratch_shapes=[pltpu.VMEM((tm, tn), F32)] if nk > 1 else [],
        compiler_params=_params(),
    )(a, b, *extras, *order_only)
    return outs[0] if n_out == 1 else outs


def _cast_into_layout(w, layout, chip, name):
    r, c = w.shape
    tr = _tile(r, 256, 16)

    def body(chip_ref, w_ref, o_ref):
        o_ref[...] = w_ref[...].astype(BF16)

    return pl.pallas_call(
        body, name=name,
        grid_spec=pltpu.PrefetchScalarGridSpec(
            num_scalar_prefetch=1, grid=(r // tr,),
            in_specs=[pl.BlockSpec((tr, c), lambda i, chip_ref: (i, 0))],
            out_specs=layout.block_spec(tr, lambda i, chip_ref: (chip_ref[0], i))),
        out_shape=jax.ShapeDtypeStruct(layout.full_shape(), BF16), compiler_params=_params(),
    )(chip, w)


def _in_segments(gw, heads, sw):
    main = 4 * gw
    return [(0, main, 0), (main + 2 * heads, 3 * sw, main), (main, 2 * heads, main + 3 * sw)]


def _pieces(seg_start, width, dst_start, ics):
    out = []
    g = seg_start
    while g < seg_start + width:
        j, cj = divmod(g, ics)
        wdt = min(ics - cj, seg_start + width - g)
        out.append((j, cj, dst_start + (g - seg_start), wdt))
        g += wdt
    return out


def _repack_w_in(w_sh, gw, heads, sw, after):
    ns, d, ics = w_sh.shape
    main = 4 * gw + 3 * sw
    tr = _tile(d, 128, 16)
    pieces = [p for seg in _in_segments(gw, heads, sw) for p in _pieces(*seg, ics)]

    def body(w_ref, after_ref, m_ref, ab_ref):
        ab_ref[...] = jnp.zeros_like(ab_ref)
        for j, cj, cd, wdt in pieces:
            if cd >= main:
                ab_ref[:, cd - main:cd - main + wdt] = w_ref[j, :, cj:cj + wdt]
            else:
                m_ref[:, cd:cd + wdt] = w_ref[j, :, cj:cj + wdt]

    return pl.pallas_call(
        body, name="repack_w_in", grid=(d // tr,),
        in_specs=[pl.BlockSpec((ns, tr, ics), lambda i: (0, i, 0)), ANY],
        out_specs=[pl.BlockSpec((tr, main), lambda i: (i, 0)), pl.BlockSpec((tr, LANES), lambda i: (i, 0))],
        out_shape=[jax.ShapeDtypeStruct((d, main), BF16), jax.ShapeDtypeStruct((d, LANES), BF16)],
        compiler_params=_params(),
    )(w_sh, after)


def _unpack_dw_in(dw_main, dw_ab, gw, heads, sw, ics):
    d = dw_main.shape[0]
    tr = _tile(d, 128, 16)
    main = 4 * gw + 3 * sw
    pieces = [p for seg in _in_segments(gw, heads, sw) for p in _pieces(*seg, ics)]

    def body(m_ref, ab_ref, o_ref):
        for j, cj, cd, wdt in pieces:
            if cd >= main:
                o_ref[j, :, cj:cj + wdt] = ab_ref[:, cd - main:cd - main + wdt].astype(BF16)
            else:
                o_ref[j, :, cj:cj + wdt] = m_ref[:, cd:cd + wdt].astype(BF16)

    return pl.pallas_call(
        body, name="unpack_dw_in", grid=(d // tr,),
        in_specs=[pl.BlockSpec((tr, main), lambda i: (i, 0)), pl.BlockSpec((tr, LANES), lambda i: (i, 0))],
        out_specs=pl.BlockSpec((N_CHIPS, tr, ics), lambda i: (0, i, 0)),
        out_shape=jax.ShapeDtypeStruct((N_CHIPS, d, ics), BF16), compiler_params=_params(),
    )(dw_main, dw_ab)


def _adamw(w, g, m, v, name):
    r, c = w.shape
    tr = _tile(r, 128, 8)
    c1 = 1.0 - ADAM_B1 ** ADAM_STEP
    c2 = 1.0 - ADAM_B2 ** ADAM_STEP

    def body(w_ref, g_ref, m_ref, v_ref, d_ref, nm_ref, nv_ref):
        gg = g_ref[...]
        nm = ADAM_B1 * m_ref[...] + (1.0 - ADAM_B1) * gg
        nv = ADAM_B2 * v_ref[...] + (1.0 - ADAM_B2) * jnp.square(gg)
        m_hat = nm / c1
        v_hat = nv / c2
        d_ref[...] = -ADAM_LR * (m_hat / (jnp.sqrt(v_hat) + ADAM_EPS) + ADAM_WD * w_ref[...])
        nm_ref[...] = nm
        nv_ref[...] = nv

    spec = pl.BlockSpec((tr, c), lambda i: (i, 0))
    return pl.pallas_call(
        body, name=name, grid=(r // tr,), in_specs=[spec] * 4, out_specs=[spec] * 3,
        out_shape=[jax.ShapeDtypeStruct((r, c), F32)] * 3, compiler_params=_params(),
    )(w, g, m, v)


def _row_spec(tt, d):
    return pl.BlockSpec((tt, d), lambda i: (i, 0))


def _vec_spec(d):
    return pl.BlockSpec((1, d), lambda i: (0, 0))


def _norm_fwd(x, g):
    t, d = x.shape
    tt = _tile(t, 256, 16)

    def body(x_ref, g_ref, o_ref):
        o_ref[...] = _rms_fwd(x_ref[...], g_ref[...]).astype(BF16)

    return pl.pallas_call(
        body, name="norm_mix_pre", grid=(t // tt,), in_specs=[_row_spec(tt, d), _vec_spec(d)],
        out_specs=_row_spec(tt, d), out_shape=jax.ShapeDtypeStruct((t, d), BF16), compiler_params=_params(),
    )(x, g)


def _mid_fwd(x, mix, g_post, g_pre):
    t, d = x.shape
    tt = _tile(t, 128, 16)

    def body(x_ref, mix_ref, gp_ref, gn_ref, h_ref, hn_ref):
        h = x_ref[...] + _rms_fwd(mix_ref[...], gp_ref[...])
        h_ref[...] = h
        hn_ref[...] = _rms_fwd(h, gn_ref[...]).astype(BF16)

    return pl.pallas_call(
        body, name="mid_fwd", grid=(t // tt,),
        in_specs=[_row_spec(tt, d), _row_spec(tt, d), _vec_spec(d), _vec_spec(d)],
        out_specs=[_row_spec(tt, d), _row_spec(tt, d)],
        out_shape=[jax.ShapeDtypeStruct((t, d), F32), jax.ShapeDtypeStruct((t, d), BF16)],
        compiler_params=_params(),
    )(x, mix, g_post, g_pre)


def _head_fwd_bwd(h, ff, tgt, g_post):
    t, d = h.shape
    tt = _tile(t, 128, 16)

    def body(h_ref, ff_ref, t_ref, g_ref, loss_ref, dy_ref, dff_ref, dg_ref):
        i = pl.program_id(0)

        @pl.when(i == 0)
        def _():
            loss_ref[...] = jnp.zeros_like(loss_ref)
            dg_ref[...] = jnp.zeros_like(dg_ref)

        ff = ff_ref[...]
        g = g_ref[...]
        e = h_ref[...] + _rms_fwd(ff, g) - t_ref[...]
        loss_ref[...] += 0.5 * jnp.sum(jnp.mean(e * e, axis=-1, keepdims=True))
        dy = e * (1.0 / d)
        dy_ref[...] = dy
        dff, dg = _rms_bwd(dy, ff, g)
        dff_ref[...] = dff.astype(BF16)
        dg_ref[...] += dg

    return pl.pallas_call(
        body, name="loss_head", grid=(t // tt,),
        in_specs=[_row_spec(tt, d)] * 3 + [_vec_spec(d)],
        out_specs=[pl.BlockSpec((8, LANES), lambda i: (0, 0)), _row_spec(tt, d), _row_spec(tt, d), _vec_spec(d)],
        out_shape=[jax.ShapeDtypeStruct((8, LANES), F32), jax.ShapeDtypeStruct((t, d), F32),
                   jax.ShapeDtypeStruct((t, d), BF16), jax.ShapeDtypeStruct((1, d), F32)],
        compiler_params=_params(),
    )(h, ff, tgt, g_post)


def _mid_bwd(d_hn, h, g_pre, dy, mix, g_post, after):
    t, d = h.shape
    tt = _tile(t, 128, 16)

    def body(dhn_ref, h_ref, gn_ref, dy_ref, mix_ref, gp_ref, after_ref, dh_ref, dmix_ref, dgn_ref, dgp_ref):
        i = pl.program_id(0)

        @pl.when(i == 0)
        def _():
            dgn_ref[...] = jnp.zeros_like(dgn_ref)
            dgp_ref[...] = jnp.zeros_like(dgp_ref)

        dx, dgn = _rms_bwd(dhn_ref[...], h_ref[...], gn_ref[...])
        dh = dy_ref[...] + dx
        dh_ref[...] = dh
        dmix, dgp = _rms_bwd(dh, mix_ref[...], gp_ref[...])
        dmix_ref[...] = dmix.astype(BF16)
        dgn_ref[...] += dgn
        dgp_ref[...] += dgp

    return pl.pallas_call(
        body, name="mid_bwd", grid=(t // tt,),
        in_specs=[_row_spec(tt, d), _row_spec(tt, d), _vec_spec(d), _row_spec(tt, d), _row_spec(tt, d), _vec_spec(d),
                  ANY],
        out_specs=[_row_spec(tt, d), _row_spec(tt, d), _vec_spec(d), _vec_spec(d)],
        out_shape=[jax.ShapeDtypeStruct((t, d), F32), jax.ShapeDtypeStruct((t, d), BF16),
                   jax.ShapeDtypeStruct((1, d), F32), jax.ShapeDtypeStruct((1, d), F32)],
        compiler_params=_params(),
    )(d_hn, h, g_pre, dy, mix, g_post, after)


def _first_bwd(d_xn, d_xn_ab, x, g, dh):
    t, d = x.shape
    tt = _tile(t, 128, 16)

    def body(a_ref, b_ref, x_ref, g_ref, dh_ref, dx_ref, dg_ref):
        i = pl.program_id(0)

        @pl.when(i == 0)
        def _():
            dg_ref[...] = jnp.zeros_like(dg_ref)

        dx, dg = _rms_bwd(a_ref[...] + b_ref[...], x_ref[...], g_ref[...])
        dx_ref[...] = dh_ref[...] + dx
        dg_ref[...] += dg

    return pl.pallas_call(
        body, name="first_bwd", grid=(t // tt,),
        in_specs=[_row_spec(tt, d), _row_spec(tt, d), _row_spec(tt, d), _vec_spec(d), _row_spec(tt, d)],
        out_specs=[_row_spec(tt, d), _vec_spec(d)],
        out_shape=[jax.ShapeDtypeStruct((t, d), F32), jax.ShapeDtypeStruct((1, d), F32)],
        compiler_params=_params(),
    )(d_xn, d_xn_ab, x, g, dh)


HALO = 8


def _cur(ts, tc, off):
    return pl.BlockSpec((1, ts, tc), lambda ci, b, s: (b, s, off + ci))


def _prev(ts, tc, off):
    return pl.BlockSpec((1, HALO, tc), lambda ci, b, s: (b, jnp.maximum(s * (ts // HALO) - 1, 0), off + ci))


def _next(ts, tc, off, seq):
    last = seq // HALO - 1
    return pl.BlockSpec((1, HALO, tc), lambda ci, b, s: (b, jnp.minimum((s + 1) * (ts // HALO), last), off + ci))


def _conv_w_spec(kw, tc):
    return pl.BlockSpec((kw, tc), lambda ci, b, s: (0, ci))


def _conv_taps(w, buf, kw, ts):
    acc = w[0:1, :] * buf[HALO - (kw - 1):HALO - (kw - 1) + ts, :]
    for j in range(1, kw):
        acc = acc + w[j:j + 1, :] * buf[HALO - (kw - 1) + j:HALO - (kw - 1) + j + ts, :]
    return acc


def _silu_grad(x):
    s = _sigmoid(x)
    return s * (1.0 + x * (1.0 - s))


def _qkv_conv_fwd(proj3, w, width):
    bsz, seq, _ = proj3.shape
    kw = w.shape[0]
    ts, tc = _tile(seq, 256, 8), _tile(width, 512, LANES)

    def body(u_ref, up_ref, w_ref, o_ref, buf):
        s = pl.program_id(2)
        buf[0:HALO, :] = jnp.where(s == 0, 0.0, up_ref[0])
        buf[HALO:HALO + ts, :] = u_ref[0]
        pre = _conv_taps(w_ref[...], buf, kw, ts)
        o_ref[0] = pre * _sigmoid(pre)

    return pl.pallas_call(
        body, name="qkv_conv_fwd", grid=(width // tc, bsz, seq // ts),
        in_specs=[_cur(ts, tc, 0), _prev(ts, tc, 0), _conv_w_spec(kw, tc)],
        out_specs=_cur(ts, tc, 0), out_shape=jax.ShapeDtypeStruct((bsz, seq, width), F32),
        scratch_shapes=[pltpu.VMEM((HALO + ts, tc), F32)], compiler_params=_params(),
    )(proj3, proj3, w)


def _qkv_conv_bwd(proj3, dparts, w, width):
    bsz, seq, _ = proj3.shape
    kw = w.shape[0]
    n_parts = len(dparts)
    part_w = width // n_parts
    ts, tc = _tile(seq, 256, 8), _tile(part_w, 512, LANES)
    n_s = seq // ts
    npt = part_w // tc
    last = seq // HALO - 1

    def part_cur(p):
        def index(ci, b, s):
            use = (ci // npt) == p
            return jnp.where(use, b, 0), jnp.where(use, s, 0), jnp.where(use, ci % npt, 0)
        return pl.BlockSpec((1, ts, tc), index)

    def part_next(p):
        def index(ci, b, s):
            use = (ci // npt) == p
            return (jnp.where(use, b, 0), jnp.where(use, jnp.minimum((s + 1) * (ts // HALO), last), 0),
                    jnp.where(use, ci % npt, 0))
        return pl.BlockSpec((1, HALO, tc), index)

    def body(u_ref, up_ref, un_ref, *rest):
        d_refs, dn_refs = rest[:n_parts], rest[n_parts:2 * n_parts]
        w_ref, du_ref, dw_ref, buf, gbuf = rest[2 * n_parts:]
        ci, b, s = pl.program_id(0), pl.program_id(1), pl.program_id(2)

        @pl.when((b == 0) & (s == 0))
        def _():
            dw_ref[...] = jnp.zeros_like(dw_ref)

        w = w_ref[...]
        buf[0:HALO, :] = jnp.where(s == 0, 0.0, up_ref[0])
        buf[HALO:HALO + ts, :] = u_ref[0]
        buf[HALO + ts:, :] = un_ref[0]
        acc = w[0:1, :] * buf[HALO - (kw - 1):HALO - (kw - 1) + ts + HALO, :]
        for j in range(1, kw):
            acc = acc + w[j:j + 1, :] * buf[HALO - (kw - 1) + j:HALO - (kw - 1) + j + ts + HALO, :]
        for p in range(n_parts):
            @pl.when(ci // npt == p)
            def _(p=p):
                gbuf[0:ts, :] = d_refs[p][0]
                gbuf[ts:, :] = jnp.where(s == n_s - 1, 0.0, dn_refs[p][0])

        gbuf[...] = gbuf[...] * _silu_grad(acc)
        g_cur = gbuf[0:ts, :]
        for j in range(kw):
            dw_ref[j:j + 1, :] += jnp.sum(g_cur * buf[HALO - (kw - 1) + j:HALO - (kw - 1) + j + ts, :], axis=0, keepdims=True)
        du = w[0:1, :] * gbuf[kw - 1:kw - 1 + ts, :]
        for j in range(1, kw):
            du = du + w[j:j + 1, :] * gbuf[kw - 1 - j:kw - 1 - j + ts, :]
        du_ref[0] = du.astype(BF16)

    return pl.pallas_call(
        body, name="qkv_conv_bwd", grid=(width // tc, bsz, n_s),
        in_specs=[_cur(ts, tc, 0), _prev(ts, tc, 0), _next(ts, tc, 0, seq)]
        + [part_cur(p) for p in range(n_parts)] + [part_next(p) for p in range(n_parts)] + [_conv_w_spec(kw, tc)],
        out_specs=[_cur(ts, tc, 0), _conv_w_spec(kw, tc)],
        out_shape=[jax.ShapeDtypeStruct((bsz, seq, width), BF16), jax.ShapeDtypeStruct((kw, width), F32)],
        scratch_shapes=[pltpu.VMEM((HALO + ts + HALO, tc), F32), pltpu.VMEM((ts + HALO, tc), F32)],
        compiler_params=_params(),
    )(proj3, proj3, proj3, *dparts, *dparts, w)


def _sc_fwd(proj3, w, off, sw):
    bsz, seq, _ = proj3.shape
    kw = w.shape[0]
    ts, tc = _tile(seq, 256, 8), _tile(sw, 512, LANES)
    ob, oc, oh = off // tc, (off + sw) // tc, (off + 2 * sw) // tc

    def body(b_ref, c_ref, cp_ref, h_ref, hp_ref, w_ref, o_ref, buf):
        s = pl.program_id(2)
        buf[0:HALO, :] = jnp.where(s == 0, 0.0, cp_ref[0] * hp_ref[0])
        buf[HALO:HALO + ts, :] = c_ref[0] * h_ref[0]
        o_ref[0] = (b_ref[0] * _conv_taps(w_ref[...], buf, kw, ts)).astype(BF16)

    return pl.pallas_call(
        body, name="sc_fwd", grid=(sw // tc, bsz, seq // ts),
        in_specs=[_cur(ts, tc, ob), _cur(ts, tc, oc), _prev(ts, tc, oc), _cur(ts, tc, oh), _prev(ts, tc, oh),
                  _conv_w_spec(kw, tc)],
        out_specs=_cur(ts, tc, 0), out_shape=jax.ShapeDtypeStruct((bsz, seq, sw), BF16),
        scratch_shapes=[pltpu.VMEM((HALO + ts, tc), F32)], compiler_params=_params(),
    )(proj3, proj3, proj3, proj3, proj3, w)


def _sc_bwd(proj3, dmixed3, w, off, sw, d_off):
    bsz, seq, _ = proj3.shape
    kw = w.shape[0]
    ts, tc = _tile(seq, 256, 8), _tile(sw, 512, LANES)
    n_s = seq // ts
    ob, oc, oh, od = off // tc, (off + sw) // tc, (off + 2 * sw) // tc, d_off // tc

    def body(d_ref, dn_ref, b_ref, bn_ref, c_ref, cp_ref, h_ref, hp_ref, w_ref,
             db_ref, dc_ref, dh_ref, dw_ref, buf, gbuf):
        b, s = pl.program_id(1), pl.program_id(2)

        @pl.when((b == 0) & (s == 0))
        def _():
            dw_ref[...] = jnp.zeros_like(dw_ref)

        w = w_ref[...]
        cc, hh = c_ref[0], h_ref[0]
        buf[0:HALO, :] = jnp.where(s == 0, 0.0, cp_ref[0] * hp_ref[0])
        buf[HALO:HALO + ts, :] = cc * hh
        dout = d_ref[0]
        db_ref[0] = (dout * _conv_taps(w, buf, kw, ts)).astype(BF16)
        g_cur = dout * b_ref[0]
        gbuf[0:ts, :] = g_cur
        gbuf[ts:, :] = jnp.where(s == n_s - 1, 0.0, dn_ref[0] * bn_ref[0])
        for j in range(kw):
            dw_ref[j:j + 1, :] += jnp.sum(g_cur * buf[HALO - (kw - 1) + j:HALO - (kw - 1) + j + ts, :], axis=0, keepdims=True)
        dp = w[0:1, :] * gbuf[kw - 1:kw - 1 + ts, :]
        for j in range(1, kw):
            dp = dp + w[j:j + 1, :] * gbuf[kw - 1 - j:kw - 1 - j + ts, :]
        dc_ref[0] = (dp * hh).astype(BF16)
        dh_ref[0] = (dp * cc).astype(BF16)

    out = jax.ShapeDtypeStruct((bsz, seq, sw), BF16)
    return pl.pallas_call(
        body, name="sc_bwd", grid=(sw // tc, bsz, n_s),
        in_specs=[_cur(ts, tc, od), _next(ts, tc, od, seq), _cur(ts, tc, ob), _next(ts, tc, ob, seq),
                  _cur(ts, tc, oc), _prev(ts, tc, oc), _cur(ts, tc, oh), _prev(ts, tc, oh), _conv_w_spec(kw, tc)],
        out_specs=[_cur(ts, tc, 0)] * 3 + [_conv_w_spec(kw, tc)],
        out_shape=[out, out, out, jax.ShapeDtypeStruct((kw, sw), F32)],
        scratch_shapes=[pltpu.VMEM((HALO + ts, tc), F32), pltpu.VMEM((ts + HALO, tc), F32)],
        compiler_params=_params(),
    )(dmixed3, dmixed3, proj3, proj3, proj3, proj3, proj3, proj3, w)


def _tri_ones(lower):
    i = lax.broadcasted_iota(jnp.int32, (CHUNK, CHUNK), 0)
    j = lax.broadcasted_iota(jnp.int32, (CHUNK, CHUNK), 1)
    return jnp.where((i >= j) if lower else (j >= i), 1.0, 0.0).astype(F32)


def _gates_fwd(proj_ab, a_log_pad, dt_pad, heads):
    t = proj_ab.shape[0]
    gw = heads * LANES

    def body(ab_ref, al_ref, dt_ref, gc_ref, beta_ref):
        ab = ab_ref[...]
        g = -jnp.exp(al_ref[...]) * _softplus(ab + dt_ref[...])
        gc = jnp.dot(_tri_ones(True), g, precision=EXACT_SUM, preferred_element_type=F32)
        beta = _sigmoid(ab)
        for h in range(heads):
            gc_ref[:, h * LANES:(h + 1) * LANES] = jnp.broadcast_to(gc[:, h:h + 1], (CHUNK, LANES))
            beta_ref[:, h * LANES:(h + 1) * LANES] = jnp.broadcast_to(beta[:, heads + h:heads + h + 1], (CHUNK, LANES))

    return pl.pallas_call(
        body, name="gates_fwd", grid=(t // CHUNK,),
        in_specs=[_row_spec(CHUNK, LANES), _vec_spec(LANES), _vec_spec(LANES)],
        out_specs=[_row_spec(CHUNK, gw), _row_spec(CHUNK, gw)],
        out_shape=[jax.ShapeDtypeStruct((t, gw), F32)] * 2, compiler_params=_params(),
    )(proj_ab, a_log_pad, dt_pad)


def _gates_bwd(proj_ab, a_log_pad, dt_pad, dgc_b, dbeta_b, heads):
    t = proj_ab.shape[0]
    gw = heads * LANES

    def body(ab_ref, al_ref, dt_ref, dgc_ref, dbeta_ref, dab_ref, dal_ref, ddt_ref):
        i = pl.program_id(0)

        @pl.when(i == 0)
        def _():
            dal_ref[...] = jnp.zeros_like(dal_ref)
            ddt_ref[...] = jnp.zeros_like(ddt_ref)

        lane = lax.broadcasted_iota(jnp.int32, (CHUNK, LANES), 1)
        dgc = jnp.zeros((CHUNK, LANES), F32)
        dbeta = jnp.zeros((CHUNK, LANES), F32)
        for h in range(heads):
            dgc = jnp.where(lane == h, dgc_ref[:, h * LANES:(h + 1) * LANES], dgc)
            dbeta = jnp.where(lane == heads + h, dbeta_ref[:, h * LANES:(h + 1) * LANES], dbeta)
        dg = jnp.dot(_tri_ones(False), dgc, precision=EXACT_SUM, preferred_element_type=F32)
        ab = ab_ref[...]
        z = ab + dt_ref[...]
        ea = jnp.exp(al_ref[...])
        da = dg * (-ea) * _sigmoid(z)
        beta = _sigmoid(ab)
        db = dbeta * beta * (1.0 - beta)
        dab_ref[...] = jnp.where(lane < heads, da, jnp.where(lane < 2 * heads, db, 0.0)).astype(BF16)
        da_m = jnp.where(lane < heads, da, 0.0)
        ddt_ref[...] += jnp.sum(da_m, axis=0, keepdims=True)
        dal_ref[...] += jnp.sum(jnp.where(lane < heads, dg * (-ea) * _softplus(z), 0.0), axis=0, keepdims=True)

    return pl.pallas_call(
        body, name="gates_bwd", grid=(t // CHUNK,),
        in_specs=[_row_spec(CHUNK, LANES), _vec_spec(LANES), _vec_spec(LANES), _row_spec(CHUNK, gw), _row_spec(CHUNK, gw)],
        out_specs=[_row_spec(CHUNK, LANES), _vec_spec(LANES), _vec_spec(LANES)],
        out_shape=[jax.ShapeDtypeStruct((t, LANES), BF16), jax.ShapeDtypeStruct((1, LANES), F32),
                   jax.ShapeDtypeStruct((1, LANES), F32)],
        compiler_params=_params(),
    )(proj_ab, a_log_pad, dt_pad, dgc_b, dbeta_b)


def _dot(a, b, dims, hi=False):
    if hi:
        return lax.dot_general(a, b, (dims, ((), ())), precision=HI, preferred_element_type=F32)
    return lax.dot_general(a.astype(BF16), b.astype(BF16), (dims, ((), ())), preferred_element_type=F32)


NN = ((1,), (0,))
NT = ((1,), (1,))
TN = ((0,), (0,))


def _each(f, *lists):
    return [f(*xs) for xs in zip(*lists)]


def _dots(a, b, dims, hi=False):
    return _each(lambda x, y: _dot(x, y, dims, hi=hi), a, b)


def _unit_lower_inverse(ms):
    i = lax.broadcasted_iota(jnp.int32, (CHUNK, CHUNK), 0)
    j = lax.broadcasted_iota(jnp.int32, (CHUNK, CHUNK), 1)
    eye = jnp.where(i == j, 1.0, 0.0).astype(F32)
    ts = [eye - jnp.where(jnp.right_shift(i, 1) == jnp.right_shift(j, 1), m, 0.0) for m in ms]
    shift = 1
    while (1 << shift) < CHUNK:
        same_pair = jnp.right_shift(i, shift + 1) == jnp.right_shift(j, shift + 1)
        other_half = jnp.right_shift(i, shift) != jnp.right_shift(j, shift)
        offs = [jnp.where(same_pair & other_half, m, 0.0) for m in ms]
        corr = _dots(_dots(ts, offs, NN, hi=True), ts, NN, hi=True)
        ts = _each(lambda t, c: t - c, ts, corr)
        shift += 1
    return ts


def _chunk_local(qrs, krs, vs, gcbs, betabs, head_dim):
    i = lax.broadcasted_iota(jnp.int32, (CHUNK, CHUNK), 0)
    j = lax.broadcasted_iota(jnp.int32, (CHUNK, CHUNK), 1)
    scale = head_dim ** -0.5
    rqs = [lax.rsqrt(jnp.sum(q * q, axis=-1, keepdims=True) + L2_EPS) for q in qrs]
    rks = [lax.rsqrt(jnp.sum(k * k, axis=-1, keepdims=True) + L2_EPS) for k in krs]
    qhs = _each(lambda a, r: a * r, qrs, rqs)
    ks = _each(lambda a, r: a * r, krs, rks)
    qs = [a * scale for a in qhs]
    decays = [jnp.exp(jnp.where(i >= j, g[:, 0:CHUNK] - g.T[0:CHUNK, :], -jnp.inf)) for g in gcbs]
    kks = _dots(ks, ks, NT)
    qks = _dots(qs, ks, NT)
    ms = _each(lambda b, kk, d: jnp.where(i > j, b[:, 0:CHUNK] * kk * d, 0.0), betabs, kks, decays)
    tinvs = _unit_lower_inverse(ms)
    egs = [jnp.exp(g) for g in gcbs]
    rhs_ws = _each(lambda k, b, e: k * b * e, ks, betabs, egs)
    us = _dots(tinvs, _each(lambda v, b: v * b, vs, betabs), NN, hi=True)
    ws = _dots(tinvs, rhs_ws, NN, hi=True)
    out = []
    for h in range(len(qrs)):
        g_last = gcbs[h][CHUNK - 1:CHUNK, :]
        e_last = jnp.exp(g_last - gcbs[h])
        out.append(dict(rq=rqs[h], rk=rks[h], qh=qhs[h], q=qs[h], k=ks[h], decay=decays[h],
                        beta_col=betabs[h][:, 0:CHUNK], kk=kks[h], m=ms[h], tinv=tinvs[h], eg=egs[h], rhs_w=rhs_ws[h],
                        u=us[h], w=ws[h], p=qks[h] * decays[h], qd=qs[h] * egs[h], kd=ks[h] * e_last, e_last=e_last,
                        gl=jnp.exp(g_last), scale=scale, strict=i > j, incl=i >= j))
    return out


def _field(dicts, name):
    return [d[name] for d in dicts]


GDN_HEAD_GROUP = 16


def _gdn_specs(n_chunks, heads, reverse):
    hg = min(GDN_HEAD_GROUP, heads)
    assert heads % hg == 0

    def cidx(c):
        return (n_chunks - 1 - c) if reverse else c

    def tok(off):
        return pl.BlockSpec((CHUNK, hg * LANES), lambda b, h, c: (b * n_chunks + cidx(c), off // hg + h))

    state = pl.BlockSpec((None, hg, LANES, LANES), lambda b, h, c: (b * n_chunks + cidx(c), h, 0, 0))
    return hg, tok, state


def _gdn_fwd(qkv_act, gcb, betab, bsz, heads):
    t = qkv_act.shape[0]
    n_chunks = t // bsz // CHUNK
    hg, tok, state = _gdn_specs(n_chunks, heads, False)

    def body(q_ref, k_ref, v_ref, gc_ref, beta_ref, o_ref, s_ref, st):
        @pl.when(pl.program_id(2) == 0)
        def _():
            st[...] = jnp.zeros_like(st)

        sls = [slice(hh * LANES, (hh + 1) * LANES) for hh in range(hg)]
        loc = _chunk_local(*[[r[:, sl] for sl in sls] for r in (q_ref, k_ref, v_ref, gc_ref, beta_ref)], LANES)
        s0 = [st[hh] for hh in range(hg)]
        v_new = _each(lambda u, ws: u - ws, _field(loc, "u"), _dots(_field(loc, "w"), s0, NN))
        o_state = _dots(_field(loc, "qd"), s0, NN)
        o_local = _dots(_field(loc, "p"), v_new, NN)
        s_add = _dots(_field(loc, "kd"), v_new, TN)
        for hh in range(hg):
            o_ref[:, sls[hh]] = o_state[hh] + o_local[hh]
            s_ref[hh] = s0[hh]
            st[hh] = s0[hh] * loc[hh]["gl"] + s_add[hh]

    return pl.pallas_call(
        body, name="gdn_fwd", grid=(bsz, heads // hg, n_chunks),
        in_specs=[tok(0), tok(heads), tok(2 * heads), tok(0), tok(0)],
        out_specs=[tok(0), state],
        out_shape=[jax.ShapeDtypeStruct((t, heads * LANES), F32),
                   jax.ShapeDtypeStruct((bsz * n_chunks, heads, LANES, LANES), F32)],
        scratch_shapes=[pltpu.VMEM((hg, LANES, LANES), F32)], compiler_params=_params(),
    )(qkv_act, qkv_act, qkv_act, gcb, betab)


def _gdn_bwd(qkv_act, gcb, betab, states, d_o, bsz, heads):
    t = qkv_act.shape[0]
    n_chunks = t // bsz // CHUNK
    hg, tok, state = _gdn_specs(n_chunks, heads, True)

    def rowsum(a):
        return jnp.sum(a, axis=-1, keepdims=True)

    def finish_head(sl, L, v, betab, d_qd, d_kd, d_gl, d_p, d_m, d_rhs_u, d_rhs_w, d_q, d_k,
                    dq_ref, dk_ref, dv_ref, dgc_ref, dbeta_ref):
        k, decay = L["k"], L["decay"]
        dv_ref[:, sl] = betab * d_rhs_u
        e = d_m * L["m"] + d_p * L["p"]
        d_beta = rowsum(d_m * L["kk"] * decay) + rowsum(d_rhs_u * v) + rowsum(d_rhs_w * k * L["eg"])
        s_kd = rowsum(d_kd * L["kd"])
        d_gc = (rowsum(e) - rowsum(e.T) + rowsum(d_rhs_w * L["rhs_w"]) + rowsum(d_qd * L["qd"]) - s_kd)
        row = lax.broadcasted_iota(jnp.int32, (CHUNK, 1), 0)
        d_gc = d_gc + jnp.where(row == CHUNK - 1, jnp.sum(s_kd) + d_gl * jnp.sum(L["gl"][:, 0:1]), 0.0)
        dgc_ref[:, sl] = jnp.broadcast_to(d_gc, (CHUNK, LANES))
        dbeta_ref[:, sl] = jnp.broadcast_to(d_beta, (CHUNK, LANES))
        d_qh = d_q * L["scale"]
        dq_ref[:, sl] = L["rq"] * (d_qh - L["qh"] * rowsum(d_qh * L["qh"]))
        dk_ref[:, sl] = L["rk"] * (d_k - k * rowsum(d_k * k))

    def body(q_ref, k_ref, v_ref, gc_ref, beta_ref, s_ref, do_ref, dq_ref, dk_ref, dv_ref, dgc_ref, dbeta_ref, dst):
        @pl.when(pl.program_id(2) == 0)
        def _():
            dst[...] = jnp.zeros_like(dst)

        sls = [slice(hh * LANES, (hh + 1) * LANES) for hh in range(hg)]
        vs = [v_ref[:, sl] for sl in sls]
        betabs = [beta_ref[:, sl] for sl in sls]
        loc = _chunk_local([q_ref[:, sl] for sl in sls], [k_ref[:, sl] for sl in sls], vs,
                           [gc_ref[:, sl] for sl in sls], betabs, LANES)
        q, k, u, w, p, tinv, decay, qd, kd, eg = (_field(loc, n) for n in
                                                  ("q", "k", "u", "w", "p", "tinv", "decay", "qd", "kd", "eg"))
        s0 = [s_ref[hh] for hh in range(hg)]
        d_out = [do_ref[:, sl] for sl in sls]
        ds1 = [dst[hh] for hh in range(hg)]
        v_new = _each(lambda a, b: a - b, u, _dots(w, s0, NN))

        d_vnew = _each(lambda a, b: a + b, _dots(p, d_out, TN), _dots(kd, ds1, NN))
        d_qd = _dots(d_out, s0, NT)
        d_kd = _dots(v_new, ds1, NT)
        d_gl = _each(lambda a, b: jnp.sum(a * b), ds1, s0)
        d_p = _each(lambda L, a: jnp.where(L["incl"], a, 0.0), loc, _dots(d_out, v_new, NT))
        d_w = [-a for a in _dots(d_vnew, s0, NT)]
        ds_out, ds_vn = _dots(qd, d_out, TN), _dots(w, d_vnew, TN)
        for hh in range(hg):
            dst[hh] = ds_out[hh] + ds1[hh] * loc[hh]["gl"] - ds_vn[hh]

        d_rhs_u = _dots(tinv, d_vnew, TN, hi=True)
        d_rhs_w = _dots(tinv, d_w, TN, hi=True)
        d_a = _each(lambda a, b: -(a + b), _dots(d_rhs_u, u, NT, hi=True), _dots(d_rhs_w, w, NT, hi=True))
        d_m = _each(lambda L, a: jnp.where(L["strict"], a, 0.0), loc, d_a)
        g_kk = _each(lambda L, a: a * L["beta_col"] * L["decay"], loc, d_m)
        h_qk = _each(lambda a, d: a * d, d_p, decay)

        d_q = _each(lambda a, e, b: a + e * b, _dots(h_qk, k, NN), eg, d_qd)
        d_k = _each(lambda a, b, c, L, bb, rw, dk: a + b + c + bb * L["eg"] * rw + L["e_last"] * dk,
                    _dots(g_kk, k, NN), _dots(g_kk, k, TN), _dots(h_qk, q, TN), loc, betabs, d_rhs_w, d_kd)
        for hh in range(hg):
            finish_head(sls[hh], loc[hh], vs[hh], betabs[hh], d_qd[hh], d_kd[hh], d_gl[hh], d_p[hh], d_m[hh],
                        d_rhs_u[hh], d_rhs_w[hh], d_q[hh], d_k[hh], dq_ref, dk_ref, dv_ref, dgc_ref, dbeta_ref)

    tok_shape = jax.ShapeDtypeStruct((t, heads * LANES), F32)
    return pl.pallas_call(
        body, name="gdn_bwd", grid=(bsz, heads // hg, n_chunks),
        in_specs=[tok(0), tok(heads), tok(2 * heads), tok(0), tok(0), state, tok(0)],
        out_specs=[tok(0)] * 5,
        out_shape=[tok_shape] * 5,
        scratch_shapes=[pltpu.VMEM((hg, LANES, LANES), F32)], compiler_params=_params(),
    )(qkv_act, qkv_act, qkv_act, gcb, betab, states, d_o)


def _gdn_out_fwd(o, proj, gw_norm, heads, z_off, after):
    t = o.shape[0]
    ts = _tile(t, 512, 16)
    zb = z_off // LANES

    def body(o_ref, z_ref, w_ref, after_ref, out_ref):
        z = z_ref[...]
        out_ref[...] = (_rms_fwd(o_ref[...], w_ref[...]) * (z * _sigmoid(z))).astype(BF16)

    return pl.pallas_call(
        body, name="gdn_out_fwd", grid=(t // ts, heads),
        in_specs=[pl.BlockSpec((ts, LANES), lambda i, h: (i, h)), pl.BlockSpec((ts, LANES), lambda i, h: (i, zb + h)),
                  pl.BlockSpec((1, LANES), lambda i, h: (0, 0)), ANY],
        out_specs=pl.BlockSpec((ts, LANES), lambda i, h: (i, h)),
        out_shape=jax.ShapeDtypeStruct((t, heads * LANES), BF16), compiler_params=_params(),
    )(o, proj, gw_norm, after)


def _gdn_out_bwd(dmixed, o, proj, gw_norm, heads, z_off, after):
    t = o.shape[0]
    ts = _tile(t, 512, 16)
    zb = z_off // LANES

    def body(d_ref, o_ref, z_ref, w_ref, after_ref, do_ref, dz_ref, dw_ref):
        @pl.when((pl.program_id(0) == 0) & (pl.program_id(1) == 0))
        def _():
            dw_ref[...] = jnp.zeros_like(dw_ref)

        d, oo, z, w = d_ref[...], o_ref[...], z_ref[...], w_ref[...]
        on = _rms_fwd(oo, w)
        dz_ref[...] = (d * on * _silu_grad(z)).astype(BF16)
        d_o, d_w = _rms_bwd(d * (z * _sigmoid(z)), oo, w)
        do_ref[...] = d_o
        dw_ref[...] += d_w

    blk = pl.BlockSpec((ts, LANES), lambda i, h: (i, h))
    vec = pl.BlockSpec((1, LANES), lambda i, h: (0, 0))
    return pl.pallas_call(
        body, name="gdn_out_bwd", grid=(t // ts, heads),
        in_specs=[blk, blk, pl.BlockSpec((ts, LANES), lambda i, h: (i, zb + h)), vec, ANY],
        out_specs=[blk, blk, vec],
        out_shape=[jax.ShapeDtypeStruct((t, heads * LANES), F32), jax.ShapeDtypeStruct((t, heads * LANES), BF16),
                   jax.ShapeDtypeStruct((1, LANES), F32)],
        compiler_params=_params(),
    )(dmixed, o, proj, gw_norm, after)


def _place():
    x, y, c = lax.axis_index("x"), lax.axis_index("y"), lax.axis_index("c")
    return x, y, c, [(1 - x, y), (x, 1 - y), (1 - x, 1 - y)]


def _aligned(start, align):
    return start if isinstance(start, int) else pl.multiple_of(start, align)


class _Layout:
    def __init__(self, kind, shard_shape):
        self.kind = kind
        self.r, self.c = shard_shape

    def full_shape(self):
        r, c = self.r, self.c
        return {"major": (N_CHIPS, r, c), "rows": (N_CHIPS * r, c), "cols": (r, N_CHIPS * c)}[self.kind]

    def region(self, ref, j, half=None):
        r, c = self.r, self.c
        r0, nr = (0, r) if half is None else (half * (r // 2), r // 2)
        if self.kind == "major":
            return ref.at[j, pl.ds(_aligned(r0, 16), nr), :]
        if self.kind == "rows":
            return ref.at[pl.ds(_aligned(j * r + r0, 16), nr), :]
        return ref.at[pl.ds(_aligned(r0, 16), nr), pl.ds(_aligned(j * c, LANES), c)]

    def block_spec(self, tr, where):
        r, c = self.r, self.c
        if self.kind == "major":
            return pl.BlockSpec((None, tr, c), lambda *a: (where(*a)[0], where(*a)[1], 0))
        if self.kind == "rows":
            return pl.BlockSpec((tr, c), lambda *a: (where(*a)[0] * (r // tr) + where(*a)[1], 0))
        return pl.BlockSpec((tr, c), lambda *a: (where(*a)[1], where(*a)[0]))


def _remote(src, dst, send_sem, recv_sem, dev):
    return pltpu.make_async_remote_copy(src_ref=src, dst_ref=dst, send_sem=send_sem, recv_sem=recv_sem,
                                        device_id=dev, device_id_type=MESH)


def _all_gather(big, layouts, small):
    nb, ns = len(big), len(small)
    n_remote = 6 * nb + 3 * ns

    def body(*refs):
        ins, outs = refs[:nb + ns], refs[nb + ns:2 * (nb + ns)]
        send_sems, recv_sems, local_sems = refs[2 * (nb + ns):]
        x, y, c, chips = _place()
        j = 2 * x + y
        local = []
        for i in range(ns):
            local.append(pltpu.make_async_copy(ins[nb + i], outs[nb + i].at[j], local_sems.at[i]))
        for cp in local:
            cp.start()
        sends = []
        for i in range(nb):
            for p, (px, py) in enumerate(chips):
                k = 3 * i + p
                mine = layouts[i].region(outs[i], j, c)
                sends.append(_remote(mine, mine, send_sems.at[k], recv_sems.at[k], (px, py, c)))
        for i in range(ns):
            for p, (px, py) in enumerate(chips):
                k = 6 * nb + 3 * i + p
                sends.append(_remote(ins[nb + i], outs[nb + i].at[j], send_sems.at[k], recv_sems.at[k], (px, py, c)))
        for cp in sends:
            cp.start()
        for i in range(nb):
            for p, (px, py) in enumerate(chips):
                k, jp = 3 * i + p, 2 * px + py
                got = layouts[i].region(outs[i], jp, c)
                _remote(got, got, send_sems.at[k], recv_sems.at[k], (px, py, c)).wait_recv()
                fwd = _remote(got, got, send_sems.at[3 * nb + k], recv_sems.at[3 * nb + k], (x, y, 1 - c))
                fwd.start()
                sends.append(fwd)
        for i in range(ns):
            for p, (px, py) in enumerate(chips):
                k, jp = 6 * nb + 3 * i + p, 2 * px + py
                _remote(ins[nb + i], outs[nb + i].at[jp], send_sems.at[k], recv_sems.at[k], (px, py, c)).wait_recv()
        for i in range(nb):
            for p, (px, py) in enumerate(chips):
                k, jp = 3 * nb + 3 * i + p, 2 * px + py
                got = layouts[i].region(outs[i], jp, 1 - c)
                _remote(got, got, send_sems.at[k], recv_sems.at[k], (x, y, 1 - c)).wait_recv()
        for cp in sends:
            cp.wait_send()
        for cp in local:
            cp.wait()

    out_shape = [jax.ShapeDtypeStruct(lay.full_shape(), BF16) for lay in layouts]
    out_shape += [jax.ShapeDtypeStruct((N_CHIPS,) + s.shape, F32) for s in small]
    return pl.pallas_call(
        body, name="all_gather_weights", in_specs=[ANY] * (nb + ns), out_specs=[ANY] * (nb + ns), out_shape=out_shape,
        input_output_aliases={i: i for i in range(nb)},
        scratch_shapes=[pltpu.SemaphoreType.DMA((n_remote,)), pltpu.SemaphoreType.DMA((n_remote,)),
                        pltpu.SemaphoreType.DMA((ns,))],
        compiler_params=_params(),
    )(*big, *small)


HBM = pl.BlockSpec(memory_space=pltpu.HBM)
SEM = pl.BlockSpec(memory_space=pltpu.SEMAPHORE)
SPLIT_COPY = pltpu.CompilerParams(has_side_effects=pltpu.SideEffectType.DATAFLOW_SIDE_EFFECTING)


def _in_hbm(a):
    return pltpu.with_memory_space_constraint(a, pltpu.HBM)


def _split_copy_start(arrays, plan, n_copies, name, after=None):
    na = len(arrays)
    order_only = [] if after is None else [after]

    def body(*refs):
        base = na + len(order_only)
        send_sems, recv_sems = refs[base], refs[base + 1]
        thru, token = refs[base + 2:base + 2 + na], refs[base + 2 + na]
        for k, (src, dst, _, dev) in enumerate(plan(thru, *_place())):
            _remote(src, dst, send_sems.at[k], recv_sems.at[k], dev).start()
        token[...] = jnp.zeros_like(token)

    outs = pl.pallas_call(
        body, name=name, in_specs=[HBM] * na + [ANY] * len(order_only),
        out_specs=[SEM, SEM] + [HBM] * na + [pl.BlockSpec(memory_space=pltpu.VMEM)],
        out_shape=[pltpu.SemaphoreType.DMA((n_copies,)), pltpu.SemaphoreType.DMA((n_copies,))]
        + [pltpu.HBM(a.shape, a.dtype) for a in arrays] + [jax.ShapeDtypeStruct((8, LANES), F32)],
        input_output_aliases={i: 2 + i for i in range(na)}, compiler_params=SPLIT_COPY,
    )(*[_in_hbm(a) for a in arrays], *order_only)
    return (outs[0], outs[1]), list(outs[2:2 + na]), outs[2 + na]


def _split_copy_wait(sems, arrays, plan, name, after):
    na = len(arrays)

    def body(*refs):
        send_sems, recv_sems = refs[na], refs[na + 1]
        thru = refs[na + 3:]
        for k, (src, _, landing, dev) in enumerate(plan(thru, *_place())):
            cp = _remote(src, landing, send_sems.at[k], recv_sems.at[k], dev)
            cp.wait_send()
            cp.wait_recv()

    return list(pl.pallas_call(
        body, name=name, in_specs=[HBM] * na + [SEM, SEM, ANY], out_specs=[HBM] * na,
        out_shape=[pltpu.HBM(a.shape, a.dtype) for a in arrays],
        input_output_aliases={i: i for i in range(na)}, compiler_params=SPLIT_COPY,
    )(*arrays, *sems, after))


def _gather_plan(layouts):
    def plan(bufs, x, y, c, chips):
        copies = []
        for buf, lay in zip(bufs, layouts):
            mine = lay.region(buf, 2 * x + y, c)
            copies += [(mine, mine, lay.region(buf, 2 * px + py, c), (px, py, c)) for px, py in chips]
        return copies
    return plan


def _forward_plan(layouts):
    def plan(bufs, x, y, c, chips):
        copies = []
        for buf, lay in zip(bufs, layouts):
            for px, py in chips:
                got = lay.region(buf, 2 * px + py, c)
                copies.append((got, got, lay.region(buf, 2 * px + py, 1 - c), (x, y, 1 - c)))
        return copies
    return plan


def _halves_plan(layouts):
    def plan(arrays, x, y, c, chips):
        nw = len(layouts)
        copies = []
        for i, lay in enumerate(layouts):
            for j in range(N_CHIPS):
                land = arrays[nw + i].at[j]
                copies.append((lay.region(arrays[i], j, 1 - c), land, land, (x, y, 1 - c)))
        return copies
    return plan


def _partials_plan(nw):
    def plan(arrays, x, y, c, chips):
        copies = []
        for i in range(nw):
            for p, (px, py) in enumerate(chips):
                land = arrays[nw + i].at[p]
                copies.append((arrays[i].at[2 * px + py], land, land, (px, py, c)))
        return copies
    return plan


def _join_plan(nw):
    def plan(arrays, x, y, c, chips):
        return [(arrays[i], arrays[nw + i], arrays[nw + i], (x, y, 1 - c)) for i in range(nw)]
    return plan


def _forward_to_sibling(bufs, layouts):
    nb = len(bufs)

    def body(*refs):
        outs = refs[nb:2 * nb]
        send_sems, recv_sems = refs[2 * nb:]
        x, y, c, chips = _place()
        cps = []
        for i in range(nb):
            for p, (px, py) in enumerate(chips):
                got = layouts[i].region(outs[i], 2 * px + py, c)
                cps.append(_remote(got, got, send_sems.at[3 * i + p], recv_sems.at[3 * i + p], (x, y, 1 - c)))
        for cp in cps:
            cp.start()
        for i in range(nb):
            for p, (px, py) in enumerate(chips):
                theirs = layouts[i].region(outs[i], 2 * px + py, 1 - c)
                _remote(theirs, theirs, send_sems.at[3 * i + p], recv_sems.at[3 * i + p], (x, y, 1 - c)).wait_recv()
        for cp in cps:
            cp.wait_send()

    return pl.pallas_call(
        body, name="gather_forward_to_sibling", in_specs=[ANY] * nb, out_specs=[ANY] * nb,
        out_shape=[jax.ShapeDtypeStruct(b.shape, b.dtype) for b in bufs],
        input_output_aliases={i: i for i in range(nb)},
        scratch_shapes=[pltpu.SemaphoreType.DMA((3 * nb,)), pltpu.SemaphoreType.DMA((3 * nb,))],
        compiler_params=_params(),
    )(*bufs)


def _halves_to_sibling(grads, layouts, name):
    nw = len(grads)

    def body(*refs):
        ins, gots = refs[:nw], refs[nw:2 * nw]
        send_sems, recv_sems = refs[2 * nw:]
        x, y, c, _ = _place()
        cps = []
        for i in range(nw):
            for j in range(N_CHIPS):
                k = N_CHIPS * i + j
                cps.append(_remote(layouts[i].region(ins[i], j, 1 - c), gots[i].at[j],
                                   send_sems.at[k], recv_sems.at[k], (x, y, 1 - c)))
        for cp in cps:
            cp.start()
        for cp in cps:
            cp.wait()

    half = [jax.ShapeDtypeStruct((N_CHIPS, lay.r // 2, lay.c), BF16) for lay in layouts]
    return pl.pallas_call(
        body, name=name, in_specs=[ANY] * nw, out_specs=[ANY] * nw, out_shape=half,
        scratch_shapes=[pltpu.SemaphoreType.DMA((N_CHIPS * nw,)), pltpu.SemaphoreType.DMA((N_CHIPS * nw,))],
        compiler_params=_params(),
    )(*grads)


def _chip_sum(grad, got, layout, core, name):
    n, hr, c = got.shape
    tr = _tile(hr, 256, 16)
    nb = hr // tr

    def body(core_ref, a_ref, b_ref, o_ref):
        o_ref[...] = (a_ref[...].astype(F32) + b_ref[...].astype(F32)).astype(BF16)

    spec = pl.BlockSpec((None, tr, c), lambda j, i, core_ref: (j, i, 0))
    return pl.pallas_call(
        body, name=name,
        grid_spec=pltpu.PrefetchScalarGridSpec(
            num_scalar_prefetch=1, grid=(n, nb),
            in_specs=[layout.block_spec(tr, lambda j, i, core_ref: (j, core_ref[0] * nb + i)), spec],
            out_specs=spec),
        out_shape=jax.ShapeDtypeStruct((n, hr, c), BF16), compiler_params=_params(),
    )(core, grad, got)


def _halves_start(grads, layouts, name, after=None):
    lands = [lax.empty((N_CHIPS, lay.r // 2, lay.c), BF16) for lay in layouts]
    return _split_copy_start(list(grads) + lands, _halves_plan(layouts), N_CHIPS * len(grads), name, after)


def _partials_start(parts, name, after=None):
    lands = [lax.empty((3,) + p.shape[1:], BF16) for p in parts]
    return _split_copy_start(list(parts) + lands, _partials_plan(len(parts)), 3 * len(parts), name, after)


def _shard_sum(parts, got, chip, name):
    _, r, c = parts.shape
    tr = _tile(r, 256, 16)

    def body(chip_ref, o_ref, g_ref, out_ref):
        acc = o_ref[...].astype(F32)
        for p in range(3):
            acc = acc + g_ref[p].astype(F32)
        out_ref[...] = acc

    return pl.pallas_call(
        body, name=name,
        grid_spec=pltpu.PrefetchScalarGridSpec(
            num_scalar_prefetch=1, grid=(r // tr,),
            in_specs=[pl.BlockSpec((None, tr, c), lambda i, chip_ref: (chip_ref[0], i, 0)),
                      pl.BlockSpec((3, tr, c), lambda i, chip_ref: (0, i, 0))],
            out_specs=pl.BlockSpec((tr, c), lambda i, chip_ref: (i, 0))),
        out_shape=jax.ShapeDtypeStruct((r, c), F32), compiler_params=_params(),
    )(chip, parts, got)


def _join_start(halves, name):
    lands = [lax.empty(h.shape, F32) for h in halves]
    return _split_copy_start(list(halves) + lands, _join_plan(len(halves)), len(halves), name)


def _adamw_rows(w, g_half, m, v, half, filled, after, name):
    r, c = w.shape
    hr = r // 2
    tr = _tile(hr, 128, 8)
    nb = hr // tr
    c1 = 1.0 - ADAM_B1 ** ADAM_STEP
    c2 = 1.0 - ADAM_B2 ** ADAM_STEP
    n_prev = 0 if filled is None else 4

    def body(half_ref, w_ref, gin_ref, m_ref, v_ref, *rest):
        g_ref, d_ref, nm_ref, nv_ref = rest[1 + n_prev:]
        gg = gin_ref[...]
        nm = ADAM_B1 * m_ref[...] + (1.0 - ADAM_B1) * gg
        nv = ADAM_B2 * v_ref[...] + (1.0 - ADAM_B2) * jnp.square(gg)
        m_hat = nm / c1
        v_hat = nv / c2
        g_ref[...] = gg
        d_ref[...] = -ADAM_LR * (m_hat / (jnp.sqrt(v_hat) + ADAM_EPS) + ADAM_WD * w_ref[...])
        nm_ref[...] = nm
        nv_ref[...] = nv

    full = pl.BlockSpec((tr, c), lambda i, half_ref: (half_ref[0] * nb + i, 0))
    part = pl.BlockSpec((tr, c), lambda i, half_ref: (i, 0))
    return pl.pallas_call(
        body, name=name,
        grid_spec=pltpu.PrefetchScalarGridSpec(
            num_scalar_prefetch=1, grid=(nb,), in_specs=[full, part, full, full] + [ANY] * (1 + n_prev),
            out_specs=[full] * 4),
        out_shape=[jax.ShapeDtypeStruct((r, c), F32)] * 4,
        input_output_aliases={6 + k: k for k in range(n_prev)}, compiler_params=_params(),
    )(half, w, g_half, m, v, after, *([] if filled is None else filled))


def _small_all_reduce(buf):
    rows = buf.shape[0]
    n_dev = 8

    def body(b_ref, o_ref, gath, send_sems, recv_sems):
        x, y, c, _ = _place()
        me = 4 * x + 2 * y + c
        gath[me] = b_ref[...]
        cps = []
        for k in range(1, n_dev):
            px, py, pc = (x + (k >> 2)) % 2, (y + ((k >> 1) & 1)) % 2, (c + (k & 1)) % 2
            cps.append(_remote(b_ref, gath.at[me], send_sems.at[k - 1], recv_sems.at[k - 1], (px, py, pc)))
        for cp in cps:
            cp.start()
        for k in range(1, n_dev):
            px, py, pc = (x + (k >> 2)) % 2, (y + ((k >> 1) & 1)) % 2, (c + (k & 1)) % 2
            _remote(b_ref, gath.at[4 * px + 2 * py + pc], send_sems.at[k - 1], recv_sems.at[k - 1], (px, py, pc)).wait_recv()
        for cp in cps:
            cp.wait_send()
        acc = gath[0]
        for dev in range(1, n_dev):
            acc = acc + gath[dev]
        o_ref[...] = acc

    vm = pl.BlockSpec(memory_space=pltpu.VMEM)
    return pl.pallas_call(
        body, name="small_all_reduce", in_specs=[vm], out_specs=vm,
        out_shape=jax.ShapeDtypeStruct((rows, LANES), F32),
        scratch_shapes=[pltpu.VMEM((n_dev, rows, LANES), F32), pltpu.SemaphoreType.DMA((n_dev - 1,)),
                        pltpu.SemaphoreType.DMA((n_dev - 1,))],
        compiler_params=_params(),
    )(buf)


def _pad_lanes(v):
    return jnp.pad(v, ((0, 0), (0, LANES - v.shape[-1])))


def _pack(vectors):
    flat, offs, pos = [], [], 0
    for v in vectors:
        n = v.size
        n_pad = -(-n // LANES) * LANES
        flat.append(jnp.pad(v.reshape(-1), (0, n_pad - n)))
        offs.append((pos, n, v.shape))
        pos += n_pad
    total = -(-pos // (8 * LANES)) * 8 * LANES
    flat.append(jnp.zeros((total - pos,), F32))
    return jnp.concatenate(flat).reshape(-1, LANES), offs


def _unpack(buf, offs):
    flat = buf.reshape(-1)
    return [flat[pos:pos + n].reshape(shape) for pos, n, shape in offs]


def kernel(x, norm_mix_pre, w_in, conv_qkv_w, a_log, dt_bias, gdn_norm_w, conv_sc_w, w_out, norm_mix_post, norm_mlp_pre, w_up, w_down, norm_mlp_post, loss_target, m_norm_mix_pre, m_w_in, m_conv_qkv_w, m_a_log, m_dt_bias, m_gdn_norm_w, m_conv_sc_w, m_w_out, m_norm_mix_post, m_norm_mlp_pre, m_w_up, m_w_down, m_norm_mlp_post, v_norm_mix_pre, v_w_in, v_conv_qkv_w, v_a_log, v_dt_bias, v_gdn_norm_w, v_conv_sc_w, v_w_out, v_norm_mix_post, v_norm_mlp_pre, v_w_up, v_w_down, v_norm_mlp_post):
    bsz, seq, d = x.shape
    t = bsz * seq
    heads, head_dim = a_log.shape[-1], gdn_norm_w.shape[-1]
    assert head_dim == LANES and seq % CHUNK == 0
    gw = heads * head_dim
    sw = conv_sc_w.shape[-1] * N_CHIPS
    ics = w_in.shape[-1]
    main = 4 * gw + 3 * sw
    assert ics * N_CHIPS == main + 2 * heads and 2 * heads <= LANES

    lay_in = _Layout("major", w_in.shape[1:])
    lay_out = _Layout("rows", w_out.shape[1:])
    lay_up = _Layout("cols", w_up.shape[1:])
    lay_down = _Layout("rows", w_down.shape[1:])
    layouts = [lay_in, lay_out, lay_up, lay_down]
    chip = (2 * lax.axis_index("x") + lax.axis_index("y")).astype(jnp.int32).reshape(1)
    core = lax.axis_index("c").astype(jnp.int32).reshape(1)
    shards = [_cast_into_layout(w[0], lay, chip, f"cast_{n}")
              for w, lay, n in zip((w_in, w_out, w_up, w_down), layouts, ("w_in", "w_out", "w_up", "w_down"))]
    win_sh, cq_g, cs_g = _all_gather(shards[:1], layouts[:1], [conv_qkv_w[0], conv_sc_w[0]])
    plan_ou, plan_down = _gather_plan(layouts[1:3]), _gather_plan(layouts[3:])
    ou_sems, ou_bufs, ou_token = _split_copy_start(shards[1:3], plan_ou, 6, "gather_out_up_start", after=cq_g)
    w_main, w_ab = _repack_w_in(win_sh, gw, heads, sw, ou_token)
    conv_q = cq_g.transpose(1, 0, 2).reshape(conv_qkv_w.shape[1], -1)
    conv_s = cs_g.transpose(1, 0, 2).reshape(conv_sc_w.shape[1], -1)

    x2 = x.reshape(t, d)
    tgt2 = loss_target.reshape(t, d)
    xn = _norm_fwd(x2, norm_mix_pre)
    proj = _matmul(xn, w_main, "nn", [F32], "proj_main")
    proj_ab = _matmul(xn, w_ab, "nn", [F32], "proj_ab")
    proj3 = proj.reshape(bsz, seq, main)
    qkv_act = _qkv_conv_fwd(proj3, conv_q, 3 * gw).reshape(t, 3 * gw)
    a_log_pad, dt_pad = _pad_lanes(a_log), _pad_lanes(dt_bias)
    gcb, betab = _gates_fwd(proj_ab, a_log_pad, dt_pad, heads)
    o_raw, states = _gdn_fwd(qkv_act, gcb, betab, bsz, heads)
    ou_bufs = _split_copy_wait(ou_sems, ou_bufs, plan_ou, "gather_out_up_wait", o_raw)
    fwd_plan = _forward_plan(layouts[1:3])
    fwd_sems, ou_bufs, fwd_token = _split_copy_start(ou_bufs, fwd_plan, 6, "forward_out_up_start")
    down_sems, down_bufs, down_token = _split_copy_start(shards[3:], plan_down, 3, "gather_down_start", after=fwd_token)
    gdn_out = _gdn_out_fwd(o_raw, proj, gdn_norm_w, heads, 3 * gw, down_token)
    sc_out = _sc_fwd(proj3, conv_s, 4 * gw, sw).reshape(t, sw)
    mixed = jnp.concatenate([gdn_out, sc_out], axis=1)
    wout_f, wup_f = _split_copy_wait(fwd_sems, ou_bufs, fwd_plan, "forward_out_up_wait", mixed)
    mix = _matmul(mixed, wout_f, "nn", [F32], "mix_out")
    h, hn = _mid_fwd(x2, mix, norm_mix_post, norm_mlp_pre)

    def up_epilogue(acc):
        r = jnp.maximum(acc, 0.0)
        return r, r * r

    relu_up, hid = _matmul(hn, wup_f, "nn", [BF16, BF16], "mlp_up", epilogue=up_epilogue)
    down_bufs = _split_copy_wait(down_sems, down_bufs, plan_down, "gather_down_wait", hid)
    (wdown_f,) = _forward_to_sibling(down_bufs, layouts[3:])
    ff = _matmul(hid, wdown_f, "nn", [F32], "mlp_down")
    loss_blk, dy, dff, dg_mlp_post = _head_fwd_bwd(h, ff, tgt2, norm_mlp_post)

    def dup_epilogue(acc, r):
        return (acc * (2.0 * r.astype(F32)),)

    d_up = _matmul(dff, wdown_f, "nt", [BF16], "d_hid", epilogue=dup_epilogue, extras=(relu_up,))
    dw_down = _matmul(hid, dff, "tn", [BF16], "dw_down")
    plan_h_down, plan_h_up, plan_h_in = _halves_plan([lay_down]), _halves_plan([lay_up]), _halves_plan([lay_in])
    hd_sems, hd_arrays, hd_token = _halves_start([dw_down], [lay_down], "down_halves_start")
    d_hn = _matmul(d_up, wup_f, "nt", [F32], "d_hn", after=hd_token)
    dw_up = _matmul(hn, d_up, "tn", [BF16], "dw_up")
    dw_down, down_got = _split_copy_wait(hd_sems, hd_arrays, plan_h_down, "down_halves_wait", dw_up)
    hu_sems, hu_arrays, hu_token = _halves_start([dw_up], [lay_up], "up_halves_start", after=down_got)
    down_part = _chip_sum(dw_down, down_got, lay_down, core, "chip_sum_w_down")
    pd_sems, pd_arrays, pd_token = _partials_start([down_part], "down_partials_start", after=hu_token)
    dh, dmix, dg_mlp_pre, dg_mix_post = _mid_bwd(d_hn, h, norm_mlp_pre, dy, mix, norm_mix_post, pd_token)
    dmixed = _matmul(dmix, wout_f, "nt", [F32], "d_mixed")
    dw_out = _matmul(mixed, dmix, "tn", [BF16], "dw_out")
    dw_up, up_got = _split_copy_wait(hu_sems, hu_arrays, plan_h_up, "up_halves_wait", dw_out)
    up_part = _chip_sum(dw_up, up_got, lay_up, core, "chip_sum_w_up")
    out_got, = _halves_to_sibling([dw_out], [lay_out], "out_grad_halves_to_sibling")
    out_part = _chip_sum(dw_out, out_got, lay_out, core, "chip_sum_w_out")
    puo_sems, puo_arrays, puo_token = _partials_start([up_part, out_part], "up_out_partials_start", after=dmixed)
    dmixed3 = dmixed.reshape(bsz, seq, d)
    d_b, d_c, d_hsc, dw_conv_s = _sc_bwd(proj3, dmixed3, conv_s, 4 * gw, sw, gw)
    d_o, d_z, dg_gdn_norm = _gdn_out_bwd(dmixed, o_raw, proj, gdn_norm_w, heads, 3 * gw, puo_token)
    dq, dk, dv, dgc_b, dbeta_b = _gdn_bwd(qkv_act, gcb, betab, states, d_o, bsz, heads)
    d_ab, d_alog, d_dt = _gates_bwd(proj_ab, a_log_pad, dt_pad, dgc_b, dbeta_b, heads)
    d_qkv, dw_conv_q = _qkv_conv_bwd(proj3, [a.reshape(bsz, seq, gw) for a in (dq, dk, dv)], conv_q, 3 * gw)
    d_proj = jnp.concatenate([d_qkv.reshape(t, 3 * gw), d_z, d_b.reshape(t, sw), d_c.reshape(t, sw),
                              d_hsc.reshape(t, sw)], axis=1)
    dw_main = _matmul(xn, d_proj, "tn", [BF16], "dw_in_main")
    dw_ab = _matmul(xn, d_ab, "tn", [BF16], "dw_in_ab")
    dw_in = _unpack_dw_in(dw_main, dw_ab, gw, heads, sw, ics)
    hi_sems, hi_arrays, hi_token = _halves_start([dw_in], [lay_in], "in_halves_start")
    d_xn = _matmul(d_proj, w_main, "nt", [F32], "d_xn_main", after=hi_token)
    dw_in, in_got = _split_copy_wait(hi_sems, hi_arrays, plan_h_in, "in_halves_wait", d_xn)
    in_part = _chip_sum(dw_in, in_got, lay_in, core, "chip_sum_w_in")
    pi_sems, pi_arrays, pi_token = _partials_start([in_part], "in_partials_start")
    d_xn_ab = _matmul(d_ab, w_ab, "nt", [F32], "d_xn_ab", after=pi_token)
    grad_x, dg_mix_pre = _first_bwd(d_xn, d_xn_ab, x2, norm_mix_pre, dh)

    other_core = 1 - core

    def finish(parts, recvs, weights3, names, tag):
        nw = len(parts)
        halves = [_shard_sum(p, r, chip, f"shard_sum_{n}") for p, r, n in zip(parts, recvs, names)]
        sems, arrays, token = _join_start(halves, f"{tag}_join_start")
        own = [_adamw_rows(wt[0], h, m[0], v[0], core, None, token, f"adamw_own_{n}")
               for (wt, m, v), h, n in zip(weights3, arrays[:nw], names)]
        theirs = _split_copy_wait(sems, arrays, _join_plan(nw), f"{tag}_join_wait", own[-1][1])[nw:]
        return [_adamw_rows(wt[0], h, m[0], v[0], other_core, o, h, f"adamw_sibling_{n}")
                for (wt, m, v), h, o, n in zip(weights3, theirs, own, names)]

    down_part, down_recv = _split_copy_wait(pd_sems, pd_arrays, _partials_plan(1), "down_partials_wait", grad_x)
    up_part, out_part, up_recv, out_recv = _split_copy_wait(puo_sems, puo_arrays, _partials_plan(2),
                                                            "up_out_partials_wait", down_recv)
    res_up, res_down = finish([up_part, down_part], [up_recv, down_recv],
                              [(w_up, m_w_up, v_w_up), (w_down, m_w_down, v_w_down)], ("w_up", "w_down"), "mlp")
    in_part, in_recv = _split_copy_wait(pi_sems, pi_arrays, _partials_plan(1), "in_partials_wait", res_down[1])
    res_in, res_out = finish([in_part, out_part], [in_recv, out_recv],
                             [(w_in, m_w_in, v_w_in), (w_out, m_w_out, v_w_out)], ("w_in", "w_out"), "mix")

    small, offs = _pack([loss_blk[0:1, 0:1], dg_mix_pre, dw_conv_q, d_alog[:, :heads], d_dt[:, :heads], dg_gdn_norm,
                         dw_conv_s, dg_mix_post, dg_mlp_pre, dg_mlp_post])
    (loss, g_mix_pre, g_conv_q_full, g_alog, g_dt, g_gdn_norm, g_conv_s_full, g_mix_post, g_mlp_pre,
     g_mlp_post) = _unpack(_small_all_reduce(small), offs)
    j = 2 * lax.axis_index("x") + lax.axis_index("y")
    cq_w, cs_w = conv_qkv_w.shape[-1], conv_sc_w.shape[-1]
    g_conv_q = lax.dynamic_slice_in_dim(g_conv_q_full, j * cq_w, cq_w, axis=1)
    g_conv_s = lax.dynamic_slice_in_dim(g_conv_s_full, j * cs_w, cs_w, axis=1)

    big = {1: res_in, 7: res_out, 10: res_up, 11: res_down}
    grads = [g_mix_pre, None, g_conv_q, g_alog, g_dt, g_gdn_norm, g_conv_s, None, g_mix_post, g_mlp_pre, None,
             None, g_mlp_post]
    weights = [norm_mix_pre, w_in, conv_qkv_w, a_log, dt_bias, gdn_norm_w, conv_sc_w, w_out, norm_mix_post,
               norm_mlp_pre, w_up, w_down, norm_mlp_post]
    ms = [m_norm_mix_pre, m_w_in, m_conv_qkv_w, m_a_log, m_dt_bias, m_gdn_norm_w, m_conv_sc_w, m_w_out,
          m_norm_mix_post, m_norm_mlp_pre, m_w_up, m_w_down, m_norm_mlp_post]
    vs = [v_norm_mix_pre, v_w_in, v_conv_qkv_w, v_a_log, v_dt_bias, v_gdn_norm_w, v_conv_sc_w, v_w_out,
          v_norm_mix_post, v_norm_mlp_pre, v_w_up, v_w_down, v_norm_mlp_post]
    out_g, out_d, out_m, out_v = [], [], [], []
    for i, (wt, g, m, v) in enumerate(zip(weights, grads, ms, vs)):
        shape2 = wt.shape[-2:] if wt.ndim == 3 else wt.shape
        if i in big:
            g2, dl, nm, nv = big[i]
        else:
            g2 = g.reshape(shape2)
            dl, nm, nv = _adamw(wt.reshape(shape2), g2, m.reshape(shape2), v.reshape(shape2), f"adamw_{i}")
        out_g.append(g2.reshape(wt.shape))
        out_d.append(dl.reshape(wt.shape))
        out_m.append(nm.reshape(wt.shape))
        out_v.append(nv.reshape(wt.shape))

    return (loss.reshape(()), grad_x.reshape(bsz, seq, d), *out_g, *out_d, *out_m, *out_v)
```

```python
import functools

import jax
import jax.numpy as jnp
from jax import lax
from jax.experimental import pallas as pl
from jax.experimental.pallas import tpu as pltpu

CHUNK = 64
NORM_EPS = 1e-6
L2_EPS = 1e-6
N_CHIPS = 4
ADAM_LR = 0.001
ADAM_B1 = 0.9
ADAM_B2 = 0.999
ADAM_EPS = 1e-08
ADAM_WD = 0.01
ADAM_STEP = 10
LANES = 128
VMEM_LIMIT = 56 * 1024 * 1024

F32 = jnp.float32
BF16 = jnp.bfloat16
HI = lax.Precision.HIGH
EXACT_SUM = lax.Precision.HIGHEST
MESH = pl.DeviceIdType.MESH
ANY = pl.BlockSpec(memory_space=pl.ANY)


def _params(n_grid=0):
    return pltpu.CompilerParams(vmem_limit_bytes=VMEM_LIMIT)


def _tile(n, pref, align):
    if n <= pref:
        return n
    t = (pref // align) * align
    while t >= align:
        if n % t == 0:
            return t
        t -= align
    raise ValueError(f"no tile for {n}")


def _sigmoid(x):
    return 1.0 / (1.0 + jnp.exp(-x))


def _softplus(x):
    return jnp.maximum(x, 0.0) + jnp.log(1.0 + jnp.exp(-jnp.abs(x)))


def _rms_fwd(x, g):
    r = lax.rsqrt(jnp.mean(x * x, axis=-1, keepdims=True) + NORM_EPS)
    return x * r * g


def _rms_bwd(dy, x, g):
    r = lax.rsqrt(jnp.mean(x * x, axis=-1, keepdims=True) + NORM_EPS)
    xh = x * r
    dxh = dy * g
    dx = r * (dxh - xh * jnp.mean(dxh * xh, axis=-1, keepdims=True))
    dg = jnp.sum(dy * xh, axis=0, keepdims=True)
    return dx, dg


def _matmul(a, b, form, out_dtypes, name, epilogue=None, extras=(), after=None, tm=1024, tn=1024, tk=4096):
    if form == "nn":
        (m, kd), (_, n) = a.shape, b.shape
        dims = (((1,), (0,)), ((), ()))
    elif form == "nt":
        (m, kd), (n, _) = a.shape, b.shape
        dims = (((1,), (1,)), ((), ()))
    else:
        (kd, m), (_, n) = a.shape, b.shape
        dims = (((0,), (0,)), ((), ()))
    tm, tn, tk = _tile(m, tm, LANES), _tile(n, tn, LANES), _tile(kd, tk, LANES)
    nk = kd // tk
    n_extra = len(extras)
    n_out = len(out_dtypes)

    if form == "nn":
        a_spec = pl.BlockSpec((tm, tk), lambda i, j, k: (i, k))
        b_spec = pl.BlockSpec((tk, tn), lambda i, j, k: (k, j))
    elif form == "nt":
        a_spec = pl.BlockSpec((tm, tk), lambda i, j, k: (i, k))
        b_spec = pl.BlockSpec((tn, tk), lambda i, j, k: (j, k))
    else:
        a_spec = pl.BlockSpec((tk, tm), lambda i, j, k: (k, i))
        b_spec = pl.BlockSpec((tk, tn), lambda i, j, k: (k, j))
    tile_spec = pl.BlockSpec((tm, tn), lambda i, j, k: (i, j))

    order_only = [] if after is None else [after]
    n_skip = n_extra + len(order_only)

    def finish(acc, extra_refs, out_refs):
        outs = (acc,) if epilogue is None else epilogue(acc, *[e[...] for e in extra_refs])
        for o_ref, val in zip(out_refs, outs):
            o_ref[...] = val.astype(o_ref.dtype)

    def body(a_ref, b_ref, *rest):
        extra_refs = rest[:n_extra]
        out_refs = rest[n_skip:n_skip + n_out]
        if nk == 1:
            finish(lax.dot_general(a_ref[...], b_ref[...], dims, preferred_element_type=F32), extra_refs, out_refs)
            return
        acc_ref = rest[-1]
        k = pl.program_id(2)

        @pl.when(k == 0)
        def _():
            acc_ref[...] = jnp.zeros_like(acc_ref)

        acc_ref[...] += lax.dot_general(a_ref[...], b_ref[...], dims, preferred_element_type=F32)

        @pl.when(k == nk - 1)
        def _():
            finish(acc_ref[...], extra_refs, out_refs)

    outs = pl.pallas_call(
        body, name=name, grid=(m // tm, n // tn, nk),
        in_specs=[a_spec, b_spec] + [tile_spec] * n_extra + [ANY] * len(order_only),
        out_specs=[tile_spec] * n_out,
        out_shape=[jax.ShapeDtypeStruct((m, n), dt) for dt in out_dtypes],
        scratch_shapes=[pltpu.VMEM((tm, tn), F32)] if nk > 1 else [],
        compiler_params=_params(),
    )(a, b, *extras, *order_only)
    return outs[0] if n_out == 1 else outs


def _cast_into_layout(w, layout, chip, name, after=None):
    r, c = w.shape
    tr = _tile(r, 256, 16)
    order_only = [] if after is None else [after]

    def body(chip_ref, w_ref, *rest):
        rest[-1][...] = w_ref[...].astype(BF16)

    return pl.pallas_call(
        body, name=name,
        grid_spec=pltpu.PrefetchScalarGridSpec(
            num_scalar_prefetch=1, grid=(r // tr,),
            in_specs=[pl.BlockSpec((tr, c), lambda i, chip_ref: (i, 0))] + [ANY] * len(order_only),
            out_specs=layout.block_spec(tr, lambda i, chip_ref: (chip_ref[0], i))),
        out_shape=jax.ShapeDtypeStruct(layout.full_shape(), BF16), compiler_params=_params(),
    )(chip, w, *order_only)


def _in_segments(gw, heads, sw):
    main = 4 * gw
    return [(0, main, 0), (main + 2 * heads, 3 * sw, main), (main, 2 * heads, main + 3 * sw)]


def _pieces(seg_start, width, dst_start, ics):
    out = []
    g = seg_start
    while g < seg_start + width:
        j, cj = divmod(g, ics)
        wdt = min(ics - cj, seg_start + width - g)
        out.append((j, cj, dst_start + (g - seg_start), wdt))
        g += wdt
    return out


def _repack_w_in(w_sh, gw, heads, sw, after):
    ns, d, ics = w_sh.shape
    main = 4 * gw + 3 * sw
    tr = _tile(d, 128, 16)
    pieces = [p for seg in _in_segments(gw, heads, sw) for p in _pieces(*seg, ics)]

    def body(w_ref, after_ref, m_ref, ab_ref):
        ab_ref[...] = jnp.zeros_like(ab_ref)
        for j, cj, cd, wdt in pieces:
            if cd >= main:
                ab_ref[:, cd - main:cd - main + wdt] = w_ref[j, :, cj:cj + wdt]
            else:
                m_ref[:, cd:cd + wdt] = w_ref[j, :, cj:cj + wdt]

    return pl.pallas_call(
        body, name="repack_w_in", grid=(d // tr,),
        in_specs=[pl.BlockSpec((ns, tr, ics), lambda i: (0, i, 0)), ANY],
        out_specs=[pl.BlockSpec((tr, main), lambda i: (i, 0)), pl.BlockSpec((tr, LANES), lambda i: (i, 0))],
        out_shape=[jax.ShapeDtypeStruct((d, main), BF16), jax.ShapeDtypeStruct((d, LANES), BF16)],
        compiler_params=_params(),
    )(w_sh, after)


def _unpack_dw_in(dw_main, dw_ab, gw, heads, sw, ics):
    d = dw_main.shape[0]
    tr = _tile(d, 128, 16)
    main = 4 * gw + 3 * sw
    pieces = [p for seg in _in_segments(gw, heads, sw) for p in _pieces(*seg, ics)]

    def body(m_ref, ab_ref, o_ref):
        for j, cj, cd, wdt in pieces:
            if cd >= main:
                o_ref[j, :, cj:cj + wdt] = ab_ref[:, cd - main:cd - main + wdt].astype(BF16)
            else:
                o_ref[j, :, cj:cj + wdt] = m_ref[:, cd:cd + wdt].astype(BF16)

    return pl.pallas_call(
        body, name="unpack_dw_in", grid=(d // tr,),
        in_specs=[pl.BlockSpec((tr, main), lambda i: (i, 0)), pl.BlockSpec((tr, LANES), lambda i: (i, 0))],
        out_specs=pl.BlockSpec((N_CHIPS, tr, ics), lambda i: (0, i, 0)),
        out_shape=jax.ShapeDtypeStruct((N_CHIPS, d, ics), BF16), compiler_params=_params(),
    )(dw_main, dw_ab)


def _adamw(w, g, m, v, name):
    r, c = w.shape
    tr = _tile(r, 128, 8)
    c1 = 1.0 - ADAM_B1 ** ADAM_STEP
    c2 = 1.0 - ADAM_B2 ** ADAM_STEP

    def body(w_ref, g_ref, m_ref, v_ref, d_ref, nm_ref, nv_ref):
        gg = g_ref[...]
        nm = ADAM_B1 * m_ref[...] + (1.0 - ADAM_B1) * gg
        nv = ADAM_B2 * v_ref[...] + (1.0 - ADAM_B2) * jnp.square(gg)
        m_hat = nm / c1
        v_hat = nv / c2
        d_ref[...] = -ADAM_LR * (m_hat / (jnp.sqrt(v_hat) + ADAM_EPS) + ADAM_WD * w_ref[...])
        nm_ref[...] = nm
        nv_ref[...] = nv

    spec = pl.BlockSpec((tr, c), lambda i: (i, 0))
    return pl.pallas_call(
        body, name=name, grid=(r // tr,), in_specs=[spec] * 4, out_specs=[spec] * 3,
        out_shape=[jax.ShapeDtypeStruct((r, c), F32)] * 3, compiler_params=_params(),
    )(w, g, m, v)


def _row_spec(tt, d):
    return pl.BlockSpec((tt, d), lambda i: (i, 0))


def _vec_spec(d):
    return pl.BlockSpec((1, d), lambda i: (0, 0))


def _norm_fwd(x, g, after):
    t, d = x.shape
    tt = _tile(t, 256, 16)

    def body(x_ref, g_ref, after_ref, o_ref):
        o_ref[...] = _rms_fwd(x_ref[...], g_ref[...]).astype(BF16)

    return pl.pallas_call(
        body, name="norm_mix_pre", grid=(t // tt,), in_specs=[_row_spec(tt, d), _vec_spec(d), ANY],
        out_specs=_row_spec(tt, d), out_shape=jax.ShapeDtypeStruct((t, d), BF16), compiler_params=_params(),
    )(x, g, after)


def _mid_fwd(x, mix, g_post, g_pre):
    t, d = x.shape
    tt = _tile(t, 128, 16)

    def body(x_ref, mix_ref, gp_ref, gn_ref, h_ref, hn_ref):
        h = x_ref[...] + _rms_fwd(mix_ref[...], gp_ref[...])
        h_ref[...] = h
        hn_ref[...] = _rms_fwd(h, gn_ref[...]).astype(BF16)

    return pl.pallas_call(
        body, name="mid_fwd", grid=(t // tt,),
        in_specs=[_row_spec(tt, d), _row_spec(tt, d), _vec_spec(d), _vec_spec(d)],
        out_specs=[_row_spec(tt, d), _row_spec(tt, d)],
        out_shape=[jax.ShapeDtypeStruct((t, d), F32), jax.ShapeDtypeStruct((t, d), BF16)],
        compiler_params=_params(),
    )(x, mix, g_post, g_pre)


def _head_fwd_bwd(h, ff, tgt, g_post):
    t, d = h.shape
    tt = _tile(t, 128, 16)

    def body(h_ref, ff_ref, t_ref, g_ref, loss_ref, dy_ref, dff_ref, dg_ref):
        i = pl.program_id(0)

        @pl.when(i == 0)
        def _():
            loss_ref[...] = jnp.zeros_like(loss_ref)
            dg_ref[...] = jnp.zeros_like(dg_ref)

        ff = ff_ref[...]
        g = g_ref[...]
        e = h_ref[...] + _rms_fwd(ff, g) - t_ref[...]
        loss_ref[...] += 0.5 * jnp.sum(jnp.mean(e * e, axis=-1, keepdims=True))
        dy = e * (1.0 / d)
        dy_ref[...] = dy
        dff, dg = _rms_bwd(dy, ff, g)
        dff_ref[...] = dff.astype(BF16)
        dg_ref[...] += dg

    return pl.pallas_call(
        body, name="loss_head", grid=(t // tt,),
        in_specs=[_row_spec(tt, d)] * 3 + [_vec_spec(d)],
        out_specs=[pl.BlockSpec((8, LANES), lambda i: (0, 0)), _row_spec(tt, d), _row_spec(tt, d), _vec_spec(d)],
        out_shape=[jax.ShapeDtypeStruct((8, LANES), F32), jax.ShapeDtypeStruct((t, d), F32),
                   jax.ShapeDtypeStruct((t, d), BF16), jax.ShapeDtypeStruct((1, d), F32)],
        compiler_params=_params(),
    )(h, ff, tgt, g_post)


def _mid_bwd(d_hn, h, g_pre, dy, mix, g_post, after):
    t, d = h.shape
    tt = _tile(t, 128, 16)

    def body(dhn_ref, h_ref, gn_ref, dy_ref, mix_ref, gp_ref, after_ref, dh_ref, dmix_ref, dgn_ref, dgp_ref):
        i = pl.program_id(0)

        @pl.when(i == 0)
        def _():
            dgn_ref[...] = jnp.zeros_like(dgn_ref)
            dgp_ref[...] = jnp.zeros_like(dgp_ref)

        dx, dgn = _rms_bwd(dhn_ref[...], h_ref[...], gn_ref[...])
        dh = dy_ref[...] + dx
        dh_ref[...] = dh
        dmix, dgp = _rms_bwd(dh, mix_ref[...], gp_ref[...])
        dmix_ref[...] = dmix.astype(BF16)
        dgn_ref[...] += dgn
        dgp_ref[...] += dgp

    return pl.pallas_call(
        body, name="mid_bwd", grid=(t // tt,),
        in_specs=[_row_spec(tt, d), _row_spec(tt, d), _vec_spec(d), _row_spec(tt, d), _row_spec(tt, d), _vec_spec(d),
                  ANY],
        out_specs=[_row_spec(tt, d), _row_spec(tt, d), _vec_spec(d), _vec_spec(d)],
        out_shape=[jax.ShapeDtypeStruct((t, d), F32), jax.ShapeDtypeStruct((t, d), BF16),
                   jax.ShapeDtypeStruct((1, d), F32), jax.ShapeDtypeStruct((1, d), F32)],
        compiler_params=_params(),
    )(d_hn, h, g_pre, dy, mix, g_post, after)


def _first_bwd(d_xn, d_xn_ab, x, g, dh):
    t, d = x.shape
    tt = _tile(t, 128, 16)

    def body(a_ref, b_ref, x_ref, g_ref, dh_ref, dx_ref, dg_ref):
        i = pl.program_id(0)

        @pl.when(i == 0)
        def _():
            dg_ref[...] = jnp.zeros_like(dg_ref)

        dx, dg = _rms_bwd(a_ref[...] + b_ref[...], x_ref[...], g_ref[...])
        dx_ref[...] = dh_ref[...] + dx
        dg_ref[...] += dg

    return pl.pallas_call(
        body, name="first_bwd", grid=(t // tt,),
        in_specs=[_row_spec(tt, d), _row_spec(tt, d), _row_spec(tt, d), _vec_spec(d), _row_spec(tt, d)],
        out_specs=[_row_spec(tt, d), _vec_spec(d)],
        out_shape=[jax.ShapeDtypeStruct((t, d), F32), jax.ShapeDtypeStruct((1, d), F32)],
        compiler_params=_params(),
    )(d_xn, d_xn_ab, x, g, dh)


HALO = 8


def _cur(ts, tc, off):
    return pl.BlockSpec((1, ts, tc), lambda ci, b, s: (b, s, off + ci))


def _prev(ts, tc, off):
    return pl.BlockSpec((1, HALO, tc), lambda ci, b, s: (b, jnp.maximum(s * (ts // HALO) - 1, 0), off + ci))


def _next(ts, tc, off, seq):
    last = seq // HALO - 1
    return pl.BlockSpec((1, HALO, tc), lambda ci, b, s: (b, jnp.minimum((s + 1) * (ts // HALO), last), off + ci))


def _conv_w_spec(kw, tc):
    return pl.BlockSpec((kw, tc), lambda ci, b, s: (0, ci))


def _rows_back(u, prev8, k):
    if k == 0:
        return u
    rolled = pltpu.roll(u, k, axis=0)
    row = lax.broadcasted_iota(jnp.int32, (HALO, u.shape[1]), 0)
    first = jnp.where(row < k, pltpu.roll(prev8, k, axis=0), rolled[0:HALO])
    return first if u.shape[0] == HALO else jnp.concatenate([first, rolled[HALO:]], axis=0)


def _rows_ahead(g, next8, m):
    if m == 0:
        return g
    n = g.shape[0]
    rolled = pltpu.roll(g, n - m, axis=0)
    row = lax.broadcasted_iota(jnp.int32, (HALO, g.shape[1]), 0)
    last = jnp.where(row >= HALO - m, pltpu.roll(next8, HALO - m, axis=0), rolled[n - HALO:n])
    return last if n == HALO else jnp.concatenate([rolled[0:n - HALO], last], axis=0)


def _windows(u, prev8, kw):
    return [_rows_back(u, prev8, kw - 1 - j) for j in range(kw)]


def _tap_sum(w, wins):
    acc = w[0:1, :] * wins[0]
    for j in range(1, len(wins)):
        acc = acc + w[j:j + 1, :] * wins[j]
    return acc


def _silu_grad(x):
    s = _sigmoid(x)
    return s * (1.0 + x * (1.0 - s))


def _qkv_conv_fwd(proj3, w, width):
    bsz, seq, _ = proj3.shape
    kw = w.shape[0]
    ts, tc = _tile(seq, 256, 8), _tile(width, 512, LANES)

    def body(u_ref, up_ref, w_ref, o_ref):
        prev8 = jnp.where(pl.program_id(2) == 0, 0.0, up_ref[0])
        pre = _tap_sum(w_ref[...], _windows(u_ref[0], prev8, kw))
        o_ref[0] = pre * _sigmoid(pre)

    return pl.pallas_call(
        body, name="qkv_conv_fwd", grid=(width // tc, bsz, seq // ts),
        in_specs=[_cur(ts, tc, 0), _prev(ts, tc, 0), _conv_w_spec(kw, tc)],
        out_specs=_cur(ts, tc, 0), out_shape=jax.ShapeDtypeStruct((bsz, seq, width), F32),
        compiler_params=_params(),
    )(proj3, proj3, w)


def _qkv_conv_bwd(proj3, dparts, w, width):
    bsz, seq, _ = proj3.shape
    kw = w.shape[0]
    n_parts = len(dparts)
    part_w = width // n_parts
    ts, tc = _tile(seq, 256, 8), _tile(part_w, 512, LANES)
    n_s = seq // ts
    npt = part_w // tc
    last = seq // HALO - 1

    def part_cur(p):
        def index(ci, b, s):
            use = (ci // npt) == p
            return jnp.where(use, b, 0), jnp.where(use, s, 0), jnp.where(use, ci % npt, 0)
        return pl.BlockSpec((1, ts, tc), index)

    def part_next(p):
        def index(ci, b, s):
            use = (ci // npt) == p
            return (jnp.where(use, b, 0), jnp.where(use, jnp.minimum((s + 1) * (ts // HALO), last), 0),
                    jnp.where(use, ci % npt, 0))
        return pl.BlockSpec((1, HALO, tc), index)

    def body(u_ref, up_ref, un_ref, *rest):
        d_refs, dn_refs = rest[:n_parts], rest[n_parts:2 * n_parts]
        w_ref, du_ref, dw_ref, dbuf, dnbuf = rest[2 * n_parts:]
        ci, b, s = pl.program_id(0), pl.program_id(1), pl.program_id(2)

        @pl.when((b == 0) & (s == 0))
        def _():
            dw_ref[...] = jnp.zeros_like(dw_ref)

        for p in range(n_parts):
            @pl.when(ci // npt == p)
            def _(p=p):
                dbuf[...] = d_refs[p][0]
                dnbuf[...] = dn_refs[p][0]

        w = w_ref[...]
        u = u_ref[0]
        wins = _windows(u, jnp.where(s == 0, 0.0, up_ref[0]), kw)
        wins_next = _windows(un_ref[0], u[ts - HALO:ts], kw)
        g = dbuf[...] * _silu_grad(_tap_sum(w, wins))
        g_next = jnp.where(s == n_s - 1, 0.0, dnbuf[...] * _silu_grad(_tap_sum(w, wins_next)))
        for j in range(kw):
            dw_ref[j:j + 1, :] += jnp.sum(g * wins[j], axis=0, keepdims=True)
        du_ref[0] = _tap_sum(w, [_rows_ahead(g, g_next, kw - 1 - j) for j in range(kw)]).astype(BF16)

    return pl.pallas_call(
        body, name="qkv_conv_bwd", grid=(width // tc, bsz, n_s),
        in_specs=[_cur(ts, tc, 0), _prev(ts, tc, 0), _next(ts, tc, 0, seq)]
        + [part_cur(p) for p in range(n_parts)] + [part_next(p) for p in range(n_parts)] + [_conv_w_spec(kw, tc)],
        out_specs=[_cur(ts, tc, 0), _conv_w_spec(kw, tc)],
        out_shape=[jax.ShapeDtypeStruct((bsz, seq, width), BF16), jax.ShapeDtypeStruct((kw, width), F32)],
        scratch_shapes=[pltpu.VMEM((ts, tc), F32), pltpu.VMEM((HALO, tc), F32)],
        compiler_params=_params(),
    )(proj3, proj3, proj3, *dparts, *dparts, w)


def _sc_fwd(proj3, w, off, sw):
    bsz, seq, _ = proj3.shape
    kw = w.shape[0]
    ts, tc = _tile(seq, 256, 8), _tile(sw, 512, LANES)
    ob, oc, oh = off // tc, (off + sw) // tc, (off + 2 * sw) // tc

    def body(b_ref, c_ref, cp_ref, h_ref, hp_ref, w_ref, o_ref):
        prev8 = jnp.where(pl.program_id(2) == 0, 0.0, cp_ref[0] * hp_ref[0])
        o_ref[0] = (b_ref[0] * _tap_sum(w_ref[...], _windows(c_ref[0] * h_ref[0], prev8, kw))).astype(BF16)

    return pl.pallas_call(
        body, name="sc_fwd", grid=(sw // tc, bsz, seq // ts),
        in_specs=[_cur(ts, tc, ob), _cur(ts, tc, oc), _prev(ts, tc, oc), _cur(ts, tc, oh), _prev(ts, tc, oh),
                  _conv_w_spec(kw, tc)],
        out_specs=_cur(ts, tc, 0), out_shape=jax.ShapeDtypeStruct((bsz, seq, sw), BF16),
        compiler_params=_params(),
    )(proj3, proj3, proj3, proj3, proj3, w)


def _sc_bwd(proj3, dmixed3, w, off, sw, d_off):
    bsz, seq, _ = proj3.shape
    kw = w.shape[0]
    ts, tc = _tile(seq, 256, 8), _tile(sw, 512, LANES)
    n_s = seq // ts
    ob, oc, oh, od = off // tc, (off + sw) // tc, (off + 2 * sw) // tc, d_off // tc

    def body(d_ref, dn_ref, b_ref, bn_ref, c_ref, cp_ref, h_ref, hp_ref, w_ref,
             db_ref, dc_ref, dh_ref, dw_ref):
        b, s = pl.program_id(1), pl.program_id(2)

        @pl.when((b == 0) & (s == 0))
        def _():
            dw_ref[...] = jnp.zeros_like(dw_ref)

        w = w_ref[...]
        cc, hh = c_ref[0], h_ref[0]
        wins = _windows(cc * hh, jnp.where(s == 0, 0.0, cp_ref[0] * hp_ref[0]), kw)
        dout = d_ref[0]
        db_ref[0] = (dout * _tap_sum(w, wins)).astype(BF16)
        g = dout * b_ref[0]
        g_next = jnp.where(s == n_s - 1, 0.0, dn_ref[0] * bn_ref[0])
        for j in range(kw):
            dw_ref[j:j + 1, :] += jnp.sum(g * wins[j], axis=0, keepdims=True)
        dp = _tap_sum(w, [_rows_ahead(g, g_next, kw - 1 - j) for j in range(kw)])
        dc_ref[0] = (dp * hh).astype(BF16)
        dh_ref[0] = (dp * cc).astype(BF16)

    out = jax.ShapeDtypeStruct((bsz, seq, sw), BF16)
    return pl.pallas_call(
        body, name="sc_bwd", grid=(sw // tc, bsz, n_s),
        in_specs=[_cur(ts, tc, od), _next(ts, tc, od, seq), _cur(ts, tc, ob), _next(ts, tc, ob, seq),
                  _cur(ts, tc, oc), _prev(ts, tc, oc), _cur(ts, tc, oh), _prev(ts, tc, oh), _conv_w_spec(kw, tc)],
        out_specs=[_cur(ts, tc, 0)] * 3 + [_conv_w_spec(kw, tc)],
        out_shape=[out, out, out, jax.ShapeDtypeStruct((kw, sw), F32)],
        compiler_params=_params(),
    )(dmixed3, dmixed3, proj3, proj3, proj3, proj3, proj3, proj3, w)


def _tri_ones(lower):
    i = lax.broadcasted_iota(jnp.int32, (CHUNK, CHUNK), 0)
    j = lax.broadcasted_iota(jnp.int32, (CHUNK, CHUNK), 1)
    return jnp.where((i >= j) if lower else (j >= i), 1.0, 0.0).astype(F32)


def _gates_fwd(proj_ab, a_log_pad, dt_pad, heads):
    t = proj_ab.shape[0]
    gw = heads * LANES

    def body(ab_ref, al_ref, dt_ref, gc_ref, beta_ref):
        ab = ab_ref[...]
        g = -jnp.exp(al_ref[...]) * _softplus(ab + dt_ref[...])
        gc = jnp.dot(_tri_ones(True), g, precision=EXACT_SUM, preferred_element_type=F32)
        beta = _sigmoid(ab)
        for h in range(heads):
            gc_ref[:, h * LANES:(h + 1) * LANES] = jnp.broadcast_to(gc[:, h:h + 1], (CHUNK, LANES))
            beta_ref[:, h * LANES:(h + 1) * LANES] = jnp.broadcast_to(beta[:, heads + h:heads + h + 1], (CHUNK, LANES))

    return pl.pallas_call(
        body, name="gates_fwd", grid=(t // CHUNK,),
        in_specs=[_row_spec(CHUNK, LANES), _vec_spec(LANES), _vec_spec(LANES)],
        out_specs=[_row_spec(CHUNK, gw), _row_spec(CHUNK, gw)],
        out_shape=[jax.ShapeDtypeStruct((t, gw), F32)] * 2, compiler_params=_params(),
    )(proj_ab, a_log_pad, dt_pad)


def _gates_bwd(proj_ab, a_log_pad, dt_pad, dgc_b, dbeta_b, heads):
    t = proj_ab.shape[0]
    gw = heads * LANES

    def body(ab_ref, al_ref, dt_ref, dgc_ref, dbeta_ref, dab_ref, dal_ref, ddt_ref):
        i = pl.program_id(0)

        @pl.when(i == 0)
        def _():
            dal_ref[...] = jnp.zeros_like(dal_ref)
            ddt_ref[...] = jnp.zeros_like(ddt_ref)

        lane = lax.broadcasted_iota(jnp.int32, (CHUNK, LANES), 1)
        dgc = jnp.zeros((CHUNK, LANES), F32)
        dbeta = jnp.zeros((CHUNK, LANES), F32)
        for h in range(heads):
            dgc = jnp.where(lane == h, dgc_ref[:, h * LANES:(h + 1) * LANES], dgc)
            dbeta = jnp.where(lane == heads + h, dbeta_ref[:, h * LANES:(h + 1) * LANES], dbeta)
        dg = jnp.dot(_tri_ones(False), dgc, precision=EXACT_SUM, preferred_element_type=F32)
        ab = ab_ref[...]
        z = ab + dt_ref[...]
        ea = jnp.exp(al_ref[...])
        da = dg * (-ea) * _sigmoid(z)
        beta = _sigmoid(ab)
        db = dbeta * beta * (1.0 - beta)
        dab_ref[...] = jnp.where(lane < heads, da, jnp.where(lane < 2 * heads, db, 0.0)).astype(BF16)
        da_m = jnp.where(lane < heads, da, 0.0)
        ddt_ref[...] += jnp.sum(da_m, axis=0, keepdims=True)
        dal_ref[...] += jnp.sum(jnp.where(lane < heads, dg * (-ea) * _softplus(z), 0.0), axis=0, keepdims=True)

    return pl.pallas_call(
        body, name="gates_bwd", grid=(t // CHUNK,),
        in_specs=[_row_spec(CHUNK, LANES), _vec_spec(LANES), _vec_spec(LANES), _row_spec(CHUNK, gw), _row_spec(CHUNK, gw)],
        out_specs=[_row_spec(CHUNK, LANES), _vec_spec(LANES), _vec_spec(LANES)],
        out_shape=[jax.ShapeDtypeStruct((t, LANES), BF16), jax.ShapeDtypeStruct((1, LANES), F32),
                   jax.ShapeDtypeStruct((1, LANES), F32)],
        compiler_params=_params(),
    )(proj_ab, a_log_pad, dt_pad, dgc_b, dbeta_b)


def _dot(a, b, dims, hi=False):
    if hi:
        return lax.dot_general(a, b, (dims, ((), ())), precision=HI, preferred_element_type=F32)
    return lax.dot_general(a.astype(BF16), b.astype(BF16), (dims, ((), ())), preferred_element_type=F32)


NN = ((1,), (0,))
NT = ((1,), (1,))
TN = ((0,), (0,))


def _each(f, *lists):
    return [f(*xs) for xs in zip(*lists)]


def _dots(a, b, dims, hi=False):
    return _each(lambda x, y: _dot(x, y, dims, hi=hi), a, b)


def _unit_lower_inverse(ms):
    i = lax.broadcasted_iota(jnp.int32, (CHUNK, CHUNK), 0)
    j = lax.broadcasted_iota(jnp.int32, (CHUNK, CHUNK), 1)
    eye = jnp.where(i == j, 1.0, 0.0).astype(F32)
    ts = [eye - jnp.where(jnp.right_shift(i, 1) == jnp.right_shift(j, 1), m, 0.0) for m in ms]
    shift = 1
    while (1 << shift) < CHUNK:
        same_pair = jnp.right_shift(i, shift + 1) == jnp.right_shift(j, shift + 1)
        other_half = jnp.right_shift(i, shift) != jnp.right_shift(j, shift)
        offs = [jnp.where(same_pair & other_half, m, 0.0) for m in ms]
        corr = _dots(_dots(ts, offs, NN, hi=True), ts, NN, hi=True)
        ts = _each(lambda t, c: t - c, ts, corr)
        shift += 1
    return ts


def _chunk_local(qrs, krs, vs, gcbs, betabs, head_dim):
    i = lax.broadcasted_iota(jnp.int32, (CHUNK, CHUNK), 0)
    j = lax.broadcasted_iota(jnp.int32, (CHUNK, CHUNK), 1)
    scale = head_dim ** -0.5
    rqs = [lax.rsqrt(jnp.sum(q * q, axis=-1, keepdims=True) + L2_EPS) for q in qrs]
    rks = [lax.rsqrt(jnp.sum(k * k, axis=-1, keepdims=True) + L2_EPS) for k in krs]
    qhs = _each(lambda a, r: a * r, qrs, rqs)
    ks = _each(lambda a, r: a * r, krs, rks)
    qs = [a * scale for a in qhs]
    decays = [jnp.exp(jnp.where(i >= j, g[:, 0:CHUNK] - g.T[0:CHUNK, :], -jnp.inf)) for g in gcbs]
    kks = _dots(ks, ks, NT)
    qks = _dots(qs, ks, NT)
    ms = _each(lambda b, kk, d: jnp.where(i > j, b[:, 0:CHUNK] * kk * d, 0.0), betabs, kks, decays)
    tinvs = _unit_lower_inverse(ms)
    egs = [jnp.exp(g) for g in gcbs]
    rhs_ws = _each(lambda k, b, e: k * b * e, ks, betabs, egs)
    us = _dots(tinvs, _each(lambda v, b: v * b, vs, betabs), NN, hi=True)
    ws = _dots(tinvs, rhs_ws, NN, hi=True)
    out = []
    for h in range(len(qrs)):
        g_last = gcbs[h][CHUNK - 1:CHUNK, :]
        e_last = jnp.exp(g_last - gcbs[h])
        out.append(dict(rq=rqs[h], rk=rks[h], qh=qhs[h], q=qs[h], k=ks[h], decay=decays[h],
                        beta_col=betabs[h][:, 0:CHUNK], kk=kks[h], m=ms[h], tinv=tinvs[h], eg=egs[h], rhs_w=rhs_ws[h],
                        u=us[h], w=ws[h], p=qks[h] * decays[h], qd=qs[h] * egs[h], kd=ks[h] * e_last, e_last=e_last,
                        gl=jnp.exp(g_last), scale=scale, strict=i > j, incl=i >= j))
    return out


def _field(dicts, name):
    return [d[name] for d in dicts]


GDN_HEAD_GROUP = 16


def _gdn_specs(n_chunks, heads, reverse):
    hg = min(GDN_HEAD_GROUP, heads)
    assert heads % hg == 0

    def cidx(c):
        return (n_chunks - 1 - c) if reverse else c

    def tok(off):
        return pl.BlockSpec((CHUNK, hg * LANES), lambda b, h, c: (b * n_chunks + cidx(c), off // hg + h))

    state = pl.BlockSpec((None, hg, LANES, LANES), lambda b, h, c: (b * n_chunks + cidx(c), h, 0, 0))
    return hg, tok, state


def _gdn_fwd(qkv_act, gcb, betab, bsz, heads):
    t = qkv_act.shape[0]
    n_chunks = t // bsz // CHUNK
    hg, tok, state = _gdn_specs(n_chunks, heads, False)

    def body(q_ref, k_ref, v_ref, gc_ref, beta_ref, o_ref, s_ref, st):
        @pl.when(pl.program_id(2) == 0)
        def _():
            st[...] = jnp.zeros_like(st)

        sls = [slice(hh * LANES, (hh + 1) * LANES) for hh in range(hg)]
        loc = _chunk_local(*[[r[:, sl] for sl in sls] for r in (q_ref, k_ref, v_ref, gc_ref, beta_ref)], LANES)
        s0 = [st[hh] for hh in range(hg)]
        v_new = _each(lambda u, ws: u - ws, _field(loc, "u"), _dots(_field(loc, "w"), s0, NN))
        o_state = _dots(_field(loc, "qd"), s0, NN)
        o_local = _dots(_field(loc, "p"), v_new, NN)
        s_add = _dots(_field(loc, "kd"), v_new, TN)
        for hh in range(hg):
            o_ref[:, sls[hh]] = o_state[hh] + o_local[hh]
            s_ref[hh] = s0[hh]
            st[hh] = s0[hh] * loc[hh]["gl"] + s_add[hh]

    return pl.pallas_call(
        body, name="gdn_fwd", grid=(bsz, heads // hg, n_chunks),
        in_specs=[tok(0), tok(heads), tok(2 * heads), tok(0), tok(0)],
        out_specs=[tok(0), state],
        out_shape=[jax.ShapeDtypeStruct((t, heads * LANES), F32),
                   jax.ShapeDtypeStruct((bsz * n_chunks, heads, LANES, LANES), F32)],
        scratch_shapes=[pltpu.VMEM((hg, LANES, LANES), F32)], compiler_params=_params(),
    )(qkv_act, qkv_act, qkv_act, gcb, betab)


def _gdn_bwd(qkv_act, gcb, betab, states, d_o, bsz, heads):
    t = qkv_act.shape[0]
    n_chunks = t // bsz // CHUNK
    hg, tok, state = _gdn_specs(n_chunks, heads, True)

    def rowsum(a):
        return jnp.sum(a, axis=-1, keepdims=True)

    def finish_head(sl, L, v, betab, d_qd, d_kd, d_gl, d_p, d_m, d_rhs_u, d_rhs_w, d_q, d_k,
                    dq_ref, dk_ref, dv_ref, dgc_ref, dbeta_ref):
        k, decay = L["k"], L["decay"]
        dv_ref[:, sl] = betab * d_rhs_u
        e = d_m * L["m"] + d_p * L["p"]
        d_beta = rowsum(d_m * L["kk"] * decay) + rowsum(d_rhs_u * v) + rowsum(d_rhs_w * k * L["eg"])
        s_kd = rowsum(d_kd * L["kd"])
        d_gc = (rowsum(e) - rowsum(e.T) + rowsum(d_rhs_w * L["rhs_w"]) + rowsum(d_qd * L["qd"]) - s_kd)
        row = lax.broadcasted_iota(jnp.int32, (CHUNK, 1), 0)
        d_gc = d_gc + jnp.where(row == CHUNK - 1, jnp.sum(s_kd) + d_gl * jnp.sum(L["gl"][:, 0:1]), 0.0)
        dgc_ref[:, sl] = jnp.broadcast_to(d_gc, (CHUNK, LANES))
        dbeta_ref[:, sl] = jnp.broadcast_to(d_beta, (CHUNK, LANES))
        d_qh = d_q * L["scale"]
        dq_ref[:, sl] = L["rq"] * (d_qh - L["qh"] * rowsum(d_qh * L["qh"]))
        dk_ref[:, sl] = L["rk"] * (d_k - k * rowsum(d_k * k))

    def body(q_ref, k_ref, v_ref, gc_ref, beta_ref, s_ref, do_ref, dq_ref, dk_ref, dv_ref, dgc_ref, dbeta_ref, dst):
        @pl.when(pl.program_id(2) == 0)
        def _():
            dst[...] = jnp.zeros_like(dst)

        sls = [slice(hh * LANES, (hh + 1) * LANES) for hh in range(hg)]
        vs = [v_ref[:, sl] for sl in sls]
        betabs = [beta_ref[:, sl] for sl in sls]
        loc = _chunk_local([q_ref[:, sl] for sl in sls], [k_ref[:, sl] for sl in sls], vs,
                           [gc_ref[:, sl] for sl in sls], betabs, LANES)
        q, k, u, w, p, tinv, decay, qd, kd, eg = (_field(loc, n) for n in
                                                  ("q", "k", "u", "w", "p", "tinv", "decay", "qd", "kd", "eg"))
        s0 = [s_ref[hh] for hh in range(hg)]
        d_out = [do_ref[:, sl] for sl in sls]
        ds1 = [dst[hh] for hh in range(hg)]
        v_new = _each(lambda a, b: a - b, u, _dots(w, s0, NN))

        d_vnew = _each(lambda a, b: a + b, _dots(p, d_out, TN), _dots(kd, ds1, NN))
        d_qd = _dots(d_out, s0, NT)
        d_kd = _dots(v_new, ds1, NT)
        d_gl = _each(lambda a, b: jnp.sum(a * b), ds1, s0)
        d_p = _each(lambda L, a: jnp.where(L["incl"], a, 0.0), loc, _dots(d_out, v_new, NT))
        d_w = [-a for a in _dots(d_vnew, s0, NT)]
        ds_out, ds_vn = _dots(qd, d_out, TN), _dots(w, d_vnew, TN)
        for hh in range(hg):
            dst[hh] = ds_out[hh] + ds1[hh] * loc[hh]["gl"] - ds_vn[hh]

        d_rhs_u = _dots(tinv, d_vnew, TN, hi=True)
        d_rhs_w = _dots(tinv, d_w, TN, hi=True)
        d_a = _each(lambda a, b: -(a + b), _dots(d_rhs_u, u, NT, hi=True), _dots(d_rhs_w, w, NT, hi=True))
        d_m = _each(lambda L, a: jnp.where(L["strict"], a, 0.0), loc, d_a)
        g_kk = _each(lambda L, a: a * L["beta_col"] * L["decay"], loc, d_m)
        h_qk = _each(lambda a, d: a * d, d_p, decay)

        d_q = _each(lambda a, e, b: a + e * b, _dots(h_qk, k, NN), eg, d_qd)
        d_k = _each(lambda a, b, c, L, bb, rw, dk: a + b + c + bb * L["eg"] * rw + L["e_last"] * dk,
                    _dots(g_kk, k, NN), _dots(g_kk, k, TN), _dots(h_qk, q, TN), loc, betabs, d_rhs_w, d_kd)
        for hh in range(hg):
            finish_head(sls[hh], loc[hh], vs[hh], betabs[hh], d_qd[hh], d_kd[hh], d_gl[hh], d_p[hh], d_m[hh],
                        d_rhs_u[hh], d_rhs_w[hh], d_q[hh], d_k[hh], dq_ref, dk_ref, dv_ref, dgc_ref, dbeta_ref)

    tok_shape = jax.ShapeDtypeStruct((t, heads * LANES), F32)
    return pl.pallas_call(
        body, name="gdn_bwd", grid=(bsz, heads // hg, n_chunks),
        in_specs=[tok(0), tok(heads), tok(2 * heads), tok(0), tok(0), state, tok(0)],
        out_specs=[tok(0)] * 5,
        out_shape=[tok_shape] * 5,
        scratch_shapes=[pltpu.VMEM((hg, LANES, LANES), F32)], compiler_params=_params(),
    )(qkv_act, qkv_act, qkv_act, gcb, betab, states, d_o)


def _gdn_out_fwd(o, proj, gw_norm, heads, z_off, after):
    t = o.shape[0]
    ts = _tile(t, 512, 16)
    zb = z_off // LANES

    def body(o_ref, z_ref, w_ref, after_ref, out_ref):
        z = z_ref[...]
        out_ref[...] = (_rms_fwd(o_ref[...], w_ref[...]) * (z * _sigmoid(z))).astype(BF16)

    return pl.pallas_call(
        body, name="gdn_out_fwd", grid=(t // ts, heads),
        in_specs=[pl.BlockSpec((ts, LANES), lambda i, h: (i, h)), pl.BlockSpec((ts, LANES), lambda i, h: (i, zb + h)),
                  pl.BlockSpec((1, LANES), lambda i, h: (0, 0)), ANY],
        out_specs=pl.BlockSpec((ts, LANES), lambda i, h: (i, h)),
        out_shape=jax.ShapeDtypeStruct((t, heads * LANES), BF16), compiler_params=_params(),
    )(o, proj, gw_norm, after)


def _gdn_out_bwd(dmixed, o, proj, gw_norm, heads, z_off, after):
    t = o.shape[0]
    ts = _tile(t, 512, 16)
    zb = z_off // LANES

    def body(d_ref, o_ref, z_ref, w_ref, after_ref, do_ref, dz_ref, dw_ref):
        @pl.when((pl.program_id(0) == 0) & (pl.program_id(1) == 0))
        def _():
            dw_ref[...] = jnp.zeros_like(dw_ref)

        d, oo, z, w = d_ref[...], o_ref[...], z_ref[...], w_ref[...]
        on = _rms_fwd(oo, w)
        dz_ref[...] = (d * on * _silu_grad(z)).astype(BF16)
        d_o, d_w = _rms_bwd(d * (z * _sigmoid(z)), oo, w)
        do_ref[...] = d_o
        dw_ref[...] += d_w

    blk = pl.BlockSpec((ts, LANES), lambda i, h: (i, h))
    vec = pl.BlockSpec((1, LANES), lambda i, h: (0, 0))
    return pl.pallas_call(
        body, name="gdn_out_bwd", grid=(t // ts, heads),
        in_specs=[blk, blk, pl.BlockSpec((ts, LANES), lambda i, h: (i, zb + h)), vec, ANY],
        out_specs=[blk, blk, vec],
        out_shape=[jax.ShapeDtypeStruct((t, heads * LANES), F32), jax.ShapeDtypeStruct((t, heads * LANES), BF16),
                   jax.ShapeDtypeStruct((1, LANES), F32)],
        compiler_params=_params(),
    )(dmixed, o, proj, gw_norm, after)


def _place():
    x, y, c = lax.axis_index("x"), lax.axis_index("y"), lax.axis_index("c")
    return x, y, c, [(1 - x, y), (x, 1 - y), (1 - x, 1 - y)]


def _aligned(start, align):
    return start if isinstance(start, int) else pl.multiple_of(start, align)


class _Layout:
    def __init__(self, kind, shard_shape):
        self.kind = kind
        self.r, self.c = shard_shape

    def full_shape(self):
        r, c = self.r, self.c
        return {"major": (N_CHIPS, r, c), "rows": (N_CHIPS * r, c), "cols": (r, N_CHIPS * c)}[self.kind]

    def region(self, ref, j, half=None):
        r, c = self.r, self.c
        r0, nr = (0, r) if half is None else (half * (r // 2), r // 2)
        if self.kind == "major":
            return ref.at[j, pl.ds(_aligned(r0, 16), nr), :]
        if self.kind == "rows":
            return ref.at[pl.ds(_aligned(j * r + r0, 16), nr), :]
        return ref.at[pl.ds(_aligned(r0, 16), nr), pl.ds(_aligned(j * c, LANES), c)]

    def block_spec(self, tr, where):
        r, c = self.r, self.c
        if self.kind == "major":
            return pl.BlockSpec((None, tr, c), lambda *a: (where(*a)[0], where(*a)[1], 0))
        if self.kind == "rows":
            return pl.BlockSpec((tr, c), lambda *a: (where(*a)[0] * (r // tr) + where(*a)[1], 0))
        return pl.BlockSpec((tr, c), lambda *a: (where(*a)[1], where(*a)[0]))


def _remote(src, dst, send_sem, recv_sem, dev):
    return pltpu.make_async_remote_copy(src_ref=src, dst_ref=dst, send_sem=send_sem, recv_sem=recv_sem,
                                        device_id=dev, device_id_type=MESH)


def _all_gather(big, layouts, small):
    nb, ns = len(big), len(small)
    n_remote = 6 * nb + 3 * ns

    def body(*refs):
        ins, outs = refs[:nb + ns], refs[nb + ns:2 * (nb + ns)]
        send_sems, recv_sems, local_sems = refs[2 * (nb + ns):]
        x, y, c, chips = _place()
        j = 2 * x + y
        local = []
        for i in range(ns):
            local.append(pltpu.make_async_copy(ins[nb + i], outs[nb + i].at[j], local_sems.at[i]))
        for cp in local:
            cp.start()
        sends = []
        for i in range(nb):
            for p, (px, py) in enumerate(chips):
                k = 3 * i + p
                mine = layouts[i].region(outs[i], j, c)
                sends.append(_remote(mine, mine, send_sems.at[k], recv_sems.at[k], (px, py, c)))
        for i in range(ns):
            for p, (px, py) in enumerate(chips):
                k = 6 * nb + 3 * i + p
                sends.append(_remote(ins[nb + i], outs[nb + i].at[j], send_sems.at[k], recv_sems.at[k], (px, py, c)))
        for cp in sends:
            cp.start()
        for i in range(nb):
            for p, (px, py) in enumerate(chips):
                k, jp = 3 * i + p, 2 * px + py
                got = layouts[i].region(outs[i], jp, c)
                _remote(got, got, send_sems.at[k], recv_sems.at[k], (px, py, c)).wait_recv()
                fwd = _remote(got, got, send_sems.at[3 * nb + k], recv_sems.at[3 * nb + k], (x, y, 1 - c))
                fwd.start()
                sends.append(fwd)
        for i in range(ns):
            for p, (px, py) in enumerate(chips):
                k, jp = 6 * nb + 3 * i + p, 2 * px + py
                _remote(ins[nb + i], outs[nb + i].at[jp], send_sems.at[k], recv_sems.at[k], (px, py, c)).wait_recv()
        for i in range(nb):
            for p, (px, py) in enumerate(chips):
                k, jp = 3 * nb + 3 * i + p, 2 * px + py
                got = layouts[i].region(outs[i], jp, 1 - c)
                _remote(got, got, send_sems.at[k], recv_sems.at[k], (x, y, 1 - c)).wait_recv()
        for cp in sends:
            cp.wait_send()
        for cp in local:
            cp.wait()

    out_shape = [jax.ShapeDtypeStruct(lay.full_shape(), BF16) for lay in layouts]
    out_shape += [jax.ShapeDtypeStruct((N_CHIPS,) + s.shape, F32) for s in small]
    return pl.pallas_call(
        body, name="all_gather_weights", in_specs=[ANY] * (nb + ns), out_specs=[ANY] * (nb + ns), out_shape=out_shape,
        input_output_aliases={i: i for i in range(nb)},
        scratch_shapes=[pltpu.SemaphoreType.DMA((n_remote,)), pltpu.SemaphoreType.DMA((n_remote,)),
                        pltpu.SemaphoreType.DMA((ns,))],
        compiler_params=_params(),
    )(*big, *small)


HBM = pl.BlockSpec(memory_space=pltpu.HBM)
SEM = pl.BlockSpec(memory_space=pltpu.SEMAPHORE)
SPLIT_COPY = pltpu.CompilerParams(has_side_effects=pltpu.SideEffectType.DATAFLOW_SIDE_EFFECTING)


def _in_hbm(a):
    return pltpu.with_memory_space_constraint(a, pltpu.HBM)


def _split_copy_start(arrays, plan, n_copies, name, after=None):
    na = len(arrays)
    order_only = [] if after is None else [after]

    def body(*refs):
        base = na + len(order_only)
        send_sems, recv_sems = refs[base], refs[base + 1]
        thru, token = refs[base + 2:base + 2 + na], refs[base + 2 + na]
        for k, (src, dst, _, dev) in enumerate(plan(thru, *_place())):
            _remote(src, dst, send_sems.at[k], recv_sems.at[k], dev).start()
        token[...] = jnp.zeros_like(token)

    outs = pl.pallas_call(
        body, name=name, in_specs=[HBM] * na + [ANY] * len(order_only),
        out_specs=[SEM, SEM] + [HBM] * na + [pl.BlockSpec(memory_space=pltpu.VMEM)],
        out_shape=[pltpu.SemaphoreType.DMA((n_copies,)), pltpu.SemaphoreType.DMA((n_copies,))]
        + [pltpu.HBM(a.shape, a.dtype) for a in arrays] + [jax.ShapeDtypeStruct((8, LANES), F32)],
        input_output_aliases={i: 2 + i for i in range(na)}, compiler_params=SPLIT_COPY,
    )(*[_in_hbm(a) for a in arrays], *order_only)
    return (outs[0], outs[1]), list(outs[2:2 + na]), outs[2 + na]


def _split_copy_wait(sems, arrays, plan, name, after):
    na = len(arrays)

    def body(*refs):
        send_sems, recv_sems = refs[na], refs[na + 1]
        thru = refs[na + 3:]
        for k, (src, _, landing, dev) in enumerate(plan(thru, *_place())):
            cp = _remote(src, landing, send_sems.at[k], recv_sems.at[k], dev)
            cp.wait_send()
            cp.wait_recv()

    return list(pl.pallas_call(
        body, name=name, in_specs=[HBM] * na + [SEM, SEM, ANY], out_specs=[HBM] * na,
        out_shape=[pltpu.HBM(a.shape, a.dtype) for a in arrays],
        input_output_aliases={i: i for i in range(na)}, compiler_params=SPLIT_COPY,
    )(*arrays, *sems, after))


def _gather_plan(layouts):
    def plan(bufs, x, y, c, chips):
        copies = []
        for buf, lay in zip(bufs, layouts):
            mine = lay.region(buf, 2 * x + y, c)
            copies += [(mine, mine, lay.region(buf, 2 * px + py, c), (px, py, c)) for px, py in chips]
        return copies
    return plan


def _forward_plan(layouts):
    def plan(bufs, x, y, c, chips):
        copies = []
        for buf, lay in zip(bufs, layouts):
            for px, py in chips:
                got = lay.region(buf, 2 * px + py, c)
                copies.append((got, got, lay.region(buf, 2 * px + py, 1 - c), (x, y, 1 - c)))
        return copies
    return plan


def _halves_plan(layouts):
    def plan(arrays, x, y, c, chips):
        nw = len(layouts)
        copies = []
        for i, lay in enumerate(layouts):
            for j in range(N_CHIPS):
                land = arrays[nw + i].at[j]
                copies.append((lay.region(arrays[i], j, 1 - c), land, land, (x, y, 1 - c)))
        return copies
    return plan


def _partials_plan(nw):
    def plan(arrays, x, y, c, chips):
        copies = []
        for i in range(nw):
            for p, (px, py) in enumerate(chips):
                land = arrays[nw + i].at[p]
                copies.append((arrays[i].at[2 * px + py], land, land, (px, py, c)))
        return copies
    return plan


def _join_plan(nw):
    def plan(arrays, x, y, c, chips):
        return [(arrays[i], arrays[nw + i], arrays[nw + i], (x, y, 1 - c)) for i in range(nw)]
    return plan


def _forward_to_sibling(bufs, layouts, name):
    nb = len(bufs)

    def body(*refs):
        outs = refs[nb:2 * nb]
        send_sems, recv_sems = refs[2 * nb:]
        x, y, c, chips = _place()
        cps = []
        for i in range(nb):
            for p, (px, py) in enumerate(chips):
                got = layouts[i].region(outs[i], 2 * px + py, c)
                cps.append(_remote(got, got, send_sems.at[3 * i + p], recv_sems.at[3 * i + p], (x, y, 1 - c)))
        for cp in cps:
            cp.start()
        for i in range(nb):
            for p, (px, py) in enumerate(chips):
                theirs = layouts[i].region(outs[i], 2 * px + py, 1 - c)
                _remote(theirs, theirs, send_sems.at[3 * i + p], recv_sems.at[3 * i + p], (x, y, 1 - c)).wait_recv()
        for cp in cps:
            cp.wait_send()

    return pl.pallas_call(
        body, name=name, in_specs=[ANY] * nb, out_specs=[ANY] * nb,
        out_shape=[jax.ShapeDtypeStruct(b.shape, b.dtype) for b in bufs],
        input_output_aliases={i: i for i in range(nb)},
        scratch_shapes=[pltpu.SemaphoreType.DMA((3 * nb,)), pltpu.SemaphoreType.DMA((3 * nb,))],
        compiler_params=_params(),
    )(*bufs)


def _halves_to_sibling(grads, layouts, name):
    nw = len(grads)

    def body(*refs):
        ins, gots = refs[:nw], refs[nw:2 * nw]
        send_sems, recv_sems = refs[2 * nw:]
        x, y, c, _ = _place()
        cps = []
        for i in range(nw):
            for j in range(N_CHIPS):
                k = N_CHIPS * i + j
                cps.append(_remote(layouts[i].region(ins[i], j, 1 - c), gots[i].at[j],
                                   send_sems.at[k], recv_sems.at[k], (x, y, 1 - c)))
        for cp in cps:
            cp.start()
        for cp in cps:
            cp.wait()

    half = [jax.ShapeDtypeStruct((N_CHIPS, lay.r // 2, lay.c), BF16) for lay in layouts]
    return pl.pallas_call(
        body, name=name, in_specs=[ANY] * nw, out_specs=[ANY] * nw, out_shape=half,
        scratch_shapes=[pltpu.SemaphoreType.DMA((N_CHIPS * nw,)), pltpu.SemaphoreType.DMA((N_CHIPS * nw,))],
        compiler_params=_params(),
    )(*grads)


def _chip_sum(grad, got, layout, core, name):
    n, hr, c = got.shape
    tr = _tile(hr, 256, 16)
    nb = hr // tr

    def body(core_ref, a_ref, b_ref, o_ref):
        o_ref[...] = (a_ref[...].astype(F32) + b_ref[...].astype(F32)).astype(BF16)

    spec = pl.BlockSpec((None, tr, c), lambda j, i, core_ref: (j, i, 0))
    return pl.pallas_call(
        body, name=name,
        grid_spec=pltpu.PrefetchScalarGridSpec(
            num_scalar_prefetch=1, grid=(n, nb),
            in_specs=[layout.block_spec(tr, lambda j, i, core_ref: (j, core_ref[0] * nb + i)), spec],
            out_specs=spec),
        out_shape=jax.ShapeDtypeStruct((n, hr, c), BF16), compiler_params=_params(),
    )(core, grad, got)


def _halves_start(grads, layouts, name, after=None):
    lands = [lax.empty((N_CHIPS, lay.r // 2, lay.c), BF16) for lay in layouts]
    return _split_copy_start(list(grads) + lands, _halves_plan(layouts), N_CHIPS * len(grads), name, after)


def _partials_start(parts, name, after=None):
    lands = [lax.empty((3,) + p.shape[1:], BF16) for p in parts]
    return _split_copy_start(list(parts) + lands, _partials_plan(len(parts)), 3 * len(parts), name, after)


def _shard_sum(parts, got, chip, name):
    _, r, c = parts.shape
    tr = _tile(r, 256, 16)

    def body(chip_ref, o_ref, g_ref, out_ref):
        acc = o_ref[...].astype(F32)
        for p in range(3):
            acc = acc + g_ref[p].astype(F32)
        out_ref[...] = acc

    return pl.pallas_call(
        body, name=name,
        grid_spec=pltpu.PrefetchScalarGridSpec(
            num_scalar_prefetch=1, grid=(r // tr,),
            in_specs=[pl.BlockSpec((None, tr, c), lambda i, chip_ref: (chip_ref[0], i, 0)),
                      pl.BlockSpec((3, tr, c), lambda i, chip_ref: (0, i, 0))],
            out_specs=pl.BlockSpec((tr, c), lambda i, chip_ref: (i, 0))),
        out_shape=jax.ShapeDtypeStruct((r, c), F32), compiler_params=_params(),
    )(chip, parts, got)


def _join_start(halves, name):
    lands = [lax.empty(h.shape, F32) for h in halves]
    return _split_copy_start(list(halves) + lands, _join_plan(len(halves)), len(halves), name)


def _adamw_rows(w, g_half, m, v, half, filled, after, name):
    r, c = w.shape
    hr = r // 2
    tr = _tile(hr, 128, 8)
    nb = hr // tr
    c1 = 1.0 - ADAM_B1 ** ADAM_STEP
    c2 = 1.0 - ADAM_B2 ** ADAM_STEP
    n_prev = 0 if filled is None else 4

    def body(half_ref, w_ref, gin_ref, m_ref, v_ref, *rest):
        g_ref, d_ref, nm_ref, nv_ref = rest[1 + n_prev:]
        gg = gin_ref[...]
        nm = ADAM_B1 * m_ref[...] + (1.0 - ADAM_B1) * gg
        nv = ADAM_B2 * v_ref[...] + (1.0 - ADAM_B2) * jnp.square(gg)
        m_hat = nm / c1
        v_hat = nv / c2
        g_ref[...] = gg
        d_ref[...] = -ADAM_LR * (m_hat / (jnp.sqrt(v_hat) + ADAM_EPS) + ADAM_WD * w_ref[...])
        nm_ref[...] = nm
        nv_ref[...] = nv

    full = pl.BlockSpec((tr, c), lambda i, half_ref: (half_ref[0] * nb + i, 0))
    part = pl.BlockSpec((tr, c), lambda i, half_ref: (i, 0))
    return pl.pallas_call(
        body, name=name,
        grid_spec=pltpu.PrefetchScalarGridSpec(
            num_scalar_prefetch=1, grid=(nb,), in_specs=[full, part, full, full] + [ANY] * (1 + n_prev),
            out_specs=[full] * 4),
        out_shape=[jax.ShapeDtypeStruct((r, c), F32)] * 4,
        input_output_aliases={6 + k: k for k in range(n_prev)}, compiler_params=_params(),
    )(half, w, g_half, m, v, after, *([] if filled is None else filled))


def _small_all_reduce(buf):
    rows = buf.shape[0]
    n_dev = 8

    def body(b_ref, o_ref, gath, send_sems, recv_sems):
        x, y, c, _ = _place()
        me = 4 * x + 2 * y + c
        gath[me] = b_ref[...]
        cps = []
        for k in range(1, n_dev):
            px, py, pc = (x + (k >> 2)) % 2, (y + ((k >> 1) & 1)) % 2, (c + (k & 1)) % 2
            cps.append(_remote(b_ref, gath.at[me], send_sems.at[k - 1], recv_sems.at[k - 1], (px, py, pc)))
        for cp in cps:
            cp.start()
        for k in range(1, n_dev):
            px, py, pc = (x + (k >> 2)) % 2, (y + ((k >> 1) & 1)) % 2, (c + (k & 1)) % 2
            _remote(b_ref, gath.at[4 * px + 2 * py + pc], send_sems.at[k - 1], recv_sems.at[k - 1], (px, py, pc)).wait_recv()
        for cp in cps:
            cp.wait_send()
        acc = gath[0]
        for dev in range(1, n_dev):
            acc = acc + gath[dev]
        o_ref[...] = acc

    vm = pl.BlockSpec(memory_space=pltpu.VMEM)
    return pl.pallas_call(
        body, name="small_all_reduce", in_specs=[vm], out_specs=vm,
        out_shape=jax.ShapeDtypeStruct((rows, LANES), F32),
        scratch_shapes=[pltpu.VMEM((n_dev, rows, LANES), F32), pltpu.SemaphoreType.DMA((n_dev - 1,)),
                        pltpu.SemaphoreType.DMA((n_dev - 1,))],
        compiler_params=_params(),
    )(buf)


def _pad_lanes(v):
    return jnp.pad(v, ((0, 0), (0, LANES - v.shape[-1])))


def _pack(vectors):
    flat, offs, pos = [], [], 0
    for v in vectors:
        n = v.size
        n_pad = -(-n // LANES) * LANES
        flat.append(jnp.pad(v.reshape(-1), (0, n_pad - n)))
        offs.append((pos, n, v.shape))
        pos += n_pad
    total = -(-pos // (8 * LANES)) * 8 * LANES
    flat.append(jnp.zeros((total - pos,), F32))
    return jnp.concatenate(flat).reshape(-1, LANES), offs


def _unpack(buf, offs):
    flat = buf.reshape(-1)
    return [flat[pos:pos + n].reshape(shape) for pos, n, shape in offs]


def kernel(x, norm_mix_pre, w_in, conv_qkv_w, a_log, dt_bias, gdn_norm_w, conv_sc_w, w_out, norm_mix_post, norm_mlp_pre, w_up, w_down, norm_mlp_post, loss_target, m_norm_mix_pre, m_w_in, m_conv_qkv_w, m_a_log, m_dt_bias, m_gdn_norm_w, m_conv_sc_w, m_w_out, m_norm_mix_post, m_norm_mlp_pre, m_w_up, m_w_down, m_norm_mlp_post, v_norm_mix_pre, v_w_in, v_conv_qkv_w, v_a_log, v_dt_bias, v_gdn_norm_w, v_conv_sc_w, v_w_out, v_norm_mix_post, v_norm_mlp_pre, v_w_up, v_w_down, v_norm_mlp_post):
    bsz, seq, d = x.shape
    t = bsz * seq
    heads, head_dim = a_log.shape[-1], gdn_norm_w.shape[-1]
    assert head_dim == LANES and seq % CHUNK == 0
    gw = heads * head_dim
    sw = conv_sc_w.shape[-1] * N_CHIPS
    ics = w_in.shape[-1]
    main = 4 * gw + 3 * sw
    assert ics * N_CHIPS == main + 2 * heads and 2 * heads <= LANES

    lay_in = _Layout("major", w_in.shape[1:])
    lay_out = _Layout("rows", w_out.shape[1:])
    lay_up = _Layout("cols", w_up.shape[1:])
    lay_down = _Layout("rows", w_down.shape[1:])
    layouts = [lay_in, lay_out, lay_up, lay_down]
    chip = (2 * lax.axis_index("x") + lax.axis_index("y")).astype(jnp.int32).reshape(1)
    core = lax.axis_index("c").astype(jnp.int32).reshape(1)
    x2 = x.reshape(t, d)
    cq_g, cs_g = _all_gather([], [], [conv_qkv_w[0], conv_sc_w[0]])
    plan_in, plan_ou, plan_down = _gather_plan(layouts[:1]), _gather_plan(layouts[1:3]), _gather_plan(layouts[3:])
    in_buf = _cast_into_layout(w_in[0], lay_in, chip, "cast_w_in")
    in_sems, in_bufs, in_token = _split_copy_start([in_buf], plan_in, 3, "gather_in_start", after=cq_g)
    shards = [_cast_into_layout(w[0], lay, chip, f"cast_{n}", after=in_token)
              for w, lay, n in zip((w_out, w_up, w_down), layouts[1:], ("w_out", "w_up", "w_down"))]
    xn = _norm_fwd(x2, norm_mix_pre, shards[-1])
    in_bufs = _split_copy_wait(in_sems, in_bufs, plan_in, "gather_in_wait", xn)
    win_sh, = _forward_to_sibling(in_bufs, layouts[:1], "forward_w_in")
    ou_sems, ou_bufs, ou_token = _split_copy_start(shards[:2], plan_ou, 6, "gather_out_up_start", after=win_sh)
    w_main, w_ab = _repack_w_in(win_sh, gw, heads, sw, ou_token)
    conv_q = cq_g.transpose(1, 0, 2).reshape(conv_qkv_w.shape[1], -1)
    conv_s = cs_g.transpose(1, 0, 2).reshape(conv_sc_w.shape[1], -1)

    tgt2 = loss_target.reshape(t, d)
    proj = _matmul(xn, w_main, "nn", [F32], "proj_main")
    proj_ab = _matmul(xn, w_ab, "nn", [F32], "proj_ab")
    proj3 = proj.reshape(bsz, seq, main)
    qkv_act = _qkv_conv_fwd(proj3, conv_q, 3 * gw).reshape(t, 3 * gw)
    a_log_pad, dt_pad = _pad_lanes(a_log), _pad_lanes(dt_bias)
    gcb, betab = _gates_fwd(proj_ab, a_log_pad, dt_pad, heads)
    o_raw, states = _gdn_fwd(qkv_act, gcb, betab, bsz, heads)
    ou_bufs = _split_copy_wait(ou_sems, ou_bufs, plan_ou, "gather_out_up_wait", o_raw)
    fwd_plan = _forward_plan(layouts[1:3])
    fwd_sems, ou_bufs, fwd_token = _split_copy_start(ou_bufs, fwd_plan, 6, "forward_out_up_start")
    down_sems, down_bufs, down_token = _split_copy_start(shards[2:], plan_down, 3, "gather_down_start", after=fwd_token)
    gdn_out = _gdn_out_fwd(o_raw, proj, gdn_norm_w, heads, 3 * gw, down_token)
    sc_out = _sc_fwd(proj3, conv_s, 4 * gw, sw).reshape(t, sw)
    mixed = jnp.concatenate([gdn_out, sc_out], axis=1)
    wout_f, wup_f = _split_copy_wait(fwd_sems, ou_bufs, fwd_plan, "forward_out_up_wait", mixed)
    mix = _matmul(mixed, wout_f, "nn", [F32], "mix_out")
    h, hn = _mid_fwd(x2, mix, norm_mix_post, norm_mlp_pre)

    def up_epilogue(acc):
        r = jnp.maximum(acc, 0.0)
        return r, r * r

    relu_up, hid = _matmul(hn, wup_f, "nn", [BF16, BF16], "mlp_up", epilogue=up_epilogue)
    down_bufs = _split_copy_wait(down_sems, down_bufs, plan_down, "gather_down_wait", hid)
    (wdown_f,) = _forward_to_sibling(down_bufs, layouts[3:], "forward_w_down")
    ff = _matmul(hid, wdown_f, "nn", [F32], "mlp_down")
    loss_blk, dy, dff, dg_mlp_post = _head_fwd_bwd(h, ff, tgt2, norm_mlp_post)

    def dup_epilogue(acc, r):
        return (acc * (2.0 * r.astype(F32)),)

    d_up = _matmul(dff, wdown_f, "nt", [BF16], "d_hid", epilogue=dup_epilogue, extras=(relu_up,))
    dw_down = _matmul(hid, dff, "tn", [BF16], "dw_down")
    plan_h_down, plan_h_up, plan_h_in = _halves_plan([lay_down]), _halves_plan([lay_up]), _halves_plan([lay_in])
    hd_sems, hd_arrays, hd_token = _halves_start([dw_down], [lay_down], "down_halves_start")
    d_hn = _matmul(d_up, wup_f, "nt", [F32], "d_hn", after=hd_token)
    dw_up = _matmul(hn, d_up, "tn", [BF16], "dw_up")
    dw_down, down_got = _split_copy_wait(hd_sems, hd_arrays, plan_h_down, "down_halves_wait", dw_up)
    hu_sems, hu_arrays, hu_token = _halves_start([dw_up], [lay_up], "up_halves_start", after=down_got)
    down_part = _chip_sum(dw_down, down_got, lay_down, core, "chip_sum_w_down")
    pd_sems, pd_arrays, pd_token = _partials_start([down_part], "down_partials_start", after=hu_token)
    dh, dmix, dg_mlp_pre, dg_mix_post = _mid_bwd(d_hn, h, norm_mlp_pre, dy, mix, norm_mix_post, pd_token)
    dmixed = _matmul(dmix, wout_f, "nt", [F32], "d_mixed")
    dw_out = _matmul(mixed, dmix, "tn", [BF16], "dw_out")
    dw_up, up_got = _split_copy_wait(hu_sems, hu_arrays, plan_h_up, "up_halves_wait", dw_out)
    up_part = _chip_sum(dw_up, up_got, lay_up, core, "chip_sum_w_up")
    out_got, = _halves_to_sibling([dw_out], [lay_out], "out_grad_halves_to_sibling")
    out_part = _chip_sum(dw_out, out_got, lay_out, core, "chip_sum_w_out")
    puo_sems, puo_arrays, puo_token = _partials_start([up_part, out_part], "up_out_partials_start", after=dmixed)
    dmixed3 = dmixed.reshape(bsz, seq, d)
    d_b, d_c, d_hsc, dw_conv_s = _sc_bwd(proj3, dmixed3, conv_s, 4 * gw, sw, gw)
    d_o, d_z, dg_gdn_norm = _gdn_out_bwd(dmixed, o_raw, proj, gdn_norm_w, heads, 3 * gw, puo_token)
    dq, dk, dv, dgc_b, dbeta_b = _gdn_bwd(qkv_act, gcb, betab, states, d_o, bsz, heads)
    d_ab, d_alog, d_dt = _gates_bwd(proj_ab, a_log_pad, dt_pad, dgc_b, dbeta_b, heads)
    d_qkv, dw_conv_q = _qkv_conv_bwd(proj3, [a.reshape(bsz, seq, gw) for a in (dq, dk, dv)], conv_q, 3 * gw)
    d_proj = jnp.concatenate([d_qkv.reshape(t, 3 * gw), d_z, d_b.reshape(t, sw), d_c.reshape(t, sw),
                              d_hsc.reshape(t, sw)], axis=1)
    dw_main = _matmul(xn, d_proj, "tn", [BF16], "dw_in_main")
    dw_ab = _matmul(xn, d_ab, "tn", [BF16], "dw_in_ab")
    dw_in = _unpack_dw_in(dw_main, dw_ab, gw, heads, sw, ics)
    hi_sems, hi_arrays, hi_token = _halves_start([dw_in], [lay_in], "in_halves_start")
    d_xn = _matmul(d_proj, w_main, "nt", [F32], "d_xn_main", after=hi_token)
    dw_in, in_got = _split_copy_wait(hi_sems, hi_arrays, plan_h_in, "in_halves_wait", d_xn)
    in_part = _chip_sum(dw_in, in_got, lay_in, core, "chip_sum_w_in")
    pi_sems, pi_arrays, pi_token = _partials_start([in_part], "in_partials_start")
    d_xn_ab = _matmul(d_ab, w_ab, "nt", [F32], "d_xn_ab", after=pi_token)
    grad_x, dg_mix_pre = _first_bwd(d_xn, d_xn_ab, x2, norm_mix_pre, dh)

    other_core = 1 - core

    def finish(parts, recvs, weights3, names, tag):
        nw = len(parts)
        halves = [_shard_sum(p, r, chip, f"shard_sum_{n}") for p, r, n in zip(parts, recvs, names)]
        sems, arrays, token = _join_start(halves, f"{tag}_join_start")
        own = [_adamw_rows(wt[0], h, m[0], v[0], core, None, token, f"adamw_own_{n}")
               for (wt, m, v), h, n in zip(weights3, arrays[:nw], names)]
        theirs = _split_copy_wait(sems, arrays, _join_plan(nw), f"{tag}_join_wait", own[-1][1])[nw:]
        return [_adamw_rows(wt[0], h, m[0], v[0], other_core, o, h, f"adamw_sibling_{n}")
                for (wt, m, v), h, o, n in zip(weights3, theirs, own, names)]

    down_part, down_recv = _split_copy_wait(pd_sems, pd_arrays, _partials_plan(1), "down_partials_wait", grad_x)
    up_part, out_part, up_recv, out_recv = _split_copy_wait(puo_sems, puo_arrays, _partials_plan(2),
                                                            "up_out_partials_wait", down_recv)
    res_up, res_down = finish([up_part, down_part], [up_recv, down_recv],
                              [(w_up, m_w_up, v_w_up), (w_down, m_w_down, v_w_down)], ("w_up", "w_down"), "mlp")
    in_part, in_recv = _split_copy_wait(pi_sems, pi_arrays, _partials_plan(1), "in_partials_wait", res_down[1])
    res_in, res_out = finish([in_part, out_part], [in_recv, out_recv],
                             [(w_in, m_w_in, v_w_in), (w_out, m_w_out, v_w_out)], ("w_in", "w_out"), "mix")

    small, offs = _pack([loss_blk[0:1, 0:1], dg_mix_pre, dw_conv_q, d_alog[:, :heads], d_dt[:, :heads], dg_gdn_norm,
                         dw_conv_s, dg_mix_post, dg_mlp_pre, dg_mlp_post])
    (loss, g_mix_pre, g_conv_q_full, g_alog, g_dt, g_gdn_norm, g_conv_s_full, g_mix_post, g_mlp_pre,
     g_mlp_post) = _unpack(_small_all_reduce(small), offs)
    j = 2 * lax.axis_index("x") + lax.axis_index("y")
    cq_w, cs_w = conv_qkv_w.shape[-1], conv_sc_w.shape[-1]
    g_conv_q = lax.dynamic_slice_in_dim(g_conv_q_full, j * cq_w, cq_w, axis=1)
    g_conv_s = lax.dynamic_slice_in_dim(g_conv_s_full, j * cs_w, cs_w, axis=1)

    big = {1: res_in, 7: res_out, 10: res_up, 11: res_down}
    grads = [g_mix_pre, None, g_conv_q, g_alog, g_dt, g_gdn_norm, g_conv_s, None, g_mix_post, g_mlp_pre, None,
             None, g_mlp_post]
    weights = [norm_mix_pre, w_in, conv_qkv_w, a_log, dt_bias, gdn_norm_w, conv_sc_w, w_out, norm_mix_post,
               norm_mlp_pre, w_up, w_down, norm_mlp_post]
    ms = [m_norm_mix_pre, m_w_in, m_conv_qkv_w, m_a_log, m_dt_bias, m_gdn_norm_w, m_conv_sc_w, m_w_out,
          m_norm_mix_post, m_norm_mlp_pre, m_w_up, m_w_down, m_norm_mlp_post]
    vs = [v_norm_mix_pre, v_w_in, v_conv_qkv_w, v_a_log, v_dt_bias, v_gdn_norm_w, v_conv_sc_w, v_w_out,
          v_norm_mix_post, v_norm_mlp_pre, v_w_up, v_w_down, v_norm_mlp_post]
    out_g, out_d, out_m, out_v = [], [], [], []
    for i, (wt, g, m, v) in enumerate(zip(weights, grads, ms, vs)):
        shape2 = wt.shape[-2:] if wt.ndim == 3 else wt.shape
        if i in big:
            g2, dl, nm, nv = big[i]
        else:
            g2 = g.reshape(shape2)
            dl, nm, nv = _adamw(wt.reshape(shape2), g2, m.reshape(shape2), v.reshape(shape2), f"adamw_{i}")
        out_g.append(g2.reshape(wt.shape))
        out_d.append(dl.reshape(wt.shape))
        out_m.append(nm.reshape(wt.shape))
        out_v.append(nv.reshape(wt.shape))

    return (loss.reshape(()), grad_x.reshape(bsz, seq, d), *out_g, *out_d, *out_m, *out_v)
```

```python
import functools

import jax
import jax.numpy as jnp
from jax import lax
from jax.experimental import pallas as pl
from jax.experimental.pallas import tpu as pltpu

CHUNK = 64
NORM_EPS = 1e-6
L2_EPS = 1e-6
N_CHIPS = 4
ADAM_LR = 0.001
ADAM_B1 = 0.9
ADAM_B2 = 0.999
ADAM_EPS = 1e-08
ADAM_WD = 0.01
ADAM_STEP = 10
LANES = 128
VMEM_LIMIT = 56 * 1024 * 1024

F32 = jnp.float32
BF16 = jnp.bfloat16
HI = lax.Precision.HIGH
EXACT_SUM = lax.Precision.HIGHEST
MESH = pl.DeviceIdType.MESH
ANY = pl.BlockSpec(memory_space=pl.ANY)


def _params(n_grid=0):
    return pltpu.CompilerParams(vmem_limit_bytes=VMEM_LIMIT)


def _tile(n, pref, align):
    if n <= pref:
        return n
    t = (pref // align) * align
    while t >= align:
        if n % t == 0:
            return t
        t -= align
    raise ValueError(f"no tile for {n}")


def _sigmoid(x):
    return 1.0 / (1.0 + jnp.exp(-x))


def _softplus(x):
    return jnp.maximum(x, 0.0) + jnp.log(1.0 + jnp.exp(-jnp.abs(x)))


def _rms_fwd(x, g):
    r = lax.rsqrt(jnp.mean(x * x, axis=-1, keepdims=True) + NORM_EPS)
    return x * r * g


def _rms_bwd(dy, x, g):
    r = lax.rsqrt(jnp.mean(x * x, axis=-1, keepdims=True) + NORM_EPS)
    xh = x * r
    dxh = dy * g
    dx = r * (dxh - xh * jnp.mean(dxh * xh, axis=-1, keepdims=True))
    dg = jnp.sum(dy * xh, axis=0, keepdims=True)
    return dx, dg


def _matmul(a, b, form, out_dtypes, name, epilogue=None, extras=(), after=None, tm=1024, tn=1024, tk=4096):
    if form == "nn":
        (m, kd), (_, n) = a.shape, b.shape
        dims = (((1,), (0,)), ((), ()))
    elif form == "nt":
        (m, kd), (n, _) = a.shape, b.shape
        dims = (((1,), (1,)), ((), ()))
    else:
        (kd, m), (_, n) = a.shape, b.shape
        dims = (((0,), (0,)), ((), ()))
    tm, tn, tk = _tile(m, tm, LANES), _tile(n, tn, LANES), _tile(kd, tk, LANES)
    nk = kd // tk
    n_extra = len(extras)
    n_out = len(out_dtypes)

    if form == "nn":
        a_spec = pl.BlockSpec((tm, tk), lambda i, j, k: (i, k))
        b_spec = pl.BlockSpec((tk, tn), lambda i, j, k: (k, j))
    elif form == "nt":
        a_spec = pl.BlockSpec((tm, tk), lambda i, j, k: (i, k))
        b_spec = pl.BlockSpec((tn, tk), lambda i, j, k: (j, k))
    else:
        a_spec = pl.BlockSpec((tk, tm), lambda i, j, k: (k, i))
        b_spec = pl.BlockSpec((tk, tn), lambda i, j, k: (k, j))
    tile_spec = pl.BlockSpec((tm, tn), lambda i, j, k: (i, j))

    order_only = [] if after is None else [after]
    n_skip = n_extra + len(order_only)

    def finish(acc, extra_refs, out_refs):
        outs = (acc,) if epilogue is None else epilogue(acc, *[e[...] for e in extra_refs])
        for o_ref, val in zip(out_refs, outs):
            o_ref[...] = val.astype(o_ref.dtype)

    def body(a_ref, b_ref, *rest):
        extra_refs = rest[:n_extra]
        out_refs = rest[n_skip:n_skip + n_out]
        if nk == 1:
            finish(lax.dot_general(a_ref[...], b_ref[...], dims, preferred_element_type=F32), extra_refs, out_refs)
            return
        acc_ref = rest[-1]
        k = pl.program_id(2)

        @pl.when(k == 0)
        def _():
            acc_ref[...] = jnp.zeros_like(acc_ref)

        acc_ref[...] += lax.dot_general(a_ref[...], b_ref[...], dims, preferred_element_type=F32)

        @pl.when(k == nk - 1)
        def _():
            finish(acc_ref[...], extra_refs, out_refs)

    outs = pl.pallas_call(
        body, name=name, grid=(m // tm, n // tn, nk),
        in_specs=[a_spec, b_spec] + [tile_spec] * n_extra + [ANY] * len(order_only),
        out_specs=[tile_spec] * n_out,
        out_shape=[jax.ShapeDtypeStruct((m, n), dt) for dt in out_dtypes],
        scratch_shapes=[pltpu.VMEM((tm, tn), F32)] if nk > 1 else [],
        compiler_params=_params(),
    )(a, b, *extras, *order_only)
    return outs[0] if n_out == 1 else outs


def _cast_into_layout(w, layout, chip, name, after=None):
    r, c = w.shape
    tr = _tile(r, 256, 16)
    order_only = [] if after is None else [after]

    def body(chip_ref, w_ref, *rest):
        rest[-1][...] = w_ref[...].astype(BF16)

    return pl.pallas_call(
        body, name=name,
        grid_spec=pltpu.PrefetchScalarGridSpec(
            num_scalar_prefetch=1, grid=(r // tr,),
            in_specs=[pl.BlockSpec((tr, c), lambda i, chip_ref: (i, 0))] + [ANY] * len(order_only),
            out_specs=layout.block_spec(tr, lambda i, chip_ref: (chip_ref[0], i))),
        out_shape=jax.ShapeDtypeStruct(layout.full_shape(), BF16), compiler_params=_params(),
    )(chip, w, *order_only)


def _in_segments(gw, heads, sw):
    main = 4 * gw
    return [(0, main, 0), (main + 2 * heads, 3 * sw, main), (main, 2 * heads, main + 3 * sw)]


def _pieces(seg_start, width, dst_start, ics):
    out = []
    g = seg_start
    while g < seg_start + width:
        j, cj = divmod(g, ics)
        wdt = min(ics - cj, seg_start + width - g)
        out.append((j, cj, dst_start + (g - seg_start), wdt))
        g += wdt
    return out


def _repack_w_in(w_sh, gw, heads, sw, after):
    ns, d, ics = w_sh.shape
    main = 4 * gw + 3 * sw
    tr = _tile(d, 128, 16)
    pieces = [p for seg in _in_segments(gw, heads, sw) for p in _pieces(*seg, ics)]

    def body(w_ref, after_ref, m_ref, ab_ref):
        ab_ref[...] = jnp.zeros_like(ab_ref)
        for j, cj, cd, wdt in pieces:
            if cd >= main:
                ab_ref[:, cd - main:cd - main + wdt] = w_ref[j, :, cj:cj + wdt]
            else:
                m_ref[:, cd:cd + wdt] = w_ref[j, :, cj:cj + wdt]

    return pl.pallas_call(
        body, name="repack_w_in", grid=(d // tr,),
        in_specs=[pl.BlockSpec((ns, tr, ics), lambda i: (0, i, 0)), ANY],
        out_specs=[pl.BlockSpec((tr, main), lambda i: (i, 0)), pl.BlockSpec((tr, LANES), lambda i: (i, 0))],
        out_shape=[jax.ShapeDtypeStruct((d, main), BF16), jax.ShapeDtypeStruct((d, LANES), BF16)],
        compiler_params=_params(),
    )(w_sh, after)


def _unpack_dw_in(dw_main, dw_ab, gw, heads, sw, ics):
    d = dw_main.shape[0]
    tr = _tile(d, 128, 16)
    main = 4 * gw + 3 * sw
    pieces = [p for seg in _in_segments(gw, heads, sw) for p in _pieces(*seg, ics)]

    def body(m_ref, ab_ref, o_ref):
        for j, cj, cd, wdt in pieces:
            if cd >= main:
                o_ref[j, :, cj:cj + wdt] = ab_ref[:, cd - main:cd - main + wdt].astype(BF16)
            else:
                o_ref[j, :, cj:cj + wdt] = m_ref[:, cd:cd + wdt].astype(BF16)

    return pl.pallas_call(
        body, name="unpack_dw_in", grid=(d // tr,),
        in_specs=[pl.BlockSpec((tr, main), lambda i: (i, 0)), pl.BlockSpec((tr, LANES), lambda i: (i, 0))],
        out_specs=pl.BlockSpec((N_CHIPS, tr, ics), lambda i: (0, i, 0)),
        out_shape=jax.ShapeDtypeStruct((N_CHIPS, d, ics), BF16), compiler_params=_params(),
    )(dw_main, dw_ab)


def _adamw(w, g, m, v, name):
    r, c = w.shape
    tr = _tile(r, 128, 8)
    c1 = 1.0 - ADAM_B1 ** ADAM_STEP
    c2 = 1.0 - ADAM_B2 ** ADAM_STEP

    def body(w_ref, g_ref, m_ref, v_ref, d_ref, nm_ref, nv_ref):
        gg = g_ref[...]
        nm = ADAM_B1 * m_ref[...] + (1.0 - ADAM_B1) * gg
        nv = ADAM_B2 * v_ref[...] + (1.0 - ADAM_B2) * jnp.square(gg)
        m_hat = nm / c1
        v_hat = nv / c2
        d_ref[...] = -ADAM_LR * (m_hat / (jnp.sqrt(v_hat) + ADAM_EPS) + ADAM_WD * w_ref[...])
        nm_ref[...] = nm
        nv_ref[...] = nv

    spec = pl.BlockSpec((tr, c), lambda i: (i, 0))
    return pl.pallas_call(
        body, name=name, grid=(r // tr,), in_specs=[spec] * 4, out_specs=[spec] * 3,
        out_shape=[jax.ShapeDtypeStruct((r, c), F32)] * 3, compiler_params=_params(),
    )(w, g, m, v)


def _row_spec(tt, d):
    return pl.BlockSpec((tt, d), lambda i: (i, 0))


def _vec_spec(d):
    return pl.BlockSpec((1, d), lambda i: (0, 0))


def _norm_fwd(x, g, after):
    t, d = x.shape
    tt = _tile(t, 256, 16)

    def body(x_ref, g_ref, after_ref, o_ref):
        o_ref[...] = _rms_fwd(x_ref[...], g_ref[...]).astype(BF16)

    return pl.pallas_call(
        body, name="norm_mix_pre", grid=(t // tt,), in_specs=[_row_spec(tt, d), _vec_spec(d), ANY],
        out_specs=_row_spec(tt, d), out_shape=jax.ShapeDtypeStruct((t, d), BF16), compiler_params=_params(),
    )(x, g, after)


def _mid_fwd(x, mix, g_post, g_pre):
    t, d = x.shape
    tt = _tile(t, 128, 16)

    def body(x_ref, mix_ref, gp_ref, gn_ref, h_ref, hn_ref):
        h = x_ref[...] + _rms_fwd(mix_ref[...], gp_ref[...])
        h_ref[...] = h
        hn_ref[...] = _rms_fwd(h, gn_ref[...]).astype(BF16)

    return pl.pallas_call(
        body, name="mid_fwd", grid=(t // tt,),
        in_specs=[_row_spec(tt, d), _row_spec(tt, d), _vec_spec(d), _vec_spec(d)],
        out_specs=[_row_spec(tt, d), _row_spec(tt, d)],
        out_shape=[jax.ShapeDtypeStruct((t, d), F32), jax.ShapeDtypeStruct((t, d), BF16)],
        compiler_params=_params(),
    )(x, mix, g_post, g_pre)


def _head_fwd_bwd(h, ff, tgt, g_post):
    t, d = h.shape
    tt = _tile(t, 128, 16)

    def body(h_ref, ff_ref, t_ref, g_ref, loss_ref, dy_ref, dff_ref, dg_ref):
        i = pl.program_id(0)

        @pl.when(i == 0)
        def _():
            loss_ref[...] = jnp.zeros_like(loss_ref)
            dg_ref[...] = jnp.zeros_like(dg_ref)

        ff = ff_ref[...]
        g = g_ref[...]
        e = h_ref[...] + _rms_fwd(ff, g) - t_ref[...]
        loss_ref[...] += 0.5 * jnp.sum(jnp.mean(e * e, axis=-1, keepdims=True))
        dy = e * (1.0 / d)
        dy_ref[...] = dy
        dff, dg = _rms_bwd(dy, ff, g)
        dff_ref[...] = dff.astype(BF16)
        dg_ref[...] += dg

    return pl.pallas_call(
        body, name="loss_head", grid=(t // tt,),
        in_specs=[_row_spec(tt, d)] * 3 + [_vec_spec(d)],
        out_specs=[pl.BlockSpec((8, LANES), lambda i: (0, 0)), _row_spec(tt, d), _row_spec(tt, d), _vec_spec(d)],
        out_shape=[jax.ShapeDtypeStruct((8, LANES), F32), jax.ShapeDtypeStruct((t, d), F32),
                   jax.ShapeDtypeStruct((t, d), BF16), jax.ShapeDtypeStruct((1, d), F32)],
        compiler_params=_params(),
    )(h, ff, tgt, g_post)


def _mid_bwd(d_hn, h, g_pre, dy, mix, g_post, after):
    t, d = h.shape
    tt = _tile(t, 128, 16)

    def body(dhn_ref, h_ref, gn_ref, dy_ref, mix_ref, gp_ref, after_ref, dh_ref, dmix_ref, dgn_ref, dgp_ref):
        i = pl.program_id(0)

        @pl.when(i == 0)
        def _():
            dgn_ref[...] = jnp.zeros_like(dgn_ref)
            dgp_ref[...] = jnp.zeros_like(dgp_ref)

        dx, dgn = _rms_bwd(dhn_ref[...], h_ref[...], gn_ref[...])
        dh = dy_ref[...] + dx
        dh_ref[...] = dh
        dmix, dgp = _rms_bwd(dh, mix_ref[...], gp_ref[...])
        dmix_ref[...] = dmix.astype(BF16)
        dgn_ref[...] += dgn
        dgp_ref[...] += dgp

    return pl.pallas_call(
        body, name="mid_bwd", grid=(t // tt,),
        in_specs=[_row_spec(tt, d), _row_spec(tt, d), _vec_spec(d), _row_spec(tt, d), _row_spec(tt, d), _vec_spec(d),
                  ANY],
        out_specs=[_row_spec(tt, d), _row_spec(tt, d), _vec_spec(d), _vec_spec(d)],
        out_shape=[jax.ShapeDtypeStruct((t, d), F32), jax.ShapeDtypeStruct((t, d), BF16),
                   jax.ShapeDtypeStruct((1, d), F32), jax.ShapeDtypeStruct((1, d), F32)],
        compiler_params=_params(),
    )(d_hn, h, g_pre, dy, mix, g_post, after)


def _first_bwd(d_xn, d_xn_ab, x, g, dh):
    t, d = x.shape
    tt = _tile(t, 128, 16)

    def body(a_ref, b_ref, x_ref, g_ref, dh_ref, dx_ref, dg_ref):
        i = pl.program_id(0)

        @pl.when(i == 0)
        def _():
            dg_ref[...] = jnp.zeros_like(dg_ref)

        dx, dg = _rms_bwd(a_ref[...] + b_ref[...], x_ref[...], g_ref[...])
        dx_ref[...] = dh_ref[...] + dx
        dg_ref[...] += dg

    return pl.pallas_call(
        body, name="first_bwd", grid=(t // tt,),
        in_specs=[_row_spec(tt, d), _row_spec(tt, d), _row_spec(tt, d), _vec_spec(d), _row_spec(tt, d)],
        out_specs=[_row_spec(tt, d), _vec_spec(d)],
        out_shape=[jax.ShapeDtypeStruct((t, d), F32), jax.ShapeDtypeStruct((1, d), F32)],
        compiler_params=_params(),
    )(d_xn, d_xn_ab, x, g, dh)


HALO = 8


def _cur(ts, tc, off):
    return pl.BlockSpec((1, ts, tc), lambda ci, b, s: (b, s, off + ci))


def _prev(ts, tc, off):
    return pl.BlockSpec((1, HALO, tc), lambda ci, b, s: (b, jnp.maximum(s * (ts // HALO) - 1, 0), off + ci))


def _next(ts, tc, off, seq):
    last = seq // HALO - 1
    return pl.BlockSpec((1, HALO, tc), lambda ci, b, s: (b, jnp.minimum((s + 1) * (ts // HALO), last), off + ci))


def _conv_w_spec(kw, tc):
    return pl.BlockSpec((kw, tc), lambda ci, b, s: (0, ci))


def _rows_back(u, prev8, k):
    if k == 0:
        return u
    rolled = pltpu.roll(u, k, axis=0)
    row = lax.broadcasted_iota(jnp.int32, (HALO, u.shape[1]), 0)
    first = jnp.where(row < k, pltpu.roll(prev8, k, axis=0), rolled[0:HALO])
    return first if u.shape[0] == HALO else jnp.concatenate([first, rolled[HALO:]], axis=0)


def _rows_ahead(g, next8, m):
    if m == 0:
        return g
    n = g.shape[0]
    rolled = pltpu.roll(g, n - m, axis=0)
    row = lax.broadcasted_iota(jnp.int32, (HALO, g.shape[1]), 0)
    last = jnp.where(row >= HALO - m, pltpu.roll(next8, HALO - m, axis=0), rolled[n - HALO:n])
    return last if n == HALO else jnp.concatenate([rolled[0:n - HALO], last], axis=0)


def _windows(u, prev8, kw):
    return [_rows_back(u, prev8, kw - 1 - j) for j in range(kw)]


def _tap_sum(w, wins):
    acc = w[0:1, :] * wins[0]
    for j in range(1, len(wins)):
        acc = acc + w[j:j + 1, :] * wins[j]
    return acc


def _silu_grad(x):
    s = _sigmoid(x)
    return s * (1.0 + x * (1.0 - s))


def _qkv_conv_fwd(proj3, w, width):
    bsz, seq, _ = proj3.shape
    kw = w.shape[0]
    ts, tc = _tile(seq, 256, 8), _tile(width, 512, LANES)

    def body(u_ref, up_ref, w_ref, o_ref):
        prev8 = jnp.where(pl.program_id(2) == 0, 0.0, up_ref[0])
        pre = _tap_sum(w_ref[...], _windows(u_ref[0], prev8, kw))
        o_ref[0] = pre * _sigmoid(pre)

    return pl.pallas_call(
        body, name="qkv_conv_fwd", grid=(width // tc, bsz, seq // ts),
        in_specs=[_cur(ts, tc, 0), _prev(ts, tc, 0), _conv_w_spec(kw, tc)],
        out_specs=_cur(ts, tc, 0), out_shape=jax.ShapeDtypeStruct((bsz, seq, width), F32),
        compiler_params=_params(),
    )(proj3, proj3, w)


def _qkv_conv_bwd(proj3, dparts, w, width, into3):
    bsz, seq, _ = proj3.shape
    kw = w.shape[0]
    n_parts = len(dparts)
    part_w = width // n_parts
    ts, tc = _tile(seq, 256, 8), _tile(part_w, 512, LANES)
    n_s = seq // ts
    npt = part_w // tc
    last = seq // HALO - 1

    def part_cur(p):
        def index(ci, b, s):
            use = (ci // npt) == p
            return jnp.where(use, b, 0), jnp.where(use, s, 0), jnp.where(use, ci % npt, 0)
        return pl.BlockSpec((1, ts, tc), index)

    def part_next(p):
        def index(ci, b, s):
            use = (ci // npt) == p
            return (jnp.where(use, b, 0), jnp.where(use, jnp.minimum((s + 1) * (ts // HALO), last), 0),
                    jnp.where(use, ci % npt, 0))
        return pl.BlockSpec((1, HALO, tc), index)

    def body(u_ref, up_ref, un_ref, *rest):
        d_refs, dn_refs = rest[:n_parts], rest[n_parts:2 * n_parts]
        w_ref, _, du_ref, dw_ref, dbuf, dnbuf = rest[2 * n_parts:]
        ci, b, s = pl.program_id(0), pl.program_id(1), pl.program_id(2)

        @pl.when((b == 0) & (s == 0))
        def _():
            dw_ref[...] = jnp.zeros_like(dw_ref)

        for p in range(n_parts):
            @pl.when(ci // npt == p)
            def _(p=p):
                dbuf[...] = d_refs[p][0]
                dnbuf[...] = dn_refs[p][0]

        w = w_ref[...]
        u = u_ref[0]
        wins = _windows(u, jnp.where(s == 0, 0.0, up_ref[0]), kw)
        wins_next = _windows(un_ref[0], u[ts - HALO:ts], kw)
        g = dbuf[...] * _silu_grad(_tap_sum(w, wins))
        g_next = jnp.where(s == n_s - 1, 0.0, dnbuf[...] * _silu_grad(_tap_sum(w, wins_next)))
        for j in range(kw):
            dw_ref[j:j + 1, :] += jnp.sum(g * wins[j], axis=0, keepdims=True)
        du_ref[0] = _tap_sum(w, [_rows_ahead(g, g_next, kw - 1 - j) for j in range(kw)]).astype(BF16)

    return pl.pallas_call(
        body, name="qkv_conv_bwd", grid=(width // tc, bsz, n_s),
        in_specs=[_cur(ts, tc, 0), _prev(ts, tc, 0), _next(ts, tc, 0, seq)]
        + [part_cur(p) for p in range(n_parts)] + [part_next(p) for p in range(n_parts)] + [_conv_w_spec(kw, tc), ANY],
        out_specs=[_cur(ts, tc, 0), _conv_w_spec(kw, tc)],
        out_shape=[jax.ShapeDtypeStruct(into3.shape, BF16), jax.ShapeDtypeStruct((kw, width), F32)],
        input_output_aliases={4 + 2 * n_parts: 0},
        scratch_shapes=[pltpu.VMEM((ts, tc), F32), pltpu.VMEM((HALO, tc), F32)],
        compiler_params=_params(),
    )(proj3, proj3, proj3, *dparts, *dparts, w, into3)


def _sc_fwd(proj3, w, off, sw, into3, into_off):
    bsz, seq, _ = proj3.shape
    kw = w.shape[0]
    ts, tc = _tile(seq, 256, 8), _tile(sw, 512, LANES)
    ob, oc, oh = off // tc, (off + sw) // tc, (off + 2 * sw) // tc

    def body(b_ref, c_ref, cp_ref, h_ref, hp_ref, w_ref, into_ref, o_ref):
        prev8 = jnp.where(pl.program_id(2) == 0, 0.0, cp_ref[0] * hp_ref[0])
        o_ref[0] = (b_ref[0] * _tap_sum(w_ref[...], _windows(c_ref[0] * h_ref[0], prev8, kw))).astype(BF16)

    return pl.pallas_call(
        body, name="sc_fwd", grid=(sw // tc, bsz, seq // ts),
        in_specs=[_cur(ts, tc, ob), _cur(ts, tc, oc), _prev(ts, tc, oc), _cur(ts, tc, oh), _prev(ts, tc, oh),
                  _conv_w_spec(kw, tc), ANY],
        out_specs=_cur(ts, tc, into_off // tc), out_shape=jax.ShapeDtypeStruct(into3.shape, BF16),
        input_output_aliases={6: 0}, compiler_params=_params(),
    )(proj3, proj3, proj3, proj3, proj3, w, into3)


def _sc_bwd(proj3, dmixed3, w, off, sw, d_off):
    bsz, seq, _ = proj3.shape
    kw = w.shape[0]
    ts, tc = _tile(seq, 256, 8), _tile(sw, 512, LANES)
    n_s = seq // ts
    ob, oc, oh, od = off // tc, (off + sw) // tc, (off + 2 * sw) // tc, d_off // tc

    def body(d_ref, dn_ref, b_ref, bn_ref, c_ref, cp_ref, h_ref, hp_ref, w_ref,
             db_ref, dc_ref, dh_ref, dw_ref):
        b, s = pl.program_id(1), pl.program_id(2)

        @pl.when((b == 0) & (s == 0))
        def _():
            dw_ref[...] = jnp.zeros_like(dw_ref)

        w = w_ref[...]
        cc, hh = c_ref[0], h_ref[0]
        wins = _windows(cc * hh, jnp.where(s == 0, 0.0, cp_ref[0] * hp_ref[0]), kw)
        dout = d_ref[0]
        db_ref[0] = (dout * _tap_sum(w, wins)).astype(BF16)
        g = dout * b_ref[0]
        g_next = jnp.where(s == n_s - 1, 0.0, dn_ref[0] * bn_ref[0])
        for j in range(kw):
            dw_ref[j:j + 1, :] += jnp.sum(g * wins[j], axis=0, keepdims=True)
        dp = _tap_sum(w, [_rows_ahead(g, g_next, kw - 1 - j) for j in range(kw)])
        dc_ref[0] = (dp * hh).astype(BF16)
        dh_ref[0] = (dp * cc).astype(BF16)

    out = jax.ShapeDtypeStruct((bsz, seq, sw), BF16)
    return pl.pallas_call(
        body, name="sc_bwd", grid=(sw // tc, bsz, n_s),
        in_specs=[_cur(ts, tc, od), _next(ts, tc, od, seq), _cur(ts, tc, ob), _next(ts, tc, ob, seq),
                  _cur(ts, tc, oc), _prev(ts, tc, oc), _cur(ts, tc, oh), _prev(ts, tc, oh), _conv_w_spec(kw, tc)],
        out_specs=[_cur(ts, tc, 0)] * 3 + [_conv_w_spec(kw, tc)],
        out_shape=[out, out, out, jax.ShapeDtypeStruct((kw, sw), F32)],
        compiler_params=_params(),
    )(dmixed3, dmixed3, proj3, proj3, proj3, proj3, proj3, proj3, w)


def _tri_ones(lower):
    i = lax.broadcasted_iota(jnp.int32, (CHUNK, CHUNK), 0)
    j = lax.broadcasted_iota(jnp.int32, (CHUNK, CHUNK), 1)
    return jnp.where((i >= j) if lower else (j >= i), 1.0, 0.0).astype(F32)


def _gates_fwd(proj_ab, a_log_pad, dt_pad, heads):
    t = proj_ab.shape[0]
    gw = heads * LANES

    def body(ab_ref, al_ref, dt_ref, gc_ref, beta_ref):
        ab = ab_ref[...]
        g = -jnp.exp(al_ref[...]) * _softplus(ab + dt_ref[...])
        gc = jnp.dot(_tri_ones(True), g, precision=EXACT_SUM, preferred_element_type=F32)
        beta = _sigmoid(ab)
        for h in range(heads):
            gc_ref[:, h * LANES:(h + 1) * LANES] = jnp.broadcast_to(gc[:, h:h + 1], (CHUNK, LANES))
            beta_ref[:, h * LANES:(h + 1) * LANES] = jnp.broadcast_to(beta[:, heads + h:heads + h + 1], (CHUNK, LANES))

    return pl.pallas_call(
        body, name="gates_fwd", grid=(t // CHUNK,),
        in_specs=[_row_spec(CHUNK, LANES), _vec_spec(LANES), _vec_spec(LANES)],
        out_specs=[_row_spec(CHUNK, gw), _row_spec(CHUNK, gw)],
        out_shape=[jax.ShapeDtypeStruct((t, gw), F32)] * 2, compiler_params=_params(),
    )(proj_ab, a_log_pad, dt_pad)


def _gates_bwd(proj_ab, a_log_pad, dt_pad, dgc_b, dbeta_b, heads):
    t = proj_ab.shape[0]
    gw = heads * LANES

    def body(ab_ref, al_ref, dt_ref, dgc_ref, dbeta_ref, dab_ref, dal_ref, ddt_ref):
        i = pl.program_id(0)

        @pl.when(i == 0)
        def _():
            dal_ref[...] = jnp.zeros_like(dal_ref)
            ddt_ref[...] = jnp.zeros_like(ddt_ref)

        lane = lax.broadcasted_iota(jnp.int32, (CHUNK, LANES), 1)
        dgc = jnp.zeros((CHUNK, LANES), F32)
        dbeta = jnp.zeros((CHUNK, LANES), F32)
        for h in range(heads):
            dgc = jnp.where(lane == h, dgc_ref[:, h * LANES:(h + 1) * LANES], dgc)
            dbeta = jnp.where(lane == heads + h, dbeta_ref[:, h * LANES:(h + 1) * LANES], dbeta)
        dg = jnp.dot(_tri_ones(False), dgc, precision=EXACT_SUM, preferred_element_type=F32)
        ab = ab_ref[...]
        z = ab + dt_ref[...]
        ea = jnp.exp(al_ref[...])
        da = dg * (-ea) * _sigmoid(z)
        beta = _sigmoid(ab)
        db = dbeta * beta * (1.0 - beta)
        dab_ref[...] = jnp.where(lane < heads, da, jnp.where(lane < 2 * heads, db, 0.0)).astype(BF16)
        da_m = jnp.where(lane < heads, da, 0.0)
        ddt_ref[...] += jnp.sum(da_m, axis=0, keepdims=True)
        dal_ref[...] += jnp.sum(jnp.where(lane < heads, dg * (-ea) * _softplus(z), 0.0), axis=0, keepdims=True)

    return pl.pallas_call(
        body, name="gates_bwd", grid=(t // CHUNK,),
        in_specs=[_row_spec(CHUNK, LANES), _vec_spec(LANES), _vec_spec(LANES), _row_spec(CHUNK, gw), _row_spec(CHUNK, gw)],
        out_specs=[_row_spec(CHUNK, LANES), _vec_spec(LANES), _vec_spec(LANES)],
        out_shape=[jax.ShapeDtypeStruct((t, LANES), BF16), jax.ShapeDtypeStruct((1, LANES), F32),
                   jax.ShapeDtypeStruct((1, LANES), F32)],
        compiler_params=_params(),
    )(proj_ab, a_log_pad, dt_pad, dgc_b, dbeta_b)


def _dot(a, b, dims, hi=False):
    if hi:
        return lax.dot_general(a, b, (dims, ((), ())), precision=HI, preferred_element_type=F32)
    return lax.dot_general(a.astype(BF16), b.astype(BF16), (dims, ((), ())), preferred_element_type=F32)


NN = ((1,), (0,))
NT = ((1,), (1,))
TN = ((0,), (0,))


def _each(f, *lists):
    return [f(*xs) for xs in zip(*lists)]


def _dots(a, b, dims, hi=False):
    return _each(lambda x, y: _dot(x, y, dims, hi=hi), a, b)


def _unit_lower_inverse(ms):
    i = lax.broadcasted_iota(jnp.int32, (CHUNK, CHUNK), 0)
    j = lax.broadcasted_iota(jnp.int32, (CHUNK, CHUNK), 1)
    eye = jnp.where(i == j, 1.0, 0.0).astype(F32)
    ts = [eye - jnp.where(jnp.right_shift(i, 1) == jnp.right_shift(j, 1), m, 0.0) for m in ms]
    shift = 1
    while (1 << shift) < CHUNK:
        same_pair = jnp.right_shift(i, shift + 1) == jnp.right_shift(j, shift + 1)
        other_half = jnp.right_shift(i, shift) != jnp.right_shift(j, shift)
        offs = [jnp.where(same_pair & other_half, m, 0.0) for m in ms]
        corr = _dots(_dots(ts, offs, NN, hi=True), ts, NN, hi=True)
        ts = _each(lambda t, c: t - c, ts, corr)
        shift += 1
    return ts


def _chunk_local(qrs, krs, vs, gcbs, betabs, head_dim):
    i = lax.broadcasted_iota(jnp.int32, (CHUNK, CHUNK), 0)
    j = lax.broadcasted_iota(jnp.int32, (CHUNK, CHUNK), 1)
    scale = head_dim ** -0.5
    rqs = [lax.rsqrt(jnp.sum(q * q, axis=-1, keepdims=True) + L2_EPS) for q in qrs]
    rks = [lax.rsqrt(jnp.sum(k * k, axis=-1, keepdims=True) + L2_EPS) for k in krs]
    qhs = _each(lambda a, r: a * r, qrs, rqs)
    ks = _each(lambda a, r: a * r, krs, rks)
    qs = [a * scale for a in qhs]
    decays = [jnp.exp(jnp.where(i >= j, g[:, 0:CHUNK] - g.T[0:CHUNK, :], -jnp.inf)) for g in gcbs]
    kks = _dots(ks, ks, NT)
    qks = _dots(qs, ks, NT)
    ms = _each(lambda b, kk, d: jnp.where(i > j, b[:, 0:CHUNK] * kk * d, 0.0), betabs, kks, decays)
    tinvs = _unit_lower_inverse(ms)
    egs = [jnp.exp(g) for g in gcbs]
    rhs_ws = _each(lambda k, b, e: k * b * e, ks, betabs, egs)
    us = _dots(tinvs, _each(lambda v, b: v * b, vs, betabs), NN, hi=True)
    ws = _dots(tinvs, rhs_ws, NN, hi=True)
    out = []
    for h in range(len(qrs)):
        g_last = gcbs[h][CHUNK - 1:CHUNK, :]
        e_last = jnp.exp(g_last - gcbs[h])
        out.append(dict(rq=rqs[h], rk=rks[h], qh=qhs[h], q=qs[h], k=ks[h], decay=decays[h],
                        beta_col=betabs[h][:, 0:CHUNK], kk=kks[h], m=ms[h], tinv=tinvs[h], eg=egs[h], rhs_w=rhs_ws[h],
                        u=us[h], w=ws[h], p=qks[h] * decays[h], qd=qs[h] * egs[h], kd=ks[h] * e_last, e_last=e_last,
                        gl=jnp.exp(g_last), scale=scale, strict=i > j, incl=i >= j))
    return out


def _field(dicts, name):
    return [d[name] for d in dicts]


GDN_HEAD_GROUP = 16


def _gdn_specs(n_chunks, heads, reverse):
    hg = min(GDN_HEAD_GROUP, heads)
    assert heads % hg == 0

    def cidx(c):
        return (n_chunks - 1 - c) if reverse else c

    def tok(off):
        return pl.BlockSpec((CHUNK, hg * LANES), lambda b, h, c: (b * n_chunks + cidx(c), off // hg + h))

    state = pl.BlockSpec((None, hg, LANES, LANES), lambda b, h, c: (b * n_chunks + cidx(c), h, 0, 0))
    return hg, tok, state


def _gdn_fwd(qkv_act, gcb, betab, bsz, heads):
    t = qkv_act.shape[0]
    n_chunks = t // bsz // CHUNK
    hg, tok, state = _gdn_specs(n_chunks, heads, False)

    def body(q_ref, k_ref, v_ref, gc_ref, beta_ref, o_ref, s_ref, st):
        @pl.when(pl.program_id(2) == 0)
        def _():
            st[...] = jnp.zeros_like(st)

        sls = [slice(hh * LANES, (hh + 1) * LANES) for hh in range(hg)]
        loc = _chunk_local(*[[r[:, sl] for sl in sls] for r in (q_ref, k_ref, v_ref, gc_ref, beta_ref)], LANES)
        s0 = [st[hh] for hh in range(hg)]
        v_new = _each(lambda u, ws: u - ws, _field(loc, "u"), _dots(_field(loc, "w"), s0, NN))
        o_state = _dots(_field(loc, "qd"), s0, NN)
        o_local = _dots(_field(loc, "p"), v_new, NN)
        s_add = _dots(_field(loc, "kd"), v_new, TN)
        for hh in range(hg):
            o_ref[:, sls[hh]] = o_state[hh] + o_local[hh]
            s_ref[hh] = s0[hh]
            st[hh] = s0[hh] * loc[hh]["gl"] + s_add[hh]

    return pl.pallas_call(
        body, name="gdn_fwd", grid=(bsz, heads // hg, n_chunks),
        in_specs=[tok(0), tok(heads), tok(2 * heads), tok(0), tok(0)],
        out_specs=[tok(0), state],
        out_shape=[jax.ShapeDtypeStruct((t, heads * LANES), F32),
                   jax.ShapeDtypeStruct((bsz * n_chunks, heads, LANES, LANES), F32)],
        scratch_shapes=[pltpu.VMEM((hg, LANES, LANES), F32)], compiler_params=_params(),
    )(qkv_act, qkv_act, qkv_act, gcb, betab)


def _gdn_bwd(qkv_act, gcb, betab, states, d_o, bsz, heads):
    t = qkv_act.shape[0]
    n_chunks = t // bsz // CHUNK
    hg, tok, state = _gdn_specs(n_chunks, heads, True)

    def rowsum(a):
        return jnp.sum(a, axis=-1, keepdims=True)

    def finish_head(sl, L, v, betab, d_qd, d_kd, d_gl, d_p, d_m, d_rhs_u, d_rhs_w, d_q, d_k,
                    dq_ref, dk_ref, dv_ref, dgc_ref, dbeta_ref):
        k, decay = L["k"], L["decay"]
        dv_ref[:, sl] = betab * d_rhs_u
        e = d_m * L["m"] + d_p * L["p"]
        d_beta = rowsum(d_m * L["kk"] * decay) + rowsum(d_rhs_u * v) + rowsum(d_rhs_w * k * L["eg"])
        s_kd = rowsum(d_kd * L["kd"])
        d_gc = (rowsum(e) - rowsum(e.T) + rowsum(d_rhs_w * L["rhs_w"]) + rowsum(d_qd * L["qd"]) - s_kd)
        row = lax.broadcasted_iota(jnp.int32, (CHUNK, 1), 0)
        d_gc = d_gc + jnp.where(row == CHUNK - 1, jnp.sum(s_kd) + d_gl * jnp.sum(L["gl"][:, 0:1]), 0.0)
        dgc_ref[:, sl] = jnp.broadcast_to(d_gc, (CHUNK, LANES))
        dbeta_ref[:, sl] = jnp.broadcast_to(d_beta, (CHUNK, LANES))
        d_qh = d_q * L["scale"]
        dq_ref[:, sl] = L["rq"] * (d_qh - L["qh"] * rowsum(d_qh * L["qh"]))
        dk_ref[:, sl] = L["rk"] * (d_k - k * rowsum(d_k * k))

    def body(q_ref, k_ref, v_ref, gc_ref, beta_ref, s_ref, do_ref, dq_ref, dk_ref, dv_ref, dgc_ref, dbeta_ref, dst):
        @pl.when(pl.program_id(2) == 0)
        def _():
            dst[...] = jnp.zeros_like(dst)

        sls = [slice(hh * LANES, (hh + 1) * LANES) for hh in range(hg)]
        vs = [v_ref[:, sl] for sl in sls]
        betabs = [beta_ref[:, sl] for sl in sls]
        loc = _chunk_local([q_ref[:, sl] for sl in sls], [k_ref[:, sl] for sl in sls], vs,
                           [gc_ref[:, sl] for sl in sls], betabs, LANES)
        q, k, u, w, p, tinv, decay, qd, kd, eg = (_field(loc, n) for n in
                                                  ("q", "k", "u", "w", "p", "tinv", "decay", "qd", "kd", "eg"))
        s0 = [s_ref[hh] for hh in range(hg)]
        d_out = [do_ref[:, sl] for sl in sls]
        ds1 = [dst[hh] for hh in range(hg)]
        v_new = _each(lambda a, b: a - b, u, _dots(w, s0, NN))

        d_vnew = _each(lambda a, b: a + b, _dots(p, d_out, TN), _dots(kd, ds1, NN))
        d_qd = _dots(d_out, s0, NT)
        d_kd = _dots(v_new, ds1, NT)
        d_gl = _each(lambda a, b: jnp.sum(a * b), ds1, s0)
        d_p = _each(lambda L, a: jnp.where(L["incl"], a, 0.0), loc, _dots(d_out, v_new, NT))
        d_w = [-a for a in _dots(d_vnew, s0, NT)]
        ds_out, ds_vn = _dots(qd, d_out, TN), _dots(w, d_vnew, TN)
        for hh in range(hg):
            dst[hh] = ds_out[hh] + ds1[hh] * loc[hh]["gl"] - ds_vn[hh]

        d_rhs_u = _dots(tinv, d_vnew, TN, hi=True)
        d_rhs_w = _dots(tinv, d_w, TN, hi=True)
        d_a = _each(lambda a, b: -(a + b), _dots(d_rhs_u, u, NT, hi=True), _dots(d_rhs_w, w, NT, hi=True))
        d_m = _each(lambda L, a: jnp.where(L["strict"], a, 0.0), loc, d_a)
        g_kk = _each(lambda L, a: a * L["beta_col"] * L["decay"], loc, d_m)
        h_qk = _each(lambda a, d: a * d, d_p, decay)

        d_q = _each(lambda a, e, b: a + e * b, _dots(h_qk, k, NN), eg, d_qd)
        d_k = _each(lambda a, b, c, L, bb, rw, dk: a + b + c + bb * L["eg"] * rw + L["e_last"] * dk,
                    _dots(g_kk, k, NN), _dots(g_kk, k, TN), _dots(h_qk, q, TN), loc, betabs, d_rhs_w, d_kd)
        for hh in range(hg):
            finish_head(sls[hh], loc[hh], vs[hh], betabs[hh], d_qd[hh], d_kd[hh], d_gl[hh], d_p[hh], d_m[hh],
                        d_rhs_u[hh], d_rhs_w[hh], d_q[hh], d_k[hh], dq_ref, dk_ref, dv_ref, dgc_ref, dbeta_ref)

    tok_shape = jax.ShapeDtypeStruct((t, heads * LANES), F32)
    return pl.pallas_call(
        body, name="gdn_bwd", grid=(bsz, heads // hg, n_chunks),
        in_specs=[tok(0), tok(heads), tok(2 * heads), tok(0), tok(0), state, tok(0)],
        out_specs=[tok(0)] * 5,
        out_shape=[tok_shape] * 5,
        scratch_shapes=[pltpu.VMEM((hg, LANES, LANES), F32)], compiler_params=_params(),
    )(qkv_act, qkv_act, qkv_act, gcb, betab, states, d_o)


def _gdn_out_fwd(o, proj, gw_norm, heads, z_off, out_width, after):
    t = o.shape[0]
    ts = _tile(t, 512, 16)
    zb = z_off // LANES

    def body(o_ref, z_ref, w_ref, after_ref, out_ref):
        z = z_ref[...]
        out_ref[...] = (_rms_fwd(o_ref[...], w_ref[...]) * (z * _sigmoid(z))).astype(BF16)

    return pl.pallas_call(
        body, name="gdn_out_fwd", grid=(t // ts, heads),
        in_specs=[pl.BlockSpec((ts, LANES), lambda i, h: (i, h)), pl.BlockSpec((ts, LANES), lambda i, h: (i, zb + h)),
                  pl.BlockSpec((1, LANES), lambda i, h: (0, 0)), ANY],
        out_specs=pl.BlockSpec((ts, LANES), lambda i, h: (i, h)),
        out_shape=jax.ShapeDtypeStruct((t, out_width), BF16), compiler_params=_params(),
    )(o, proj, gw_norm, after)


def _gdn_out_bwd(dmixed, o, proj, gw_norm, heads, z_off, after):
    t = o.shape[0]
    ts = _tile(t, 512, 16)
    zb = z_off // LANES

    def body(d_ref, o_ref, z_ref, w_ref, after_ref, do_ref, dz_ref, dw_ref):
        @pl.when((pl.program_id(0) == 0) & (pl.program_id(1) == 0))
        def _():
            dw_ref[...] = jnp.zeros_like(dw_ref)

        d, oo, z, w = d_ref[...], o_ref[...], z_ref[...], w_ref[...]
        on = _rms_fwd(oo, w)
        dz_ref[...] = (d * on * _silu_grad(z)).astype(BF16)
        d_o, d_w = _rms_bwd(d * (z * _sigmoid(z)), oo, w)
        do_ref[...] = d_o
        dw_ref[...] += d_w

    blk = pl.BlockSpec((ts, LANES), lambda i, h: (i, h))
    vec = pl.BlockSpec((1, LANES), lambda i, h: (0, 0))
    return pl.pallas_call(
        body, name="gdn_out_bwd", grid=(t // ts, heads),
        in_specs=[blk, blk, pl.BlockSpec((ts, LANES), lambda i, h: (i, zb + h)), vec, ANY],
        out_specs=[blk, pl.BlockSpec((ts, LANES), lambda i, h: (i, zb + h)), vec],
        out_shape=[jax.ShapeDtypeStruct((t, heads * LANES), F32), jax.ShapeDtypeStruct((t, proj.shape[1]), BF16),
                   jax.ShapeDtypeStruct((1, LANES), F32)],
        compiler_params=_params(),
    )(dmixed, o, proj, gw_norm, after)


def _place_columns(into3, pieces, off):
    bsz, seq, _ = into3.shape
    n_parts, part_w = len(pieces), pieces[0].shape[-1]
    ts, tc = _tile(seq, 512, 16), _tile(part_w, 1024, LANES)
    npt = part_w // tc

    def part(p):
        def index(ci, b, s):
            use = (ci // npt) == p
            return jnp.where(use, b, 0), jnp.where(use, s, 0), jnp.where(use, ci % npt, 0)
        return pl.BlockSpec((1, ts, tc), index)

    def body(*refs):
        o_ref = refs[-1]
        for p in range(n_parts):
            @pl.when(pl.program_id(0) // npt == p)
            def _(p=p):
                o_ref[...] = refs[p][...]

    return pl.pallas_call(
        body, name="place_columns", grid=(n_parts * npt, bsz, seq // ts),
        in_specs=[part(p) for p in range(n_parts)] + [ANY],
        out_specs=_cur(ts, tc, off // tc), out_shape=jax.ShapeDtypeStruct(into3.shape, into3.dtype),
        input_output_aliases={n_parts: 0}, compiler_params=_params(),
    )(*pieces, into3)


def _place():
    x, y, c = lax.axis_index("x"), lax.axis_index("y"), lax.axis_index("c")
    return x, y, c, [(1 - x, y), (x, 1 - y), (1 - x, 1 - y)]


def _aligned(start, align):
    return start if isinstance(start, int) else pl.multiple_of(start, align)


class _Layout:
    def __init__(self, kind, shard_shape):
        self.kind = kind
        self.r, self.c = shard_shape

    def full_shape(self):
        r, c = self.r, self.c
        return {"major": (N_CHIPS, r, c), "rows": (N_CHIPS * r, c), "cols": (r, N_CHIPS * c)}[self.kind]

    def region(self, ref, j, half=None):
        r, c = self.r, self.c
        r0, nr = (0, r) if half is None else (half * (r // 2), r // 2)
        if self.kind == "major":
            return ref.at[j, pl.ds(_aligned(r0, 16), nr), :]
        if self.kind == "rows":
            return ref.at[pl.ds(_aligned(j * r + r0, 16), nr), :]
        return ref.at[pl.ds(_aligned(r0, 16), nr), pl.ds(_aligned(j * c, LANES), c)]

    def block_spec(self, tr, where):
        r, c = self.r, self.c
        if self.kind == "major":
            return pl.BlockSpec((None, tr, c), lambda *a: (where(*a)[0], where(*a)[1], 0))
        if self.kind == "rows":
            return pl.BlockSpec((tr, c), lambda *a: (where(*a)[0] * (r // tr) + where(*a)[1], 0))
        return pl.BlockSpec((tr, c), lambda *a: (where(*a)[1], where(*a)[0]))


def _remote(src, dst, send_sem, recv_sem, dev):
    return pltpu.make_async_remote_copy(src_ref=src, dst_ref=dst, send_sem=send_sem, recv_sem=recv_sem,
                                        device_id=dev, device_id_type=MESH)


def _all_gather(big, layouts, small):
    nb, ns = len(big), len(small)
    n_remote = 6 * nb + 3 * ns

    def body(*refs):
        ins, outs = refs[:nb + ns], refs[nb + ns:2 * (nb + ns)]
        send_sems, recv_sems, local_sems = refs[2 * (nb + ns):]
        x, y, c, chips = _place()
        j = 2 * x + y
        local = []
        for i in range(ns):
            local.append(pltpu.make_async_copy(ins[nb + i], outs[nb + i].at[j], local_sems.at[i]))
        for cp in local:
            cp.start()
        sends = []
        for i in range(nb):
            for p, (px, py) in enumerate(chips):
                k = 3 * i + p
                mine = layouts[i].region(outs[i], j, c)
                sends.append(_remote(mine, mine, send_sems.at[k], recv_sems.at[k], (px, py, c)))
        for i in range(ns):
            for p, (px, py) in enumerate(chips):
                k = 6 * nb + 3 * i + p
                sends.append(_remote(ins[nb + i], outs[nb + i].at[j], send_sems.at[k], recv_sems.at[k], (px, py, c)))
        for cp in sends:
            cp.start()
        for i in range(nb):
            for p, (px, py) in enumerate(chips):
                k, jp = 3 * i + p, 2 * px + py
                got = layouts[i].region(outs[i], jp, c)
                _remote(got, got, send_sems.at[k], recv_sems.at[k], (px, py, c)).wait_recv()
                fwd = _remote(got, got, send_sems.at[3 * nb + k], recv_sems.at[3 * nb + k], (x, y, 1 - c))
                fwd.start()
                sends.append(fwd)
        for i in range(ns):
            for p, (px, py) in enumerate(chips):
                k, jp = 6 * nb + 3 * i + p, 2 * px + py
                _remote(ins[nb + i], outs[nb + i].at[jp], send_sems.at[k], recv_sems.at[k], (px, py, c)).wait_recv()
        for i in range(nb):
            for p, (px, py) in enumerate(chips):
                k, jp = 3 * nb + 3 * i + p, 2 * px + py
                got = layouts[i].region(outs[i], jp, 1 - c)
                _remote(got, got, send_sems.at[k], recv_sems.at[k], (x, y, 1 - c)).wait_recv()
        for cp in sends:
            cp.wait_send()
        for cp in local:
            cp.wait()

    out_shape = [jax.ShapeDtypeStruct(lay.full_shape(), BF16) for lay in layouts]
    out_shape += [jax.ShapeDtypeStruct((N_CHIPS,) + s.shape, F32) for s in small]
    return pl.pallas_call(
        body, name="all_gather_weights", in_specs=[ANY] * (nb + ns), out_specs=[ANY] * (nb + ns), out_shape=out_shape,
        input_output_aliases={i: i for i in range(nb)},
        scratch_shapes=[pltpu.SemaphoreType.DMA((n_remote,)), pltpu.SemaphoreType.DMA((n_remote,)),
                        pltpu.SemaphoreType.DMA((ns,))],
        compiler_params=_params(),
    )(*big, *small)


HBM = pl.BlockSpec(memory_space=pltpu.HBM)
SEM = pl.BlockSpec(memory_space=pltpu.SEMAPHORE)
SPLIT_COPY = pltpu.CompilerParams(has_side_effects=pltpu.SideEffectType.DATAFLOW_SIDE_EFFECTING)


def _in_hbm(a):
    return pltpu.with_memory_space_constraint(a, pltpu.HBM)


def _split_copy_start(arrays, plan, n_copies, name, after=None):
    na = len(arrays)
    order_only = [] if after is None else [after]

    def body(*refs):
        base = na + len(order_only)
        send_sems, recv_sems = refs[base], refs[base + 1]
        thru, token = refs[base + 2:base + 2 + na], refs[base + 2 + na]
        for k, (src, dst, _, dev) in enumerate(plan(thru, *_place())):
            _remote(src, dst, send_sems.at[k], recv_sems.at[k], dev).start()
        token[...] = jnp.zeros_like(token)

    outs = pl.pallas_call(
        body, name=name, in_specs=[HBM] * na + [ANY] * len(order_only),
        out_specs=[SEM, SEM] + [HBM] * na + [pl.BlockSpec(memory_space=pltpu.VMEM)],
        out_shape=[pltpu.SemaphoreType.DMA((n_copies,)), pltpu.SemaphoreType.DMA((n_copies,))]
        + [pltpu.HBM(a.shape, a.dtype) for a in arrays] + [jax.ShapeDtypeStruct((8, LANES), F32)],
        input_output_aliases={i: 2 + i for i in range(na)}, compiler_params=SPLIT_COPY,
    )(*[_in_hbm(a) for a in arrays], *order_only)
    return (outs[0], outs[1]), list(outs[2:2 + na]), outs[2 + na]


def _split_copy_wait(sems, arrays, plan, name, after):
    na = len(arrays)

    def body(*refs):
        send_sems, recv_sems = refs[na], refs[na + 1]
        thru = refs[na + 3:]
        for k, (src, _, landing, dev) in enumerate(plan(thru, *_place())):
            cp = _remote(src, landing, send_sems.at[k], recv_sems.at[k], dev)
            cp.wait_send()
            cp.wait_recv()

    return list(pl.pallas_call(
        body, name=name, in_specs=[HBM] * na + [SEM, SEM, ANY], out_specs=[HBM] * na,
        out_shape=[pltpu.HBM(a.shape, a.dtype) for a in arrays],
        input_output_aliases={i: i for i in range(na)}, compiler_params=SPLIT_COPY,
    )(*arrays, *sems, after))


def _gather_plan(layouts):
    def plan(bufs, x, y, c, chips):
        copies = []
        for buf, lay in zip(bufs, layouts):
            mine = lay.region(buf, 2 * x + y, c)
            copies += [(mine, mine, lay.region(buf, 2 * px + py, c), (px, py, c)) for px, py in chips]
        return copies
    return plan


def _forward_plan(layouts):
    def plan(bufs, x, y, c, chips):
        copies = []
        for buf, lay in zip(bufs, layouts):
            for px, py in chips:
                got = lay.region(buf, 2 * px + py, c)
                copies.append((got, got, lay.region(buf, 2 * px + py, 1 - c), (x, y, 1 - c)))
        return copies
    return plan


def _halves_plan(layouts):
    def plan(arrays, x, y, c, chips):
        nw = len(layouts)
        copies = []
        for i, lay in enumerate(layouts):
            for j in range(N_CHIPS):
                land = arrays[nw + i].at[j]
                copies.append((lay.region(arrays[i], j, 1 - c), land, land, (x, y, 1 - c)))
        return copies
    return plan


def _partials_plan(nw):
    def plan(arrays, x, y, c, chips):
        copies = []
        for i in range(nw):
            for p, (px, py) in enumerate(chips):
                land = arrays[nw + i].at[p]
                copies.append((arrays[i].at[2 * px + py], land, land, (px, py, c)))
        return copies
    return plan


def _join_plan(nw):
    def plan(arrays, x, y, c, chips):
        return [(arrays[i], arrays[nw + i], arrays[nw + i], (x, y, 1 - c)) for i in range(nw)]
    return plan


def _forward_to_sibling(bufs, layouts, name):
    nb = len(bufs)

    def body(*refs):
        outs = refs[nb:2 * nb]
        send_sems, recv_sems = refs[2 * nb:]
        x, y, c, chips = _place()
        cps = []
        for i in range(nb):
            for p, (px, py) in enumerate(chips):
                got = layouts[i].region(outs[i], 2 * px + py, c)
                cps.append(_remote(got, got, send_sems.at[3 * i + p], recv_sems.at[3 * i + p], (x, y, 1 - c)))
        for cp in cps:
            cp.start()
        for i in range(nb):
            for p, (px, py) in enumerate(chips):
                theirs = layouts[i].region(outs[i], 2 * px + py, 1 - c)
                _remote(theirs, theirs, send_sems.at[3 * i + p], recv_sems.at[3 * i + p], (x, y, 1 - c)).wait_recv()
        for cp in cps:
            cp.wait_send()

    return pl.pallas_call(
        body, name=name, in_specs=[ANY] * nb, out_specs=[ANY] * nb,
        out_shape=[jax.ShapeDtypeStruct(b.shape, b.dtype) for b in bufs],
        input_output_aliases={i: i for i in range(nb)},
        scratch_shapes=[pltpu.SemaphoreType.DMA((3 * nb,)), pltpu.SemaphoreType.DMA((3 * nb,))],
        compiler_params=_params(),
    )(*bufs)


def _halves_to_sibling(grads, layouts, name):
    nw = len(grads)

    def body(*refs):
        ins, gots = refs[:nw], refs[nw:2 * nw]
        send_sems, recv_sems = refs[2 * nw:]
        x, y, c, _ = _place()
        cps = []
        for i in range(nw):
            for j in range(N_CHIPS):
                k = N_CHIPS * i + j
                cps.append(_remote(layouts[i].region(ins[i], j, 1 - c), gots[i].at[j],
                                   send_sems.at[k], recv_sems.at[k], (x, y, 1 - c)))
        for cp in cps:
            cp.start()
        for cp in cps:
            cp.wait()

    half = [jax.ShapeDtypeStruct((N_CHIPS, lay.r // 2, lay.c), BF16) for lay in layouts]
    return pl.pallas_call(
        body, name=name, in_specs=[ANY] * nw, out_specs=[ANY] * nw, out_shape=half,
        scratch_shapes=[pltpu.SemaphoreType.DMA((N_CHIPS * nw,)), pltpu.SemaphoreType.DMA((N_CHIPS * nw,))],
        compiler_params=_params(),
    )(*grads)


def _chip_sum(grad, got, layout, core, name):
    n, hr, c = got.shape
    tr = _tile(hr, 256, 16)
    nb = hr // tr

    def body(core_ref, a_ref, b_ref, o_ref):
        o_ref[...] = (a_ref[...].astype(F32) + b_ref[...].astype(F32)).astype(BF16)

    spec = pl.BlockSpec((None, tr, c), lambda j, i, core_ref: (j, i, 0))
    return pl.pallas_call(
        body, name=name,
        grid_spec=pltpu.PrefetchScalarGridSpec(
            num_scalar_prefetch=1, grid=(n, nb),
            in_specs=[layout.block_spec(tr, lambda j, i, core_ref: (j, core_ref[0] * nb + i)), spec],
            out_specs=spec),
        out_shape=jax.ShapeDtypeStruct((n, hr, c), BF16), compiler_params=_params(),
    )(core, grad, got)


def _halves_start(grads, layouts, name, after=None):
    lands = [lax.empty((N_CHIPS, lay.r // 2, lay.c), BF16) for lay in layouts]
    return _split_copy_start(list(grads) + lands, _halves_plan(layouts), N_CHIPS * len(grads), name, after)


def _partials_start(parts, name, after=None):
    lands = [lax.empty((3,) + p.shape[1:], BF16) for p in parts]
    return _split_copy_start(list(parts) + lands, _partials_plan(len(parts)), 3 * len(parts), name, after)


def _shard_sum(parts, got, chip, name):
    _, r, c = parts.shape
    tr = _tile(r, 256, 16)

    def body(chip_ref, o_ref, g_ref, out_ref):
        acc = o_ref[...].astype(F32)
        for p in range(3):
            acc = acc + g_ref[p].astype(F32)
        out_ref[...] = acc

    return pl.pallas_call(
        body, name=name,
        grid_spec=pltpu.PrefetchScalarGridSpec(
            num_scalar_prefetch=1, grid=(r // tr,),
            in_specs=[pl.BlockSpec((None, tr, c), lambda i, chip_ref: (chip_ref[0], i, 0)),
                      pl.BlockSpec((3, tr, c), lambda i, chip_ref: (0, i, 0))],
            out_specs=pl.BlockSpec((tr, c), lambda i, chip_ref: (i, 0))),
        out_shape=jax.ShapeDtypeStruct((r, c), F32), compiler_params=_params(),
    )(chip, parts, got)


def _join_start(halves, name):
    lands = [lax.empty(h.shape, F32) for h in halves]
    return _split_copy_start(list(halves) + lands, _join_plan(len(halves)), len(halves), name)


def _adamw_rows(w, g_half, m, v, half, filled, after, name):
    r, c = w.shape
    hr = r // 2
    tr = _tile(hr, 128, 8)
    nb = hr // tr
    c1 = 1.0 - ADAM_B1 ** ADAM_STEP
    c2 = 1.0 - ADAM_B2 ** ADAM_STEP
    n_prev = 0 if filled is None else 4

    def body(half_ref, w_ref, gin_ref, m_ref, v_ref, *rest):
        g_ref, d_ref, nm_ref, nv_ref = rest[1 + n_prev:]
        gg = gin_ref[...]
        nm = ADAM_B1 * m_ref[...] + (1.0 - ADAM_B1) * gg
        nv = ADAM_B2 * v_ref[...] + (1.0 - ADAM_B2) * jnp.square(gg)
        m_hat = nm / c1
        v_hat = nv / c2
        g_ref[...] = gg
        d_ref[...] = -ADAM_LR * (m_hat / (jnp.sqrt(v_hat) + ADAM_EPS) + ADAM_WD * w_ref[...])
        nm_ref[...] = nm
        nv_ref[...] = nv

    full = pl.BlockSpec((tr, c), lambda i, half_ref: (half_ref[0] * nb + i, 0))
    part = pl.BlockSpec((tr, c), lambda i, half_ref: (i, 0))
    return pl.pallas_call(
        body, name=name,
        grid_spec=pltpu.PrefetchScalarGridSpec(
            num_scalar_prefetch=1, grid=(nb,), in_specs=[full, part, full, full] + [ANY] * (1 + n_prev),
            out_specs=[full] * 4),
        out_shape=[jax.ShapeDtypeStruct((r, c), F32)] * 4,
        input_output_aliases={6 + k: k for k in range(n_prev)}, compiler_params=_params(),
    )(half, w, g_half, m, v, after, *([] if filled is None else filled))


def _small_all_reduce(buf):
    rows = buf.shape[0]
    n_dev = 8

    def body(b_ref, o_ref, gath, send_sems, recv_sems):
        x, y, c, _ = _place()
        me = 4 * x + 2 * y + c
        gath[me] = b_ref[...]
        cps = []
        for k in range(1, n_dev):
            px, py, pc = (x + (k >> 2)) % 2, (y + ((k >> 1) & 1)) % 2, (c + (k & 1)) % 2
            cps.append(_remote(b_ref, gath.at[me], send_sems.at[k - 1], recv_sems.at[k - 1], (px, py, pc)))
        for cp in cps:
            cp.start()
        for k in range(1, n_dev):
            px, py, pc = (x + (k >> 2)) % 2, (y + ((k >> 1) & 1)) % 2, (c + (k & 1)) % 2
            _remote(b_ref, gath.at[4 * px + 2 * py + pc], send_sems.at[k - 1], recv_sems.at[k - 1], (px, py, pc)).wait_recv()
        for cp in cps:
            cp.wait_send()
        acc = gath[0]
        for dev in range(1, n_dev):
            acc = acc + gath[dev]
        o_ref[...] = acc

    vm = pl.BlockSpec(memory_space=pltpu.VMEM)
    return pl.pallas_call(
        body, name="small_all_reduce", in_specs=[vm], out_specs=vm,
        out_shape=jax.ShapeDtypeStruct((rows, LANES), F32),
        scratch_shapes=[pltpu.VMEM((n_dev, rows, LANES), F32), pltpu.SemaphoreType.DMA((n_dev - 1,)),
                        pltpu.SemaphoreType.DMA((n_dev - 1,))],
        compiler_params=_params(),
    )(buf)


def _pad_lanes(v):
    return jnp.pad(v, ((0, 0), (0, LANES - v.shape[-1])))


def _pack(vectors):
    flat, offs, pos = [], [], 0
    for v in vectors:
        n = v.size
        n_pad = -(-n // LANES) * LANES
        flat.append(jnp.pad(v.reshape(-1), (0, n_pad - n)))
        offs.append((pos, n, v.shape))
        pos += n_pad
    total = -(-pos // (8 * LANES)) * 8 * LANES
    flat.append(jnp.zeros((total - pos,), F32))
    return jnp.concatenate(flat).reshape(-1, LANES), offs


def _unpack(buf, offs):
    flat = buf.reshape(-1)
    return [flat[pos:pos + n].reshape(shape) for pos, n, shape in offs]


def kernel(x, norm_mix_pre, w_in, conv_qkv_w, a_log, dt_bias, gdn_norm_w, conv_sc_w, w_out, norm_mix_post, norm_mlp_pre, w_up, w_down, norm_mlp_post, loss_target, m_norm_mix_pre, m_w_in, m_conv_qkv_w, m_a_log, m_dt_bias, m_gdn_norm_w, m_conv_sc_w, m_w_out, m_norm_mix_post, m_norm_mlp_pre, m_w_up, m_w_down, m_norm_mlp_post, v_norm_mix_pre, v_w_in, v_conv_qkv_w, v_a_log, v_dt_bias, v_gdn_norm_w, v_conv_sc_w, v_w_out, v_norm_mix_post, v_norm_mlp_pre, v_w_up, v_w_down, v_norm_mlp_post):
    bsz, seq, d = x.shape
    t = bsz * seq
    heads, head_dim = a_log.shape[-1], gdn_norm_w.shape[-1]
    assert head_dim == LANES and seq % CHUNK == 0
    gw = heads * head_dim
    sw = conv_sc_w.shape[-1] * N_CHIPS
    ics = w_in.shape[-1]
    main = 4 * gw + 3 * sw
    assert ics * N_CHIPS == main + 2 * heads and 2 * heads <= LANES

    lay_in = _Layout("major", w_in.shape[1:])
    lay_out = _Layout("rows", w_out.shape[1:])
    lay_up = _Layout("cols", w_up.shape[1:])
    lay_down = _Layout("rows", w_down.shape[1:])
    layouts = [lay_in, lay_out, lay_up, lay_down]
    chip = (2 * lax.axis_index("x") + lax.axis_index("y")).astype(jnp.int32).reshape(1)
    core = lax.axis_index("c").astype(jnp.int32).reshape(1)
    x2 = x.reshape(t, d)
    cq_g, cs_g = _all_gather([], [], [conv_qkv_w[0], conv_sc_w[0]])
    plan_in, plan_ou, plan_down = _gather_plan(layouts[:1]), _gather_plan(layouts[1:3]), _gather_plan(layouts[3:])
    in_buf = _cast_into_layout(w_in[0], lay_in, chip, "cast_w_in")
    in_sems, in_bufs, in_token = _split_copy_start([in_buf], plan_in, 3, "gather_in_start", after=cq_g)
    shards = [_cast_into_layout(w[0], lay, chip, f"cast_{n}", after=in_token)
              for w, lay, n in zip((w_out, w_up, w_down), layouts[1:], ("w_out", "w_up", "w_down"))]
    xn = _norm_fwd(x2, norm_mix_pre, shards[-1])
    in_bufs = _split_copy_wait(in_sems, in_bufs, plan_in, "gather_in_wait", xn)
    win_sh, = _forward_to_sibling(in_bufs, layouts[:1], "forward_w_in")
    ou_sems, ou_bufs, ou_token = _split_copy_start(shards[:2], plan_ou, 6, "gather_out_up_start", after=win_sh)
    w_main, w_ab = _repack_w_in(win_sh, gw, heads, sw, ou_token)
    conv_q = cq_g.transpose(1, 0, 2).reshape(conv_qkv_w.shape[1], -1)
    conv_s = cs_g.transpose(1, 0, 2).reshape(conv_sc_w.shape[1], -1)

    tgt2 = loss_target.reshape(t, d)
    proj = _matmul(xn, w_main, "nn", [F32], "proj_main")
    proj_ab = _matmul(xn, w_ab, "nn", [F32], "proj_ab")
    proj3 = proj.reshape(bsz, seq, main)
    qkv_act = _qkv_conv_fwd(proj3, conv_q, 3 * gw).reshape(t, 3 * gw)
    a_log_pad, dt_pad = _pad_lanes(a_log), _pad_lanes(dt_bias)
    gcb, betab = _gates_fwd(proj_ab, a_log_pad, dt_pad, heads)
    o_raw, states = _gdn_fwd(qkv_act, gcb, betab, bsz, heads)
    ou_bufs = _split_copy_wait(ou_sems, ou_bufs, plan_ou, "gather_out_up_wait", o_raw)
    fwd_plan = _forward_plan(layouts[1:3])
    fwd_sems, ou_bufs, fwd_token = _split_copy_start(ou_bufs, fwd_plan, 6, "forward_out_up_start")
    down_sems, down_bufs, down_token = _split_copy_start(shards[2:], plan_down, 3, "gather_down_start", after=fwd_token)
    gdn_out = _gdn_out_fwd(o_raw, proj, gdn_norm_w, heads, 3 * gw, gw + sw, down_token)
    mixed = _sc_fwd(proj3, conv_s, 4 * gw, sw, gdn_out.reshape(bsz, seq, gw + sw), gw).reshape(t, gw + sw)
    wout_f, wup_f = _split_copy_wait(fwd_sems, ou_bufs, fwd_plan, "forward_out_up_wait", mixed)
    mix = _matmul(mixed, wout_f, "nn", [F32], "mix_out")
    h, hn = _mid_fwd(x2, mix, norm_mix_post, norm_mlp_pre)

    def up_epilogue(acc):
        r = jnp.maximum(acc, 0.0)
        return r, r * r

    relu_up, hid = _matmul(hn, wup_f, "nn", [BF16, BF16], "mlp_up", epilogue=up_epilogue)
    down_bufs = _split_copy_wait(down_sems, down_bufs, plan_down, "gather_down_wait", hid)
    (wdown_f,) = _forward_to_sibling(down_bufs, layouts[3:], "forward_w_down")
    ff = _matmul(hid, wdown_f, "nn", [F32], "mlp_down")
    loss_blk, dy, dff, dg_mlp_post = _head_fwd_bwd(h, ff, tgt2, norm_mlp_post)

    def dup_epilogue(acc, r):
        return (acc * (2.0 * r.astype(F32)),)

    d_up = _matmul(dff, wdown_f, "nt", [BF16], "d_hid", epilogue=dup_epilogue, extras=(relu_up,))
    dw_down = _matmul(hid, dff, "tn", [BF16], "dw_down")
    plan_h_down, plan_h_up, plan_h_in = _halves_plan([lay_down]), _halves_plan([lay_up]), _halves_plan([lay_in])
    hd_sems, hd_arrays, hd_token = _halves_start([dw_down], [lay_down], "down_halves_start")
    d_hn = _matmul(d_up, wup_f, "nt", [F32], "d_hn", after=hd_token)
    dw_up = _matmul(hn, d_up, "tn", [BF16], "dw_up")
    dw_down, down_got = _split_copy_wait(hd_sems, hd_arrays, plan_h_down, "down_halves_wait", dw_up)
    hu_sems, hu_arrays, hu_token = _halves_start([dw_up], [lay_up], "up_halves_start", after=down_got)
    down_part = _chip_sum(dw_down, down_got, lay_down, core, "chip_sum_w_down")
    pd_sems, pd_arrays, pd_token = _partials_start([down_part], "down_partials_start", after=hu_token)
    dh, dmix, dg_mlp_pre, dg_mix_post = _mid_bwd(d_hn, h, norm_mlp_pre, dy, mix, norm_mix_post, pd_token)
    dmixed = _matmul(dmix, wout_f, "nt", [F32], "d_mixed")
    dw_out = _matmul(mixed, dmix, "tn", [BF16], "dw_out")
    dw_up, up_got = _split_copy_wait(hu_sems, hu_arrays, plan_h_up, "up_halves_wait", dw_out)
    up_part = _chip_sum(dw_up, up_got, lay_up, core, "chip_sum_w_up")
    out_got, = _halves_to_sibling([dw_out], [lay_out], "out_grad_halves_to_sibling")
    out_part = _chip_sum(dw_out, out_got, lay_out, core, "chip_sum_w_out")
    puo_sems, puo_arrays, puo_token = _partials_start([up_part, out_part], "up_out_partials_start", after=dmixed)
    dmixed3 = dmixed.reshape(bsz, seq, d)
    d_b, d_c, d_hsc, dw_conv_s = _sc_bwd(proj3, dmixed3, conv_s, 4 * gw, sw, gw)
    d_o, d_z, dg_gdn_norm = _gdn_out_bwd(dmixed, o_raw, proj, gdn_norm_w, heads, 3 * gw, puo_token)
    dq, dk, dv, dgc_b, dbeta_b = _gdn_bwd(qkv_act, gcb, betab, states, d_o, bsz, heads)
    d_ab, d_alog, d_dt = _gates_bwd(proj_ab, a_log_pad, dt_pad, dgc_b, dbeta_b, heads)
    d_proj3, dw_conv_q = _qkv_conv_bwd(proj3, [a.reshape(bsz, seq, gw) for a in (dq, dk, dv)], conv_q, 3 * gw,
                                       d_z.reshape(bsz, seq, main))
    d_proj = _place_columns(d_proj3, [d_b, d_c, d_hsc], 4 * gw).reshape(t, main)
    dw_main = _matmul(xn, d_proj, "tn", [BF16], "dw_in_main")
    dw_ab = _matmul(xn, d_ab, "tn", [BF16], "dw_in_ab")
    dw_in = _unpack_dw_in(dw_main, dw_ab, gw, heads, sw, ics)
    hi_sems, hi_arrays, hi_token = _halves_start([dw_in], [lay_in], "in_halves_start")
    d_xn = _matmul(d_proj, w_main, "nt", [F32], "d_xn_main", after=hi_token)
    dw_in, in_got = _split_copy_wait(hi_sems, hi_arrays, plan_h_in, "in_halves_wait", d_xn)
    in_part = _chip_sum(dw_in, in_got, lay_in, core, "chip_sum_w_in")
    pi_sems, pi_arrays, pi_token = _partials_start([in_part], "in_partials_start")
    d_xn_ab = _matmul(d_ab, w_ab, "nt", [F32], "d_xn_ab", after=pi_token)
    grad_x, dg_mix_pre = _first_bwd(d_xn, d_xn_ab, x2, norm_mix_pre, dh)

    other_core = 1 - core

    def finish(parts, recvs, weights3, names, tag):
        nw = len(parts)
        halves = [_shard_sum(p, r, chip, f"shard_sum_{n}") for p, r, n in zip(parts, recvs, names)]
        sems, arrays, token = _join_start(halves, f"{tag}_join_start")
        own = [_adamw_rows(wt[0], h, m[0], v[0], core, None, token, f"adamw_own_{n}")
               for (wt, m, v), h, n in zip(weights3, arrays[:nw], names)]
        theirs = _split_copy_wait(sems, arrays, _join_plan(nw), f"{tag}_join_wait", own[-1][1])[nw:]
        return [_adamw_rows(wt[0], h, m[0], v[0], other_core, o, h, f"adamw_sibling_{n}")
                for (wt, m, v), h, o, n in zip(weights3, theirs, own, names)]

    down_part, down_recv = _split_copy_wait(pd_sems, pd_arrays, _partials_plan(1), "down_partials_wait", grad_x)
    up_part, out_part, up_recv, out_recv = _split_copy_wait(puo_sems, puo_arrays, _partials_plan(2),
                                                            "up_out_partials_wait", down_recv)
    res_up, res_down = finish([up_part, down_part], [up_recv, down_recv],
                              [(w_up, m_w_up, v_w_up), (w_down, m_w_down, v_w_down)], ("w_up", "w_down"), "mlp")
    in_part, in_recv = _split_copy_wait(pi_sems, pi_arrays, _partials_plan(1), "in_partials_wait", res_down[1])
    res_in, res_out = finish([in_part, out_part], [in_recv, out_recv],
                             [(w_in, m_w_in, v_w_in), (w_out, m_w_out, v_w_out)], ("w_in", "w_out"), "mix")

    small, offs = _pack([loss_blk[0:1, 0:1], dg_mix_pre, dw_conv_q, d_alog[:, :heads], d_dt[:, :heads], dg_gdn_norm,
                         dw_conv_s, dg_mix_post, dg_mlp_pre, dg_mlp_post])
    (loss, g_mix_pre, g_conv_q_full, g_alog, g_dt, g_gdn_norm, g_conv_s_full, g_mix_post, g_mlp_pre,
     g_mlp_post) = _unpack(_small_all_reduce(small), offs)
    j = 2 * lax.axis_index("x") + lax.axis_index("y")
    cq_w, cs_w = conv_qkv_w.shape[-1], conv_sc_w.shape[-1]
    g_conv_q = lax.dynamic_slice_in_dim(g_conv_q_full, j * cq_w, cq_w, axis=1)
    g_conv_s = lax.dynamic_slice_in_dim(g_conv_s_full, j * cs_w, cs_w, axis=1)

    big = {1: res_in, 7: res_out, 10: res_up, 11: res_down}
    grads = [g_mix_pre, None, g_conv_q, g_alog, g_dt, g_gdn_norm, g_conv_s, None, g_mix_post, g_mlp_pre, None,
             None, g_mlp_post]
    weights = [norm_mix_pre, w_in, conv_qkv_w, a_log, dt_bias, gdn_norm_w, conv_sc_w, w_out, norm_mix_post,
               norm_mlp_pre, w_up, w_down, norm_mlp_post]
    ms = [m_norm_mix_pre, m_w_in, m_conv_qkv_w, m_a_log, m_dt_bias, m_gdn_norm_w, m_conv_sc_w, m_w_out,
          m_norm_mix_post, m_norm_mlp_pre, m_w_up, m_w_down, m_norm_mlp_post]
    vs = [v_norm_mix_pre, v_w_in, v_conv_qkv_w, v_a_log, v_dt_bias, v_gdn_norm_w, v_conv_sc_w, v_w_out,
          v_norm_mix_post, v_norm_mlp_pre, v_w_up, v_w_down, v_norm_mlp_post]
    out_g, out_d, out_m, out_v = [], [], [], []
    for i, (wt, g, m, v) in enumerate(zip(weights, grads, ms, vs)):
        shape2 = wt.shape[-2:] if wt.ndim == 3 else wt.shape
        if i in big:
            g2, dl, nm, nv = big[i]
        else:
            g2 = g.reshape(shape2)
            dl, nm, nv = _adamw(wt.reshape(shape2), g2, m.reshape(shape2), v.reshape(shape2), f"adamw_{i}")
        out_g.append(g2.reshape(wt.shape))
        out_d.append(dl.reshape(wt.shape))
        out_m.append(nm.reshape(wt.shape))
        out_v.append(nv.reshape(wt.shape))

    return (loss.reshape(()), grad_x.reshape(bsz, seq, d), *out_g, *out_d, *out_m, *out_v)
```

```python
import functools

import jax
import jax.numpy as jnp
from jax import lax
from jax.experimental import pallas as pl
from jax.experimental.pallas import tpu as pltpu

CHUNK = 64
NORM_EPS = 1e-6
L2_EPS = 1e-6
N_CHIPS = 4
ADAM_LR = 0.001
ADAM_B1 = 0.9
ADAM_B2 = 0.999
ADAM_EPS = 1e-08
ADAM_WD = 0.01
ADAM_STEP = 10
LANES = 128
VMEM_LIMIT = 56 * 1024 * 1024

F32 = jnp.float32
BF16 = jnp.bfloat16
HI = lax.Precision.HIGH
EXACT_SUM = lax.Precision.HIGHEST
MESH = pl.DeviceIdType.MESH
ANY = pl.BlockSpec(memory_space=pl.ANY)


def _params(n_grid=0):
    return pltpu.CompilerParams(vmem_limit_bytes=VMEM_LIMIT)


def _tile(n, pref, align):
    if n <= pref:
        return n
    t = (pref // align) * align
    while t >= align:
        if n % t == 0:
            return t
        t -= align
    raise ValueError(f"no tile for {n}")


def _sigmoid(x):
    return 1.0 / (1.0 + jnp.exp(-x))


def _softplus(x):
    return jnp.maximum(x, 0.0) + jnp.log(1.0 + jnp.exp(-jnp.abs(x)))


def _rms_fwd(x, g):
    r = lax.rsqrt(jnp.mean(x * x, axis=-1, keepdims=True) + NORM_EPS)
    return x * r * g


def _rms_bwd(dy, x, g):
    r = lax.rsqrt(jnp.mean(x * x, axis=-1, keepdims=True) + NORM_EPS)
    xh = x * r
    dxh = dy * g
    dx = r * (dxh - xh * jnp.mean(dxh * xh, axis=-1, keepdims=True))
    dg = jnp.sum(dy * xh, axis=0, keepdims=True)
    return dx, dg


def _matmul(a, b, form, out_dtypes, name, epilogue=None, extras=(), after=None, tm=1024, tn=1024, tk=4096):
    if form == "nn":
        (m, kd), (_, n) = a.shape, b.shape
        dims = (((1,), (0,)), ((), ()))
    elif form == "nt":
        (m, kd), (n, _) = a.shape, b.shape
        dims = (((1,), (1,)), ((), ()))
    else:
        (kd, m), (_, n) = a.shape, b.shape
        dims = (((0,), (0,)), ((), ()))
    tm, tn, tk = _tile(m, tm, LANES), _tile(n, tn, LANES), _tile(kd, tk, LANES)
    nk = kd // tk
    n_extra = len(extras)
    n_out = len(out_dtypes)

    if form == "nn":
        a_spec = pl.BlockSpec((tm, tk), lambda i, j, k: (i, k))
        b_spec = pl.BlockSpec((tk, tn), lambda i, j, k: (k, j))
    elif form == "nt":
        a_spec = pl.BlockSpec((tm, tk), lambda i, j, k: (i, k))
        b_spec = pl.BlockSpec((tn, tk), lambda i, j, k: (j, k))
    else:
        a_spec = pl.BlockSpec((tk, tm), lambda i, j, k: (k, i))
        b_spec = pl.BlockSpec((tk, tn), lambda i, j, k: (k, j))
    tile_spec = pl.BlockSpec((tm, tn), lambda i, j, k: (i, j))

    order_only = [] if after is None else [after]
    n_skip = n_extra + len(order_only)

    def finish(acc, extra_refs, out_refs):
        outs = (acc,) if epilogue is None else epilogue(acc, *[e[...] for e in extra_refs])
        for o_ref, val in zip(out_refs, outs):
            o_ref[...] = val.astype(o_ref.dtype)

    def body(a_ref, b_ref, *rest):
        extra_refs = rest[:n_extra]
        out_refs = rest[n_skip:n_skip + n_out]
        if nk == 1:
            finish(lax.dot_general(a_ref[...], b_ref[...], dims, preferred_element_type=F32), extra_refs, out_refs)
            return
        acc_ref = rest[-1]
        k = pl.program_id(2)

        @pl.when(k == 0)
        def _():
            acc_ref[...] = jnp.zeros_like(acc_ref)

        acc_ref[...] += lax.dot_general(a_ref[...], b_ref[...], dims, preferred_element_type=F32)

        @pl.when(k == nk - 1)
        def _():
            finish(acc_ref[...], extra_refs, out_refs)

    outs = pl.pallas_call(
        body, name=name, grid=(m // tm, n // tn, nk),
        in_specs=[a_spec, b_spec] + [tile_spec] * n_extra + [ANY] * len(order_only),
        out_specs=[tile_spec] * n_out,
        out_shape=[jax.ShapeDtypeStruct((m, n), dt) for dt in out_dtypes],
        scratch_shapes=[pltpu.VMEM((tm, tn), F32)] if nk > 1 else [],
        compiler_params=_params(),
    )(a, b, *extras, *order_only)
    return outs[0] if n_out == 1 else outs


def _cast_into_layout(w, layout, chip, name, after=None):
    r, c = w.shape
    tr = _tile(r, 256, 16)
    order_only = [] if after is None else [after]

    def body(chip_ref, w_ref, *rest):
        rest[-1][...] = w_ref[...].astype(BF16)

    return pl.pallas_call(
        body, name=name,
        grid_spec=pltpu.PrefetchScalarGridSpec(
            num_scalar_prefetch=1, grid=(r // tr,),
            in_specs=[pl.BlockSpec((tr, c), lambda i, chip_ref: (i, 0))] + [ANY] * len(order_only),
            out_specs=layout.block_spec(tr, lambda i, chip_ref: (chip_ref[0], i))),
        out_shape=jax.ShapeDtypeStruct(layout.full_shape(), BF16), compiler_params=_params(),
    )(chip, w, *order_only)


def _in_segments(gw, heads, sw):
    main = 4 * gw
    return [(0, main, 0), (main + 2 * heads, 3 * sw, main), (main, 2 * heads, main + 3 * sw)]


def _pieces(seg_start, width, dst_start, ics):
    out = []
    g = seg_start
    while g < seg_start + width:
        j, cj = divmod(g, ics)
        wdt = min(ics - cj, seg_start + width - g)
        out.append((j, cj, dst_start + (g - seg_start), wdt))
        g += wdt
    return out


def _repack_w_in(w_sh, gw, heads, sw, after):
    ns, d, ics = w_sh.shape
    main = 4 * gw + 3 * sw
    tr = _tile(d, 128, 16)
    pieces = [p for seg in _in_segments(gw, heads, sw) for p in _pieces(*seg, ics)]

    def body(w_ref, after_ref, m_ref, ab_ref):
        ab_ref[...] = jnp.zeros_like(ab_ref)
        for j, cj, cd, wdt in pieces:
            if cd >= main:
                ab_ref[:, cd - main:cd - main + wdt] = w_ref[j, :, cj:cj + wdt]
            else:
                m_ref[:, cd:cd + wdt] = w_ref[j, :, cj:cj + wdt]

    return pl.pallas_call(
        body, name="repack_w_in", grid=(d // tr,),
        in_specs=[pl.BlockSpec((ns, tr, ics), lambda i: (0, i, 0)), ANY],
        out_specs=[pl.BlockSpec((tr, main), lambda i: (i, 0)), pl.BlockSpec((tr, LANES), lambda i: (i, 0))],
        out_shape=[jax.ShapeDtypeStruct((d, main), BF16), jax.ShapeDtypeStruct((d, LANES), BF16)],
        compiler_params=_params(),
    )(w_sh, after)


def _unpack_dw_in(dw_main, dw_ab, gw, heads, sw, ics):
    d = dw_main.shape[0]
    tr = _tile(d, 128, 16)
    main = 4 * gw + 3 * sw
    pieces = [p for seg in _in_segments(gw, heads, sw) for p in _pieces(*seg, ics)]

    def body(m_ref, ab_ref, o_ref):
        for j, cj, cd, wdt in pieces:
            if cd >= main:
                o_ref[j, :, cj:cj + wdt] = ab_ref[:, cd - main:cd - main + wdt].astype(BF16)
            else:
                o_ref[j, :, cj:cj + wdt] = m_ref[:, cd:cd + wdt].astype(BF16)

    return pl.pallas_call(
        body, name="unpack_dw_in", grid=(d // tr,),
        in_specs=[pl.BlockSpec((tr, main), lambda i: (i, 0)), pl.BlockSpec((tr, LANES), lambda i: (i, 0))],
        out_specs=pl.BlockSpec((N_CHIPS, tr, ics), lambda i: (0, i, 0)),
        out_shape=jax.ShapeDtypeStruct((N_CHIPS, d, ics), BF16), compiler_params=_params(),
    )(dw_main, dw_ab)


def _adamw(w, g, m, v, name):
    r, c = w.shape
    tr = _tile(r, 128, 8)
    c1 = 1.0 - ADAM_B1 ** ADAM_STEP
    c2 = 1.0 - ADAM_B2 ** ADAM_STEP

    def body(w_ref, g_ref, m_ref, v_ref, d_ref, nm_ref, nv_ref):
        gg = g_ref[...]
        nm = ADAM_B1 * m_ref[...] + (1.0 - ADAM_B1) * gg
        nv = ADAM_B2 * v_ref[...] + (1.0 - ADAM_B2) * jnp.square(gg)
        m_hat = nm / c1
        v_hat = nv / c2
        d_ref[...] = -ADAM_LR * (m_hat / (jnp.sqrt(v_hat) + ADAM_EPS) + ADAM_WD * w_ref[...])
        nm_ref[...] = nm
        nv_ref[...] = nv

    spec = pl.BlockSpec((tr, c), lambda i: (i, 0))
    return pl.pallas_call(
        body, name=name, grid=(r // tr,), in_specs=[spec] * 4, out_specs=[spec] * 3,
        out_shape=[jax.ShapeDtypeStruct((r, c), F32)] * 3, compiler_params=_params(),
    )(w, g, m, v)


def _row_spec(tt, d):
    return pl.BlockSpec((tt, d), lambda i: (i, 0))


def _vec_spec(d):
    return pl.BlockSpec((1, d), lambda i: (0, 0))


def _norm_fwd(x, g, after):
    t, d = x.shape
    tt = _tile(t, 256, 16)

    def body(x_ref, g_ref, after_ref, o_ref):
        o_ref[...] = _rms_fwd(x_ref[...], g_ref[...]).astype(BF16)

    return pl.pallas_call(
        body, name="norm_mix_pre", grid=(t // tt,), in_specs=[_row_spec(tt, d), _vec_spec(d), ANY],
        out_specs=_row_spec(tt, d), out_shape=jax.ShapeDtypeStruct((t, d), BF16), compiler_params=_params(),
    )(x, g, after)


def _mid_fwd(x, mix, g_post, g_pre):
    t, d = x.shape
    tt = _tile(t, 128, 16)

    def body(x_ref, mix_ref, gp_ref, gn_ref, h_ref, hn_ref):
        h = x_ref[...] + _rms_fwd(mix_ref[...], gp_ref[...])
        h_ref[...] = h
        hn_ref[...] = _rms_fwd(h, gn_ref[...]).astype(BF16)

    return pl.pallas_call(
        body, name="mid_fwd", grid=(t // tt,),
        in_specs=[_row_spec(tt, d), _row_spec(tt, d), _vec_spec(d), _vec_spec(d)],
        out_specs=[_row_spec(tt, d), _row_spec(tt, d)],
        out_shape=[jax.ShapeDtypeStruct((t, d), F32), jax.ShapeDtypeStruct((t, d), BF16)],
        compiler_params=_params(),
    )(x, mix, g_post, g_pre)


def _head_fwd_bwd(h, ff, tgt, g_post):
    t, d = h.shape
    tt = _tile(t, 128, 16)

    def body(h_ref, ff_ref, t_ref, g_ref, loss_ref, dy_ref, dff_ref, dg_ref):
        i = pl.program_id(0)

        @pl.when(i == 0)
        def _():
            loss_ref[...] = jnp.zeros_like(loss_ref)
            dg_ref[...] = jnp.zeros_like(dg_ref)

        ff = ff_ref[...]
        g = g_ref[...]
        e = h_ref[...] + _rms_fwd(ff, g) - t_ref[...]
        loss_ref[...] += 0.5 * jnp.sum(jnp.mean(e * e, axis=-1, keepdims=True))
        dy = e * (1.0 / d)
        dy_ref[...] = dy
        dff, dg = _rms_bwd(dy, ff, g)
        dff_ref[...] = dff.astype(BF16)
        dg_ref[...] += dg

    return pl.pallas_call(
        body, name="loss_head", grid=(t // tt,),
        in_specs=[_row_spec(tt, d)] * 3 + [_vec_spec(d)],
        out_specs=[pl.BlockSpec((8, LANES), lambda i: (0, 0)), _row_spec(tt, d), _row_spec(tt, d), _vec_spec(d)],
        out_shape=[jax.ShapeDtypeStruct((8, LANES), F32), jax.ShapeDtypeStruct((t, d), F32),
                   jax.ShapeDtypeStruct((t, d), BF16), jax.ShapeDtypeStruct((1, d), F32)],
        compiler_params=_params(),
    )(h, ff, tgt, g_post)


def _mid_bwd(d_hn, h, g_pre, dy, mix, g_post, after):
    t, d = h.shape
    tt = _tile(t, 128, 16)

    def body(dhn_ref, h_ref, gn_ref, dy_ref, mix_ref, gp_ref, after_ref, dh_ref, dmix_ref, dgn_ref, dgp_ref):
        i = pl.program_id(0)

        @pl.when(i == 0)
        def _():
            dgn_ref[...] = jnp.zeros_like(dgn_ref)
            dgp_ref[...] = jnp.zeros_like(dgp_ref)

        dx, dgn = _rms_bwd(dhn_ref[...], h_ref[...], gn_ref[...])
        dh = dy_ref[...] + dx
        dh_ref[...] = dh
        dmix, dgp = _rms_bwd(dh, mix_ref[...], gp_ref[...])
        dmix_ref[...] = dmix.astype(BF16)
        dgn_ref[...] += dgn
        dgp_ref[...] += dgp

    return pl.pallas_call(
        body, name="mid_bwd", grid=(t // tt,),
        in_specs=[_row_spec(tt, d), _row_spec(tt, d), _vec_spec(d), _row_spec(tt, d), _row_spec(tt, d), _vec_spec(d),
                  ANY],
        out_specs=[_row_spec(tt, d), _row_spec(tt, d), _vec_spec(d), _vec_spec(d)],
        out_shape=[jax.ShapeDtypeStruct((t, d), F32), jax.ShapeDtypeStruct((t, d), BF16),
                   jax.ShapeDtypeStruct((1, d), F32), jax.ShapeDtypeStruct((1, d), F32)],
        compiler_params=_params(),
    )(d_hn, h, g_pre, dy, mix, g_post, after)


def _first_bwd(d_xn, d_xn_ab, x, g, dh):
    t, d = x.shape
    tt = _tile(t, 128, 16)

    def body(a_ref, b_ref, x_ref, g_ref, dh_ref, dx_ref, dg_ref):
        i = pl.program_id(0)

        @pl.when(i == 0)
        def _():
            dg_ref[...] = jnp.zeros_like(dg_ref)

        dx, dg = _rms_bwd(a_ref[...] + b_ref[...], x_ref[...], g_ref[...])
        dx_ref[...] = dh_ref[...] + dx
        dg_ref[...] += dg

    return pl.pallas_call(
        body, name="first_bwd", grid=(t // tt,),
        in_specs=[_row_spec(tt, d), _row_spec(tt, d), _row_spec(tt, d), _vec_spec(d), _row_spec(tt, d)],
        out_specs=[_row_spec(tt, d), _vec_spec(d)],
        out_shape=[jax.ShapeDtypeStruct((t, d), F32), jax.ShapeDtypeStruct((1, d), F32)],
        compiler_params=_params(),
    )(d_xn, d_xn_ab, x, g, dh)


HALO = 8


def _cur(ts, tc, off):
    return pl.BlockSpec((1, ts, tc), lambda ci, b, s: (b, s, off + ci))


def _prev(ts, tc, off):
    return pl.BlockSpec((1, HALO, tc), lambda ci, b, s: (b, jnp.maximum(s * (ts // HALO) - 1, 0), off + ci))


def _next(ts, tc, off, seq):
    last = seq // HALO - 1
    return pl.BlockSpec((1, HALO, tc), lambda ci, b, s: (b, jnp.minimum((s + 1) * (ts // HALO), last), off + ci))


def _conv_w_spec(kw, tc):
    return pl.BlockSpec((kw, tc), lambda ci, b, s: (0, ci))


def _rows_back(u, prev8, k):
    if k == 0:
        return u
    rolled = pltpu.roll(u, k, axis=0)
    row = lax.broadcasted_iota(jnp.int32, (HALO, u.shape[1]), 0)
    first = jnp.where(row < k, pltpu.roll(prev8, k, axis=0), rolled[0:HALO])
    return first if u.shape[0] == HALO else jnp.concatenate([first, rolled[HALO:]], axis=0)


def _rows_ahead(g, next8, m):
    if m == 0:
        return g
    n = g.shape[0]
    rolled = pltpu.roll(g, n - m, axis=0)
    row = lax.broadcasted_iota(jnp.int32, (HALO, g.shape[1]), 0)
    last = jnp.where(row >= HALO - m, pltpu.roll(next8, HALO - m, axis=0), rolled[n - HALO:n])
    return last if n == HALO else jnp.concatenate([rolled[0:n - HALO], last], axis=0)


def _windows(u, prev8, kw):
    return [_rows_back(u, prev8, kw - 1 - j) for j in range(kw)]


def _tap_sum(w, wins):
    acc = w[0:1, :] * wins[0]
    for j in range(1, len(wins)):
        acc = acc + w[j:j + 1, :] * wins[j]
    return acc


def _silu_grad(x):
    s = _sigmoid(x)
    return s * (1.0 + x * (1.0 - s))


def _qkv_conv_fwd(proj3, w, width):
    bsz, seq, _ = proj3.shape
    kw = w.shape[0]
    ts, tc = _tile(seq, 256, 8), _tile(width, 512, LANES)

    def body(u_ref, up_ref, w_ref, o_ref):
        prev8 = jnp.where(pl.program_id(2) == 0, 0.0, up_ref[0])
        pre = _tap_sum(w_ref[...], _windows(u_ref[0], prev8, kw))
        o_ref[0] = pre * _sigmoid(pre)

    return pl.pallas_call(
        body, name="qkv_conv_fwd", grid=(width // tc, bsz, seq // ts),
        in_specs=[_cur(ts, tc, 0), _prev(ts, tc, 0), _conv_w_spec(kw, tc)],
        out_specs=_cur(ts, tc, 0), out_shape=jax.ShapeDtypeStruct((bsz, seq, width), F32),
        compiler_params=_params(),
    )(proj3, proj3, w)


def _qkv_conv_bwd(proj3, dparts, w, width, into3):
    bsz, seq, _ = proj3.shape
    kw = w.shape[0]
    n_parts = len(dparts)
    part_w = width // n_parts
    ts, tc = _tile(seq, 256, 8), _tile(part_w, 512, LANES)
    n_s = seq // ts
    npt = part_w // tc
    last = seq // HALO - 1

    def part_cur(p):
        def index(ci, b, s):
            use = (ci // npt) == p
            return jnp.where(use, b, 0), jnp.where(use, s, 0), jnp.where(use, ci % npt, 0)
        return pl.BlockSpec((1, ts, tc), index)

    def part_next(p):
        def index(ci, b, s):
            use = (ci // npt) == p
            return (jnp.where(use, b, 0), jnp.where(use, jnp.minimum((s + 1) * (ts // HALO), last), 0),
                    jnp.where(use, ci % npt, 0))
        return pl.BlockSpec((1, HALO, tc), index)

    def body(u_ref, up_ref, un_ref, *rest):
        d_refs, dn_refs = rest[:n_parts], rest[n_parts:2 * n_parts]
        w_ref, _, du_ref, dw_ref, dbuf, dnbuf = rest[2 * n_parts:]
        ci, b, s = pl.program_id(0), pl.program_id(1), pl.program_id(2)

        @pl.when((b == 0) & (s == 0))
        def _():
            dw_ref[...] = jnp.zeros_like(dw_ref)

        for p in range(n_parts):
            @pl.when(ci // npt == p)
            def _(p=p):
                dbuf[...] = d_refs[p][0]
                dnbuf[...] = dn_refs[p][0]

        w = w_ref[...]
        u = u_ref[0]
        wins = _windows(u, jnp.where(s == 0, 0.0, up_ref[0]), kw)
        wins_next = _windows(un_ref[0], u[ts - HALO:ts], kw)
        g = dbuf[...] * _silu_grad(_tap_sum(w, wins))
        g_next = jnp.where(s == n_s - 1, 0.0, dnbuf[...] * _silu_grad(_tap_sum(w, wins_next)))
        for j in range(kw):
            dw_ref[j:j + 1, :] += jnp.sum(g * wins[j], axis=0, keepdims=True)
        du_ref[0] = _tap_sum(w, [_rows_ahead(g, g_next, kw - 1 - j) for j in range(kw)]).astype(BF16)

    return pl.pallas_call(
        body, name="qkv_conv_bwd", grid=(width // tc, bsz, n_s),
        in_specs=[_cur(ts, tc, 0), _prev(ts, tc, 0), _next(ts, tc, 0, seq)]
        + [part_cur(p) for p in range(n_parts)] + [part_next(p) for p in range(n_parts)] + [_conv_w_spec(kw, tc), ANY],
        out_specs=[_cur(ts, tc, 0), _conv_w_spec(kw, tc)],
        out_shape=[jax.ShapeDtypeStruct(into3.shape, BF16), jax.ShapeDtypeStruct((kw, width), F32)],
        input_output_aliases={4 + 2 * n_parts: 0},
        scratch_shapes=[pltpu.VMEM((ts, tc), F32), pltpu.VMEM((HALO, tc), F32)],
        compiler_params=_params(),
    )(proj3, proj3, proj3, *dparts, *dparts, w, into3)


def _sc_fwd(proj3, w, off, sw, into3, into_off):
    bsz, seq, _ = proj3.shape
    kw = w.shape[0]
    ts, tc = _tile(seq, 256, 8), _tile(sw, 512, LANES)
    ob, oc, oh = off // tc, (off + sw) // tc, (off + 2 * sw) // tc

    def body(b_ref, c_ref, cp_ref, h_ref, hp_ref, w_ref, into_ref, o_ref):
        prev8 = jnp.where(pl.program_id(2) == 0, 0.0, cp_ref[0] * hp_ref[0])
        o_ref[0] = (b_ref[0] * _tap_sum(w_ref[...], _windows(c_ref[0] * h_ref[0], prev8, kw))).astype(BF16)

    return pl.pallas_call(
        body, name="sc_fwd", grid=(sw // tc, bsz, seq // ts),
        in_specs=[_cur(ts, tc, ob), _cur(ts, tc, oc), _prev(ts, tc, oc), _cur(ts, tc, oh), _prev(ts, tc, oh),
                  _conv_w_spec(kw, tc), ANY],
        out_specs=_cur(ts, tc, into_off // tc), out_shape=jax.ShapeDtypeStruct(into3.shape, BF16),
        input_output_aliases={6: 0}, compiler_params=_params(),
    )(proj3, proj3, proj3, proj3, proj3, w, into3)


def _sc_bwd(proj3, dmixed3, w, off, sw, d_off):
    bsz, seq, _ = proj3.shape
    kw = w.shape[0]
    ts, tc = _tile(seq, 256, 8), _tile(sw, 512, LANES)
    n_s = seq // ts
    ob, oc, oh, od = off // tc, (off + sw) // tc, (off + 2 * sw) // tc, d_off // tc

    def body(d_ref, dn_ref, b_ref, bn_ref, c_ref, cp_ref, h_ref, hp_ref, w_ref,
             db_ref, dc_ref, dh_ref, dw_ref):
        b, s = pl.program_id(1), pl.program_id(2)

        @pl.when((b == 0) & (s == 0))
        def _():
            dw_ref[...] = jnp.zeros_like(dw_ref)

        w = w_ref[...]
        cc, hh = c_ref[0], h_ref[0]
        wins = _windows(cc * hh, jnp.where(s == 0, 0.0, cp_ref[0] * hp_ref[0]), kw)
        dout = d_ref[0]
        db_ref[0] = (dout * _tap_sum(w, wins)).astype(BF16)
        g = dout * b_ref[0]
        g_next = jnp.where(s == n_s - 1, 0.0, dn_ref[0] * bn_ref[0])
        for j in range(kw):
            dw_ref[j:j + 1, :] += jnp.sum(g * wins[j], axis=0, keepdims=True)
        dp = _tap_sum(w, [_rows_ahead(g, g_next, kw - 1 - j) for j in range(kw)])
        dc_ref[0] = (dp * hh).astype(BF16)
        dh_ref[0] = (dp * cc).astype(BF16)

    out = jax.ShapeDtypeStruct((bsz, seq, sw), BF16)
    return pl.pallas_call(
        body, name="sc_bwd", grid=(sw // tc, bsz, n_s),
        in_specs=[_cur(ts, tc, od), _next(ts, tc, od, seq), _cur(ts, tc, ob), _next(ts, tc, ob, seq),
                  _cur(ts, tc, oc), _prev(ts, tc, oc), _cur(ts, tc, oh), _prev(ts, tc, oh), _conv_w_spec(kw, tc)],
        out_specs=[_cur(ts, tc, 0)] * 3 + [_conv_w_spec(kw, tc)],
        out_shape=[out, out, out, jax.ShapeDtypeStruct((kw, sw), F32)],
        compiler_params=_params(),
    )(dmixed3, dmixed3, proj3, proj3, proj3, proj3, proj3, proj3, w)


def _tri_ones(lower):
    i = lax.broadcasted_iota(jnp.int32, (CHUNK, CHUNK), 0)
    j = lax.broadcasted_iota(jnp.int32, (CHUNK, CHUNK), 1)
    return jnp.where((i >= j) if lower else (j >= i), 1.0, 0.0).astype(F32)


def _gates_fwd(proj_ab, a_log_pad, dt_pad, heads):
    t = proj_ab.shape[0]
    gw = heads * LANES

    def body(ab_ref, al_ref, dt_ref, gc_ref, beta_ref):
        ab = ab_ref[...]
        g = -jnp.exp(al_ref[...]) * _softplus(ab + dt_ref[...])
        gc = jnp.dot(_tri_ones(True), g, precision=EXACT_SUM, preferred_element_type=F32)
        beta = _sigmoid(ab)
        for h in range(heads):
            gc_ref[:, h * LANES:(h + 1) * LANES] = jnp.broadcast_to(gc[:, h:h + 1], (CHUNK, LANES))
            beta_ref[:, h * LANES:(h + 1) * LANES] = jnp.broadcast_to(beta[:, heads + h:heads + h + 1], (CHUNK, LANES))

    return pl.pallas_call(
        body, name="gates_fwd", grid=(t // CHUNK,),
        in_specs=[_row_spec(CHUNK, LANES), _vec_spec(LANES), _vec_spec(LANES)],
        out_specs=[_row_spec(CHUNK, gw), _row_spec(CHUNK, gw)],
        out_shape=[jax.ShapeDtypeStruct((t, gw), F32)] * 2, compiler_params=_params(),
    )(proj_ab, a_log_pad, dt_pad)


def _gates_bwd(proj_ab, a_log_pad, dt_pad, dgc_b, dbeta_b, heads):
    t = proj_ab.shape[0]
    gw = heads * LANES

    def body(ab_ref, al_ref, dt_ref, dgc_ref, dbeta_ref, dab_ref, dal_ref, ddt_ref):
        i = pl.program_id(0)

        @pl.when(i == 0)
        def _():
            dal_ref[...] = jnp.zeros_like(dal_ref)
            ddt_ref[...] = jnp.zeros_like(ddt_ref)

        lane = lax.broadcasted_iota(jnp.int32, (CHUNK, LANES), 1)
        dgc = jnp.zeros((CHUNK, LANES), F32)
        dbeta = jnp.zeros((CHUNK, LANES), F32)
        for h in range(heads):
            dgc = jnp.where(lane == h, dgc_ref[:, h * LANES:(h + 1) * LANES], dgc)
            dbeta = jnp.where(lane == heads + h, dbeta_ref[:, h * LANES:(h + 1) * LANES], dbeta)
        dg = jnp.dot(_tri_ones(False), dgc, precision=EXACT_SUM, preferred_element_type=F32)
        ab = ab_ref[...]
        z = ab + dt_ref[...]
        ea = jnp.exp(al_ref[...])
        da = dg * (-ea) * _sigmoid(z)
        beta = _sigmoid(ab)
        db = dbeta * beta * (1.0 - beta)
        dab_ref[...] = jnp.where(lane < heads, da, jnp.where(lane < 2 * heads, db, 0.0)).astype(BF16)
        da_m = jnp.where(lane < heads, da, 0.0)
        ddt_ref[...] += jnp.sum(da_m, axis=0, keepdims=True)
        dal_ref[...] += jnp.sum(jnp.where(lane < heads, dg * (-ea) * _softplus(z), 0.0), axis=0, keepdims=True)

    return pl.pallas_call(
        body, name="gates_bwd", grid=(t // CHUNK,),
        in_specs=[_row_spec(CHUNK, LANES), _vec_spec(LANES), _vec_spec(LANES), _row_spec(CHUNK, gw), _row_spec(CHUNK, gw)],
        out_specs=[_row_spec(CHUNK, LANES), _vec_spec(LANES), _vec_spec(LANES)],
        out_shape=[jax.ShapeDtypeStruct((t, LANES), BF16), jax.ShapeDtypeStruct((1, LANES), F32),
                   jax.ShapeDtypeStruct((1, LANES), F32)],
        compiler_params=_params(),
    )(proj_ab, a_log_pad, dt_pad, dgc_b, dbeta_b)


def _dot(a, b, dims, hi=False):
    if hi:
        return lax.dot_general(a, b, (dims, ((), ())), precision=HI, preferred_element_type=F32)
    return lax.dot_general(a.astype(BF16), b.astype(BF16), (dims, ((), ())), preferred_element_type=F32)


NN = ((1,), (0,))
NT = ((1,), (1,))
TN = ((0,), (0,))


def _each(f, *lists):
    return [f(*xs) for xs in zip(*lists)]


def _dots(a, b, dims, hi=False):
    return _each(lambda x, y: _dot(x, y, dims, hi=hi), a, b)


def _unit_lower_inverse(ms):
    i = lax.broadcasted_iota(jnp.int32, (CHUNK, CHUNK), 0)
    j = lax.broadcasted_iota(jnp.int32, (CHUNK, CHUNK), 1)
    eye = jnp.where(i == j, 1.0, 0.0).astype(F32)
    ts = [eye - jnp.where(jnp.right_shift(i, 1) == jnp.right_shift(j, 1), m, 0.0) for m in ms]
    shift = 1
    while (1 << shift) < CHUNK:
        same_pair = jnp.right_shift(i, shift + 1) == jnp.right_shift(j, shift + 1)
        other_half = jnp.right_shift(i, shift) != jnp.right_shift(j, shift)
        offs = [jnp.where(same_pair & other_half, m, 0.0) for m in ms]
        corr = _dots(_dots(ts, offs, NN, hi=True), ts, NN, hi=True)
        ts = _each(lambda t, c: t - c, ts, corr)
        shift += 1
    return ts


def _chunk_local(qrs, krs, vs, gcbs, betabs, head_dim):
    i = lax.broadcasted_iota(jnp.int32, (CHUNK, CHUNK), 0)
    j = lax.broadcasted_iota(jnp.int32, (CHUNK, CHUNK), 1)
    scale = head_dim ** -0.5
    rqs = [lax.rsqrt(jnp.sum(q * q, axis=-1, keepdims=True) + L2_EPS) for q in qrs]
    rks = [lax.rsqrt(jnp.sum(k * k, axis=-1, keepdims=True) + L2_EPS) for k in krs]
    qhs = _each(lambda a, r: a * r, qrs, rqs)
    ks = _each(lambda a, r: a * r, krs, rks)
    qs = [a * scale for a in qhs]
    decays = [jnp.exp(jnp.where(i >= j, g[:, 0:CHUNK] - g.T[0:CHUNK, :], -jnp.inf)) for g in gcbs]
    kks = _dots(ks, ks, NT)
    qks = _dots(qs, ks, NT)
    ms = _each(lambda b, kk, d: jnp.where(i > j, b[:, 0:CHUNK] * kk * d, 0.0), betabs, kks, decays)
    tinvs = _unit_lower_inverse(ms)
    egs = [jnp.exp(g) for g in gcbs]
    rhs_ws = _each(lambda k, b, e: k * b * e, ks, betabs, egs)
    us = _dots(tinvs, _each(lambda v, b: v * b, vs, betabs), NN, hi=True)
    ws = _dots(tinvs, rhs_ws, NN, hi=True)
    out = []
    for h in range(len(qrs)):
        g_last = gcbs[h][CHUNK - 1:CHUNK, :]
        e_last = jnp.exp(g_last - gcbs[h])
        out.append(dict(rq=rqs[h], rk=rks[h], qh=qhs[h], q=qs[h], k=ks[h], decay=decays[h],
                        beta_col=betabs[h][:, 0:CHUNK], kk=kks[h], m=ms[h], tinv=tinvs[h], eg=egs[h], rhs_w=rhs_ws[h],
                        u=us[h], w=ws[h], p=qks[h] * decays[h], qd=qs[h] * egs[h], kd=ks[h] * e_last, e_last=e_last,
                        gl=jnp.exp(g_last), scale=scale, strict=i > j, incl=i >= j))
    return out


def _field(dicts, name):
    return [d[name] for d in dicts]


GDN_HEAD_GROUP = 16


def _gdn_specs(n_chunks, heads, reverse):
    hg = min(GDN_HEAD_GROUP, heads)
    assert heads % hg == 0

    def cidx(c):
        return (n_chunks - 1 - c) if reverse else c

    def tok(off):
        return pl.BlockSpec((CHUNK, hg * LANES), lambda b, h, c: (b * n_chunks + cidx(c), off // hg + h))

    state = pl.BlockSpec((None, hg, LANES, LANES), lambda b, h, c: (b * n_chunks + cidx(c), h, 0, 0))
    return hg, tok, state


def _gdn_fwd(qkv_act, gcb, betab, bsz, heads):
    t = qkv_act.shape[0]
    n_chunks = t // bsz // CHUNK
    hg, tok, state = _gdn_specs(n_chunks, heads, False)

    def body(q_ref, k_ref, v_ref, gc_ref, beta_ref, o_ref, s_ref, st):
        @pl.when(pl.program_id(2) == 0)
        def _():
            st[...] = jnp.zeros_like(st)

        sls = [slice(hh * LANES, (hh + 1) * LANES) for hh in range(hg)]
        loc = _chunk_local(*[[r[:, sl] for sl in sls] for r in (q_ref, k_ref, v_ref, gc_ref, beta_ref)], LANES)
        s0 = [st[hh] for hh in range(hg)]
        v_new = _each(lambda u, ws: u - ws, _field(loc, "u"), _dots(_field(loc, "w"), s0, NN))
        o_state = _dots(_field(loc, "qd"), s0, NN)
        o_local = _dots(_field(loc, "p"), v_new, NN)
        s_add = _dots(_field(loc, "kd"), v_new, TN)
        for hh in range(hg):
            o_ref[:, sls[hh]] = o_state[hh] + o_local[hh]
            s_ref[hh] = s0[hh]
            st[hh] = s0[hh] * loc[hh]["gl"] + s_add[hh]

    return pl.pallas_call(
        body, name="gdn_fwd", grid=(bsz, heads // hg, n_chunks),
        in_specs=[tok(0), tok(heads), tok(2 * heads), tok(0), tok(0)],
        out_specs=[tok(0), state],
        out_shape=[jax.ShapeDtypeStruct((t, heads * LANES), F32),
                   jax.ShapeDtypeStruct((bsz * n_chunks, heads, LANES, LANES), F32)],
        scratch_shapes=[pltpu.VMEM((hg, LANES, LANES), F32)], compiler_params=_params(),
    )(qkv_act, qkv_act, qkv_act, gcb, betab)


def _gdn_bwd(qkv_act, gcb, betab, states, d_o, bsz, heads):
    t = qkv_act.shape[0]
    n_chunks = t // bsz // CHUNK
    hg, tok, state = _gdn_specs(n_chunks, heads, True)

    def rowsum(a):
        return jnp.sum(a, axis=-1, keepdims=True)

    def finish_head(sl, L, v, betab, d_qd, d_kd, d_gl, d_p, d_m, d_rhs_u, d_rhs_w, d_q, d_k,
                    dq_ref, dk_ref, dv_ref, dgc_ref, dbeta_ref):
        k, decay = L["k"], L["decay"]
        dv_ref[:, sl] = betab * d_rhs_u
        e = d_m * L["m"] + d_p * L["p"]
        d_beta = rowsum(d_m * L["kk"] * decay) + rowsum(d_rhs_u * v) + rowsum(d_rhs_w * k * L["eg"])
        s_kd = rowsum(d_kd * L["kd"])
        d_gc = (rowsum(e) - rowsum(e.T) + rowsum(d_rhs_w * L["rhs_w"]) + rowsum(d_qd * L["qd"]) - s_kd)
        row = lax.broadcasted_iota(jnp.int32, (CHUNK, 1), 0)
        d_gc = d_gc + jnp.where(row == CHUNK - 1, jnp.sum(s_kd) + d_gl * jnp.sum(L["gl"][:, 0:1]), 0.0)
        dgc_ref[:, sl] = jnp.broadcast_to(d_gc, (CHUNK, LANES))
        dbeta_ref[:, sl] = jnp.broadcast_to(d_beta, (CHUNK, LANES))
        d_qh = d_q * L["scale"]
        dq_ref[:, sl] = L["rq"] * (d_qh - L["qh"] * rowsum(d_qh * L["qh"]))
        dk_ref[:, sl] = L["rk"] * (d_k - k * rowsum(d_k * k))

    def body(q_ref, k_ref, v_ref, gc_ref, beta_ref, s_ref, do_ref, dq_ref, dk_ref, dv_ref, dgc_ref, dbeta_ref, dst):
        @pl.when(pl.program_id(2) == 0)
        def _():
            dst[...] = jnp.zeros_like(dst)

        sls = [slice(hh * LANES, (hh + 1) * LANES) for hh in range(hg)]
        vs = [v_ref[:, sl] for sl in sls]
        betabs = [beta_ref[:, sl] for sl in sls]
        loc = _chunk_local([q_ref[:, sl] for sl in sls], [k_ref[:, sl] for sl in sls], vs,
                           [gc_ref[:, sl] for sl in sls], betabs, LANES)
        q, k, u, w, p, tinv, decay, qd, kd, eg = (_field(loc, n) for n in
                                                  ("q", "k", "u", "w", "p", "tinv", "decay", "qd", "kd", "eg"))
        s0 = [s_ref[hh] for hh in range(hg)]
        d_out = [do_ref[:, sl] for sl in sls]
        ds1 = [dst[hh] for hh in range(hg)]
        v_new = _each(lambda a, b: a - b, u, _dots(w, s0, NN))

        d_vnew = _each(lambda a, b: a + b, _dots(p, d_out, TN), _dots(kd, ds1, NN))
        d_qd = _dots(d_out, s0, NT)
        d_kd = _dots(v_new, ds1, NT)
        d_gl = _each(lambda a, b: jnp.sum(a * b), ds1, s0)
        d_p = _each(lambda L, a: jnp.where(L["incl"], a, 0.0), loc, _dots(d_out, v_new, NT))
        d_w = [-a for a in _dots(d_vnew, s0, NT)]
        ds_out, ds_vn = _dots(qd, d_out, TN), _dots(w, d_vnew, TN)
        for hh in range(hg):
            dst[hh] = ds_out[hh] + ds1[hh] * loc[hh]["gl"] - ds_vn[hh]

        d_rhs_u = _dots(tinv, d_vnew, TN, hi=True)
        d_rhs_w = _dots(tinv, d_w, TN, hi=True)
        d_a = _each(lambda a, b: -(a + b), _dots(d_rhs_u, u, NT, hi=True), _dots(d_rhs_w, w, NT, hi=True))
        d_m = _each(lambda L, a: jnp.where(L["strict"], a, 0.0), loc, d_a)
        g_kk = _each(lambda L, a: a * L["beta_col"] * L["decay"], loc, d_m)
        h_qk = _each(lambda a, d: a * d, d_p, decay)

        d_q = _each(lambda a, e, b: a + e * b, _dots(h_qk, k, NN), eg, d_qd)
        d_k = _each(lambda a, b, c, L, bb, rw, dk: a + b + c + bb * L["eg"] * rw + L["e_last"] * dk,
                    _dots(g_kk, k, NN), _dots(g_kk, k, TN), _dots(h_qk, q, TN), loc, betabs, d_rhs_w, d_kd)
        for hh in range(hg):
            finish_head(sls[hh], loc[hh], vs[hh], betabs[hh], d_qd[hh], d_kd[hh], d_gl[hh], d_p[hh], d_m[hh],
                        d_rhs_u[hh], d_rhs_w[hh], d_q[hh], d_k[hh], dq_ref, dk_ref, dv_ref, dgc_ref, dbeta_ref)

    tok_shape = jax.ShapeDtypeStruct((t, heads * LANES), F32)
    return pl.pallas_call(
        body, name="gdn_bwd", grid=(bsz, heads // hg, n_chunks),
        in_specs=[tok(0), tok(heads), tok(2 * heads), tok(0), tok(0), state, tok(0)],
        out_specs=[tok(0)] * 5,
        out_shape=[tok_shape] * 5,
        scratch_shapes=[pltpu.VMEM((hg, LANES, LANES), F32)], compiler_params=_params(),
    )(qkv_act, qkv_act, qkv_act, gcb, betab, states, d_o)


def _gdn_out_fwd(o, proj, gw_norm, heads, z_off, out_width, after):
    t = o.shape[0]
    ts = _tile(t, 512, 16)
    zb = z_off // LANES

    def body(o_ref, z_ref, w_ref, after_ref, out_ref):
        z = z_ref[...]
        out_ref[...] = (_rms_fwd(o_ref[...], w_ref[...]) * (z * _sigmoid(z))).astype(BF16)

    return pl.pallas_call(
        body, name="gdn_out_fwd", grid=(t // ts, heads),
        in_specs=[pl.BlockSpec((ts, LANES), lambda i, h: (i, h)), pl.BlockSpec((ts, LANES), lambda i, h: (i, zb + h)),
                  pl.BlockSpec((1, LANES), lambda i, h: (0, 0)), ANY],
        out_specs=pl.BlockSpec((ts, LANES), lambda i, h: (i, h)),
        out_shape=jax.ShapeDtypeStruct((t, out_width), BF16), compiler_params=_params(),
    )(o, proj, gw_norm, after)


def _gdn_out_bwd(dmixed, o, proj, gw_norm, heads, z_off, after):
    t = o.shape[0]
    ts = _tile(t, 512, 16)
    zb = z_off // LANES

    def body(d_ref, o_ref, z_ref, w_ref, after_ref, do_ref, dz_ref, dw_ref):
        @pl.when((pl.program_id(0) == 0) & (pl.program_id(1) == 0))
        def _():
            dw_ref[...] = jnp.zeros_like(dw_ref)

        d, oo, z, w = d_ref[...], o_ref[...], z_ref[...], w_ref[...]
        on = _rms_fwd(oo, w)
        dz_ref[...] = (d * on * _silu_grad(z)).astype(BF16)
        d_o, d_w = _rms_bwd(d * (z * _sigmoid(z)), oo, w)
        do_ref[...] = d_o
        dw_ref[...] += d_w

    blk = pl.BlockSpec((ts, LANES), lambda i, h: (i, h))
    vec = pl.BlockSpec((1, LANES), lambda i, h: (0, 0))
    return pl.pallas_call(
        body, name="gdn_out_bwd", grid=(t // ts, heads),
        in_specs=[blk, blk, pl.BlockSpec((ts, LANES), lambda i, h: (i, zb + h)), vec, ANY],
        out_specs=[blk, pl.BlockSpec((ts, LANES), lambda i, h: (i, zb + h)), vec],
        out_shape=[jax.ShapeDtypeStruct((t, heads * LANES), F32), jax.ShapeDtypeStruct((t, proj.shape[1]), BF16),
                   jax.ShapeDtypeStruct((1, LANES), F32)],
        compiler_params=_params(),
    )(dmixed, o, proj, gw_norm, after)


def _place_columns(into3, pieces, off):
    bsz, seq, _ = into3.shape
    n_parts, part_w = len(pieces), pieces[0].shape[-1]
    ts, tc = _tile(seq, 512, 16), _tile(part_w, 1024, LANES)
    npt = part_w // tc

    def part(p):
        def index(ci, b, s):
            use = (ci // npt) == p
            return jnp.where(use, b, 0), jnp.where(use, s, 0), jnp.where(use, ci % npt, 0)
        return pl.BlockSpec((1, ts, tc), index)

    def body(*refs):
        o_ref = refs[-1]
        for p in range(n_parts):
            @pl.when(pl.program_id(0) // npt == p)
            def _(p=p):
                o_ref[...] = refs[p][...]

    return pl.pallas_call(
        body, name="place_columns", grid=(n_parts * npt, bsz, seq // ts),
        in_specs=[part(p) for p in range(n_parts)] + [ANY],
        out_specs=_cur(ts, tc, off // tc), out_shape=jax.ShapeDtypeStruct(into3.shape, into3.dtype),
        input_output_aliases={n_parts: 0}, compiler_params=_params(),
    )(*pieces, into3)


def _place():
    x, y, c = lax.axis_index("x"), lax.axis_index("y"), lax.axis_index("c")
    return x, y, c, [(1 - x, y), (x, 1 - y), (1 - x, 1 - y)]


def _aligned(start, align):
    return start if isinstance(start, int) else pl.multiple_of(start, align)


class _Layout:
    def __init__(self, kind, shard_shape):
        self.kind = kind
        self.r, self.c = shard_shape

    def full_shape(self):
        r, c = self.r, self.c
        return {"major": (N_CHIPS, r, c), "rows": (N_CHIPS * r, c), "cols": (r, N_CHIPS * c)}[self.kind]

    def region(self, ref, j, half=None):
        r, c = self.r, self.c
        r0, nr = (0, r) if half is None else (half * (r // 2), r // 2)
        if self.kind == "major":
            return ref.at[j, pl.ds(_aligned(r0, 16), nr), :]
        if self.kind == "rows":
            return ref.at[pl.ds(_aligned(j * r + r0, 16), nr), :]
        return ref.at[pl.ds(_aligned(r0, 16), nr), pl.ds(_aligned(j * c, LANES), c)]

    def block_spec(self, tr, where):
        r, c = self.r, self.c
        if self.kind == "major":
            return pl.BlockSpec((None, tr, c), lambda *a: (where(*a)[0], where(*a)[1], 0))
        if self.kind == "rows":
            return pl.BlockSpec((tr, c), lambda *a: (where(*a)[0] * (r // tr) + where(*a)[1], 0))
        return pl.BlockSpec((tr, c), lambda *a: (where(*a)[1], where(*a)[0]))


def _remote(src, dst, send_sem, recv_sem, dev):
    return pltpu.make_async_remote_copy(src_ref=src, dst_ref=dst, send_sem=send_sem, recv_sem=recv_sem,
                                        device_id=dev, device_id_type=MESH)


def _all_gather(big, layouts, small):
    nb, ns = len(big), len(small)
    n_remote = 6 * nb + 3 * ns

    def body(*refs):
        ins, outs = refs[:nb + ns], refs[nb + ns:2 * (nb + ns)]
        send_sems, recv_sems, local_sems = refs[2 * (nb + ns):]
        x, y, c, chips = _place()
        j = 2 * x + y
        local = []
        for i in range(ns):
            local.append(pltpu.make_async_copy(ins[nb + i], outs[nb + i].at[j], local_sems.at[i]))
        for cp in local:
            cp.start()
        sends = []
        for i in range(nb):
            for p, (px, py) in enumerate(chips):
                k = 3 * i + p
                mine = layouts[i].region(outs[i], j, c)
                sends.append(_remote(mine, mine, send_sems.at[k], recv_sems.at[k], (px, py, c)))
        for i in range(ns):
            for p, (px, py) in enumerate(chips):
                k = 6 * nb + 3 * i + p
                sends.append(_remote(ins[nb + i], outs[nb + i].at[j], send_sems.at[k], recv_sems.at[k], (px, py, c)))
        for cp in sends:
            cp.start()
        for i in range(nb):
            for p, (px, py) in enumerate(chips):
                k, jp = 3 * i + p, 2 * px + py
                got = layouts[i].region(outs[i], jp, c)
                _remote(got, got, send_sems.at[k], recv_sems.at[k], (px, py, c)).wait_recv()
                fwd = _remote(got, got, send_sems.at[3 * nb + k], recv_sems.at[3 * nb + k], (x, y, 1 - c))
                fwd.start()
                sends.append(fwd)
        for i in range(ns):
            for p, (px, py) in enumerate(chips):
                k, jp = 6 * nb + 3 * i + p, 2 * px + py
                _remote(ins[nb + i], outs[nb + i].at[jp], send_sems.at[k], recv_sems.at[k], (px, py, c)).wait_recv()
        for i in range(nb):
            for p, (px, py) in enumerate(chips):
                k, jp = 3 * nb + 3 * i + p, 2 * px + py
                got = layouts[i].region(outs[i], jp, 1 - c)
                _remote(got, got, send_sems.at[k], recv_sems.at[k], (x, y, 1 - c)).wait_recv()
        for cp in sends:
            cp.wait_send()
        for cp in local:
            cp.wait()

    out_shape = [jax.ShapeDtypeStruct(lay.full_shape(), BF16) for lay in layouts]
    out_shape += [jax.ShapeDtypeStruct((N_CHIPS,) + s.shape, F32) for s in small]
    return pl.pallas_call(
        body, name="all_gather_weights", in_specs=[ANY] * (nb + ns), out_specs=[ANY] * (nb + ns), out_shape=out_shape,
        input_output_aliases={i: i for i in range(nb)},
        scratch_shapes=[pltpu.SemaphoreType.DMA((n_remote,)), pltpu.SemaphoreType.DMA((n_remote,)),
                        pltpu.SemaphoreType.DMA((ns,))],
        compiler_params=_params(),
    )(*big, *small)


HBM = pl.BlockSpec(memory_space=pltpu.HBM)
SEM = pl.BlockSpec(memory_space=pltpu.SEMAPHORE)
SPLIT_COPY = pltpu.CompilerParams(has_side_effects=pltpu.SideEffectType.DATAFLOW_SIDE_EFFECTING)


def _in_hbm(a):
    return pltpu.with_memory_space_constraint(a, pltpu.HBM)


def _split_copy_start(arrays, plan, n_copies, name, after=None):
    na = len(arrays)
    order_only = [] if after is None else [after]

    def body(*refs):
        base = na + len(order_only)
        send_sems, recv_sems = refs[base], refs[base + 1]
        thru, token = refs[base + 2:base + 2 + na], refs[base + 2 + na]
        for k, (src, dst, _, dev) in enumerate(plan(thru, *_place())):
            _remote(src, dst, send_sems.at[k], recv_sems.at[k], dev).start()
        token[...] = jnp.zeros_like(token)

    outs = pl.pallas_call(
        body, name=name, in_specs=[HBM] * na + [ANY] * len(order_only),
        out_specs=[SEM, SEM] + [HBM] * na + [pl.BlockSpec(memory_space=pltpu.VMEM)],
        out_shape=[pltpu.SemaphoreType.DMA((n_copies,)), pltpu.SemaphoreType.DMA((n_copies,))]
        + [pltpu.HBM(a.shape, a.dtype) for a in arrays] + [jax.ShapeDtypeStruct((8, LANES), F32)],
        input_output_aliases={i: 2 + i for i in range(na)}, compiler_params=SPLIT_COPY,
    )(*[_in_hbm(a) for a in arrays], *order_only)
    return (outs[0], outs[1]), list(outs[2:2 + na]), outs[2 + na]


def _split_copy_wait(sems, arrays, plan, name, after):
    na = len(arrays)

    def body(*refs):
        send_sems, recv_sems = refs[na], refs[na + 1]
        thru = refs[na + 3:]
        for k, (src, _, landing, dev) in enumerate(plan(thru, *_place())):
            cp = _remote(src, landing, send_sems.at[k], recv_sems.at[k], dev)
            cp.wait_send()
            cp.wait_recv()

    return list(pl.pallas_call(
        body, name=name, in_specs=[HBM] * na + [SEM, SEM, ANY], out_specs=[HBM] * na,
        out_shape=[pltpu.HBM(a.shape, a.dtype) for a in arrays],
        input_output_aliases={i: i for i in range(na)}, compiler_params=SPLIT_COPY,
    )(*arrays, *sems, after))


def _gather_plan(layouts):
    def plan(bufs, x, y, c, chips):
        copies = []
        for buf, lay in zip(bufs, layouts):
            mine = lay.region(buf, 2 * x + y, c)
            copies += [(mine, mine, lay.region(buf, 2 * px + py, c), (px, py, c)) for px, py in chips]
        return copies
    return plan


def _forward_plan(layouts):
    def plan(bufs, x, y, c, chips):
        copies = []
        for buf, lay in zip(bufs, layouts):
            for px, py in chips:
                got = lay.region(buf, 2 * px + py, c)
                copies.append((got, got, lay.region(buf, 2 * px + py, 1 - c), (x, y, 1 - c)))
        return copies
    return plan


def _halves_plan(layouts):
    def plan(arrays, x, y, c, chips):
        nw = len(layouts)
        copies = []
        for i, lay in enumerate(layouts):
            for j in range(N_CHIPS):
                land = arrays[nw + i].at[j]
                copies.append((lay.region(arrays[i], j, 1 - c), land, land, (x, y, 1 - c)))
        return copies
    return plan


def _partials_plan(nw):
    def plan(arrays, x, y, c, chips):
        copies = []
        for i in range(nw):
            for p, (px, py) in enumerate(chips):
                land = arrays[nw + i].at[p]
                copies.append((arrays[i].at[2 * px + py], land, land, (px, py, c)))
        return copies
    return plan


def _join_plan(nw):
    def plan(arrays, x, y, c, chips):
        return [(arrays[i], arrays[nw + i], arrays[nw + i], (x, y, 1 - c)) for i in range(nw)]
    return plan


def _forward_to_sibling(bufs, layouts, name):
    nb = len(bufs)

    def body(*refs):
        outs = refs[nb:2 * nb]
        send_sems, recv_sems = refs[2 * nb:]
        x, y, c, chips = _place()
        cps = []
        for i in range(nb):
            for p, (px, py) in enumerate(chips):
                got = layouts[i].region(outs[i], 2 * px + py, c)
                cps.append(_remote(got, got, send_sems.at[3 * i + p], recv_sems.at[3 * i + p], (x, y, 1 - c)))
        for cp in cps:
            cp.start()
        for i in range(nb):
            for p, (px, py) in enumerate(chips):
                theirs = layouts[i].region(outs[i], 2 * px + py, 1 - c)
                _remote(theirs, theirs, send_sems.at[3 * i + p], recv_sems.at[3 * i + p], (x, y, 1 - c)).wait_recv()
        for cp in cps:
            cp.wait_send()

    return pl.pallas_call(
        body, name=name, in_specs=[ANY] * nb, out_specs=[ANY] * nb,
        out_shape=[jax.ShapeDtypeStruct(b.shape, b.dtype) for b in bufs],
        input_output_aliases={i: i for i in range(nb)},
        scratch_shapes=[pltpu.SemaphoreType.DMA((3 * nb,)), pltpu.SemaphoreType.DMA((3 * nb,))],
        compiler_params=_params(),
    )(*bufs)


def _halves_to_sibling(grads, layouts, name):
    nw = len(grads)

    def body(*refs):
        ins, gots = refs[:nw], refs[nw:2 * nw]
        send_sems, recv_sems = refs[2 * nw:]
        x, y, c, _ = _place()
        cps = []
        for i in range(nw):
            for j in range(N_CHIPS):
                k = N_CHIPS * i + j
                cps.append(_remote(layouts[i].region(ins[i], j, 1 - c), gots[i].at[j],
                                   send_sems.at[k], recv_sems.at[k], (x, y, 1 - c)))
        for cp in cps:
            cp.start()
        for cp in cps:
            cp.wait()

    half = [jax.ShapeDtypeStruct((N_CHIPS, lay.r // 2, lay.c), BF16) for lay in layouts]
    return pl.pallas_call(
        body, name=name, in_specs=[ANY] * nw, out_specs=[ANY] * nw, out_shape=half,
        scratch_shapes=[pltpu.SemaphoreType.DMA((N_CHIPS * nw,)), pltpu.SemaphoreType.DMA((N_CHIPS * nw,))],
        compiler_params=_params(),
    )(*grads)


def _chip_sum(grad, got, layout, core, name):
    n, hr, c = got.shape
    tr = _tile(hr, 256, 16)
    nb = hr // tr

    def body(core_ref, a_ref, b_ref, o_ref):
        o_ref[...] = (a_ref[...].astype(F32) + b_ref[...].astype(F32)).astype(BF16)

    spec = pl.BlockSpec((None, tr, c), lambda j, i, core_ref: (j, i, 0))
    return pl.pallas_call(
        body, name=name,
        grid_spec=pltpu.PrefetchScalarGridSpec(
            num_scalar_prefetch=1, grid=(n, nb),
            in_specs=[layout.block_spec(tr, lambda j, i, core_ref: (j, core_ref[0] * nb + i)), spec],
            out_specs=spec),
        out_shape=jax.ShapeDtypeStruct((n, hr, c), BF16), compiler_params=_params(),
    )(core, grad, got)


def _halves_start(grads, layouts, name, after=None):
    lands = [lax.empty((N_CHIPS, lay.r // 2, lay.c), BF16) for lay in layouts]
    return _split_copy_start(list(grads) + lands, _halves_plan(layouts), N_CHIPS * len(grads), name, after)


def _partials_start(parts, name, after=None):
    lands = [lax.empty((3,) + p.shape[1:], BF16) for p in parts]
    return _split_copy_start(list(parts) + lands, _partials_plan(len(parts)), 3 * len(parts), name, after)


def _shard_sum(parts, got, chip, name):
    _, r, c = parts.shape
    tr = _tile(r, 256, 16)

    def body(chip_ref, o_ref, g_ref, out_ref):
        acc = o_ref[...].astype(F32)
        for p in range(3):
            acc = acc + g_ref[p].astype(F32)
        out_ref[...] = acc

    return pl.pallas_call(
        body, name=name,
        grid_spec=pltpu.PrefetchScalarGridSpec(
            num_scalar_prefetch=1, grid=(r // tr,),
            in_specs=[pl.BlockSpec((None, tr, c), lambda i, chip_ref: (chip_ref[0], i, 0)),
                      pl.BlockSpec((3, tr, c), lambda i, chip_ref: (0, i, 0))],
            out_specs=pl.BlockSpec((tr, c), lambda i, chip_ref: (i, 0))),
        out_shape=jax.ShapeDtypeStruct((r, c), F32), compiler_params=_params(),
    )(chip, parts, got)


def _join_start(halves, name):
    lands = [lax.empty(h.shape, F32) for h in halves]
    return _split_copy_start(list(halves) + lands, _join_plan(len(halves)), len(halves), name)


def _adamw_rows(w, g_half, m, v, half, filled, after, name):
    r, c = w.shape
    hr = r // 2
    tr = _tile(hr, 128, 8)
    nb = hr // tr
    c1 = 1.0 - ADAM_B1 ** ADAM_STEP
    c2 = 1.0 - ADAM_B2 ** ADAM_STEP
    n_prev = 0 if filled is None else 4

    def body(half_ref, w_ref, gin_ref, m_ref, v_ref, *rest):
        g_ref, d_ref, nm_ref, nv_ref = rest[1 + n_prev:]
        gg = gin_ref[...]
        nm = ADAM_B1 * m_ref[...] + (1.0 - ADAM_B1) * gg
        nv = ADAM_B2 * v_ref[...] + (1.0 - ADAM_B2) * jnp.square(gg)
        m_hat = nm / c1
        v_hat = nv / c2
        g_ref[...] = gg
        d_ref[...] = -ADAM_LR * (m_hat / (jnp.sqrt(v_hat) + ADAM_EPS) + ADAM_WD * w_ref[...])
        nm_ref[...] = nm
        nv_ref[...] = nv

    full = pl.BlockSpec((tr, c), lambda i, half_ref: (half_ref[0] * nb + i, 0))
    part = pl.BlockSpec((tr, c), lambda i, half_ref: (i, 0))
    return pl.pallas_call(
        body, name=name,
        grid_spec=pltpu.PrefetchScalarGridSpec(
            num_scalar_prefetch=1, grid=(nb,), in_specs=[full, part, full, full] + [ANY] * (1 + n_prev),
            out_specs=[full] * 4),
        out_shape=[jax.ShapeDtypeStruct((r, c), F32)] * 4,
        input_output_aliases={6 + k: k for k in range(n_prev)}, compiler_params=_params(),
    )(half, w, g_half, m, v, after, *([] if filled is None else filled))


def _small_all_reduce(arrays):
    n_dev = 8
    n = len(arrays)
    width = max(a.shape[1] for a in arrays)
    starts, pos = [], 0
    for a in arrays:
        starts.append(pos)
        pos += a.shape[0]
    rows = -(-pos // 8) * 8

    def body(*refs):
        ins, outs = refs[:n], refs[n:2 * n]
        mine, gath, send_sems, recv_sems = refs[2 * n:]
        x, y, c, _ = _place()
        me = 4 * x + 2 * y + c
        mine[...] = jnp.zeros_like(mine)
        for a_ref, r0 in zip(ins, starts):
            mine[r0:r0 + a_ref.shape[0], 0:a_ref.shape[1]] = a_ref[...]
        gath[me] = mine[...]
        cps = []
        for k in range(1, n_dev):
            px, py, pc = (x + (k >> 2)) % 2, (y + ((k >> 1) & 1)) % 2, (c + (k & 1)) % 2
            cps.append(_remote(mine, gath.at[me], send_sems.at[k - 1], recv_sems.at[k - 1], (px, py, pc)))
        for cp in cps:
            cp.start()
        for k in range(1, n_dev):
            px, py, pc = (x + (k >> 2)) % 2, (y + ((k >> 1) & 1)) % 2, (c + (k & 1)) % 2
            _remote(mine, gath.at[4 * px + 2 * py + pc], send_sems.at[k - 1], recv_sems.at[k - 1], (px, py, pc)).wait_recv()
        for cp in cps:
            cp.wait_send()
        acc = gath[0]
        for dev in range(1, n_dev):
            acc = acc + gath[dev]
        mine[...] = acc
        for o_ref, r0 in zip(outs, starts):
            o_ref[...] = mine[r0:r0 + o_ref.shape[0], 0:o_ref.shape[1]]

    vm = pl.BlockSpec(memory_space=pltpu.VMEM)
    return pl.pallas_call(
        body, name="small_all_reduce", in_specs=[vm] * n, out_specs=[vm] * n,
        out_shape=[jax.ShapeDtypeStruct(a.shape, F32) for a in arrays],
        scratch_shapes=[pltpu.VMEM((rows, width), F32), pltpu.VMEM((n_dev, rows, width), F32),
                        pltpu.SemaphoreType.DMA((n_dev - 1,)), pltpu.SemaphoreType.DMA((n_dev - 1,))],
        compiler_params=_params(),
    )(*arrays)


def _pad_lanes(v):
    return jnp.pad(v, ((0, 0), (0, LANES - v.shape[-1])))


def kernel(x, norm_mix_pre, w_in, conv_qkv_w, a_log, dt_bias, gdn_norm_w, conv_sc_w, w_out, norm_mix_post, norm_mlp_pre, w_up, w_down, norm_mlp_post, loss_target, m_norm_mix_pre, m_w_in, m_conv_qkv_w, m_a_log, m_dt_bias, m_gdn_norm_w, m_conv_sc_w, m_w_out, m_norm_mix_post, m_norm_mlp_pre, m_w_up, m_w_down, m_norm_mlp_post, v_norm_mix_pre, v_w_in, v_conv_qkv_w, v_a_log, v_dt_bias, v_gdn_norm_w, v_conv_sc_w, v_w_out, v_norm_mix_post, v_norm_mlp_pre, v_w_up, v_w_down, v_norm_mlp_post):
    bsz, seq, d = x.shape
    t = bsz * seq
    heads, head_dim = a_log.shape[-1], gdn_norm_w.shape[-1]
    assert head_dim == LANES and seq % CHUNK == 0
    gw = heads * head_dim
    sw = conv_sc_w.shape[-1] * N_CHIPS
    ics = w_in.shape[-1]
    main = 4 * gw + 3 * sw
    assert ics * N_CHIPS == main + 2 * heads and 2 * heads <= LANES

    lay_in = _Layout("major", w_in.shape[1:])
    lay_out = _Layout("rows", w_out.shape[1:])
    lay_up = _Layout("cols", w_up.shape[1:])
    lay_down = _Layout("rows", w_down.shape[1:])
    layouts = [lay_in, lay_out, lay_up, lay_down]
    chip = (2 * lax.axis_index("x") + lax.axis_index("y")).astype(jnp.int32).reshape(1)
    core = lax.axis_index("c").astype(jnp.int32).reshape(1)
    x2 = x.reshape(t, d)
    cq_g, cs_g = _all_gather([], [], [conv_qkv_w[0], conv_sc_w[0]])
    plan_in, plan_ou, plan_down = _gather_plan(layouts[:1]), _gather_plan(layouts[1:3]), _gather_plan(layouts[3:])
    in_buf = _cast_into_layout(w_in[0], lay_in, chip, "cast_w_in")
    in_sems, in_bufs, in_token = _split_copy_start([in_buf], plan_in, 3, "gather_in_start", after=cq_g)
    shards = [_cast_into_layout(w[0], lay, chip, f"cast_{n}", after=in_token)
              for w, lay, n in zip((w_out, w_up, w_down), layouts[1:], ("w_out", "w_up", "w_down"))]
    xn = _norm_fwd(x2, norm_mix_pre, shards[-1])
    in_bufs = _split_copy_wait(in_sems, in_bufs, plan_in, "gather_in_wait", xn)
    win_sh, = _forward_to_sibling(in_bufs, layouts[:1], "forward_w_in")
    ou_sems, ou_bufs, ou_token = _split_copy_start(shards[:2], plan_ou, 6, "gather_out_up_start", after=win_sh)
    w_main, w_ab = _repack_w_in(win_sh, gw, heads, sw, ou_token)
    conv_q = cq_g.transpose(1, 0, 2).reshape(conv_qkv_w.shape[1], -1)
    conv_s = cs_g.transpose(1, 0, 2).reshape(conv_sc_w.shape[1], -1)

    tgt2 = loss_target.reshape(t, d)
    proj = _matmul(xn, w_main, "nn", [F32], "proj_main")
    proj_ab = _matmul(xn, w_ab, "nn", [F32], "proj_ab")
    proj3 = proj.reshape(bsz, seq, main)
    qkv_act = _qkv_conv_fwd(proj3, conv_q, 3 * gw).reshape(t, 3 * gw)
    a_log_pad, dt_pad = _pad_lanes(a_log), _pad_lanes(dt_bias)
    gcb, betab = _gates_fwd(proj_ab, a_log_pad, dt_pad, heads)
    o_raw, states = _gdn_fwd(qkv_act, gcb, betab, bsz, heads)
    ou_bufs = _split_copy_wait(ou_sems, ou_bufs, plan_ou, "gather_out_up_wait", o_raw)
    fwd_plan = _forward_plan(layouts[1:3])
    fwd_sems, ou_bufs, fwd_token = _split_copy_start(ou_bufs, fwd_plan, 6, "forward_out_up_start")
    down_sems, down_bufs, down_token = _split_copy_start(shards[2:], plan_down, 3, "gather_down_start", after=fwd_token)
    gdn_out = _gdn_out_fwd(o_raw, proj, gdn_norm_w, heads, 3 * gw, gw + sw, down_token)
    mixed = _sc_fwd(proj3, conv_s, 4 * gw, sw, gdn_out.reshape(bsz, seq, gw + sw), gw).reshape(t, gw + sw)
    wout_f, wup_f = _split_copy_wait(fwd_sems, ou_bufs, fwd_plan, "forward_out_up_wait", mixed)
    mix = _matmul(mixed, wout_f, "nn", [F32], "mix_out")
    h, hn = _mid_fwd(x2, mix, norm_mix_post, norm_mlp_pre)

    def up_epilogue(acc):
        r = jnp.maximum(acc, 0.0)
        return r, r * r

    relu_up, hid = _matmul(hn, wup_f, "nn", [BF16, BF16], "mlp_up", epilogue=up_epilogue)
    down_bufs = _split_copy_wait(down_sems, down_bufs, plan_down, "gather_down_wait", hid)
    (wdown_f,) = _forward_to_sibling(down_bufs, layouts[3:], "forward_w_down")
    ff = _matmul(hid, wdown_f, "nn", [F32], "mlp_down")
    loss_blk, dy, dff, dg_mlp_post = _head_fwd_bwd(h, ff, tgt2, norm_mlp_post)

    def dup_epilogue(acc, r):
        return (acc * (2.0 * r.astype(F32)),)

    d_up = _matmul(dff, wdown_f, "nt", [BF16], "d_hid", epilogue=dup_epilogue, extras=(relu_up,))
    dw_down = _matmul(hid, dff, "tn", [BF16], "dw_down")
    plan_h_down, plan_h_up, plan_h_in = _halves_plan([lay_down]), _halves_plan([lay_up]), _halves_plan([lay_in])
    hd_sems, hd_arrays, hd_token = _halves_start([dw_down], [lay_down], "down_halves_start")
    d_hn = _matmul(d_up, wup_f, "nt", [F32], "d_hn", after=hd_token)
    dw_up = _matmul(hn, d_up, "tn", [BF16], "dw_up")
    dw_down, down_got = _split_copy_wait(hd_sems, hd_arrays, plan_h_down, "down_halves_wait", dw_up)
    hu_sems, hu_arrays, hu_token = _halves_start([dw_up], [lay_up], "up_halves_start", after=down_got)
    down_part = _chip_sum(dw_down, down_got, lay_down, core, "chip_sum_w_down")
    pd_sems, pd_arrays, pd_token = _partials_start([down_part], "down_partials_start", after=hu_token)
    dh, dmix, dg_mlp_pre, dg_mix_post = _mid_bwd(d_hn, h, norm_mlp_pre, dy, mix, norm_mix_post, pd_token)
    dmixed = _matmul(dmix, wout_f, "nt", [F32], "d_mixed")
    dw_out = _matmul(mixed, dmix, "tn", [BF16], "dw_out")
    dw_up, up_got = _split_copy_wait(hu_sems, hu_arrays, plan_h_up, "up_halves_wait", dw_out)
    up_part = _chip_sum(dw_up, up_got, lay_up, core, "chip_sum_w_up")
    out_got, = _halves_to_sibling([dw_out], [lay_out], "out_grad_halves_to_sibling")
    out_part = _chip_sum(dw_out, out_got, lay_out, core, "chip_sum_w_out")
    puo_sems, puo_arrays, puo_token = _partials_start([up_part, out_part], "up_out_partials_start", after=dmixed)
    dmixed3 = dmixed.reshape(bsz, seq, d)
    d_b, d_c, d_hsc, dw_conv_s = _sc_bwd(proj3, dmixed3, conv_s, 4 * gw, sw, gw)
    d_o, d_z, dg_gdn_norm = _gdn_out_bwd(dmixed, o_raw, proj, gdn_norm_w, heads, 3 * gw, puo_token)
    dq, dk, dv, dgc_b, dbeta_b = _gdn_bwd(qkv_act, gcb, betab, states, d_o, bsz, heads)
    d_ab, d_alog, d_dt = _gates_bwd(proj_ab, a_log_pad, dt_pad, dgc_b, dbeta_b, heads)
    d_proj3, dw_conv_q = _qkv_conv_bwd(proj3, [a.reshape(bsz, seq, gw) for a in (dq, dk, dv)], conv_q, 3 * gw,
                                       d_z.reshape(bsz, seq, main))
    d_proj = _place_columns(d_proj3, [d_b, d_c, d_hsc], 4 * gw).reshape(t, main)
    dw_main = _matmul(xn, d_proj, "tn", [BF16], "dw_in_main")
    dw_ab = _matmul(xn, d_ab, "tn", [BF16], "dw_in_ab")
    dw_in = _unpack_dw_in(dw_main, dw_ab, gw, heads, sw, ics)
    hi_sems, hi_arrays, hi_token = _halves_start([dw_in], [lay_in], "in_halves_start")
    d_xn = _matmul(d_proj, w_main, "nt", [F32], "d_xn_main", after=hi_token)
    dw_in, in_got = _split_copy_wait(hi_sems, hi_arrays, plan_h_in, "in_halves_wait", d_xn)
    in_part = _chip_sum(dw_in, in_got, lay_in, core, "chip_sum_w_in")
    pi_sems, pi_arrays, pi_token = _partials_start([in_part], "in_partials_start")
    d_xn_ab = _matmul(d_ab, w_ab, "nt", [F32], "d_xn_ab", after=pi_token)
    grad_x, dg_mix_pre = _first_bwd(d_xn, d_xn_ab, x2, norm_mix_pre, dh)

    other_core = 1 - core

    def finish(parts, recvs, weights3, names, tag):
        nw = len(parts)
        halves = [_shard_sum(p, r, chip, f"shard_sum_{n}") for p, r, n in zip(parts, recvs, names)]
        sems, arrays, token = _join_start(halves, f"{tag}_join_start")
        own = [_adamw_rows(wt[0], h, m[0], v[0], core, None, token, f"adamw_own_{n}")
               for (wt, m, v), h, n in zip(weights3, arrays[:nw], names)]
        theirs = _split_copy_wait(sems, arrays, _join_plan(nw), f"{tag}_join_wait", own[-1][1])[nw:]
        return [_adamw_rows(wt[0], h, m[0], v[0], other_core, o, h, f"adamw_sibling_{n}")
                for (wt, m, v), h, o, n in zip(weights3, theirs, own, names)]

    down_part, down_recv = _split_copy_wait(pd_sems, pd_arrays, _partials_plan(1), "down_partials_wait", grad_x)
    up_part, out_part, up_recv, out_recv = _split_copy_wait(puo_sems, puo_arrays, _partials_plan(2),
                                                            "up_out_partials_wait", down_recv)
    res_up, res_down = finish([up_part, down_part], [up_recv, down_recv],
                              [(w_up, m_w_up, v_w_up), (w_down, m_w_down, v_w_down)], ("w_up", "w_down"), "mlp")
    in_part, in_recv = _split_copy_wait(pi_sems, pi_arrays, _partials_plan(1), "in_partials_wait", res_down[1])
    res_in, res_out = finish([in_part, out_part], [in_recv, out_recv],
                             [(w_in, m_w_in, v_w_in), (w_out, m_w_out, v_w_out)], ("w_in", "w_out"), "mix")

    (loss_sum, g_mix_pre, g_conv_q_full, g_alog_pad, g_dt_pad, g_gdn_norm, g_conv_s_full, g_mix_post, g_mlp_pre,
     g_mlp_post) = _small_all_reduce([loss_blk, dg_mix_pre, dw_conv_q, d_alog, d_dt, dg_gdn_norm, dw_conv_s, dg_mix_post,
                                      dg_mlp_pre, dg_mlp_post])
    loss, g_alog, g_dt = loss_sum[0, 0], g_alog_pad[:, :heads], g_dt_pad[:, :heads]
    j = 2 * lax.axis_index("x") + lax.axis_index("y")
    cq_w, cs_w = conv_qkv_w.shape[-1], conv_sc_w.shape[-1]
    g_conv_q = lax.dynamic_slice_in_dim(g_conv_q_full, j * cq_w, cq_w, axis=1)
    g_conv_s = lax.dynamic_slice_in_dim(g_conv_s_full, j * cs_w, cs_w, axis=1)

    big = {1: res_in, 7: res_out, 10: res_up, 11: res_down}
    grads = [g_mix_pre, None, g_conv_q, g_alog, g_dt, g_gdn_norm, g_conv_s, None, g_mix_post, g_mlp_pre, None,
             None, g_mlp_post]
    weights = [norm_mix_pre, w_in, conv_qkv_w, a_log, dt_bias, gdn_norm_w, conv_sc_w, w_out, norm_mix_post,
               norm_mlp_pre, w_up, w_down, norm_mlp_post]
    ms = [m_norm_mix_pre, m_w_in, m_conv_qkv_w, m_a_log, m_dt_bias, m_gdn_norm_w, m_conv_sc_w, m_w_out,
          m_norm_mix_post, m_norm_mlp_pre, m_w_up, m_w_down, m_norm_mlp_post]
    vs = [v_norm_mix_pre, v_w_in, v_conv_qkv_w, v_a_log, v_dt_bias, v_gdn_norm_w, v_conv_sc_w, v_w_out,
          v_norm_mix_post, v_norm_mlp_pre, v_w_up, v_w_down, v_norm_mlp_post]
    out_g, out_d, out_m, out_v = [], [], [], []
    for i, (wt, g, m, v) in enumerate(zip(weights, grads, ms, vs)):
        shape2 = wt.shape[-2:] if wt.ndim == 3 else wt.shape
        if i in big:
            g2, dl, nm, nv = big[i]
        else:
            g2 = g.reshape(shape2)
            dl, nm, nv = _adamw(wt.reshape(shape2), g2, m.reshape(shape2), v.reshape(shape2), f"adamw_{i}")
        out_g.append(g2.reshape(wt.shape))
        out_d.append(dl.reshape(wt.shape))
        out_m.append(nm.reshape(wt.shape))
        out_v.append(nv.reshape(wt.shape))

    return (loss.reshape(()), grad_x.reshape(bsz, seq, d), *out_g, *out_d, *out_m, *out_v)
```

```python
import functools

import jax
import jax.numpy as jnp
from jax import lax
from jax.experimental import pallas as pl
from jax.experimental.pallas import tpu as pltpu

CHUNK = 64
NORM_EPS = 1e-6
L2_EPS = 1e-6
N_CHIPS = 4
ADAM_LR = 0.001
ADAM_B1 = 0.9
ADAM_B2 = 0.999
ADAM_EPS = 1e-08
ADAM_WD = 0.01
ADAM_STEP = 10
LANES = 128
VMEM_LIMIT = 56 * 1024 * 1024

F32 = jnp.float32
BF16 = jnp.bfloat16
HI = lax.Precision.HIGH
EXACT_SUM = lax.Precision.HIGHEST
MESH = pl.DeviceIdType.MESH
ANY = pl.BlockSpec(memory_space=pl.ANY)


def _params(n_grid=0):
    return pltpu.CompilerParams(vmem_limit_bytes=VMEM_LIMIT)


def _tile(n, pref, align):
    if n <= pref:
        return n
    t = (pref // align) * align
    while t >= align:
        if n % t == 0:
            return t
        t -= align
    raise ValueError(f"no tile for {n}")


def _sigmoid(x):
    return 1.0 / (1.0 + jnp.exp(-x))


def _softplus(x):
    return jnp.maximum(x, 0.0) + jnp.log(1.0 + jnp.exp(-jnp.abs(x)))


def _rms_fwd(x, g):
    r = lax.rsqrt(jnp.mean(x * x, axis=-1, keepdims=True) + NORM_EPS)
    return x * r * g


def _rms_bwd(dy, x, g):
    r = lax.rsqrt(jnp.mean(x * x, axis=-1, keepdims=True) + NORM_EPS)
    xh = x * r
    dxh = dy * g
    dx = r * (dxh - xh * jnp.mean(dxh * xh, axis=-1, keepdims=True))
    dg = jnp.sum(dy * xh, axis=0, keepdims=True)
    return dx, dg


def _matmul(a, b, form, out_dtypes, name, epilogue=None, extras=(), after=None, tm=1024, tn=1024, tk=4096):
    if form == "nn":
        (m, kd), (_, n) = a.shape, b.shape
        dims = (((1,), (0,)), ((), ()))
    elif form == "nt":
        (m, kd), (n, _) = a.shape, b.shape
        dims = (((1,), (1,)), ((), ()))
    else:
        (kd, m), (_, n) = a.shape, b.shape
        dims = (((0,), (0,)), ((), ()))
    tm, tn, tk = _tile(m, tm, LANES), _tile(n, tn, LANES), _tile(kd, tk, LANES)
    nk = kd // tk
    n_extra = len(extras)
    n_out = len(out_dtypes)

    if form == "nn":
        a_spec = pl.BlockSpec((tm, tk), lambda i, j, k: (i, k))
        b_spec = pl.BlockSpec((tk, tn), lambda i, j, k: (k, j))
    elif form == "nt":
        a_spec = pl.BlockSpec((tm, tk), lambda i, j, k: (i, k))
        b_spec = pl.BlockSpec((tn, tk), lambda i, j, k: (j, k))
    else:
        a_spec = pl.BlockSpec((tk, tm), lambda i, j, k: (k, i))
        b_spec = pl.BlockSpec((tk, tn), lambda i, j, k: (k, j))
    tile_spec = pl.BlockSpec((tm, tn), lambda i, j, k: (i, j))

    order_only = [] if after is None else [after]
    n_skip = n_extra + len(order_only)

    def finish(acc, extra_refs, out_refs):
        outs = (acc,) if epilogue is None else epilogue(acc, *[e[...] for e in extra_refs])
        for o_ref, val in zip(out_refs, outs):
            o_ref[...] = val.astype(o_ref.dtype)

    def body(a_ref, b_ref, *rest):
        extra_refs = rest[:n_extra]
        out_refs = rest[n_skip:n_skip + n_out]
        if nk == 1:
            finish(lax.dot_general(a_ref[...], b_ref[...], dims, preferred_element_type=F32), extra_refs, out_refs)
            return
        acc_ref = rest[-1]
        k = pl.program_id(2)

        @pl.when(k == 0)
        def _():
            acc_ref[...] = jnp.zeros_like(acc_ref)

        acc_ref[...] += lax.dot_general(a_ref[...], b_ref[...], dims, preferred_element_type=F32)

        @pl.when(k == nk - 1)
        def _():
            finish(acc_ref[...], extra_refs, out_refs)

    outs = pl.pallas_call(
        body, name=name, grid=(m // tm, n // tn, nk),
        in_specs=[a_spec, b_spec] + [tile_spec] * n_extra + [ANY] * len(order_only),
        out_specs=[tile_spec] * n_out,
        out_shape=[jax.ShapeDtypeStruct((m, n), dt) for dt in out_dtypes],
        scratch_shapes=[pltpu.VMEM((tm, tn), F32)] if nk > 1 else [],
        compiler_params=_params(),
    )(a, b, *extras, *order_only)
    return outs[0] if n_out == 1 else outs


def _cast_into_layout(w, layout, chip, name, after=None):
    r, c = w.shape
    tr = _tile(r, 256, 16)
    order_only = [] if after is None else [after]

    def body(chip_ref, w_ref, *rest):
        rest[-1][...] = w_ref[...].astype(BF16)

    return pl.pallas_call(
        body, name=name,
        grid_spec=pltpu.PrefetchScalarGridSpec(
            num_scalar_prefetch=1, grid=(r // tr,),
            in_specs=[pl.BlockSpec((tr, c), lambda i, chip_ref: (i, 0))] + [ANY] * len(order_only),
            out_specs=layout.block_spec(tr, lambda i, chip_ref: (chip_ref[0], i))),
        out_shape=jax.ShapeDtypeStruct(layout.full_shape(), BF16), compiler_params=_params(),
    )(chip, w, *order_only)


def _in_segments(gw, heads, sw):
    main = 4 * gw
    return [(0, main, 0), (main + 2 * heads, 3 * sw, main), (main, 2 * heads, main + 3 * sw)]


def _pieces(seg_start, width, dst_start, ics):
    out = []
    g = seg_start
    while g < seg_start + width:
        j, cj = divmod(g, ics)
        wdt = min(ics - cj, seg_start + width - g)
        out.append((j, cj, dst_start + (g - seg_start), wdt))
        g += wdt
    return out


def _repack_w_in(w_sh, gw, heads, sw, after):
    ns, d, ics = w_sh.shape
    main = 4 * gw + 3 * sw
    tr = _tile(d, 128, 16)
    pieces = [p for seg in _in_segments(gw, heads, sw) for p in _pieces(*seg, ics)]

    def body(w_ref, after_ref, m_ref, ab_ref):
        ab_ref[...] = jnp.zeros_like(ab_ref)
        for j, cj, cd, wdt in pieces:
            if cd >= main:
                ab_ref[:, cd - main:cd - main + wdt] = w_ref[j, :, cj:cj + wdt]
            else:
                m_ref[:, cd:cd + wdt] = w_ref[j, :, cj:cj + wdt]

    return pl.pallas_call(
        body, name="repack_w_in", grid=(d // tr,),
        in_specs=[pl.BlockSpec((ns, tr, ics), lambda i: (0, i, 0)), ANY],
        out_specs=[pl.BlockSpec((tr, main), lambda i: (i, 0)), pl.BlockSpec((tr, LANES), lambda i: (i, 0))],
        out_shape=[jax.ShapeDtypeStruct((d, main), BF16), jax.ShapeDtypeStruct((d, LANES), BF16)],
        compiler_params=_params(),
    )(w_sh, after)


def _unpack_dw_in(dw_main, dw_ab, gw, heads, sw, ics):
    d = dw_main.shape[0]
    tr = _tile(d, 128, 16)
    main = 4 * gw + 3 * sw
    pieces = [p for seg in _in_segments(gw, heads, sw) for p in _pieces(*seg, ics)]

    def body(m_ref, ab_ref, o_ref):
        for j, cj, cd, wdt in pieces:
            if cd >= main:
                o_ref[j, :, cj:cj + wdt] = ab_ref[:, cd - main:cd - main + wdt].astype(BF16)
            else:
                o_ref[j, :, cj:cj + wdt] = m_ref[:, cd:cd + wdt].astype(BF16)

    return pl.pallas_call(
        body, name="unpack_dw_in", grid=(d // tr,),
        in_specs=[pl.BlockSpec((tr, main), lambda i: (i, 0)), pl.BlockSpec((tr, LANES), lambda i: (i, 0))],
        out_specs=pl.BlockSpec((N_CHIPS, tr, ics), lambda i: (0, i, 0)),
        out_shape=jax.ShapeDtypeStruct((N_CHIPS, d, ics), BF16), compiler_params=_params(),
    )(dw_main, dw_ab)


def _adamw(w, g, m, v, name):
    r, c = w.shape
    tr = _tile(r, 128, 8)
    c1 = 1.0 - ADAM_B1 ** ADAM_STEP
    c2 = 1.0 - ADAM_B2 ** ADAM_STEP

    def body(w_ref, g_ref, m_ref, v_ref, d_ref, nm_ref, nv_ref):
        gg = g_ref[...]
        nm = ADAM_B1 * m_ref[...] + (1.0 - ADAM_B1) * gg
        nv = ADAM_B2 * v_ref[...] + (1.0 - ADAM_B2) * jnp.square(gg)
        m_hat = nm / c1
        v_hat = nv / c2
        d_ref[...] = -ADAM_LR * (m_hat / (jnp.sqrt(v_hat) + ADAM_EPS) + ADAM_WD * w_ref[...])
        nm_ref[...] = nm
        nv_ref[...] = nv

    spec = pl.BlockSpec((tr, c), lambda i: (i, 0))
    return pl.pallas_call(
        body, name=name, grid=(r // tr,), in_specs=[spec] * 4, out_specs=[spec] * 3,
        out_shape=[jax.ShapeDtypeStruct((r, c), F32)] * 3, compiler_params=_params(),
    )(w, g, m, v)


def _row_spec(tt, d):
    return pl.BlockSpec((tt, d), lambda i: (i, 0))


def _vec_spec(d):
    return pl.BlockSpec((1, d), lambda i: (0, 0))


def _norm_fwd(x, g, after):
    t, d = x.shape
    tt = _tile(t, 256, 16)

    def body(x_ref, g_ref, after_ref, o_ref):
        o_ref[...] = _rms_fwd(x_ref[...], g_ref[...]).astype(BF16)

    return pl.pallas_call(
        body, name="norm_mix_pre", grid=(t // tt,), in_specs=[_row_spec(tt, d), _vec_spec(d), ANY],
        out_specs=_row_spec(tt, d), out_shape=jax.ShapeDtypeStruct((t, d), BF16), compiler_params=_params(),
    )(x, g, after)


def _mid_fwd(x, mix, g_post, g_pre):
    t, d = x.shape
    tt = _tile(t, 128, 16)

    def body(x_ref, mix_ref, gp_ref, gn_ref, h_ref, hn_ref):
        h = x_ref[...] + _rms_fwd(mix_ref[...], gp_ref[...])
        h_ref[...] = h
        hn_ref[...] = _rms_fwd(h, gn_ref[...]).astype(BF16)

    return pl.pallas_call(
        body, name="mid_fwd", grid=(t // tt,),
        in_specs=[_row_spec(tt, d), _row_spec(tt, d), _vec_spec(d), _vec_spec(d)],
        out_specs=[_row_spec(tt, d), _row_spec(tt, d)],
        out_shape=[jax.ShapeDtypeStruct((t, d), F32), jax.ShapeDtypeStruct((t, d), BF16)],
        compiler_params=_params(),
    )(x, mix, g_post, g_pre)


def _head_fwd_bwd(h, ff, tgt, g_post):
    t, d = h.shape
    tt = _tile(t, 128, 16)

    def body(h_ref, ff_ref, t_ref, g_ref, loss_ref, dy_ref, dff_ref, dg_ref):
        i = pl.program_id(0)

        @pl.when(i == 0)
        def _():
            loss_ref[...] = jnp.zeros_like(loss_ref)
            dg_ref[...] = jnp.zeros_like(dg_ref)

        ff = ff_ref[...]
        g = g_ref[...]
        e = h_ref[...] + _rms_fwd(ff, g) - t_ref[...]
        loss_ref[...] += 0.5 * jnp.sum(jnp.mean(e * e, axis=-1, keepdims=True))
        dy = e * (1.0 / d)
        dy_ref[...] = dy
        dff, dg = _rms_bwd(dy, ff, g)
        dff_ref[...] = dff.astype(BF16)
        dg_ref[...] += dg

    return pl.pallas_call(
        body, name="loss_head", grid=(t // tt,),
        in_specs=[_row_spec(tt, d)] * 3 + [_vec_spec(d)],
        out_specs=[pl.BlockSpec((8, LANES), lambda i: (0, 0)), _row_spec(tt, d), _row_spec(tt, d), _vec_spec(d)],
        out_shape=[jax.ShapeDtypeStruct((8, LANES), F32), jax.ShapeDtypeStruct((t, d), F32),
                   jax.ShapeDtypeStruct((t, d), BF16), jax.ShapeDtypeStruct((1, d), F32)],
        compiler_params=_params(),
    )(h, ff, tgt, g_post)


def _mid_bwd(d_hn, h, g_pre, dy, mix, g_post, after):
    t, d = h.shape
    tt = _tile(t, 128, 16)

    def body(dhn_ref, h_ref, gn_ref, dy_ref, mix_ref, gp_ref, after_ref, dh_ref, dmix_ref, dgn_ref, dgp_ref):
        i = pl.program_id(0)

        @pl.when(i == 0)
        def _():
            dgn_ref[...] = jnp.zeros_like(dgn_ref)
            dgp_ref[...] = jnp.zeros_like(dgp_ref)

        dx, dgn = _rms_bwd(dhn_ref[...], h_ref[...], gn_ref[...])
        dh = dy_ref[...] + dx
        dh_ref[...] = dh
        dmix, dgp = _rms_bwd(dh, mix_ref[...], gp_ref[...])
        dmix_ref[...] = dmix.astype(BF16)
        dgn_ref[...] += dgn
        dgp_ref[...] += dgp

    return pl.pallas_call(
        body, name="mid_bwd", grid=(t // tt,),
        in_specs=[_row_spec(tt, d), _row_spec(tt, d), _vec_spec(d), _row_spec(tt, d), _row_spec(tt, d), _vec_spec(d),
                  ANY],
        out_specs=[_row_spec(tt, d), _row_spec(tt, d), _vec_spec(d), _vec_spec(d)],
        out_shape=[jax.ShapeDtypeStruct((t, d), F32), jax.ShapeDtypeStruct((t, d), BF16),
                   jax.ShapeDtypeStruct((1, d), F32), jax.ShapeDtypeStruct((1, d), F32)],
        compiler_params=_params(),
    )(d_hn, h, g_pre, dy, mix, g_post, after)


def _first_bwd(d_xn, d_xn_ab, x, g, dh):
    t, d = x.shape
    tt = _tile(t, 128, 16)

    def body(a_ref, b_ref, x_ref, g_ref, dh_ref, dx_ref, dg_ref):
        i = pl.program_id(0)

        @pl.when(i == 0)
        def _():
            dg_ref[...] = jnp.zeros_like(dg_ref)

        dx, dg = _rms_bwd(a_ref[...] + b_ref[...], x_ref[...], g_ref[...])
        dx_ref[...] = dh_ref[...] + dx
        dg_ref[...] += dg

    return pl.pallas_call(
        body, name="first_bwd", grid=(t // tt,),
        in_specs=[_row_spec(tt, d), _row_spec(tt, d), _row_spec(tt, d), _vec_spec(d), _row_spec(tt, d)],
        out_specs=[_row_spec(tt, d), _vec_spec(d)],
        out_shape=[jax.ShapeDtypeStruct((t, d), F32), jax.ShapeDtypeStruct((1, d), F32)],
        compiler_params=_params(),
    )(d_xn, d_xn_ab, x, g, dh)


HALO = 8


def _cur(ts, tc, off):
    return pl.BlockSpec((1, ts, tc), lambda ci, b, s: (b, s, off + ci))


def _prev(ts, tc, off):
    return pl.BlockSpec((1, HALO, tc), lambda ci, b, s: (b, jnp.maximum(s * (ts // HALO) - 1, 0), off + ci))


def _next(ts, tc, off, seq):
    last = seq // HALO - 1
    return pl.BlockSpec((1, HALO, tc), lambda ci, b, s: (b, jnp.minimum((s + 1) * (ts // HALO), last), off + ci))


def _conv_w_spec(kw, tc):
    return pl.BlockSpec((kw, tc), lambda ci, b, s: (0, ci))


def _rows_back(u, prev8, k):
    if k == 0:
        return u
    rolled = pltpu.roll(u, k, axis=0)
    row = lax.broadcasted_iota(jnp.int32, (HALO, u.shape[1]), 0)
    first = jnp.where(row < k, pltpu.roll(prev8, k, axis=0), rolled[0:HALO])
    return first if u.shape[0] == HALO else jnp.concatenate([first, rolled[HALO:]], axis=0)


def _rows_ahead(g, next8, m):
    if m == 0:
        return g
    n = g.shape[0]
    rolled = pltpu.roll(g, n - m, axis=0)
    row = lax.broadcasted_iota(jnp.int32, (HALO, g.shape[1]), 0)
    last = jnp.where(row >= HALO - m, pltpu.roll(next8, HALO - m, axis=0), rolled[n - HALO:n])
    return last if n == HALO else jnp.concatenate([rolled[0:n - HALO], last], axis=0)


def _windows(u, prev8, kw):
    return [_rows_back(u, prev8, kw - 1 - j) for j in range(kw)]


def _tap_sum(w, wins):
    acc = w[0:1, :] * wins[0]
    for j in range(1, len(wins)):
        acc = acc + w[j:j + 1, :] * wins[j]
    return acc


def _silu_grad(x):
    s = _sigmoid(x)
    return s * (1.0 + x * (1.0 - s))


def _qkv_conv_fwd(proj3, w, width):
    bsz, seq, _ = proj3.shape
    kw = w.shape[0]
    ts, tc = _tile(seq, 256, 8), _tile(width, 512, LANES)

    def body(u_ref, up_ref, w_ref, o_ref):
        prev8 = jnp.where(pl.program_id(2) == 0, 0.0, up_ref[0])
        pre = _tap_sum(w_ref[...], _windows(u_ref[0], prev8, kw))
        o_ref[0] = pre * _sigmoid(pre)

    return pl.pallas_call(
        body, name="qkv_conv_fwd", grid=(width // tc, bsz, seq // ts),
        in_specs=[_cur(ts, tc, 0), _prev(ts, tc, 0), _conv_w_spec(kw, tc)],
        out_specs=_cur(ts, tc, 0), out_shape=jax.ShapeDtypeStruct((bsz, seq, width), F32),
        compiler_params=_params(),
    )(proj3, proj3, w)


def _qkv_conv_bwd(proj3, dparts, w, width, into3):
    bsz, seq, _ = proj3.shape
    kw = w.shape[0]
    n_parts = len(dparts)
    part_w = width // n_parts
    ts, tc = _tile(seq, 256, 8), _tile(part_w, 512, LANES)
    n_s = seq // ts
    npt = part_w // tc
    last = seq // HALO - 1

    def part_cur(p):
        def index(ci, b, s):
            use = (ci // npt) == p
            return jnp.where(use, b, 0), jnp.where(use, s, 0), jnp.where(use, ci % npt, 0)
        return pl.BlockSpec((1, ts, tc), index)

    def part_next(p):
        def index(ci, b, s):
            use = (ci // npt) == p
            return (jnp.where(use, b, 0), jnp.where(use, jnp.minimum((s + 1) * (ts // HALO), last), 0),
                    jnp.where(use, ci % npt, 0))
        return pl.BlockSpec((1, HALO, tc), index)

    def body(u_ref, up_ref, un_ref, *rest):
        d_refs, dn_refs = rest[:n_parts], rest[n_parts:2 * n_parts]
        w_ref, _, du_ref, dw_ref, dbuf, dnbuf = rest[2 * n_parts:]
        ci, b, s = pl.program_id(0), pl.program_id(1), pl.program_id(2)

        @pl.when((b == 0) & (s == 0))
        def _():
            dw_ref[...] = jnp.zeros_like(dw_ref)

        for p in range(n_parts):
            @pl.when(ci // npt == p)
            def _(p=p):
                dbuf[...] = d_refs[p][0]
                dnbuf[...] = dn_refs[p][0]

        w = w_ref[...]
        u = u_ref[0]
        wins = _windows(u, jnp.where(s == 0, 0.0, up_ref[0]), kw)
        wins_next = _windows(un_ref[0], u[ts - HALO:ts], kw)
        g = dbuf[...] * _silu_grad(_tap_sum(w, wins))
        g_next = jnp.where(s == n_s - 1, 0.0, dnbuf[...] * _silu_grad(_tap_sum(w, wins_next)))
        for j in range(kw):
            dw_ref[j:j + 1, :] += jnp.sum(g * wins[j], axis=0, keepdims=True)
        du_ref[0] = _tap_sum(w, [_rows_ahead(g, g_next, kw - 1 - j) for j in range(kw)]).astype(BF16)

    return pl.pallas_call(
        body, name="qkv_conv_bwd", grid=(width // tc, bsz, n_s),
        in_specs=[_cur(ts, tc, 0), _prev(ts, tc, 0), _next(ts, tc, 0, seq)]
        + [part_cur(p) for p in range(n_parts)] + [part_next(p) for p in range(n_parts)] + [_conv_w_spec(kw, tc), ANY],
        out_specs=[_cur(ts, tc, 0), _conv_w_spec(kw, tc)],
        out_shape=[jax.ShapeDtypeStruct(into3.shape, BF16), jax.ShapeDtypeStruct((kw, width), F32)],
        input_output_aliases={4 + 2 * n_parts: 0},
        scratch_shapes=[pltpu.VMEM((ts, tc), F32), pltpu.VMEM((HALO, tc), F32)],
        compiler_params=_params(),
    )(proj3, proj3, proj3, *dparts, *dparts, w, into3)


def _sc_fwd(proj3, w, off, sw, into3, into_off):
    bsz, seq, _ = proj3.shape
    kw = w.shape[0]
    ts, tc = _tile(seq, 256, 8), _tile(sw, 512, LANES)
    ob, oc, oh = off // tc, (off + sw) // tc, (off + 2 * sw) // tc

    def body(b_ref, c_ref, cp_ref, h_ref, hp_ref, w_ref, into_ref, o_ref):
        prev8 = jnp.where(pl.program_id(2) == 0, 0.0, cp_ref[0] * hp_ref[0])
        o_ref[0] = (b_ref[0] * _tap_sum(w_ref[...], _windows(c_ref[0] * h_ref[0], prev8, kw))).astype(BF16)

    return pl.pallas_call(
        body, name="sc_fwd", grid=(sw // tc, bsz, seq // ts),
        in_specs=[_cur(ts, tc, ob), _cur(ts, tc, oc), _prev(ts, tc, oc), _cur(ts, tc, oh), _prev(ts, tc, oh),
                  _conv_w_spec(kw, tc), ANY],
        out_specs=_cur(ts, tc, into_off // tc), out_shape=jax.ShapeDtypeStruct(into3.shape, BF16),
        input_output_aliases={6: 0}, compiler_params=_params(),
    )(proj3, proj3, proj3, proj3, proj3, w, into3)


def _sc_bwd(proj3, dmixed3, w, off, sw, d_off):
    bsz, seq, _ = proj3.shape
    kw = w.shape[0]
    ts, tc = _tile(seq, 256, 8), _tile(sw, 512, LANES)
    n_s = seq // ts
    ob, oc, oh, od = off // tc, (off + sw) // tc, (off + 2 * sw) // tc, d_off // tc

    def body(d_ref, dn_ref, b_ref, bn_ref, c_ref, cp_ref, h_ref, hp_ref, w_ref,
             db_ref, dc_ref, dh_ref, dw_ref):
        b, s = pl.program_id(1), pl.program_id(2)

        @pl.when((b == 0) & (s == 0))
        def _():
            dw_ref[...] = jnp.zeros_like(dw_ref)

        w = w_ref[...]
        cc, hh = c_ref[0], h_ref[0]
        wins = _windows(cc * hh, jnp.where(s == 0, 0.0, cp_ref[0] * hp_ref[0]), kw)
        dout = d_ref[0]
        db_ref[0] = (dout * _tap_sum(w, wins)).astype(BF16)
        g = dout * b_ref[0]
        g_next = jnp.where(s == n_s - 1, 0.0, dn_ref[0] * bn_ref[0])
        for j in range(kw):
            dw_ref[j:j + 1, :] += jnp.sum(g * wins[j], axis=0, keepdims=True)
        dp = _tap_sum(w, [_rows_ahead(g, g_next, kw - 1 - j) for j in range(kw)])
        dc_ref[0] = (dp * hh).astype(BF16)
        dh_ref[0] = (dp * cc).astype(BF16)

    out = jax.ShapeDtypeStruct((bsz, seq, sw), BF16)
    return pl.pallas_call(
        body, name="sc_bwd", grid=(sw // tc, bsz, n_s),
        in_specs=[_cur(ts, tc, od), _next(ts, tc, od, seq), _cur(ts, tc, ob), _next(ts, tc, ob, seq),
                  _cur(ts, tc, oc), _prev(ts, tc, oc), _cur(ts, tc, oh), _prev(ts, tc, oh), _conv_w_spec(kw, tc)],
        out_specs=[_cur(ts, tc, 0)] * 3 + [_conv_w_spec(kw, tc)],
        out_shape=[out, out, out, jax.ShapeDtypeStruct((kw, sw), F32)],
        compiler_params=_params(),
    )(dmixed3, dmixed3, proj3, proj3, proj3, proj3, proj3, proj3, w)


def _tri_ones(lower):
    i = lax.broadcasted_iota(jnp.int32, (CHUNK, CHUNK), 0)
    j = lax.broadcasted_iota(jnp.int32, (CHUNK, CHUNK), 1)
    return jnp.where((i >= j) if lower else (j >= i), 1.0, 0.0).astype(F32)


def _gates_fwd(proj_ab, a_log_pad, dt_pad, heads):
    t = proj_ab.shape[0]
    gw = heads * LANES

    def body(ab_ref, al_ref, dt_ref, gc_ref, beta_ref):
        ab = ab_ref[...]
        g = -jnp.exp(al_ref[...]) * _softplus(ab + dt_ref[...])
        gc = jnp.dot(_tri_ones(True), g, precision=EXACT_SUM, preferred_element_type=F32)
        beta = _sigmoid(ab)
        for h in range(heads):
            gc_ref[:, h * LANES:(h + 1) * LANES] = jnp.broadcast_to(gc[:, h:h + 1], (CHUNK, LANES))
            beta_ref[:, h * LANES:(h + 1) * LANES] = jnp.broadcast_to(beta[:, heads + h:heads + h + 1], (CHUNK, LANES))

    return pl.pallas_call(
        body, name="gates_fwd", grid=(t // CHUNK,),
        in_specs=[_row_spec(CHUNK, LANES), _vec_spec(LANES), _vec_spec(LANES)],
        out_specs=[_row_spec(CHUNK, gw), _row_spec(CHUNK, gw)],
        out_shape=[jax.ShapeDtypeStruct((t, gw), F32)] * 2, compiler_params=_params(),
    )(proj_ab, a_log_pad, dt_pad)


def _gates_bwd(proj_ab, a_log_pad, dt_pad, dgc_b, dbeta_b, heads):
    t = proj_ab.shape[0]
    gw = heads * LANES

    def body(ab_ref, al_ref, dt_ref, dgc_ref, dbeta_ref, dab_ref, dal_ref, ddt_ref):
        i = pl.program_id(0)

        @pl.when(i == 0)
        def _():
            dal_ref[...] = jnp.zeros_like(dal_ref)
            ddt_ref[...] = jnp.zeros_like(ddt_ref)

        lane = lax.broadcasted_iota(jnp.int32, (CHUNK, LANES), 1)
        dgc = jnp.zeros((CHUNK, LANES), F32)
        dbeta = jnp.zeros((CHUNK, LANES), F32)
        for h in range(heads):
            dgc = jnp.where(lane == h, dgc_ref[:, h * LANES:(h + 1) * LANES], dgc)
            dbeta = jnp.where(lane == heads + h, dbeta_ref[:, h * LANES:(h + 1) * LANES], dbeta)
        dg = jnp.dot(_tri_ones(False), dgc, precision=EXACT_SUM, preferred_element_type=F32)
        ab = ab_ref[...]
        z = ab + dt_ref[...]
        ea = jnp.exp(al_ref[...])
        da = dg * (-ea) * _sigmoid(z)
        beta = _sigmoid(ab)
        db = dbeta * beta * (1.0 - beta)
        dab_ref[...] = jnp.where(lane < heads, da, jnp.where(lane < 2 * heads, db, 0.0)).astype(BF16)
        da_m = jnp.where(lane < heads, da, 0.0)
        ddt_ref[...] += jnp.sum(da_m, axis=0, keepdims=True)
        dal_ref[...] += jnp.sum(jnp.where(lane < heads, dg * (-ea) * _softplus(z), 0.0), axis=0, keepdims=True)

    return pl.pallas_call(
        body, name="gates_bwd", grid=(t // CHUNK,),
        in_specs=[_row_spec(CHUNK, LANES), _vec_spec(LANES), _vec_spec(LANES), _row_spec(CHUNK, gw), _row_spec(CHUNK, gw)],
        out_specs=[_row_spec(CHUNK, LANES), _vec_spec(LANES), _vec_spec(LANES)],
        out_shape=[jax.ShapeDtypeStruct((t, LANES), BF16), jax.ShapeDtypeStruct((1, LANES), F32),
                   jax.ShapeDtypeStruct((1, LANES), F32)],
        compiler_params=_params(),
    )(proj_ab, a_log_pad, dt_pad, dgc_b, dbeta_b)


def _dot(a, b, dims, hi=False):
    if hi:
        return lax.dot_general(a, b, (dims, ((), ())), precision=HI, preferred_element_type=F32)
    return lax.dot_general(a.astype(BF16), b.astype(BF16), (dims, ((), ())), preferred_element_type=F32)


NN = ((1,), (0,))
NT = ((1,), (1,))
TN = ((0,), (0,))


def _each(f, *lists):
    return [f(*xs) for xs in zip(*lists)]


def _dots(a, b, dims, hi=False):
    return _each(lambda x, y: _dot(x, y, dims, hi=hi), a, b)


def _unit_lower_inverse(ms):
    i = lax.broadcasted_iota(jnp.int32, (CHUNK, CHUNK), 0)
    j = lax.broadcasted_iota(jnp.int32, (CHUNK, CHUNK), 1)
    eye = jnp.where(i == j, 1.0, 0.0).astype(F32)
    ts = [eye - jnp.where(jnp.right_shift(i, 1) == jnp.right_shift(j, 1), m, 0.0) for m in ms]
    shift = 1
    while (1 << shift) < CHUNK:
        same_pair = jnp.right_shift(i, shift + 1) == jnp.right_shift(j, shift + 1)
        other_half = jnp.right_shift(i, shift) != jnp.right_shift(j, shift)
        offs = [jnp.where(same_pair & other_half, m, 0.0) for m in ms]
        corr = _dots(_dots(ts, offs, NN, hi=True), ts, NN, hi=True)
        ts = _each(lambda t, c: t - c, ts, corr)
        shift += 1
    return ts


def _chunk_local(qrs, krs, vs, gcbs, betabs, head_dim, solved=None):
    i = lax.broadcasted_iota(jnp.int32, (CHUNK, CHUNK), 0)
    j = lax.broadcasted_iota(jnp.int32, (CHUNK, CHUNK), 1)
    scale = head_dim ** -0.5
    rqs = [lax.rsqrt(jnp.sum(q * q, axis=-1, keepdims=True) + L2_EPS) for q in qrs]
    rks = [lax.rsqrt(jnp.sum(k * k, axis=-1, keepdims=True) + L2_EPS) for k in krs]
    qhs = _each(lambda a, r: a * r, qrs, rqs)
    ks = _each(lambda a, r: a * r, krs, rks)
    qs = [a * scale for a in qhs]
    decays = [jnp.exp(jnp.where(i >= j, g[:, 0:CHUNK] - g.T[0:CHUNK, :], -jnp.inf)) for g in gcbs]
    kks = _dots(ks, ks, NT)
    qks = _dots(qs, ks, NT)
    ms = _each(lambda b, kk, d: jnp.where(i > j, b[:, 0:CHUNK] * kk * d, 0.0), betabs, kks, decays)
    egs = [jnp.exp(g) for g in gcbs]
    rhs_ws = _each(lambda k, b, e: k * b * e, ks, betabs, egs)
    if solved is None:
        tinvs = _unit_lower_inverse(ms)
        us = _dots(tinvs, _each(lambda v, b: v * b, vs, betabs), NN, hi=True)
        ws = _dots(tinvs, rhs_ws, NN, hi=True)
    else:
        tinvs, us, ws = solved
    out = []
    for h in range(len(qrs)):
        g_last = gcbs[h][CHUNK - 1:CHUNK, :]
        e_last = jnp.exp(g_last - gcbs[h])
        out.append(dict(rq=rqs[h], rk=rks[h], qh=qhs[h], q=qs[h], k=ks[h], decay=decays[h],
                        beta_col=betabs[h][:, 0:CHUNK], kk=kks[h], m=ms[h], tinv=tinvs[h], eg=egs[h], rhs_w=rhs_ws[h],
                        u=us[h], w=ws[h], p=qks[h] * decays[h], qd=qs[h] * egs[h], kd=ks[h] * e_last, e_last=e_last,
                        gl=jnp.exp(g_last), scale=scale, strict=i > j, incl=i >= j))
    return out


def _field(dicts, name):
    return [d[name] for d in dicts]


GDN_HEAD_GROUP = 16


def _gdn_specs(n_chunks, heads, reverse):
    hg = min(GDN_HEAD_GROUP, heads)
    assert heads % hg == 0

    def cidx(c):
        return (n_chunks - 1 - c) if reverse else c

    def tok(off):
        return pl.BlockSpec((CHUNK, hg * LANES), lambda b, h, c: (b * n_chunks + cidx(c), off // hg + h))

    state = pl.BlockSpec((None, hg, LANES, LANES), lambda b, h, c: (b * n_chunks + cidx(c), h, 0, 0))
    inverse = pl.BlockSpec((None, hg, CHUNK, CHUNK), lambda b, h, c: (b * n_chunks + cidx(c), h, 0, 0))
    return hg, tok, state, inverse


def _gdn_fwd(qkv_act, gcb, betab, bsz, heads):
    t = qkv_act.shape[0]
    n_chunks = t // bsz // CHUNK
    hg, tok, state, inverse = _gdn_specs(n_chunks, heads, False)

    def body(q_ref, k_ref, v_ref, gc_ref, beta_ref, o_ref, s_ref, t_ref, u_ref, w_ref, st):
        @pl.when(pl.program_id(2) == 0)
        def _():
            st[...] = jnp.zeros_like(st)

        sls = [slice(hh * LANES, (hh + 1) * LANES) for hh in range(hg)]
        loc = _chunk_local(*[[r[:, sl] for sl in sls] for r in (q_ref, k_ref, v_ref, gc_ref, beta_ref)], LANES)
        s0 = [st[hh] for hh in range(hg)]
        v_new = _each(lambda u, ws: u - ws, _field(loc, "u"), _dots(_field(loc, "w"), s0, NN))
        o_state = _dots(_field(loc, "qd"), s0, NN)
        o_local = _dots(_field(loc, "p"), v_new, NN)
        s_add = _dots(_field(loc, "kd"), v_new, TN)
        for hh in range(hg):
            o_ref[:, sls[hh]] = o_state[hh] + o_local[hh]
            s_ref[hh] = s0[hh]
            t_ref[hh] = loc[hh]["tinv"]
            u_ref[:, sls[hh]] = loc[hh]["u"]
            w_ref[:, sls[hh]] = loc[hh]["w"]
            st[hh] = s0[hh] * loc[hh]["gl"] + s_add[hh]

    tok_shape = jax.ShapeDtypeStruct((t, heads * LANES), F32)
    return pl.pallas_call(
        body, name="gdn_fwd", grid=(bsz, heads // hg, n_chunks),
        in_specs=[tok(0), tok(heads), tok(2 * heads), tok(0), tok(0)],
        out_specs=[tok(0), state, inverse, tok(0), tok(0)],
        out_shape=[tok_shape, jax.ShapeDtypeStruct((bsz * n_chunks, heads, LANES, LANES), F32),
                   jax.ShapeDtypeStruct((bsz * n_chunks, heads, CHUNK, CHUNK), F32), tok_shape, tok_shape],
        scratch_shapes=[pltpu.VMEM((hg, LANES, LANES), F32)], compiler_params=_params(),
    )(qkv_act, qkv_act, qkv_act, gcb, betab)


def _gdn_bwd(qkv_act, gcb, betab, states, solved, d_o, bsz, heads):
    t = qkv_act.shape[0]
    n_chunks = t // bsz // CHUNK
    hg, tok, state, inverse = _gdn_specs(n_chunks, heads, True)

    def rowsum(a):
        return jnp.sum(a, axis=-1, keepdims=True)

    def finish_head(sl, L, v, betab, d_qd, d_kd, d_gl, d_p, d_m, d_rhs_u, d_rhs_w, d_q, d_k,
                    dq_ref, dk_ref, dv_ref, dgc_ref, dbeta_ref):
        k, decay = L["k"], L["decay"]
        dv_ref[:, sl] = betab * d_rhs_u
        e = d_m * L["m"] + d_p * L["p"]
        d_beta = rowsum(d_m * L["kk"] * decay) + rowsum(d_rhs_u * v) + rowsum(d_rhs_w * k * L["eg"])
        s_kd = rowsum(d_kd * L["kd"])
        d_gc = (rowsum(e) - rowsum(e.T) + rowsum(d_rhs_w * L["rhs_w"]) + rowsum(d_qd * L["qd"]) - s_kd)
        row = lax.broadcasted_iota(jnp.int32, (CHUNK, 1), 0)
        d_gc = d_gc + jnp.where(row == CHUNK - 1, jnp.sum(s_kd) + d_gl * jnp.sum(L["gl"][:, 0:1]), 0.0)
        dgc_ref[:, sl] = jnp.broadcast_to(d_gc, (CHUNK, LANES))
        dbeta_ref[:, sl] = jnp.broadcast_to(d_beta, (CHUNK, LANES))
        d_qh = d_q * L["scale"]
        dq_ref[:, sl] = L["rq"] * (d_qh - L["qh"] * rowsum(d_qh * L["qh"]))
        dk_ref[:, sl] = L["rk"] * (d_k - k * rowsum(d_k * k))

    def body(q_ref, k_ref, v_ref, gc_ref, beta_ref, s_ref, t_ref, u_ref, w_ref, do_ref,
             dq_ref, dk_ref, dv_ref, dgc_ref, dbeta_ref, dst):
        @pl.when(pl.program_id(2) == 0)
        def _():
            dst[...] = jnp.zeros_like(dst)

        sls = [slice(hh * LANES, (hh + 1) * LANES) for hh in range(hg)]
        vs = [v_ref[:, sl] for sl in sls]
        betabs = [beta_ref[:, sl] for sl in sls]
        solved = ([t_ref[hh] for hh in range(hg)], [u_ref[:, sl] for sl in sls], [w_ref[:, sl] for sl in sls])
        loc = _chunk_local([q_ref[:, sl] for sl in sls], [k_ref[:, sl] for sl in sls], vs,
                           [gc_ref[:, sl] for sl in sls], betabs, LANES, solved)
        q, k, u, w, p, tinv, decay, qd, kd, eg = (_field(loc, n) for n in
                                                  ("q", "k", "u", "w", "p", "tinv", "decay", "qd", "kd", "eg"))
        s0 = [s_ref[hh] for hh in range(hg)]
        d_out = [do_ref[:, sl] for sl in sls]
        ds1 = [dst[hh] for hh in range(hg)]
        v_new = _each(lambda a, b: a - b, u, _dots(w, s0, NN))

        d_vnew = _each(lambda a, b: a + b, _dots(p, d_out, TN), _dots(kd, ds1, NN))
        d_qd = _dots(d_out, s0, NT)
        d_kd = _dots(v_new, ds1, NT)
        d_gl = _each(lambda a, b: jnp.sum(a * b), ds1, s0)
        d_p = _each(lambda L, a: jnp.where(L["incl"], a, 0.0), loc, _dots(d_out, v_new, NT))
        d_w = [-a for a in _dots(d_vnew, s0, NT)]
        ds_out, ds_vn = _dots(qd, d_out, TN), _dots(w, d_vnew, TN)
        for hh in range(hg):
            dst[hh] = ds_out[hh] + ds1[hh] * loc[hh]["gl"] - ds_vn[hh]

        d_rhs_u = _dots(tinv, d_vnew, TN, hi=True)
        d_rhs_w = _dots(tinv, d_w, TN, hi=True)
        d_a = _each(lambda a, b: -(a + b), _dots(d_rhs_u, u, NT, hi=True), _dots(d_rhs_w, w, NT, hi=True))
        d_m = _each(lambda L, a: jnp.where(L["strict"], a, 0.0), loc, d_a)
        g_kk = _each(lambda L, a: a * L["beta_col"] * L["decay"], loc, d_m)
        h_qk = _each(lambda a, d: a * d, d_p, decay)

        d_q = _each(lambda a, e, b: a + e * b, _dots(h_qk, k, NN), eg, d_qd)
        d_k = _each(lambda a, b, c, L, bb, rw, dk: a + b + c + bb * L["eg"] * rw + L["e_last"] * dk,
                    _dots(g_kk, k, NN), _dots(g_kk, k, TN), _dots(h_qk, q, TN), loc, betabs, d_rhs_w, d_kd)
        for hh in range(hg):
            finish_head(sls[hh], loc[hh], vs[hh], betabs[hh], d_qd[hh], d_kd[hh], d_gl[hh], d_p[hh], d_m[hh],
                        d_rhs_u[hh], d_rhs_w[hh], d_q[hh], d_k[hh], dq_ref, dk_ref, dv_ref, dgc_ref, dbeta_ref)

    tok_shape = jax.ShapeDtypeStruct((t, heads * LANES), F32)
    return pl.pallas_call(
        body, name="gdn_bwd", grid=(bsz, heads // hg, n_chunks),
        in_specs=[tok(0), tok(heads), tok(2 * heads), tok(0), tok(0), state, inverse, tok(0), tok(0), tok(0)],
        out_specs=[tok(0)] * 5,
        out_shape=[tok_shape] * 5,
        scratch_shapes=[pltpu.VMEM((hg, LANES, LANES), F32)], compiler_params=_params(),
    )(qkv_act, qkv_act, qkv_act, gcb, betab, states, *solved, d_o)


def _gdn_out_fwd(o, proj, gw_norm, heads, z_off, out_width, after):
    t = o.shape[0]
    ts = _tile(t, 512, 16)
    zb = z_off // LANES

    def body(o_ref, z_ref, w_ref, after_ref, out_ref):
        z = z_ref[...]
        out_ref[...] = (_rms_fwd(o_ref[...], w_ref[...]) * (z * _sigmoid(z))).astype(BF16)

    return pl.pallas_call(
        body, name="gdn_out_fwd", grid=(t // ts, heads),
        in_specs=[pl.BlockSpec((ts, LANES), lambda i, h: (i, h)), pl.BlockSpec((ts, LANES), lambda i, h: (i, zb + h)),
                  pl.BlockSpec((1, LANES), lambda i, h: (0, 0)), ANY],
        out_specs=pl.BlockSpec((ts, LANES), lambda i, h: (i, h)),
        out_shape=jax.ShapeDtypeStruct((t, out_width), BF16), compiler_params=_params(),
    )(o, proj, gw_norm, after)


def _gdn_out_bwd(dmixed, o, proj, gw_norm, heads, z_off, after):
    t = o.shape[0]
    ts = _tile(t, 512, 16)
    zb = z_off // LANES

    def body(d_ref, o_ref, z_ref, w_ref, after_ref, do_ref, dz_ref, dw_ref):
        @pl.when((pl.program_id(0) == 0) & (pl.program_id(1) == 0))
        def _():
            dw_ref[...] = jnp.zeros_like(dw_ref)

        d, oo, z, w = d_ref[...], o_ref[...], z_ref[...], w_ref[...]
        on = _rms_fwd(oo, w)
        dz_ref[...] = (d * on * _silu_grad(z)).astype(BF16)
        d_o, d_w = _rms_bwd(d * (z * _sigmoid(z)), oo, w)
        do_ref[...] = d_o
        dw_ref[...] += d_w

    blk = pl.BlockSpec((ts, LANES), lambda i, h: (i, h))
    vec = pl.BlockSpec((1, LANES), lambda i, h: (0, 0))
    return pl.pallas_call(
        body, name="gdn_out_bwd", grid=(t // ts, heads),
        in_specs=[blk, blk, pl.BlockSpec((ts, LANES), lambda i, h: (i, zb + h)), vec, ANY],
        out_specs=[blk, pl.BlockSpec((ts, LANES), lambda i, h: (i, zb + h)), vec],
        out_shape=[jax.ShapeDtypeStruct((t, heads * LANES), F32), jax.ShapeDtypeStruct((t, proj.shape[1]), BF16),
                   jax.ShapeDtypeStruct((1, LANES), F32)],
        compiler_params=_params(),
    )(dmixed, o, proj, gw_norm, after)


def _place_columns(into3, pieces, off):
    bsz, seq, _ = into3.shape
    n_parts, part_w = len(pieces), pieces[0].shape[-1]
    ts, tc = _tile(seq, 512, 16), _tile(part_w, 1024, LANES)
    npt = part_w // tc

    def part(p):
        def index(ci, b, s):
            use = (ci // npt) == p
            return jnp.where(use, b, 0), jnp.where(use, s, 0), jnp.where(use, ci % npt, 0)
        return pl.BlockSpec((1, ts, tc), index)

    def body(*refs):
        o_ref = refs[-1]
        for p in range(n_parts):
            @pl.when(pl.program_id(0) // npt == p)
            def _(p=p):
                o_ref[...] = refs[p][...]

    return pl.pallas_call(
        body, name="place_columns", grid=(n_parts * npt, bsz, seq // ts),
        in_specs=[part(p) for p in range(n_parts)] + [ANY],
        out_specs=_cur(ts, tc, off // tc), out_shape=jax.ShapeDtypeStruct(into3.shape, into3.dtype),
        input_output_aliases={n_parts: 0}, compiler_params=_params(),
    )(*pieces, into3)


def _place():
    x, y, c = lax.axis_index("x"), lax.axis_index("y"), lax.axis_index("c")
    return x, y, c, [(1 - x, y), (x, 1 - y), (1 - x, 1 - y)]


def _aligned(start, align):
    return start if isinstance(start, int) else pl.multiple_of(start, align)


class _Layout:
    def __init__(self, kind, shard_shape):
        self.kind = kind
        self.r, self.c = shard_shape

    def full_shape(self):
        r, c = self.r, self.c
        return {"major": (N_CHIPS, r, c), "rows": (N_CHIPS * r, c), "cols": (r, N_CHIPS * c)}[self.kind]

    def region(self, ref, j, half=None):
        r, c = self.r, self.c
        r0, nr = (0, r) if half is None else (half * (r // 2), r // 2)
        if self.kind == "major":
            return ref.at[j, pl.ds(_aligned(r0, 16), nr), :]
        if self.kind == "rows":
            return ref.at[pl.ds(_aligned(j * r + r0, 16), nr), :]
        return ref.at[pl.ds(_aligned(r0, 16), nr), pl.ds(_aligned(j * c, LANES), c)]

    def block_spec(self, tr, where):
        r, c = self.r, self.c
        if self.kind == "major":
            return pl.BlockSpec((None, tr, c), lambda *a: (where(*a)[0], where(*a)[1], 0))
        if self.kind == "rows":
            return pl.BlockSpec((tr, c), lambda *a: (where(*a)[0] * (r // tr) + where(*a)[1], 0))
        return pl.BlockSpec((tr, c), lambda *a: (where(*a)[1], where(*a)[0]))


def _remote(src, dst, send_sem, recv_sem, dev):
    return pltpu.make_async_remote_copy(src_ref=src, dst_ref=dst, send_sem=send_sem, recv_sem=recv_sem,
                                        device_id=dev, device_id_type=MESH)


def _all_gather(big, layouts, small):
    nb, ns = len(big), len(small)
    n_remote = 6 * nb + 3 * ns

    def body(*refs):
        ins, outs = refs[:nb + ns], refs[nb + ns:2 * (nb + ns)]
        send_sems, recv_sems, local_sems = refs[2 * (nb + ns):]
        x, y, c, chips = _place()
        j = 2 * x + y
        local = []
        for i in range(ns):
            local.append(pltpu.make_async_copy(ins[nb + i], outs[nb + i].at[j], local_sems.at[i]))
        for cp in local:
            cp.start()
        sends = []
        for i in range(nb):
            for p, (px, py) in enumerate(chips):
                k = 3 * i + p
                mine = layouts[i].region(outs[i], j, c)
                sends.append(_remote(mine, mine, send_sems.at[k], recv_sems.at[k], (px, py, c)))
        for i in range(ns):
            for p, (px, py) in enumerate(chips):
                k = 6 * nb + 3 * i + p
                sends.append(_remote(ins[nb + i], outs[nb + i].at[j], send_sems.at[k], recv_sems.at[k], (px, py, c)))
        for cp in sends:
            cp.start()
        for i in range(nb):
            for p, (px, py) in enumerate(chips):
                k, jp = 3 * i + p, 2 * px + py
                got = layouts[i].region(outs[i], jp, c)
                _remote(got, got, send_sems.at[k], recv_sems.at[k], (px, py, c)).wait_recv()
                fwd = _remote(got, got, send_sems.at[3 * nb + k], recv_sems.at[3 * nb + k], (x, y, 1 - c))
                fwd.start()
                sends.append(fwd)
        for i in range(ns):
            for p, (px, py) in enumerate(chips):
                k, jp = 6 * nb + 3 * i + p, 2 * px + py
                _remote(ins[nb + i], outs[nb + i].at[jp], send_sems.at[k], recv_sems.at[k], (px, py, c)).wait_recv()
        for i in range(nb):
            for p, (px, py) in enumerate(chips):
                k, jp = 3 * nb + 3 * i + p, 2 * px + py
                got = layouts[i].region(outs[i], jp, 1 - c)
                _remote(got, got, send_sems.at[k], recv_sems.at[k], (x, y, 1 - c)).wait_recv()
        for cp in sends:
            cp.wait_send()
        for cp in local:
            cp.wait()

    out_shape = [jax.ShapeDtypeStruct(lay.full_shape(), BF16) for lay in layouts]
    out_shape += [jax.ShapeDtypeStruct((N_CHIPS,) + s.shape, F32) for s in small]
    return pl.pallas_call(
        body, name="all_gather_weights", in_specs=[ANY] * (nb + ns), out_specs=[ANY] * (nb + ns), out_shape=out_shape,
        input_output_aliases={i: i for i in range(nb)},
        scratch_shapes=[pltpu.SemaphoreType.DMA((n_remote,)), pltpu.SemaphoreType.DMA((n_remote,)),
                        pltpu.SemaphoreType.DMA((ns,))],
        compiler_params=_params(),
    )(*big, *small)


HBM = pl.BlockSpec(memory_space=pltpu.HBM)
SEM = pl.BlockSpec(memory_space=pltpu.SEMAPHORE)
SPLIT_COPY = pltpu.CompilerParams(has_side_effects=pltpu.SideEffectType.DATAFLOW_SIDE_EFFECTING)


def _in_hbm(a):
    return pltpu.with_memory_space_constraint(a, pltpu.HBM)


def _split_copy_start(arrays, plan, n_copies, name, after=None):
    na = len(arrays)
    order_only = [] if after is None else [after]

    def body(*refs):
        base = na + len(order_only)
        send_sems, recv_sems = refs[base], refs[base + 1]
        thru, token = refs[base + 2:base + 2 + na], refs[base + 2 + na]
        for k, (src, dst, _, dev) in enumerate(plan(thru, *_place())):
            _remote(src, dst, send_sems.at[k], recv_sems.at[k], dev).start()
        token[...] = jnp.zeros_like(token)

    outs = pl.pallas_call(
        body, name=name, in_specs=[HBM] * na + [ANY] * len(order_only),
        out_specs=[SEM, SEM] + [HBM] * na + [pl.BlockSpec(memory_space=pltpu.VMEM)],
        out_shape=[pltpu.SemaphoreType.DMA((n_copies,)), pltpu.SemaphoreType.DMA((n_copies,))]
        + [pltpu.HBM(a.shape, a.dtype) for a in arrays] + [jax.ShapeDtypeStruct((8, LANES), F32)],
        input_output_aliases={i: 2 + i for i in range(na)}, compiler_params=SPLIT_COPY,
    )(*[_in_hbm(a) for a in arrays], *order_only)
    return (outs[0], outs[1]), list(outs[2:2 + na]), outs[2 + na]


def _split_copy_wait(sems, arrays, plan, name, after):
    na = len(arrays)

    def body(*refs):
        send_sems, recv_sems = refs[na], refs[na + 1]
        thru = refs[na + 3:]
        for k, (src, _, landing, dev) in enumerate(plan(thru, *_place())):
            cp = _remote(src, landing, send_sems.at[k], recv_sems.at[k], dev)
            cp.wait_send()
            cp.wait_recv()

    return list(pl.pallas_call(
        body, name=name, in_specs=[HBM] * na + [SEM, SEM, ANY], out_specs=[HBM] * na,
        out_shape=[pltpu.HBM(a.shape, a.dtype) for a in arrays],
        input_output_aliases={i: i for i in range(na)}, compiler_params=SPLIT_COPY,
    )(*arrays, *sems, after))


def _gather_plan(layouts):
    def plan(bufs, x, y, c, chips):
        copies = []
        for buf, lay in zip(bufs, layouts):
            mine = lay.region(buf, 2 * x + y, c)
            copies += [(mine, mine, lay.region(buf, 2 * px + py, c), (px, py, c)) for px, py in chips]
        return copies
    return plan


def _forward_plan(layouts):
    def plan(bufs, x, y, c, chips):
        copies = []
        for buf, lay in zip(bufs, layouts):
            for px, py in chips:
                got = lay.region(buf, 2 * px + py, c)
                copies.append((got, got, lay.region(buf, 2 * px + py, 1 - c), (x, y, 1 - c)))
        return copies
    return plan


def _halves_plan(layouts):
    def plan(arrays, x, y, c, chips):
        nw = len(layouts)
        copies = []
        for i, lay in enumerate(layouts):
            for j in range(N_CHIPS):
                land = arrays[nw + i].at[j]
                copies.append((lay.region(arrays[i], j, 1 - c), land, land, (x, y, 1 - c)))
        return copies
    return plan


def _partials_plan(nw):
    def plan(arrays, x, y, c, chips):
        copies = []
        for i in range(nw):
            for p, (px, py) in enumerate(chips):
                land = arrays[nw + i].at[p]
                copies.append((arrays[i].at[2 * px + py], land, land, (px, py, c)))
        return copies
    return plan


def _join_plan(nw):
    def plan(arrays, x, y, c, chips):
        return [(arrays[i], arrays[nw + i], arrays[nw + i], (x, y, 1 - c)) for i in range(nw)]
    return plan


def _forward_to_sibling(bufs, layouts, name):
    nb = len(bufs)

    def body(*refs):
        outs = refs[nb:2 * nb]
        send_sems, recv_sems = refs[2 * nb:]
        x, y, c, chips = _place()
        cps = []
        for i in range(nb):
            for p, (px, py) in enumerate(chips):
                got = layouts[i].region(outs[i], 2 * px + py, c)
                cps.append(_remote(got, got, send_sems.at[3 * i + p], recv_sems.at[3 * i + p], (x, y, 1 - c)))
        for cp in cps:
            cp.start()
        for i in range(nb):
            for p, (px, py) in enumerate(chips):
                theirs = layouts[i].region(outs[i], 2 * px + py, 1 - c)
                _remote(theirs, theirs, send_sems.at[3 * i + p], recv_sems.at[3 * i + p], (x, y, 1 - c)).wait_recv()
        for cp in cps:
            cp.wait_send()

    return pl.pallas_call(
        body, name=name, in_specs=[ANY] * nb, out_specs=[ANY] * nb,
        out_shape=[jax.ShapeDtypeStruct(b.shape, b.dtype) for b in bufs],
        input_output_aliases={i: i for i in range(nb)},
        scratch_shapes=[pltpu.SemaphoreType.DMA((3 * nb,)), pltpu.SemaphoreType.DMA((3 * nb,))],
        compiler_params=_params(),
    )(*bufs)


def _halves_to_sibling(grads, layouts, name):
    nw = len(grads)

    def body(*refs):
        ins, gots = refs[:nw], refs[nw:2 * nw]
        send_sems, recv_sems = refs[2 * nw:]
        x, y, c, _ = _place()
        cps = []
        for i in range(nw):
            for j in range(N_CHIPS):
                k = N_CHIPS * i + j
                cps.append(_remote(layouts[i].region(ins[i], j, 1 - c), gots[i].at[j],
                                   send_sems.at[k], recv_sems.at[k], (x, y, 1 - c)))
        for cp in cps:
            cp.start()
        for cp in cps:
            cp.wait()

    half = [jax.ShapeDtypeStruct((N_CHIPS, lay.r // 2, lay.c), BF16) for lay in layouts]
    return pl.pallas_call(
        body, name=name, in_specs=[ANY] * nw, out_specs=[ANY] * nw, out_shape=half,
        scratch_shapes=[pltpu.SemaphoreType.DMA((N_CHIPS * nw,)), pltpu.SemaphoreType.DMA((N_CHIPS * nw,))],
        compiler_params=_params(),
    )(*grads)


def _chip_sum(grad, got, layout, core, name):
    n, hr, c = got.shape
    tr = _tile(hr, 256, 16)
    nb = hr // tr

    def body(core_ref, a_ref, b_ref, o_ref):
        o_ref[...] = (a_ref[...].astype(F32) + b_ref[...].astype(F32)).astype(BF16)

    spec = pl.BlockSpec((None, tr, c), lambda j, i, core_ref: (j, i, 0))
    return pl.pallas_call(
        body, name=name,
        grid_spec=pltpu.PrefetchScalarGridSpec(
            num_scalar_prefetch=1, grid=(n, nb),
            in_specs=[layout.block_spec(tr, lambda j, i, core_ref: (j, core_ref[0] * nb + i)), spec],
            out_specs=spec),
        out_shape=jax.ShapeDtypeStruct((n, hr, c), BF16), compiler_params=_params(),
    )(core, grad, got)


def _halves_start(grads, layouts, name, after=None):
    lands = [lax.empty((N_CHIPS, lay.r // 2, lay.c), BF16) for lay in layouts]
    return _split_copy_start(list(grads) + lands, _halves_plan(layouts), N_CHIPS * len(grads), name, after)


def _partials_start(parts, name, after=None):
    lands = [lax.empty((3,) + p.shape[1:], BF16) for p in parts]
    return _split_copy_start(list(parts) + lands, _partials_plan(len(parts)), 3 * len(parts), name, after)


def _shard_sum(parts, got, chip, name):
    _, r, c = parts.shape
    tr = _tile(r, 256, 16)

    def body(chip_ref, o_ref, g_ref, out_ref):
        acc = o_ref[...].astype(F32)
        for p in range(3):
            acc = acc + g_ref[p].astype(F32)
        out_ref[...] = acc

    return pl.pallas_call(
        body, name=name,
        grid_spec=pltpu.PrefetchScalarGridSpec(
            num_scalar_prefetch=1, grid=(r // tr,),
            in_specs=[pl.BlockSpec((None, tr, c), lambda i, chip_ref: (chip_ref[0], i, 0)),
                      pl.BlockSpec((3, tr, c), lambda i, chip_ref: (0, i, 0))],
            out_specs=pl.BlockSpec((tr, c), lambda i, chip_ref: (i, 0))),
        out_shape=jax.ShapeDtypeStruct((r, c), F32), compiler_params=_params(),
    )(chip, parts, got)


def _join_start(halves, name):
    lands = [lax.empty(h.shape, F32) for h in halves]
    return _split_copy_start(list(halves) + lands, _join_plan(len(halves)), len(halves), name)


def _adamw_rows(w, g_half, m, v, half, filled, after, name):
    r, c = w.shape
    hr = r // 2
    tr = _tile(hr, 128, 8)
    nb = hr // tr
    c1 = 1.0 - ADAM_B1 ** ADAM_STEP
    c2 = 1.0 - ADAM_B2 ** ADAM_STEP
    n_prev = 0 if filled is None else 4

    def body(half_ref, w_ref, gin_ref, m_ref, v_ref, *rest):
        g_ref, d_ref, nm_ref, nv_ref = rest[1 + n_prev:]
        gg = gin_ref[...]
        nm = ADAM_B1 * m_ref[...] + (1.0 - ADAM_B1) * gg
        nv = ADAM_B2 * v_ref[...] + (1.0 - ADAM_B2) * jnp.square(gg)
        m_hat = nm / c1
        v_hat = nv / c2
        g_ref[...] = gg
        d_ref[...] = -ADAM_LR * (m_hat / (jnp.sqrt(v_hat) + ADAM_EPS) + ADAM_WD * w_ref[...])
        nm_ref[...] = nm
        nv_ref[...] = nv

    full = pl.BlockSpec((tr, c), lambda i, half_ref: (half_ref[0] * nb + i, 0))
    part = pl.BlockSpec((tr, c), lambda i, half_ref: (i, 0))
    return pl.pallas_call(
        body, name=name,
        grid_spec=pltpu.PrefetchScalarGridSpec(
            num_scalar_prefetch=1, grid=(nb,), in_specs=[full, part, full, full] + [ANY] * (1 + n_prev),
            out_specs=[full] * 4),
        out_shape=[jax.ShapeDtypeStruct((r, c), F32)] * 4,
        input_output_aliases={6 + k: k for k in range(n_prev)}, compiler_params=_params(),
    )(half, w, g_half, m, v, after, *([] if filled is None else filled))


def _small_all_reduce(buf):
    rows = buf.shape[0]
    n_dev = 8

    def body(b_ref, o_ref, gath, send_sems, recv_sems):
        x, y, c, _ = _place()
        me = 4 * x + 2 * y + c
        gath[me] = b_ref[...]
        cps = []
        for k in range(1, n_dev):
            px, py, pc = (x + (k >> 2)) % 2, (y + ((k >> 1) & 1)) % 2, (c + (k & 1)) % 2
            cps.append(_remote(b_ref, gath.at[me], send_sems.at[k - 1], recv_sems.at[k - 1], (px, py, pc)))
        for cp in cps:
            cp.start()
        for k in range(1, n_dev):
            px, py, pc = (x + (k >> 2)) % 2, (y + ((k >> 1) & 1)) % 2, (c + (k & 1)) % 2
            _remote(b_ref, gath.at[4 * px + 2 * py + pc], send_sems.at[k - 1], recv_sems.at[k - 1], (px, py, pc)).wait_recv()
        for cp in cps:
            cp.wait_send()
        acc = gath[0]
        for dev in range(1, n_dev):
            acc = acc + gath[dev]
        o_ref[...] = acc

    vm = pl.BlockSpec(memory_space=pltpu.VMEM)
    return pl.pallas_call(
        body, name="small_all_reduce", in_specs=[vm], out_specs=vm,
        out_shape=jax.ShapeDtypeStruct((rows, LANES), F32),
        scratch_shapes=[pltpu.VMEM((n_dev, rows, LANES), F32), pltpu.SemaphoreType.DMA((n_dev - 1,)),
                        pltpu.SemaphoreType.DMA((n_dev - 1,))],
        compiler_params=_params(),
    )(buf)


def _pad_lanes(v):
    return jnp.pad(v, ((0, 0), (0, LANES - v.shape[-1])))


def _pack(vectors):
    flat, offs, pos = [], [], 0
    for v in vectors:
        n = v.size
        n_pad = -(-n // LANES) * LANES
        flat.append(jnp.pad(v.reshape(-1), (0, n_pad - n)))
        offs.append((pos, n, v.shape))
        pos += n_pad
    total = -(-pos // (8 * LANES)) * 8 * LANES
    flat.append(jnp.zeros((total - pos,), F32))
    return jnp.concatenate(flat).reshape(-1, LANES), offs


def _unpack(buf, offs):
    flat = buf.reshape(-1)
    return [flat[pos:pos + n].reshape(shape) for pos, n, shape in offs]


def kernel(x, norm_mix_pre, w_in, conv_qkv_w, a_log, dt_bias, gdn_norm_w, conv_sc_w, w_out, norm_mix_post, norm_mlp_pre, w_up, w_down, norm_mlp_post, loss_target, m_norm_mix_pre, m_w_in, m_conv_qkv_w, m_a_log, m_dt_bias, m_gdn_norm_w, m_conv_sc_w, m_w_out, m_norm_mix_post, m_norm_mlp_pre, m_w_up, m_w_down, m_norm_mlp_post, v_norm_mix_pre, v_w_in, v_conv_qkv_w, v_a_log, v_dt_bias, v_gdn_norm_w, v_conv_sc_w, v_w_out, v_norm_mix_post, v_norm_mlp_pre, v_w_up, v_w_down, v_norm_mlp_post):
    bsz, seq, d = x.shape
    t = bsz * seq
    heads, head_dim = a_log.shape[-1], gdn_norm_w.shape[-1]
    assert head_dim == LANES and seq % CHUNK == 0
    gw = heads * head_dim
    sw = conv_sc_w.shape[-1] * N_CHIPS
    ics = w_in.shape[-1]
    main = 4 * gw + 3 * sw
    assert ics * N_CHIPS == main + 2 * heads and 2 * heads <= LANES

    lay_in = _Layout("major", w_in.shape[1:])
    lay_out = _Layout("rows", w_out.shape[1:])
    lay_up = _Layout("cols", w_up.shape[1:])
    lay_down = _Layout("rows", w_down.shape[1:])
    layouts = [lay_in, lay_out, lay_up, lay_down]
    chip = (2 * lax.axis_index("x") + lax.axis_index("y")).astype(jnp.int32).reshape(1)
    core = lax.axis_index("c").astype(jnp.int32).reshape(1)
    x2 = x.reshape(t, d)
    cq_g, cs_g = _all_gather([], [], [conv_qkv_w[0], conv_sc_w[0]])
    plan_in, plan_ou, plan_down = _gather_plan(layouts[:1]), _gather_plan(layouts[1:3]), _gather_plan(layouts[3:])
    in_buf = _cast_into_layout(w_in[0], lay_in, chip, "cast_w_in")
    in_sems, in_bufs, in_token = _split_copy_start([in_buf], plan_in, 3, "gather_in_start", after=cq_g)
    shards = [_cast_into_layout(w[0], lay, chip, f"cast_{n}", after=in_token)
              for w, lay, n in zip((w_out, w_up, w_down), layouts[1:], ("w_out", "w_up", "w_down"))]
    xn = _norm_fwd(x2, norm_mix_pre, shards[-1])
    in_bufs = _split_copy_wait(in_sems, in_bufs, plan_in, "gather_in_wait", xn)
    win_sh, = _forward_to_sibling(in_bufs, layouts[:1], "forward_w_in")
    ou_sems, ou_bufs, ou_token = _split_copy_start(shards[:2], plan_ou, 6, "gather_out_up_start", after=win_sh)
    w_main, w_ab = _repack_w_in(win_sh, gw, heads, sw, ou_token)
    conv_q = cq_g.transpose(1, 0, 2).reshape(conv_qkv_w.shape[1], -1)
    conv_s = cs_g.transpose(1, 0, 2).reshape(conv_sc_w.shape[1], -1)

    tgt2 = loss_target.reshape(t, d)
    proj = _matmul(xn, w_main, "nn", [F32], "proj_main")
    proj_ab = _matmul(xn, w_ab, "nn", [F32], "proj_ab")
    proj3 = proj.reshape(bsz, seq, main)
    qkv_act = _qkv_conv_fwd(proj3, conv_q, 3 * gw).reshape(t, 3 * gw)
    a_log_pad, dt_pad = _pad_lanes(a_log), _pad_lanes(dt_bias)
    gcb, betab = _gates_fwd(proj_ab, a_log_pad, dt_pad, heads)
    o_raw, states, *gdn_solved = _gdn_fwd(qkv_act, gcb, betab, bsz, heads)
    ou_bufs = _split_copy_wait(ou_sems, ou_bufs, plan_ou, "gather_out_up_wait", o_raw)
    fwd_plan = _forward_plan(layouts[1:3])
    fwd_sems, ou_bufs, fwd_token = _split_copy_start(ou_bufs, fwd_plan, 6, "forward_out_up_start")
    down_sems, down_bufs, down_token = _split_copy_start(shards[2:], plan_down, 3, "gather_down_start", after=fwd_token)
    gdn_out = _gdn_out_fwd(o_raw, proj, gdn_norm_w, heads, 3 * gw, gw + sw, down_token)
    mixed = _sc_fwd(proj3, conv_s, 4 * gw, sw, gdn_out.reshape(bsz, seq, gw + sw), gw).reshape(t, gw + sw)
    wout_f, wup_f = _split_copy_wait(fwd_sems, ou_bufs, fwd_plan, "forward_out_up_wait", mixed)
    mix = _matmul(mixed, wout_f, "nn", [F32], "mix_out")
    h, hn = _mid_fwd(x2, mix, norm_mix_post, norm_mlp_pre)

    def up_epilogue(acc):
        r = jnp.maximum(acc, 0.0)
        return r, r * r

    relu_up, hid = _matmul(hn, wup_f, "nn", [BF16, BF16], "mlp_up", epilogue=up_epilogue)
    down_bufs = _split_copy_wait(down_sems, down_bufs, plan_down, "gather_down_wait", hid)
    (wdown_f,) = _forward_to_sibling(down_bufs, layouts[3:], "forward_w_down")
    ff = _matmul(hid, wdown_f, "nn", [F32], "mlp_down")
    loss_blk, dy, dff, dg_mlp_post = _head_fwd_bwd(h, ff, tgt2, norm_mlp_post)

    def dup_epilogue(acc, r):
        return (acc * (2.0 * r.astype(F32)),)

    d_up = _matmul(dff, wdown_f, "nt", [BF16], "d_hid", epilogue=dup_epilogue, extras=(relu_up,))
    dw_down = _matmul(hid, dff, "tn", [BF16], "dw_down")
    plan_h_down, plan_h_up, plan_h_in = _halves_plan([lay_down]), _halves_plan([lay_up]), _halves_plan([lay_in])
    hd_sems, hd_arrays, hd_token = _halves_start([dw_down], [lay_down], "down_halves_start")
    d_hn = _matmul(d_up, wup_f, "nt", [F32], "d_hn", after=hd_token)
    dw_up = _matmul(hn, d_up, "tn", [BF16], "dw_up")
    dw_down, down_got = _split_copy_wait(hd_sems, hd_arrays, plan_h_down, "down_halves_wait", dw_up)
    hu_sems, hu_arrays, hu_token = _halves_start([dw_up], [lay_up], "up_halves_start", after=down_got)
    down_part = _chip_sum(dw_down, down_got, lay_down, core, "chip_sum_w_down")
    pd_sems, pd_arrays, pd_token = _partials_start([down_part], "down_partials_start", after=hu_token)
    dh, dmix, dg_mlp_pre, dg_mix_post = _mid_bwd(d_hn, h, norm_mlp_pre, dy, mix, norm_mix_post, pd_token)
    dmixed = _matmul(dmix, wout_f, "nt", [F32], "d_mixed")
    dw_out = _matmul(mixed, dmix, "tn", [BF16], "dw_out")
    dw_up, up_got = _split_copy_wait(hu_sems, hu_arrays, plan_h_up, "up_halves_wait", dw_out)
    up_part = _chip_sum(dw_up, up_got, lay_up, core, "chip_sum_w_up")
    out_got, = _halves_to_sibling([dw_out], [lay_out], "out_grad_halves_to_sibling")
    out_part = _chip_sum(dw_out, out_got, lay_out, core, "chip_sum_w_out")
    puo_sems, puo_arrays, puo_token = _partials_start([up_part, out_part], "up_out_partials_start", after=dmixed)
    dmixed3 = dmixed.reshape(bsz, seq, d)
    d_b, d_c, d_hsc, dw_conv_s = _sc_bwd(proj3, dmixed3, conv_s, 4 * gw, sw, gw)
    d_o, d_z, dg_gdn_norm = _gdn_out_bwd(dmixed, o_raw, proj, gdn_norm_w, heads, 3 * gw, puo_token)
    dq, dk, dv, dgc_b, dbeta_b = _gdn_bwd(qkv_act, gcb, betab, states, gdn_solved, d_o, bsz, heads)
    d_ab, d_alog, d_dt = _gates_bwd(proj_ab, a_log_pad, dt_pad, dgc_b, dbeta_b, heads)
    d_proj3, dw_conv_q = _qkv_conv_bwd(proj3, [a.reshape(bsz, seq, gw) for a in (dq, dk, dv)], conv_q, 3 * gw,
                                       d_z.reshape(bsz, seq, main))
    d_proj = _place_columns(d_proj3, [d_b, d_c, d_hsc], 4 * gw).reshape(t, main)
    dw_main = _matmul(xn, d_proj, "tn", [BF16], "dw_in_main")
    dw_ab = _matmul(xn, d_ab, "tn", [BF16], "dw_in_ab")
    dw_in = _unpack_dw_in(dw_main, dw_ab, gw, heads, sw, ics)
    hi_sems, hi_arrays, hi_token = _halves_start([dw_in], [lay_in], "in_halves_start")
    d_xn = _matmul(d_proj, w_main, "nt", [F32], "d_xn_main", after=hi_token)
    dw_in, in_got = _split_copy_wait(hi_sems, hi_arrays, plan_h_in, "in_halves_wait", d_xn)
    in_part = _chip_sum(dw_in, in_got, lay_in, core, "chip_sum_w_in")
    pi_sems, pi_arrays, pi_token = _partials_start([in_part], "in_partials_start")
    d_xn_ab = _matmul(d_ab, w_ab, "nt", [F32], "d_xn_ab", after=pi_token)
    grad_x, dg_mix_pre = _first_bwd(d_xn, d_xn_ab, x2, norm_mix_pre, dh)

    other_core = 1 - core

    def finish(parts, recvs, weights3, names, tag):
        nw = len(parts)
        halves = [_shard_sum(p, r, chip, f"shard_sum_{n}") for p, r, n in zip(parts, recvs, names)]
        sems, arrays, token = _join_start(halves, f"{tag}_join_start")
        own = [_adamw_rows(wt[0], h, m[0], v[0], core, None, token, f"adamw_own_{n}")
               for (wt, m, v), h, n in zip(weights3, arrays[:nw], names)]
        theirs = _split_copy_wait(sems, arrays, _join_plan(nw), f"{tag}_join_wait", own[-1][1])[nw:]
        return [_adamw_rows(wt[0], h, m[0], v[0], other_core, o, h, f"adamw_sibling_{n}")
                for (wt, m, v), h, o, n in zip(weights3, theirs, own, names)]

    down_part, down_recv = _split_copy_wait(pd_sems, pd_arrays, _partials_plan(1), "down_partials_wait", grad_x)
    up_part, out_part, up_recv, out_recv = _split_copy_wait(puo_sems, puo_arrays, _partials_plan(2),
                                                            "up_out_partials_wait", down_recv)
    res_up, res_down = finish([up_part, down_part], [up_recv, down_recv],
                              [(w_up, m_w_up, v_w_up), (w_down, m_w_down, v_w_down)], ("w_up", "w_down"), "mlp")
    in_part, in_recv = _split_copy_wait(pi_sems, pi_arrays, _partials_plan(1), "in_partials_wait", res_down[1])
    res_in, res_out = finish([in_part, out_part], [in_recv, out_recv],
                             [(w_in, m_w_in, v_w_in), (w_out, m_w_out, v_w_out)], ("w_in", "w_out"), "mix")

    small, offs = _pack([loss_blk[0:1, 0:1], dg_mix_pre, dw_conv_q, d_alog[:, :heads], d_dt[:, :heads], dg_gdn_norm,
                         dw_conv_s, dg_mix_post, dg_mlp_pre, dg_mlp_post])
    (loss, g_mix_pre, g_conv_q_full, g_alog, g_dt, g_gdn_norm, g_conv_s_full, g_mix_post, g_mlp_pre,
     g_mlp_post) = _unpack(_small_all_reduce(small), offs)
    j = 2 * lax.axis_index("x") + lax.axis_index("y")
    cq_w, cs_w = conv_qkv_w.shape[-1], conv_sc_w.shape[-1]
    g_conv_q = lax.dynamic_slice_in_dim(g_conv_q_full, j * cq_w, cq_w, axis=1)
    g_conv_s = lax.dynamic_slice_in_dim(g_conv_s_full, j * cs_w, cs_w, axis=1)

    big = {1: res_in, 7: res_out, 10: res_up, 11: res_down}
    grads = [g_mix_pre, None, g_conv_q, g_alog, g_dt, g_gdn_norm, g_conv_s, None, g_mix_post, g_mlp_pre, None,
             None, g_mlp_post]
    weights = [norm_mix_pre, w_in, conv_qkv_w, a_log, dt_bias, gdn_norm_w, conv_sc_w, w_out, norm_mix_post,
               norm_mlp_pre, w_up, w_down, norm_mlp_post]
    ms = [m_norm_mix_pre, m_w_in, m_conv_qkv_w, m_a_log, m_dt_bias, m_gdn_norm_w, m_conv_sc_w, m_w_out,
          m_norm_mix_post, m_norm_mlp_pre, m_w_up, m_w_down, m_norm_mlp_post]
    vs = [v_norm_mix_pre, v_w_in, v_conv_qkv_w, v_a_log, v_dt_bias, v_gdn_norm_w, v_conv_sc_w, v_w_out,
          v_norm_mix_post, v_norm_mlp_pre, v_w_up, v_w_down, v_norm_mlp_post]
    out_g, out_d, out_m, out_v = [], [], [], []
    for i, (wt, g, m, v) in enumerate(zip(weights, grads, ms, vs)):
        shape2 = wt.shape[-2:] if wt.ndim == 3 else wt.shape
        if i in big:
            g2, dl, nm, nv = big[i]
        else:
            g2 = g.reshape(shape2)
            dl, nm, nv = _adamw(wt.reshape(shape2), g2, m.reshape(shape2), v.reshape(shape2), f"adamw_{i}")
        out_g.append(g2.reshape(wt.shape))
        out_d.append(dl.reshape(wt.shape))
        out_m.append(nm.reshape(wt.shape))
        out_v.append(nv.reshape(wt.shape))

    return (loss.reshape(()), grad_x.reshape(bsz, seq, d), *out_g, *out_d, *out_m, *out_v)
```

```python
import functools

import jax
import jax.numpy as jnp
from jax import lax
from jax.experimental import pallas as pl
from jax.experimental.pallas import tpu as pltpu

CHUNK = 64
NORM_EPS = 1e-6
L2_EPS = 1e-6
N_CHIPS = 4
ADAM_LR = 0.001
ADAM_B1 = 0.9
ADAM_B2 = 0.999
ADAM_EPS = 1e-08
ADAM_WD = 0.01
ADAM_STEP = 10
LANES = 128
VMEM_LIMIT = 56 * 1024 * 1024

F32 = jnp.float32
BF16 = jnp.bfloat16
HI = lax.Precision.HIGH
EXACT_SUM = lax.Precision.HIGHEST
MESH = pl.DeviceIdType.MESH
ANY = pl.BlockSpec(memory_space=pl.ANY)


def _params(n_grid=0):
    return pltpu.CompilerParams(vmem_limit_bytes=VMEM_LIMIT)


def _tile(n, pref, align):
    if n <= pref:
        return n
    t = (pref // align) * align
    while t >= align:
        if n % t == 0:
            return t
        t -= align
    raise ValueError(f"no tile for {n}")


def _sigmoid(x):
    return 1.0 / (1.0 + jnp.exp(-x))


def _softplus(x):
    return jnp.maximum(x, 0.0) + jnp.log(1.0 + jnp.exp(-jnp.abs(x)))


def _rms_fwd(x, g):
    r = lax.rsqrt(jnp.mean(x * x, axis=-1, keepdims=True) + NORM_EPS)
    return x * r * g


def _rms_bwd(dy, x, g):
    r = lax.rsqrt(jnp.mean(x * x, axis=-1, keepdims=True) + NORM_EPS)
    xh = x * r
    dxh = dy * g
    dx = r * (dxh - xh * jnp.mean(dxh * xh, axis=-1, keepdims=True))
    dg = jnp.sum(dy * xh, axis=0, keepdims=True)
    return dx, dg


def _matmul(a, b, form, out_dtypes, name, epilogue=None, extras=(), after=None, tm=1024, tn=1024, tk=4096):
    if form == "nn":
        (m, kd), (_, n) = a.shape, b.shape
        dims = (((1,), (0,)), ((), ()))
    elif form == "nt":
        (m, kd), (n, _) = a.shape, b.shape
        dims = (((1,), (1,)), ((), ()))
    else:
        (kd, m), (_, n) = a.shape, b.shape
        dims = (((0,), (0,)), ((), ()))
    tm, tn, tk = _tile(m, tm, LANES), _tile(n, tn, LANES), _tile(kd, tk, LANES)
    nk = kd // tk
    n_extra = len(extras)
    n_out = len(out_dtypes)

    if form == "nn":
        a_spec = pl.BlockSpec((tm, tk), lambda i, j, k: (i, k))
        b_spec = pl.BlockSpec((tk, tn), lambda i, j, k: (k, j))
    elif form == "nt":
        a_spec = pl.BlockSpec((tm, tk), lambda i, j, k: (i, k))
        b_spec = pl.BlockSpec((tn, tk), lambda i, j, k: (j, k))
    else:
        a_spec = pl.BlockSpec((tk, tm), lambda i, j, k: (k, i))
        b_spec = pl.BlockSpec((tk, tn), lambda i, j, k: (k, j))
    tile_spec = pl.BlockSpec((tm, tn), lambda i, j, k: (i, j))

    order_only = [] if after is None else [after]
    n_skip = n_extra + len(order_only)

    def finish(acc, extra_refs, out_refs):
        outs = (acc,) if epilogue is None else epilogue(acc, *[e[...] for e in extra_refs])
        for o_ref, val in zip(out_refs, outs):
            o_ref[...] = val.astype(o_ref.dtype)

    def body(a_ref, b_ref, *rest):
        extra_refs = rest[:n_extra]
        out_refs = rest[n_skip:n_skip + n_out]
        if nk == 1:
            finish(lax.dot_general(a_ref[...], b_ref[...], dims, preferred_element_type=F32), extra_refs, out_refs)
            return
        acc_ref = rest[-1]
        k = pl.program_id(2)

        @pl.when(k == 0)
        def _():
            acc_ref[...] = jnp.zeros_like(acc_ref)

        acc_ref[...] += lax.dot_general(a_ref[...], b_ref[...], dims, preferred_element_type=F32)

        @pl.when(k == nk - 1)
        def _():
            finish(acc_ref[...], extra_refs, out_refs)

    outs = pl.pallas_call(
        body, name=name, grid=(m // tm, n // tn, nk),
        in_specs=[a_spec, b_spec] + [tile_spec] * n_extra + [ANY] * len(order_only),
        out_specs=[tile_spec] * n_out,
        out_shape=[jax.ShapeDtypeStruct((m, n), dt) for dt in out_dtypes],
        scratch_shapes=[pltpu.VMEM((tm, tn), F32)] if nk > 1 else [],
        compiler_params=_params(),
    )(a, b, *extras, *order_only)
    return outs[0] if n_out == 1 else outs


def _cast_into_layout(w, layout, chip, name, after=None):
    r, c = w.shape
    tr = _tile(r, 256, 16)
    order_only = [] if after is None else [after]

    def body(chip_ref, w_ref, *rest):
        rest[-1][...] = w_ref[...].astype(BF16)

    return pl.pallas_call(
        body, name=name,
        grid_spec=pltpu.PrefetchScalarGridSpec(
            num_scalar_prefetch=1, grid=(r // tr,),
            in_specs=[pl.BlockSpec((tr, c), lambda i, chip_ref: (i, 0))] + [ANY] * len(order_only),
            out_specs=layout.block_spec(tr, lambda i, chip_ref: (chip_ref[0], i))),
        out_shape=jax.ShapeDtypeStruct(layout.full_shape(), BF16), compiler_params=_params(),
    )(chip, w, *order_only)


def _in_segments(gw, heads, sw):
    main = 4 * gw
    return [(0, main, 0), (main + 2 * heads, 3 * sw, main), (main, 2 * heads, main + 3 * sw)]


def _pieces(seg_start, width, dst_start, ics):
    out = []
    g = seg_start
    while g < seg_start + width:
        j, cj = divmod(g, ics)
        wdt = min(ics - cj, seg_start + width - g)
        out.append((j, cj, dst_start + (g - seg_start), wdt))
        g += wdt
    return out


def _repack_w_in(w_sh, gw, heads, sw, after):
    ns, d, ics = w_sh.shape
    main = 4 * gw + 3 * sw
    tr = _tile(d, 128, 16)
    pieces = [p for seg in _in_segments(gw, heads, sw) for p in _pieces(*seg, ics)]

    def body(w_ref, after_ref, m_ref, ab_ref):
        ab_ref[...] = jnp.zeros_like(ab_ref)
        for j, cj, cd, wdt in pieces:
            if cd >= main:
                ab_ref[:, cd - main:cd - main + wdt] = w_ref[j, :, cj:cj + wdt]
            else:
                m_ref[:, cd:cd + wdt] = w_ref[j, :, cj:cj + wdt]

    return pl.pallas_call(
        body, name="repack_w_in", grid=(d // tr,),
        in_specs=[pl.BlockSpec((ns, tr, ics), lambda i: (0, i, 0)), ANY],
        out_specs=[pl.BlockSpec((tr, main), lambda i: (i, 0)), pl.BlockSpec((tr, LANES), lambda i: (i, 0))],
        out_shape=[jax.ShapeDtypeStruct((d, main), BF16), jax.ShapeDtypeStruct((d, LANES), BF16)],
        compiler_params=_params(),
    )(w_sh, after)


def _unpack_dw_in(dw_main, dw_ab, gw, heads, sw, ics):
    d = dw_main.shape[0]
    tr = _tile(d, 128, 16)
    main = 4 * gw + 3 * sw
    pieces = [p for seg in _in_segments(gw, heads, sw) for p in _pieces(*seg, ics)]

    def body(m_ref, ab_ref, o_ref):
        for j, cj, cd, wdt in pieces:
            if cd >= main:
                o_ref[j, :, cj:cj + wdt] = ab_ref[:, cd - main:cd - main + wdt].astype(BF16)
            else:
                o_ref[j, :, cj:cj + wdt] = m_ref[:, cd:cd + wdt].astype(BF16)

    return pl.pallas_call(
        body, name="unpack_dw_in", grid=(d // tr,),
        in_specs=[pl.BlockSpec((tr, main), lambda i: (i, 0)), pl.BlockSpec((tr, LANES), lambda i: (i, 0))],
        out_specs=pl.BlockSpec((N_CHIPS, tr, ics), lambda i: (0, i, 0)),
        out_shape=jax.ShapeDtypeStruct((N_CHIPS, d, ics), BF16), compiler_params=_params(),
    )(dw_main, dw_ab)


def _adamw(w, g, m, v, name):
    r, c = w.shape
    tr = _tile(r, 128, 8)
    c1 = 1.0 - ADAM_B1 ** ADAM_STEP
    c2 = 1.0 - ADAM_B2 ** ADAM_STEP

    def body(w_ref, g_ref, m_ref, v_ref, d_ref, nm_ref, nv_ref):
        gg = g_ref[...]
        nm = ADAM_B1 * m_ref[...] + (1.0 - ADAM_B1) * gg
        nv = ADAM_B2 * v_ref[...] + (1.0 - ADAM_B2) * jnp.square(gg)
        m_hat = nm / c1
        v_hat = nv / c2
        d_ref[...] = -ADAM_LR * (m_hat / (jnp.sqrt(v_hat) + ADAM_EPS) + ADAM_WD * w_ref[...])
        nm_ref[...] = nm
        nv_ref[...] = nv

    spec = pl.BlockSpec((tr, c), lambda i: (i, 0))
    return pl.pallas_call(
        body, name=name, grid=(r // tr,), in_specs=[spec] * 4, out_specs=[spec] * 3,
        out_shape=[jax.ShapeDtypeStruct((r, c), F32)] * 3, compiler_params=_params(),
    )(w, g, m, v)


def _row_spec(tt, d):
    return pl.BlockSpec((tt, d), lambda i: (i, 0))


def _vec_spec(d):
    return pl.BlockSpec((1, d), lambda i: (0, 0))


def _norm_fwd(x, g, after):
    t, d = x.shape
    tt = _tile(t, 256, 16)

    def body(x_ref, g_ref, after_ref, o_ref):
        o_ref[...] = _rms_fwd(x_ref[...], g_ref[...]).astype(BF16)

    return pl.pallas_call(
        body, name="norm_mix_pre", grid=(t // tt,), in_specs=[_row_spec(tt, d), _vec_spec(d), ANY],
        out_specs=_row_spec(tt, d), out_shape=jax.ShapeDtypeStruct((t, d), BF16), compiler_params=_params(),
    )(x, g, after)


def _mid_fwd(x, mix, g_post, g_pre):
    t, d = x.shape
    tt = _tile(t, 128, 16)

    def body(x_ref, mix_ref, gp_ref, gn_ref, h_ref, hn_ref):
        h = x_ref[...] + _rms_fwd(mix_ref[...], gp_ref[...])
        h_ref[...] = h
        hn_ref[...] = _rms_fwd(h, gn_ref[...]).astype(BF16)

    return pl.pallas_call(
        body, name="mid_fwd", grid=(t // tt,),
        in_specs=[_row_spec(tt, d), _row_spec(tt, d), _vec_spec(d), _vec_spec(d)],
        out_specs=[_row_spec(tt, d), _row_spec(tt, d)],
        out_shape=[jax.ShapeDtypeStruct((t, d), F32), jax.ShapeDtypeStruct((t, d), BF16)],
        compiler_params=_params(),
    )(x, mix, g_post, g_pre)


def _head_fwd_bwd(h, ff, tgt, g_post):
    t, d = h.shape
    tt = _tile(t, 128, 16)

    def body(h_ref, ff_ref, t_ref, g_ref, loss_ref, dy_ref, dff_ref, dg_ref):
        i = pl.program_id(0)

        @pl.when(i == 0)
        def _():
            loss_ref[...] = jnp.zeros_like(loss_ref)
            dg_ref[...] = jnp.zeros_like(dg_ref)

        ff = ff_ref[...]
        g = g_ref[...]
        e = h_ref[...] + _rms_fwd(ff, g) - t_ref[...]
        loss_ref[...] += 0.5 * jnp.sum(jnp.mean(e * e, axis=-1, keepdims=True))
        dy = e * (1.0 / d)
        dy_ref[...] = dy
        dff, dg = _rms_bwd(dy, ff, g)
        dff_ref[...] = dff.astype(BF16)
        dg_ref[...] += dg

    return pl.pallas_call(
        body, name="loss_head", grid=(t // tt,),
        in_specs=[_row_spec(tt, d)] * 3 + [_vec_spec(d)],
        out_specs=[pl.BlockSpec((8, LANES), lambda i: (0, 0)), _row_spec(tt, d), _row_spec(tt, d), _vec_spec(d)],
        out_shape=[jax.ShapeDtypeStruct((8, LANES), F32), jax.ShapeDtypeStruct((t, d), F32),
                   jax.ShapeDtypeStruct((t, d), BF16), jax.ShapeDtypeStruct((1, d), F32)],
        compiler_params=_params(),
    )(h, ff, tgt, g_post)


def _mid_bwd(d_hn, h, g_pre, dy, mix, g_post, after):
    t, d = h.shape
    tt = _tile(t, 128, 16)

    def body(dhn_ref, h_ref, gn_ref, dy_ref, mix_ref, gp_ref, after_ref, dh_ref, dmix_ref, dgn_ref, dgp_ref):
        i = pl.program_id(0)

        @pl.when(i == 0)
        def _():
            dgn_ref[...] = jnp.zeros_like(dgn_ref)
            dgp_ref[...] = jnp.zeros_like(dgp_ref)

        dx, dgn = _rms_bwd(dhn_ref[...], h_ref[...], gn_ref[...])
        dh = dy_ref[...] + dx
        dh_ref[...] = dh
        dmix, dgp = _rms_bwd(dh, mix_ref[...], gp_ref[...])
        dmix_ref[...] = dmix.astype(BF16)
        dgn_ref[...] += dgn
        dgp_ref[...] += dgp

    return pl.pallas_call(
        body, name="mid_bwd", grid=(t // tt,),
        in_specs=[_row_spec(tt, d), _row_spec(tt, d), _vec_spec(d), _row_spec(tt, d), _row_spec(tt, d), _vec_spec(d),
                  ANY],
        out_specs=[_row_spec(tt, d), _row_spec(tt, d), _vec_spec(d), _vec_spec(d)],
        out_shape=[jax.ShapeDtypeStruct((t, d), F32), jax.ShapeDtypeStruct((t, d), BF16),
                   jax.ShapeDtypeStruct((1, d), F32), jax.ShapeDtypeStruct((1, d), F32)],
        compiler_params=_params(),
    )(d_hn, h, g_pre, dy, mix, g_post, after)


def _first_bwd(d_xn, d_xn_ab, x, g, dh):
    t, d = x.shape
    tt = _tile(t, 128, 16)

    def body(a_ref, b_ref, x_ref, g_ref, dh_ref, dx_ref, dg_ref):
        i = pl.program_id(0)

        @pl.when(i == 0)
        def _():
            dg_ref[...] = jnp.zeros_like(dg_ref)

        dx, dg = _rms_bwd(a_ref[...] + b_ref[...], x_ref[...], g_ref[...])
        dx_ref[...] = dh_ref[...] + dx
        dg_ref[...] += dg

    return pl.pallas_call(
        body, name="first_bwd", grid=(t // tt,),
        in_specs=[_row_spec(tt, d), _row_spec(tt, d), _row_spec(tt, d), _vec_spec(d), _row_spec(tt, d)],
        out_specs=[_row_spec(tt, d), _vec_spec(d)],
        out_shape=[jax.ShapeDtypeStruct((t, d), F32), jax.ShapeDtypeStruct((1, d), F32)],
        compiler_params=_params(),
    )(d_xn, d_xn_ab, x, g, dh)


HALO = 8


def _cur(ts, tc, off):
    return pl.BlockSpec((1, ts, tc), lambda ci, b, s: (b, s, off + ci))


def _prev(ts, tc, off):
    return pl.BlockSpec((1, HALO, tc), lambda ci, b, s: (b, jnp.maximum(s * (ts // HALO) - 1, 0), off + ci))


def _next(ts, tc, off, seq):
    last = seq // HALO - 1
    return pl.BlockSpec((1, HALO, tc), lambda ci, b, s: (b, jnp.minimum((s + 1) * (ts // HALO), last), off + ci))


def _conv_w_spec(kw, tc):
    return pl.BlockSpec((kw, tc), lambda ci, b, s: (0, ci))


def _rows_back(u, prev8, k):
    if k == 0:
        return u
    rolled = pltpu.roll(u, k, axis=0)
    row = lax.broadcasted_iota(jnp.int32, (HALO, u.shape[1]), 0)
    first = jnp.where(row < k, pltpu.roll(prev8, k, axis=0), rolled[0:HALO])
    return first if u.shape[0] == HALO else jnp.concatenate([first, rolled[HALO:]], axis=0)


def _rows_ahead(g, next8, m):
    if m == 0:
        return g
    n = g.shape[0]
    rolled = pltpu.roll(g, n - m, axis=0)
    row = lax.broadcasted_iota(jnp.int32, (HALO, g.shape[1]), 0)
    last = jnp.where(row >= HALO - m, pltpu.roll(next8, HALO - m, axis=0), rolled[n - HALO:n])
    return last if n == HALO else jnp.concatenate([rolled[0:n - HALO], last], axis=0)


def _windows(u, prev8, kw):
    return [_rows_back(u, prev8, kw - 1 - j) for j in range(kw)]


def _tap_sum(w, wins):
    acc = w[0:1, :] * wins[0]
    for j in range(1, len(wins)):
        acc = acc + w[j:j + 1, :] * wins[j]
    return acc


def _silu_grad(x):
    s = _sigmoid(x)
    return s * (1.0 + x * (1.0 - s))


def _qkv_conv_fwd(proj3, w, width):
    bsz, seq, _ = proj3.shape
    kw = w.shape[0]
    ts, tc = _tile(seq, 256, 8), _tile(width, 512, LANES)

    def body(u_ref, up_ref, w_ref, o_ref):
        prev8 = jnp.where(pl.program_id(2) == 0, 0.0, up_ref[0])
        pre = _tap_sum(w_ref[...], _windows(u_ref[0], prev8, kw))
        o_ref[0] = pre * _sigmoid(pre)

    return pl.pallas_call(
        body, name="qkv_conv_fwd", grid=(width // tc, bsz, seq // ts),
        in_specs=[_cur(ts, tc, 0), _prev(ts, tc, 0), _conv_w_spec(kw, tc)],
        out_specs=_cur(ts, tc, 0), out_shape=jax.ShapeDtypeStruct((bsz, seq, width), F32),
        compiler_params=_params(),
    )(proj3, proj3, w)


def _qkv_conv_bwd(proj3, dparts, w, width, into3):
    bsz, seq, _ = proj3.shape
    kw = w.shape[0]
    n_parts = len(dparts)
    part_w = width // n_parts
    ts, tc = _tile(seq, 256, 8), _tile(part_w, 512, LANES)
    n_s = seq // ts
    npt = part_w // tc
    last = seq // HALO - 1

    def part_cur(p):
        def index(ci, b, s):
            use = (ci // npt) == p
            return jnp.where(use, b, 0), jnp.where(use, s, 0), jnp.where(use, ci % npt, 0)
        return pl.BlockSpec((1, ts, tc), index)

    def part_next(p):
        def index(ci, b, s):
            use = (ci // npt) == p
            return (jnp.where(use, b, 0), jnp.where(use, jnp.minimum((s + 1) * (ts // HALO), last), 0),
                    jnp.where(use, ci % npt, 0))
        return pl.BlockSpec((1, HALO, tc), index)

    def body(u_ref, up_ref, un_ref, *rest):
        d_refs, dn_refs = rest[:n_parts], rest[n_parts:2 * n_parts]
        w_ref, _, du_ref, dw_ref, dbuf, dnbuf = rest[2 * n_parts:]
        ci, b, s = pl.program_id(0), pl.program_id(1), pl.program_id(2)

        @pl.when((b == 0) & (s == 0))
        def _():
            dw_ref[...] = jnp.zeros_like(dw_ref)

        for p in range(n_parts):
            @pl.when(ci // npt == p)
            def _(p=p):
                dbuf[...] = d_refs[p][0]
                dnbuf[...] = dn_refs[p][0]

        w = w_ref[...]
        u = u_ref[0]
        wins = _windows(u, jnp.where(s == 0, 0.0, up_ref[0]), kw)
        wins_next = _windows(un_ref[0], u[ts - HALO:ts], kw)
        g = dbuf[...] * _silu_grad(_tap_sum(w, wins))
        g_next = jnp.where(s == n_s - 1, 0.0, dnbuf[...] * _silu_grad(_tap_sum(w, wins_next)))
        for j in range(kw):
            dw_ref[j:j + 1, :] += jnp.sum(g * wins[j], axis=0, keepdims=True)
        du_ref[0] = _tap_sum(w, [_rows_ahead(g, g_next, kw - 1 - j) for j in range(kw)]).astype(BF16)

    return pl.pallas_call(
        body, name="qkv_conv_bwd", grid=(width // tc, bsz, n_s),
        in_specs=[_cur(ts, tc, 0), _prev(ts, tc, 0), _next(ts, tc, 0, seq)]
        + [part_cur(p) for p in range(n_parts)] + [part_next(p) for p in range(n_parts)] + [_conv_w_spec(kw, tc), ANY],
        out_specs=[_cur(ts, tc, 0), _conv_w_spec(kw, tc)],
        out_shape=[jax.ShapeDtypeStruct(into3.shape, BF16), jax.ShapeDtypeStruct((kw, width), F32)],
        input_output_aliases={4 + 2 * n_parts: 0},
        scratch_shapes=[pltpu.VMEM((ts, tc), F32), pltpu.VMEM((HALO, tc), F32)],
        compiler_params=_params(),
    )(proj3, proj3, proj3, *dparts, *dparts, w, into3)


def _sc_fwd(proj3, w, off, sw, into3, into_off):
    bsz, seq, _ = proj3.shape
    kw = w.shape[0]
    ts, tc = _tile(seq, 256, 8), _tile(sw, 512, LANES)
    ob, oc, oh = off // tc, (off + sw) // tc, (off + 2 * sw) // tc

    def body(b_ref, c_ref, cp_ref, h_ref, hp_ref, w_ref, into_ref, o_ref):
        prev8 = jnp.where(pl.program_id(2) == 0, 0.0, cp_ref[0] * hp_ref[0])
        o_ref[0] = (b_ref[0] * _tap_sum(w_ref[...], _windows(c_ref[0] * h_ref[0], prev8, kw))).astype(BF16)

    return pl.pallas_call(
        body, name="sc_fwd", grid=(sw // tc, bsz, seq // ts),
        in_specs=[_cur(ts, tc, ob), _cur(ts, tc, oc), _prev(ts, tc, oc), _cur(ts, tc, oh), _prev(ts, tc, oh),
                  _conv_w_spec(kw, tc), ANY],
        out_specs=_cur(ts, tc, into_off // tc), out_shape=jax.ShapeDtypeStruct(into3.shape, BF16),
        input_output_aliases={6: 0}, compiler_params=_params(),
    )(proj3, proj3, proj3, proj3, proj3, w, into3)


def _sc_bwd(proj3, dmixed3, w, off, sw, d_off):
    bsz, seq, _ = proj3.shape
    kw = w.shape[0]
    ts, tc = _tile(seq, 256, 8), _tile(sw, 512, LANES)
    n_s = seq // ts
    ob, oc, oh, od = off // tc, (off + sw) // tc, (off + 2 * sw) // tc, d_off // tc

    def body(d_ref, dn_ref, b_ref, bn_ref, c_ref, cp_ref, h_ref, hp_ref, w_ref,
             db_ref, dc_ref, dh_ref, dw_ref):
        b, s = pl.program_id(1), pl.program_id(2)

        @pl.when((b == 0) & (s == 0))
        def _():
            dw_ref[...] = jnp.zeros_like(dw_ref)

        w = w_ref[...]
        cc, hh = c_ref[0], h_ref[0]
        wins = _windows(cc * hh, jnp.where(s == 0, 0.0, cp_ref[0] * hp_ref[0]), kw)
        dout = d_ref[0]
        db_ref[0] = (dout * _tap_sum(w, wins)).astype(BF16)
        g = dout * b_ref[0]
        g_next = jnp.where(s == n_s - 1, 0.0, dn_ref[0] * bn_ref[0])
        for j in range(kw):
            dw_ref[j:j + 1, :] += jnp.sum(g * wins[j], axis=0, keepdims=True)
        dp = _tap_sum(w, [_rows_ahead(g, g_next, kw - 1 - j) for j in range(kw)])
        dc_ref[0] = (dp * hh).astype(BF16)
        dh_ref[0] = (dp * cc).astype(BF16)

    out = jax.ShapeDtypeStruct((bsz, seq, sw), BF16)
    return pl.pallas_call(
        body, name="sc_bwd", grid=(sw // tc, bsz, n_s),
        in_specs=[_cur(ts, tc, od), _next(ts, tc, od, seq), _cur(ts, tc, ob), _next(ts, tc, ob, seq),
                  _cur(ts, tc, oc), _prev(ts, tc, oc), _cur(ts, tc, oh), _prev(ts, tc, oh), _conv_w_spec(kw, tc)],
        out_specs=[_cur(ts, tc, 0)] * 3 + [_conv_w_spec(kw, tc)],
        out_shape=[out, out, out, jax.ShapeDtypeStruct((kw, sw), F32)],
        compiler_params=_params(),
    )(dmixed3, dmixed3, proj3, proj3, proj3, proj3, proj3, proj3, w)


def _tri_ones(lower):
    i = lax.broadcasted_iota(jnp.int32, (CHUNK, CHUNK), 0)
    j = lax.broadcasted_iota(jnp.int32, (CHUNK, CHUNK), 1)
    return jnp.where((i >= j) if lower else (j >= i), 1.0, 0.0).astype(F32)


def _gates_fwd(proj_ab, a_log_pad, dt_pad, heads):
    t = proj_ab.shape[0]
    gw = heads * LANES

    def body(ab_ref, al_ref, dt_ref, gc_ref, beta_ref):
        ab = ab_ref[...]
        g = -jnp.exp(al_ref[...]) * _softplus(ab + dt_ref[...])
        gc = jnp.dot(_tri_ones(True), g, precision=EXACT_SUM, preferred_element_type=F32)
        beta = _sigmoid(ab)
        for h in range(heads):
            gc_ref[:, h * LANES:(h + 1) * LANES] = jnp.broadcast_to(gc[:, h:h + 1], (CHUNK, LANES))
            beta_ref[:, h * LANES:(h + 1) * LANES] = jnp.broadcast_to(beta[:, heads + h:heads + h + 1], (CHUNK, LANES))

    return pl.pallas_call(
        body, name="gates_fwd", grid=(t // CHUNK,),
        in_specs=[_row_spec(CHUNK, LANES), _vec_spec(LANES), _vec_spec(LANES)],
        out_specs=[_row_spec(CHUNK, gw), _row_spec(CHUNK, gw)],
        out_shape=[jax.ShapeDtypeStruct((t, gw), F32)] * 2, compiler_params=_params(),
    )(proj_ab, a_log_pad, dt_pad)


def _gates_bwd(proj_ab, a_log_pad, dt_pad, dgc_b, dbeta_b, heads):
    t = proj_ab.shape[0]
    gw = heads * LANES

    def body(ab_ref, al_ref, dt_ref, dgc_ref, dbeta_ref, dab_ref, dal_ref, ddt_ref):
        i = pl.program_id(0)

        @pl.when(i == 0)
        def _():
            dal_ref[...] = jnp.zeros_like(dal_ref)
            ddt_ref[...] = jnp.zeros_like(ddt_ref)

        lane = lax.broadcasted_iota(jnp.int32, (CHUNK, LANES), 1)
        dgc = jnp.zeros((CHUNK, LANES), F32)
        dbeta = jnp.zeros((CHUNK, LANES), F32)
        for h in range(heads):
            dgc = jnp.where(lane == h, dgc_ref[:, h * LANES:(h + 1) * LANES], dgc)
            dbeta = jnp.where(lane == heads + h, dbeta_ref[:, h * LANES:(h + 1) * LANES], dbeta)
        dg = jnp.dot(_tri_ones(False), dgc, precision=EXACT_SUM, preferred_element_type=F32)
        ab = ab_ref[...]
        z = ab + dt_ref[...]
        ea = jnp.exp(al_ref[...])
        da = dg * (-ea) * _sigmoid(z)
        beta = _sigmoid(ab)
        db = dbeta * beta * (1.0 - beta)
        dab_ref[...] = jnp.where(lane < heads, da, jnp.where(lane < 2 * heads, db, 0.0)).astype(BF16)
        da_m = jnp.where(lane < heads, da, 0.0)
        ddt_ref[...] += jnp.sum(da_m, axis=0, keepdims=True)
        dal_ref[...] += jnp.sum(jnp.where(lane < heads, dg * (-ea) * _softplus(z), 0.0), axis=0, keepdims=True)

    return pl.pallas_call(
        body, name="gates_bwd", grid=(t // CHUNK,),
        in_specs=[_row_spec(CHUNK, LANES), _vec_spec(LANES), _vec_spec(LANES), _row_spec(CHUNK, gw), _row_spec(CHUNK, gw)],
        out_specs=[_row_spec(CHUNK, LANES), _vec_spec(LANES), _vec_spec(LANES)],
        out_shape=[jax.ShapeDtypeStruct((t, LANES), BF16), jax.ShapeDtypeStruct((1, LANES), F32),
                   jax.ShapeDtypeStruct((1, LANES), F32)],
        compiler_params=_params(),
    )(proj_ab, a_log_pad, dt_pad, dgc_b, dbeta_b)


def _dot(a, b, dims, hi=False):
    if hi:
        return lax.dot_general(a, b, (dims, ((), ())), precision=HI, preferred_element_type=F32)
    return lax.dot_general(a.astype(BF16), b.astype(BF16), (dims, ((), ())), preferred_element_type=F32)


NN = ((1,), (0,))
NT = ((1,), (1,))
TN = ((0,), (0,))


def _each(f, *lists):
    return [f(*xs) for xs in zip(*lists)]


def _dots(a, b, dims, hi=False):
    return _each(lambda x, y: _dot(x, y, dims, hi=hi), a, b)


def _unit_lower_inverse(ms):
    i = lax.broadcasted_iota(jnp.int32, (CHUNK, CHUNK), 0)
    j = lax.broadcasted_iota(jnp.int32, (CHUNK, CHUNK), 1)
    eye = jnp.where(i == j, 1.0, 0.0).astype(F32)
    ts = [eye - jnp.where(jnp.right_shift(i, 1) == jnp.right_shift(j, 1), m, 0.0) for m in ms]
    shift = 1
    while (1 << shift) < CHUNK:
        same_pair = jnp.right_shift(i, shift + 1) == jnp.right_shift(j, shift + 1)
        other_half = jnp.right_shift(i, shift) != jnp.right_shift(j, shift)
        offs = [jnp.where(same_pair & other_half, m, 0.0) for m in ms]
        corr = _dots(_dots(ts, offs, NN, hi=True), ts, NN, hi=True)
        ts = _each(lambda t, c: t - c, ts, corr)
        shift += 1
    return ts


def _chunk_local(qrs, krs, vs, gcbs, betabs, head_dim, solved=None):
    i = lax.broadcasted_iota(jnp.int32, (CHUNK, CHUNK), 0)
    j = lax.broadcasted_iota(jnp.int32, (CHUNK, CHUNK), 1)
    scale = head_dim ** -0.5
    rqs = [lax.rsqrt(jnp.sum(q * q, axis=-1, keepdims=True) + L2_EPS) for q in qrs]
    rks = [lax.rsqrt(jnp.sum(k * k, axis=-1, keepdims=True) + L2_EPS) for k in krs]
    qhs = _each(lambda a, r: a * r, qrs, rqs)
    ks = _each(lambda a, r: a * r, krs, rks)
    qs = [a * scale for a in qhs]
    decays = [jnp.exp(jnp.where(i >= j, g[:, 0:CHUNK] - g.T[0:CHUNK, :], -jnp.inf)) for g in gcbs]
    kks = _dots(ks, ks, NT)
    qks = _dots(qs, ks, NT)
    ms = _each(lambda b, kk, d: jnp.where(i > j, b[:, 0:CHUNK] * kk * d, 0.0), betabs, kks, decays)
    egs = [jnp.exp(g) for g in gcbs]
    rhs_ws = _each(lambda k, b, e: k * b * e, ks, betabs, egs)
    if solved is None:
        tinvs = _unit_lower_inverse(ms)
        us = _dots(tinvs, _each(lambda v, b: v * b, vs, betabs), NN, hi=True)
        ws = _dots(tinvs, rhs_ws, NN, hi=True)
    else:
        tinvs, us, ws = solved
    out = []
    for h in range(len(qrs)):
        g_last = gcbs[h][CHUNK - 1:CHUNK, :]
        e_last = jnp.exp(g_last - gcbs[h])
        out.append(dict(rq=rqs[h], rk=rks[h], qh=qhs[h], q=qs[h], k=ks[h], decay=decays[h],
                        beta_col=betabs[h][:, 0:CHUNK], kk=kks[h], m=ms[h], tinv=tinvs[h], eg=egs[h], rhs_w=rhs_ws[h],
                        u=us[h], w=ws[h], p=qks[h] * decays[h], qd=qs[h] * egs[h], kd=ks[h] * e_last, e_last=e_last,
                        gl=jnp.exp(g_last), scale=scale, strict=i > j, incl=i >= j))
    return out


def _field(dicts, name):
    return [d[name] for d in dicts]


GDN_HEAD_GROUP = 16


def _gdn_specs(n_chunks, heads, reverse):
    hg = min(GDN_HEAD_GROUP, heads)
    assert heads % hg == 0

    def cidx(c):
        return (n_chunks - 1 - c) if reverse else c

    def tok(off):
        return pl.BlockSpec((CHUNK, hg * LANES), lambda b, h, c: (b * n_chunks + cidx(c), off // hg + h))

    state = pl.BlockSpec((None, hg, LANES, LANES), lambda b, h, c: (b * n_chunks + cidx(c), h, 0, 0))
    inverse = pl.BlockSpec((None, hg, CHUNK, CHUNK), lambda b, h, c: (b * n_chunks + cidx(c), h, 0, 0))
    return hg, tok, state, inverse


def _gdn_fwd(qkv_act, gcb, betab, bsz, heads):
    t = qkv_act.shape[0]
    n_chunks = t // bsz // CHUNK
    hg, tok, state, inverse = _gdn_specs(n_chunks, heads, False)

    def body(q_ref, k_ref, v_ref, gc_ref, beta_ref, o_ref, s_ref, t_ref, u_ref, w_ref, st):
        @pl.when(pl.program_id(2) == 0)
        def _():
            st[...] = jnp.zeros_like(st)

        sls = [slice(hh * LANES, (hh + 1) * LANES) for hh in range(hg)]
        loc = _chunk_local(*[[r[:, sl] for sl in sls] for r in (q_ref, k_ref, v_ref, gc_ref, beta_ref)], LANES)
        s0 = [st[hh] for hh in range(hg)]
        v_new = _each(lambda u, ws: u - ws, _field(loc, "u"), _dots(_field(loc, "w"), s0, NN))
        o_state = _dots(_field(loc, "qd"), s0, NN)
        o_local = _dots(_field(loc, "p"), v_new, NN)
        s_add = _dots(_field(loc, "kd"), v_new, TN)
        for hh in range(hg):
            o_ref[:, sls[hh]] = o_state[hh] + o_local[hh]
            s_ref[hh] = s0[hh]
            t_ref[hh] = loc[hh]["tinv"]
            u_ref[:, sls[hh]] = loc[hh]["u"]
            w_ref[:, sls[hh]] = loc[hh]["w"]
            st[hh] = s0[hh] * loc[hh]["gl"] + s_add[hh]

    tok_shape = jax.ShapeDtypeStruct((t, heads * LANES), F32)
    return pl.pallas_call(
        body, name="gdn_fwd", grid=(bsz, heads // hg, n_chunks),
        in_specs=[tok(0), tok(heads), tok(2 * heads), tok(0), tok(0)],
        out_specs=[tok(0), state, inverse, tok(0), tok(0)],
        out_shape=[tok_shape, jax.ShapeDtypeStruct((bsz * n_chunks, heads, LANES, LANES), F32),
                   jax.ShapeDtypeStruct((bsz * n_chunks, heads, CHUNK, CHUNK), F32), tok_shape, tok_shape],
        scratch_shapes=[pltpu.VMEM((hg, LANES, LANES), F32)], compiler_params=_params(),
    )(qkv_act, qkv_act, qkv_act, gcb, betab)


def _gdn_bwd(qkv_act, gcb, betab, states, solved, d_o, bsz, heads):
    t = qkv_act.shape[0]
    n_chunks = t // bsz // CHUNK
    hg, tok, state, inverse = _gdn_specs(n_chunks, heads, True)

    def rowsum(a):
        return jnp.sum(a, axis=-1, keepdims=True)

    def finish_head(sl, L, v, betab, d_qd, d_kd, d_gl, d_p, d_m, d_rhs_u, d_rhs_w, d_q, d_k,
                    dq_ref, dk_ref, dv_ref, dgc_ref, dbeta_ref):
        k, decay = L["k"], L["decay"]
        dv_ref[:, sl] = betab * d_rhs_u
        e = d_m * L["m"] + d_p * L["p"]
        d_beta = rowsum(d_m * L["kk"] * decay) + rowsum(d_rhs_u * v) + rowsum(d_rhs_w * k * L["eg"])
        s_kd = rowsum(d_kd * L["kd"])
        d_gc = (rowsum(e) - rowsum(e.T) + rowsum(d_rhs_w * L["rhs_w"]) + rowsum(d_qd * L["qd"]) - s_kd)
        row = lax.broadcasted_iota(jnp.int32, (CHUNK, 1), 0)
        d_gc = d_gc + jnp.where(row == CHUNK - 1, jnp.sum(s_kd) + d_gl * jnp.sum(L["gl"][:, 0:1]), 0.0)
        dgc_ref[:, sl] = jnp.broadcast_to(d_gc, (CHUNK, LANES))
        dbeta_ref[:, sl] = jnp.broadcast_to(d_beta, (CHUNK, LANES))
        d_qh = d_q * L["scale"]
        dq_ref[:, sl] = L["rq"] * (d_qh - L["qh"] * rowsum(d_qh * L["qh"]))
        dk_ref[:, sl] = L["rk"] * (d_k - k * rowsum(d_k * k))

    def body(q_ref, k_ref, v_ref, gc_ref, beta_ref, s_ref, t_ref, u_ref, w_ref, do_ref,
             dq_ref, dk_ref, dv_ref, dgc_ref, dbeta_ref, dst):
        @pl.when(pl.program_id(2) == 0)
        def _():
            dst[...] = jnp.zeros_like(dst)

        sls = [slice(hh * LANES, (hh + 1) * LANES) for hh in range(hg)]
        vs = [v_ref[:, sl] for sl in sls]
        betabs = [beta_ref[:, sl] for sl in sls]
        solved = ([t_ref[hh] for hh in range(hg)], [u_ref[:, sl] for sl in sls], [w_ref[:, sl] for sl in sls])
        loc = _chunk_local([q_ref[:, sl] for sl in sls], [k_ref[:, sl] for sl in sls], vs,
                           [gc_ref[:, sl] for sl in sls], betabs, LANES, solved)
        q, k, u, w, p, tinv, decay, qd, kd, eg = (_field(loc, n) for n in
                                                  ("q", "k", "u", "w", "p", "tinv", "decay", "qd", "kd", "eg"))
        s0 = [s_ref[hh] for hh in range(hg)]
        d_out = [do_ref[:, sl] for sl in sls]
        ds1 = [dst[hh] for hh in range(hg)]
        v_new = _each(lambda a, b: a - b, u, _dots(w, s0, NN))

        d_vnew = _each(lambda a, b: a + b, _dots(p, d_out, TN), _dots(kd, ds1, NN))
        d_qd = _dots(d_out, s0, NT)
        d_kd = _dots(v_new, ds1, NT)
        d_gl = _each(lambda a, b: jnp.sum(a * b), ds1, s0)
        d_p = _each(lambda L, a: jnp.where(L["incl"], a, 0.0), loc, _dots(d_out, v_new, NT))
        d_w = [-a for a in _dots(d_vnew, s0, NT)]
        ds_out, ds_vn = _dots(qd, d_out, TN), _dots(w, d_vnew, TN)
        for hh in range(hg):
            dst[hh] = ds_out[hh] + ds1[hh] * loc[hh]["gl"] - ds_vn[hh]

        d_rhs_u = _dots(tinv, d_vnew, TN, hi=True)
        d_rhs_w = _dots(tinv, d_w, TN, hi=True)
        d_a = _each(lambda a, b: -(a + b), _dots(d_rhs_u, u, NT, hi=True), _dots(d_rhs_w, w, NT, hi=True))
        d_m = _each(lambda L, a: jnp.where(L["strict"], a, 0.0), loc, d_a)
        g_kk = _each(lambda L, a: a * L["beta_col"] * L["decay"], loc, d_m)
        h_qk = _each(lambda a, d: a * d, d_p, decay)

        d_q = _each(lambda a, e, b: a + e * b, _dots(h_qk, k, NN), eg, d_qd)
        d_k = _each(lambda a, b, c, L, bb, rw, dk: a + b + c + bb * L["eg"] * rw + L["e_last"] * dk,
                    _dots(g_kk, k, NN), _dots(g_kk, k, TN), _dots(h_qk, q, TN), loc, betabs, d_rhs_w, d_kd)
        for hh in range(hg):
            finish_head(sls[hh], loc[hh], vs[hh], betabs[hh], d_qd[hh], d_kd[hh], d_gl[hh], d_p[hh], d_m[hh],
                        d_rhs_u[hh], d_rhs_w[hh], d_q[hh], d_k[hh], dq_ref, dk_ref, dv_ref, dgc_ref, dbeta_ref)

    tok_shape = jax.ShapeDtypeStruct((t, heads * LANES), F32)
    return pl.pallas_call(
        body, name="gdn_bwd", grid=(bsz, heads // hg, n_chunks),
        in_specs=[tok(0), tok(heads), tok(2 * heads), tok(0), tok(0), state, inverse, tok(0), tok(0), tok(0)],
        out_specs=[tok(0)] * 5,
        out_shape=[tok_shape] * 5,
        scratch_shapes=[pltpu.VMEM((hg, LANES, LANES), F32)], compiler_params=_params(),
    )(qkv_act, qkv_act, qkv_act, gcb, betab, states, *solved, d_o)


def _gdn_out_fwd(o, proj, gw_norm, heads, z_off, out_width, after):
    t = o.shape[0]
    gw = heads * LANES
    ts = _tile(t, 256, 16)
    assert z_off % gw == 0
    zb = z_off // gw

    def body(o_ref, z_ref, w_ref, after_ref, out_ref):
        w = w_ref[...]
        for h in range(heads):
            sl = slice(h * LANES, (h + 1) * LANES)
            z = z_ref[:, sl]
            out_ref[:, sl] = (_rms_fwd(o_ref[:, sl], w) * (z * _sigmoid(z))).astype(BF16)

    return pl.pallas_call(
        body, name="gdn_out_fwd", grid=(t // ts,),
        in_specs=[pl.BlockSpec((ts, gw), lambda i: (i, 0)), pl.BlockSpec((ts, gw), lambda i: (i, zb)),
                  pl.BlockSpec((1, LANES), lambda i: (0, 0)), ANY],
        out_specs=pl.BlockSpec((ts, gw), lambda i: (i, 0)),
        out_shape=jax.ShapeDtypeStruct((t, out_width), BF16), compiler_params=_params(),
    )(o, proj, gw_norm, after)


def _gdn_out_bwd(dmixed, o, proj, gw_norm, heads, z_off, after):
    t = o.shape[0]
    gw = heads * LANES
    ts = _tile(t, 256, 16)
    assert z_off % gw == 0
    zb = z_off // gw

    def body(d_ref, o_ref, z_ref, w_ref, after_ref, do_ref, dz_ref, dw_ref):
        @pl.when(pl.program_id(0) == 0)
        def _():
            dw_ref[...] = jnp.zeros_like(dw_ref)

        w = w_ref[...]
        d_w_sum = jnp.zeros_like(w)
        for h in range(heads):
            sl = slice(h * LANES, (h + 1) * LANES)
            d, oo, z = d_ref[:, sl], o_ref[:, sl], z_ref[:, sl]
            dz_ref[:, sl] = (d * _rms_fwd(oo, w) * _silu_grad(z)).astype(BF16)
            d_o, d_w = _rms_bwd(d * (z * _sigmoid(z)), oo, w)
            do_ref[:, sl] = d_o
            d_w_sum = d_w_sum + d_w
        dw_ref[...] += d_w_sum

    blk = pl.BlockSpec((ts, gw), lambda i: (i, 0))
    vec = pl.BlockSpec((1, LANES), lambda i: (0, 0))
    return pl.pallas_call(
        body, name="gdn_out_bwd", grid=(t // ts,),
        in_specs=[blk, blk, pl.BlockSpec((ts, gw), lambda i: (i, zb)), vec, ANY],
        out_specs=[blk, pl.BlockSpec((ts, gw), lambda i: (i, zb)), vec],
        out_shape=[jax.ShapeDtypeStruct((t, heads * LANES), F32), jax.ShapeDtypeStruct((t, proj.shape[1]), BF16),
                   jax.ShapeDtypeStruct((1, LANES), F32)],
        compiler_params=_params(),
    )(dmixed, o, proj, gw_norm, after)


def _place_columns(into3, pieces, off):
    bsz, seq, _ = into3.shape
    n_parts, part_w = len(pieces), pieces[0].shape[-1]
    ts, tc = _tile(seq, 512, 16), _tile(part_w, 1024, LANES)
    npt = part_w // tc

    def part(p):
        def index(ci, b, s):
            use = (ci // npt) == p
            return jnp.where(use, b, 0), jnp.where(use, s, 0), jnp.where(use, ci % npt, 0)
        return pl.BlockSpec((1, ts, tc), index)

    def body(*refs):
        o_ref = refs[-1]
        for p in range(n_parts):
            @pl.when(pl.program_id(0) // npt == p)
            def _(p=p):
                o_ref[...] = refs[p][...]

    return pl.pallas_call(
        body, name="place_columns", grid=(n_parts * npt, bsz, seq // ts),
        in_specs=[part(p) for p in range(n_parts)] + [ANY],
        out_specs=_cur(ts, tc, off // tc), out_shape=jax.ShapeDtypeStruct(into3.shape, into3.dtype),
        input_output_aliases={n_parts: 0}, compiler_params=_params(),
    )(*pieces, into3)


def _place():
    x, y, c = lax.axis_index("x"), lax.axis_index("y"), lax.axis_index("c")
    return x, y, c, [(1 - x, y), (x, 1 - y), (1 - x, 1 - y)]


def _aligned(start, align):
    return start if isinstance(start, int) else pl.multiple_of(start, align)


class _Layout:
    def __init__(self, kind, shard_shape):
        self.kind = kind
        self.r, self.c = shard_shape

    def full_shape(self):
        r, c = self.r, self.c
        return {"major": (N_CHIPS, r, c), "rows": (N_CHIPS * r, c), "cols": (r, N_CHIPS * c)}[self.kind]

    def region(self, ref, j, half=None):
        r, c = self.r, self.c
        r0, nr = (0, r) if half is None else (half * (r // 2), r // 2)
        if self.kind == "major":
            return ref.at[j, pl.ds(_aligned(r0, 16), nr), :]
        if self.kind == "rows":
            return ref.at[pl.ds(_aligned(j * r + r0, 16), nr), :]
        return ref.at[pl.ds(_aligned(r0, 16), nr), pl.ds(_aligned(j * c, LANES), c)]

    def block_spec(self, tr, where):
        r, c = self.r, self.c
        if self.kind == "major":
            return pl.BlockSpec((None, tr, c), lambda *a: (where(*a)[0], where(*a)[1], 0))
        if self.kind == "rows":
            return pl.BlockSpec((tr, c), lambda *a: (where(*a)[0] * (r // tr) + where(*a)[1], 0))
        return pl.BlockSpec((tr, c), lambda *a: (where(*a)[1], where(*a)[0]))


def _remote(src, dst, send_sem, recv_sem, dev):
    return pltpu.make_async_remote_copy(src_ref=src, dst_ref=dst, send_sem=send_sem, recv_sem=recv_sem,
                                        device_id=dev, device_id_type=MESH)


def _all_gather(big, layouts, small):
    nb, ns = len(big), len(small)
    n_remote = 6 * nb + 3 * ns

    def body(*refs):
        ins, outs = refs[:nb + ns], refs[nb + ns:2 * (nb + ns)]
        send_sems, recv_sems, local_sems = refs[2 * (nb + ns):]
        x, y, c, chips = _place()
        j = 2 * x + y
        local = []
        for i in range(ns):
            local.append(pltpu.make_async_copy(ins[nb + i], outs[nb + i].at[j], local_sems.at[i]))
        for cp in local:
            cp.start()
        sends = []
        for i in range(nb):
            for p, (px, py) in enumerate(chips):
                k = 3 * i + p
                mine = layouts[i].region(outs[i], j, c)
                sends.append(_remote(mine, mine, send_sems.at[k], recv_sems.at[k], (px, py, c)))
        for i in range(ns):
            for p, (px, py) in enumerate(chips):
                k = 6 * nb + 3 * i + p
                sends.append(_remote(ins[nb + i], outs[nb + i].at[j], send_sems.at[k], recv_sems.at[k], (px, py, c)))
        for cp in sends:
            cp.start()
        for i in range(nb):
            for p, (px, py) in enumerate(chips):
                k, jp = 3 * i + p, 2 * px + py
                got = layouts[i].region(outs[i], jp, c)
                _remote(got, got, send_sems.at[k], recv_sems.at[k], (px, py, c)).wait_recv()
                fwd = _remote(got, got, send_sems.at[3 * nb + k], recv_sems.at[3 * nb + k], (x, y, 1 - c))
                fwd.start()
                sends.append(fwd)
        for i in range(ns):
            for p, (px, py) in enumerate(chips):
                k, jp = 6 * nb + 3 * i + p, 2 * px + py
                _remote(ins[nb + i], outs[nb + i].at[jp], send_sems.at[k], recv_sems.at[k], (px, py, c)).wait_recv()
        for i in range(nb):
            for p, (px, py) in enumerate(chips):
                k, jp = 3 * nb + 3 * i + p, 2 * px + py
                got = layouts[i].region(outs[i], jp, 1 - c)
                _remote(got, got, send_sems.at[k], recv_sems.at[k], (x, y, 1 - c)).wait_recv()
        for cp in sends:
            cp.wait_send()
        for cp in local:
            cp.wait()

    out_shape = [jax.ShapeDtypeStruct(lay.full_shape(), BF16) for lay in layouts]
    out_shape += [jax.ShapeDtypeStruct((N_CHIPS,) + s.shape, F32) for s in small]
    return pl.pallas_call(
        body, name="all_gather_weights", in_specs=[ANY] * (nb + ns), out_specs=[ANY] * (nb + ns), out_shape=out_shape,
        input_output_aliases={i: i for i in range(nb)},
        scratch_shapes=[pltpu.SemaphoreType.DMA((n_remote,)), pltpu.SemaphoreType.DMA((n_remote,)),
                        pltpu.SemaphoreType.DMA((ns,))],
        compiler_params=_params(),
    )(*big, *small)


HBM = pl.BlockSpec(memory_space=pltpu.HBM)
SEM = pl.BlockSpec(memory_space=pltpu.SEMAPHORE)
SPLIT_COPY = pltpu.CompilerParams(has_side_effects=pltpu.SideEffectType.DATAFLOW_SIDE_EFFECTING)


def _in_hbm(a):
    return pltpu.with_memory_space_constraint(a, pltpu.HBM)


def _split_copy_start(arrays, plan, n_copies, name, after=None):
    na = len(arrays)
    order_only = [] if after is None else [after]

    def body(*refs):
        base = na + len(order_only)
        send_sems, recv_sems = refs[base], refs[base + 1]
        thru, token = refs[base + 2:base + 2 + na], refs[base + 2 + na]
        for k, (src, dst, _, dev) in enumerate(plan(thru, *_place())):
            _remote(src, dst, send_sems.at[k], recv_sems.at[k], dev).start()
        token[...] = jnp.zeros_like(token)

    outs = pl.pallas_call(
        body, name=name, in_specs=[HBM] * na + [ANY] * len(order_only),
        out_specs=[SEM, SEM] + [HBM] * na + [pl.BlockSpec(memory_space=pltpu.VMEM)],
        out_shape=[pltpu.SemaphoreType.DMA((n_copies,)), pltpu.SemaphoreType.DMA((n_copies,))]
        + [pltpu.HBM(a.shape, a.dtype) for a in arrays] + [jax.ShapeDtypeStruct((8, LANES), F32)],
        input_output_aliases={i: 2 + i for i in range(na)}, compiler_params=SPLIT_COPY,
    )(*[_in_hbm(a) for a in arrays], *order_only)
    return (outs[0], outs[1]), list(outs[2:2 + na]), outs[2 + na]


def _split_copy_wait(sems, arrays, plan, name, after):
    na = len(arrays)

    def body(*refs):
        send_sems, recv_sems = refs[na], refs[na + 1]
        thru = refs[na + 3:]
        for k, (src, _, landing, dev) in enumerate(plan(thru, *_place())):
            cp = _remote(src, landing, send_sems.at[k], recv_sems.at[k], dev)
            cp.wait_send()
            cp.wait_recv()

    return list(pl.pallas_call(
        body, name=name, in_specs=[HBM] * na + [SEM, SEM, ANY], out_specs=[HBM] * na,
        out_shape=[pltpu.HBM(a.shape, a.dtype) for a in arrays],
        input_output_aliases={i: i for i in range(na)}, compiler_params=SPLIT_COPY,
    )(*arrays, *sems, after))


def _gather_plan(layouts):
    def plan(bufs, x, y, c, chips):
        copies = []
        for buf, lay in zip(bufs, layouts):
            mine = lay.region(buf, 2 * x + y, c)
            copies += [(mine, mine, lay.region(buf, 2 * px + py, c), (px, py, c)) for px, py in chips]
        return copies
    return plan


def _forward_plan(layouts):
    def plan(bufs, x, y, c, chips):
        copies = []
        for buf, lay in zip(bufs, layouts):
            for px, py in chips:
                got = lay.region(buf, 2 * px + py, c)
                copies.append((got, got, lay.region(buf, 2 * px + py, 1 - c), (x, y, 1 - c)))
        return copies
    return plan


def _halves_plan(layouts):
    def plan(arrays, x, y, c, chips):
        nw = len(layouts)
        copies = []
        for i, lay in enumerate(layouts):
            for j in range(N_CHIPS):
                land = arrays[nw + i].at[j]
                copies.append((lay.region(arrays[i], j, 1 - c), land, land, (x, y, 1 - c)))
        return copies
    return plan


def _partials_plan(nw):
    def plan(arrays, x, y, c, chips):
        copies = []
        for i in range(nw):
            for p, (px, py) in enumerate(chips):
                land = arrays[nw + i].at[p]
                copies.append((arrays[i].at[2 * px + py], land, land, (px, py, c)))
        return copies
    return plan


def _join_plan(nw):
    def plan(arrays, x, y, c, chips):
        return [(arrays[i], arrays[nw + i], arrays[nw + i], (x, y, 1 - c)) for i in range(nw)]
    return plan


def _forward_to_sibling(bufs, layouts, name):
    nb = len(bufs)

    def body(*refs):
        outs = refs[nb:2 * nb]
        send_sems, recv_sems = refs[2 * nb:]
        x, y, c, chips = _place()
        cps = []
        for i in range(nb):
            for p, (px, py) in enumerate(chips):
                got = layouts[i].region(outs[i], 2 * px + py, c)
                cps.append(_remote(got, got, send_sems.at[3 * i + p], recv_sems.at[3 * i + p], (x, y, 1 - c)))
        for cp in cps:
            cp.start()
        for i in range(nb):
            for p, (px, py) in enumerate(chips):
                theirs = layouts[i].region(outs[i], 2 * px + py, 1 - c)
                _remote(theirs, theirs, send_sems.at[3 * i + p], recv_sems.at[3 * i + p], (x, y, 1 - c)).wait_recv()
        for cp in cps:
            cp.wait_send()

    return pl.pallas_call(
        body, name=name, in_specs=[ANY] * nb, out_specs=[ANY] * nb,
        out_shape=[jax.ShapeDtypeStruct(b.shape, b.dtype) for b in bufs],
        input_output_aliases={i: i for i in range(nb)},
        scratch_shapes=[pltpu.SemaphoreType.DMA((3 * nb,)), pltpu.SemaphoreType.DMA((3 * nb,))],
        compiler_params=_params(),
    )(*bufs)


def _halves_to_sibling(grads, layouts, name):
    nw = len(grads)

    def body(*refs):
        ins, gots = refs[:nw], refs[nw:2 * nw]
        send_sems, recv_sems = refs[2 * nw:]
        x, y, c, _ = _place()
        cps = []
        for i in range(nw):
            for j in range(N_CHIPS):
                k = N_CHIPS * i + j
                cps.append(_remote(layouts[i].region(ins[i], j, 1 - c), gots[i].at[j],
                                   send_sems.at[k], recv_sems.at[k], (x, y, 1 - c)))
        for cp in cps:
            cp.start()
        for cp in cps:
            cp.wait()

    half = [jax.ShapeDtypeStruct((N_CHIPS, lay.r // 2, lay.c), BF16) for lay in layouts]
    return pl.pallas_call(
        body, name=name, in_specs=[ANY] * nw, out_specs=[ANY] * nw, out_shape=half,
        scratch_shapes=[pltpu.SemaphoreType.DMA((N_CHIPS * nw,)), pltpu.SemaphoreType.DMA((N_CHIPS * nw,))],
        compiler_params=_params(),
    )(*grads)


def _chip_sum(grad, got, layout, core, name):
    n, hr, c = got.shape
    tr = _tile(hr, 256, 16)
    nb = hr // tr

    def body(core_ref, a_ref, b_ref, o_ref):
        o_ref[...] = (a_ref[...].astype(F32) + b_ref[...].astype(F32)).astype(BF16)

    spec = pl.BlockSpec((None, tr, c), lambda j, i, core_ref: (j, i, 0))
    return pl.pallas_call(
        body, name=name,
        grid_spec=pltpu.PrefetchScalarGridSpec(
            num_scalar_prefetch=1, grid=(n, nb),
            in_specs=[layout.block_spec(tr, lambda j, i, core_ref: (j, core_ref[0] * nb + i)), spec],
            out_specs=spec),
        out_shape=jax.ShapeDtypeStruct((n, hr, c), BF16), compiler_params=_params(),
    )(core, grad, got)


def _halves_start(grads, layouts, name, after=None):
    lands = [lax.empty((N_CHIPS, lay.r // 2, lay.c), BF16) for lay in layouts]
    return _split_copy_start(list(grads) + lands, _halves_plan(layouts), N_CHIPS * len(grads), name, after)


def _partials_start(parts, name, after=None):
    lands = [lax.empty((3,) + p.shape[1:], BF16) for p in parts]
    return _split_copy_start(list(parts) + lands, _partials_plan(len(parts)), 3 * len(parts), name, after)


def _shard_sum(parts, got, chip, name):
    _, r, c = parts.shape
    tr = _tile(r, 256, 16)

    def body(chip_ref, o_ref, g_ref, out_ref):
        acc = o_ref[...].astype(F32)
        for p in range(3):
            acc = acc + g_ref[p].astype(F32)
        out_ref[...] = acc

    return pl.pallas_call(
        body, name=name,
        grid_spec=pltpu.PrefetchScalarGridSpec(
            num_scalar_prefetch=1, grid=(r // tr,),
            in_specs=[pl.BlockSpec((None, tr, c), lambda i, chip_ref: (chip_ref[0], i, 0)),
                      pl.BlockSpec((3, tr, c), lambda i, chip_ref: (0, i, 0))],
            out_specs=pl.BlockSpec((tr, c), lambda i, chip_ref: (i, 0))),
        out_shape=jax.ShapeDtypeStruct((r, c), F32), compiler_params=_params(),
    )(chip, parts, got)


def _join_start(halves, name):
    lands = [lax.empty(h.shape, F32) for h in halves]
    return _split_copy_start(list(halves) + lands, _join_plan(len(halves)), len(halves), name)


def _adamw_rows(w, g_half, m, v, half, filled, after, name):
    r, c = w.shape
    hr = r // 2
    tr = _tile(hr, 128, 8)
    nb = hr // tr
    c1 = 1.0 - ADAM_B1 ** ADAM_STEP
    c2 = 1.0 - ADAM_B2 ** ADAM_STEP
    n_prev = 0 if filled is None else 4

    def body(half_ref, w_ref, gin_ref, m_ref, v_ref, *rest):
        g_ref, d_ref, nm_ref, nv_ref = rest[1 + n_prev:]
        gg = gin_ref[...]
        nm = ADAM_B1 * m_ref[...] + (1.0 - ADAM_B1) * gg
        nv = ADAM_B2 * v_ref[...] + (1.0 - ADAM_B2) * jnp.square(gg)
        m_hat = nm / c1
        v_hat = nv / c2
        g_ref[...] = gg
        d_ref[...] = -ADAM_LR * (m_hat / (jnp.sqrt(v_hat) + ADAM_EPS) + ADAM_WD * w_ref[...])
        nm_ref[...] = nm
        nv_ref[...] = nv

    full = pl.BlockSpec((tr, c), lambda i, half_ref: (half_ref[0] * nb + i, 0))
    part = pl.BlockSpec((tr, c), lambda i, half_ref: (i, 0))
    return pl.pallas_call(
        body, name=name,
        grid_spec=pltpu.PrefetchScalarGridSpec(
            num_scalar_prefetch=1, grid=(nb,), in_specs=[full, part, full, full] + [ANY] * (1 + n_prev),
            out_specs=[full] * 4),
        out_shape=[jax.ShapeDtypeStruct((r, c), F32)] * 4,
        input_output_aliases={6 + k: k for k in range(n_prev)}, compiler_params=_params(),
    )(half, w, g_half, m, v, after, *([] if filled is None else filled))


def _small_all_reduce(buf):
    rows = buf.shape[0]
    n_dev = 8

    def body(b_ref, o_ref, gath, send_sems, recv_sems):
        x, y, c, _ = _place()
        me = 4 * x + 2 * y + c
        gath[me] = b_ref[...]
        cps = []
        for k in range(1, n_dev):
            px, py, pc = (x + (k >> 2)) % 2, (y + ((k >> 1) & 1)) % 2, (c + (k & 1)) % 2
            cps.append(_remote(b_ref, gath.at[me], send_sems.at[k - 1], recv_sems.at[k - 1], (px, py, pc)))
        for cp in cps:
            cp.start()
        for k in range(1, n_dev):
            px, py, pc = (x + (k >> 2)) % 2, (y + ((k >> 1) & 1)) % 2, (c + (k & 1)) % 2
            _remote(b_ref, gath.at[4 * px + 2 * py + pc], send_sems.at[k - 1], recv_sems.at[k - 1], (px, py, pc)).wait_recv()
        for cp in cps:
            cp.wait_send()
        acc = gath[0]
        for dev in range(1, n_dev):
            acc = acc + gath[dev]
        o_ref[...] = acc

    vm = pl.BlockSpec(memory_space=pltpu.VMEM)
    return pl.pallas_call(
        body, name="small_all_reduce", in_specs=[vm], out_specs=vm,
        out_shape=jax.ShapeDtypeStruct((rows, LANES), F32),
        scratch_shapes=[pltpu.VMEM((n_dev, rows, LANES), F32), pltpu.SemaphoreType.DMA((n_dev - 1,)),
                        pltpu.SemaphoreType.DMA((n_dev - 1,))],
        compiler_params=_params(),
    )(buf)


def _pad_lanes(v):
    return jnp.pad(v, ((0, 0), (0, LANES - v.shape[-1])))


def _pack(vectors):
    flat, offs, pos = [], [], 0
    for v in vectors:
        n = v.size
        n_pad = -(-n // LANES) * LANES
        flat.append(jnp.pad(v.reshape(-1), (0, n_pad - n)))
        offs.append((pos, n, v.shape))
        pos += n_pad
    total = -(-pos // (8 * LANES)) * 8 * LANES
    flat.append(jnp.zeros((total - pos,), F32))
    return jnp.concatenate(flat).reshape(-1, LANES), offs


def _unpack(buf, offs):
    flat = buf.reshape(-1)
    return [flat[pos:pos + n].reshape(shape) for pos, n, shape in offs]


def kernel(x, norm_mix_pre, w_in, conv_qkv_w, a_log, dt_bias, gdn_norm_w, conv_sc_w, w_out, norm_mix_post, norm_mlp_pre, w_up, w_down, norm_mlp_post, loss_target, m_norm_mix_pre, m_w_in, m_conv_qkv_w, m_a_log, m_dt_bias, m_gdn_norm_w, m_conv_sc_w, m_w_out, m_norm_mix_post, m_norm_mlp_pre, m_w_up, m_w_down, m_norm_mlp_post, v_norm_mix_pre, v_w_in, v_conv_qkv_w, v_a_log, v_dt_bias, v_gdn_norm_w, v_conv_sc_w, v_w_out, v_norm_mix_post, v_norm_mlp_pre, v_w_up, v_w_down, v_norm_mlp_post):
    bsz, seq, d = x.shape
    t = bsz * seq
    heads, head_dim = a_log.shape[-1], gdn_norm_w.shape[-1]
    assert head_dim == LANES and seq % CHUNK == 0
    gw = heads * head_dim
    sw = conv_sc_w.shape[-1] * N_CHIPS
    ics = w_in.shape[-1]
    main = 4 * gw + 3 * sw
    assert ics * N_CHIPS == main + 2 * heads and 2 * heads <= LANES

    lay_in = _Layout("major", w_in.shape[1:])
    lay_out = _Layout("rows", w_out.shape[1:])
    lay_up = _Layout("cols", w_up.shape[1:])
    lay_down = _Layout("rows", w_down.shape[1:])
    layouts = [lay_in, lay_out, lay_up, lay_down]
    chip = (2 * lax.axis_index("x") + lax.axis_index("y")).astype(jnp.int32).reshape(1)
    core = lax.axis_index("c").astype(jnp.int32).reshape(1)
    x2 = x.reshape(t, d)
    cq_g, cs_g = _all_gather([], [], [conv_qkv_w[0], conv_sc_w[0]])
    plan_in, plan_ou, plan_down = _gather_plan(layouts[:1]), _gather_plan(layouts[1:3]), _gather_plan(layouts[3:])
    in_buf = _cast_into_layout(w_in[0], lay_in, chip, "cast_w_in")
    in_sems, in_bufs, in_token = _split_copy_start([in_buf], plan_in, 3, "gather_in_start", after=cq_g)
    shards = [_cast_into_layout(w[0], lay, chip, f"cast_{n}", after=in_token)
              for w, lay, n in zip((w_out, w_up, w_down), layouts[1:], ("w_out", "w_up", "w_down"))]
    xn = _norm_fwd(x2, norm_mix_pre, shards[-1])
    in_bufs = _split_copy_wait(in_sems, in_bufs, plan_in, "gather_in_wait", xn)
    win_sh, = _forward_to_sibling(in_bufs, layouts[:1], "forward_w_in")
    ou_sems, ou_bufs, ou_token = _split_copy_start(shards[:2], plan_ou, 6, "gather_out_up_start", after=win_sh)
    w_main, w_ab = _repack_w_in(win_sh, gw, heads, sw, ou_token)
    conv_q = cq_g.transpose(1, 0, 2).reshape(conv_qkv_w.shape[1], -1)
    conv_s = cs_g.transpose(1, 0, 2).reshape(conv_sc_w.shape[1], -1)

    tgt2 = loss_target.reshape(t, d)
    proj = _matmul(xn, w_main, "nn", [F32], "proj_main")
    proj_ab = _matmul(xn, w_ab, "nn", [F32], "proj_ab")
    proj3 = proj.reshape(bsz, seq, main)
    qkv_act = _qkv_conv_fwd(proj3, conv_q, 3 * gw).reshape(t, 3 * gw)
    a_log_pad, dt_pad = _pad_lanes(a_log), _pad_lanes(dt_bias)
    gcb, betab = _gates_fwd(proj_ab, a_log_pad, dt_pad, heads)
    o_raw, states, *gdn_solved = _gdn_fwd(qkv_act, gcb, betab, bsz, heads)
    ou_bufs = _split_copy_wait(ou_sems, ou_bufs, plan_ou, "gather_out_up_wait", o_raw)
    fwd_plan = _forward_plan(layouts[1:3])
    fwd_sems, ou_bufs, fwd_token = _split_copy_start(ou_bufs, fwd_plan, 6, "forward_out_up_start")
    down_sems, down_bufs, down_token = _split_copy_start(shards[2:], plan_down, 3, "gather_down_start", after=fwd_token)
    gdn_out = _gdn_out_fwd(o_raw, proj, gdn_norm_w, heads, 3 * gw, gw + sw, down_token)
    mixed = _sc_fwd(proj3, conv_s, 4 * gw, sw, gdn_out.reshape(bsz, seq, gw + sw), gw).reshape(t, gw + sw)
    wout_f, wup_f = _split_copy_wait(fwd_sems, ou_bufs, fwd_plan, "forward_out_up_wait", mixed)
    mix = _matmul(mixed, wout_f, "nn", [F32], "mix_out")
    h, hn = _mid_fwd(x2, mix, norm_mix_post, norm_mlp_pre)

    def up_epilogue(acc):
        r = jnp.maximum(acc, 0.0)
        return r, r * r

    relu_up, hid = _matmul(hn, wup_f, "nn", [BF16, BF16], "mlp_up", epilogue=up_epilogue)
    down_bufs = _split_copy_wait(down_sems, down_bufs, plan_down, "gather_down_wait", hid)
    (wdown_f,) = _forward_to_sibling(down_bufs, layouts[3:], "forward_w_down")
    ff = _matmul(hid, wdown_f, "nn", [F32], "mlp_down")
    loss_blk, dy, dff, dg_mlp_post = _head_fwd_bwd(h, ff, tgt2, norm_mlp_post)

    def dup_epilogue(acc, r):
        return (acc * (2.0 * r.astype(F32)),)

    d_up = _matmul(dff, wdown_f, "nt", [BF16], "d_hid", epilogue=dup_epilogue, extras=(relu_up,))
    dw_down = _matmul(hid, dff, "tn", [BF16], "dw_down")
    plan_h_down, plan_h_up, plan_h_in = _halves_plan([lay_down]), _halves_plan([lay_up]), _halves_plan([lay_in])
    hd_sems, hd_arrays, hd_token = _halves_start([dw_down], [lay_down], "down_halves_start")
    d_hn = _matmul(d_up, wup_f, "nt", [F32], "d_hn", after=hd_token)
    dw_up = _matmul(hn, d_up, "tn", [BF16], "dw_up")
    dw_down, down_got = _split_copy_wait(hd_sems, hd_arrays, plan_h_down, "down_halves_wait", dw_up)
    hu_sems, hu_arrays, hu_token = _halves_start([dw_up], [lay_up], "up_halves_start", after=down_got)
    down_part = _chip_sum(dw_down, down_got, lay_down, core, "chip_sum_w_down")
    pd_sems, pd_arrays, pd_token = _partials_start([down_part], "down_partials_start", after=hu_token)
    dh, dmix, dg_mlp_pre, dg_mix_post = _mid_bwd(d_hn, h, norm_mlp_pre, dy, mix, norm_mix_post, pd_token)
    dmixed = _matmul(dmix, wout_f, "nt", [F32], "d_mixed")
    dw_out = _matmul(mixed, dmix, "tn", [BF16], "dw_out")
    dw_up, up_got = _split_copy_wait(hu_sems, hu_arrays, plan_h_up, "up_halves_wait", dw_out)
    up_part = _chip_sum(dw_up, up_got, lay_up, core, "chip_sum_w_up")
    out_got, = _halves_to_sibling([dw_out], [lay_out], "out_grad_halves_to_sibling")
    out_part = _chip_sum(dw_out, out_got, lay_out, core, "chip_sum_w_out")
    puo_sems, puo_arrays, puo_token = _partials_start([up_part, out_part], "up_out_partials_start", after=dmixed)
    dmixed3 = dmixed.reshape(bsz, seq, d)
    d_b, d_c, d_hsc, dw_conv_s = _sc_bwd(proj3, dmixed3, conv_s, 4 * gw, sw, gw)
    d_o, d_z, dg_gdn_norm = _gdn_out_bwd(dmixed, o_raw, proj, gdn_norm_w, heads, 3 * gw, puo_token)
    dq, dk, dv, dgc_b, dbeta_b = _gdn_bwd(qkv_act, gcb, betab, states, gdn_solved, d_o, bsz, heads)
    d_ab, d_alog, d_dt = _gates_bwd(proj_ab, a_log_pad, dt_pad, dgc_b, dbeta_b, heads)
    d_proj3, dw_conv_q = _qkv_conv_bwd(proj3, [a.reshape(bsz, seq, gw) for a in (dq, dk, dv)], conv_q, 3 * gw,
                                       d_z.reshape(bsz, seq, main))
    d_proj = _place_columns(d_proj3, [d_b, d_c, d_hsc], 4 * gw).reshape(t, main)
    dw_main = _matmul(xn, d_proj, "tn", [BF16], "dw_in_main")
    dw_ab = _matmul(xn, d_ab, "tn", [BF16], "dw_in_ab")
    dw_in = _unpack_dw_in(dw_main, dw_ab, gw, heads, sw, ics)
    hi_sems, hi_arrays, hi_token = _halves_start([dw_in], [lay_in], "in_halves_start")
    d_xn = _matmul(d_proj, w_main, "nt", [F32], "d_xn_main", after=hi_token)
    dw_in, in_got = _split_copy_wait(hi_sems, hi_arrays, plan_h_in, "in_halves_wait", d_xn)
    in_part = _chip_sum(dw_in, in_got, lay_in, core, "chip_sum_w_in")
    pi_sems, pi_arrays, pi_token = _partials_start([in_part], "in_partials_start")
    d_xn_ab = _matmul(d_ab, w_ab, "nt", [F32], "d_xn_ab", after=pi_token)
    grad_x, dg_mix_pre = _first_bwd(d_xn, d_xn_ab, x2, norm_mix_pre, dh)

    other_core = 1 - core

    def finish(parts, recvs, weights3, names, tag):
        nw = len(parts)
        halves = [_shard_sum(p, r, chip, f"shard_sum_{n}") for p, r, n in zip(parts, recvs, names)]
        sems, arrays, token = _join_start(halves, f"{tag}_join_start")
        own = [_adamw_rows(wt[0], h, m[0], v[0], core, None, token, f"adamw_own_{n}")
               for (wt, m, v), h, n in zip(weights3, arrays[:nw], names)]
        theirs = _split_copy_wait(sems, arrays, _join_plan(nw), f"{tag}_join_wait", own[-1][1])[nw:]
        return [_adamw_rows(wt[0], h, m[0], v[0], other_core, o, h, f"adamw_sibling_{n}")
                for (wt, m, v), h, o, n in zip(weights3, theirs, own, names)]

    down_part, down_recv = _split_copy_wait(pd_sems, pd_arrays, _partials_plan(1), "down_partials_wait", grad_x)
    up_part, out_part, up_recv, out_recv = _split_copy_wait(puo_sems, puo_arrays, _partials_plan(2),
                                                            "up_out_partials_wait", down_recv)
    res_up, res_down = finish([up_part, down_part], [up_recv, down_recv],
                              [(w_up, m_w_up, v_w_up), (w_down, m_w_down, v_w_down)], ("w_up", "w_down"), "mlp")
    in_part, in_recv = _split_copy_wait(pi_sems, pi_arrays, _partials_plan(1), "in_partials_wait", res_down[1])
    res_in, res_out = finish([in_part, out_part], [in_recv, out_recv],
                             [(w_in, m_w_in, v_w_in), (w_out, m_w_out, v_w_out)], ("w_in", "w_out"), "mix")

    small, offs = _pack([loss_blk[0:1, 0:1], dg_mix_pre, dw_conv_q, d_alog[:, :heads], d_dt[:, :heads], dg_gdn_norm,
                         dw_conv_s, dg_mix_post, dg_mlp_pre, dg_mlp_post])
    (loss, g_mix_pre, g_conv_q_full, g_alog, g_dt, g_gdn_norm, g_conv_s_full, g_mix_post, g_mlp_pre,
     g_mlp_post) = _unpack(_small_all_reduce(small), offs)
    j = 2 * lax.axis_index("x") + lax.axis_index("y")
    cq_w, cs_w = conv_qkv_w.shape[-1], conv_sc_w.shape[-1]
    g_conv_q = lax.dynamic_slice_in_dim(g_conv_q_full, j * cq_w, cq_w, axis=1)
    g_conv_s = lax.dynamic_slice_in_dim(g_conv_s_full, j * cs_w, cs_w, axis=1)

    big = {1: res_in, 7: res_out, 10: res_up, 11: res_down}
    grads = [g_mix_pre, None, g_conv_q, g_alog, g_dt, g_gdn_norm, g_conv_s, None, g_mix_post, g_mlp_pre, None,
             None, g_mlp_post]
    weights = [norm_mix_pre, w_in, conv_qkv_w, a_log, dt_bias, gdn_norm_w, conv_sc_w, w_out, norm_mix_post,
               norm_mlp_pre, w_up, w_down, norm_mlp_post]
    ms = [m_norm_mix_pre, m_w_in, m_conv_qkv_w, m_a_log, m_dt_bias, m_gdn_norm_w, m_conv_sc_w, m_w_out,
          m_norm_mix_post, m_norm_mlp_pre, m_w_up, m_w_down, m_norm_mlp_post]
    vs = [v_norm_mix_pre, v_w_in, v_conv_qkv_w, v_a_log, v_dt_bias, v_gdn_norm_w, v_conv_sc_w, v_w_out,
          v_norm_mix_post, v_norm_mlp_pre, v_w_up, v_w_down, v_norm_mlp_post]
    out_g, out_d, out_m, out_v = [], [], [], []
    for i, (wt, g, m, v) in enumerate(zip(weights, grads, ms, vs)):
        shape2 = wt.shape[-2:] if wt.ndim == 3 else wt.shape
        if i in big:
            g2, dl, nm, nv = big[i]
        else:
            g2 = g.reshape(shape2)
            dl, nm, nv = _adamw(wt.reshape(shape2), g2, m.reshape(shape2), v.reshape(shape2), f"adamw_{i}")
        out_g.append(g2.reshape(wt.shape))
        out_d.append(dl.reshape(wt.shape))
        out_m.append(nm.reshape(wt.shape))
        out_v.append(nv.reshape(wt.shape))

    return (loss.reshape(()), grad_x.reshape(bsz, seq, d), *out_g, *out_d, *out_m, *out_v)
```

```python
import functools

import jax
import jax.numpy as jnp
from jax import lax
from jax.experimental import pallas as pl
from jax.experimental.pallas import tpu as pltpu

CHUNK = 64
NORM_EPS = 1e-6
L2_EPS = 1e-6
N_CHIPS = 4
ADAM_LR = 0.001
ADAM_B1 = 0.9
ADAM_B2 = 0.999
ADAM_EPS = 1e-08
ADAM_WD = 0.01
ADAM_STEP = 10
LANES = 128
VMEM_LIMIT = 56 * 1024 * 1024

F32 = jnp.float32
BF16 = jnp.bfloat16
HI = lax.Precision.HIGH
EXACT_SUM = lax.Precision.HIGHEST
MESH = pl.DeviceIdType.MESH
ANY = pl.BlockSpec(memory_space=pl.ANY)


def _params(n_grid=0):
    return pltpu.CompilerParams(vmem_limit_bytes=VMEM_LIMIT)


def _tile(n, pref, align):
    if n <= pref:
        return n
    t = (pref // align) * align
    while t >= align:
        if n % t == 0:
            return t
        t -= align
    raise ValueError(f"no tile for {n}")


def _sigmoid(x):
    return 1.0 / (1.0 + jnp.exp(-x))


def _softplus(x):
    return jnp.maximum(x, 0.0) + jnp.log(1.0 + jnp.exp(-jnp.abs(x)))


def _rms_fwd(x, g):
    r = lax.rsqrt(jnp.mean(x * x, axis=-1, keepdims=True) + NORM_EPS)
    return x * r * g


def _rms_bwd(dy, x, g):
    r = lax.rsqrt(jnp.mean(x * x, axis=-1, keepdims=True) + NORM_EPS)
    xh = x * r
    dxh = dy * g
    dx = r * (dxh - xh * jnp.mean(dxh * xh, axis=-1, keepdims=True))
    dg = jnp.sum(dy * xh, axis=0, keepdims=True)
    return dx, dg


def _matmul(a, b, form, out_dtypes, name, epilogue=None, extras=(), after=None, tm=1024, tn=1024, tk=4096):
    if form == "nn":
        (m, kd), (_, n) = a.shape, b.shape
        dims = (((1,), (0,)), ((), ()))
    elif form == "nt":
        (m, kd), (n, _) = a.shape, b.shape
        dims = (((1,), (1,)), ((), ()))
    else:
        (kd, m), (_, n) = a.shape, b.shape
        dims = (((0,), (0,)), ((), ()))
    tm, tn, tk = _tile(m, tm, LANES), _tile(n, tn, LANES), _tile(kd, tk, LANES)
    nk = kd // tk
    n_extra = len(extras)
    n_out = len(out_dtypes)

    if form == "nn":
        a_spec = pl.BlockSpec((tm, tk), lambda i, j, k: (i, k))
        b_spec = pl.BlockSpec((tk, tn), lambda i, j, k: (k, j))
    elif form == "nt":
        a_spec = pl.BlockSpec((tm, tk), lambda i, j, k: (i, k))
        b_spec = pl.BlockSpec((tn, tk), lambda i, j, k: (j, k))
    else:
        a_spec = pl.BlockSpec((tk, tm), lambda i, j, k: (k, i))
        b_spec = pl.BlockSpec((tk, tn), lambda i, j, k: (k, j))
    tile_spec = pl.BlockSpec((tm, tn), lambda i, j, k: (i, j))

    order_only = [] if after is None else [after]
    n_skip = n_extra + len(order_only)

    def finish(acc, extra_refs, out_refs):
        outs = (acc,) if epilogue is None else epilogue(acc, *[e[...] for e in extra_refs])
        for o_ref, val in zip(out_refs, outs):
            o_ref[...] = val.astype(o_ref.dtype)

    def body(a_ref, b_ref, *rest):
        extra_refs = rest[:n_extra]
        out_refs = rest[n_skip:n_skip + n_out]
        if nk == 1:
            finish(lax.dot_general(a_ref[...], b_ref[...], dims, preferred_element_type=F32), extra_refs, out_refs)
            return
        acc_ref = rest[-1]
        k = pl.program_id(2)

        @pl.when(k == 0)
        def _():
            acc_ref[...] = jnp.zeros_like(acc_ref)

        acc_ref[...] += lax.dot_general(a_ref[...], b_ref[...], dims, preferred_element_type=F32)

        @pl.when(k == nk - 1)
        def _():
            finish(acc_ref[...], extra_refs, out_refs)

    outs = pl.pallas_call(
        body, name=name, grid=(m // tm, n // tn, nk),
        in_specs=[a_spec, b_spec] + [tile_spec] * n_extra + [ANY] * len(order_only),
        out_specs=[tile_spec] * n_out,
        out_shape=[jax.ShapeDtypeStruct((m, n), dt) for dt in out_dtypes],
        scratch_shapes=[pltpu.VMEM((tm, tn), F32)] if nk > 1 else [],
        compiler_params=_params(),
    )(a, b, *extras, *order_only)
    return outs[0] if n_out == 1 else outs


def _cast_into_layout(w, layout, chip, name, after=None):
    r, c = w.shape
    tr = _tile(r, 256, 16)
    order_only = [] if after is None else [after]

    def body(chip_ref, w_ref, *rest):
        rest[-1][...] = w_ref[...].astype(BF16)

    return pl.pallas_call(
        body, name=name,
        grid_spec=pltpu.PrefetchScalarGridSpec(
            num_scalar_prefetch=1, grid=(r // tr,),
            in_specs=[pl.BlockSpec((tr, c), lambda i, chip_ref: (i, 0))] + [ANY] * len(order_only),
            out_specs=layout.block_spec(tr, lambda i, chip_ref: (chip_ref[0], i))),
        out_shape=jax.ShapeDtypeStruct(layout.full_shape(), BF16), compiler_params=_params(),
    )(chip, w, *order_only)


def _in_segments(gw, heads, sw):
    main = 4 * gw
    return [(0, main, 0), (main + 2 * heads, 3 * sw, main), (main, 2 * heads, main + 3 * sw)]


def _pieces(seg_start, width, dst_start, ics):
    out = []
    g = seg_start
    while g < seg_start + width:
        j, cj = divmod(g, ics)
        wdt = min(ics - cj, seg_start + width - g)
        out.append((j, cj, dst_start + (g - seg_start), wdt))
        g += wdt
    return out


def _repack_w_in(w_sh, gw, heads, sw, after):
    ns, d, ics = w_sh.shape
    main = 4 * gw + 3 * sw
    tr = _tile(d, 128, 16)
    pieces = [p for seg in _in_segments(gw, heads, sw) for p in _pieces(*seg, ics)]

    def body(w_ref, after_ref, m_ref, ab_ref):
        ab_ref[...] = jnp.zeros_like(ab_ref)
        for j, cj, cd, wdt in pieces:
            if cd >= main:
                ab_ref[:, cd - main:cd - main + wdt] = w_ref[j, :, cj:cj + wdt]
            else:
                m_ref[:, cd:cd + wdt] = w_ref[j, :, cj:cj + wdt]

    return pl.pallas_call(
        body, name="repack_w_in", grid=(d // tr,),
        in_specs=[pl.BlockSpec((ns, tr, ics), lambda i: (0, i, 0)), ANY],
        out_specs=[pl.BlockSpec((tr, main), lambda i: (i, 0)), pl.BlockSpec((tr, LANES), lambda i: (i, 0))],
        out_shape=[jax.ShapeDtypeStruct((d, main), BF16), jax.ShapeDtypeStruct((d, LANES), BF16)],
        compiler_params=_params(),
    )(w_sh, after)


def _unpack_dw_in(dw_main, dw_ab, gw, heads, sw, ics):
    d = dw_main.shape[0]
    tr = _tile(d, 128, 16)
    main = 4 * gw + 3 * sw
    pieces = [p for seg in _in_segments(gw, heads, sw) for p in _pieces(*seg, ics)]

    def body(m_ref, ab_ref, o_ref):
        for j, cj, cd, wdt in pieces:
            if cd >= main:
                o_ref[j, :, cj:cj + wdt] = ab_ref[:, cd - main:cd - main + wdt].astype(BF16)
            else:
                o_ref[j, :, cj:cj + wdt] = m_ref[:, cd:cd + wdt].astype(BF16)

    return pl.pallas_call(
        body, name="unpack_dw_in", grid=(d // tr,),
        in_specs=[pl.BlockSpec((tr, main), lambda i: (i, 0)), pl.BlockSpec((tr, LANES), lambda i: (i, 0))],
        out_specs=pl.BlockSpec((N_CHIPS, tr, ics), lambda i: (0, i, 0)),
        out_shape=jax.ShapeDtypeStruct((N_CHIPS, d, ics), BF16), compiler_params=_params(),
    )(dw_main, dw_ab)


def _adamw(w, g, m, v, name):
    r, c = w.shape
    tr = _tile(r, 128, 8)
    c1 = 1.0 - ADAM_B1 ** ADAM_STEP
    c2 = 1.0 - ADAM_B2 ** ADAM_STEP

    def body(w_ref, g_ref, m_ref, v_ref, d_ref, nm_ref, nv_ref):
        gg = g_ref[...]
        nm = ADAM_B1 * m_ref[...] + (1.0 - ADAM_B1) * gg
        nv = ADAM_B2 * v_ref[...] + (1.0 - ADAM_B2) * jnp.square(gg)
        m_hat = nm / c1
        v_hat = nv / c2
        d_ref[...] = -ADAM_LR * (m_hat / (jnp.sqrt(v_hat) + ADAM_EPS) + ADAM_WD * w_ref[...])
        nm_ref[...] = nm
        nv_ref[...] = nv

    spec = pl.BlockSpec((tr, c), lambda i: (i, 0))
    return pl.pallas_call(
        body, name=name, grid=(r // tr,), in_specs=[spec] * 4, out_specs=[spec] * 3,
        out_shape=[jax.ShapeDtypeStruct((r, c), F32)] * 3, compiler_params=_params(),
    )(w, g, m, v)


def _row_spec(tt, d):
    return pl.BlockSpec((tt, d), lambda i: (i, 0))


def _vec_spec(d):
    return pl.BlockSpec((1, d), lambda i: (0, 0))


def _norm_fwd(x, g, after):
    t, d = x.shape
    tt = _tile(t, 256, 16)

    def body(x_ref, g_ref, after_ref, o_ref):
        o_ref[...] = _rms_fwd(x_ref[...], g_ref[...]).astype(BF16)

    return pl.pallas_call(
        body, name="norm_mix_pre", grid=(t // tt,), in_specs=[_row_spec(tt, d), _vec_spec(d), ANY],
        out_specs=_row_spec(tt, d), out_shape=jax.ShapeDtypeStruct((t, d), BF16), compiler_params=_params(),
    )(x, g, after)


def _mid_fwd(x, mix, g_post, g_pre):
    t, d = x.shape
    tt = _tile(t, 128, 16)

    def body(x_ref, mix_ref, gp_ref, gn_ref, h_ref, hn_ref):
        h = x_ref[...] + _rms_fwd(mix_ref[...], gp_ref[...])
        h_ref[...] = h
        hn_ref[...] = _rms_fwd(h, gn_ref[...]).astype(BF16)

    return pl.pallas_call(
        body, name="mid_fwd", grid=(t // tt,),
        in_specs=[_row_spec(tt, d), _row_spec(tt, d), _vec_spec(d), _vec_spec(d)],
        out_specs=[_row_spec(tt, d), _row_spec(tt, d)],
        out_shape=[jax.ShapeDtypeStruct((t, d), F32), jax.ShapeDtypeStruct((t, d), BF16)],
        compiler_params=_params(),
    )(x, mix, g_post, g_pre)


def _head_fwd_bwd(h, ff, tgt, g_post):
    t, d = h.shape
    tt = _tile(t, 128, 16)

    def body(h_ref, ff_ref, t_ref, g_ref, loss_ref, dy_ref, dff_ref, dg_ref):
        i = pl.program_id(0)

        @pl.when(i == 0)
        def _():
            loss_ref[...] = jnp.zeros_like(loss_ref)
            dg_ref[...] = jnp.zeros_like(dg_ref)

        ff = ff_ref[...]
        g = g_ref[...]
        e = h_ref[...] + _rms_fwd(ff, g) - t_ref[...]
        loss_ref[...] += 0.5 * jnp.sum(jnp.mean(e * e, axis=-1, keepdims=True))
        dy = e * (1.0 / d)
        dy_ref[...] = dy
        dff, dg = _rms_bwd(dy, ff, g)
        dff_ref[...] = dff.astype(BF16)
        dg_ref[...] += dg

    return pl.pallas_call(
        body, name="loss_head", grid=(t // tt,),
        in_specs=[_row_spec(tt, d)] * 3 + [_vec_spec(d)],
        out_specs=[pl.BlockSpec((8, LANES), lambda i: (0, 0)), _row_spec(tt, d), _row_spec(tt, d), _vec_spec(d)],
        out_shape=[jax.ShapeDtypeStruct((8, LANES), F32), jax.ShapeDtypeStruct((t, d), F32),
                   jax.ShapeDtypeStruct((t, d), BF16), jax.ShapeDtypeStruct((1, d), F32)],
        compiler_params=_params(),
    )(h, ff, tgt, g_post)


def _mid_bwd(d_hn, h, g_pre, dy, mix, g_post, after):
    t, d = h.shape
    tt = _tile(t, 128, 16)

    def body(dhn_ref, h_ref, gn_ref, dy_ref, mix_ref, gp_ref, after_ref, dh_ref, dmix_ref, dgn_ref, dgp_ref):
        i = pl.program_id(0)

        @pl.when(i == 0)
        def _():
            dgn_ref[...] = jnp.zeros_like(dgn_ref)
            dgp_ref[...] = jnp.zeros_like(dgp_ref)

        dx, dgn = _rms_bwd(dhn_ref[...], h_ref[...], gn_ref[...])
        dh = dy_ref[...] + dx
        dh_ref[...] = dh
        dmix, dgp = _rms_bwd(dh, mix_ref[...], gp_ref[...])
        dmix_ref[...] = dmix.astype(BF16)
        dgn_ref[...] += dgn
        dgp_ref[...] += dgp

    return pl.pallas_call(
        body, name="mid_bwd", grid=(t // tt,),
        in_specs=[_row_spec(tt, d), _row_spec(tt, d), _vec_spec(d), _row_spec(tt, d), _row_spec(tt, d), _vec_spec(d),
                  ANY],
        out_specs=[_row_spec(tt, d), _row_spec(tt, d), _vec_spec(d), _vec_spec(d)],
        out_shape=[jax.ShapeDtypeStruct((t, d), F32), jax.ShapeDtypeStruct((t, d), BF16),
                   jax.ShapeDtypeStruct((1, d), F32), jax.ShapeDtypeStruct((1, d), F32)],
        compiler_params=_params(),
    )(d_hn, h, g_pre, dy, mix, g_post, after)


def _first_bwd(d_xn, d_xn_ab, x, g, dh):
    t, d = x.shape
    tt = _tile(t, 128, 16)

    def body(a_ref, b_ref, x_ref, g_ref, dh_ref, dx_ref, dg_ref):
        i = pl.program_id(0)

        @pl.when(i == 0)
        def _():
            dg_ref[...] = jnp.zeros_like(dg_ref)

        dx, dg = _rms_bwd(a_ref[...] + b_ref[...], x_ref[...], g_ref[...])
        dx_ref[...] = dh_ref[...] + dx
        dg_ref[...] += dg

    return pl.pallas_call(
        body, name="first_bwd", grid=(t // tt,),
        in_specs=[_row_spec(tt, d), _row_spec(tt, d), _row_spec(tt, d), _vec_spec(d), _row_spec(tt, d)],
        out_specs=[_row_spec(tt, d), _vec_spec(d)],
        out_shape=[jax.ShapeDtypeStruct((t, d), F32), jax.ShapeDtypeStruct((1, d), F32)],
        compiler_params=_params(),
    )(d_xn, d_xn_ab, x, g, dh)


HALO = 8


def _cur(ts, tc, off):
    return pl.BlockSpec((1, ts, tc), lambda ci, b, s: (b, s, off + ci))


def _prev(ts, tc, off):
    return pl.BlockSpec((1, HALO, tc), lambda ci, b, s: (b, jnp.maximum(s * (ts // HALO) - 1, 0), off + ci))


def _next(ts, tc, off, seq):
    last = seq // HALO - 1
    return pl.BlockSpec((1, HALO, tc), lambda ci, b, s: (b, jnp.minimum((s + 1) * (ts // HALO), last), off + ci))


def _conv_w_spec(kw, tc):
    return pl.BlockSpec((kw, tc), lambda ci, b, s: (0, ci))


def _rows_back(u, prev8, k):
    if k == 0:
        return u
    rolled = pltpu.roll(u, k, axis=0)
    row = lax.broadcasted_iota(jnp.int32, (HALO, u.shape[1]), 0)
    first = jnp.where(row < k, pltpu.roll(prev8, k, axis=0), rolled[0:HALO])
    return first if u.shape[0] == HALO else jnp.concatenate([first, rolled[HALO:]], axis=0)


def _rows_ahead(g, next8, m):
    if m == 0:
        return g
    n = g.shape[0]
    rolled = pltpu.roll(g, n - m, axis=0)
    row = lax.broadcasted_iota(jnp.int32, (HALO, g.shape[1]), 0)
    last = jnp.where(row >= HALO - m, pltpu.roll(next8, HALO - m, axis=0), rolled[n - HALO:n])
    return last if n == HALO else jnp.concatenate([rolled[0:n - HALO], last], axis=0)


def _windows(u, prev8, kw):
    return [_rows_back(u, prev8, kw - 1 - j) for j in range(kw)]


def _tap_sum(w, wins):
    acc = w[0:1, :] * wins[0]
    for j in range(1, len(wins)):
        acc = acc + w[j:j + 1, :] * wins[j]
    return acc


def _silu_grad(x):
    s = _sigmoid(x)
    return s * (1.0 + x * (1.0 - s))


def _qkv_conv_fwd(proj3, w, width):
    bsz, seq, _ = proj3.shape
    kw = w.shape[0]
    ts, tc = _tile(seq, 256, 8), _tile(width, 512, LANES)

    def body(u_ref, up_ref, w_ref, o_ref):
        prev8 = jnp.where(pl.program_id(2) == 0, 0.0, up_ref[0])
        pre = _tap_sum(w_ref[...], _windows(u_ref[0], prev8, kw))
        o_ref[0] = pre * _sigmoid(pre)

    return pl.pallas_call(
        body, name="qkv_conv_fwd", grid=(width // tc, bsz, seq // ts),
        in_specs=[_cur(ts, tc, 0), _prev(ts, tc, 0), _conv_w_spec(kw, tc)],
        out_specs=_cur(ts, tc, 0), out_shape=jax.ShapeDtypeStruct((bsz, seq, width), F32),
        compiler_params=_params(),
    )(proj3, proj3, w)


def _qkv_conv_bwd(proj3, dparts, w, width, into3):
    bsz, seq, _ = proj3.shape
    kw = w.shape[0]
    n_parts = len(dparts)
    part_w = width // n_parts
    ts, tc = _tile(seq, 256, 8), _tile(part_w, 512, LANES)
    n_s = seq // ts
    npt = part_w // tc
    last = seq // HALO - 1

    def part_cur(p):
        def index(ci, b, s):
            use = (ci // npt) == p
            return jnp.where(use, b, 0), jnp.where(use, s, 0), jnp.where(use, ci % npt, 0)
        return pl.BlockSpec((1, ts, tc), index)

    def part_next(p):
        def index(ci, b, s):
            use = (ci // npt) == p
            return (jnp.where(use, b, 0), jnp.where(use, jnp.minimum((s + 1) * (ts // HALO), last), 0),
                    jnp.where(use, ci % npt, 0))
        return pl.BlockSpec((1, HALO, tc), index)

    def body(u_ref, up_ref, un_ref, *rest):
        d_refs, dn_refs = rest[:n_parts], rest[n_parts:2 * n_parts]
        w_ref, _, du_ref, dw_ref, dbuf, dnbuf = rest[2 * n_parts:]
        ci, b, s = pl.program_id(0), pl.program_id(1), pl.program_id(2)

        @pl.when((b == 0) & (s == 0))
        def _():
            dw_ref[...] = jnp.zeros_like(dw_ref)

        for p in range(n_parts):
            @pl.when(ci // npt == p)
            def _(p=p):
                dbuf[...] = d_refs[p][0]
                dnbuf[...] = dn_refs[p][0]

        w = w_ref[...]
        u = u_ref[0]
        wins = _windows(u, jnp.where(s == 0, 0.0, up_ref[0]), kw)
        wins_next = _windows(un_ref[0], u[ts - HALO:ts], kw)
        g = dbuf[...] * _silu_grad(_tap_sum(w, wins))
        g_next = jnp.where(s == n_s - 1, 0.0, dnbuf[...] * _silu_grad(_tap_sum(w, wins_next)))
        for j in range(kw):
            dw_ref[j:j + 1, :] += jnp.sum(g * wins[j], axis=0, keepdims=True)
        du_ref[0] = _tap_sum(w, [_rows_ahead(g, g_next, kw - 1 - j) for j in range(kw)]).astype(BF16)

    return pl.pallas_call(
        body, name="qkv_conv_bwd", grid=(width // tc, bsz, n_s),
        in_specs=[_cur(ts, tc, 0), _prev(ts, tc, 0), _next(ts, tc, 0, seq)]
        + [part_cur(p) for p in range(n_parts)] + [part_next(p) for p in range(n_parts)] + [_conv_w_spec(kw, tc), ANY],
        out_specs=[_cur(ts, tc, 0), _conv_w_spec(kw, tc)],
        out_shape=[jax.ShapeDtypeStruct(into3.shape, BF16), jax.ShapeDtypeStruct((kw, width), F32)],
        input_output_aliases={4 + 2 * n_parts: 0},
        scratch_shapes=[pltpu.VMEM((ts, tc), F32), pltpu.VMEM((HALO, tc), F32)],
        compiler_params=_params(),
    )(proj3, proj3, proj3, *dparts, *dparts, w, into3)


def _sc_fwd(proj3, w, off, sw, into3, into_off):
    bsz, seq, _ = proj3.shape
    kw = w.shape[0]
    ts, tc = _tile(seq, 256, 8), _tile(sw, 512, LANES)
    ob, oc, oh = off // tc, (off + sw) // tc, (off + 2 * sw) // tc

    def body(b_ref, c_ref, cp_ref, h_ref, hp_ref, w_ref, into_ref, o_ref):
        prev8 = jnp.where(pl.program_id(2) == 0, 0.0, cp_ref[0] * hp_ref[0])
        o_ref[0] = (b_ref[0] * _tap_sum(w_ref[...], _windows(c_ref[0] * h_ref[0], prev8, kw))).astype(BF16)

    return pl.pallas_call(
        body, name="sc_fwd", grid=(sw // tc, bsz, seq // ts),
        in_specs=[_cur(ts, tc, ob), _cur(ts, tc, oc), _prev(ts, tc, oc), _cur(ts, tc, oh), _prev(ts, tc, oh),
                  _conv_w_spec(kw, tc), ANY],
        out_specs=_cur(ts, tc, into_off // tc), out_shape=jax.ShapeDtypeStruct(into3.shape, BF16),
        input_output_aliases={6: 0}, compiler_params=_params(),
    )(proj3, proj3, proj3, proj3, proj3, w, into3)


def _sc_bwd(proj3, dmixed3, w, off, sw, d_off):
    bsz, seq, _ = proj3.shape
    kw = w.shape[0]
    ts, tc = _tile(seq, 256, 8), _tile(sw, 512, LANES)
    n_s = seq // ts
    ob, oc, oh, od = off // tc, (off + sw) // tc, (off + 2 * sw) // tc, d_off // tc

    def body(d_ref, dn_ref, b_ref, bn_ref, c_ref, cp_ref, h_ref, hp_ref, w_ref,
             db_ref, dc_ref, dh_ref, dw_ref):
        b, s = pl.program_id(1), pl.program_id(2)

        @pl.when((b == 0) & (s == 0))
        def _():
            dw_ref[...] = jnp.zeros_like(dw_ref)

        w = w_ref[...]
        cc, hh = c_ref[0], h_ref[0]
        wins = _windows(cc * hh, jnp.where(s == 0, 0.0, cp_ref[0] * hp_ref[0]), kw)
        dout = d_ref[0]
        db_ref[0] = (dout * _tap_sum(w, wins)).astype(BF16)
        g = dout * b_ref[0]
        g_next = jnp.where(s == n_s - 1, 0.0, dn_ref[0] * bn_ref[0])
        for j in range(kw):
            dw_ref[j:j + 1, :] += jnp.sum(g * wins[j], axis=0, keepdims=True)
        dp = _tap_sum(w, [_rows_ahead(g, g_next, kw - 1 - j) for j in range(kw)])
        dc_ref[0] = (dp * hh).astype(BF16)
        dh_ref[0] = (dp * cc).astype(BF16)

    out = jax.ShapeDtypeStruct((bsz, seq, sw), BF16)
    return pl.pallas_call(
        body, name="sc_bwd", grid=(sw // tc, bsz, n_s),
        in_specs=[_cur(ts, tc, od), _next(ts, tc, od, seq), _cur(ts, tc, ob), _next(ts, tc, ob, seq),
                  _cur(ts, tc, oc), _prev(ts, tc, oc), _cur(ts, tc, oh), _prev(ts, tc, oh), _conv_w_spec(kw, tc)],
        out_specs=[_cur(ts, tc, 0)] * 3 + [_conv_w_spec(kw, tc)],
        out_shape=[out, out, out, jax.ShapeDtypeStruct((kw, sw), F32)],
        compiler_params=_params(),
    )(dmixed3, dmixed3, proj3, proj3, proj3, proj3, proj3, proj3, w)


GATE_CHUNKS = 4


def _tri_ones(lower):
    i = lax.broadcasted_iota(jnp.int32, (CHUNK, CHUNK), 0)
    j = lax.broadcasted_iota(jnp.int32, (CHUNK, CHUNK), 1)
    return jnp.where((i >= j) if lower else (j >= i), 1.0, 0.0).astype(F32)


def _gates_fwd(proj_ab, a_log_pad, dt_pad, heads):
    t = proj_ab.shape[0]
    gw = heads * LANES

    rows = GATE_CHUNKS * CHUNK

    def body(ab_ref, al_ref, dt_ref, gc_ref, beta_ref):
        for n in range(GATE_CHUNKS):
            rs = slice(n * CHUNK, (n + 1) * CHUNK)
            ab = ab_ref[rs, :]
            g = -jnp.exp(al_ref[...]) * _softplus(ab + dt_ref[...])
            gc = jnp.dot(_tri_ones(True), g, precision=EXACT_SUM, preferred_element_type=F32)
            beta = _sigmoid(ab)
            for h in range(heads):
                gc_ref[rs, h * LANES:(h + 1) * LANES] = jnp.broadcast_to(gc[:, h:h + 1], (CHUNK, LANES))
                beta_ref[rs, h * LANES:(h + 1) * LANES] = jnp.broadcast_to(beta[:, heads + h:heads + h + 1], (CHUNK, LANES))

    return pl.pallas_call(
        body, name="gates_fwd", grid=(t // rows,),
        in_specs=[_row_spec(rows, LANES), _vec_spec(LANES), _vec_spec(LANES)],
        out_specs=[_row_spec(rows, gw), _row_spec(rows, gw)],
        out_shape=[jax.ShapeDtypeStruct((t, gw), F32)] * 2, compiler_params=_params(),
    )(proj_ab, a_log_pad, dt_pad)


def _gates_bwd(proj_ab, a_log_pad, dt_pad, dgc_b, dbeta_b, heads):
    t = proj_ab.shape[0]
    gw = heads * LANES

    def body(ab_ref, al_ref, dt_ref, dgc_ref, dbeta_ref, dab_ref, dal_ref, ddt_ref):
        i = pl.program_id(0)

        @pl.when(i == 0)
        def _():
            dal_ref[...] = jnp.zeros_like(dal_ref)
            ddt_ref[...] = jnp.zeros_like(ddt_ref)

        lane = lax.broadcasted_iota(jnp.int32, (CHUNK, LANES), 1)
        for n in range(GATE_CHUNKS):
            rs = slice(n * CHUNK, (n + 1) * CHUNK)
            dgc = jnp.zeros((CHUNK, LANES), F32)
            dbeta = jnp.zeros((CHUNK, LANES), F32)
            for h in range(heads):
                dgc = jnp.where(lane == h, dgc_ref[rs, h * LANES:(h + 1) * LANES], dgc)
                dbeta = jnp.where(lane == heads + h, dbeta_ref[rs, h * LANES:(h + 1) * LANES], dbeta)
            dg = jnp.dot(_tri_ones(False), dgc, precision=EXACT_SUM, preferred_element_type=F32)
            ab = ab_ref[rs, :]
            z = ab + dt_ref[...]
            ea = jnp.exp(al_ref[...])
            da = dg * (-ea) * _sigmoid(z)
            beta = _sigmoid(ab)
            db = dbeta * beta * (1.0 - beta)
            dab_ref[rs, :] = jnp.where(lane < heads, da, jnp.where(lane < 2 * heads, db, 0.0)).astype(BF16)
            da_m = jnp.where(lane < heads, da, 0.0)
            ddt_ref[...] += jnp.sum(da_m, axis=0, keepdims=True)
            dal_ref[...] += jnp.sum(jnp.where(lane < heads, dg * (-ea) * _softplus(z), 0.0), axis=0, keepdims=True)

    rows = GATE_CHUNKS * CHUNK
    return pl.pallas_call(
        body, name="gates_bwd", grid=(t // rows,),
        in_specs=[_row_spec(rows, LANES), _vec_spec(LANES), _vec_spec(LANES), _row_spec(rows, gw), _row_spec(rows, gw)],
        out_specs=[_row_spec(rows, LANES), _vec_spec(LANES), _vec_spec(LANES)],
        out_shape=[jax.ShapeDtypeStruct((t, LANES), BF16), jax.ShapeDtypeStruct((1, LANES), F32),
                   jax.ShapeDtypeStruct((1, LANES), F32)],
        compiler_params=_params(),
    )(proj_ab, a_log_pad, dt_pad, dgc_b, dbeta_b)


def _dot(a, b, dims, hi=False):
    if hi:
        return lax.dot_general(a, b, (dims, ((), ())), precision=HI, preferred_element_type=F32)
    return lax.dot_general(a.astype(BF16), b.astype(BF16), (dims, ((), ())), preferred_element_type=F32)


NN = ((1,), (0,))
NT = ((1,), (1,))
TN = ((0,), (0,))


def _each(f, *lists):
    return [f(*xs) for xs in zip(*lists)]


def _dots(a, b, dims, hi=False):
    return _each(lambda x, y: _dot(x, y, dims, hi=hi), a, b)


def _unit_lower_inverse(ms):
    i = lax.broadcasted_iota(jnp.int32, (CHUNK, CHUNK), 0)
    j = lax.broadcasted_iota(jnp.int32, (CHUNK, CHUNK), 1)
    eye = jnp.where(i == j, 1.0, 0.0).astype(F32)
    ts = [eye - jnp.where(jnp.right_shift(i, 1) == jnp.right_shift(j, 1), m, 0.0) for m in ms]
    shift = 1
    while (1 << shift) < CHUNK:
        same_pair = jnp.right_shift(i, shift + 1) == jnp.right_shift(j, shift + 1)
        other_half = jnp.right_shift(i, shift) != jnp.right_shift(j, shift)
        offs = [jnp.where(same_pair & other_half, m, 0.0) for m in ms]
        corr = _dots(_dots(ts, offs, NN, hi=True), ts, NN, hi=True)
        ts = _each(lambda t, c: t - c, ts, corr)
        shift += 1
    return ts


def _chunk_local(qrs, krs, vs, gcbs, betabs, head_dim, solved=None):
    i = lax.broadcasted_iota(jnp.int32, (CHUNK, CHUNK), 0)
    j = lax.broadcasted_iota(jnp.int32, (CHUNK, CHUNK), 1)
    scale = head_dim ** -0.5
    rqs = [lax.rsqrt(jnp.sum(q * q, axis=-1, keepdims=True) + L2_EPS) for q in qrs]
    rks = [lax.rsqrt(jnp.sum(k * k, axis=-1, keepdims=True) + L2_EPS) for k in krs]
    qhs = _each(lambda a, r: a * r, qrs, rqs)
    ks = _each(lambda a, r: a * r, krs, rks)
    qs = [a * scale for a in qhs]
    decays = [jnp.exp(jnp.where(i >= j, g[:, 0:CHUNK] - g.T[0:CHUNK, :], -jnp.inf)) for g in gcbs]
    kks = _dots(ks, ks, NT)
    qks = _dots(qs, ks, NT)
    ms = _each(lambda b, kk, d: jnp.where(i > j, b[:, 0:CHUNK] * kk * d, 0.0), betabs, kks, decays)
    egs = [jnp.exp(g) for g in gcbs]
    rhs_ws = _each(lambda k, b, e: k * b * e, ks, betabs, egs)
    if solved is None:
        tinvs = _unit_lower_inverse(ms)
        us = _dots(tinvs, _each(lambda v, b: v * b, vs, betabs), NN, hi=True)
        ws = _dots(tinvs, rhs_ws, NN, hi=True)
    else:
        tinvs, us, ws = solved
    out = []
    for h in range(len(qrs)):
        g_last = gcbs[h][CHUNK - 1:CHUNK, :]
        e_last = jnp.exp(g_last - gcbs[h])
        out.append(dict(rq=rqs[h], rk=rks[h], qh=qhs[h], q=qs[h], k=ks[h], decay=decays[h],
                        beta_col=betabs[h][:, 0:CHUNK], kk=kks[h], m=ms[h], tinv=tinvs[h], eg=egs[h], rhs_w=rhs_ws[h],
                        u=us[h], w=ws[h], p=qks[h] * decays[h], qd=qs[h] * egs[h], kd=ks[h] * e_last, e_last=e_last,
                        gl=jnp.exp(g_last), scale=scale, strict=i > j, incl=i >= j))
    return out


def _field(dicts, name):
    return [d[name] for d in dicts]


GDN_HEAD_GROUP = 16


def _gdn_specs(n_chunks, heads, reverse):
    hg = min(GDN_HEAD_GROUP, heads)
    assert heads % hg == 0

    def cidx(c):
        return (n_chunks - 1 - c) if reverse else c

    def tok(off):
        return pl.BlockSpec((CHUNK, hg * LANES), lambda b, h, c: (b * n_chunks + cidx(c), off // hg + h))

    state = pl.BlockSpec((None, hg, LANES, LANES), lambda b, h, c: (b * n_chunks + cidx(c), h, 0, 0))
    inverse = pl.BlockSpec((None, hg, CHUNK, CHUNK), lambda b, h, c: (b * n_chunks + cidx(c), h, 0, 0))
    return hg, tok, state, inverse


def _gdn_fwd(qkv_act, gcb, betab, bsz, heads):
    t = qkv_act.shape[0]
    n_chunks = t // bsz // CHUNK
    hg, tok, state, inverse = _gdn_specs(n_chunks, heads, False)

    def body(q_ref, k_ref, v_ref, gc_ref, beta_ref, o_ref, s_ref, t_ref, u_ref, w_ref, st):
        @pl.when(pl.program_id(2) == 0)
        def _():
            st[...] = jnp.zeros_like(st)

        sls = [slice(hh * LANES, (hh + 1) * LANES) for hh in range(hg)]
        loc = _chunk_local(*[[r[:, sl] for sl in sls] for r in (q_ref, k_ref, v_ref, gc_ref, beta_ref)], LANES)
        s0 = [st[hh] for hh in range(hg)]
        v_new = _each(lambda u, ws: u - ws, _field(loc, "u"), _dots(_field(loc, "w"), s0, NN))
        o_state = _dots(_field(loc, "qd"), s0, NN)
        o_local = _dots(_field(loc, "p"), v_new, NN)
        s_add = _dots(_field(loc, "kd"), v_new, TN)
        for hh in range(hg):
            o_ref[:, sls[hh]] = o_state[hh] + o_local[hh]
            s_ref[hh] = s0[hh]
            t_ref[hh] = loc[hh]["tinv"]
            u_ref[:, sls[hh]] = loc[hh]["u"]
            w_ref[:, sls[hh]] = loc[hh]["w"]
            st[hh] = s0[hh] * loc[hh]["gl"] + s_add[hh]

    tok_shape = jax.ShapeDtypeStruct((t, heads * LANES), F32)
    return pl.pallas_call(
        body, name="gdn_fwd", grid=(bsz, heads // hg, n_chunks),
        in_specs=[tok(0), tok(heads), tok(2 * heads), tok(0), tok(0)],
        out_specs=[tok(0), state, inverse, tok(0), tok(0)],
        out_shape=[tok_shape, jax.ShapeDtypeStruct((bsz * n_chunks, heads, LANES, LANES), F32),
                   jax.ShapeDtypeStruct((bsz * n_chunks, heads, CHUNK, CHUNK), F32), tok_shape, tok_shape],
        scratch_shapes=[pltpu.VMEM((hg, LANES, LANES), F32)], compiler_params=_params(),
    )(qkv_act, qkv_act, qkv_act, gcb, betab)


def _gdn_bwd(qkv_act, gcb, betab, states, solved, d_o, bsz, heads):
    t = qkv_act.shape[0]
    n_chunks = t // bsz // CHUNK
    hg, tok, state, inverse = _gdn_specs(n_chunks, heads, True)

    def rowsum(a):
        return jnp.sum(a, axis=-1, keepdims=True)

    def finish_head(sl, L, v, betab, d_qd, d_kd, d_gl, d_p, d_m, d_rhs_u, d_rhs_w, d_q, d_k,
                    dq_ref, dk_ref, dv_ref, dgc_ref, dbeta_ref):
        k, decay = L["k"], L["decay"]
        dv_ref[:, sl] = betab * d_rhs_u
        e = d_m * L["m"] + d_p * L["p"]
        d_beta = rowsum(d_m * L["kk"] * decay) + rowsum(d_rhs_u * v) + rowsum(d_rhs_w * k * L["eg"])
        s_kd = rowsum(d_kd * L["kd"])
        d_gc = (rowsum(e) - rowsum(e.T) + rowsum(d_rhs_w * L["rhs_w"]) + rowsum(d_qd * L["qd"]) - s_kd)
        row = lax.broadcasted_iota(jnp.int32, (CHUNK, 1), 0)
        d_gc = d_gc + jnp.where(row == CHUNK - 1, jnp.sum(s_kd) + d_gl * jnp.sum(L["gl"][:, 0:1]), 0.0)
        dgc_ref[:, sl] = jnp.broadcast_to(d_gc, (CHUNK, LANES))
        dbeta_ref[:, sl] = jnp.broadcast_to(d_beta, (CHUNK, LANES))
        d_qh = d_q * L["scale"]
        dq_ref[:, sl] = L["rq"] * (d_qh - L["qh"] * rowsum(d_qh * L["qh"]))
        dk_ref[:, sl] = L["rk"] * (d_k - k * rowsum(d_k * k))

    def body(q_ref, k_ref, v_ref, gc_ref, beta_ref, s_ref, t_ref, u_ref, w_ref, do_ref,
             dq_ref, dk_ref, dv_ref, dgc_ref, dbeta_ref, dst):
        @pl.when(pl.program_id(2) == 0)
        def _():
            dst[...] = jnp.zeros_like(dst)

        sls = [slice(hh * LANES, (hh + 1) * LANES) for hh in range(hg)]
        vs = [v_ref[:, sl] for sl in sls]
        betabs = [beta_ref[:, sl] for sl in sls]
        solved = ([t_ref[hh] for hh in range(hg)], [u_ref[:, sl] for sl in sls], [w_ref[:, sl] for sl in sls])
        loc = _chunk_local([q_ref[:, sl] for sl in sls], [k_ref[:, sl] for sl in sls], vs,
                           [gc_ref[:, sl] for sl in sls], betabs, LANES, solved)
        q, k, u, w, p, tinv, decay, qd, kd, eg = (_field(loc, n) for n in
                                                  ("q", "k", "u", "w", "p", "tinv", "decay", "qd", "kd", "eg"))
        s0 = [s_ref[hh] for hh in range(hg)]
        d_out = [do_ref[:, sl] for sl in sls]
        ds1 = [dst[hh] for hh in range(hg)]
        v_new = _each(lambda a, b: a - b, u, _dots(w, s0, NN))

        d_vnew = _each(lambda a, b: a + b, _dots(p, d_out, TN), _dots(kd, ds1, NN))
        d_qd = _dots(d_out, s0, NT)
        d_kd = _dots(v_new, ds1, NT)
        d_gl = _each(lambda a, b: jnp.sum(a * b), ds1, s0)
        d_p = _each(lambda L, a: jnp.where(L["incl"], a, 0.0), loc, _dots(d_out, v_new, NT))
        d_w = [-a for a in _dots(d_vnew, s0, NT)]
        ds_out, ds_vn = _dots(qd, d_out, TN), _dots(w, d_vnew, TN)
        for hh in range(hg):
            dst[hh] = ds_out[hh] + ds1[hh] * loc[hh]["gl"] - ds_vn[hh]

        d_rhs_u = _dots(tinv, d_vnew, TN, hi=True)
        d_rhs_w = _dots(tinv, d_w, TN, hi=True)
        d_a = _each(lambda a, b: -(a + b), _dots(d_rhs_u, u, NT, hi=True), _dots(d_rhs_w, w, NT, hi=True))
        d_m = _each(lambda L, a: jnp.where(L["strict"], a, 0.0), loc, d_a)
        g_kk = _each(lambda L, a: a * L["beta_col"] * L["decay"], loc, d_m)
        h_qk = _each(lambda a, d: a * d, d_p, decay)

        d_q = _each(lambda a, e, b: a + e * b, _dots(h_qk, k, NN), eg, d_qd)
        d_k = _each(lambda a, b, c, L, bb, rw, dk: a + b + c + bb * L["eg"] * rw + L["e_last"] * dk,
                    _dots(g_kk, k, NN), _dots(g_kk, k, TN), _dots(h_qk, q, TN), loc, betabs, d_rhs_w, d_kd)
        for hh in range(hg):
            finish_head(sls[hh], loc[hh], vs[hh], betabs[hh], d_qd[hh], d_kd[hh], d_gl[hh], d_p[hh], d_m[hh],
                        d_rhs_u[hh], d_rhs_w[hh], d_q[hh], d_k[hh], dq_ref, dk_ref, dv_ref, dgc_ref, dbeta_ref)

    tok_shape = jax.ShapeDtypeStruct((t, heads * LANES), F32)
    return pl.pallas_call(
        body, name="gdn_bwd", grid=(bsz, heads // hg, n_chunks),
        in_specs=[tok(0), tok(heads), tok(2 * heads), tok(0), tok(0), state, inverse, tok(0), tok(0), tok(0)],
        out_specs=[tok(0)] * 5,
        out_shape=[tok_shape] * 5,
        scratch_shapes=[pltpu.VMEM((hg, LANES, LANES), F32)], compiler_params=_params(),
    )(qkv_act, qkv_act, qkv_act, gcb, betab, states, *solved, d_o)


def _gdn_out_fwd(o, proj, gw_norm, heads, z_off, out_width, after):
    t = o.shape[0]
    gw = heads * LANES
    ts = _tile(t, 256, 16)
    assert z_off % gw == 0
    zb = z_off // gw

    def body(o_ref, z_ref, w_ref, after_ref, out_ref):
        w = w_ref[...]
        for h in range(heads):
            sl = slice(h * LANES, (h + 1) * LANES)
            z = z_ref[:, sl]
            out_ref[:, sl] = (_rms_fwd(o_ref[:, sl], w) * (z * _sigmoid(z))).astype(BF16)

    return pl.pallas_call(
        body, name="gdn_out_fwd", grid=(t // ts,),
        in_specs=[pl.BlockSpec((ts, gw), lambda i: (i, 0)), pl.BlockSpec((ts, gw), lambda i: (i, zb)),
                  pl.BlockSpec((1, LANES), lambda i: (0, 0)), ANY],
        out_specs=pl.BlockSpec((ts, gw), lambda i: (i, 0)),
        out_shape=jax.ShapeDtypeStruct((t, out_width), BF16), compiler_params=_params(),
    )(o, proj, gw_norm, after)


def _gdn_out_bwd(dmixed, o, proj, gw_norm, heads, z_off, after):
    t = o.shape[0]
    gw = heads * LANES
    ts = _tile(t, 256, 16)
    assert z_off % gw == 0
    zb = z_off // gw

    def body(d_ref, o_ref, z_ref, w_ref, after_ref, do_ref, dz_ref, dw_ref):
        @pl.when(pl.program_id(0) == 0)
        def _():
            dw_ref[...] = jnp.zeros_like(dw_ref)

        w = w_ref[...]
        d_w_sum = jnp.zeros_like(w)
        for h in range(heads):
            sl = slice(h * LANES, (h + 1) * LANES)
            d, oo, z = d_ref[:, sl], o_ref[:, sl], z_ref[:, sl]
            dz_ref[:, sl] = (d * _rms_fwd(oo, w) * _silu_grad(z)).astype(BF16)
            d_o, d_w = _rms_bwd(d * (z * _sigmoid(z)), oo, w)
            do_ref[:, sl] = d_o
            d_w_sum = d_w_sum + d_w
        dw_ref[...] += d_w_sum

    blk = pl.BlockSpec((ts, gw), lambda i: (i, 0))
    vec = pl.BlockSpec((1, LANES), lambda i: (0, 0))
    return pl.pallas_call(
        body, name="gdn_out_bwd", grid=(t // ts,),
        in_specs=[blk, blk, pl.BlockSpec((ts, gw), lambda i: (i, zb)), vec, ANY],
        out_specs=[blk, pl.BlockSpec((ts, gw), lambda i: (i, zb)), vec],
        out_shape=[jax.ShapeDtypeStruct((t, heads * LANES), F32), jax.ShapeDtypeStruct((t, proj.shape[1]), BF16),
                   jax.ShapeDtypeStruct((1, LANES), F32)],
        compiler_params=_params(),
    )(dmixed, o, proj, gw_norm, after)


def _place_columns(into3, pieces, off):
    bsz, seq, _ = into3.shape
    n_parts, part_w = len(pieces), pieces[0].shape[-1]
    ts, tc = _tile(seq, 512, 16), _tile(part_w, 1024, LANES)
    npt = part_w // tc

    def part(p):
        def index(ci, b, s):
            use = (ci // npt) == p
            return jnp.where(use, b, 0), jnp.where(use, s, 0), jnp.where(use, ci % npt, 0)
        return pl.BlockSpec((1, ts, tc), index)

    def body(*refs):
        o_ref = refs[-1]
        for p in range(n_parts):
            @pl.when(pl.program_id(0) // npt == p)
            def _(p=p):
                o_ref[...] = refs[p][...]

    return pl.pallas_call(
        body, name="place_columns", grid=(n_parts * npt, bsz, seq // ts),
        in_specs=[part(p) for p in range(n_parts)] + [ANY],
        out_specs=_cur(ts, tc, off // tc), out_shape=jax.ShapeDtypeStruct(into3.shape, into3.dtype),
        input_output_aliases={n_parts: 0}, compiler_params=_params(),
    )(*pieces, into3)


def _place():
    x, y, c = lax.axis_index("x"), lax.axis_index("y"), lax.axis_index("c")
    return x, y, c, [(1 - x, y), (x, 1 - y), (1 - x, 1 - y)]


def _aligned(start, align):
    return start if isinstance(start, int) else pl.multiple_of(start, align)


class _Layout:
    def __init__(self, kind, shard_shape):
        self.kind = kind
        self.r, self.c = shard_shape

    def full_shape(self):
        r, c = self.r, self.c
        return {"major": (N_CHIPS, r, c), "rows": (N_CHIPS * r, c), "cols": (r, N_CHIPS * c)}[self.kind]

    def region(self, ref, j, half=None):
        r, c = self.r, self.c
        r0, nr = (0, r) if half is None else (half * (r // 2), r // 2)
        if self.kind == "major":
            return ref.at[j, pl.ds(_aligned(r0, 16), nr), :]
        if self.kind == "rows":
            return ref.at[pl.ds(_aligned(j * r + r0, 16), nr), :]
        return ref.at[pl.ds(_aligned(r0, 16), nr), pl.ds(_aligned(j * c, LANES), c)]

    def block_spec(self, tr, where):
        r, c = self.r, self.c
        if self.kind == "major":
            return pl.BlockSpec((None, tr, c), lambda *a: (where(*a)[0], where(*a)[1], 0))
        if self.kind == "rows":
            return pl.BlockSpec((tr, c), lambda *a: (where(*a)[0] * (r // tr) + where(*a)[1], 0))
        return pl.BlockSpec((tr, c), lambda *a: (where(*a)[1], where(*a)[0]))


def _remote(src, dst, send_sem, recv_sem, dev):
    return pltpu.make_async_remote_copy(src_ref=src, dst_ref=dst, send_sem=send_sem, recv_sem=recv_sem,
                                        device_id=dev, device_id_type=MESH)


def _all_gather(big, layouts, small):
    nb, ns = len(big), len(small)
    n_remote = 6 * nb + 3 * ns

    def body(*refs):
        ins, outs = refs[:nb + ns], refs[nb + ns:2 * (nb + ns)]
        send_sems, recv_sems, local_sems = refs[2 * (nb + ns):]
        x, y, c, chips = _place()
        j = 2 * x + y
        local = []
        for i in range(ns):
            local.append(pltpu.make_async_copy(ins[nb + i], outs[nb + i].at[j], local_sems.at[i]))
        for cp in local:
            cp.start()
        sends = []
        for i in range(nb):
            for p, (px, py) in enumerate(chips):
                k = 3 * i + p
                mine = layouts[i].region(outs[i], j, c)
                sends.append(_remote(mine, mine, send_sems.at[k], recv_sems.at[k], (px, py, c)))
        for i in range(ns):
            for p, (px, py) in enumerate(chips):
                k = 6 * nb + 3 * i + p
                sends.append(_remote(ins[nb + i], outs[nb + i].at[j], send_sems.at[k], recv_sems.at[k], (px, py, c)))
        for cp in sends:
            cp.start()
        for i in range(nb):
            for p, (px, py) in enumerate(chips):
                k, jp = 3 * i + p, 2 * px + py
                got = layouts[i].region(outs[i], jp, c)
                _remote(got, got, send_sems.at[k], recv_sems.at[k], (px, py, c)).wait_recv()
                fwd = _remote(got, got, send_sems.at[3 * nb + k], recv_sems.at[3 * nb + k], (x, y, 1 - c))
                fwd.start()
                sends.append(fwd)
        for i in range(ns):
            for p, (px, py) in enumerate(chips):
                k, jp = 6 * nb + 3 * i + p, 2 * px + py
                _remote(ins[nb + i], outs[nb + i].at[jp], send_sems.at[k], recv_sems.at[k], (px, py, c)).wait_recv()
        for i in range(nb):
            for p, (px, py) in enumerate(chips):
                k, jp = 3 * nb + 3 * i + p, 2 * px + py
                got = layouts[i].region(outs[i], jp, 1 - c)
                _remote(got, got, send_sems.at[k], recv_sems.at[k], (x, y, 1 - c)).wait_recv()
        for cp in sends:
            cp.wait_send()
        for cp in local:
            cp.wait()

    out_shape = [jax.ShapeDtypeStruct(lay.full_shape(), BF16) for lay in layouts]
    out_shape += [jax.ShapeDtypeStruct((N_CHIPS,) + s.shape, F32) for s in small]
    return pl.pallas_call(
        body, name="all_gather_weights", in_specs=[ANY] * (nb + ns), out_specs=[ANY] * (nb + ns), out_shape=out_shape,
        input_output_aliases={i: i for i in range(nb)},
        scratch_shapes=[pltpu.SemaphoreType.DMA((n_remote,)), pltpu.SemaphoreType.DMA((n_remote,)),
                        pltpu.SemaphoreType.DMA((ns,))],
        compiler_params=_params(),
    )(*big, *small)


HBM = pl.BlockSpec(memory_space=pltpu.HBM)
SEM = pl.BlockSpec(memory_space=pltpu.SEMAPHORE)
SPLIT_COPY = pltpu.CompilerParams(has_side_effects=pltpu.SideEffectType.DATAFLOW_SIDE_EFFECTING)


def _in_hbm(a):
    return pltpu.with_memory_space_constraint(a, pltpu.HBM)


def _split_copy_start(arrays, plan, n_copies, name, after=None):
    na = len(arrays)
    order_only = [] if after is None else [after]

    def body(*refs):
        base = na + len(order_only)
        send_sems, recv_sems = refs[base], refs[base + 1]
        thru, token = refs[base + 2:base + 2 + na], refs[base + 2 + na]
        for k, (src, dst, _, dev) in enumerate(plan(thru, *_place())):
            _remote(src, dst, send_sems.at[k], recv_sems.at[k], dev).start()
        token[...] = jnp.zeros_like(token)

    outs = pl.pallas_call(
        body, name=name, in_specs=[HBM] * na + [ANY] * len(order_only),
        out_specs=[SEM, SEM] + [HBM] * na + [pl.BlockSpec(memory_space=pltpu.VMEM)],
        out_shape=[pltpu.SemaphoreType.DMA((n_copies,)), pltpu.SemaphoreType.DMA((n_copies,))]
        + [pltpu.HBM(a.shape, a.dtype) for a in arrays] + [jax.ShapeDtypeStruct((8, LANES), F32)],
        input_output_aliases={i: 2 + i for i in range(na)}, compiler_params=SPLIT_COPY,
    )(*[_in_hbm(a) for a in arrays], *order_only)
    return (outs[0], outs[1]), list(outs[2:2 + na]), outs[2 + na]


def _split_copy_wait(sems, arrays, plan, name, after):
    na = len(arrays)

    def body(*refs):
        send_sems, recv_sems = refs[na], refs[na + 1]
        thru = refs[na + 3:]
        for k, (src, _, landing, dev) in enumerate(plan(thru, *_place())):
            cp = _remote(src, landing, send_sems.at[k], recv_sems.at[k], dev)
            cp.wait_send()
            cp.wait_recv()

    return list(pl.pallas_call(
        body, name=name, in_specs=[HBM] * na + [SEM, SEM, ANY], out_specs=[HBM] * na,
        out_shape=[pltpu.HBM(a.shape, a.dtype) for a in arrays],
        input_output_aliases={i: i for i in range(na)}, compiler_params=SPLIT_COPY,
    )(*arrays, *sems, after))


def _gather_plan(layouts):
    def plan(bufs, x, y, c, chips):
        copies = []
        for buf, lay in zip(bufs, layouts):
            mine = lay.region(buf, 2 * x + y, c)
            copies += [(mine, mine, lay.region(buf, 2 * px + py, c), (px, py, c)) for px, py in chips]
        return copies
    return plan


def _forward_plan(layouts):
    def plan(bufs, x, y, c, chips):
        copies = []
        for buf, lay in zip(bufs, layouts):
            for px, py in chips:
                got = lay.region(buf, 2 * px + py, c)
                copies.append((got, got, lay.region(buf, 2 * px + py, 1 - c), (x, y, 1 - c)))
        return copies
    return plan


def _halves_plan(layouts):
    def plan(arrays, x, y, c, chips):
        nw = len(layouts)
        copies = []
        for i, lay in enumerate(layouts):
            for j in range(N_CHIPS):
                land = arrays[nw + i].at[j]
                copies.append((lay.region(arrays[i], j, 1 - c), land, land, (x, y, 1 - c)))
        return copies
    return plan


def _partials_plan(nw):
    def plan(arrays, x, y, c, chips):
        copies = []
        for i in range(nw):
            for p, (px, py) in enumerate(chips):
                land = arrays[nw + i].at[p]
                copies.append((arrays[i].at[2 * px + py], land, land, (px, py, c)))
        return copies
    return plan


def _join_plan(nw):
    def plan(arrays, x, y, c, chips):
        return [(arrays[i], arrays[nw + i], arrays[nw + i], (x, y, 1 - c)) for i in range(nw)]
    return plan


def _forward_to_sibling(bufs, layouts, name):
    nb = len(bufs)

    def body(*refs):
        outs = refs[nb:2 * nb]
        send_sems, recv_sems = refs[2 * nb:]
        x, y, c, chips = _place()
        cps = []
        for i in range(nb):
            for p, (px, py) in enumerate(chips):
                got = layouts[i].region(outs[i], 2 * px + py, c)
                cps.append(_remote(got, got, send_sems.at[3 * i + p], recv_sems.at[3 * i + p], (x, y, 1 - c)))
        for cp in cps:
            cp.start()
        for i in range(nb):
            for p, (px, py) in enumerate(chips):
                theirs = layouts[i].region(outs[i], 2 * px + py, 1 - c)
                _remote(theirs, theirs, send_sems.at[3 * i + p], recv_sems.at[3 * i + p], (x, y, 1 - c)).wait_recv()
        for cp in cps:
            cp.wait_send()

    return pl.pallas_call(
        body, name=name, in_specs=[ANY] * nb, out_specs=[ANY] * nb,
        out_shape=[jax.ShapeDtypeStruct(b.shape, b.dtype) for b in bufs],
        input_output_aliases={i: i for i in range(nb)},
        scratch_shapes=[pltpu.SemaphoreType.DMA((3 * nb,)), pltpu.SemaphoreType.DMA((3 * nb,))],
        compiler_params=_params(),
    )(*bufs)


def _halves_to_sibling(grads, layouts, name):
    nw = len(grads)

    def body(*refs):
        ins, gots = refs[:nw], refs[nw:2 * nw]
        send_sems, recv_sems = refs[2 * nw:]
        x, y, c, _ = _place()
        cps = []
        for i in range(nw):
            for j in range(N_CHIPS):
                k = N_CHIPS * i + j
                cps.append(_remote(layouts[i].region(ins[i], j, 1 - c), gots[i].at[j],
                                   send_sems.at[k], recv_sems.at[k], (x, y, 1 - c)))
        for cp in cps:
            cp.start()
        for cp in cps:
            cp.wait()

    half = [jax.ShapeDtypeStruct((N_CHIPS, lay.r // 2, lay.c), BF16) for lay in layouts]
    return pl.pallas_call(
        body, name=name, in_specs=[ANY] * nw, out_specs=[ANY] * nw, out_shape=half,
        scratch_shapes=[pltpu.SemaphoreType.DMA((N_CHIPS * nw,)), pltpu.SemaphoreType.DMA((N_CHIPS * nw,))],
        compiler_params=_params(),
    )(*grads)


def _chip_sum(grad, got, layout, core, name):
    n, hr, c = got.shape
    tr = _tile(hr, 256, 16)
    nb = hr // tr

    def body(core_ref, a_ref, b_ref, o_ref):
        o_ref[...] = (a_ref[...].astype(F32) + b_ref[...].astype(F32)).astype(BF16)

    spec = pl.BlockSpec((None, tr, c), lambda j, i, core_ref: (j, i, 0))
    return pl.pallas_call(
        body, name=name,
        grid_spec=pltpu.PrefetchScalarGridSpec(
            num_scalar_prefetch=1, grid=(n, nb),
            in_specs=[layout.block_spec(tr, lambda j, i, core_ref: (j, core_ref[0] * nb + i)), spec],
            out_specs=spec),
        out_shape=jax.ShapeDtypeStruct((n, hr, c), BF16), compiler_params=_params(),
    )(core, grad, got)


def _halves_start(grads, layouts, name, after=None):
    lands = [lax.empty((N_CHIPS, lay.r // 2, lay.c), BF16) for lay in layouts]
    return _split_copy_start(list(grads) + lands, _halves_plan(layouts), N_CHIPS * len(grads), name, after)


def _partials_start(parts, name, after=None):
    lands = [lax.empty((3,) + p.shape[1:], BF16) for p in parts]
    return _split_copy_start(list(parts) + lands, _partials_plan(len(parts)), 3 * len(parts), name, after)


def _shard_sum(parts, got, chip, name):
    _, r, c = parts.shape
    tr = _tile(r, 256, 16)

    def body(chip_ref, o_ref, g_ref, out_ref):
        acc = o_ref[...].astype(F32)
        for p in range(3):
            acc = acc + g_ref[p].astype(F32)
        out_ref[...] = acc

    return pl.pallas_call(
        body, name=name,
        grid_spec=pltpu.PrefetchScalarGridSpec(
            num_scalar_prefetch=1, grid=(r // tr,),
            in_specs=[pl.BlockSpec((None, tr, c), lambda i, chip_ref: (chip_ref[0], i, 0)),
                      pl.BlockSpec((3, tr, c), lambda i, chip_ref: (0, i, 0))],
            out_specs=pl.BlockSpec((tr, c), lambda i, chip_ref: (i, 0))),
        out_shape=jax.ShapeDtypeStruct((r, c), F32), compiler_params=_params(),
    )(chip, parts, got)


def _join_start(halves, name):
    lands = [lax.empty(h.shape, F32) for h in halves]
    return _split_copy_start(list(halves) + lands, _join_plan(len(halves)), len(halves), name)


def _adamw_rows(w, g_half, m, v, half, filled, after, name):
    r, c = w.shape
    hr = r // 2
    tr = _tile(hr, 128, 8)
    nb = hr // tr
    c1 = 1.0 - ADAM_B1 ** ADAM_STEP
    c2 = 1.0 - ADAM_B2 ** ADAM_STEP
    n_prev = 0 if filled is None else 4

    def body(half_ref, w_ref, gin_ref, m_ref, v_ref, *rest):
        g_ref, d_ref, nm_ref, nv_ref = rest[1 + n_prev:]
        gg = gin_ref[...]
        nm = ADAM_B1 * m_ref[...] + (1.0 - ADAM_B1) * gg
        nv = ADAM_B2 * v_ref[...] + (1.0 - ADAM_B2) * jnp.square(gg)
        m_hat = nm / c1
        v_hat = nv / c2
        g_ref[...] = gg
        d_ref[...] = -ADAM_LR * (m_hat / (jnp.sqrt(v_hat) + ADAM_EPS) + ADAM_WD * w_ref[...])
        nm_ref[...] = nm
        nv_ref[...] = nv

    full = pl.BlockSpec((tr, c), lambda i, half_ref: (half_ref[0] * nb + i, 0))
    part = pl.BlockSpec((tr, c), lambda i, half_ref: (i, 0))
    return pl.pallas_call(
        body, name=name,
        grid_spec=pltpu.PrefetchScalarGridSpec(
            num_scalar_prefetch=1, grid=(nb,), in_specs=[full, part, full, full] + [ANY] * (1 + n_prev),
            out_specs=[full] * 4),
        out_shape=[jax.ShapeDtypeStruct((r, c), F32)] * 4,
        input_output_aliases={6 + k: k for k in range(n_prev)}, compiler_params=_params(),
    )(half, w, g_half, m, v, after, *([] if filled is None else filled))


def _small_all_reduce(buf):
    rows = buf.shape[0]
    n_dev = 8

    def body(b_ref, o_ref, gath, send_sems, recv_sems):
        x, y, c, _ = _place()
        me = 4 * x + 2 * y + c
        gath[me] = b_ref[...]
        cps = []
        for k in range(1, n_dev):
            px, py, pc = (x + (k >> 2)) % 2, (y + ((k >> 1) & 1)) % 2, (c + (k & 1)) % 2
            cps.append(_remote(b_ref, gath.at[me], send_sems.at[k - 1], recv_sems.at[k - 1], (px, py, pc)))
        for cp in cps:
            cp.start()
        for k in range(1, n_dev):
            px, py, pc = (x + (k >> 2)) % 2, (y + ((k >> 1) & 1)) % 2, (c + (k & 1)) % 2
            _remote(b_ref, gath.at[4 * px + 2 * py + pc], send_sems.at[k - 1], recv_sems.at[k - 1], (px, py, pc)).wait_recv()
        for cp in cps:
            cp.wait_send()
        acc = gath[0]
        for dev in range(1, n_dev):
            acc = acc + gath[dev]
        o_ref[...] = acc

    vm = pl.BlockSpec(memory_space=pltpu.VMEM)
    return pl.pallas_call(
        body, name="small_all_reduce", in_specs=[vm], out_specs=vm,
        out_shape=jax.ShapeDtypeStruct((rows, LANES), F32),
        scratch_shapes=[pltpu.VMEM((n_dev, rows, LANES), F32), pltpu.SemaphoreType.DMA((n_dev - 1,)),
                        pltpu.SemaphoreType.DMA((n_dev - 1,))],
        compiler_params=_params(),
    )(buf)


def _pad_lanes(v):
    return jnp.pad(v, ((0, 0), (0, LANES - v.shape[-1])))


def _pack(vectors):
    flat, offs, pos = [], [], 0
    for v in vectors:
        n = v.size
        n_pad = -(-n // LANES) * LANES
        flat.append(jnp.pad(v.reshape(-1), (0, n_pad - n)))
        offs.append((pos, n, v.shape))
        pos += n_pad
    total = -(-pos // (8 * LANES)) * 8 * LANES
    flat.append(jnp.zeros((total - pos,), F32))
    return jnp.concatenate(flat).reshape(-1, LANES), offs


def _unpack(buf, offs):
    flat = buf.reshape(-1)
    return [flat[pos:pos + n].reshape(shape) for pos, n, shape in offs]


def kernel(x, norm_mix_pre, w_in, conv_qkv_w, a_log, dt_bias, gdn_norm_w, conv_sc_w, w_out, norm_mix_post, norm_mlp_pre, w_up, w_down, norm_mlp_post, loss_target, m_norm_mix_pre, m_w_in, m_conv_qkv_w, m_a_log, m_dt_bias, m_gdn_norm_w, m_conv_sc_w, m_w_out, m_norm_mix_post, m_norm_mlp_pre, m_w_up, m_w_down, m_norm_mlp_post, v_norm_mix_pre, v_w_in, v_conv_qkv_w, v_a_log, v_dt_bias, v_gdn_norm_w, v_conv_sc_w, v_w_out, v_norm_mix_post, v_norm_mlp_pre, v_w_up, v_w_down, v_norm_mlp_post):
    bsz, seq, d = x.shape
    t = bsz * seq
    heads, head_dim = a_log.shape[-1], gdn_norm_w.shape[-1]
    assert head_dim == LANES and seq % CHUNK == 0
    gw = heads * head_dim
    sw = conv_sc_w.shape[-1] * N_CHIPS
    ics = w_in.shape[-1]
    main = 4 * gw + 3 * sw
    assert ics * N_CHIPS == main + 2 * heads and 2 * heads <= LANES

    lay_in = _Layout("major", w_in.shape[1:])
    lay_out = _Layout("rows", w_out.shape[1:])
    lay_up = _Layout("cols", w_up.shape[1:])
    lay_down = _Layout("rows", w_down.shape[1:])
    layouts = [lay_in, lay_out, lay_up, lay_down]
    chip = (2 * lax.axis_index("x") + lax.axis_index("y")).astype(jnp.int32).reshape(1)
    core = lax.axis_index("c").astype(jnp.int32).reshape(1)
    x2 = x.reshape(t, d)
    cq_g, cs_g = _all_gather([], [], [conv_qkv_w[0], conv_sc_w[0]])
    plan_in, plan_ou, plan_down = _gather_plan(layouts[:1]), _gather_plan(layouts[1:3]), _gather_plan(layouts[3:])
    in_buf = _cast_into_layout(w_in[0], lay_in, chip, "cast_w_in")
    in_sems, in_bufs, in_token = _split_copy_start([in_buf], plan_in, 3, "gather_in_start", after=cq_g)
    shards = [_cast_into_layout(w[0], lay, chip, f"cast_{n}", after=in_token)
              for w, lay, n in zip((w_out, w_up, w_down), layouts[1:], ("w_out", "w_up", "w_down"))]
    xn = _norm_fwd(x2, norm_mix_pre, shards[-1])
    in_bufs = _split_copy_wait(in_sems, in_bufs, plan_in, "gather_in_wait", xn)
    win_sh, = _forward_to_sibling(in_bufs, layouts[:1], "forward_w_in")
    ou_sems, ou_bufs, ou_token = _split_copy_start(shards[:2], plan_ou, 6, "gather_out_up_start", after=win_sh)
    w_main, w_ab = _repack_w_in(win_sh, gw, heads, sw, ou_token)
    conv_q = cq_g.transpose(1, 0, 2).reshape(conv_qkv_w.shape[1], -1)
    conv_s = cs_g.transpose(1, 0, 2).reshape(conv_sc_w.shape[1], -1)

    tgt2 = loss_target.reshape(t, d)
    proj = _matmul(xn, w_main, "nn", [F32], "proj_main")
    proj_ab = _matmul(xn, w_ab, "nn", [F32], "proj_ab")
    proj3 = proj.reshape(bsz, seq, main)
    qkv_act = _qkv_conv_fwd(proj3, conv_q, 3 * gw).reshape(t, 3 * gw)
    a_log_pad, dt_pad = _pad_lanes(a_log), _pad_lanes(dt_bias)
    gcb, betab = _gates_fwd(proj_ab, a_log_pad, dt_pad, heads)
    o_raw, states, *gdn_solved = _gdn_fwd(qkv_act, gcb, betab, bsz, heads)
    ou_bufs = _split_copy_wait(ou_sems, ou_bufs, plan_ou, "gather_out_up_wait", o_raw)
    fwd_plan = _forward_plan(layouts[1:3])
    fwd_sems, ou_bufs, fwd_token = _split_copy_start(ou_bufs, fwd_plan, 6, "forward_out_up_start")
    down_sems, down_bufs, down_token = _split_copy_start(shards[2:], plan_down, 3, "gather_down_start", after=fwd_token)
    gdn_out = _gdn_out_fwd(o_raw, proj, gdn_norm_w, heads, 3 * gw, gw + sw, down_token)
    mixed = _sc_fwd(proj3, conv_s, 4 * gw, sw, gdn_out.reshape(bsz, seq, gw + sw), gw).reshape(t, gw + sw)
    wout_f, wup_f = _split_copy_wait(fwd_sems, ou_bufs, fwd_plan, "forward_out_up_wait", mixed)
    mix = _matmul(mixed, wout_f, "nn", [F32], "mix_out")
    h, hn = _mid_fwd(x2, mix, norm_mix_post, norm_mlp_pre)

    def up_epilogue(acc):
        r = jnp.maximum(acc, 0.0)
        return r, r * r

    relu_up, hid = _matmul(hn, wup_f, "nn", [BF16, BF16], "mlp_up", epilogue=up_epilogue)
    down_bufs = _split_copy_wait(down_sems, down_bufs, plan_down, "gather_down_wait", hid)
    (wdown_f,) = _forward_to_sibling(down_bufs, layouts[3:], "forward_w_down")
    ff = _matmul(hid, wdown_f, "nn", [F32], "mlp_down")
    loss_blk, dy, dff, dg_mlp_post = _head_fwd_bwd(h, ff, tgt2, norm_mlp_post)

    def dup_epilogue(acc, r):
        return (acc * (2.0 * r.astype(F32)),)

    d_up = _matmul(dff, wdown_f, "nt", [BF16], "d_hid", epilogue=dup_epilogue, extras=(relu_up,))
    dw_down = _matmul(hid, dff, "tn", [BF16], "dw_down")
    plan_h_down, plan_h_up, plan_h_in = _halves_plan([lay_down]), _halves_plan([lay_up]), _halves_plan([lay_in])
    hd_sems, hd_arrays, hd_token = _halves_start([dw_down], [lay_down], "down_halves_start")
    d_hn = _matmul(d_up, wup_f, "nt", [F32], "d_hn", after=hd_token)
    dw_up = _matmul(hn, d_up, "tn", [BF16], "dw_up")
    dw_down, down_got = _split_copy_wait(hd_sems, hd_arrays, plan_h_down, "down_halves_wait", dw_up)
    hu_sems, hu_arrays, hu_token = _halves_start([dw_up], [lay_up], "up_halves_start", after=down_got)
    down_part = _chip_sum(dw_down, down_got, lay_down, core, "chip_sum_w_down")
    pd_sems, pd_arrays, pd_token = _partials_start([down_part], "down_partials_start", after=hu_token)
    dh, dmix, dg_mlp_pre, dg_mix_post = _mid_bwd(d_hn, h, norm_mlp_pre, dy, mix, norm_mix_post, pd_token)
    dmixed = _matmul(dmix, wout_f, "nt", [F32], "d_mixed")
    dw_out = _matmul(mixed, dmix, "tn", [BF16], "dw_out")
    dw_up, up_got = _split_copy_wait(hu_sems, hu_arrays, plan_h_up, "up_halves_wait", dw_out)
    up_part = _chip_sum(dw_up, up_got, lay_up, core, "chip_sum_w_up")
    out_got, = _halves_to_sibling([dw_out], [lay_out], "out_grad_halves_to_sibling")
    out_part = _chip_sum(dw_out, out_got, lay_out, core, "chip_sum_w_out")
    puo_sems, puo_arrays, puo_token = _partials_start([up_part, out_part], "up_out_partials_start", after=dmixed)
    dmixed3 = dmixed.reshape(bsz, seq, d)
    d_b, d_c, d_hsc, dw_conv_s = _sc_bwd(proj3, dmixed3, conv_s, 4 * gw, sw, gw)
    d_o, d_z, dg_gdn_norm = _gdn_out_bwd(dmixed, o_raw, proj, gdn_norm_w, heads, 3 * gw, puo_token)
    dq, dk, dv, dgc_b, dbeta_b = _gdn_bwd(qkv_act, gcb, betab, states, gdn_solved, d_o, bsz, heads)
    d_ab, d_alog, d_dt = _gates_bwd(proj_ab, a_log_pad, dt_pad, dgc_b, dbeta_b, heads)
    d_proj3, dw_conv_q = _qkv_conv_bwd(proj3, [a.reshape(bsz, seq, gw) for a in (dq, dk, dv)], conv_q, 3 * gw,
                                       d_z.reshape(bsz, seq, main))
    d_proj = _place_columns(d_proj3, [d_b, d_c, d_hsc], 4 * gw).reshape(t, main)
    dw_main = _matmul(xn, d_proj, "tn", [BF16], "dw_in_main")
    dw_ab = _matmul(xn, d_ab, "tn", [BF16], "dw_in_ab")
    dw_in = _unpack_dw_in(dw_main, dw_ab, gw, heads, sw, ics)
    hi_sems, hi_arrays, hi_token = _halves_start([dw_in], [lay_in], "in_halves_start")
    d_xn = _matmul(d_proj, w_main, "nt", [F32], "d_xn_main", after=hi_token)
    dw_in, in_got = _split_copy_wait(hi_sems, hi_arrays, plan_h_in, "in_halves_wait", d_xn)
    in_part = _chip_sum(dw_in, in_got, lay_in, core, "chip_sum_w_in")
    pi_sems, pi_arrays, pi_token = _partials_start([in_part], "in_partials_start")
    d_xn_ab = _matmul(d_ab, w_ab, "nt", [F32], "d_xn_ab", after=pi_token)
    grad_x, dg_mix_pre = _first_bwd(d_xn, d_xn_ab, x2, norm_mix_pre, dh)

    other_core = 1 - core

    def finish(parts, recvs, weights3, names, tag):
        nw = len(parts)
        halves = [_shard_sum(p, r, chip, f"shard_sum_{n}") for p, r, n in zip(parts, recvs, names)]
        sems, arrays, token = _join_start(halves, f"{tag}_join_start")
        own = [_adamw_rows(wt[0], h, m[0], v[0], core, None, token, f"adamw_own_{n}")
               for (wt, m, v), h, n in zip(weights3, arrays[:nw], names)]
        theirs = _split_copy_wait(sems, arrays, _join_plan(nw), f"{tag}_join_wait", own[-1][1])[nw:]
        return [_adamw_rows(wt[0], h, m[0], v[0], other_core, o, h, f"adamw_sibling_{n}")
                for (wt, m, v), h, o, n in zip(weights3, theirs, own, names)]

    down_part, down_recv = _split_copy_wait(pd_sems, pd_arrays, _partials_plan(1), "down_partials_wait", grad_x)
    up_part, out_part, up_recv, out_recv = _split_copy_wait(puo_sems, puo_arrays, _partials_plan(2),
                                                            "up_out_partials_wait", down_recv)
    res_up, res_down = finish([up_part, down_part], [up_recv, down_recv],
                              [(w_up, m_w_up, v_w_up), (w_down, m_w_down, v_w_down)], ("w_up", "w_down"), "mlp")
    in_part, in_recv = _split_copy_wait(pi_sems, pi_arrays, _partials_plan(1), "in_partials_wait", res_down[1])
    res_in, res_out = finish([in_part, out_part], [in_recv, out_recv],
                             [(w_in, m_w_in, v_w_in), (w_out, m_w_out, v_w_out)], ("w_in", "w_out"), "mix")

    small, offs = _pack([loss_blk[0:1, 0:1], dg_mix_pre, dw_conv_q, d_alog[:, :heads], d_dt[:, :heads], dg_gdn_norm,
                         dw_conv_s, dg_mix_post, dg_mlp_pre, dg_mlp_post])
    (loss, g_mix_pre, g_conv_q_full, g_alog, g_dt, g_gdn_norm, g_conv_s_full, g_mix_post, g_mlp_pre,
     g_mlp_post) = _unpack(_small_all_reduce(small), offs)
    j = 2 * lax.axis_index("x") + lax.axis_index("y")
    cq_w, cs_w = conv_qkv_w.shape[-1], conv_sc_w.shape[-1]
    g_conv_q = lax.dynamic_slice_in_dim(g_conv_q_full, j * cq_w, cq_w, axis=1)
    g_conv_s = lax.dynamic_slice_in_dim(g_conv_s_full, j * cs_w, cs_w, axis=1)

    big = {1: res_in, 7: res_out, 10: res_up, 11: res_down}
    grads = [g_mix_pre, None, g_conv_q, g_alog, g_dt, g_gdn_norm, g_conv_s, None, g_mix_post, g_mlp_pre, None,
             None, g_mlp_post]
    weights = [norm_mix_pre, w_in, conv_qkv_w, a_log, dt_bias, gdn_norm_w, conv_sc_w, w_out, norm_mix_post,
               norm_mlp_pre, w_up, w_down, norm_mlp_post]
    ms = [m_norm_mix_pre, m_w_in, m_conv_qkv_w, m_a_log, m_dt_bias, m_gdn_norm_w, m_conv_sc_w, m_w_out,
          m_norm_mix_post, m_norm_mlp_pre, m_w_up, m_w_down, m_norm_mlp_post]
    vs = [v_norm_mix_pre, v_w_in, v_conv_qkv_w, v_a_log, v_dt_bias, v_gdn_norm_w, v_conv_sc_w, v_w_out,
          v_norm_mix_post, v_norm_mlp_pre, v_w_up, v_w_down, v_norm_mlp_post]
    out_g, out_d, out_m, out_v = [], [], [], []
    for i, (wt, g, m, v) in enumerate(zip(weights, grads, ms, vs)):
        shape2 = wt.shape[-2:] if wt.ndim == 3 else wt.shape
        if i in big:
            g2, dl, nm, nv = big[i]
        else:
            g2 = g.reshape(shape2)
            dl, nm, nv = _adamw(wt.reshape(shape2), g2, m.reshape(shape2), v.reshape(shape2), f"adamw_{i}")
        out_g.append(g2.reshape(wt.shape))
        out_d.append(dl.reshape(wt.shape))
        out_m.append(nm.reshape(wt.shape))
        out_v.append(nv.reshape(wt.shape))

    return (loss.reshape(()), grad_x.reshape(bsz, seq, d), *out_g, *out_d, *out_m, *out_v)
```
